```python
import math
import jax, jax.numpy as jnp
from jax import lax
import numpy as np

D_MODEL = 1024
BATCH = 4
SEQ = 4096
DEPTH = 4

MEM_LEN = 256
D_POOL = D_MODEL // 2
POOL_WINDOWS = (2, 4, 8, 16)
N_POOL_GROUPS = len(POOL_WINDOWS)
POOL_GROUP_DIM = D_POOL // N_POOL_GROUPS
D_SG = D_MODEL // 2
N_SG_HEADS = 4
SG_HEAD_DIM = D_SG // N_SG_HEADS
CHUNK = 128
D_CONV = D_MODEL
CONV_WIDTH = 3
N_XHEADS = 4
XHEAD_DIM = D_MODEL // N_XHEADS
N_GROUPS = 4
EXPERTS_PER_GROUP = 8
N_EXPERTS = N_GROUPS * EXPERTS_PER_GROUP
TOP_K_INNER = 2
D_EXPERT = 512
DISPATCH_BLOCK = 128
N_EVEN = (DEPTH + 1) // 2
N_ODD = DEPTH // 2
ALPHA = (2.0 * DEPTH) ** 0.25
BETA = (8.0 * DEPTH) ** -0.25
LN_EPS = 1e-5

kernel_name = "hybrid_pool_sgate_shortconv_hmoe_deepnorm"


def layer_norm(x, g, b):
    xf = x.astype(jnp.float32)
    mu = jnp.mean(xf, axis=-1, keepdims=True)
    var = jnp.mean(jnp.square(xf - mu), axis=-1, keepdims=True)
    y = (xf - mu) * lax.rsqrt(var + LN_EPS)
    return (y * g.astype(jnp.float32) + b.astype(jnp.float32)).astype(x.dtype)


def pool_mixer(a, w_pool, pool_scale):
    B, S, _ = a.shape
    a4 = a.astype(jnp.float32).reshape(B, S, N_POOL_GROUPS, POOL_GROUP_DIM)
    cs = jnp.cumsum(a4, axis=1)
    pos = jnp.arange(1, S + 1, dtype=jnp.float32)
    diffs = []
    for g, w in enumerate(POOL_WINDOWS):
        cs_g = cs[:, :, g]
        lagged = jnp.pad(cs_g, ((0, 0), (w, 0), (0, 0)))[:, :S]
        cnt = jnp.minimum(pos, float(w))[None, :, None]
        diffs.append((cs_g - lagged) / cnt - a4[:, :, g])
    d = jnp.stack(diffs, axis=2)
    y = jnp.einsum('bsgc,gcd->bsgd', d, w_pool.astype(jnp.float32))
    y = y.reshape(B, S, D_POOL) * pool_scale.astype(jnp.float32)
    return y.astype(a.dtype)


def spatial_gating(z, ln_v_g, ln_v_b, w_spatial, b_spatial):
    B, S, _ = z.shape
    u, v = z[..., :D_SG], z[..., D_SG:]
    v = layer_norm(v, ln_v_g, ln_v_b)
    vc = v.reshape(B, S // CHUNK, CHUNK, N_SG_HEADS, SG_HEAD_DIM)
    mask = jnp.tril(jnp.ones((CHUNK, CHUNK), dtype=w_spatial.dtype))
    ws = w_spatial * mask[None]
    sv = jnp.einsum('hij,bnjhc->bnihc', ws, vc) + b_spatial.T[:, :, None]
    return u * sv.reshape(B, S, D_SG)


def even_mixer(x, w_in, w_pool, pool_scale, ln_v_g, ln_v_b, w_spatial, b_spatial, w_out):
    h = x @ w_in
    a_out = pool_mixer(h[..., :D_POOL], w_pool, pool_scale)
    b_out = spatial_gating(jax.nn.gelu(h[..., D_POOL:]), ln_v_g, ln_v_b, w_spatial, b_spatial)
    return jnp.concatenate([a_out, b_out], axis=-1) @ w_out


def odd_mixer(x, w_in, conv_w, conv_b, w_out):
    S = x.shape[1]
    h = x @ w_in
    bg, cg, z = h[..., :D_CONV], h[..., D_CONV:2 * D_CONV], h[..., 2 * D_CONV:]
    zc = cg * z
    zp = jnp.pad(zc, ((0, 0), (CONV_WIDTH - 1, 0), (0, 0)))
    conv = conv_b + zp[:, 0:S] * conv_w[:, 0] + zp[:, 1:S + 1] * conv_w[:, 1] + zp[:, 2:S + 2] * conv_w[:, 2]
    return (bg * conv) @ w_out


def memory_cross_attn(x, mem, wq, wk, wv, wo):
    B, S, D = x.shape
    M = mem.shape[1]
    q = (x @ wq).reshape(B, S, N_XHEADS, XHEAD_DIM)
    k = (mem @ wk).reshape(B, M, N_XHEADS, XHEAD_DIM)
    v = (mem @ wv).reshape(B, M, N_XHEADS, XHEAD_DIM)
    s = jnp.einsum('bshd,bmhd->bhsm', q, k).astype(jnp.float32) / math.sqrt(XHEAD_DIM)
    p = jax.nn.softmax(s, axis=-1).astype(v.dtype)
    o = jnp.einsum('bhsm,bmhd->bshd', p, v).reshape(B, S, D)
    return o @ wo


def expert_dispatch(xf, expert_ids, gates, w1, w3, w2):
    N, D = xf.shape
    A = N * TOP_K_INNER
    e_flat = expert_ids.reshape(-1)
    g_flat = gates.reshape(-1)
    tok = jnp.arange(A, dtype=jnp.int32) // TOP_K_INNER
    order = jnp.argsort(e_flat)
    se, stok, sg = e_flat[order], tok[order], g_flat[order]
    counts = jnp.zeros((N_EXPERTS,), jnp.int32).at[e_flat].add(1)
    start = jnp.cumsum(counts) - counts
    pcounts = (counts + DISPATCH_BLOCK - 1) // DISPATCH_BLOCK * DISPATCH_BLOCK
    pend = jnp.cumsum(pcounts)
    pstart = pend - pcounts
    dest = pstart[se] + (jnp.arange(A, dtype=jnp.int32) - start[se])
    n_blocks = (A + N_EXPERTS * (DISPATCH_BLOCK - 1) + DISPATCH_BLOCK - 1) // DISPATCH_BLOCK
    P = n_blocks * DISPATCH_BLOCK
    buf_tok = jnp.full((P,), N, jnp.int32).at[dest].set(stok)
    buf_gate = jnp.zeros((P,), gates.dtype).at[dest].set(sg)
    block_e = jnp.minimum(
        jnp.searchsorted(pend, jnp.arange(n_blocks, dtype=jnp.int32) * DISPATCH_BLOCK, side='right'),
        N_EXPERTS - 1)
    xpad = jnp.concatenate([xf, jnp.zeros((1, D), xf.dtype)], axis=0)
    xb = xpad[buf_tok].reshape(n_blocks, DISPATCH_BLOCK, D)

    def expert_block(args):
        xblk, e = args
        hid = jax.nn.silu(xblk @ w1[e]) * (xblk @ w3[e])
        return hid @ w2[e]

    yb = lax.map(expert_block, (xb, block_e)).reshape(P, D) * buf_gate[:, None]
    return jnp.zeros((N + 1, D), yb.dtype).at[buf_tok].add(yb)[:N]


def hierarchical_moe(x, wr_g, br_g, wr_e, br_e, w1, w3, w2):
    B, S, D = x.shape
    xf = x.reshape(B * S, D)
    x32 = xf.astype(jnp.float32)
    logits_g = x32 @ wr_g.astype(jnp.float32) + br_g.astype(jnp.float32)
    p_g = jax.nn.softmax(logits_g, axis=-1)
    g_sel = jnp.argmax(logits_g, axis=-1).astype(jnp.int32)
    gate_g = jnp.take_along_axis(p_g, g_sel[:, None], axis=1)[:, 0]
    logits_all = jnp.einsum('nd,gde->nge', x32, wr_e.astype(jnp.float32)) + br_e.astype(jnp.float32)
    logits_e = jnp.take_along_axis(logits_all, g_sel[:, None, None], axis=1)[:, 0]
    top_v, top_i = lax.top_k(logits_e, TOP_K_INNER)
    w_inner = jax.nn.softmax(top_v, axis=-1)
    expert_ids = g_sel[:, None] * EXPERTS_PER_GROUP + top_i.astype(jnp.int32)
    gates = (gate_g[:, None] * w_inner).astype(x.dtype)
    y = expert_dispatch(xf, expert_ids, gates, w1, w3, w2)
    return y.reshape(B, S, D)


def setup_inputs(seed: int = 0) -> dict:
    key = jax.random.key(seed)
    ks = iter(jax.random.split(key, 40))
    f32 = jnp.float32
    D = D_MODEL

    def nrm(shape, scale):
        return jax.random.normal(next(ks), shape, f32) * scale

    d_in_even = D_POOL + 2 * D_SG
    return {
        "x": nrm((BATCH, SEQ, D), 1.0),
        "mem": nrm((BATCH, MEM_LEN, D), 1.0),
        "w_in_even": nrm((N_EVEN, D, d_in_even), D ** -0.5),
        "w_pool": nrm((N_EVEN, N_POOL_GROUPS, POOL_GROUP_DIM, POOL_GROUP_DIM), POOL_GROUP_DIM ** -0.5),
        "pool_scale": 1.0 + nrm((N_EVEN, D_POOL), 0.1),
        "ln_v_g": 1.0 + nrm((N_EVEN, D_SG), 0.02),
        "ln_v_b": nrm((N_EVEN, D_SG), 0.02),
        "w_spatial": nrm((N_EVEN, N_SG_HEADS, CHUNK, CHUNK), CHUNK ** -0.5),
        "b_spatial": 1.0 + nrm((N_EVEN, N_SG_HEADS, CHUNK), 0.1),
        "w_out_even": nrm((N_EVEN, D_POOL + D_SG, D), (D_POOL + D_SG) ** -0.5 * BETA),
        "w_in_odd": nrm((N_ODD, D, 3 * D_CONV), D ** -0.5),
        "conv_w": nrm((N_ODD, D_CONV, CONV_WIDTH), CONV_WIDTH ** -0.5),
        "conv_b": nrm((N_ODD, D_CONV), 0.02),
        "w_out_odd": nrm((N_ODD, D_CONV, D), D_CONV ** -0.5 * BETA),
        "wq_x": nrm((DEPTH, D, D), D ** -0.5),
        "wk_x": nrm((DEPTH, D, D), D ** -0.5),
        "wv_x": nrm((DEPTH, D, D), D ** -0.5 * BETA),
        "wo_x": nrm((DEPTH, D, D), D ** -0.5 * BETA),
        "ln_g": 1.0 + nrm((DEPTH, 3, D), 0.02),
        "ln_b": nrm((DEPTH, 3, D), 0.02),
        "wr_group": nrm((DEPTH, D, N_GROUPS), D ** -0.5),
        "br_group": nrm((DEPTH, N_GROUPS), 0.01),
        "wr_expert": nrm((DEPTH, N_GROUPS, D, EXPERTS_PER_GROUP), D ** -0.5),
        "br_expert": nrm((DEPTH, N_GROUPS, EXPERTS_PER_GROUP), 0.01),
        "w1": nrm((DEPTH, N_EXPERTS, D, D_EXPERT), D ** -0.5),
        "w3": nrm((DEPTH, N_EXPERTS, D, D_EXPERT), D ** -0.5),
        "w2": nrm((DEPTH, N_EXPERTS, D_EXPERT, D), D_EXPERT ** -0.5 * BETA),
    }


def reference(x, mem, w_in_even, w_pool, pool_scale, ln_v_g, ln_v_b, w_spatial, b_spatial,
              w_out_even, w_in_odd, conv_w, conv_b, w_out_odd, wq_x, wk_x, wv_x, wo_x,
              ln_g, ln_b, wr_group, br_group, wr_expert, br_expert, w1, w3, w2):
    for l in range(DEPTH):
        i = l // 2
        if l % 2 == 0:
            mix = even_mixer(x, w_in_even[i], w_pool[i], pool_scale[i], ln_v_g[i], ln_v_b[i],
                             w_spatial[i], b_spatial[i], w_out_even[i])
        else:
            mix = odd_mixer(x, w_in_odd[i], conv_w[i], conv_b[i], w_out_odd[i])
        x = layer_norm(ALPHA * x + mix, ln_g[l, 0], ln_b[l, 0])
        xa = memory_cross_attn(x, mem, wq_x[l], wk_x[l], wv_x[l], wo_x[l])
        x = layer_norm(ALPHA * x + xa, ln_g[l, 1], ln_b[l, 1])
        ff = hierarchical_moe(x, wr_group[l], br_group[l], wr_expert[l], br_expert[l],
                              w1[l], w3[l], w2[l])
        x = layer_norm(ALPHA * x + ff, ln_g[l, 2], ln_b[l, 2])
    return x
```

```python
import functools
import math

import jax
import jax.numpy as jnp
from jax import lax
from jax.experimental import pallas as pl
from jax.experimental.pallas import tpu as pltpu

F32 = jnp.float32
BF16 = jnp.bfloat16
I32 = jnp.int32

POOL_WINDOWS = (2, 4, 8, 16)
N_SG_HEADS = 4
CHUNK = 128
CONV_WIDTH = 3
N_XHEADS = 4
N_GROUPS = 4
EXPERTS_PER_GROUP = 8
N_EXPERTS = N_GROUPS * EXPERTS_PER_GROUP
TOP_K = 2
DEPTH = 4
ALPHA = (2.0 * DEPTH) ** 0.25
LN_EPS = 1e-5

LANES = 128
SUBLANES = 8
TM = 512
POOL_HALO = 16
CONV_HALO = 8
EXPERT_BLOCK = 256
ROUTE_COLS = 128
VMEM_LIMIT = 48 * 1024 * 1024

_NT = (((1,), (1,)), ((), ()))


def _dot(a, b):
    return jnp.dot(a, b, preferred_element_type=F32)


def _layer_norm(y, g, b):
    mu = jnp.mean(y, axis=-1, keepdims=True)
    yc = y - mu
    var = jnp.mean(yc * yc, axis=-1, keepdims=True)
    return yc * lax.rsqrt(var + LN_EPS) * g + b


def _gelu_tanh(x):
    c = math.sqrt(2.0 / math.pi)
    return 0.5 * x * (1.0 + jnp.tanh(c * (x + 0.044715 * (x * x * x))))


def _const_spec(shape):
    nd = len(shape)
    return pl.BlockSpec(shape, lambda i: (0,) * nd)


def _params():
    return pltpu.CompilerParams(dimension_semantics=("arbitrary",), vmem_limit_bytes=VMEM_LIMIT)


def _even_kernel(x_ref, xh_ref, win_ref, wpool_ref, pscale_ref, lvg_ref, lvb_ref, ws_ref, bst_ref,
                 wout_ref, g_ref, b_ref, o_ref, a_scr, cat_scr, *, tiles_per_seq):
    tm = x_ref.shape[0]
    d_pool = a_scr.shape[1]
    d_sg = lvg_ref.shape[1]
    pgd = d_pool // len(POOL_WINDOWS)
    hd_dim = d_sg // N_SG_HEADS
    seq_tile = pl.program_id(0) % tiles_per_seq

    x = x_ref[...]
    h = _dot(x.astype(BF16), win_ref[...])

    ah = _dot(xh_ref[...].astype(BF16), win_ref[:, :d_pool])
    a_scr[0:POOL_HALO, :] = jnp.where(seq_tile == 0, 0.0, ah)
    a_scr[POOL_HALO:POOL_HALO + tm, :] = h[:, :d_pool]
    pos = seq_tile * tm + lax.broadcasted_iota(I32, (tm, 1), 0)
    for g, w in enumerate(POOL_WINDOWS):
        cs = slice(g * pgd, (g + 1) * pgd)
        tok = a_scr[POOL_HALO:POOL_HALO + tm, cs]
        acc = tok
        for j in range(1, w):
            acc = acc + a_scr[POOL_HALO - j:POOL_HALO - j + tm, cs]
        cnt = jnp.minimum(pos + 1, w).astype(F32)
        d = acc / cnt - tok
        yg = _dot(d.astype(BF16), wpool_ref[g])
        cat_scr[:, cs] = (yg * pscale_ref[:, cs]).astype(BF16)

    z = _gelu_tanh(h[:, d_pool:])
    u = z[:, :d_sg]
    v = _layer_norm(z[:, d_sg:], lvg_ref[...], lvb_ref[...]).astype(BF16)
    row = lax.broadcasted_iota(I32, (CHUNK, CHUNK), 0)
    col = lax.broadcasted_iota(I32, (CHUNK, CHUNK), 1)
    for hd in range(N_SG_HEADS):
        hs = slice(hd * hd_dim, (hd + 1) * hd_dim)
        wsm = jnp.where(row >= col, ws_ref[hd], 0.0).astype(BF16)
        bcol = bst_ref[:, hd:hd + 1]
        for ck in range(tm // CHUNK):
            rs = slice(ck * CHUNK, (ck + 1) * CHUNK)
            sv = _dot(wsm, v[rs, hs]) + bcol
            cat_scr[rs, d_pool + hd * hd_dim:d_pool + (hd + 1) * hd_dim] = (u[rs, hs] * sv).astype(BF16)

    mix = _dot(cat_scr[...], wout_ref[...])
    o_ref[...] = _layer_norm(ALPHA * x + mix, g_ref[...], b_ref[...])


def _even_mixer(x, seq, w_in, w_pool, pool_scale, ln_v_g, ln_v_b, w_spatial, b_spatial, w_out, g, b):
    n, d = x.shape
    d_in = w_in.shape[1]
    d_pool = pool_scale.shape[0]
    d_sg = ln_v_g.shape[0]
    kern = functools.partial(_even_kernel, tiles_per_seq=seq // TM)
    halo_blocks = TM // POOL_HALO
    return pl.pallas_call(
        kern,
        out_shape=jax.ShapeDtypeStruct((n, d), F32),
        grid=(n // TM,),
        in_specs=[
            pl.BlockSpec((TM, d), lambda i: (i, 0)),
            pl.BlockSpec((POOL_HALO, d), lambda i: (jnp.maximum(i * halo_blocks - 1, 0), 0)),
            _const_spec((d, d_in)),
            _const_spec(w_pool.shape),
            _const_spec((1, d_pool)),
            _const_spec((1, d_sg)),
            _const_spec((1, d_sg)),
            _const_spec(w_spatial.shape),
            _const_spec((CHUNK, N_SG_HEADS)),
            _const_spec(w_out.shape),
            _const_spec((1, d)),
            _const_spec((1, d)),
        ],
        out_specs=pl.BlockSpec((TM, d), lambda i: (i, 0)),
        scratch_shapes=[pltpu.VMEM((POOL_HALO + TM, d_pool), F32), pltpu.VMEM((TM, d_pool + d_sg), BF16)],
        compiler_params=_params(),
        name="even_mixer",
    )(x, x, w_in.astype(BF16), w_pool.astype(BF16), pool_scale.reshape(1, -1), ln_v_g.reshape(1, -1),
      ln_v_b.reshape(1, -1), w_spatial, b_spatial.T, w_out.astype(BF16), g.reshape(1, -1), b.reshape(1, -1))


def _odd_kernel(x_ref, xh_ref, win_ref, cwt_ref, cb_ref, wout_ref, g_ref, b_ref, o_ref, zc_scr,
                *, tiles_per_seq):
    tm, d = x_ref.shape
    seq_tile = pl.program_id(0) % tiles_per_seq

    x = x_ref[...]
    h = _dot(x.astype(BF16), win_ref[...])
    hh = _dot(xh_ref[...].astype(BF16), win_ref[:, d:])
    zc_scr[0:CONV_HALO, :] = jnp.where(seq_tile == 0, 0.0, hh[:, :d] * hh[:, d:])
    zc_scr[CONV_HALO:CONV_HALO + tm, :] = h[:, d:2 * d] * h[:, 2 * d:]
    conv = cb_ref[...]
    for j in range(CONV_WIDTH):
        off = CONV_HALO - (CONV_WIDTH - 1) + j
        conv = conv + zc_scr[off:off + tm, :] * cwt_ref[j:j + 1, :]
    y = _dot((h[:, :d] * conv).astype(BF16), wout_ref[...])
    o_ref[...] = _layer_norm(ALPHA * x + y, g_ref[...], b_ref[...])


def _odd_mixer(x, seq, w_in, conv_w, conv_b, w_out, g, b):
    n, d = x.shape
    kern = functools.partial(_odd_kernel, tiles_per_seq=seq // TM)
    halo_blocks = TM // CONV_HALO
    return pl.pallas_call(
        kern,
        out_shape=jax.ShapeDtypeStruct((n, d), F32),
        grid=(n // TM,),
        in_specs=[
            pl.BlockSpec((TM, d), lambda i: (i, 0)),
            pl.BlockSpec((CONV_HALO, d), lambda i: (jnp.maximum(i * halo_blocks - 1, 0), 0)),
            _const_spec(w_in.shape),
            _const_spec((CONV_WIDTH, d)),
            _const_spec((1, d)),
            _const_spec(w_out.shape),
            _const_spec((1, d)),
            _const_spec((1, d)),
        ],
        out_specs=pl.BlockSpec((TM, d), lambda i: (i, 0)),
        scratch_shapes=[pltpu.VMEM((CONV_HALO + TM, d), F32)],
        compiler_params=_params(),
        name="odd_mixer",
    )(x, x, w_in.astype(BF16), conv_w.T, conv_b.reshape(1, -1), w_out.astype(BF16),
      g.reshape(1, -1), b.reshape(1, -1))


def _kv_kernel(mem_ref, wk_ref, wv_ref, k_ref, v_ref):
    m = mem_ref[...].astype(BF16)
    k_ref[...] = _dot(m, wk_ref[...]).astype(BF16)
    v_ref[...] = _dot(m, wv_ref[...]).astype(BF16)


def _memory_kv(mem2d, wk, wv):
    nl, d, _ = wk.shape
    rows = mem2d.shape[0]
    out = jax.ShapeDtypeStruct((nl, rows, d), BF16)
    wspec = pl.BlockSpec((None, d, d), lambda l: (l, 0, 0))
    ospec = pl.BlockSpec((None, rows, d), lambda l: (l, 0, 0))
    return pl.pallas_call(
        _kv_kernel,
        out_shape=(out, out),
        grid=(nl,),
        in_specs=[_const_spec((rows, d)), wspec, wspec],
        out_specs=(ospec, ospec),
        compiler_params=_params(),
        name="memory_kv",
    )(mem2d, wk.astype(BF16), wv.astype(BF16))


def _attn_kernel(x_ref, k_ref, v_ref, wq_ref, wo_ref, g_ref, b_ref, o_ref, o_scr):
    d = x_ref.shape[1]
    hd_dim = d // N_XHEADS
    x = x_ref[...]
    q = _dot(x.astype(BF16), wq_ref[...]) * (1.0 / math.sqrt(hd_dim))
    for hd in range(N_XHEADS):
        hs = slice(hd * hd_dim, (hd + 1) * hd_dim)
        s = lax.dot_general(q[:, hs].astype(BF16), k_ref[:, hs], _NT, preferred_element_type=F32)
        p = jnp.exp(s - jnp.max(s, axis=-1, keepdims=True))
        p = p / jnp.sum(p, axis=-1, keepdims=True)
        o_scr[:, hs] = _dot(p.astype(BF16), v_ref[:, hs]).astype(BF16)
    xa = _dot(o_scr[...], wo_ref[...])
    o_ref[...] = _layer_norm(ALPHA * x + xa, g_ref[...], b_ref[...])


def _cross_attn(x, seq, k, v, wq, wo, g, b):
    n, d = x.shape
    m = k.shape[1]
    tiles_per_seq = seq // TM
    kvspec = pl.BlockSpec((None, m, d), lambda i: (i // tiles_per_seq, 0, 0))
    return pl.pallas_call(
        _attn_kernel,
        out_shape=jax.ShapeDtypeStruct((n, d), F32),
        grid=(n // TM,),
        in_specs=[
            pl.BlockSpec((TM, d), lambda i: (i, 0)),
            kvspec, kvspec,
            _const_spec((d, d)), _const_spec((d, d)),
            _const_spec((1, d)), _const_spec((1, d)),
        ],
        out_specs=pl.BlockSpec((TM, d), lambda i: (i, 0)),
        scratch_shapes=[pltpu.VMEM((TM, d), BF16)],
        compiler_params=_params(),
        name="cross_attn",
    )(x, k, v, wq.astype(BF16), wo.astype(BF16), g.reshape(1, -1), b.reshape(1, -1))


def _router_kernel(x_ref, wcat_ref, bias_ref, ri_ref, rg_ref, cnt_ref, carry_scr):
    tm = x_ref.shape[0]
    step = pl.program_id(0)

    @pl.when(step == 0)
    def _():
        carry_scr[...] = jnp.zeros_like(carry_scr)

    x = x_ref[...]
    xh = x.astype(BF16)
    xl = (x - xh.astype(F32)).astype(BF16)
    r1 = _dot(xh, wcat_ref[...])
    r2 = _dot(xl, wcat_ref[:, :ROUTE_COLS])
    logits = r1[:, :ROUTE_COLS] + r1[:, ROUTE_COLS:] + r2 + bias_ref[...]

    lane = lax.broadcasted_iota(I32, (tm, ROUTE_COLS), 1).astype(F32)
    neg = -jnp.inf

    def first_argmax(vals):
        mx = jnp.max(vals, axis=-1, keepdims=True)
        idx = jnp.min(jnp.where(vals == mx, lane, float(ROUTE_COLS)), axis=-1, keepdims=True)
        return mx, idx

    gl = jnp.where(lane < N_GROUPS, logits, neg)
    gmax, g_sel = first_argmax(gl)
    gate_g = 1.0 / jnp.sum(jnp.exp(gl - gmax), axis=-1, keepdims=True)

    lo = N_GROUPS + g_sel * EXPERTS_PER_GROUP
    el = jnp.where((lane >= lo) & (lane < lo + EXPERTS_PER_GROUP), logits, neg)
    m1, i1 = first_argmax(el)
    m2, i2 = first_argmax(jnp.where(lane == i1, neg, el))
    e21 = jnp.exp(m2 - m1)
    w1 = 1.0 / (1.0 + e21)
    w2 = e21 / (1.0 + e21)

    oh1 = lane == i1
    oh2 = lane == i2
    oh = (oh1 | oh2).astype(BF16)
    r = lax.broadcasted_iota(I32, (tm, tm), 0)
    c = lax.broadcasted_iota(I32, (tm, tm), 1)
    before = _dot((r > c).astype(BF16), oh) + carry_scr[...]
    rank1 = jnp.sum(jnp.where(oh1, before, 0.0), axis=-1, keepdims=True)
    rank2 = jnp.sum(jnp.where(oh2, before, 0.0), axis=-1, keepdims=True)
    carry_scr[...] += jnp.sum(oh.astype(F32), axis=0, keepdims=True)

    l4 = lax.broadcasted_iota(I32, ri_ref.shape, 1)
    ri_ref[...] = jnp.where(l4 == 0, i1 - N_GROUPS,
                            jnp.where(l4 == 1, i2 - N_GROUPS,
                                      jnp.where(l4 == 2, rank1, rank2))).astype(I32)
    l2 = lax.broadcasted_iota(I32, rg_ref.shape, 1)
    rg_ref[...] = jnp.where(l2 == 0, gate_g * w1, gate_g * w2)
    cnt_ref[...] = carry_scr[...].astype(I32)


def _router(x, wr_g, br_g, wr_e, br_e):
    n, d = x.shape
    w = jnp.concatenate([wr_g, jnp.transpose(wr_e, (1, 0, 2)).reshape(d, N_EXPERTS)], axis=1)
    w = jnp.pad(w, ((0, 0), (0, ROUTE_COLS - w.shape[1])))
    w_hi = w.astype(BF16)
    w_lo = (w - w_hi.astype(F32)).astype(BF16)
    bias = jnp.pad(jnp.concatenate([br_g, br_e.reshape(-1)]), (0, ROUTE_COLS - N_GROUPS - N_EXPERTS))
    ri, rg, cnt = pl.pallas_call(
        _router_kernel,
        out_shape=(jax.ShapeDtypeStruct((n, 4), I32), jax.ShapeDtypeStruct((n, TOP_K), F32),
                   jax.ShapeDtypeStruct((1, ROUTE_COLS), I32)),
        grid=(n // TM,),
        in_specs=[pl.BlockSpec((TM, d), lambda i: (i, 0)),
                  _const_spec((d, 2 * ROUTE_COLS)), _const_spec((1, ROUTE_COLS))],
        out_specs=(pl.BlockSpec((TM, 4), lambda i: (i, 0)), pl.BlockSpec((TM, TOP_K), lambda i: (i, 0)),
                   _const_spec((1, ROUTE_COLS))),
        scratch_shapes=[pltpu.VMEM((1, ROUTE_COLS), F32)],
        compiler_params=_params(),
        name="router",
    )(x, jnp.concatenate([w_hi, w_lo], axis=1), bias.reshape(1, -1))
    return ri, rg, cnt[0, N_GROUPS:N_GROUPS + N_EXPERTS]


def _to_slabs(slab_ref, val):
    for j in range(SUBLANES):
        slab_ref[:, j, :] = val[:, j * LANES:(j + 1) * LANES]


def _from_slabs(slab_ref):
    return jnp.concatenate([slab_ref[:, j, :] for j in range(SUBLANES)], axis=1)


def _dispatch_kernel(pad_ref, nb_ref, dest_ref, x_ref, xs_hbm, slab_scr, zero_scr, sem):
    step = pl.program_id(0)
    tokens = x_ref.shape[0]

    @pl.when(step == 0)
    def _():
        zero_scr[...] = jnp.zeros_like(zero_scr)

        def fill(row):
            return pltpu.make_async_copy(zero_scr, xs_hbm.at[pl.ds(row, EXPERT_BLOCK)], sem)

        def fill_block(b, _):
            f = fill(b * EXPERT_BLOCK)
            f.start()
            f.wait()
            return 0

        for e in range(N_EXPERTS):
            fill(pad_ref[e]).start()
        for e in range(N_EXPERTS):
            fill(pad_ref[e]).wait()
        lax.fori_loop(nb_ref[0], xs_hbm.shape[0] // EXPERT_BLOCK, fill_block, 0)

    _to_slabs(slab_scr, x_ref[...])

    def row_copy(a):
        return pltpu.make_async_copy(slab_scr.at[pl.ds(a // TOP_K, 1)], xs_hbm.at[pl.ds(dest_ref[a], 1)], sem)

    def issue(a, _):
        row_copy(a).start()
        return 0

    def drain(a, _):
        row_copy(a).wait()
        return 0

    lax.fori_loop(0, tokens * TOP_K, issue, 0)
    lax.fori_loop(0, tokens * TOP_K, drain, 0)


def _dispatch(x, dest_flat, pad_start, n_used, rows):
    n, d = x.shape
    assert d == SUBLANES * LANES
    return pl.pallas_call(
        _dispatch_kernel,
        out_shape=jax.ShapeDtypeStruct((rows + EXPERT_BLOCK, SUBLANES, LANES), F32),
        grid_spec=pltpu.PrefetchScalarGridSpec(
            num_scalar_prefetch=2,
            grid=(n // TM,),
            in_specs=[pl.BlockSpec((TM * TOP_K,), lambda i, pad, nb: (i,), memory_space=pltpu.SMEM),
                      pl.BlockSpec((TM, d), lambda i, pad, nb: (i, 0))],
            out_specs=pl.BlockSpec(memory_space=pl.ANY),
            scratch_shapes=[pltpu.VMEM((TM, SUBLANES, LANES), F32),
                            pltpu.VMEM((EXPERT_BLOCK, SUBLANES, LANES), F32),
                            pltpu.SemaphoreType.DMA],
        ),
        compiler_params=_params(),
        name="dispatch",
    )(pad_start, n_used, dest_flat, x)


def _expert_kernel(be_ref, nb_ref, xs_ref, w1_ref, w3_ref, w2_ref, y_ref):
    used = pl.program_id(0) < nb_ref[0]

    @pl.when(used)
    def _():
        xb = _from_slabs(xs_ref).astype(BF16)
        h1 = _dot(xb, w1_ref[...])
        h3 = _dot(xb, w3_ref[...])
        hid = h1 * (1.0 / (1.0 + jnp.exp(-h1))) * h3
        _to_slabs(y_ref, _dot(hid.astype(BF16), w2_ref[...]))

    @pl.when(jnp.logical_not(used))
    def _():
        y_ref[...] = jnp.zeros_like(y_ref)


def _expert_mlp(xs, block_expert, n_used, w1, w3, w2, n_blocks):
    d, de = w1.shape[1], w1.shape[2]

    def row_map(b, be, nb):
        return (jnp.minimum(b, nb[0] - 1), 0, 0)

    def w_map(b, be, nb):
        return (be[jnp.minimum(b, nb[0] - 1)], 0, 0)

    return pl.pallas_call(
        _expert_kernel,
        out_shape=jax.ShapeDtypeStruct((n_blocks * EXPERT_BLOCK, SUBLANES, LANES), F32),
        grid_spec=pltpu.PrefetchScalarGridSpec(
            num_scalar_prefetch=2,
            grid=(n_blocks,),
            in_specs=[pl.BlockSpec((EXPERT_BLOCK, SUBLANES, LANES), row_map),
                      pl.BlockSpec((None, d, de), w_map),
                      pl.BlockSpec((None, d, de), w_map),
                      pl.BlockSpec((None, de, d), w_map)],
            out_specs=pl.BlockSpec((EXPERT_BLOCK, SUBLANES, LANES), lambda b, be, nb: (b, 0, 0)),
        ),
        compiler_params=_params(),
        name="expert_mlp",
    )(block_expert, n_used, xs, w1, w3, w2)


def _combine_kernel(dest_ref, y_hbm, x_ref, gate_ref, g_ref, b_ref, o_ref, ybuf, sem):
    tm = x_ref.shape[0]

    def issue(a, _):
        pltpu.make_async_copy(y_hbm.at[pl.ds(dest_ref[a], 1)],
                              ybuf.at[a % TOP_K, pl.ds(a // TOP_K, 1)], sem).start()
        return 0

    def drain(a, _):
        pltpu.make_async_copy(y_hbm.at[pl.ds(dest_ref[a], 1)],
                              ybuf.at[a % TOP_K, pl.ds(a // TOP_K, 1)], sem).wait()
        return 0

    lax.fori_loop(0, tm * TOP_K, issue, 0)
    lax.fori_loop(0, tm * TOP_K, drain, 0)
    gates = gate_ref[...]
    ff = gates[:, 0:1] * _from_slabs(ybuf.at[0]) + gates[:, 1:2] * _from_slabs(ybuf.at[1])
    o_ref[...] = _layer_norm(ALPHA * x_ref[...] + ff, g_ref[...], b_ref[...])


def _combine(x, y, dest_flat, gates, g, b):
    n, d = x.shape
    tc = TM // 2
    return pl.pallas_call(
        _combine_kernel,
        out_shape=jax.ShapeDtypeStruct((n, d), F32),
        grid=(n // tc,),
        in_specs=[pl.BlockSpec((tc * TOP_K,), lambda i: (i,), memory_space=pltpu.SMEM),
                  pl.BlockSpec(memory_space=pl.ANY),
                  pl.BlockSpec((tc, d), lambda i: (i, 0)),
                  pl.BlockSpec((tc, TOP_K), lambda i: (i, 0)),
                  _const_spec((1, d)), _const_spec((1, d))],
        out_specs=pl.BlockSpec((tc, d), lambda i: (i, 0)),
        scratch_shapes=[pltpu.VMEM((TOP_K, tc, SUBLANES, LANES), F32), pltpu.SemaphoreType.DMA],
        compiler_params=_params(),
        name="combine",
    )(dest_flat, y, x, gates, g.reshape(1, -1), b.reshape(1, -1))


def _moe(x, wr_g, br_g, wr_e, br_e, w1, w3, w2, g, b):
    n, d = x.shape
    n_blocks = (n * TOP_K + N_EXPERTS * (EXPERT_BLOCK - 1) + EXPERT_BLOCK - 1) // EXPERT_BLOCK
    ri, gates, counts = _router(x, wr_g, br_g, wr_e, br_e)

    blocks_e = (counts + EXPERT_BLOCK - 1) // EXPERT_BLOCK
    blocks_end = jnp.cumsum(blocks_e)
    run_start = (blocks_end - blocks_e) * EXPERT_BLOCK
    n_used = blocks_end[-1:].astype(I32)
    block_expert = jnp.minimum(
        jnp.searchsorted(blocks_end, jnp.arange(n_blocks, dtype=I32), side="right"),
        N_EXPERTS - 1).astype(I32)
    dest = (run_start[ri[:, :TOP_K]] + ri[:, TOP_K:]).astype(I32).reshape(-1)
    pad_start = (run_start + counts).astype(I32)

    xs = _dispatch(x, dest, pad_start, n_used, n_blocks * EXPERT_BLOCK)
    y = _expert_mlp(xs, block_expert, n_used, w1, w3, w2, n_blocks)
    return _combine(x, y, dest, gates, g, b)


def kernel(x, mem, w_in_even, w_pool, pool_scale, ln_v_g, ln_v_b, w_spatial, b_spatial, w_out_even,
           w_in_odd, conv_w, conv_b, w_out_odd, wq_x, wk_x, wv_x, wo_x, ln_g, ln_b, wr_group,
           br_group, wr_expert, br_expert, w1, w3, w2):
    bsz, seq, d = x.shape
    assert seq % TM == 0 and d % LANES == 0
    mlen = mem.shape[1]
    k_all, v_all = _memory_kv(mem.reshape(bsz * mlen, d), wk_x, wv_x)
    k_all = k_all.reshape(DEPTH, bsz, mlen, d)
    v_all = v_all.reshape(DEPTH, bsz, mlen, d)
    w1b, w3b, w2b = w1.astype(BF16), w3.astype(BF16), w2.astype(BF16)

    h = x.reshape(bsz * seq, d)
    for l in range(DEPTH):
        i = l // 2
        if l % 2 == 0:
            h = _even_mixer(h, seq, w_in_even[i], w_pool[i], pool_scale[i], ln_v_g[i], ln_v_b[i],
                            w_spatial[i], b_spatial[i], w_out_even[i], ln_g[l, 0], ln_b[l, 0])
        else:
            h = _odd_mixer(h, seq, w_in_odd[i], conv_w[i], conv_b[i], w_out_odd[i],
                           ln_g[l, 0], ln_b[l, 0])
        h = _cross_attn(h, seq, k_all[l], v_all[l], wq_x[l], wo_x[l], ln_g[l, 1], ln_b[l, 1])
        h = _moe(h, wr_group[l], br_group[l], wr_expert[l], br_expert[l], w1b[l], w3b[l], w2b[l],
                 ln_g[l, 2], ln_b[l, 2])
    return h.reshape(bsz, seq, d)
```

```python
import functools
import math

import jax
import jax.numpy as jnp
from jax import lax
from jax.experimental import pallas as pl
from jax.experimental.pallas import tpu as pltpu

F32 = jnp.float32
BF16 = jnp.bfloat16
I32 = jnp.int32

POOL_WINDOWS = (2, 4, 8, 16)
N_SG_HEADS = 4
CHUNK = 128
CONV_WIDTH = 3
N_XHEADS = 4
N_GROUPS = 4
EXPERTS_PER_GROUP = 8
N_EXPERTS = N_GROUPS * EXPERTS_PER_GROUP
TOP_K = 2
DEPTH = 4
ALPHA = (2.0 * DEPTH) ** 0.25
LN_EPS = 1e-5

LANES = 128
SUBLANES = 8
TM = 512
POOL_HALO = 16
CONV_HALO = 8
EXPERT_BLOCK = 256
ROUTE_COLS = 128
VMEM_LIMIT = 48 * 1024 * 1024

_NT = (((1,), (1,)), ((), ()))


def _dot(a, b):
    return jnp.dot(a, b, preferred_element_type=F32)


def _layer_norm(y, g, b):
    mu = jnp.mean(y, axis=-1, keepdims=True)
    yc = y - mu
    var = jnp.mean(yc * yc, axis=-1, keepdims=True)
    return yc * lax.rsqrt(var + LN_EPS) * g + b


def _gelu_tanh(x):
    c = math.sqrt(2.0 / math.pi)
    return 0.5 * x * (1.0 + jnp.tanh(c * (x + 0.044715 * (x * x * x))))


def _const_spec(shape):
    nd = len(shape)
    return pl.BlockSpec(shape, lambda i: (0,) * nd)


def _params():
    return pltpu.CompilerParams(dimension_semantics=("arbitrary",), vmem_limit_bytes=VMEM_LIMIT)


def _even_kernel(x_ref, xh_ref, win_ref, wpool_ref, pscale_ref, lvg_ref, lvb_ref, ws_ref, bst_ref,
                 wout_ref, g_ref, b_ref, o_ref, a_scr, cat_scr, *, tiles_per_seq):
    tm = x_ref.shape[0]
    d_pool = a_scr.shape[1]
    d_sg = lvg_ref.shape[1]
    pgd = d_pool // len(POOL_WINDOWS)
    hd_dim = d_sg // N_SG_HEADS
    seq_tile = pl.program_id(0) % tiles_per_seq

    x = x_ref[...]
    h = _dot(x.astype(BF16), win_ref[...])

    ah = _dot(xh_ref[...].astype(BF16), win_ref[:, :d_pool])
    a_scr[0:POOL_HALO, :] = jnp.where(seq_tile == 0, 0.0, ah)
    a_scr[POOL_HALO:POOL_HALO + tm, :] = h[:, :d_pool]
    pos = seq_tile * tm + lax.broadcasted_iota(I32, (tm, 1), 0)
    for g, w in enumerate(POOL_WINDOWS):
        cs = slice(g * pgd, (g + 1) * pgd)
        tok = a_scr[POOL_HALO:POOL_HALO + tm, cs]
        acc = tok
        for j in range(1, w):
            acc = acc + a_scr[POOL_HALO - j:POOL_HALO - j + tm, cs]
        cnt = jnp.minimum(pos + 1, w).astype(F32)
        d = acc / cnt - tok
        yg = _dot(d.astype(BF16), wpool_ref[g])
        cat_scr[:, cs] = (yg * pscale_ref[:, cs]).astype(BF16)

    z = _gelu_tanh(h[:, d_pool:])
    u = z[:, :d_sg]
    v = _layer_norm(z[:, d_sg:], lvg_ref[...], lvb_ref[...]).astype(BF16)
    row = lax.broadcasted_iota(I32, (CHUNK, CHUNK), 0)
    col = lax.broadcasted_iota(I32, (CHUNK, CHUNK), 1)
    for hd in range(N_SG_HEADS):
        hs = slice(hd * hd_dim, (hd + 1) * hd_dim)
        wsm = jnp.where(row >= col, ws_ref[hd], 0.0).astype(BF16)
        bcol = bst_ref[:, hd:hd + 1]
        for ck in range(tm // CHUNK):
            rs = slice(ck * CHUNK, (ck + 1) * CHUNK)
            sv = _dot(wsm, v[rs, hs]) + bcol
            cat_scr[rs, d_pool + hd * hd_dim:d_pool + (hd + 1) * hd_dim] = (u[rs, hs] * sv).astype(BF16)

    mix = _dot(cat_scr[...], wout_ref[...])
    o_ref[...] = _layer_norm(ALPHA * x + mix, g_ref[...], b_ref[...])


def _even_mixer(x, seq, w_in, w_pool, pool_scale, ln_v_g, ln_v_b, w_spatial, b_spatial, w_out, g, b):
    n, d = x.shape
    d_in = w_in.shape[1]
    d_pool = pool_scale.shape[0]
    d_sg = ln_v_g.shape[0]
    kern = functools.partial(_even_kernel, tiles_per_seq=seq // TM)
    halo_blocks = TM // POOL_HALO
    return pl.pallas_call(
        kern,
        out_shape=jax.ShapeDtypeStruct((n, d), F32),
        grid=(n // TM,),
        in_specs=[
            pl.BlockSpec((TM, d), lambda i: (i, 0)),
            pl.BlockSpec((POOL_HALO, d), lambda i: (jnp.maximum(i * halo_blocks - 1, 0), 0)),
            _const_spec((d, d_in)),
            _const_spec(w_pool.shape),
            _const_spec((1, d_pool)),
            _const_spec((1, d_sg)),
            _const_spec((1, d_sg)),
            _const_spec(w_spatial.shape),
            _const_spec((CHUNK, N_SG_HEADS)),
            _const_spec(w_out.shape),
            _const_spec((1, d)),
            _const_spec((1, d)),
        ],
        out_specs=pl.BlockSpec((TM, d), lambda i: (i, 0)),
        scratch_shapes=[pltpu.VMEM((POOL_HALO + TM, d_pool), F32), pltpu.VMEM((TM, d_pool + d_sg), BF16)],
        compiler_params=_params(),
        name="even_mixer",
    )(x, x, w_in.astype(BF16), w_pool.astype(BF16), pool_scale.reshape(1, -1), ln_v_g.reshape(1, -1),
      ln_v_b.reshape(1, -1), w_spatial, b_spatial.T, w_out.astype(BF16), g.reshape(1, -1), b.reshape(1, -1))


def _odd_kernel(x_ref, xh_ref, win_ref, cwt_ref, cb_ref, wout_ref, g_ref, b_ref, o_ref, zc_scr,
                *, tiles_per_seq):
    tm, d = x_ref.shape
    seq_tile = pl.program_id(0) % tiles_per_seq

    x = x_ref[...]
    h = _dot(x.astype(BF16), win_ref[...])
    hh = _dot(xh_ref[...].astype(BF16), win_ref[:, d:])
    zc_scr[0:CONV_HALO, :] = jnp.where(seq_tile == 0, 0.0, hh[:, :d] * hh[:, d:])
    zc_scr[CONV_HALO:CONV_HALO + tm, :] = h[:, d:2 * d] * h[:, 2 * d:]
    conv = cb_ref[...]
    for j in range(CONV_WIDTH):
        off = CONV_HALO - (CONV_WIDTH - 1) + j
        conv = conv + zc_scr[off:off + tm, :] * cwt_ref[j:j + 1, :]
    y = _dot((h[:, :d] * conv).astype(BF16), wout_ref[...])
    o_ref[...] = _layer_norm(ALPHA * x + y, g_ref[...], b_ref[...])


def _odd_mixer(x, seq, w_in, conv_w, conv_b, w_out, g, b):
    n, d = x.shape
    kern = functools.partial(_odd_kernel, tiles_per_seq=seq // TM)
    halo_blocks = TM // CONV_HALO
    return pl.pallas_call(
        kern,
        out_shape=jax.ShapeDtypeStruct((n, d), F32),
        grid=(n // TM,),
        in_specs=[
            pl.BlockSpec((TM, d), lambda i: (i, 0)),
            pl.BlockSpec((CONV_HALO, d), lambda i: (jnp.maximum(i * halo_blocks - 1, 0), 0)),
            _const_spec(w_in.shape),
            _const_spec((CONV_WIDTH, d)),
            _const_spec((1, d)),
            _const_spec(w_out.shape),
            _const_spec((1, d)),
            _const_spec((1, d)),
        ],
        out_specs=pl.BlockSpec((TM, d), lambda i: (i, 0)),
        scratch_shapes=[pltpu.VMEM((CONV_HALO + TM, d), F32)],
        compiler_params=_params(),
        name="odd_mixer",
    )(x, x, w_in.astype(BF16), conv_w.T, conv_b.reshape(1, -1), w_out.astype(BF16),
      g.reshape(1, -1), b.reshape(1, -1))


def _kv_kernel(mem_ref, wk_ref, wv_ref, k_ref, v_ref):
    m = mem_ref[...].astype(BF16)
    k_ref[...] = _dot(m, wk_ref[...]).astype(BF16)
    v_ref[...] = _dot(m, wv_ref[...]).astype(BF16)


def _memory_kv(mem2d, wk, wv):
    nl, d, _ = wk.shape
    rows = mem2d.shape[0]
    out = jax.ShapeDtypeStruct((nl, rows, d), BF16)
    wspec = pl.BlockSpec((None, d, d), lambda l: (l, 0, 0))
    ospec = pl.BlockSpec((None, rows, d), lambda l: (l, 0, 0))
    return pl.pallas_call(
        _kv_kernel,
        out_shape=(out, out),
        grid=(nl,),
        in_specs=[_const_spec((rows, d)), wspec, wspec],
        out_specs=(ospec, ospec),
        compiler_params=_params(),
        name="memory_kv",
    )(mem2d, wk.astype(BF16), wv.astype(BF16))


def _attn_kernel(x_ref, k_ref, v_ref, wq_ref, wo_ref, g_ref, b_ref, o_ref, o_scr):
    d = x_ref.shape[1]
    hd_dim = d // N_XHEADS
    x = x_ref[...]
    q = _dot(x.astype(BF16), wq_ref[...]) * (1.0 / math.sqrt(hd_dim))
    for hd in range(N_XHEADS):
        hs = slice(hd * hd_dim, (hd + 1) * hd_dim)
        s = lax.dot_general(q[:, hs].astype(BF16), k_ref[:, hs], _NT, preferred_element_type=F32)
        p = jnp.exp(s - jnp.max(s, axis=-1, keepdims=True))
        p = p / jnp.sum(p, axis=-1, keepdims=True)
        o_scr[:, hs] = _dot(p.astype(BF16), v_ref[:, hs]).astype(BF16)
    xa = _dot(o_scr[...], wo_ref[...])
    o_ref[...] = _layer_norm(ALPHA * x + xa, g_ref[...], b_ref[...])


def _cross_attn(x, seq, k, v, wq, wo, g, b):
    n, d = x.shape
    m = k.shape[1]
    tiles_per_seq = seq // TM
    kvspec = pl.BlockSpec((None, m, d), lambda i: (i // tiles_per_seq, 0, 0))
    return pl.pallas_call(
        _attn_kernel,
        out_shape=jax.ShapeDtypeStruct((n, d), F32),
        grid=(n // TM,),
        in_specs=[
            pl.BlockSpec((TM, d), lambda i: (i, 0)),
            kvspec, kvspec,
            _const_spec((d, d)), _const_spec((d, d)),
            _const_spec((1, d)), _const_spec((1, d)),
        ],
        out_specs=pl.BlockSpec((TM, d), lambda i: (i, 0)),
        scratch_shapes=[pltpu.VMEM((TM, d), BF16)],
        compiler_params=_params(),
        name="cross_attn",
    )(x, k, v, wq.astype(BF16), wo.astype(BF16), g.reshape(1, -1), b.reshape(1, -1))


def _router_kernel(x_ref, wcat_ref, bias_ref, ri_ref, rg_ref, cnt_ref, carry_scr):
    tm = x_ref.shape[0]
    step = pl.program_id(0)

    @pl.when(step == 0)
    def _():
        carry_scr[...] = jnp.zeros_like(carry_scr)

    x = x_ref[...]
    xh = x.astype(BF16)
    xl = (x - xh.astype(F32)).astype(BF16)
    r1 = _dot(xh, wcat_ref[...])
    r2 = _dot(xl, wcat_ref[:, :ROUTE_COLS])
    logits = r1[:, :ROUTE_COLS] + r1[:, ROUTE_COLS:] + r2 + bias_ref[...]

    lane = lax.broadcasted_iota(I32, (tm, ROUTE_COLS), 1).astype(F32)
    neg = -jnp.inf

    def first_argmax(vals):
        mx = jnp.max(vals, axis=-1, keepdims=True)
        idx = jnp.min(jnp.where(vals == mx, lane, float(ROUTE_COLS)), axis=-1, keepdims=True)
        return mx, idx

    gl = jnp.where(lane < N_GROUPS, logits, neg)
    gmax, g_sel = first_argmax(gl)
    gate_g = 1.0 / jnp.sum(jnp.exp(gl - gmax), axis=-1, keepdims=True)

    lo = N_GROUPS + g_sel * EXPERTS_PER_GROUP
    el = jnp.where((lane >= lo) & (lane < lo + EXPERTS_PER_GROUP), logits, neg)
    m1, i1 = first_argmax(el)
    m2, i2 = first_argmax(jnp.where(lane == i1, neg, el))
    e21 = jnp.exp(m2 - m1)
    w1 = 1.0 / (1.0 + e21)
    w2 = e21 / (1.0 + e21)

    oh1 = lane == i1
    oh2 = lane == i2
    oh = (oh1 | oh2).astype(BF16)
    r = lax.broadcasted_iota(I32, (tm, tm), 0)
    c = lax.broadcasted_iota(I32, (tm, tm), 1)
    before = _dot((r > c).astype(BF16), oh) + carry_scr[...]
    rank1 = jnp.sum(jnp.where(oh1, before, 0.0), axis=-1, keepdims=True)
    rank2 = jnp.sum(jnp.where(oh2, before, 0.0), axis=-1, keepdims=True)
    carry_scr[...] += jnp.sum(oh.astype(F32), axis=0, keepdims=True)

    l4 = lax.broadcasted_iota(I32, ri_ref.shape, 1)
    ri_ref[...] = jnp.where(l4 == 0, i1 - N_GROUPS,
                            jnp.where(l4 == 1, i2 - N_GROUPS,
                                      jnp.where(l4 == 2, rank1, rank2))).astype(I32)
    l2 = lax.broadcasted_iota(I32, rg_ref.shape, 1)
    rg_ref[...] = jnp.where(l2 == 0, gate_g * w1, gate_g * w2)
    cnt_ref[...] = carry_scr[...].astype(I32)


def _router(x, wr_g, br_g, wr_e, br_e):
    n, d = x.shape
    w = jnp.concatenate([wr_g, jnp.transpose(wr_e, (1, 0, 2)).reshape(d, N_EXPERTS)], axis=1)
    w = jnp.pad(w, ((0, 0), (0, ROUTE_COLS - w.shape[1])))
    w_hi = w.astype(BF16)
    w_lo = (w - w_hi.astype(F32)).astype(BF16)
    bias = jnp.pad(jnp.concatenate([br_g, br_e.reshape(-1)]), (0, ROUTE_COLS - N_GROUPS - N_EXPERTS))
    ri, rg, cnt = pl.pallas_call(
        _router_kernel,
        out_shape=(jax.ShapeDtypeStruct((n, 4), I32), jax.ShapeDtypeStruct((n, TOP_K), F32),
                   jax.ShapeDtypeStruct((1, ROUTE_COLS), I32)),
        grid=(n // TM,),
        in_specs=[pl.BlockSpec((TM, d), lambda i: (i, 0)),
                  _const_spec((d, 2 * ROUTE_COLS)), _const_spec((1, ROUTE_COLS))],
        out_specs=(pl.BlockSpec((TM, 4), lambda i: (i, 0)), pl.BlockSpec((TM, TOP_K), lambda i: (i, 0)),
                   _const_spec((1, ROUTE_COLS))),
        scratch_shapes=[pltpu.VMEM((1, ROUTE_COLS), F32)],
        compiler_params=_params(),
        name="router",
    )(x, jnp.concatenate([w_hi, w_lo], axis=1), bias.reshape(1, -1))
    return ri, rg, cnt[0, N_GROUPS:N_GROUPS + N_EXPERTS]


def _to_slabs(slab_ref, val):
    for j in range(SUBLANES):
        slab_ref[:, j, :] = val[:, j * LANES:(j + 1) * LANES]


def _from_slabs(slab_ref):
    return jnp.concatenate([slab_ref[:, j, :] for j in range(SUBLANES)], axis=1)


def _dispatch_kernel(pad_ref, nb_ref, dest_ref, x_ref, xs_hbm, slab_scr, zero_scr, sems, fill_sem):
    step = pl.program_id(0)
    tokens = x_ref.shape[0]
    slot = step % 2

    @pl.when(step == 0)
    def _():
        zero_scr[...] = jnp.zeros_like(zero_scr)

        def fill(row):
            return pltpu.make_async_copy(zero_scr, xs_hbm.at[pl.ds(row, EXPERT_BLOCK)], fill_sem)

        def fill_block(b, _):
            f = fill(b * EXPERT_BLOCK)
            f.start()
            f.wait()
            return 0

        for e in range(N_EXPERTS):
            fill(pad_ref[e]).start()
        for e in range(N_EXPERTS):
            fill(pad_ref[e]).wait()
        lax.fori_loop(nb_ref[0], xs_hbm.shape[0] // EXPERT_BLOCK, fill_block, 0)

    def wait_slot(s):
        for _ in range(TOP_K):
            pltpu.make_async_copy(slab_scr.at[s], xs_hbm.at[pl.ds(0, tokens)], sems.at[s]).wait()

    _to_slabs(slab_scr.at[slot], x_ref[...])

    def issue(t, _):
        for k in range(TOP_K):
            pltpu.make_async_copy(slab_scr.at[slot, pl.ds(t, 1)],
                                  xs_hbm.at[pl.ds(dest_ref[TOP_K * t + k], 1)], sems.at[slot]).start()
        return 0

    lax.fori_loop(0, tokens, issue, 0, unroll=4)

    @pl.when(step > 0)
    def _():
        wait_slot(1 - slot)

    @pl.when(step == pl.num_programs(0) - 1)
    def _():
        wait_slot(slot)


def _dispatch(x, dest_flat, pad_start, n_used, rows):
    n, d = x.shape
    assert d == SUBLANES * LANES
    return pl.pallas_call(
        _dispatch_kernel,
        out_shape=jax.ShapeDtypeStruct((rows + EXPERT_BLOCK, SUBLANES, LANES), F32),
        grid_spec=pltpu.PrefetchScalarGridSpec(
            num_scalar_prefetch=2,
            grid=(n // TM,),
            in_specs=[pl.BlockSpec((TM * TOP_K,), lambda i, pad, nb: (i,), memory_space=pltpu.SMEM),
                      pl.BlockSpec((TM, d), lambda i, pad, nb: (i, 0))],
            out_specs=pl.BlockSpec(memory_space=pl.ANY),
            scratch_shapes=[pltpu.VMEM((2, TM, SUBLANES, LANES), F32),
                            pltpu.VMEM((EXPERT_BLOCK, SUBLANES, LANES), F32),
                            pltpu.SemaphoreType.DMA((2,)),
                            pltpu.SemaphoreType.DMA],
        ),
        compiler_params=_params(),
        name="dispatch",
    )(pad_start, n_used, dest_flat, x)


def _expert_kernel(be_ref, nb_ref, xs_ref, w1_ref, w3_ref, w2_ref, y_ref, w1_scr, w3_scr, w2_scr):
    b = pl.program_id(0)
    used = b < nb_ref[0]

    @pl.when(used & ((b == 0) | (be_ref[b] != be_ref[jnp.maximum(b - 1, 0)])))
    def _():
        w1_scr[...] = w1_ref[...].astype(BF16)
        w3_scr[...] = w3_ref[...].astype(BF16)
        w2_scr[...] = w2_ref[...].astype(BF16)

    @pl.when(used)
    def _():
        xb = _from_slabs(xs_ref).astype(BF16)
        h1 = _dot(xb, w1_scr[...])
        h3 = _dot(xb, w3_scr[...])
        hid = h1 * (1.0 / (1.0 + jnp.exp(-h1))) * h3
        _to_slabs(y_ref, _dot(hid.astype(BF16), w2_scr[...]))

    @pl.when(jnp.logical_not(used))
    def _():
        y_ref[...] = jnp.zeros_like(y_ref)


def _expert_mlp(xs, block_expert, n_used, layer, w1, w3, w2, n_blocks):
    d, de = w1.shape[2], w1.shape[3]

    def row_map(b, be, nb):
        return (jnp.minimum(b, nb[0] - 1), 0, 0)

    def w_map(b, be, nb):
        return (layer, be[jnp.minimum(b, nb[0] - 1)], 0, 0)

    return pl.pallas_call(
        _expert_kernel,
        out_shape=jax.ShapeDtypeStruct((n_blocks * EXPERT_BLOCK, SUBLANES, LANES), F32),
        grid_spec=pltpu.PrefetchScalarGridSpec(
            num_scalar_prefetch=2,
            grid=(n_blocks,),
            in_specs=[pl.BlockSpec((EXPERT_BLOCK, SUBLANES, LANES), row_map),
                      pl.BlockSpec((None, None, d, de), w_map),
                      pl.BlockSpec((None, None, d, de), w_map),
                      pl.BlockSpec((None, None, de, d), w_map)],
            out_specs=pl.BlockSpec((EXPERT_BLOCK, SUBLANES, LANES), lambda b, be, nb: (b, 0, 0)),
            scratch_shapes=[pltpu.VMEM((d, de), BF16), pltpu.VMEM((d, de), BF16), pltpu.VMEM((de, d), BF16)],
        ),
        compiler_params=_params(),
        name="expert_mlp",
    )(block_expert, n_used, xs, w1, w3, w2)


def _combine_kernel(dest_ref, dest_next_ref, y_hbm, x_ref, gate_ref, g_ref, b_ref, o_ref, ybuf, sems):
    tm = x_ref.shape[0]
    step = pl.program_id(0)
    slot = step % 2

    def gather(dref, s):
        def issue(t, _):
            for k in range(TOP_K):
                pltpu.make_async_copy(y_hbm.at[pl.ds(dref[TOP_K * t + k], 1)],
                                      ybuf.at[s, k, pl.ds(t, 1)], sems.at[s]).start()
            return 0
        lax.fori_loop(0, tm, issue, 0, unroll=4)

    @pl.when(step == 0)
    def _():
        gather(dest_ref, slot)

    @pl.when(step + 1 < pl.num_programs(0))
    def _():
        gather(dest_next_ref, 1 - slot)

    for k in range(TOP_K):
        pltpu.make_async_copy(y_hbm.at[pl.ds(0, tm)], ybuf.at[slot, k], sems.at[slot]).wait()
    gates = gate_ref[...]
    ff = gates[:, 0:1] * _from_slabs(ybuf.at[slot, 0]) + gates[:, 1:2] * _from_slabs(ybuf.at[slot, 1])
    o_ref[...] = _layer_norm(ALPHA * x_ref[...] + ff, g_ref[...], b_ref[...])


def _combine(x, y, dest_flat, gates, g, b):
    n, d = x.shape
    tc = TM // 2
    last = n // tc - 1
    return pl.pallas_call(
        _combine_kernel,
        out_shape=jax.ShapeDtypeStruct((n, d), F32),
        grid=(n // tc,),
        in_specs=[pl.BlockSpec((tc * TOP_K,), lambda i: (i,), memory_space=pltpu.SMEM),
                  pl.BlockSpec((tc * TOP_K,), lambda i: (jnp.minimum(i + 1, last),), memory_space=pltpu.SMEM),
                  pl.BlockSpec(memory_space=pl.ANY),
                  pl.BlockSpec((tc, d), lambda i: (i, 0)),
                  pl.BlockSpec((tc, TOP_K), lambda i: (i, 0)),
                  _const_spec((1, d)), _const_spec((1, d))],
        out_specs=pl.BlockSpec((tc, d), lambda i: (i, 0)),
        scratch_shapes=[pltpu.VMEM((2, TOP_K, tc, SUBLANES, LANES), F32), pltpu.SemaphoreType.DMA((2,))],
        compiler_params=_params(),
        name="combine",
    )(dest_flat, dest_flat, y, x, gates, g.reshape(1, -1), b.reshape(1, -1))


def _moe(x, layer, wr_g, br_g, wr_e, br_e, w1, w3, w2, g, b):
    n, d = x.shape
    n_blocks = (n * TOP_K + N_EXPERTS * (EXPERT_BLOCK - 1) + EXPERT_BLOCK - 1) // EXPERT_BLOCK
    ri, gates, counts = _router(x, wr_g, br_g, wr_e, br_e)

    blocks_e = (counts + EXPERT_BLOCK - 1) // EXPERT_BLOCK
    blocks_end = jnp.cumsum(blocks_e)
    run_start = (blocks_end - blocks_e) * EXPERT_BLOCK
    n_used = blocks_end[-1:].astype(I32)
    block_ids = jnp.arange(n_blocks, dtype=I32)
    block_expert = jnp.minimum(jnp.sum(blocks_end[None, :] <= block_ids[:, None], axis=1),
                               N_EXPERTS - 1).astype(I32)
    expert_ids = jnp.arange(N_EXPERTS, dtype=I32)
    start_of = jnp.sum(jnp.where(ri[:, :TOP_K, None] == expert_ids, run_start, 0), axis=-1)
    dest = (start_of + ri[:, TOP_K:]).astype(I32).reshape(-1)
    pad_start = (run_start + counts).astype(I32)

    xs = _dispatch(x, dest, pad_start, n_used, n_blocks * EXPERT_BLOCK)
    y = _expert_mlp(xs, block_expert, n_used, layer, w1, w3, w2, n_blocks)
    return _combine(x, y, dest, gates, g, b)


def kernel(x, mem, w_in_even, w_pool, pool_scale, ln_v_g, ln_v_b, w_spatial, b_spatial, w_out_even,
           w_in_odd, conv_w, conv_b, w_out_odd, wq_x, wk_x, wv_x, wo_x, ln_g, ln_b, wr_group,
           br_group, wr_expert, br_expert, w1, w3, w2):
    bsz, seq, d = x.shape
    assert seq % TM == 0 and d % LANES == 0
    mlen = mem.shape[1]
    k_all, v_all = _memory_kv(mem.reshape(bsz * mlen, d), wk_x, wv_x)
    k_all = k_all.reshape(DEPTH, bsz, mlen, d)
    v_all = v_all.reshape(DEPTH, bsz, mlen, d)
    h = x.reshape(bsz * seq, d)
    for l in range(DEPTH):
        i = l // 2
        if l % 2 == 0:
            h = _even_mixer(h, seq, w_in_even[i], w_pool[i], pool_scale[i], ln_v_g[i], ln_v_b[i],
                            w_spatial[i], b_spatial[i], w_out_even[i], ln_g[l, 0], ln_b[l, 0])
        else:
            h = _odd_mixer(h, seq, w_in_odd[i], conv_w[i], conv_b[i], w_out_odd[i],
                           ln_g[l, 0], ln_b[l, 0])
        h = _cross_attn(h, seq, k_all[l], v_all[l], wq_x[l], wo_x[l], ln_g[l, 1], ln_b[l, 1])
        h = _moe(h, l, wr_group[l], br_group[l], wr_expert[l], br_expert[l], w1, w3, w2,
                 ln_g[l, 2], ln_b[l, 2])
    return h.reshape(bsz, seq, d)
```

```python
import functools
import math

import jax
import jax.numpy as jnp
from jax import lax
from jax.experimental import pallas as pl
from jax.experimental.pallas import tpu as pltpu
from jax.experimental.pallas import tpu_sc as plsc

F32 = jnp.float32
BF16 = jnp.bfloat16
I32 = jnp.int32

POOL_WINDOWS = (2, 4, 8, 16)
N_SG_HEADS = 4
CHUNK = 128
CONV_WIDTH = 3
N_XHEADS = 4
N_GROUPS = 4
EXPERTS_PER_GROUP = 8
N_EXPERTS = N_GROUPS * EXPERTS_PER_GROUP
TOP_K = 2
DEPTH = 4
ALPHA = (2.0 * DEPTH) ** 0.25
LN_EPS = 1e-5

LANES = 128
SC_CORES = 2
SC_WORKERS = 32
SC_CHUNK = 32
TM = 512
POOL_HALO = 16
CONV_HALO = 8
EXPERT_BLOCK = 256
ROUTE_COLS = 128
VMEM_LIMIT = 48 * 1024 * 1024

_NT = (((1,), (1,)), ((), ()))


def _dot(a, b):
    return jnp.dot(a, b, preferred_element_type=F32)


def _layer_norm(y, g, b):
    mu = jnp.mean(y, axis=-1, keepdims=True)
    yc = y - mu
    var = jnp.mean(yc * yc, axis=-1, keepdims=True)
    return yc * lax.rsqrt(var + LN_EPS) * g + b


def _gelu_tanh(x):
    c = math.sqrt(2.0 / math.pi)
    return 0.5 * x * (1.0 + jnp.tanh(c * (x + 0.044715 * (x * x * x))))


def _const_spec(shape):
    nd = len(shape)
    return pl.BlockSpec(shape, lambda i: (0,) * nd)


def _params():
    return pltpu.CompilerParams(dimension_semantics=("arbitrary",), vmem_limit_bytes=VMEM_LIMIT)


def _even_kernel(x_ref, xh_ref, win_ref, wpool_ref, pscale_ref, lvg_ref, lvb_ref, ws_ref, bst_ref,
                 wout_ref, g_ref, b_ref, o_ref, a_scr, cat_scr, *, tiles_per_seq):
    tm = x_ref.shape[0]
    d_pool = a_scr.shape[1]
    d_sg = lvg_ref.shape[1]
    pgd = d_pool // len(POOL_WINDOWS)
    hd_dim = d_sg // N_SG_HEADS
    seq_tile = pl.program_id(0) % tiles_per_seq

    x = x_ref[...]
    h = _dot(x.astype(BF16), win_ref[...])

    ah = _dot(xh_ref[...].astype(BF16), win_ref[:, :d_pool])
    a_scr[0:POOL_HALO, :] = jnp.where(seq_tile == 0, 0.0, ah)
    a_scr[POOL_HALO:POOL_HALO + tm, :] = h[:, :d_pool]
    pos = seq_tile * tm + lax.broadcasted_iota(I32, (tm, 1), 0)
    for g, w in enumerate(POOL_WINDOWS):
        cs = slice(g * pgd, (g + 1) * pgd)
        tok = a_scr[POOL_HALO:POOL_HALO + tm, cs]
        acc = tok
        for j in range(1, w):
            acc = acc + a_scr[POOL_HALO - j:POOL_HALO - j + tm, cs]
        cnt = jnp.minimum(pos + 1, w).astype(F32)
        d = acc / cnt - tok
        yg = _dot(d.astype(BF16), wpool_ref[g])
        cat_scr[:, cs] = (yg * pscale_ref[:, cs]).astype(BF16)

    z = _gelu_tanh(h[:, d_pool:])
    u = z[:, :d_sg]
    v = _layer_norm(z[:, d_sg:], lvg_ref[...], lvb_ref[...]).astype(BF16)
    row = lax.broadcasted_iota(I32, (CHUNK, CHUNK), 0)
    col = lax.broadcasted_iota(I32, (CHUNK, CHUNK), 1)
    for hd in range(N_SG_HEADS):
        hs = slice(hd * hd_dim, (hd + 1) * hd_dim)
        wsm = jnp.where(row >= col, ws_ref[hd], 0.0).astype(BF16)
        bcol = bst_ref[:, hd:hd + 1]
        for ck in range(tm // CHUNK):
            rs = slice(ck * CHUNK, (ck + 1) * CHUNK)
            sv = _dot(wsm, v[rs, hs]) + bcol
            cat_scr[rs, d_pool + hd * hd_dim:d_pool + (hd + 1) * hd_dim] = (u[rs, hs] * sv).astype(BF16)

    mix = _dot(cat_scr[...], wout_ref[...])
    o_ref[...] = _layer_norm(ALPHA * x + mix, g_ref[...], b_ref[...])


def _even_mixer(x, seq, w_in, w_pool, pool_scale, ln_v_g, ln_v_b, w_spatial, b_spatial, w_out, g, b):
    n, d = x.shape
    d_in = w_in.shape[1]
    d_pool = pool_scale.shape[0]
    d_sg = ln_v_g.shape[0]
    kern = functools.partial(_even_kernel, tiles_per_seq=seq // TM)
    halo_blocks = TM // POOL_HALO
    return pl.pallas_call(
        kern,
        out_shape=jax.ShapeDtypeStruct((n, d), F32),
        grid=(n // TM,),
        in_specs=[
            pl.BlockSpec((TM, d), lambda i: (i, 0)),
            pl.BlockSpec((POOL_HALO, d), lambda i: (jnp.maximum(i * halo_blocks - 1, 0), 0)),
            _const_spec((d, d_in)),
            _const_spec(w_pool.shape),
            _const_spec((1, d_pool)),
            _const_spec((1, d_sg)),
            _const_spec((1, d_sg)),
            _const_spec(w_spatial.shape),
            _const_spec((CHUNK, N_SG_HEADS)),
            _const_spec(w_out.shape),
            _const_spec((1, d)),
            _const_spec((1, d)),
        ],
        out_specs=pl.BlockSpec((TM, d), lambda i: (i, 0)),
        scratch_shapes=[pltpu.VMEM((POOL_HALO + TM, d_pool), F32), pltpu.VMEM((TM, d_pool + d_sg), BF16)],
        compiler_params=_params(),
        name="even_mixer",
    )(x, x, w_in.astype(BF16), w_pool.astype(BF16), pool_scale.reshape(1, -1), ln_v_g.reshape(1, -1),
      ln_v_b.reshape(1, -1), w_spatial, b_spatial.T, w_out.astype(BF16), g.reshape(1, -1), b.reshape(1, -1))


def _odd_kernel(x_ref, xh_ref, win_ref, cwt_ref, cb_ref, wout_ref, g_ref, b_ref, o_ref, zc_scr,
                *, tiles_per_seq):
    tm, d = x_ref.shape
    seq_tile = pl.program_id(0) % tiles_per_seq

    x = x_ref[...]
    h = _dot(x.astype(BF16), win_ref[...])
    hh = _dot(xh_ref[...].astype(BF16), win_ref[:, d:])
    zc_scr[0:CONV_HALO, :] = jnp.where(seq_tile == 0, 0.0, hh[:, :d] * hh[:, d:])
    zc_scr[CONV_HALO:CONV_HALO + tm, :] = h[:, d:2 * d] * h[:, 2 * d:]
    conv = cb_ref[...]
    for j in range(CONV_WIDTH):
        off = CONV_HALO - (CONV_WIDTH - 1) + j
        conv = conv + zc_scr[off:off + tm, :] * cwt_ref[j:j + 1, :]
    y = _dot((h[:, :d] * conv).astype(BF16), wout_ref[...])
    o_ref[...] = _layer_norm(ALPHA * x + y, g_ref[...], b_ref[...])


def _odd_mixer(x, seq, w_in, conv_w, conv_b, w_out, g, b):
    n, d = x.shape
    kern = functools.partial(_odd_kernel, tiles_per_seq=seq // TM)
    halo_blocks = TM // CONV_HALO
    return pl.pallas_call(
        kern,
        out_shape=jax.ShapeDtypeStruct((n, d), F32),
        grid=(n // TM,),
        in_specs=[
            pl.BlockSpec((TM, d), lambda i: (i, 0)),
            pl.BlockSpec((CONV_HALO, d), lambda i: (jnp.maximum(i * halo_blocks - 1, 0), 0)),
            _const_spec(w_in.shape),
            _const_spec((CONV_WIDTH, d)),
            _const_spec((1, d)),
            _const_spec(w_out.shape),
            _const_spec((1, d)),
            _const_spec((1, d)),
        ],
        out_specs=pl.BlockSpec((TM, d), lambda i: (i, 0)),
        scratch_shapes=[pltpu.VMEM((CONV_HALO + TM, d), F32)],
        compiler_params=_params(),
        name="odd_mixer",
    )(x, x, w_in.astype(BF16), conv_w.T, conv_b.reshape(1, -1), w_out.astype(BF16),
      g.reshape(1, -1), b.reshape(1, -1))


def _kv_kernel(mem_ref, wk_ref, wv_ref, k_ref, v_ref):
    m = mem_ref[...].astype(BF16)
    k_ref[...] = _dot(m, wk_ref[...]).astype(BF16)
    v_ref[...] = _dot(m, wv_ref[...]).astype(BF16)


def _memory_kv(mem2d, wk, wv):
    nl, d, _ = wk.shape
    rows = mem2d.shape[0]
    out = jax.ShapeDtypeStruct((nl, rows, d), BF16)
    wspec = pl.BlockSpec((None, d, d), lambda l: (l, 0, 0))
    ospec = pl.BlockSpec((None, rows, d), lambda l: (l, 0, 0))
    return pl.pallas_call(
        _kv_kernel,
        out_shape=(out, out),
        grid=(nl,),
        in_specs=[_const_spec((rows, d)), wspec, wspec],
        out_specs=(ospec, ospec),
        compiler_params=_params(),
        name="memory_kv",
    )(mem2d, wk.astype(BF16), wv.astype(BF16))


def _attn_kernel(x_ref, k_ref, v_ref, wq_ref, wo_ref, g_ref, b_ref, o_ref, o_scr):
    d = x_ref.shape[1]
    hd_dim = d // N_XHEADS
    x = x_ref[...]
    q = _dot(x.astype(BF16), wq_ref[...]) * (1.0 / math.sqrt(hd_dim))
    for hd in range(N_XHEADS):
        hs = slice(hd * hd_dim, (hd + 1) * hd_dim)
        s = lax.dot_general(q[:, hs].astype(BF16), k_ref[:, hs], _NT, preferred_element_type=F32)
        p = jnp.exp(s - jnp.max(s, axis=-1, keepdims=True))
        p = p / jnp.sum(p, axis=-1, keepdims=True)
        o_scr[:, hs] = _dot(p.astype(BF16), v_ref[:, hs]).astype(BF16)
    xa = _dot(o_scr[...], wo_ref[...])
    o_ref[...] = _layer_norm(ALPHA * x + xa, g_ref[...], b_ref[...])


def _cross_attn(x, seq, k, v, wq, wo, g, b):
    n, d = x.shape
    m = k.shape[1]
    tiles_per_seq = seq // TM
    kvspec = pl.BlockSpec((None, m, d), lambda i: (i // tiles_per_seq, 0, 0))
    return pl.pallas_call(
        _attn_kernel,
        out_shape=jax.ShapeDtypeStruct((n, d), F32),
        grid=(n // TM,),
        in_specs=[
            pl.BlockSpec((TM, d), lambda i: (i, 0)),
            kvspec, kvspec,
            _const_spec((d, d)), _const_spec((d, d)),
            _const_spec((1, d)), _const_spec((1, d)),
        ],
        out_specs=pl.BlockSpec((TM, d), lambda i: (i, 0)),
        scratch_shapes=[pltpu.VMEM((TM, d), BF16)],
        compiler_params=_params(),
        name="cross_attn",
    )(x, k, v, wq.astype(BF16), wo.astype(BF16), g.reshape(1, -1), b.reshape(1, -1))


def _router_kernel(x_ref, wcat_ref, bias_ref, ri_ref, rg_ref, cnt_ref, carry_scr):
    tm = x_ref.shape[0]
    step = pl.program_id(0)

    @pl.when(step == 0)
    def _():
        carry_scr[...] = jnp.zeros_like(carry_scr)

    x = x_ref[...]
    xh = x.astype(BF16)
    xl = (x - xh.astype(F32)).astype(BF16)
    r1 = _dot(xh, wcat_ref[...])
    r2 = _dot(xl, wcat_ref[:, :ROUTE_COLS])
    logits = r1[:, :ROUTE_COLS] + r1[:, ROUTE_COLS:] + r2 + bias_ref[...]

    lane = lax.broadcasted_iota(I32, (tm, ROUTE_COLS), 1).astype(F32)
    neg = -jnp.inf

    def first_argmax(vals):
        mx = jnp.max(vals, axis=-1, keepdims=True)
        idx = jnp.min(jnp.where(vals == mx, lane, float(ROUTE_COLS)), axis=-1, keepdims=True)
        return mx, idx

    gl = jnp.where(lane < N_GROUPS, logits, neg)
    gmax, g_sel = first_argmax(gl)
    gate_g = 1.0 / jnp.sum(jnp.exp(gl - gmax), axis=-1, keepdims=True)

    lo = N_GROUPS + g_sel * EXPERTS_PER_GROUP
    el = jnp.where((lane >= lo) & (lane < lo + EXPERTS_PER_GROUP), logits, neg)
    m1, i1 = first_argmax(el)
    m2, i2 = first_argmax(jnp.where(lane == i1, neg, el))
    e21 = jnp.exp(m2 - m1)
    w1 = 1.0 / (1.0 + e21)
    w2 = e21 / (1.0 + e21)

    oh1 = lane == i1
    oh2 = lane == i2
    oh = (oh1 | oh2).astype(BF16)
    r = lax.broadcasted_iota(I32, (tm, tm), 0)
    c = lax.broadcasted_iota(I32, (tm, tm), 1)
    before = _dot((r > c).astype(BF16), oh) + carry_scr[...]
    rank1 = jnp.sum(jnp.where(oh1, before, 0.0), axis=-1, keepdims=True)
    rank2 = jnp.sum(jnp.where(oh2, before, 0.0), axis=-1, keepdims=True)
    carry_scr[...] += jnp.sum(oh.astype(F32), axis=0, keepdims=True)

    l4 = lax.broadcasted_iota(I32, ri_ref.shape, 1)
    ri_ref[...] = jnp.where(l4 == 0, i1 - N_GROUPS,
                            jnp.where(l4 == 1, i2 - N_GROUPS,
                                      jnp.where(l4 == 2, rank1, rank2))).astype(I32)
    l2 = lax.broadcasted_iota(I32, rg_ref.shape, 1)
    rg_ref[...] = jnp.where(l2 == 0, gate_g * w1, gate_g * w2)
    cnt_ref[...] = carry_scr[...].astype(I32)


def _router(x, wr_g, br_g, wr_e, br_e):
    n, d = x.shape
    w = jnp.concatenate([wr_g, jnp.transpose(wr_e, (1, 0, 2)).reshape(d, N_EXPERTS)], axis=1)
    w = jnp.pad(w, ((0, 0), (0, ROUTE_COLS - w.shape[1])))
    w_hi = w.astype(BF16)
    w_lo = (w - w_hi.astype(F32)).astype(BF16)
    bias = jnp.pad(jnp.concatenate([br_g, br_e.reshape(-1)]), (0, ROUTE_COLS - N_GROUPS - N_EXPERTS))
    ri, rg, cnt = pl.pallas_call(
        _router_kernel,
        out_shape=(jax.ShapeDtypeStruct((n, 4), I32), jax.ShapeDtypeStruct((n, TOP_K), F32),
                   jax.ShapeDtypeStruct((1, ROUTE_COLS), I32)),
        grid=(n // TM,),
        in_specs=[pl.BlockSpec((TM, d), lambda i: (i, 0)),
                  _const_spec((d, 2 * ROUTE_COLS)), _const_spec((1, ROUTE_COLS))],
        out_specs=(pl.BlockSpec((TM, 4), lambda i: (i, 0)), pl.BlockSpec((TM, TOP_K), lambda i: (i, 0)),
                   _const_spec((1, ROUTE_COLS))),
        scratch_shapes=[pltpu.VMEM((1, ROUTE_COLS), F32)],
        compiler_params=_params(),
        name="router",
    )(x, jnp.concatenate([w_hi, w_lo], axis=1), bias.reshape(1, -1))
    return ri, rg, cnt[0, N_GROUPS:N_GROUPS + N_EXPERTS]


def _sc_gather(table, idx):
    b = idx.shape[0]
    d = table.shape[1]
    per_worker = b // SC_WORKERS
    n_chunks = per_worker // SC_CHUNK
    assert per_worker * SC_WORKERS == b and n_chunks * SC_CHUNK == per_worker
    mesh = plsc.VectorSubcoreMesh(core_axis_name="c", subcore_axis_name="s")

    @functools.partial(
        pl.kernel, mesh=mesh,
        out_type=jax.ShapeDtypeStruct((b, d), table.dtype),
        scratch_types=[pltpu.VMEM((per_worker,), I32),
                       pltpu.VMEM((SC_CHUNK, d), table.dtype),
                       pltpu.SemaphoreType.DMA],
        name="sc_gather",
    )
    def gather(table_hbm, idx_hbm, out_hbm, idx_v, rows_v, sem):
        wid = lax.axis_index("s") * SC_CORES + lax.axis_index("c")
        base = wid * per_worker
        pltpu.sync_copy(idx_hbm.at[pl.ds(base, per_worker)], idx_v)

        @pl.loop(0, n_chunks)
        def _(c):
            off = pl.multiple_of(c * SC_CHUNK, SC_CHUNK)
            pltpu.async_copy(table_hbm.at[idx_v.at[pl.ds(off, SC_CHUNK)]], rows_v, sem).wait()
            pltpu.sync_copy(rows_v, out_hbm.at[pl.ds(base + off, SC_CHUNK)])

    return gather(table, idx)


def _expert_kernel(be_ref, nb_ref, xs_ref, w1_ref, w3_ref, w2_ref, y_ref, w1_scr, w3_scr, w2_scr):
    b = pl.program_id(0)
    used = b < nb_ref[0]

    @pl.when(used & ((b == 0) | (be_ref[b] != be_ref[jnp.maximum(b - 1, 0)])))
    def _():
        w1_scr[...] = w1_ref[...].astype(BF16)
        w3_scr[...] = w3_ref[...].astype(BF16)
        w2_scr[...] = w2_ref[...].astype(BF16)

    @pl.when(used)
    def _():
        xb = xs_ref[...].astype(BF16)
        h1 = _dot(xb, w1_scr[...])
        h3 = _dot(xb, w3_scr[...])
        hid = h1 * (1.0 / (1.0 + jnp.exp(-h1))) * h3
        y_ref[...] = _dot(hid.astype(BF16), w2_scr[...])

    @pl.when(jnp.logical_not(used))
    def _():
        y_ref[...] = jnp.zeros_like(y_ref)


def _expert_mlp(xs, block_expert, n_used, layer, w1, w3, w2, n_blocks):
    d, de = w1.shape[2], w1.shape[3]

    def row_map(b, be, nb):
        return (jnp.minimum(b, nb[0] - 1), 0)

    def w_map(b, be, nb):
        return (layer, be[jnp.minimum(b, nb[0] - 1)], 0, 0)

    return pl.pallas_call(
        _expert_kernel,
        out_shape=jax.ShapeDtypeStruct((n_blocks * EXPERT_BLOCK, d), F32),
        grid_spec=pltpu.PrefetchScalarGridSpec(
            num_scalar_prefetch=2,
            grid=(n_blocks,),
            in_specs=[pl.BlockSpec((EXPERT_BLOCK, d), row_map),
                      pl.BlockSpec((None, None, d, de), w_map),
                      pl.BlockSpec((None, None, d, de), w_map),
                      pl.BlockSpec((None, None, de, d), w_map)],
            out_specs=pl.BlockSpec((EXPERT_BLOCK, d), lambda b, be, nb: (b, 0)),
            scratch_shapes=[pltpu.VMEM((d, de), BF16), pltpu.VMEM((d, de), BF16), pltpu.VMEM((de, d), BF16)],
        ),
        compiler_params=_params(),
        name="expert_mlp",
    )(block_expert, n_used, xs, w1, w3, w2)


def _combine_kernel(x_ref, y0_ref, y1_ref, gate_ref, g_ref, b_ref, o_ref):
    gates = gate_ref[...]
    ff = gates[:, 0:1] * y0_ref[...] + gates[:, 1:2] * y1_ref[...]
    o_ref[...] = _layer_norm(ALPHA * x_ref[...] + ff, g_ref[...], b_ref[...])


def _combine(x, yg, gates, g, b):
    n, d = x.shape
    tiles = n // TM
    return pl.pallas_call(
        _combine_kernel,
        out_shape=jax.ShapeDtypeStruct((n, d), F32),
        grid=(tiles,),
        in_specs=[pl.BlockSpec((TM, d), lambda i: (i, 0)),
                  pl.BlockSpec((TM, d), lambda i: (i, 0)),
                  pl.BlockSpec((TM, d), lambda i: (i + tiles, 0)),
                  pl.BlockSpec((TM, TOP_K), lambda i: (i, 0)),
                  _const_spec((1, d)), _const_spec((1, d))],
        out_specs=pl.BlockSpec((TM, d), lambda i: (i, 0)),
        compiler_params=_params(),
        name="combine",
    )(x, yg, yg, gates, g.reshape(1, -1), b.reshape(1, -1))


def _moe(x, layer, wr_g, br_g, wr_e, br_e, w1, w3, w2, g, b):
    n, d = x.shape
    n_blocks = (n * TOP_K + N_EXPERTS * (EXPERT_BLOCK - 1) + EXPERT_BLOCK - 1) // EXPERT_BLOCK
    ri, gates, counts = _router(x, wr_g, br_g, wr_e, br_e)

    blocks_e = (counts + EXPERT_BLOCK - 1) // EXPERT_BLOCK
    blocks_end = jnp.cumsum(blocks_e)
    run_start = (blocks_end - blocks_e) * EXPERT_BLOCK
    n_used = blocks_end[-1:].astype(I32)
    block_ids = jnp.arange(n_blocks, dtype=I32)
    block_expert = jnp.minimum(jnp.sum(blocks_end[None, :] <= block_ids[:, None], axis=1),
                               N_EXPERTS - 1).astype(I32)
    expert_ids = jnp.arange(N_EXPERTS, dtype=I32)
    start_of = jnp.sum(jnp.where(ri[:, :TOP_K, None] == expert_ids, run_start, 0), axis=-1)
    dest = (start_of + ri[:, TOP_K:]).astype(I32)
    tokens = jnp.broadcast_to(jnp.arange(n, dtype=I32)[:, None], dest.shape)
    row_token = jnp.zeros((n_blocks * EXPERT_BLOCK,), I32).at[dest.reshape(-1)].set(tokens.reshape(-1))

    xs = _sc_gather(x, row_token)
    y = _expert_mlp(xs, block_expert, n_used, layer, w1, w3, w2, n_blocks)
    yg = _sc_gather(y, dest.T.reshape(-1))
    return _combine(x, yg, gates, g, b)


def kernel(x, mem, w_in_even, w_pool, pool_scale, ln_v_g, ln_v_b, w_spatial, b_spatial, w_out_even,
           w_in_odd, conv_w, conv_b, w_out_odd, wq_x, wk_x, wv_x, wo_x, ln_g, ln_b, wr_group,
           br_group, wr_expert, br_expert, w1, w3, w2):
    bsz, seq, d = x.shape
    assert seq % TM == 0 and d % LANES == 0
    mlen = mem.shape[1]
    k_all, v_all = _memory_kv(mem.reshape(bsz * mlen, d), wk_x, wv_x)
    k_all = k_all.reshape(DEPTH, bsz, mlen, d)
    v_all = v_all.reshape(DEPTH, bsz, mlen, d)
    h = x.reshape(bsz * seq, d)
    for l in range(DEPTH):
        i = l // 2
        if l % 2 == 0:
            h = _even_mixer(h, seq, w_in_even[i], w_pool[i], pool_scale[i], ln_v_g[i], ln_v_b[i],
                            w_spatial[i], b_spatial[i], w_out_even[i], ln_g[l, 0], ln_b[l, 0])
        else:
            h = _odd_mixer(h, seq, w_in_odd[i], conv_w[i], conv_b[i], w_out_odd[i],
                           ln_g[l, 0], ln_b[l, 0])
        h = _cross_attn(h, seq, k_all[l], v_all[l], wq_x[l], wo_x[l], ln_g[l, 1], ln_b[l, 1])
        h = _moe(h, l, wr_group[l], br_group[l], wr_expert[l], br_expert[l], w1, w3, w2,
                 ln_g[l, 2], ln_b[l, 2])
    return h.reshape(bsz, seq, d)
```

```python
import functools
import math

import jax
import jax.numpy as jnp
from jax import lax
from jax.experimental import pallas as pl
from jax.experimental.pallas import tpu as pltpu
from jax.experimental.pallas import tpu_sc as plsc

F32 = jnp.float32
BF16 = jnp.bfloat16
I32 = jnp.int32

POOL_WINDOWS = (2, 4, 8, 16)
N_SG_HEADS = 4
CHUNK = 128
CONV_WIDTH = 3
N_XHEADS = 4
N_GROUPS = 4
EXPERTS_PER_GROUP = 8
N_EXPERTS = N_GROUPS * EXPERTS_PER_GROUP
TOP_K = 2
DEPTH = 4
ALPHA = (2.0 * DEPTH) ** 0.25
LN_EPS = 1e-5

LANES = 128
SC_CORES = 2
SC_WORKERS = 32
SC_CHUNK = 32
TM = 512
POOL_HALO = 16
CONV_HALO = 8
EXPERT_BLOCK = 256
ROUTE_COLS = 128
VMEM_LIMIT = 48 * 1024 * 1024

_NT = (((1,), (1,)), ((), ()))


def _dot(a, b):
    return jnp.dot(a, b, preferred_element_type=F32)


def _layer_norm(y, g, b):
    mu = jnp.mean(y, axis=-1, keepdims=True)
    yc = y - mu
    var = jnp.mean(yc * yc, axis=-1, keepdims=True)
    return yc * lax.rsqrt(var + LN_EPS) * g + b


def _gelu_tanh(x):
    c = math.sqrt(2.0 / math.pi)
    return 0.5 * x * (1.0 + jnp.tanh(c * (x + 0.044715 * (x * x * x))))


def _const_spec(shape):
    nd = len(shape)
    return pl.BlockSpec(shape, lambda i: (0,) * nd)


def _params():
    return pltpu.CompilerParams(dimension_semantics=("arbitrary",), vmem_limit_bytes=VMEM_LIMIT)


def _even_kernel(x_ref, xh_ref, win_ref, wpool_ref, pscale_ref, lvg_ref, lvb_ref, ws_ref, bst_ref,
                 wout_ref, g_ref, b_ref, o_ref, a_scr, cat_scr, *, tiles_per_seq):
    tm = x_ref.shape[0]
    d_pool = a_scr.shape[1]
    d_sg = lvg_ref.shape[1]
    pgd = d_pool // len(POOL_WINDOWS)
    hd_dim = d_sg // N_SG_HEADS
    seq_tile = pl.program_id(0) % tiles_per_seq

    x = x_ref[...]
    h = _dot(x.astype(BF16), win_ref[...])

    ah = _dot(xh_ref[...].astype(BF16), win_ref[:, :d_pool])
    a_scr[0:POOL_HALO, :] = jnp.where(seq_tile == 0, 0.0, ah)
    a_scr[POOL_HALO:POOL_HALO + tm, :] = h[:, :d_pool]
    pos = seq_tile * tm + lax.broadcasted_iota(I32, (tm, 1), 0)
    for g, w in enumerate(POOL_WINDOWS):
        cs = slice(g * pgd, (g + 1) * pgd)
        tok = a_scr[POOL_HALO:POOL_HALO + tm, cs]
        acc = tok
        for j in range(1, w):
            acc = acc + a_scr[POOL_HALO - j:POOL_HALO - j + tm, cs]
        cnt = jnp.minimum(pos + 1, w).astype(F32)
        d = acc / cnt - tok
        yg = _dot(d.astype(BF16), wpool_ref[g])
        cat_scr[:, cs] = (yg * pscale_ref[:, cs]).astype(BF16)

    z = _gelu_tanh(h[:, d_pool:])
    u = z[:, :d_sg]
    v = _layer_norm(z[:, d_sg:], lvg_ref[...], lvb_ref[...]).astype(BF16)
    row = lax.broadcasted_iota(I32, (CHUNK, CHUNK), 0)
    col = lax.broadcasted_iota(I32, (CHUNK, CHUNK), 1)
    for hd in range(N_SG_HEADS):
        hs = slice(hd * hd_dim, (hd + 1) * hd_dim)
        wsm = jnp.where(row >= col, ws_ref[hd], 0.0).astype(BF16)
        bcol = bst_ref[:, hd:hd + 1]
        for ck in range(tm // CHUNK):
            rs = slice(ck * CHUNK, (ck + 1) * CHUNK)
            sv = _dot(wsm, v[rs, hs]) + bcol
            cat_scr[rs, d_pool + hd * hd_dim:d_pool + (hd + 1) * hd_dim] = (u[rs, hs] * sv).astype(BF16)

    mix = _dot(cat_scr[...], wout_ref[...])
    o_ref[...] = _layer_norm(ALPHA * x + mix, g_ref[...], b_ref[...])


def _even_mixer(x, seq, w_in, w_pool, pool_scale, ln_v_g, ln_v_b, w_spatial, b_spatial, w_out, g, b):
    n, d = x.shape
    d_in = w_in.shape[1]
    d_pool = pool_scale.shape[0]
    d_sg = ln_v_g.shape[0]
    kern = functools.partial(_even_kernel, tiles_per_seq=seq // TM)
    halo_blocks = TM // POOL_HALO
    return pl.pallas_call(
        kern,
        out_shape=jax.ShapeDtypeStruct((n, d), F32),
        grid=(n // TM,),
        in_specs=[
            pl.BlockSpec((TM, d), lambda i: (i, 0)),
            pl.BlockSpec((POOL_HALO, d), lambda i: (jnp.maximum(i * halo_blocks - 1, 0), 0)),
            _const_spec((d, d_in)),
            _const_spec(w_pool.shape),
            _const_spec((1, d_pool)),
            _const_spec((1, d_sg)),
            _const_spec((1, d_sg)),
            _const_spec(w_spatial.shape),
            _const_spec((CHUNK, N_SG_HEADS)),
            _const_spec(w_out.shape),
            _const_spec((1, d)),
            _const_spec((1, d)),
        ],
        out_specs=pl.BlockSpec((TM, d), lambda i: (i, 0)),
        scratch_shapes=[pltpu.VMEM((POOL_HALO + TM, d_pool), F32), pltpu.VMEM((TM, d_pool + d_sg), BF16)],
        compiler_params=_params(),
        name="even_mixer",
    )(x, x, w_in.astype(BF16), w_pool.astype(BF16), pool_scale.reshape(1, -1), ln_v_g.reshape(1, -1),
      ln_v_b.reshape(1, -1), w_spatial, b_spatial.T, w_out.astype(BF16), g.reshape(1, -1), b.reshape(1, -1))


def _odd_kernel(x_ref, xh_ref, win_ref, cwt_ref, cb_ref, wout_ref, g_ref, b_ref, o_ref, zc_scr,
                *, tiles_per_seq):
    tm, d = x_ref.shape
    seq_tile = pl.program_id(0) % tiles_per_seq

    x = x_ref[...]
    h = _dot(x.astype(BF16), win_ref[...])
    hh = _dot(xh_ref[...].astype(BF16), win_ref[:, d:])
    zc_scr[0:CONV_HALO, :] = jnp.where(seq_tile == 0, 0.0, hh[:, :d] * hh[:, d:])
    zc_scr[CONV_HALO:CONV_HALO + tm, :] = h[:, d:2 * d] * h[:, 2 * d:]
    conv = cb_ref[...]
    for j in range(CONV_WIDTH):
        off = CONV_HALO - (CONV_WIDTH - 1) + j
        conv = conv + zc_scr[off:off + tm, :] * cwt_ref[j:j + 1, :]
    y = _dot((h[:, :d] * conv).astype(BF16), wout_ref[...])
    o_ref[...] = _layer_norm(ALPHA * x + y, g_ref[...], b_ref[...])


def _odd_mixer(x, seq, w_in, conv_w, conv_b, w_out, g, b):
    n, d = x.shape
    kern = functools.partial(_odd_kernel, tiles_per_seq=seq // TM)
    halo_blocks = TM // CONV_HALO
    return pl.pallas_call(
        kern,
        out_shape=jax.ShapeDtypeStruct((n, d), F32),
        grid=(n // TM,),
        in_specs=[
            pl.BlockSpec((TM, d), lambda i: (i, 0)),
            pl.BlockSpec((CONV_HALO, d), lambda i: (jnp.maximum(i * halo_blocks - 1, 0), 0)),
            _const_spec(w_in.shape),
            _const_spec((CONV_WIDTH, d)),
            _const_spec((1, d)),
            _const_spec(w_out.shape),
            _const_spec((1, d)),
            _const_spec((1, d)),
        ],
        out_specs=pl.BlockSpec((TM, d), lambda i: (i, 0)),
        scratch_shapes=[pltpu.VMEM((CONV_HALO + TM, d), F32)],
        compiler_params=_params(),
        name="odd_mixer",
    )(x, x, w_in.astype(BF16), conv_w.T, conv_b.reshape(1, -1), w_out.astype(BF16),
      g.reshape(1, -1), b.reshape(1, -1))


def _kv_kernel(mem_ref, wk_ref, wv_ref, k_ref, v_ref):
    m = mem_ref[...].astype(BF16)
    k_ref[...] = _dot(m, wk_ref[...]).astype(BF16)
    v_ref[...] = _dot(m, wv_ref[...]).astype(BF16)


def _memory_kv(mem2d, wk, wv):
    nl, d, _ = wk.shape
    rows = mem2d.shape[0]
    out = jax.ShapeDtypeStruct((nl, rows, d), BF16)
    wspec = pl.BlockSpec((None, d, d), lambda l: (l, 0, 0))
    ospec = pl.BlockSpec((None, rows, d), lambda l: (l, 0, 0))
    return pl.pallas_call(
        _kv_kernel,
        out_shape=(out, out),
        grid=(nl,),
        in_specs=[_const_spec((rows, d)), wspec, wspec],
        out_specs=(ospec, ospec),
        compiler_params=_params(),
        name="memory_kv",
    )(mem2d, wk.astype(BF16), wv.astype(BF16))


def _attn_kernel(x_ref, k_ref, v_ref, wq_ref, wo_ref, g_ref, b_ref, o_ref, o_scr):
    d = x_ref.shape[1]
    hd_dim = d // N_XHEADS
    x = x_ref[...]
    q = _dot(x.astype(BF16), wq_ref[...]) * (1.0 / math.sqrt(hd_dim))
    for hd in range(N_XHEADS):
        hs = slice(hd * hd_dim, (hd + 1) * hd_dim)
        s = lax.dot_general(q[:, hs].astype(BF16), k_ref[:, hs], _NT, preferred_element_type=F32)
        p = jnp.exp(s - jnp.max(s, axis=-1, keepdims=True))
        p = p / jnp.sum(p, axis=-1, keepdims=True)
        o_scr[:, hs] = _dot(p.astype(BF16), v_ref[:, hs]).astype(BF16)
    xa = _dot(o_scr[...], wo_ref[...])
    o_ref[...] = _layer_norm(ALPHA * x + xa, g_ref[...], b_ref[...])


def _cross_attn(x, seq, k, v, wq, wo, g, b):
    n, d = x.shape
    m = k.shape[1]
    tiles_per_seq = seq // TM
    kvspec = pl.BlockSpec((None, m, d), lambda i: (i // tiles_per_seq, 0, 0))
    return pl.pallas_call(
        _attn_kernel,
        out_shape=jax.ShapeDtypeStruct((n, d), F32),
        grid=(n // TM,),
        in_specs=[
            pl.BlockSpec((TM, d), lambda i: (i, 0)),
            kvspec, kvspec,
            _const_spec((d, d)), _const_spec((d, d)),
            _const_spec((1, d)), _const_spec((1, d)),
        ],
        out_specs=pl.BlockSpec((TM, d), lambda i: (i, 0)),
        scratch_shapes=[pltpu.VMEM((TM, d), BF16)],
        compiler_params=_params(),
        name="cross_attn",
    )(x, k, v, wq.astype(BF16), wo.astype(BF16), g.reshape(1, -1), b.reshape(1, -1))


def _router_kernel(x_ref, wcat_ref, bias_ref, ri_ref, rg_ref, cnt_ref, carry_scr):
    tm = x_ref.shape[0]
    step = pl.program_id(0)

    @pl.when(step == 0)
    def _():
        carry_scr[...] = jnp.zeros_like(carry_scr)

    x = x_ref[...]
    xh = x.astype(BF16)
    xl = (x - xh.astype(F32)).astype(BF16)
    r1 = _dot(xh, wcat_ref[...])
    r2 = _dot(xl, wcat_ref[:, :ROUTE_COLS])
    logits = r1[:, :ROUTE_COLS] + r1[:, ROUTE_COLS:] + r2 + bias_ref[...]

    lane = lax.broadcasted_iota(I32, (tm, ROUTE_COLS), 1).astype(F32)
    neg = -jnp.inf

    def first_argmax(vals):
        mx = jnp.max(vals, axis=-1, keepdims=True)
        idx = jnp.min(jnp.where(vals == mx, lane, float(ROUTE_COLS)), axis=-1, keepdims=True)
        return mx, idx

    gl = jnp.where(lane < N_GROUPS, logits, neg)
    gmax, g_sel = first_argmax(gl)
    gate_g = 1.0 / jnp.sum(jnp.exp(gl - gmax), axis=-1, keepdims=True)

    lo = N_GROUPS + g_sel * EXPERTS_PER_GROUP
    el = jnp.where((lane >= lo) & (lane < lo + EXPERTS_PER_GROUP), logits, neg)
    m1, i1 = first_argmax(el)
    m2, i2 = first_argmax(jnp.where(lane == i1, neg, el))
    e21 = jnp.exp(m2 - m1)
    w1 = 1.0 / (1.0 + e21)
    w2 = e21 / (1.0 + e21)

    oh1 = lane == i1
    oh2 = lane == i2
    oh = (oh1 | oh2).astype(BF16)
    r = lax.broadcasted_iota(I32, (tm, tm), 0)
    c = lax.broadcasted_iota(I32, (tm, tm), 1)
    before = _dot((r > c).astype(BF16), oh) + carry_scr[...]
    rank1 = jnp.sum(jnp.where(oh1, before, 0.0), axis=-1, keepdims=True)
    rank2 = jnp.sum(jnp.where(oh2, before, 0.0), axis=-1, keepdims=True)
    carry_scr[...] += jnp.sum(oh.astype(F32), axis=0, keepdims=True)

    l4 = lax.broadcasted_iota(I32, ri_ref.shape, 1)
    ri_ref[...] = jnp.where(l4 == 0, i1 - N_GROUPS,
                            jnp.where(l4 == 1, i2 - N_GROUPS,
                                      jnp.where(l4 == 2, rank1, rank2))).astype(I32)
    l2 = lax.broadcasted_iota(I32, rg_ref.shape, 1)
    rg_ref[...] = jnp.where(l2 == 0, gate_g * w1, gate_g * w2)
    cnt_ref[...] = carry_scr[...].astype(I32)


def _router(x, wr_g, br_g, wr_e, br_e):
    n, d = x.shape
    w = jnp.concatenate([wr_g, jnp.transpose(wr_e, (1, 0, 2)).reshape(d, N_EXPERTS)], axis=1)
    w = jnp.pad(w, ((0, 0), (0, ROUTE_COLS - w.shape[1])))
    w_hi = w.astype(BF16)
    w_lo = (w - w_hi.astype(F32)).astype(BF16)
    bias = jnp.pad(jnp.concatenate([br_g, br_e.reshape(-1)]), (0, ROUTE_COLS - N_GROUPS - N_EXPERTS))
    ri, rg, cnt = pl.pallas_call(
        _router_kernel,
        out_shape=(jax.ShapeDtypeStruct((n, 4), I32), jax.ShapeDtypeStruct((n, TOP_K), F32),
                   jax.ShapeDtypeStruct((1, ROUTE_COLS), I32)),
        grid=(n // TM,),
        in_specs=[pl.BlockSpec((TM, d), lambda i: (i, 0)),
                  _const_spec((d, 2 * ROUTE_COLS)), _const_spec((1, ROUTE_COLS))],
        out_specs=(pl.BlockSpec((TM, 4), lambda i: (i, 0)), pl.BlockSpec((TM, TOP_K), lambda i: (i, 0)),
                   _const_spec((1, ROUTE_COLS))),
        scratch_shapes=[pltpu.VMEM((1, ROUTE_COLS), F32)],
        compiler_params=_params(),
        name="router",
    )(x, jnp.concatenate([w_hi, w_lo], axis=1), bias.reshape(1, -1))
    return ri, rg, cnt[0, N_GROUPS:N_GROUPS + N_EXPERTS]


def _sc_gather(table, idx):
    b = idx.shape[0]
    d = table.shape[1]
    per_worker = b // SC_WORKERS
    n_chunks = per_worker // SC_CHUNK
    assert per_worker * SC_WORKERS == b and n_chunks * SC_CHUNK == per_worker
    mesh = plsc.VectorSubcoreMesh(core_axis_name="c", subcore_axis_name="s")

    @functools.partial(
        pl.kernel, mesh=mesh,
        out_type=jax.ShapeDtypeStruct((b, d), table.dtype),
        scratch_types=[pltpu.VMEM((per_worker,), I32),
                       pltpu.VMEM((SC_CHUNK, d), table.dtype),
                       pltpu.SemaphoreType.DMA],
        name="sc_gather",
    )
    def gather(table_hbm, idx_hbm, out_hbm, idx_v, rows_v, sem):
        wid = lax.axis_index("s") * SC_CORES + lax.axis_index("c")
        base = wid * per_worker
        pltpu.sync_copy(idx_hbm.at[pl.ds(base, per_worker)], idx_v)

        @pl.loop(0, n_chunks)
        def _(c):
            off = pl.multiple_of(c * SC_CHUNK, SC_CHUNK)
            pltpu.async_copy(table_hbm.at[idx_v.at[pl.ds(off, SC_CHUNK)]], rows_v, sem).wait()
            pltpu.sync_copy(rows_v, out_hbm.at[pl.ds(base + off, SC_CHUNK)])

    return gather(table, idx)


def _expert_kernel(be_ref, nb_ref, xs_ref, w1_ref, w3_ref, w2_ref, y_ref, w1_scr, w3_scr, w2_scr):
    b = pl.program_id(0)
    used = b < nb_ref[0]

    @pl.when(used & ((b == 0) | (be_ref[b] != be_ref[jnp.maximum(b - 1, 0)])))
    def _():
        w1_scr[...] = w1_ref[...].astype(BF16)
        w3_scr[...] = w3_ref[...].astype(BF16)
        w2_scr[...] = w2_ref[...].astype(BF16)

    @pl.when(used)
    def _():
        xb = xs_ref[...].astype(BF16)
        h1 = _dot(xb, w1_scr[...])
        h3 = _dot(xb, w3_scr[...])
        hid = h1 * (1.0 / (1.0 + jnp.exp(-h1))) * h3
        y_ref[...] = _dot(hid.astype(BF16), w2_scr[...])

    @pl.when(jnp.logical_not(used))
    def _():
        y_ref[...] = jnp.zeros_like(y_ref)


def _expert_mlp(xs, block_expert, n_used, layer, w1, w3, w2, n_blocks):
    d, de = w1.shape[2], w1.shape[3]

    def row_map(b, be, nb):
        return (jnp.minimum(b, nb[0] - 1), 0)

    def w_map(b, be, nb):
        return (layer, be[jnp.minimum(b, nb[0] - 1)], 0, 0)

    return pl.pallas_call(
        _expert_kernel,
        out_shape=jax.ShapeDtypeStruct((n_blocks * EXPERT_BLOCK, d), F32),
        grid_spec=pltpu.PrefetchScalarGridSpec(
            num_scalar_prefetch=2,
            grid=(n_blocks,),
            in_specs=[pl.BlockSpec((EXPERT_BLOCK, d), row_map),
                      pl.BlockSpec((None, None, d, de), w_map),
                      pl.BlockSpec((None, None, d, de), w_map),
                      pl.BlockSpec((None, None, de, d), w_map)],
            out_specs=pl.BlockSpec((EXPERT_BLOCK, d), lambda b, be, nb: (b, 0)),
            scratch_shapes=[pltpu.VMEM((d, de), BF16), pltpu.VMEM((d, de), BF16), pltpu.VMEM((de, d), BF16)],
        ),
        compiler_params=_params(),
        name="expert_mlp",
    )(block_expert, n_used, xs, w1, w3, w2)


def _combine_kernel(x_ref, y0_ref, y1_ref, gate_ref, g_ref, b_ref, o_ref):
    gates = gate_ref[...]
    ff = gates[:, 0:1] * y0_ref[...] + gates[:, 1:2] * y1_ref[...]
    o_ref[...] = _layer_norm(ALPHA * x_ref[...] + ff, g_ref[...], b_ref[...])


def _combine(x, yg, gates, g, b):
    n, d = x.shape
    tiles = n // TM
    return pl.pallas_call(
        _combine_kernel,
        out_shape=jax.ShapeDtypeStruct((n, d), F32),
        grid=(tiles,),
        in_specs=[pl.BlockSpec((TM, d), lambda i: (i, 0)),
                  pl.BlockSpec((TM, d), lambda i: (i, 0)),
                  pl.BlockSpec((TM, d), lambda i: (i + tiles, 0)),
                  pl.BlockSpec((TM, TOP_K), lambda i: (i, 0)),
                  _const_spec((1, d)), _const_spec((1, d))],
        out_specs=pl.BlockSpec((TM, d), lambda i: (i, 0)),
        compiler_params=_params(),
        name="combine",
    )(x, yg, yg, gates, g.reshape(1, -1), b.reshape(1, -1))


def _moe(x, layer, wr_g, br_g, wr_e, br_e, w1, w3, w2, g, b):
    n, d = x.shape
    n_blocks = (n * TOP_K + N_EXPERTS * (EXPERT_BLOCK - 1) + EXPERT_BLOCK - 1) // EXPERT_BLOCK
    ri, gates, counts = _router(x, wr_g, br_g, wr_e, br_e)

    blocks_e = (counts + EXPERT_BLOCK - 1) // EXPERT_BLOCK
    blocks_end = jnp.cumsum(blocks_e)
    run_start = (blocks_end - blocks_e) * EXPERT_BLOCK
    n_used = blocks_end[-1:].astype(I32)
    block_ids = jnp.arange(n_blocks, dtype=I32)
    block_expert = jnp.minimum(jnp.sum(blocks_end[None, :] <= block_ids[:, None], axis=1),
                               N_EXPERTS - 1).astype(I32)
    expert_ids = jnp.arange(N_EXPERTS, dtype=I32)
    start_of = jnp.sum(jnp.where(ri[:, :TOP_K, None] == expert_ids, run_start, 0), axis=-1)
    dest = (start_of + ri[:, TOP_K:]).astype(I32)
    rows = n_blocks * EXPERT_BLOCK
    tokens = jnp.broadcast_to(jnp.arange(n, dtype=I32)[:, None], dest.shape)
    row_token = (jnp.arange(rows, dtype=I32) % n).at[dest.reshape(-1)].set(
        tokens.reshape(-1), unique_indices=True, mode="promise_in_bounds")

    xs = _sc_gather(x, row_token)
    y = _expert_mlp(xs, block_expert, n_used, layer, w1, w3, w2, n_blocks)
    yg = _sc_gather(y, dest.T.reshape(-1))
    return _combine(x, yg, gates, g, b)


def kernel(x, mem, w_in_even, w_pool, pool_scale, ln_v_g, ln_v_b, w_spatial, b_spatial, w_out_even,
           w_in_odd, conv_w, conv_b, w_out_odd, wq_x, wk_x, wv_x, wo_x, ln_g, ln_b, wr_group,
           br_group, wr_expert, br_expert, w1, w3, w2):
    bsz, seq, d = x.shape
    assert seq % TM == 0 and d % LANES == 0
    mlen = mem.shape[1]
    k_all, v_all = _memory_kv(mem.reshape(bsz * mlen, d), wk_x, wv_x)
    k_all = k_all.reshape(DEPTH, bsz, mlen, d)
    v_all = v_all.reshape(DEPTH, bsz, mlen, d)
    h = x.reshape(bsz * seq, d)
    for l in range(DEPTH):
        i = l // 2
        if l % 2 == 0:
            h = _even_mixer(h, seq, w_in_even[i], w_pool[i], pool_scale[i], ln_v_g[i], ln_v_b[i],
                            w_spatial[i], b_spatial[i], w_out_even[i], ln_g[l, 0], ln_b[l, 0])
        else:
            h = _odd_mixer(h, seq, w_in_odd[i], conv_w[i], conv_b[i], w_out_odd[i],
                           ln_g[l, 0], ln_b[l, 0])
        h = _cross_attn(h, seq, k_all[l], v_all[l], wq_x[l], wo_x[l], ln_g[l, 1], ln_b[l, 1])
        h = _moe(h, l, wr_group[l], br_group[l], wr_expert[l], br_expert[l], w1, w3, w2,
                 ln_g[l, 2], ln_b[l, 2])
    return h.reshape(bsz, seq, d)
```

```python
import functools
import math

import jax
import jax.numpy as jnp
from jax import lax
from jax.experimental import pallas as pl
from jax.experimental.pallas import tpu as pltpu
from jax.experimental.pallas import tpu_sc as plsc

F32 = jnp.float32
BF16 = jnp.bfloat16
I32 = jnp.int32

POOL_WINDOWS = (2, 4, 8, 16)
N_SG_HEADS = 4
CHUNK = 128
CONV_WIDTH = 3
N_XHEADS = 4
N_GROUPS = 4
EXPERTS_PER_GROUP = 8
N_EXPERTS = N_GROUPS * EXPERTS_PER_GROUP
TOP_K = 2
DEPTH = 4
ALPHA = (2.0 * DEPTH) ** 0.25
LN_EPS = 1e-5

LANES = 128
SC_CORES = 2
SC_WORKERS = 32
SC_LANES = 16
SC_CHUNK = 32
TM = 512
POOL_HALO = 16
CONV_HALO = 8
EXPERT_BLOCK = 256
ROUTE_COLS = 128
VMEM_LIMIT = 48 * 1024 * 1024

_NT = (((1,), (1,)), ((), ()))


def _dot(a, b):
    return jnp.dot(a, b, preferred_element_type=F32)


def _layer_norm(y, g, b):
    mu = jnp.mean(y, axis=-1, keepdims=True)
    yc = y - mu
    var = jnp.mean(yc * yc, axis=-1, keepdims=True)
    return yc * lax.rsqrt(var + LN_EPS) * g + b


def _gelu_tanh(x):
    c = math.sqrt(2.0 / math.pi)
    return 0.5 * x * (1.0 + jnp.tanh(c * (x + 0.044715 * (x * x * x))))


def _const_spec(shape):
    nd = len(shape)
    return pl.BlockSpec(shape, lambda i: (0,) * nd)


def _params():
    return pltpu.CompilerParams(dimension_semantics=("arbitrary",), vmem_limit_bytes=VMEM_LIMIT)


def _even_kernel(x_ref, xh_ref, win_ref, wpool_ref, pscale_ref, lvg_ref, lvb_ref, ws_ref, bst_ref,
                 wout_ref, g_ref, b_ref, o_ref, a_scr, cat_scr, *, tiles_per_seq):
    tm = x_ref.shape[0]
    d_pool = a_scr.shape[1]
    d_sg = lvg_ref.shape[1]
    pgd = d_pool // len(POOL_WINDOWS)
    hd_dim = d_sg // N_SG_HEADS
    seq_tile = pl.program_id(0) % tiles_per_seq

    x = x_ref[...]
    h = _dot(x.astype(BF16), win_ref[...])

    ah = _dot(xh_ref[...].astype(BF16), win_ref[:, :d_pool])
    a_scr[0:POOL_HALO, :] = jnp.where(seq_tile == 0, 0.0, ah)
    a_scr[POOL_HALO:POOL_HALO + tm, :] = h[:, :d_pool]
    pos = seq_tile * tm + lax.broadcasted_iota(I32, (tm, 1), 0)
    for g, w in enumerate(POOL_WINDOWS):
        cs = slice(g * pgd, (g + 1) * pgd)
        tok = a_scr[POOL_HALO:POOL_HALO + tm, cs]
        acc = tok
        for j in range(1, w):
            acc = acc + a_scr[POOL_HALO - j:POOL_HALO - j + tm, cs]
        cnt = jnp.minimum(pos + 1, w).astype(F32)
        d = acc / cnt - tok
        yg = _dot(d.astype(BF16), wpool_ref[g])
        cat_scr[:, cs] = (yg * pscale_ref[:, cs]).astype(BF16)

    z = _gelu_tanh(h[:, d_pool:])
    u = z[:, :d_sg]
    v = _layer_norm(z[:, d_sg:], lvg_ref[...], lvb_ref[...]).astype(BF16)
    row = lax.broadcasted_iota(I32, (CHUNK, CHUNK), 0)
    col = lax.broadcasted_iota(I32, (CHUNK, CHUNK), 1)
    for hd in range(N_SG_HEADS):
        hs = slice(hd * hd_dim, (hd + 1) * hd_dim)
        wsm = jnp.where(row >= col, ws_ref[hd], 0.0).astype(BF16)
        bcol = bst_ref[:, hd:hd + 1]
        for ck in range(tm // CHUNK):
            rs = slice(ck * CHUNK, (ck + 1) * CHUNK)
            sv = _dot(wsm, v[rs, hs]) + bcol
            cat_scr[rs, d_pool + hd * hd_dim:d_pool + (hd + 1) * hd_dim] = (u[rs, hs] * sv).astype(BF16)

    mix = _dot(cat_scr[...], wout_ref[...])
    o_ref[...] = _layer_norm(ALPHA * x + mix, g_ref[...], b_ref[...])


def _even_mixer(x, seq, w_in, w_pool, pool_scale, ln_v_g, ln_v_b, w_spatial, b_spatial, w_out, g, b):
    n, d = x.shape
    d_in = w_in.shape[1]
    d_pool = pool_scale.shape[0]
    d_sg = ln_v_g.shape[0]
    kern = functools.partial(_even_kernel, tiles_per_seq=seq // TM)
    halo_blocks = TM // POOL_HALO
    return pl.pallas_call(
        kern,
        out_shape=jax.ShapeDtypeStruct((n, d), F32),
        grid=(n // TM,),
        in_specs=[
            pl.BlockSpec((TM, d), lambda i: (i, 0)),
            pl.BlockSpec((POOL_HALO, d), lambda i: (jnp.maximum(i * halo_blocks - 1, 0), 0)),
            _const_spec((d, d_in)),
            _const_spec(w_pool.shape),
            _const_spec((1, d_pool)),
            _const_spec((1, d_sg)),
            _const_spec((1, d_sg)),
            _const_spec(w_spatial.shape),
            _const_spec((CHUNK, N_SG_HEADS)),
            _const_spec(w_out.shape),
            _const_spec((1, d)),
            _const_spec((1, d)),
        ],
        out_specs=pl.BlockSpec((TM, d), lambda i: (i, 0)),
        scratch_shapes=[pltpu.VMEM((POOL_HALO + TM, d_pool), F32), pltpu.VMEM((TM, d_pool + d_sg), BF16)],
        compiler_params=_params(),
        name="even_mixer",
    )(x, x, w_in.astype(BF16), w_pool.astype(BF16), pool_scale.reshape(1, -1), ln_v_g.reshape(1, -1),
      ln_v_b.reshape(1, -1), w_spatial, b_spatial.T, w_out.astype(BF16), g.reshape(1, -1), b.reshape(1, -1))


def _odd_kernel(x_ref, xh_ref, win_ref, cwt_ref, cb_ref, wout_ref, g_ref, b_ref, o_ref, zc_scr,
                *, tiles_per_seq):
    tm, d = x_ref.shape
    seq_tile = pl.program_id(0) % tiles_per_seq

    x = x_ref[...]
    h = _dot(x.astype(BF16), win_ref[...])
    hh = _dot(xh_ref[...].astype(BF16), win_ref[:, d:])
    zc_scr[0:CONV_HALO, :] = jnp.where(seq_tile == 0, 0.0, hh[:, :d] * hh[:, d:])
    zc_scr[CONV_HALO:CONV_HALO + tm, :] = h[:, d:2 * d] * h[:, 2 * d:]
    conv = cb_ref[...]
    for j in range(CONV_WIDTH):
        off = CONV_HALO - (CONV_WIDTH - 1) + j
        conv = conv + zc_scr[off:off + tm, :] * cwt_ref[j:j + 1, :]
    y = _dot((h[:, :d] * conv).astype(BF16), wout_ref[...])
    o_ref[...] = _layer_norm(ALPHA * x + y, g_ref[...], b_ref[...])


def _odd_mixer(x, seq, w_in, conv_w, conv_b, w_out, g, b):
    n, d = x.shape
    kern = functools.partial(_odd_kernel, tiles_per_seq=seq // TM)
    halo_blocks = TM // CONV_HALO
    return pl.pallas_call(
        kern,
        out_shape=jax.ShapeDtypeStruct((n, d), F32),
        grid=(n // TM,),
        in_specs=[
            pl.BlockSpec((TM, d), lambda i: (i, 0)),
            pl.BlockSpec((CONV_HALO, d), lambda i: (jnp.maximum(i * halo_blocks - 1, 0), 0)),
            _const_spec(w_in.shape),
            _const_spec((CONV_WIDTH, d)),
            _const_spec((1, d)),
            _const_spec(w_out.shape),
            _const_spec((1, d)),
            _const_spec((1, d)),
        ],
        out_specs=pl.BlockSpec((TM, d), lambda i: (i, 0)),
        scratch_shapes=[pltpu.VMEM((CONV_HALO + TM, d), F32)],
        compiler_params=_params(),
        name="odd_mixer",
    )(x, x, w_in.astype(BF16), conv_w.T, conv_b.reshape(1, -1), w_out.astype(BF16),
      g.reshape(1, -1), b.reshape(1, -1))


def _kv_kernel(mem_ref, wk_ref, wv_ref, k_ref, v_ref):
    m = mem_ref[...].astype(BF16)
    k_ref[...] = _dot(m, wk_ref[...]).astype(BF16)
    v_ref[...] = _dot(m, wv_ref[...]).astype(BF16)


def _memory_kv(mem2d, wk, wv):
    nl, d, _ = wk.shape
    rows = mem2d.shape[0]
    out = jax.ShapeDtypeStruct((nl, rows, d), BF16)
    wspec = pl.BlockSpec((None, d, d), lambda l: (l, 0, 0))
    ospec = pl.BlockSpec((None, rows, d), lambda l: (l, 0, 0))
    return pl.pallas_call(
        _kv_kernel,
        out_shape=(out, out),
        grid=(nl,),
        in_specs=[_const_spec((rows, d)), wspec, wspec],
        out_specs=(ospec, ospec),
        compiler_params=_params(),
        name="memory_kv",
    )(mem2d, wk.astype(BF16), wv.astype(BF16))


def _attn_kernel(x_ref, k_ref, v_ref, wq_ref, wo_ref, g_ref, b_ref, o_ref, o_scr):
    d = x_ref.shape[1]
    hd_dim = d // N_XHEADS
    x = x_ref[...]
    q = _dot(x.astype(BF16), wq_ref[...]) * (1.0 / math.sqrt(hd_dim))
    for hd in range(N_XHEADS):
        hs = slice(hd * hd_dim, (hd + 1) * hd_dim)
        s = lax.dot_general(q[:, hs].astype(BF16), k_ref[:, hs], _NT, preferred_element_type=F32)
        p = jnp.exp(s - jnp.max(s, axis=-1, keepdims=True))
        p = p / jnp.sum(p, axis=-1, keepdims=True)
        o_scr[:, hs] = _dot(p.astype(BF16), v_ref[:, hs]).astype(BF16)
    xa = _dot(o_scr[...], wo_ref[...])
    o_ref[...] = _layer_norm(ALPHA * x + xa, g_ref[...], b_ref[...])


def _cross_attn(x, seq, k, v, wq, wo, g, b):
    n, d = x.shape
    m = k.shape[1]
    tiles_per_seq = seq // TM
    kvspec = pl.BlockSpec((None, m, d), lambda i: (i // tiles_per_seq, 0, 0))
    return pl.pallas_call(
        _attn_kernel,
        out_shape=jax.ShapeDtypeStruct((n, d), F32),
        grid=(n // TM,),
        in_specs=[
            pl.BlockSpec((TM, d), lambda i: (i, 0)),
            kvspec, kvspec,
            _const_spec((d, d)), _const_spec((d, d)),
            _const_spec((1, d)), _const_spec((1, d)),
        ],
        out_specs=pl.BlockSpec((TM, d), lambda i: (i, 0)),
        scratch_shapes=[pltpu.VMEM((TM, d), BF16)],
        compiler_params=_params(),
        name="cross_attn",
    )(x, k, v, wq.astype(BF16), wo.astype(BF16), g.reshape(1, -1), b.reshape(1, -1))


def _router_kernel(x_ref, wcat_ref, bias_ref, ri_ref, rg_ref, cnt_ref, carry_scr):
    tm = x_ref.shape[0]
    step = pl.program_id(0)

    @pl.when(step == 0)
    def _():
        carry_scr[...] = jnp.zeros_like(carry_scr)

    x = x_ref[...]
    xh = x.astype(BF16)
    xl = (x - xh.astype(F32)).astype(BF16)
    r1 = _dot(xh, wcat_ref[...])
    r2 = _dot(xl, wcat_ref[:, :ROUTE_COLS])
    logits = r1[:, :ROUTE_COLS] + r1[:, ROUTE_COLS:] + r2 + bias_ref[...]

    lane = lax.broadcasted_iota(I32, (tm, ROUTE_COLS), 1).astype(F32)
    neg = -jnp.inf

    def first_argmax(vals):
        mx = jnp.max(vals, axis=-1, keepdims=True)
        idx = jnp.min(jnp.where(vals == mx, lane, float(ROUTE_COLS)), axis=-1, keepdims=True)
        return mx, idx

    gl = jnp.where(lane < N_GROUPS, logits, neg)
    gmax, g_sel = first_argmax(gl)
    gate_g = 1.0 / jnp.sum(jnp.exp(gl - gmax), axis=-1, keepdims=True)

    lo = N_GROUPS + g_sel * EXPERTS_PER_GROUP
    el = jnp.where((lane >= lo) & (lane < lo + EXPERTS_PER_GROUP), logits, neg)
    m1, i1 = first_argmax(el)
    m2, i2 = first_argmax(jnp.where(lane == i1, neg, el))
    e21 = jnp.exp(m2 - m1)
    w1 = 1.0 / (1.0 + e21)
    w2 = e21 / (1.0 + e21)

    oh1 = lane == i1
    oh2 = lane == i2
    oh = (oh1 | oh2).astype(BF16)
    r = lax.broadcasted_iota(I32, (tm, tm), 0)
    c = lax.broadcasted_iota(I32, (tm, tm), 1)
    before = _dot((r > c).astype(BF16), oh) + carry_scr[...]
    rank1 = jnp.sum(jnp.where(oh1, before, 0.0), axis=-1, keepdims=True)
    rank2 = jnp.sum(jnp.where(oh2, before, 0.0), axis=-1, keepdims=True)
    carry_scr[...] += jnp.sum(oh.astype(F32), axis=0, keepdims=True)

    l4 = lax.broadcasted_iota(I32, ri_ref.shape, 1)
    ri_ref[...] = jnp.where(l4 == 0, i1 - N_GROUPS,
                            jnp.where(l4 == 1, i2 - N_GROUPS,
                                      jnp.where(l4 == 2, rank1, rank2))).astype(I32)
    l2 = lax.broadcasted_iota(I32, rg_ref.shape, 1)
    rg_ref[...] = jnp.where(l2 == 0, gate_g * w1, gate_g * w2)
    cnt_ref[...] = carry_scr[...].astype(I32)


def _router(x, wr_g, br_g, wr_e, br_e):
    n, d = x.shape
    w = jnp.concatenate([wr_g, jnp.transpose(wr_e, (1, 0, 2)).reshape(d, N_EXPERTS)], axis=1)
    w = jnp.pad(w, ((0, 0), (0, ROUTE_COLS - w.shape[1])))
    w_hi = w.astype(BF16)
    w_lo = (w - w_hi.astype(F32)).astype(BF16)
    bias = jnp.pad(jnp.concatenate([br_g, br_e.reshape(-1)]), (0, ROUTE_COLS - N_GROUPS - N_EXPERTS))
    ri, rg, cnt = pl.pallas_call(
        _router_kernel,
        out_shape=(jax.ShapeDtypeStruct((n, 4), I32), jax.ShapeDtypeStruct((n, TOP_K), F32),
                   jax.ShapeDtypeStruct((1, ROUTE_COLS), I32)),
        grid=(n // TM,),
        in_specs=[pl.BlockSpec((TM, d), lambda i: (i, 0)),
                  _const_spec((d, 2 * ROUTE_COLS)), _const_spec((1, ROUTE_COLS))],
        out_specs=(pl.BlockSpec((TM, 4), lambda i: (i, 0)), pl.BlockSpec((TM, TOP_K), lambda i: (i, 0)),
                   _const_spec((1, ROUTE_COLS))),
        scratch_shapes=[pltpu.VMEM((1, ROUTE_COLS), F32)],
        compiler_params=_params(),
        name="router",
    )(x, jnp.concatenate([w_hi, w_lo], axis=1), bias.reshape(1, -1))
    return ri, rg, cnt[0, N_GROUPS:N_GROUPS + N_EXPERTS]


def _sc_gather(table, idx):
    b = idx.shape[0]
    d = table.shape[1]
    per_worker = b // SC_WORKERS
    n_chunks = per_worker // SC_CHUNK
    assert per_worker * SC_WORKERS == b and n_chunks * SC_CHUNK == per_worker and n_chunks % 2 == 0
    mesh = plsc.VectorSubcoreMesh(core_axis_name="c", subcore_axis_name="s")

    @functools.partial(
        pl.kernel, mesh=mesh,
        out_type=jax.ShapeDtypeStruct((b, d), table.dtype),
        scratch_types=[pltpu.VMEM((per_worker,), I32),
                       pltpu.VMEM((2, SC_CHUNK, d), table.dtype),
                       pltpu.SemaphoreType.DMA((2,)),
                       pltpu.SemaphoreType.DMA((2,))],
        name="sc_gather",
    )
    def gather(table_hbm, idx_hbm, out_hbm, idx_v, rows_v, gsem, wsem):
        wid = lax.axis_index("s") * SC_CORES + lax.axis_index("c")
        base = wid * per_worker
        pltpu.sync_copy(idx_hbm.at[pl.ds(base, per_worker)], idx_v)

        def fetch(c, slot):
            off = pl.multiple_of(c * SC_CHUNK, SC_CHUNK)
            return pltpu.make_async_copy(table_hbm.at[idx_v.at[pl.ds(off, SC_CHUNK)]], rows_v.at[slot],
                                         gsem.at[slot])

        def put(c, slot):
            off = pl.multiple_of(c * SC_CHUNK, SC_CHUNK)
            return pltpu.make_async_copy(rows_v.at[slot], out_hbm.at[pl.ds(base + off, SC_CHUNK)],
                                         wsem.at[slot])

        fetch(0, 0).start()

        @pl.loop(0, n_chunks, step=2)
        def _(c0):
            for slot in range(2):
                c = c0 + slot

                @pl.when(c + 1 < n_chunks)
                def _():
                    @pl.when(c >= 1)
                    def _():
                        put(c - 1, 1 - slot).wait()
                    fetch(c + 1, 1 - slot).start()

                fetch(c, slot).wait()
                put(c, slot).start()

        put(n_chunks - 2, 0).wait()
        put(n_chunks - 1, 1).wait()

    return gather(table, idx)


def _sc_row_tokens(dest_flat, rows, n):
    a = dest_flat.shape[0]
    lanes = SC_LANES
    assert a % lanes == 0 and rows % lanes == 0
    mesh = plsc.VectorSubcoreMesh(core_axis_name="c", subcore_axis_name="s")

    @functools.partial(
        pl.kernel, mesh=mesh,
        out_type=jax.ShapeDtypeStruct((rows,), I32),
        scratch_types=[pltpu.VMEM((a,), I32), pltpu.VMEM((rows,), I32)],
        compiler_params=pltpu.CompilerParams(needs_layout_passes=False),
        name="sc_row_tokens",
    )
    def invert(dest_hbm, out_hbm, dest_v, map_v):
        @pl.when((lax.axis_index("s") == 0) & (lax.axis_index("c") == 0))
        def _():
            pltpu.sync_copy(dest_hbm, dest_v)
            lane = lax.iota(I32, lanes)

            @pl.loop(0, rows // lanes)
            def _(i):
                map_v[pl.ds(i * lanes, lanes)] = lax.rem(i * lanes + lane, n)

            @pl.loop(0, a // lanes)
            def _(i):
                plsc.store_scatter(map_v, [dest_v[pl.ds(i * lanes, lanes)]], (i * lanes + lane) // TOP_K)

            pltpu.sync_copy(map_v, out_hbm)

    return invert(dest_flat)


def _expert_kernel(be_ref, nb_ref, xs_ref, w1_ref, w3_ref, w2_ref, y_ref, w1_scr, w3_scr, w2_scr):
    b = pl.program_id(0)
    used = b < nb_ref[0]

    @pl.when(used & ((b == 0) | (be_ref[b] != be_ref[jnp.maximum(b - 1, 0)])))
    def _():
        w1_scr[...] = w1_ref[...].astype(BF16)
        w3_scr[...] = w3_ref[...].astype(BF16)
        w2_scr[...] = w2_ref[...].astype(BF16)

    @pl.when(used)
    def _():
        xb = xs_ref[...].astype(BF16)
        h1 = _dot(xb, w1_scr[...])
        h3 = _dot(xb, w3_scr[...])
        hid = h1 * (1.0 / (1.0 + jnp.exp(-h1))) * h3
        y_ref[...] = _dot(hid.astype(BF16), w2_scr[...])

    @pl.when(jnp.logical_not(used))
    def _():
        y_ref[...] = jnp.zeros_like(y_ref)


def _expert_mlp(xs, block_expert, n_used, layer, w1, w3, w2, n_blocks):
    d, de = w1.shape[2], w1.shape[3]

    def row_map(b, be, nb):
        return (jnp.minimum(b, nb[0] - 1), 0)

    def w_map(b, be, nb):
        return (layer, be[jnp.minimum(b, nb[0] - 1)], 0, 0)

    return pl.pallas_call(
        _expert_kernel,
        out_shape=jax.ShapeDtypeStruct((n_blocks * EXPERT_BLOCK, d), F32),
        grid_spec=pltpu.PrefetchScalarGridSpec(
            num_scalar_prefetch=2,
            grid=(n_blocks,),
            in_specs=[pl.BlockSpec((EXPERT_BLOCK, d), row_map),
                      pl.BlockSpec((None, None, d, de), w_map),
                      pl.BlockSpec((None, None, d, de), w_map),
                      pl.BlockSpec((None, None, de, d), w_map)],
            out_specs=pl.BlockSpec((EXPERT_BLOCK, d), lambda b, be, nb: (b, 0)),
            scratch_shapes=[pltpu.VMEM((d, de), BF16), pltpu.VMEM((d, de), BF16), pltpu.VMEM((de, d), BF16)],
        ),
        compiler_params=_params(),
        name="expert_mlp",
    )(block_expert, n_used, xs, w1, w3, w2)


def _combine_kernel(x_ref, y0_ref, y1_ref, gate_ref, g_ref, b_ref, o_ref):
    gates = gate_ref[...]
    ff = gates[:, 0:1] * y0_ref[...] + gates[:, 1:2] * y1_ref[...]
    o_ref[...] = _layer_norm(ALPHA * x_ref[...] + ff, g_ref[...], b_ref[...])


def _combine(x, yg, gates, g, b):
    n, d = x.shape
    tiles = n // TM
    return pl.pallas_call(
        _combine_kernel,
        out_shape=jax.ShapeDtypeStruct((n, d), F32),
        grid=(tiles,),
        in_specs=[pl.BlockSpec((TM, d), lambda i: (i, 0)),
                  pl.BlockSpec((TM, d), lambda i: (i, 0)),
                  pl.BlockSpec((TM, d), lambda i: (i + tiles, 0)),
                  pl.BlockSpec((TM, TOP_K), lambda i: (i, 0)),
                  _const_spec((1, d)), _const_spec((1, d))],
        out_specs=pl.BlockSpec((TM, d), lambda i: (i, 0)),
        compiler_params=_params(),
        name="combine",
    )(x, yg, yg, gates, g.reshape(1, -1), b.reshape(1, -1))


def _moe(x, layer, wr_g, br_g, wr_e, br_e, w1, w3, w2, g, b):
    n, d = x.shape
    n_blocks = (n * TOP_K + N_EXPERTS * (EXPERT_BLOCK - 1) + EXPERT_BLOCK - 1) // EXPERT_BLOCK
    ri, gates, counts = _router(x, wr_g, br_g, wr_e, br_e)

    blocks_e = (counts + EXPERT_BLOCK - 1) // EXPERT_BLOCK
    blocks_end = jnp.cumsum(blocks_e)
    run_start = (blocks_end - blocks_e) * EXPERT_BLOCK
    n_used = blocks_end[-1:].astype(I32)
    block_ids = jnp.arange(n_blocks, dtype=I32)
    block_expert = jnp.minimum(jnp.sum(blocks_end[None, :] <= block_ids[:, None], axis=1),
                               N_EXPERTS - 1).astype(I32)
    expert_ids = jnp.arange(N_EXPERTS, dtype=I32)
    start_of = jnp.sum(jnp.where(ri[:, :TOP_K, None] == expert_ids, run_start, 0), axis=-1)
    dest = (start_of + ri[:, TOP_K:]).astype(I32)
    row_token = _sc_row_tokens(dest.reshape(-1), n_blocks * EXPERT_BLOCK, n)

    xs = _sc_gather(x, row_token)
    y = _expert_mlp(xs, block_expert, n_used, layer, w1, w3, w2, n_blocks)
    yg = _sc_gather(y, dest.T.reshape(-1))
    return _combine(x, yg, gates, g, b)


def kernel(x, mem, w_in_even, w_pool, pool_scale, ln_v_g, ln_v_b, w_spatial, b_spatial, w_out_even,
           w_in_odd, conv_w, conv_b, w_out_odd, wq_x, wk_x, wv_x, wo_x, ln_g, ln_b, wr_group,
           br_group, wr_expert, br_expert, w1, w3, w2):
    bsz, seq, d = x.shape
    assert seq % TM == 0 and d % LANES == 0
    mlen = mem.shape[1]
    k_all, v_all = _memory_kv(mem.reshape(bsz * mlen, d), wk_x, wv_x)
    k_all = k_all.reshape(DEPTH, bsz, mlen, d)
    v_all = v_all.reshape(DEPTH, bsz, mlen, d)
    h = x.reshape(bsz * seq, d)
    for l in range(DEPTH):
        i = l // 2
        if l % 2 == 0:
            h = _even_mixer(h, seq, w_in_even[i], w_pool[i], pool_scale[i], ln_v_g[i], ln_v_b[i],
                            w_spatial[i], b_spatial[i], w_out_even[i], ln_g[l, 0], ln_b[l, 0])
        else:
            h = _odd_mixer(h, seq, w_in_odd[i], conv_w[i], conv_b[i], w_out_odd[i],
                           ln_g[l, 0], ln_b[l, 0])
        h = _cross_attn(h, seq, k_all[l], v_all[l], wq_x[l], wo_x[l], ln_g[l, 1], ln_b[l, 1])
        h = _moe(h, l, wr_group[l], br_group[l], wr_expert[l], br_expert[l], w1, w3, w2,
                 ln_g[l, 2], ln_b[l, 2])
    return h.reshape(bsz, seq, d)
```

```python
import functools
import math

import jax
import jax.numpy as jnp
from jax import lax
from jax.experimental import pallas as pl
from jax.experimental.pallas import tpu as pltpu
from jax.experimental.pallas import tpu_sc as plsc

F32 = jnp.float32
BF16 = jnp.bfloat16
I32 = jnp.int32

POOL_WINDOWS = (2, 4, 8, 16)
N_SG_HEADS = 4
CHUNK = 128
CONV_WIDTH = 3
N_XHEADS = 4
N_GROUPS = 4
EXPERTS_PER_GROUP = 8
N_EXPERTS = N_GROUPS * EXPERTS_PER_GROUP
TOP_K = 2
DEPTH = 4
ALPHA = (2.0 * DEPTH) ** 0.25
LN_EPS = 1e-5

LANES = 128
SC_CORES = 2
SC_WORKERS = 32
SC_LANES = 16
SC_CHUNK = 64
TM = 512
POOL_HALO = 16
CONV_HALO = 8
EXPERT_BLOCK = 256
ROUTE_COLS = 128
VMEM_LIMIT = 48 * 1024 * 1024

_NT = (((1,), (1,)), ((), ()))


def _dot(a, b):
    return jnp.dot(a, b, preferred_element_type=F32)


def _layer_norm(y, g, b):
    mu = jnp.mean(y, axis=-1, keepdims=True)
    yc = y - mu
    var = jnp.mean(yc * yc, axis=-1, keepdims=True)
    return yc * lax.rsqrt(var + LN_EPS) * g + b


def _gelu_tanh(x):
    c = math.sqrt(2.0 / math.pi)
    return 0.5 * x * (1.0 + jnp.tanh(c * (x + 0.044715 * (x * x * x))))


def _pack_halves(v):
    c = v.shape[1] // 2
    lo = pltpu.bitcast(v[:, :c].astype(BF16).astype(F32), jnp.uint32)
    hi = pltpu.bitcast(v[:, c:].astype(BF16).astype(F32), jnp.uint32)
    return pltpu.bitcast((hi & jnp.uint32(0xFFFF0000)) | (lo >> 16), I32)


def _unpack_halves(w):
    u = pltpu.bitcast(w, jnp.uint32)
    return pltpu.bitcast(u << 16, F32), pltpu.bitcast(u & jnp.uint32(0xFFFF0000), F32)


def _const_spec(shape):
    nd = len(shape)
    return pl.BlockSpec(shape, lambda i: (0,) * nd)


def _params():
    return pltpu.CompilerParams(dimension_semantics=("arbitrary",), vmem_limit_bytes=VMEM_LIMIT)


def _even_kernel(x_ref, xh_ref, win_ref, wpool_ref, pscale_ref, lvg_ref, lvb_ref, ws_ref, bst_ref,
                 wout_ref, g_ref, b_ref, o_ref, a_scr, cat_scr, *, tiles_per_seq):
    tm = x_ref.shape[0]
    d_pool = a_scr.shape[1]
    d_sg = lvg_ref.shape[1]
    pgd = d_pool // len(POOL_WINDOWS)
    hd_dim = d_sg // N_SG_HEADS
    seq_tile = pl.program_id(0) % tiles_per_seq

    x = x_ref[...]
    h = _dot(x.astype(BF16), win_ref[...])

    ah = _dot(xh_ref[...].astype(BF16), win_ref[:, :d_pool])
    a_scr[0:POOL_HALO, :] = jnp.where(seq_tile == 0, 0.0, ah)
    a_scr[POOL_HALO:POOL_HALO + tm, :] = h[:, :d_pool]
    pos = seq_tile * tm + lax.broadcasted_iota(I32, (tm, 1), 0)
    for g, w in enumerate(POOL_WINDOWS):
        cs = slice(g * pgd, (g + 1) * pgd)
        tok = a_scr[POOL_HALO:POOL_HALO + tm, cs]
        acc = tok
        for j in range(1, w):
            acc = acc + a_scr[POOL_HALO - j:POOL_HALO - j + tm, cs]
        cnt = jnp.minimum(pos + 1, w).astype(F32)
        d = acc / cnt - tok
        yg = _dot(d.astype(BF16), wpool_ref[g])
        cat_scr[:, cs] = (yg * pscale_ref[:, cs]).astype(BF16)

    z = _gelu_tanh(h[:, d_pool:])
    u = z[:, :d_sg]
    v = _layer_norm(z[:, d_sg:], lvg_ref[...], lvb_ref[...]).astype(BF16)
    row = lax.broadcasted_iota(I32, (CHUNK, CHUNK), 0)
    col = lax.broadcasted_iota(I32, (CHUNK, CHUNK), 1)
    for hd in range(N_SG_HEADS):
        hs = slice(hd * hd_dim, (hd + 1) * hd_dim)
        wsm = jnp.where(row >= col, ws_ref[hd], 0.0).astype(BF16)
        bcol = bst_ref[:, hd:hd + 1]
        for ck in range(tm // CHUNK):
            rs = slice(ck * CHUNK, (ck + 1) * CHUNK)
            sv = _dot(wsm, v[rs, hs]) + bcol
            cat_scr[rs, d_pool + hd * hd_dim:d_pool + (hd + 1) * hd_dim] = (u[rs, hs] * sv).astype(BF16)

    mix = _dot(cat_scr[...], wout_ref[...])
    o_ref[...] = _layer_norm(ALPHA * x + mix, g_ref[...], b_ref[...])


def _even_mixer(x, seq, w_in, w_pool, pool_scale, ln_v_g, ln_v_b, w_spatial, b_spatial, w_out, g, b):
    n, d = x.shape
    d_in = w_in.shape[1]
    d_pool = pool_scale.shape[0]
    d_sg = ln_v_g.shape[0]
    kern = functools.partial(_even_kernel, tiles_per_seq=seq // TM)
    halo_blocks = TM // POOL_HALO
    return pl.pallas_call(
        kern,
        out_shape=jax.ShapeDtypeStruct((n, d), F32),
        grid=(n // TM,),
        in_specs=[
            pl.BlockSpec((TM, d), lambda i: (i, 0)),
            pl.BlockSpec((POOL_HALO, d), lambda i: (jnp.maximum(i * halo_blocks - 1, 0), 0)),
            _const_spec((d, d_in)),
            _const_spec(w_pool.shape),
            _const_spec((1, d_pool)),
            _const_spec((1, d_sg)),
            _const_spec((1, d_sg)),
            _const_spec(w_spatial.shape),
            _const_spec((CHUNK, N_SG_HEADS)),
            _const_spec(w_out.shape),
            _const_spec((1, d)),
            _const_spec((1, d)),
        ],
        out_specs=pl.BlockSpec((TM, d), lambda i: (i, 0)),
        scratch_shapes=[pltpu.VMEM((POOL_HALO + TM, d_pool), F32), pltpu.VMEM((TM, d_pool + d_sg), BF16)],
        compiler_params=_params(),
        name="even_mixer",
    )(x, x, w_in.astype(BF16), w_pool.astype(BF16), pool_scale.reshape(1, -1), ln_v_g.reshape(1, -1),
      ln_v_b.reshape(1, -1), w_spatial, b_spatial.T, w_out.astype(BF16), g.reshape(1, -1), b.reshape(1, -1))


def _odd_kernel(x_ref, xh_ref, win_ref, cwt_ref, cb_ref, wout_ref, g_ref, b_ref, o_ref, zc_scr,
                *, tiles_per_seq):
    tm, d = x_ref.shape
    seq_tile = pl.program_id(0) % tiles_per_seq

    x = x_ref[...]
    h = _dot(x.astype(BF16), win_ref[...])
    hh = _dot(xh_ref[...].astype(BF16), win_ref[:, d:])
    zc_scr[0:CONV_HALO, :] = jnp.where(seq_tile == 0, 0.0, hh[:, :d] * hh[:, d:])
    zc_scr[CONV_HALO:CONV_HALO + tm, :] = h[:, d:2 * d] * h[:, 2 * d:]
    conv = cb_ref[...]
    for j in range(CONV_WIDTH):
        off = CONV_HALO - (CONV_WIDTH - 1) + j
        conv = conv + zc_scr[off:off + tm, :] * cwt_ref[j:j + 1, :]
    y = _dot((h[:, :d] * conv).astype(BF16), wout_ref[...])
    o_ref[...] = _layer_norm(ALPHA * x + y, g_ref[...], b_ref[...])


def _odd_mixer(x, seq, w_in, conv_w, conv_b, w_out, g, b):
    n, d = x.shape
    kern = functools.partial(_odd_kernel, tiles_per_seq=seq // TM)
    halo_blocks = TM // CONV_HALO
    return pl.pallas_call(
        kern,
        out_shape=jax.ShapeDtypeStruct((n, d), F32),
        grid=(n // TM,),
        in_specs=[
            pl.BlockSpec((TM, d), lambda i: (i, 0)),
            pl.BlockSpec((CONV_HALO, d), lambda i: (jnp.maximum(i * halo_blocks - 1, 0), 0)),
            _const_spec(w_in.shape),
            _const_spec((CONV_WIDTH, d)),
            _const_spec((1, d)),
            _const_spec(w_out.shape),
            _const_spec((1, d)),
            _const_spec((1, d)),
        ],
        out_specs=pl.BlockSpec((TM, d), lambda i: (i, 0)),
        scratch_shapes=[pltpu.VMEM((CONV_HALO + TM, d), F32)],
        compiler_params=_params(),
        name="odd_mixer",
    )(x, x, w_in.astype(BF16), conv_w.T, conv_b.reshape(1, -1), w_out.astype(BF16),
      g.reshape(1, -1), b.reshape(1, -1))


def _kv_kernel(mem_ref, wk_ref, wv_ref, k_ref, v_ref):
    m = mem_ref[...].astype(BF16)
    k_ref[...] = _dot(m, wk_ref[...]).astype(BF16)
    v_ref[...] = _dot(m, wv_ref[...]).astype(BF16)


def _memory_kv(mem2d, wk, wv):
    nl, d, _ = wk.shape
    rows = mem2d.shape[0]
    out = jax.ShapeDtypeStruct((nl, rows, d), BF16)
    wspec = pl.BlockSpec((None, d, d), lambda l: (l, 0, 0))
    ospec = pl.BlockSpec((None, rows, d), lambda l: (l, 0, 0))
    return pl.pallas_call(
        _kv_kernel,
        out_shape=(out, out),
        grid=(nl,),
        in_specs=[_const_spec((rows, d)), wspec, wspec],
        out_specs=(ospec, ospec),
        compiler_params=_params(),
        name="memory_kv",
    )(mem2d, wk.astype(BF16), wv.astype(BF16))


def _attn_kernel(x_ref, k_ref, v_ref, wq_ref, wo_ref, g_ref, b_ref, o_ref, op_ref, o_scr):
    d = x_ref.shape[1]
    hd_dim = d // N_XHEADS
    x = x_ref[...]
    q = _dot(x.astype(BF16), wq_ref[...]) * (1.0 / math.sqrt(hd_dim))
    for hd in range(N_XHEADS):
        hs = slice(hd * hd_dim, (hd + 1) * hd_dim)
        s = lax.dot_general(q[:, hs].astype(BF16), k_ref[:, hs], _NT, preferred_element_type=F32)
        p = jnp.exp(s - jnp.max(s, axis=-1, keepdims=True))
        p = p / jnp.sum(p, axis=-1, keepdims=True)
        o_scr[:, hs] = _dot(p.astype(BF16), v_ref[:, hs]).astype(BF16)
    xa = _dot(o_scr[...], wo_ref[...])
    out = _layer_norm(ALPHA * x + xa, g_ref[...], b_ref[...])
    o_ref[...] = out
    op_ref[...] = _pack_halves(out)


def _cross_attn(x, seq, k, v, wq, wo, g, b):
    n, d = x.shape
    m = k.shape[1]
    tiles_per_seq = seq // TM
    kvspec = pl.BlockSpec((None, m, d), lambda i: (i // tiles_per_seq, 0, 0))
    return pl.pallas_call(
        _attn_kernel,
        out_shape=(jax.ShapeDtypeStruct((n, d), F32), jax.ShapeDtypeStruct((n, d // 2), I32)),
        grid=(n // TM,),
        in_specs=[
            pl.BlockSpec((TM, d), lambda i: (i, 0)),
            kvspec, kvspec,
            _const_spec((d, d)), _const_spec((d, d)),
            _const_spec((1, d)), _const_spec((1, d)),
        ],
        out_specs=(pl.BlockSpec((TM, d), lambda i: (i, 0)), pl.BlockSpec((TM, d // 2), lambda i: (i, 0))),
        scratch_shapes=[pltpu.VMEM((TM, d), BF16)],
        compiler_params=_params(),
        name="cross_attn",
    )(x, k, v, wq.astype(BF16), wo.astype(BF16), g.reshape(1, -1), b.reshape(1, -1))


def _router_kernel(x_ref, wcat_ref, bias_ref, ri_ref, rg_ref, cnt_ref, carry_scr):
    tm = x_ref.shape[0]
    step = pl.program_id(0)

    @pl.when(step == 0)
    def _():
        carry_scr[...] = jnp.zeros_like(carry_scr)

    x = x_ref[...]
    xh = x.astype(BF16)
    xl = (x - xh.astype(F32)).astype(BF16)
    r1 = _dot(xh, wcat_ref[...])
    r2 = _dot(xl, wcat_ref[:, :ROUTE_COLS])
    logits = r1[:, :ROUTE_COLS] + r1[:, ROUTE_COLS:] + r2 + bias_ref[...]

    lane = lax.broadcasted_iota(I32, (tm, ROUTE_COLS), 1).astype(F32)
    neg = -jnp.inf

    def first_argmax(vals):
        mx = jnp.max(vals, axis=-1, keepdims=True)
        idx = jnp.min(jnp.where(vals == mx, lane, float(ROUTE_COLS)), axis=-1, keepdims=True)
        return mx, idx

    gl = jnp.where(lane < N_GROUPS, logits, neg)
    gmax, g_sel = first_argmax(gl)
    gate_g = 1.0 / jnp.sum(jnp.exp(gl - gmax), axis=-1, keepdims=True)

    lo = N_GROUPS + g_sel * EXPERTS_PER_GROUP
    el = jnp.where((lane >= lo) & (lane < lo + EXPERTS_PER_GROUP), logits, neg)
    m1, i1 = first_argmax(el)
    m2, i2 = first_argmax(jnp.where(lane == i1, neg, el))
    e21 = jnp.exp(m2 - m1)
    w1 = 1.0 / (1.0 + e21)
    w2 = e21 / (1.0 + e21)

    oh1 = lane == i1
    oh2 = lane == i2
    oh = (oh1 | oh2).astype(BF16)
    r = lax.broadcasted_iota(I32, (tm, tm), 0)
    c = lax.broadcasted_iota(I32, (tm, tm), 1)
    before = _dot((r > c).astype(BF16), oh) + carry_scr[...]
    rank1 = jnp.sum(jnp.where(oh1, before, 0.0), axis=-1, keepdims=True)
    rank2 = jnp.sum(jnp.where(oh2, before, 0.0), axis=-1, keepdims=True)
    carry_scr[...] += jnp.sum(oh.astype(F32), axis=0, keepdims=True)

    l4 = lax.broadcasted_iota(I32, ri_ref.shape, 1)
    ri_ref[...] = jnp.where(l4 == 0, i1 - N_GROUPS,
                            jnp.where(l4 == 1, i2 - N_GROUPS,
                                      jnp.where(l4 == 2, rank1, rank2))).astype(I32)
    l2 = lax.broadcasted_iota(I32, rg_ref.shape, 1)
    rg_ref[...] = jnp.where(l2 == 0, gate_g * w1, gate_g * w2)
    cnt_ref[...] = carry_scr[...].astype(I32)


def _router(x, wr_g, br_g, wr_e, br_e):
    n, d = x.shape
    w = jnp.concatenate([wr_g, jnp.transpose(wr_e, (1, 0, 2)).reshape(d, N_EXPERTS)], axis=1)
    w = jnp.pad(w, ((0, 0), (0, ROUTE_COLS - w.shape[1])))
    w_hi = w.astype(BF16)
    w_lo = (w - w_hi.astype(F32)).astype(BF16)
    bias = jnp.pad(jnp.concatenate([br_g, br_e.reshape(-1)]), (0, ROUTE_COLS - N_GROUPS - N_EXPERTS))
    ri, rg, cnt = pl.pallas_call(
        _router_kernel,
        out_shape=(jax.ShapeDtypeStruct((n, 4), I32), jax.ShapeDtypeStruct((n, TOP_K), F32),
                   jax.ShapeDtypeStruct((1, ROUTE_COLS), I32)),
        grid=(n // TM,),
        in_specs=[pl.BlockSpec((TM, d), lambda i: (i, 0)),
                  _const_spec((d, 2 * ROUTE_COLS)), _const_spec((1, ROUTE_COLS))],
        out_specs=(pl.BlockSpec((TM, 4), lambda i: (i, 0)), pl.BlockSpec((TM, TOP_K), lambda i: (i, 0)),
                   _const_spec((1, ROUTE_COLS))),
        scratch_shapes=[pltpu.VMEM((1, ROUTE_COLS), F32)],
        compiler_params=_params(),
        name="router",
    )(x, jnp.concatenate([w_hi, w_lo], axis=1), bias.reshape(1, -1))
    return ri, rg, cnt[0, N_GROUPS:N_GROUPS + N_EXPERTS]


def _sc_gather(table, idx):
    b = idx.shape[0]
    d = table.shape[1]
    per_worker = b // SC_WORKERS
    n_chunks = per_worker // SC_CHUNK
    assert per_worker * SC_WORKERS == b and n_chunks * SC_CHUNK == per_worker and n_chunks % 2 == 0
    mesh = plsc.VectorSubcoreMesh(core_axis_name="c", subcore_axis_name="s")

    @functools.partial(
        pl.kernel, mesh=mesh,
        out_type=jax.ShapeDtypeStruct((b, d), table.dtype),
        scratch_types=[pltpu.VMEM((per_worker,), I32),
                       pltpu.VMEM((2, SC_CHUNK, d), table.dtype),
                       pltpu.SemaphoreType.DMA((2,)),
                       pltpu.SemaphoreType.DMA((2,))],
        name="sc_gather",
    )
    def gather(table_hbm, idx_hbm, out_hbm, idx_v, rows_v, gsem, wsem):
        wid = lax.axis_index("s") * SC_CORES + lax.axis_index("c")
        base = wid * per_worker
        pltpu.sync_copy(idx_hbm.at[pl.ds(base, per_worker)], idx_v)

        def fetch(c, slot):
            off = pl.multiple_of(c * SC_CHUNK, SC_CHUNK)
            return pltpu.make_async_copy(table_hbm.at[idx_v.at[pl.ds(off, SC_CHUNK)]], rows_v.at[slot],
                                         gsem.at[slot])

        def put(c, slot):
            off = pl.multiple_of(c * SC_CHUNK, SC_CHUNK)
            return pltpu.make_async_copy(rows_v.at[slot], out_hbm.at[pl.ds(base + off, SC_CHUNK)],
                                         wsem.at[slot])

        fetch(0, 0).start()

        @pl.loop(0, n_chunks, step=2)
        def _(c0):
            for slot in range(2):
                c = c0 + slot

                @pl.when(c + 1 < n_chunks)
                def _():
                    @pl.when(c >= 1)
                    def _():
                        put(c - 1, 1 - slot).wait()
                    fetch(c + 1, 1 - slot).start()

                fetch(c, slot).wait()
                put(c, slot).start()

        put(n_chunks - 2, 0).wait()
        put(n_chunks - 1, 1).wait()

    return gather(table, idx)


def _sc_row_tokens(dest_flat, rows, n):
    a = dest_flat.shape[0]
    lanes = SC_LANES
    assert a % lanes == 0 and rows % lanes == 0
    mesh = plsc.VectorSubcoreMesh(core_axis_name="c", subcore_axis_name="s")

    @functools.partial(
        pl.kernel, mesh=mesh,
        out_type=jax.ShapeDtypeStruct((rows,), I32),
        scratch_types=[pltpu.VMEM((a,), I32), pltpu.VMEM((rows,), I32)],
        compiler_params=pltpu.CompilerParams(needs_layout_passes=False),
        name="sc_row_tokens",
    )
    def invert(dest_hbm, out_hbm, dest_v, map_v):
        @pl.when((lax.axis_index("s") == 0) & (lax.axis_index("c") == 0))
        def _():
            pltpu.sync_copy(dest_hbm, dest_v)
            lane = lax.iota(I32, lanes)

            @pl.loop(0, rows // lanes)
            def _(i):
                map_v[pl.ds(i * lanes, lanes)] = lax.rem(i * lanes + lane, n)

            @pl.loop(0, a // lanes)
            def _(i):
                plsc.store_scatter(map_v, [dest_v[pl.ds(i * lanes, lanes)]], (i * lanes + lane) // TOP_K)

            pltpu.sync_copy(map_v, out_hbm)

    return invert(dest_flat)


def _expert_kernel(be_ref, nb_ref, xs_ref, w1_ref, w3_ref, w2_ref, y_ref, w1_scr, w3_scr, w2_scr):
    b = pl.program_id(0)
    used = b < nb_ref[0]

    @pl.when(used & ((b == 0) | (be_ref[b] != be_ref[jnp.maximum(b - 1, 0)])))
    def _():
        w1_scr[...] = w1_ref[...].astype(BF16)
        w3_scr[...] = w3_ref[...].astype(BF16)
        w2_scr[...] = w2_ref[...].astype(BF16)

    @pl.when(used)
    def _():
        x_lo, x_hi = _unpack_halves(xs_ref[...])
        xb = jnp.concatenate([x_lo.astype(BF16), x_hi.astype(BF16)], axis=1)
        h1 = _dot(xb, w1_scr[...])
        h3 = _dot(xb, w3_scr[...])
        hid = h1 * (1.0 / (1.0 + jnp.exp(-h1))) * h3
        y_ref[...] = _pack_halves(_dot(hid.astype(BF16), w2_scr[...]))

    @pl.when(jnp.logical_not(used))
    def _():
        y_ref[...] = jnp.zeros_like(y_ref)


def _expert_mlp(xs, block_expert, n_used, layer, w1, w3, w2, n_blocks):
    d, de = w1.shape[2], w1.shape[3]

    def row_map(b, be, nb):
        return (jnp.minimum(b, nb[0] - 1), 0)

    def w_map(b, be, nb):
        return (layer, be[jnp.minimum(b, nb[0] - 1)], 0, 0)

    return pl.pallas_call(
        _expert_kernel,
        out_shape=jax.ShapeDtypeStruct((n_blocks * EXPERT_BLOCK, d // 2), I32),
        grid_spec=pltpu.PrefetchScalarGridSpec(
            num_scalar_prefetch=2,
            grid=(n_blocks,),
            in_specs=[pl.BlockSpec((EXPERT_BLOCK, d // 2), row_map),
                      pl.BlockSpec((None, None, d, de), w_map),
                      pl.BlockSpec((None, None, d, de), w_map),
                      pl.BlockSpec((None, None, de, d), w_map)],
            out_specs=pl.BlockSpec((EXPERT_BLOCK, d // 2), lambda b, be, nb: (b, 0)),
            scratch_shapes=[pltpu.VMEM((d, de), BF16), pltpu.VMEM((d, de), BF16), pltpu.VMEM((de, d), BF16)],
        ),
        compiler_params=_params(),
        name="expert_mlp",
    )(block_expert, n_used, xs, w1, w3, w2)


def _combine_kernel(x_ref, y0_ref, y1_ref, gate_ref, g_ref, b_ref, o_ref):
    gates = gate_ref[...]
    g0, g1 = gates[:, 0:1], gates[:, 1:2]
    y0_lo, y0_hi = _unpack_halves(y0_ref[...])
    y1_lo, y1_hi = _unpack_halves(y1_ref[...])
    ff = jnp.concatenate([g0 * y0_lo + g1 * y1_lo, g0 * y0_hi + g1 * y1_hi], axis=1)
    o_ref[...] = _layer_norm(ALPHA * x_ref[...] + ff, g_ref[...], b_ref[...])


def _combine(x, yg, gates, g, b):
    n, d = x.shape
    tiles = n // TM
    return pl.pallas_call(
        _combine_kernel,
        out_shape=jax.ShapeDtypeStruct((n, d), F32),
        grid=(tiles,),
        in_specs=[pl.BlockSpec((TM, d), lambda i: (i, 0)),
                  pl.BlockSpec((TM, d // 2), lambda i: (i, 0)),
                  pl.BlockSpec((TM, d // 2), lambda i: (i + tiles, 0)),
                  pl.BlockSpec((TM, TOP_K), lambda i: (i, 0)),
                  _const_spec((1, d)), _const_spec((1, d))],
        out_specs=pl.BlockSpec((TM, d), lambda i: (i, 0)),
        compiler_params=_params(),
        name="combine",
    )(x, yg, yg, gates, g.reshape(1, -1), b.reshape(1, -1))


def _moe(x, x_packed, layer, wr_g, br_g, wr_e, br_e, w1, w3, w2, g, b):
    n, d = x.shape
    n_blocks = (n * TOP_K + N_EXPERTS * (EXPERT_BLOCK - 1) + EXPERT_BLOCK - 1) // EXPERT_BLOCK
    ri, gates, counts = _router(x, wr_g, br_g, wr_e, br_e)

    blocks_e = (counts + EXPERT_BLOCK - 1) // EXPERT_BLOCK
    blocks_end = jnp.cumsum(blocks_e)
    run_start = (blocks_end - blocks_e) * EXPERT_BLOCK
    n_used = blocks_end[-1:].astype(I32)
    block_ids = jnp.arange(n_blocks, dtype=I32)
    block_expert = jnp.minimum(jnp.sum(blocks_end[None, :] <= block_ids[:, None], axis=1),
                               N_EXPERTS - 1).astype(I32)
    expert_ids = jnp.arange(N_EXPERTS, dtype=I32)
    start_of = jnp.sum(jnp.where(ri[:, :TOP_K, None] == expert_ids, run_start, 0), axis=-1)
    dest = (start_of + ri[:, TOP_K:]).astype(I32)
    row_token = _sc_row_tokens(dest.reshape(-1), n_blocks * EXPERT_BLOCK, n)

    xs = _sc_gather(x_packed, row_token)
    y = _expert_mlp(xs, block_expert, n_used, layer, w1, w3, w2, n_blocks)
    yg = _sc_gather(y, dest.T.reshape(-1))
    return _combine(x, yg, gates, g, b)


def kernel(x, mem, w_in_even, w_pool, pool_scale, ln_v_g, ln_v_b, w_spatial, b_spatial, w_out_even,
           w_in_odd, conv_w, conv_b, w_out_odd, wq_x, wk_x, wv_x, wo_x, ln_g, ln_b, wr_group,
           br_group, wr_expert, br_expert, w1, w3, w2):
    bsz, seq, d = x.shape
    assert seq % TM == 0 and d % LANES == 0
    mlen = mem.shape[1]
    k_all, v_all = _memory_kv(mem.reshape(bsz * mlen, d), wk_x, wv_x)
    k_all = k_all.reshape(DEPTH, bsz, mlen, d)
    v_all = v_all.reshape(DEPTH, bsz, mlen, d)
    h = x.reshape(bsz * seq, d)
    for l in range(DEPTH):
        i = l // 2
        if l % 2 == 0:
            h = _even_mixer(h, seq, w_in_even[i], w_pool[i], pool_scale[i], ln_v_g[i], ln_v_b[i],
                            w_spatial[i], b_spatial[i], w_out_even[i], ln_g[l, 0], ln_b[l, 0])
        else:
            h = _odd_mixer(h, seq, w_in_odd[i], conv_w[i], conv_b[i], w_out_odd[i],
                           ln_g[l, 0], ln_b[l, 0])
        h, hp = _cross_attn(h, seq, k_all[l], v_all[l], wq_x[l], wo_x[l], ln_g[l, 1], ln_b[l, 1])
        h = _moe(h, hp, l, wr_group[l], br_group[l], wr_expert[l], br_expert[l], w1, w3, w2,
                 ln_g[l, 2], ln_b[l, 2])
    return h.reshape(bsz, seq, d)
```

```python
import functools
import math

import jax
import jax.numpy as jnp
from jax import lax
from jax.experimental import pallas as pl
from jax.experimental.pallas import tpu as pltpu
from jax.experimental.pallas import tpu_sc as plsc

F32 = jnp.float32
BF16 = jnp.bfloat16
I32 = jnp.int32

POOL_WINDOWS = (2, 4, 8, 16)
N_SG_HEADS = 4
CHUNK = 128
CONV_WIDTH = 3
N_XHEADS = 4
N_GROUPS = 4
EXPERTS_PER_GROUP = 8
N_EXPERTS = N_GROUPS * EXPERTS_PER_GROUP
TOP_K = 2
DEPTH = 4
ALPHA = (2.0 * DEPTH) ** 0.25
LN_EPS = 1e-5

LANES = 128
SC_CORES = 2
SC_WORKERS = 32
SC_LANES = 16
SC_CHUNK = 64
TM = 512
POOL_HALO = 16
CONV_HALO = 8
EXPERT_BLOCK = 512
ROUTE_COLS = 128
VMEM_LIMIT = 48 * 1024 * 1024

_NT = (((1,), (1,)), ((), ()))


def _dot(a, b):
    return jnp.dot(a, b, preferred_element_type=F32)


def _layer_norm(y, g, b):
    mu = jnp.mean(y, axis=-1, keepdims=True)
    yc = y - mu
    var = jnp.mean(yc * yc, axis=-1, keepdims=True)
    return yc * lax.rsqrt(var + LN_EPS) * g + b


def _gelu_tanh(x):
    c = math.sqrt(2.0 / math.pi)
    return 0.5 * x * (1.0 + jnp.tanh(c * (x + 0.044715 * (x * x * x))))


def _pack_halves(v):
    c = v.shape[1] // 2
    lo = pltpu.bitcast(v[:, :c].astype(BF16).astype(F32), jnp.uint32)
    hi = pltpu.bitcast(v[:, c:].astype(BF16).astype(F32), jnp.uint32)
    return pltpu.bitcast((hi & jnp.uint32(0xFFFF0000)) | (lo >> 16), I32)


def _unpack_halves(w):
    u = pltpu.bitcast(w, jnp.uint32)
    return pltpu.bitcast(u << 16, F32), pltpu.bitcast(u & jnp.uint32(0xFFFF0000), F32)


def _const_spec(shape):
    nd = len(shape)
    return pl.BlockSpec(shape, lambda i: (0,) * nd)


def _params():
    return pltpu.CompilerParams(dimension_semantics=("arbitrary",), vmem_limit_bytes=VMEM_LIMIT)


def _even_kernel(x_ref, xh_ref, win_ref, wpool_ref, pscale_ref, lvg_ref, lvb_ref, ws_ref, bst_ref,
                 wout_ref, g_ref, b_ref, o_ref, a_scr, cat_scr, *, tiles_per_seq):
    tm = x_ref.shape[0]
    d_pool = a_scr.shape[1]
    d_sg = lvg_ref.shape[1]
    pgd = d_pool // len(POOL_WINDOWS)
    hd_dim = d_sg // N_SG_HEADS
    seq_tile = pl.program_id(0) % tiles_per_seq

    x = x_ref[...]
    h = _dot(x.astype(BF16), win_ref[...])

    ah = _dot(xh_ref[...].astype(BF16), win_ref[:, :d_pool])
    a_scr[0:POOL_HALO, :] = jnp.where(seq_tile == 0, 0.0, ah)
    a_scr[POOL_HALO:POOL_HALO + tm, :] = h[:, :d_pool]
    pos = seq_tile * tm + lax.broadcasted_iota(I32, (tm, 1), 0)
    for g, w in enumerate(POOL_WINDOWS):
        cs = slice(g * pgd, (g + 1) * pgd)
        tok = a_scr[POOL_HALO:POOL_HALO + tm, cs]
        acc = tok
        for j in range(1, w):
            acc = acc + a_scr[POOL_HALO - j:POOL_HALO - j + tm, cs]
        cnt = jnp.minimum(pos + 1, w).astype(F32)
        d = acc / cnt - tok
        yg = _dot(d.astype(BF16), wpool_ref[g])
        cat_scr[:, cs] = (yg * pscale_ref[:, cs]).astype(BF16)

    z = _gelu_tanh(h[:, d_pool:])
    u = z[:, :d_sg]
    v = _layer_norm(z[:, d_sg:], lvg_ref[...], lvb_ref[...]).astype(BF16)
    row = lax.broadcasted_iota(I32, (CHUNK, CHUNK), 0)
    col = lax.broadcasted_iota(I32, (CHUNK, CHUNK), 1)
    for hd in range(N_SG_HEADS):
        hs = slice(hd * hd_dim, (hd + 1) * hd_dim)
        wsm = jnp.where(row >= col, ws_ref[hd], 0.0).astype(BF16)
        bcol = bst_ref[:, hd:hd + 1]
        for ck in range(tm // CHUNK):
            rs = slice(ck * CHUNK, (ck + 1) * CHUNK)
            sv = _dot(wsm, v[rs, hs]) + bcol
            cat_scr[rs, d_pool + hd * hd_dim:d_pool + (hd + 1) * hd_dim] = (u[rs, hs] * sv).astype(BF16)

    mix = _dot(cat_scr[...], wout_ref[...])
    o_ref[...] = _layer_norm(ALPHA * x + mix, g_ref[...], b_ref[...])


def _even_mixer(x, seq, w_in, w_pool, pool_scale, ln_v_g, ln_v_b, w_spatial, b_spatial, w_out, g, b):
    n, d = x.shape
    d_in = w_in.shape[1]
    d_pool = pool_scale.shape[0]
    d_sg = ln_v_g.shape[0]
    kern = functools.partial(_even_kernel, tiles_per_seq=seq // TM)
    halo_blocks = TM // POOL_HALO
    return pl.pallas_call(
        kern,
        out_shape=jax.ShapeDtypeStruct((n, d), F32),
        grid=(n // TM,),
        in_specs=[
            pl.BlockSpec((TM, d), lambda i: (i, 0)),
            pl.BlockSpec((POOL_HALO, d), lambda i: (jnp.maximum(i * halo_blocks - 1, 0), 0)),
            _const_spec((d, d_in)),
            _const_spec(w_pool.shape),
            _const_spec((1, d_pool)),
            _const_spec((1, d_sg)),
            _const_spec((1, d_sg)),
            _const_spec(w_spatial.shape),
            _const_spec((CHUNK, N_SG_HEADS)),
            _const_spec(w_out.shape),
            _const_spec((1, d)),
            _const_spec((1, d)),
        ],
        out_specs=pl.BlockSpec((TM, d), lambda i: (i, 0)),
        scratch_shapes=[pltpu.VMEM((POOL_HALO + TM, d_pool), F32), pltpu.VMEM((TM, d_pool + d_sg), BF16)],
        compiler_params=_params(),
        name="even_mixer",
    )(x, x, w_in.astype(BF16), w_pool.astype(BF16), pool_scale.reshape(1, -1), ln_v_g.reshape(1, -1),
      ln_v_b.reshape(1, -1), w_spatial, b_spatial.T, w_out.astype(BF16), g.reshape(1, -1), b.reshape(1, -1))


def _odd_kernel(x_ref, xh_ref, win_ref, cwt_ref, cb_ref, wout_ref, g_ref, b_ref, o_ref, zc_scr,
                *, tiles_per_seq):
    tm, d = x_ref.shape
    seq_tile = pl.program_id(0) % tiles_per_seq

    x = x_ref[...]
    h = _dot(x.astype(BF16), win_ref[...])
    hh = _dot(xh_ref[...].astype(BF16), win_ref[:, d:])
    zc_scr[0:CONV_HALO, :] = jnp.where(seq_tile == 0, 0.0, hh[:, :d] * hh[:, d:])
    zc_scr[CONV_HALO:CONV_HALO + tm, :] = h[:, d:2 * d] * h[:, 2 * d:]
    conv = cb_ref[...]
    for j in range(CONV_WIDTH):
        off = CONV_HALO - (CONV_WIDTH - 1) + j
        conv = conv + zc_scr[off:off + tm, :] * cwt_ref[j:j + 1, :]
    y = _dot((h[:, :d] * conv).astype(BF16), wout_ref[...])
    o_ref[...] = _layer_norm(ALPHA * x + y, g_ref[...], b_ref[...])


def _odd_mixer(x, seq, w_in, conv_w, conv_b, w_out, g, b):
    n, d = x.shape
    kern = functools.partial(_odd_kernel, tiles_per_seq=seq // TM)
    halo_blocks = TM // CONV_HALO
    return pl.pallas_call(
        kern,
        out_shape=jax.ShapeDtypeStruct((n, d), F32),
        grid=(n // TM,),
        in_specs=[
            pl.BlockSpec((TM, d), lambda i: (i, 0)),
            pl.BlockSpec((CONV_HALO, d), lambda i: (jnp.maximum(i * halo_blocks - 1, 0), 0)),
            _const_spec(w_in.shape),
            _const_spec((CONV_WIDTH, d)),
            _const_spec((1, d)),
            _const_spec(w_out.shape),
            _const_spec((1, d)),
            _const_spec((1, d)),
        ],
        out_specs=pl.BlockSpec((TM, d), lambda i: (i, 0)),
        scratch_shapes=[pltpu.VMEM((CONV_HALO + TM, d), F32)],
        compiler_params=_params(),
        name="odd_mixer",
    )(x, x, w_in.astype(BF16), conv_w.T, conv_b.reshape(1, -1), w_out.astype(BF16),
      g.reshape(1, -1), b.reshape(1, -1))


def _kv_kernel(mem_ref, wk_ref, wv_ref, k_ref, v_ref):
    m = mem_ref[...].astype(BF16)
    k_ref[...] = _dot(m, wk_ref[...]).astype(BF16)
    v_ref[...] = _dot(m, wv_ref[...]).astype(BF16)


def _memory_kv(mem2d, wk, wv):
    nl, d, _ = wk.shape
    rows = mem2d.shape[0]
    out = jax.ShapeDtypeStruct((nl, rows, d), BF16)
    wspec = pl.BlockSpec((None, d, d), lambda l: (l, 0, 0))
    ospec = pl.BlockSpec((None, rows, d), lambda l: (l, 0, 0))
    return pl.pallas_call(
        _kv_kernel,
        out_shape=(out, out),
        grid=(nl,),
        in_specs=[_const_spec((rows, d)), wspec, wspec],
        out_specs=(ospec, ospec),
        compiler_params=_params(),
        name="memory_kv",
    )(mem2d, wk.astype(BF16), wv.astype(BF16))


def _attn_kernel(x_ref, k_ref, v_ref, wq_ref, wo_ref, g_ref, b_ref, o_ref, op_ref, o_scr):
    d = x_ref.shape[1]
    hd_dim = d // N_XHEADS
    x = x_ref[...]
    q = _dot(x.astype(BF16), wq_ref[...]) * (1.0 / math.sqrt(hd_dim))
    for hd in range(N_XHEADS):
        hs = slice(hd * hd_dim, (hd + 1) * hd_dim)
        s = lax.dot_general(q[:, hs].astype(BF16), k_ref[:, hs], _NT, preferred_element_type=F32)
        p = jnp.exp(s - jnp.max(s, axis=-1, keepdims=True))
        p = p / jnp.sum(p, axis=-1, keepdims=True)
        o_scr[:, hs] = _dot(p.astype(BF16), v_ref[:, hs]).astype(BF16)
    xa = _dot(o_scr[...], wo_ref[...])
    out = _layer_norm(ALPHA * x + xa, g_ref[...], b_ref[...])
    o_ref[...] = out
    op_ref[...] = _pack_halves(out)


def _cross_attn(x, seq, k, v, wq, wo, g, b):
    n, d = x.shape
    m = k.shape[1]
    tiles_per_seq = seq // TM
    kvspec = pl.BlockSpec((None, m, d), lambda i: (i // tiles_per_seq, 0, 0))
    return pl.pallas_call(
        _attn_kernel,
        out_shape=(jax.ShapeDtypeStruct((n, d), F32), jax.ShapeDtypeStruct((n, d // 2), I32)),
        grid=(n // TM,),
        in_specs=[
            pl.BlockSpec((TM, d), lambda i: (i, 0)),
            kvspec, kvspec,
            _const_spec((d, d)), _const_spec((d, d)),
            _const_spec((1, d)), _const_spec((1, d)),
        ],
        out_specs=(pl.BlockSpec((TM, d), lambda i: (i, 0)), pl.BlockSpec((TM, d // 2), lambda i: (i, 0))),
        scratch_shapes=[pltpu.VMEM((TM, d), BF16)],
        compiler_params=_params(),
        name="cross_attn",
    )(x, k, v, wq.astype(BF16), wo.astype(BF16), g.reshape(1, -1), b.reshape(1, -1))


def _router_kernel(x_ref, wcat_ref, bias_ref, ri_ref, rg_ref, cnt_ref, carry_scr):
    tm = x_ref.shape[0]
    step = pl.program_id(0)

    @pl.when(step == 0)
    def _():
        carry_scr[...] = jnp.zeros_like(carry_scr)

    x = x_ref[...]
    xh = x.astype(BF16)
    xl = (x - xh.astype(F32)).astype(BF16)
    r1 = _dot(xh, wcat_ref[...])
    r2 = _dot(xl, wcat_ref[:, :ROUTE_COLS])
    logits = r1[:, :ROUTE_COLS] + r1[:, ROUTE_COLS:] + r2 + bias_ref[...]

    lane = lax.broadcasted_iota(I32, (tm, ROUTE_COLS), 1).astype(F32)
    neg = -jnp.inf

    def first_argmax(vals):
        mx = jnp.max(vals, axis=-1, keepdims=True)
        idx = jnp.min(jnp.where(vals == mx, lane, float(ROUTE_COLS)), axis=-1, keepdims=True)
        return mx, idx

    gl = jnp.where(lane < N_GROUPS, logits, neg)
    gmax, g_sel = first_argmax(gl)
    gate_g = 1.0 / jnp.sum(jnp.exp(gl - gmax), axis=-1, keepdims=True)

    lo = N_GROUPS + g_sel * EXPERTS_PER_GROUP
    el = jnp.where((lane >= lo) & (lane < lo + EXPERTS_PER_GROUP), logits, neg)
    m1, i1 = first_argmax(el)
    m2, i2 = first_argmax(jnp.where(lane == i1, neg, el))
    e21 = jnp.exp(m2 - m1)
    w1 = 1.0 / (1.0 + e21)
    w2 = e21 / (1.0 + e21)

    oh1 = lane == i1
    oh2 = lane == i2
    oh = (oh1 | oh2).astype(BF16)
    r = lax.broadcasted_iota(I32, (tm, tm), 0)
    c = lax.broadcasted_iota(I32, (tm, tm), 1)
    before = _dot((r > c).astype(BF16), oh) + carry_scr[...]
    rank1 = jnp.sum(jnp.where(oh1, before, 0.0), axis=-1, keepdims=True)
    rank2 = jnp.sum(jnp.where(oh2, before, 0.0), axis=-1, keepdims=True)
    carry_scr[...] += jnp.sum(oh.astype(F32), axis=0, keepdims=True)

    l4 = lax.broadcasted_iota(I32, ri_ref.shape, 1)
    ri_ref[...] = jnp.where(l4 == 0, i1 - N_GROUPS,
                            jnp.where(l4 == 1, i2 - N_GROUPS,
                                      jnp.where(l4 == 2, rank1, rank2))).astype(I32)
    l2 = lax.broadcasted_iota(I32, rg_ref.shape, 1)
    rg_ref[...] = jnp.where(l2 == 0, gate_g * w1, gate_g * w2)
    cnt_ref[...] = carry_scr[...].astype(I32)


def _router(x, wr_g, br_g, wr_e, br_e):
    n, d = x.shape
    w = jnp.concatenate([wr_g, jnp.transpose(wr_e, (1, 0, 2)).reshape(d, N_EXPERTS)], axis=1)
    w = jnp.pad(w, ((0, 0), (0, ROUTE_COLS - w.shape[1])))
    w_hi = w.astype(BF16)
    w_lo = (w - w_hi.astype(F32)).astype(BF16)
    bias = jnp.pad(jnp.concatenate([br_g, br_e.reshape(-1)]), (0, ROUTE_COLS - N_GROUPS - N_EXPERTS))
    ri, rg, cnt = pl.pallas_call(
        _router_kernel,
        out_shape=(jax.ShapeDtypeStruct((n, 4), I32), jax.ShapeDtypeStruct((n, TOP_K), F32),
                   jax.ShapeDtypeStruct((1, ROUTE_COLS), I32)),
        grid=(n // TM,),
        in_specs=[pl.BlockSpec((TM, d), lambda i: (i, 0)),
                  _const_spec((d, 2 * ROUTE_COLS)), _const_spec((1, ROUTE_COLS))],
        out_specs=(pl.BlockSpec((TM, 4), lambda i: (i, 0)), pl.BlockSpec((TM, TOP_K), lambda i: (i, 0)),
                   _const_spec((1, ROUTE_COLS))),
        scratch_shapes=[pltpu.VMEM((1, ROUTE_COLS), F32)],
        compiler_params=_params(),
        name="router",
    )(x, jnp.concatenate([w_hi, w_lo], axis=1), bias.reshape(1, -1))
    return ri, rg, cnt[0, N_GROUPS:N_GROUPS + N_EXPERTS]


def _sc_gather(table, idx):
    b = idx.shape[0]
    d = table.shape[1]
    per_worker = b // SC_WORKERS
    n_chunks = per_worker // SC_CHUNK
    assert per_worker * SC_WORKERS == b and n_chunks * SC_CHUNK == per_worker and n_chunks % 2 == 0
    mesh = plsc.VectorSubcoreMesh(core_axis_name="c", subcore_axis_name="s")

    @functools.partial(
        pl.kernel, mesh=mesh,
        out_type=jax.ShapeDtypeStruct((b, d), table.dtype),
        scratch_types=[pltpu.VMEM((per_worker,), I32),
                       pltpu.VMEM((2, SC_CHUNK, d), table.dtype),
                       pltpu.SemaphoreType.DMA((2,)),
                       pltpu.SemaphoreType.DMA((2,))],
        name="sc_gather",
    )
    def gather(table_hbm, idx_hbm, out_hbm, idx_v, rows_v, gsem, wsem):
        wid = lax.axis_index("s") * SC_CORES + lax.axis_index("c")
        base = wid * per_worker
        pltpu.sync_copy(idx_hbm.at[pl.ds(base, per_worker)], idx_v)

        def fetch(c, slot):
            off = pl.multiple_of(c * SC_CHUNK, SC_CHUNK)
            return pltpu.make_async_copy(table_hbm.at[idx_v.at[pl.ds(off, SC_CHUNK)]], rows_v.at[slot],
                                         gsem.at[slot])

        def put(c, slot):
            off = pl.multiple_of(c * SC_CHUNK, SC_CHUNK)
            return pltpu.make_async_copy(rows_v.at[slot], out_hbm.at[pl.ds(base + off, SC_CHUNK)],
                                         wsem.at[slot])

        fetch(0, 0).start()

        @pl.loop(0, n_chunks, step=2)
        def _(c0):
            for slot in range(2):
                c = c0 + slot

                @pl.when(c + 1 < n_chunks)
                def _():
                    @pl.when(c >= 1)
                    def _():
                        put(c - 1, 1 - slot).wait()
                    fetch(c + 1, 1 - slot).start()

                fetch(c, slot).wait()
                put(c, slot).start()

        put(n_chunks - 2, 0).wait()
        put(n_chunks - 1, 1).wait()

    return gather(table, idx)


def _sc_row_tokens(dest_flat, rows, n):
    a = dest_flat.shape[0]
    lanes = SC_LANES
    assert a % lanes == 0 and rows % lanes == 0
    mesh = plsc.VectorSubcoreMesh(core_axis_name="c", subcore_axis_name="s")

    @functools.partial(
        pl.kernel, mesh=mesh,
        out_type=jax.ShapeDtypeStruct((rows,), I32),
        scratch_types=[pltpu.VMEM((a,), I32), pltpu.VMEM((rows,), I32)],
        compiler_params=pltpu.CompilerParams(needs_layout_passes=False),
        name="sc_row_tokens",
    )
    def invert(dest_hbm, out_hbm, dest_v, map_v):
        @pl.when((lax.axis_index("s") == 0) & (lax.axis_index("c") == 0))
        def _():
            pltpu.sync_copy(dest_hbm, dest_v)
            lane = lax.iota(I32, lanes)

            @pl.loop(0, rows // lanes)
            def _(i):
                map_v[pl.ds(i * lanes, lanes)] = lax.rem(i * lanes + lane, n)

            @pl.loop(0, a // lanes)
            def _(i):
                plsc.store_scatter(map_v, [dest_v[pl.ds(i * lanes, lanes)]], (i * lanes + lane) // TOP_K)

            pltpu.sync_copy(map_v, out_hbm)

    return invert(dest_flat)


def _expert_kernel(be_ref, nb_ref, xs_ref, w1_ref, w3_ref, w2_ref, y_ref, w1_scr, w3_scr, w2_scr):
    b = pl.program_id(0)
    used = b < nb_ref[0]

    @pl.when(used & ((b == 0) | (be_ref[b] != be_ref[jnp.maximum(b - 1, 0)])))
    def _():
        w1_scr[...] = w1_ref[...].astype(BF16)
        w3_scr[...] = w3_ref[...].astype(BF16)
        w2_scr[...] = w2_ref[...].astype(BF16)

    @pl.when(used)
    def _():
        x_lo, x_hi = _unpack_halves(xs_ref[...])
        xb = jnp.concatenate([x_lo.astype(BF16), x_hi.astype(BF16)], axis=1)
        h1 = _dot(xb, w1_scr[...])
        h3 = _dot(xb, w3_scr[...])
        hid = h1 * (1.0 / (1.0 + jnp.exp(-h1))) * h3
        y_ref[...] = _pack_halves(_dot(hid.astype(BF16), w2_scr[...]))

    @pl.when(jnp.logical_not(used))
    def _():
        y_ref[...] = jnp.zeros_like(y_ref)


def _expert_mlp(xs, block_expert, n_used, layer, w1, w3, w2, n_blocks):
    d, de = w1.shape[2], w1.shape[3]

    def row_map(b, be, nb):
        return (jnp.minimum(b, nb[0] - 1), 0)

    def w_map(b, be, nb):
        return (layer, be[jnp.minimum(b, nb[0] - 1)], 0, 0)

    return pl.pallas_call(
        _expert_kernel,
        out_shape=jax.ShapeDtypeStruct((n_blocks * EXPERT_BLOCK, d // 2), I32),
        grid_spec=pltpu.PrefetchScalarGridSpec(
            num_scalar_prefetch=2,
            grid=(n_blocks,),
            in_specs=[pl.BlockSpec((EXPERT_BLOCK, d // 2), row_map),
                      pl.BlockSpec((None, None, d, de), w_map),
                      pl.BlockSpec((None, None, d, de), w_map),
                      pl.BlockSpec((None, None, de, d), w_map)],
            out_specs=pl.BlockSpec((EXPERT_BLOCK, d // 2), lambda b, be, nb: (b, 0)),
            scratch_shapes=[pltpu.VMEM((d, de), BF16), pltpu.VMEM((d, de), BF16), pltpu.VMEM((de, d), BF16)],
        ),
        compiler_params=_params(),
        name="expert_mlp",
    )(block_expert, n_used, xs, w1, w3, w2)


def _combine_kernel(x_ref, y0_ref, y1_ref, gate_ref, g_ref, b_ref, o_ref):
    gates = gate_ref[...]
    g0, g1 = gates[:, 0:1], gates[:, 1:2]
    y0_lo, y0_hi = _unpack_halves(y0_ref[...])
    y1_lo, y1_hi = _unpack_halves(y1_ref[...])
    ff = jnp.concatenate([g0 * y0_lo + g1 * y1_lo, g0 * y0_hi + g1 * y1_hi], axis=1)
    o_ref[...] = _layer_norm(ALPHA * x_ref[...] + ff, g_ref[...], b_ref[...])


def _combine(x, yg, gates, g, b):
    n, d = x.shape
    tiles = n // TM
    return pl.pallas_call(
        _combine_kernel,
        out_shape=jax.ShapeDtypeStruct((n, d), F32),
        grid=(tiles,),
        in_specs=[pl.BlockSpec((TM, d), lambda i: (i, 0)),
                  pl.BlockSpec((TM, d // 2), lambda i: (i, 0)),
                  pl.BlockSpec((TM, d // 2), lambda i: (i + tiles, 0)),
                  pl.BlockSpec((TM, TOP_K), lambda i: (i, 0)),
                  _const_spec((1, d)), _const_spec((1, d))],
        out_specs=pl.BlockSpec((TM, d), lambda i: (i, 0)),
        compiler_params=_params(),
        name="combine",
    )(x, yg, yg, gates, g.reshape(1, -1), b.reshape(1, -1))


def _moe(x, x_packed, layer, wr_g, br_g, wr_e, br_e, w1, w3, w2, g, b):
    n, d = x.shape
    n_blocks = (n * TOP_K + N_EXPERTS * (EXPERT_BLOCK - 1) + EXPERT_BLOCK - 1) // EXPERT_BLOCK
    ri, gates, counts = _router(x, wr_g, br_g, wr_e, br_e)

    blocks_e = (counts + EXPERT_BLOCK - 1) // EXPERT_BLOCK
    blocks_end = jnp.cumsum(blocks_e)
    run_start = (blocks_end - blocks_e) * EXPERT_BLOCK
    n_used = blocks_end[-1:].astype(I32)
    block_ids = jnp.arange(n_blocks, dtype=I32)
    block_expert = jnp.minimum(jnp.sum(blocks_end[None, :] <= block_ids[:, None], axis=1),
                               N_EXPERTS - 1).astype(I32)
    expert_ids = jnp.arange(N_EXPERTS, dtype=I32)
    start_of = jnp.sum(jnp.where(ri[:, :TOP_K, None] == expert_ids, run_start, 0), axis=-1)
    dest = (start_of + ri[:, TOP_K:]).astype(I32)
    row_token = _sc_row_tokens(dest.reshape(-1), n_blocks * EXPERT_BLOCK, n)

    xs = _sc_gather(x_packed, row_token)
    y = _expert_mlp(xs, block_expert, n_used, layer, w1, w3, w2, n_blocks)
    yg = _sc_gather(y, dest.T.reshape(-1))
    return _combine(x, yg, gates, g, b)


def kernel(x, mem, w_in_even, w_pool, pool_scale, ln_v_g, ln_v_b, w_spatial, b_spatial, w_out_even,
           w_in_odd, conv_w, conv_b, w_out_odd, wq_x, wk_x, wv_x, wo_x, ln_g, ln_b, wr_group,
           br_group, wr_expert, br_expert, w1, w3, w2):
    bsz, seq, d = x.shape
    assert seq % TM == 0 and d % LANES == 0
    mlen = mem.shape[1]
    k_all, v_all = _memory_kv(mem.reshape(bsz * mlen, d), wk_x, wv_x)
    k_all = k_all.reshape(DEPTH, bsz, mlen, d)
    v_all = v_all.reshape(DEPTH, bsz, mlen, d)
    h = x.reshape(bsz * seq, d)
    for l in range(DEPTH):
        i = l // 2
        if l % 2 == 0:
            h = _even_mixer(h, seq, w_in_even[i], w_pool[i], pool_scale[i], ln_v_g[i], ln_v_b[i],
                            w_spatial[i], b_spatial[i], w_out_even[i], ln_g[l, 0], ln_b[l, 0])
        else:
            h = _odd_mixer(h, seq, w_in_odd[i], conv_w[i], conv_b[i], w_out_odd[i],
                           ln_g[l, 0], ln_b[l, 0])
        h, hp = _cross_attn(h, seq, k_all[l], v_all[l], wq_x[l], wo_x[l], ln_g[l, 1], ln_b[l, 1])
        h = _moe(h, hp, l, wr_group[l], br_group[l], wr_expert[l], br_expert[l], w1, w3, w2,
                 ln_g[l, 2], ln_b[l, 2])
    return h.reshape(bsz, seq, d)
```

```python
import functools
import math

import jax
import jax.numpy as jnp
from jax import lax
from jax.experimental import pallas as pl
from jax.experimental.pallas import tpu as pltpu
from jax.experimental.pallas import tpu_sc as plsc

F32 = jnp.float32
BF16 = jnp.bfloat16
I32 = jnp.int32

POOL_WINDOWS = (2, 4, 8, 16)
N_SG_HEADS = 4
CHUNK = 128
CONV_WIDTH = 3
N_XHEADS = 4
N_GROUPS = 4
EXPERTS_PER_GROUP = 8
N_EXPERTS = N_GROUPS * EXPERTS_PER_GROUP
TOP_K = 2
DEPTH = 4
ALPHA = (2.0 * DEPTH) ** 0.25
LN_EPS = 1e-5

LANES = 128
SC_CORES = 2
SC_WORKERS = 32
SC_LANES = 16
SC_CHUNK = 64
TM = 1024
SUB_TILES = 2
ROUTER_TM = 512
POOL_HALO = 16
CONV_HALO = 8
EXPERT_BLOCK = 512
ROUTE_COLS = 128
VMEM_LIMIT = 48 * 1024 * 1024

_NT = (((1,), (1,)), ((), ()))


def _dot(a, b):
    return jnp.dot(a, b, preferred_element_type=F32)


def _layer_norm(y, g, b):
    mu = jnp.mean(y, axis=-1, keepdims=True)
    yc = y - mu
    var = jnp.mean(yc * yc, axis=-1, keepdims=True)
    return yc * lax.rsqrt(var + LN_EPS) * g + b


def _gelu_tanh(x):
    c = math.sqrt(2.0 / math.pi)
    return 0.5 * x * (1.0 + jnp.tanh(c * (x + 0.044715 * (x * x * x))))


def _pack_halves(v):
    c = v.shape[1] // 2
    lo = pltpu.bitcast(v[:, :c].astype(BF16).astype(F32), jnp.uint32)
    hi = pltpu.bitcast(v[:, c:].astype(BF16).astype(F32), jnp.uint32)
    return pltpu.bitcast((hi & jnp.uint32(0xFFFF0000)) | (lo >> 16), I32)


def _unpack_halves(w):
    u = pltpu.bitcast(w, jnp.uint32)
    return pltpu.bitcast(u << 16, F32), pltpu.bitcast(u & jnp.uint32(0xFFFF0000), F32)


def _const_spec(shape):
    nd = len(shape)
    return pl.BlockSpec(shape, lambda i: (0,) * nd)


def _params():
    return pltpu.CompilerParams(dimension_semantics=("arbitrary",), vmem_limit_bytes=VMEM_LIMIT)


def _even_kernel(x_ref, xh_ref, win_ref, wpool_ref, pscale_ref, lvg_ref, lvb_ref, ws_ref, bst_ref,
                 wout_ref, g_ref, b_ref, o_ref, a_scr, cat_scr, *, tiles_per_seq):
    tm = x_ref.shape[0]
    d_pool = a_scr.shape[1]
    d_sg = lvg_ref.shape[1]
    pgd = d_pool // len(POOL_WINDOWS)
    hd_dim = d_sg // N_SG_HEADS
    seq_tile = pl.program_id(0) % tiles_per_seq

    x = x_ref[...]
    h = _dot(x.astype(BF16), win_ref[...])

    ah = _dot(xh_ref[...].astype(BF16), win_ref[:, :d_pool])
    a_scr[0:POOL_HALO, :] = jnp.where(seq_tile == 0, 0.0, ah)
    a_scr[POOL_HALO:POOL_HALO + tm, :] = h[:, :d_pool]
    pos = seq_tile * tm + lax.broadcasted_iota(I32, (tm, 1), 0)
    for g, w in enumerate(POOL_WINDOWS):
        cs = slice(g * pgd, (g + 1) * pgd)
        tok = a_scr[POOL_HALO:POOL_HALO + tm, cs]
        acc = tok
        for j in range(1, w):
            acc = acc + a_scr[POOL_HALO - j:POOL_HALO - j + tm, cs]
        cnt = jnp.minimum(pos + 1, w).astype(F32)
        d = acc / cnt - tok
        yg = _dot(d.astype(BF16), wpool_ref[g])
        cat_scr[:, cs] = (yg * pscale_ref[:, cs]).astype(BF16)

    z = _gelu_tanh(h[:, d_pool:])
    u = z[:, :d_sg]
    v = _layer_norm(z[:, d_sg:], lvg_ref[...], lvb_ref[...]).astype(BF16)
    row = lax.broadcasted_iota(I32, (CHUNK, CHUNK), 0)
    col = lax.broadcasted_iota(I32, (CHUNK, CHUNK), 1)
    for hd in range(N_SG_HEADS):
        hs = slice(hd * hd_dim, (hd + 1) * hd_dim)
        wsm = jnp.where(row >= col, ws_ref[hd], 0.0).astype(BF16)
        bcol = bst_ref[:, hd:hd + 1]
        for ck in range(tm // CHUNK):
            rs = slice(ck * CHUNK, (ck + 1) * CHUNK)
            sv = _dot(wsm, v[rs, hs]) + bcol
            cat_scr[rs, d_pool + hd * hd_dim:d_pool + (hd + 1) * hd_dim] = (u[rs, hs] * sv).astype(BF16)

    mix = _dot(cat_scr[...], wout_ref[...])
    o_ref[...] = _layer_norm(ALPHA * x + mix, g_ref[...], b_ref[...])


def _even_mixer(x, seq, w_in, w_pool, pool_scale, ln_v_g, ln_v_b, w_spatial, b_spatial, w_out, g, b):
    n, d = x.shape
    d_in = w_in.shape[1]
    d_pool = pool_scale.shape[0]
    d_sg = ln_v_g.shape[0]
    kern = functools.partial(_even_kernel, tiles_per_seq=seq // TM)
    halo_blocks = TM // POOL_HALO
    return pl.pallas_call(
        kern,
        out_shape=jax.ShapeDtypeStruct((n, d), F32),
        grid=(n // TM,),
        in_specs=[
            pl.BlockSpec((TM, d), lambda i: (i, 0)),
            pl.BlockSpec((POOL_HALO, d), lambda i: (jnp.maximum(i * halo_blocks - 1, 0), 0)),
            _const_spec((d, d_in)),
            _const_spec(w_pool.shape),
            _const_spec((1, d_pool)),
            _const_spec((1, d_sg)),
            _const_spec((1, d_sg)),
            _const_spec(w_spatial.shape),
            _const_spec((CHUNK, N_SG_HEADS)),
            _const_spec(w_out.shape),
            _const_spec((1, d)),
            _const_spec((1, d)),
        ],
        out_specs=pl.BlockSpec((TM, d), lambda i: (i, 0)),
        scratch_shapes=[pltpu.VMEM((POOL_HALO + TM, d_pool), F32), pltpu.VMEM((TM, d_pool + d_sg), BF16)],
        compiler_params=_params(),
        name="even_mixer",
    )(x, x, w_in.astype(BF16), w_pool.astype(BF16), pool_scale.reshape(1, -1), ln_v_g.reshape(1, -1),
      ln_v_b.reshape(1, -1), w_spatial, b_spatial.T, w_out.astype(BF16), g.reshape(1, -1), b.reshape(1, -1))


def _odd_kernel(x_ref, xh_ref, win_ref, cwt_ref, cb_ref, wout_ref, g_ref, b_ref, o_ref, zc_scr,
                *, tiles_per_seq):
    tm, d = x_ref.shape
    seq_tile = pl.program_id(0) % tiles_per_seq

    x = x_ref[...]
    h = _dot(x.astype(BF16), win_ref[...])
    hh = _dot(xh_ref[...].astype(BF16), win_ref[:, d:])
    zc_scr[0:CONV_HALO, :] = jnp.where(seq_tile == 0, 0.0, hh[:, :d] * hh[:, d:])
    zc_scr[CONV_HALO:CONV_HALO + tm, :] = h[:, d:2 * d] * h[:, 2 * d:]
    conv = cb_ref[...]
    for j in range(CONV_WIDTH):
        off = CONV_HALO - (CONV_WIDTH - 1) + j
        conv = conv + zc_scr[off:off + tm, :] * cwt_ref[j:j + 1, :]
    y = _dot((h[:, :d] * conv).astype(BF16), wout_ref[...])
    o_ref[...] = _layer_norm(ALPHA * x + y, g_ref[...], b_ref[...])


def _odd_mixer(x, seq, w_in, conv_w, conv_b, w_out, g, b):
    n, d = x.shape
    kern = functools.partial(_odd_kernel, tiles_per_seq=seq // TM)
    halo_blocks = TM // CONV_HALO
    return pl.pallas_call(
        kern,
        out_shape=jax.ShapeDtypeStruct((n, d), F32),
        grid=(n // TM,),
        in_specs=[
            pl.BlockSpec((TM, d), lambda i: (i, 0)),
            pl.BlockSpec((CONV_HALO, d), lambda i: (jnp.maximum(i * halo_blocks - 1, 0), 0)),
            _const_spec(w_in.shape),
            _const_spec((CONV_WIDTH, d)),
            _const_spec((1, d)),
            _const_spec(w_out.shape),
            _const_spec((1, d)),
            _const_spec((1, d)),
        ],
        out_specs=pl.BlockSpec((TM, d), lambda i: (i, 0)),
        scratch_shapes=[pltpu.VMEM((CONV_HALO + TM, d), F32)],
        compiler_params=_params(),
        name="odd_mixer",
    )(x, x, w_in.astype(BF16), conv_w.T, conv_b.reshape(1, -1), w_out.astype(BF16),
      g.reshape(1, -1), b.reshape(1, -1))


def _kv_kernel(mem_ref, wk_ref, wv_ref, k_ref, v_ref):
    m = mem_ref[...].astype(BF16)
    k_ref[...] = _dot(m, wk_ref[...]).astype(BF16)
    v_ref[...] = _dot(m, wv_ref[...]).astype(BF16)


def _memory_kv(mem2d, wk, wv):
    nl, d, _ = wk.shape
    rows = mem2d.shape[0]
    out = jax.ShapeDtypeStruct((nl, rows, d), BF16)
    wspec = pl.BlockSpec((None, d, d), lambda l: (l, 0, 0))
    ospec = pl.BlockSpec((None, rows, d), lambda l: (l, 0, 0))
    return pl.pallas_call(
        _kv_kernel,
        out_shape=(out, out),
        grid=(nl,),
        in_specs=[_const_spec((rows, d)), wspec, wspec],
        out_specs=(ospec, ospec),
        compiler_params=_params(),
        name="memory_kv",
    )(mem2d, wk.astype(BF16), wv.astype(BF16))


def _attn_kernel(x_ref, k_ref, v_ref, wq_ref, wo_ref, g_ref, b_ref, o_ref, op_ref, o_scr):
    tm, d = x_ref.shape
    hd_dim = d // N_XHEADS
    sub = tm // SUB_TILES
    for st in range(SUB_TILES):
        rs = slice(st * sub, (st + 1) * sub)
        x = x_ref[rs, :]
        q = _dot(x.astype(BF16), wq_ref[...]) * (1.0 / math.sqrt(hd_dim))
        for hd in range(N_XHEADS):
            hs = slice(hd * hd_dim, (hd + 1) * hd_dim)
            s = lax.dot_general(q[:, hs].astype(BF16), k_ref[:, hs], _NT, preferred_element_type=F32)
            p = jnp.exp(s - jnp.max(s, axis=-1, keepdims=True))
            p = p / jnp.sum(p, axis=-1, keepdims=True)
            o_scr[rs, hs] = _dot(p.astype(BF16), v_ref[:, hs]).astype(BF16)
        xa = _dot(o_scr[rs, :], wo_ref[...])
        out = _layer_norm(ALPHA * x + xa, g_ref[...], b_ref[...])
        o_ref[rs, :] = out
        op_ref[rs, :] = _pack_halves(out)


def _cross_attn(x, seq, k, v, wq, wo, g, b):
    n, d = x.shape
    m = k.shape[1]
    tiles_per_seq = seq // TM
    kvspec = pl.BlockSpec((None, m, d), lambda i: (i // tiles_per_seq, 0, 0))
    return pl.pallas_call(
        _attn_kernel,
        out_shape=(jax.ShapeDtypeStruct((n, d), F32), jax.ShapeDtypeStruct((n, d // 2), I32)),
        grid=(n // TM,),
        in_specs=[
            pl.BlockSpec((TM, d), lambda i: (i, 0)),
            kvspec, kvspec,
            _const_spec((d, d)), _const_spec((d, d)),
            _const_spec((1, d)), _const_spec((1, d)),
        ],
        out_specs=(pl.BlockSpec((TM, d), lambda i: (i, 0)), pl.BlockSpec((TM, d // 2), lambda i: (i, 0))),
        scratch_shapes=[pltpu.VMEM((TM, d), BF16)],
        compiler_params=_params(),
        name="cross_attn",
    )(x, k, v, wq.astype(BF16), wo.astype(BF16), g.reshape(1, -1), b.reshape(1, -1))


def _router_kernel(x_ref, wcat_ref, bias_ref, ri_ref, rg_ref, cnt_ref, carry_scr):
    tm = x_ref.shape[0]
    step = pl.program_id(0)

    @pl.when(step == 0)
    def _():
        carry_scr[...] = jnp.zeros_like(carry_scr)

    x = x_ref[...]
    xh = x.astype(BF16)
    xl = (x - xh.astype(F32)).astype(BF16)
    r1 = _dot(xh, wcat_ref[...])
    r2 = _dot(xl, wcat_ref[:, :ROUTE_COLS])
    logits = r1[:, :ROUTE_COLS] + r1[:, ROUTE_COLS:] + r2 + bias_ref[...]

    lane = lax.broadcasted_iota(I32, (tm, ROUTE_COLS), 1).astype(F32)
    neg = -jnp.inf

    def first_argmax(vals):
        mx = jnp.max(vals, axis=-1, keepdims=True)
        idx = jnp.min(jnp.where(vals == mx, lane, float(ROUTE_COLS)), axis=-1, keepdims=True)
        return mx, idx

    gl = jnp.where(lane < N_GROUPS, logits, neg)
    gmax, g_sel = first_argmax(gl)
    gate_g = 1.0 / jnp.sum(jnp.exp(gl - gmax), axis=-1, keepdims=True)

    lo = N_GROUPS + g_sel * EXPERTS_PER_GROUP
    el = jnp.where((lane >= lo) & (lane < lo + EXPERTS_PER_GROUP), logits, neg)
    m1, i1 = first_argmax(el)
    m2, i2 = first_argmax(jnp.where(lane == i1, neg, el))
    e21 = jnp.exp(m2 - m1)
    w1 = 1.0 / (1.0 + e21)
    w2 = e21 / (1.0 + e21)

    oh1 = lane == i1
    oh2 = lane == i2
    oh = (oh1 | oh2).astype(BF16)
    r = lax.broadcasted_iota(I32, (tm, tm), 0)
    c = lax.broadcasted_iota(I32, (tm, tm), 1)
    before = _dot((r > c).astype(BF16), oh) + carry_scr[...]
    rank1 = jnp.sum(jnp.where(oh1, before, 0.0), axis=-1, keepdims=True)
    rank2 = jnp.sum(jnp.where(oh2, before, 0.0), axis=-1, keepdims=True)
    carry_scr[...] += jnp.sum(oh.astype(F32), axis=0, keepdims=True)

    l4 = lax.broadcasted_iota(I32, ri_ref.shape, 1)
    ri_ref[...] = jnp.where(l4 == 0, i1 - N_GROUPS,
                            jnp.where(l4 == 1, i2 - N_GROUPS,
                                      jnp.where(l4 == 2, rank1, rank2))).astype(I32)
    l2 = lax.broadcasted_iota(I32, rg_ref.shape, 1)
    rg_ref[...] = jnp.where(l2 == 0, gate_g * w1, gate_g * w2)
    cnt_ref[...] = carry_scr[...].astype(I32)


def _router(x, wr_g, br_g, wr_e, br_e):
    n, d = x.shape
    w = jnp.concatenate([wr_g, jnp.transpose(wr_e, (1, 0, 2)).reshape(d, N_EXPERTS)], axis=1)
    w = jnp.pad(w, ((0, 0), (0, ROUTE_COLS - w.shape[1])))
    w_hi = w.astype(BF16)
    w_lo = (w - w_hi.astype(F32)).astype(BF16)
    bias = jnp.pad(jnp.concatenate([br_g, br_e.reshape(-1)]), (0, ROUTE_COLS - N_GROUPS - N_EXPERTS))
    ri, rg, cnt = pl.pallas_call(
        _router_kernel,
        out_shape=(jax.ShapeDtypeStruct((n, 4), I32), jax.ShapeDtypeStruct((n, TOP_K), F32),
                   jax.ShapeDtypeStruct((1, ROUTE_COLS), I32)),
        grid=(n // ROUTER_TM,),
        in_specs=[pl.BlockSpec((ROUTER_TM, d), lambda i: (i, 0)),
                  _const_spec((d, 2 * ROUTE_COLS)), _const_spec((1, ROUTE_COLS))],
        out_specs=(pl.BlockSpec((ROUTER_TM, 4), lambda i: (i, 0)),
                   pl.BlockSpec((ROUTER_TM, TOP_K), lambda i: (i, 0)),
                   _const_spec((1, ROUTE_COLS))),
        scratch_shapes=[pltpu.VMEM((1, ROUTE_COLS), F32)],
        compiler_params=_params(),
        name="router",
    )(x, jnp.concatenate([w_hi, w_lo], axis=1), bias.reshape(1, -1))
    return ri, rg, cnt[0, N_GROUPS:N_GROUPS + N_EXPERTS]


def _sc_gather(table, idx):
    b = idx.shape[0]
    d = table.shape[1]
    per_worker = b // SC_WORKERS
    n_chunks = per_worker // SC_CHUNK
    assert per_worker * SC_WORKERS == b and n_chunks * SC_CHUNK == per_worker and n_chunks % 2 == 0
    mesh = plsc.VectorSubcoreMesh(core_axis_name="c", subcore_axis_name="s")

    @functools.partial(
        pl.kernel, mesh=mesh,
        out_type=jax.ShapeDtypeStruct((b, d), table.dtype),
        scratch_types=[pltpu.VMEM((per_worker,), I32),
                       pltpu.VMEM((2, SC_CHUNK, d), table.dtype),
                       pltpu.SemaphoreType.DMA((2,)),
                       pltpu.SemaphoreType.DMA((2,))],
        name="sc_gather",
    )
    def gather(table_hbm, idx_hbm, out_hbm, idx_v, rows_v, gsem, wsem):
        wid = lax.axis_index("s") * SC_CORES + lax.axis_index("c")
        base = wid * per_worker
        pltpu.sync_copy(idx_hbm.at[pl.ds(base, per_worker)], idx_v)

        def fetch(c, slot):
            off = pl.multiple_of(c * SC_CHUNK, SC_CHUNK)
            return pltpu.make_async_copy(table_hbm.at[idx_v.at[pl.ds(off, SC_CHUNK)]], rows_v.at[slot],
                                         gsem.at[slot])

        def put(c, slot):
            off = pl.multiple_of(c * SC_CHUNK, SC_CHUNK)
            return pltpu.make_async_copy(rows_v.at[slot], out_hbm.at[pl.ds(base + off, SC_CHUNK)],
                                         wsem.at[slot])

        fetch(0, 0).start()

        @pl.loop(0, n_chunks, step=2)
        def _(c0):
            for slot in range(2):
                c = c0 + slot

                @pl.when(c + 1 < n_chunks)
                def _():
                    @pl.when(c >= 1)
                    def _():
                        put(c - 1, 1 - slot).wait()
                    fetch(c + 1, 1 - slot).start()

                fetch(c, slot).wait()
                put(c, slot).start()

        put(n_chunks - 2, 0).wait()
        put(n_chunks - 1, 1).wait()

    return gather(table, idx)


def _sc_row_tokens(dest_flat, rows, n):
    a = dest_flat.shape[0]
    lanes = SC_LANES
    assert a % lanes == 0 and rows % lanes == 0
    mesh = plsc.VectorSubcoreMesh(core_axis_name="c", subcore_axis_name="s")

    @functools.partial(
        pl.kernel, mesh=mesh,
        out_type=jax.ShapeDtypeStruct((rows,), I32),
        scratch_types=[pltpu.VMEM((a,), I32), pltpu.VMEM((rows,), I32)],
        compiler_params=pltpu.CompilerParams(needs_layout_passes=False),
        name="sc_row_tokens",
    )
    def invert(dest_hbm, out_hbm, dest_v, map_v):
        @pl.when((lax.axis_index("s") == 0) & (lax.axis_index("c") == 0))
        def _():
            pltpu.sync_copy(dest_hbm, dest_v)
            lane = lax.iota(I32, lanes)

            @pl.loop(0, rows // lanes)
            def _(i):
                map_v[pl.ds(i * lanes, lanes)] = lax.rem(i * lanes + lane, n)

            @pl.loop(0, a // lanes)
            def _(i):
                plsc.store_scatter(map_v, [dest_v[pl.ds(i * lanes, lanes)]], (i * lanes + lane) // TOP_K)

            pltpu.sync_copy(map_v, out_hbm)

    return invert(dest_flat)


def _expert_kernel(be_ref, nb_ref, xs_ref, w1_ref, w3_ref, w2_ref, y_ref, w1_scr, w3_scr, w2_scr):
    b = pl.program_id(0)
    used = b < nb_ref[0]

    @pl.when(used & ((b == 0) | (be_ref[b] != be_ref[jnp.maximum(b - 1, 0)])))
    def _():
        w1_scr[...] = w1_ref[...].astype(BF16)
        w3_scr[...] = w3_ref[...].astype(BF16)
        w2_scr[...] = w2_ref[...].astype(BF16)

    @pl.when(used)
    def _():
        x_lo, x_hi = _unpack_halves(xs_ref[...])
        xb = jnp.concatenate([x_lo.astype(BF16), x_hi.astype(BF16)], axis=1)
        h1 = _dot(xb, w1_scr[...])
        h3 = _dot(xb, w3_scr[...])
        hid = h1 * (1.0 / (1.0 + jnp.exp(-h1))) * h3
        y_ref[...] = _pack_halves(_dot(hid.astype(BF16), w2_scr[...]))

    @pl.when(jnp.logical_not(used))
    def _():
        y_ref[...] = jnp.zeros_like(y_ref)


def _expert_mlp(xs, block_expert, n_used, layer, w1, w3, w2, n_blocks):
    d, de = w1.shape[2], w1.shape[3]

    def row_map(b, be, nb):
        return (jnp.minimum(b, nb[0] - 1), 0)

    def w_map(b, be, nb):
        return (layer, be[jnp.minimum(b, nb[0] - 1)], 0, 0)

    return pl.pallas_call(
        _expert_kernel,
        out_shape=jax.ShapeDtypeStruct((n_blocks * EXPERT_BLOCK, d // 2), I32),
        grid_spec=pltpu.PrefetchScalarGridSpec(
            num_scalar_prefetch=2,
            grid=(n_blocks,),
            in_specs=[pl.BlockSpec((EXPERT_BLOCK, d // 2), row_map),
                      pl.BlockSpec((None, None, d, de), w_map),
                      pl.BlockSpec((None, None, d, de), w_map),
                      pl.BlockSpec((None, None, de, d), w_map)],
            out_specs=pl.BlockSpec((EXPERT_BLOCK, d // 2), lambda b, be, nb: (b, 0)),
            scratch_shapes=[pltpu.VMEM((d, de), BF16), pltpu.VMEM((d, de), BF16), pltpu.VMEM((de, d), BF16)],
        ),
        compiler_params=_params(),
        name="expert_mlp",
    )(block_expert, n_used, xs, w1, w3, w2)


def _combine_kernel(x_ref, y0_ref, y1_ref, gate_ref, g_ref, b_ref, o_ref):
    gates = gate_ref[...]
    g0, g1 = gates[:, 0:1], gates[:, 1:2]
    y0_lo, y0_hi = _unpack_halves(y0_ref[...])
    y1_lo, y1_hi = _unpack_halves(y1_ref[...])
    ff = jnp.concatenate([g0 * y0_lo + g1 * y1_lo, g0 * y0_hi + g1 * y1_hi], axis=1)
    o_ref[...] = _layer_norm(ALPHA * x_ref[...] + ff, g_ref[...], b_ref[...])


def _combine(x, yg, gates, g, b):
    n, d = x.shape
    tiles = n // TM
    return pl.pallas_call(
        _combine_kernel,
        out_shape=jax.ShapeDtypeStruct((n, d), F32),
        grid=(tiles,),
        in_specs=[pl.BlockSpec((TM, d), lambda i: (i, 0)),
                  pl.BlockSpec((TM, d // 2), lambda i: (i, 0)),
                  pl.BlockSpec((TM, d // 2), lambda i: (i + tiles, 0)),
                  pl.BlockSpec((TM, TOP_K), lambda i: (i, 0)),
                  _const_spec((1, d)), _const_spec((1, d))],
        out_specs=pl.BlockSpec((TM, d), lambda i: (i, 0)),
        compiler_params=_params(),
        name="combine",
    )(x, yg, yg, gates, g.reshape(1, -1), b.reshape(1, -1))


def _moe(x, x_packed, layer, wr_g, br_g, wr_e, br_e, w1, w3, w2, g, b):
    n, d = x.shape
    n_blocks = (n * TOP_K + N_EXPERTS * (EXPERT_BLOCK - 1) + EXPERT_BLOCK - 1) // EXPERT_BLOCK
    ri, gates, counts = _router(x, wr_g, br_g, wr_e, br_e)

    blocks_e = (counts + EXPERT_BLOCK - 1) // EXPERT_BLOCK
    blocks_end = jnp.cumsum(blocks_e)
    run_start = (blocks_end - blocks_e) * EXPERT_BLOCK
    n_used = blocks_end[-1:].astype(I32)
    block_ids = jnp.arange(n_blocks, dtype=I32)
    block_expert = jnp.minimum(jnp.sum(blocks_end[None, :] <= block_ids[:, None], axis=1),
                               N_EXPERTS - 1).astype(I32)
    expert_ids = jnp.arange(N_EXPERTS, dtype=I32)
    start_of = jnp.sum(jnp.where(ri[:, :TOP_K, None] == expert_ids, run_start, 0), axis=-1)
    dest = (start_of + ri[:, TOP_K:]).astype(I32)
    row_token = _sc_row_tokens(dest.reshape(-1), n_blocks * EXPERT_BLOCK, n)

    xs = _sc_gather(x_packed, row_token)
    y = _expert_mlp(xs, block_expert, n_used, layer, w1, w3, w2, n_blocks)
    yg = _sc_gather(y, dest.T.reshape(-1))
    return _combine(x, yg, gates, g, b)


def kernel(x, mem, w_in_even, w_pool, pool_scale, ln_v_g, ln_v_b, w_spatial, b_spatial, w_out_even,
           w_in_odd, conv_w, conv_b, w_out_odd, wq_x, wk_x, wv_x, wo_x, ln_g, ln_b, wr_group,
           br_group, wr_expert, br_expert, w1, w3, w2):
    bsz, seq, d = x.shape
    assert seq % TM == 0 and d % LANES == 0
    mlen = mem.shape[1]
    k_all, v_all = _memory_kv(mem.reshape(bsz * mlen, d), wk_x, wv_x)
    k_all = k_all.reshape(DEPTH, bsz, mlen, d)
    v_all = v_all.reshape(DEPTH, bsz, mlen, d)
    h = x.reshape(bsz * seq, d)
    for l in range(DEPTH):
        i = l // 2
        if l % 2 == 0:
            h = _even_mixer(h, seq, w_in_even[i], w_pool[i], pool_scale[i], ln_v_g[i], ln_v_b[i],
                            w_spatial[i], b_spatial[i], w_out_even[i], ln_g[l, 0], ln_b[l, 0])
        else:
            h = _odd_mixer(h, seq, w_in_odd[i], conv_w[i], conv_b[i], w_out_odd[i],
                           ln_g[l, 0], ln_b[l, 0])
        h, hp = _cross_attn(h, seq, k_all[l], v_all[l], wq_x[l], wo_x[l], ln_g[l, 1], ln_b[l, 1])
        h = _moe(h, hp, l, wr_group[l], br_group[l], wr_expert[l], br_expert[l], w1, w3, w2,
                 ln_g[l, 2], ln_b[l, 2])
    return h.reshape(bsz, seq, d)
```

```python
import functools
import math

import jax
import jax.numpy as jnp
from jax import lax
from jax.experimental import pallas as pl
from jax.experimental.pallas import tpu as pltpu
from jax.experimental.pallas import tpu_sc as plsc

F32 = jnp.float32
BF16 = jnp.bfloat16
I32 = jnp.int32

POOL_WINDOWS = (2, 4, 8, 16)
N_SG_HEADS = 4
CHUNK = 128
CONV_WIDTH = 3
N_XHEADS = 4
N_GROUPS = 4
EXPERTS_PER_GROUP = 8
N_EXPERTS = N_GROUPS * EXPERTS_PER_GROUP
TOP_K = 2
DEPTH = 4
ALPHA = (2.0 * DEPTH) ** 0.25
LN_EPS = 1e-5

LANES = 128
SC_CORES = 2
SC_WORKERS = 32
SC_LANES = 16
SC_CHUNK = 64
TM = 1024
SUB_TILES = 2
ROUTER_TM = 512
POOL_HALO = 16
CONV_HALO = 8
EXPERT_BLOCK = 512
ROUTE_COLS = 128
VMEM_LIMIT = 48 * 1024 * 1024

_NT = (((1,), (1,)), ((), ()))


def _dot(a, b):
    return jnp.dot(a, b, preferred_element_type=F32)


def _layer_norm(y, g, b):
    mu = jnp.mean(y, axis=-1, keepdims=True)
    yc = y - mu
    var = jnp.mean(yc * yc, axis=-1, keepdims=True)
    return yc * lax.rsqrt(var + LN_EPS) * g + b


def _gelu_tanh(x):
    c = math.sqrt(2.0 / math.pi)
    return 0.5 * x * (1.0 + jnp.tanh(c * (x + 0.044715 * (x * x * x))))


def _pack_halves(v):
    c = v.shape[1] // 2
    lo = pltpu.bitcast(v[:, :c].astype(BF16).astype(F32), jnp.uint32)
    hi = pltpu.bitcast(v[:, c:].astype(BF16).astype(F32), jnp.uint32)
    return pltpu.bitcast((hi & jnp.uint32(0xFFFF0000)) | (lo >> 16), I32)


def _unpack_halves(w):
    u = pltpu.bitcast(w, jnp.uint32)
    return pltpu.bitcast(u << 16, F32), pltpu.bitcast(u & jnp.uint32(0xFFFF0000), F32)


def _const_spec(shape):
    nd = len(shape)
    return pl.BlockSpec(shape, lambda i: (0,) * nd)


def _params():
    return pltpu.CompilerParams(dimension_semantics=("arbitrary",), vmem_limit_bytes=VMEM_LIMIT)


def _even_kernel(x_ref, xh_ref, win_ref, wpool_ref, pscale_ref, lvg_ref, lvb_ref, ws_ref, bst_ref,
                 wout_ref, g_ref, b_ref, o_ref, a_scr, cat_scr, *, tiles_per_seq):
    tm = x_ref.shape[0]
    d_pool = a_scr.shape[1]
    d_sg = lvg_ref.shape[1]
    pgd = d_pool // len(POOL_WINDOWS)
    hd_dim = d_sg // N_SG_HEADS
    seq_tile = pl.program_id(0) % tiles_per_seq

    x = x_ref[...]
    h = _dot(x.astype(BF16), win_ref[...])

    ah = _dot(xh_ref[...].astype(BF16), win_ref[:, :d_pool])
    a_scr[0:POOL_HALO, :] = jnp.where(seq_tile == 0, 0.0, ah)
    a_scr[POOL_HALO:POOL_HALO + tm, :] = h[:, :d_pool]
    pos = seq_tile * tm + lax.broadcasted_iota(I32, (tm, 1), 0)
    for g, w in enumerate(POOL_WINDOWS):
        cs = slice(g * pgd, (g + 1) * pgd)
        tok = a_scr[POOL_HALO:POOL_HALO + tm, cs]
        acc = tok
        for j in range(1, w):
            acc = acc + a_scr[POOL_HALO - j:POOL_HALO - j + tm, cs]
        cnt = jnp.minimum(pos + 1, w).astype(F32)
        d = acc / cnt - tok
        yg = _dot(d.astype(BF16), wpool_ref[g])
        cat_scr[:, cs] = (yg * pscale_ref[:, cs]).astype(BF16)

    z = _gelu_tanh(h[:, d_pool:])
    u = z[:, :d_sg]
    v = _layer_norm(z[:, d_sg:], lvg_ref[...], lvb_ref[...]).astype(BF16)
    row = lax.broadcasted_iota(I32, (CHUNK, CHUNK), 0)
    col = lax.broadcasted_iota(I32, (CHUNK, CHUNK), 1)
    for hd in range(N_SG_HEADS):
        hs = slice(hd * hd_dim, (hd + 1) * hd_dim)
        wsm = jnp.where(row >= col, ws_ref[hd], 0.0).astype(BF16)
        bcol = bst_ref[:, hd:hd + 1]
        for ck in range(tm // CHUNK):
            rs = slice(ck * CHUNK, (ck + 1) * CHUNK)
            sv = _dot(wsm, v[rs, hs]) + bcol
            cat_scr[rs, d_pool + hd * hd_dim:d_pool + (hd + 1) * hd_dim] = (u[rs, hs] * sv).astype(BF16)

    mix = _dot(cat_scr[...], wout_ref[...])
    o_ref[...] = _layer_norm(ALPHA * x + mix, g_ref[...], b_ref[...])


def _even_mixer(x, seq, w_in, w_pool, pool_scale, ln_v_g, ln_v_b, w_spatial, b_spatial, w_out, g, b):
    n, d = x.shape
    d_in = w_in.shape[1]
    d_pool = pool_scale.shape[0]
    d_sg = ln_v_g.shape[0]
    kern = functools.partial(_even_kernel, tiles_per_seq=seq // TM)
    halo_blocks = TM // POOL_HALO
    return pl.pallas_call(
        kern,
        out_shape=jax.ShapeDtypeStruct((n, d), F32),
        grid=(n // TM,),
        in_specs=[
            pl.BlockSpec((TM, d), lambda i: (i, 0)),
            pl.BlockSpec((POOL_HALO, d), lambda i: (jnp.maximum(i * halo_blocks - 1, 0), 0)),
            _const_spec((d, d_in)),
            _const_spec(w_pool.shape),
            _const_spec((1, d_pool)),
            _const_spec((1, d_sg)),
            _const_spec((1, d_sg)),
            _const_spec(w_spatial.shape),
            _const_spec((CHUNK, N_SG_HEADS)),
            _const_spec(w_out.shape),
            _const_spec((1, d)),
            _const_spec((1, d)),
        ],
        out_specs=pl.BlockSpec((TM, d), lambda i: (i, 0)),
        scratch_shapes=[pltpu.VMEM((POOL_HALO + TM, d_pool), F32), pltpu.VMEM((TM, d_pool + d_sg), BF16)],
        compiler_params=_params(),
        name="even_mixer",
    )(x, x, w_in.astype(BF16), w_pool.astype(BF16), pool_scale.reshape(1, -1), ln_v_g.reshape(1, -1),
      ln_v_b.reshape(1, -1), w_spatial, b_spatial.T, w_out.astype(BF16), g.reshape(1, -1), b.reshape(1, -1))


def _odd_kernel(x_ref, xh_ref, win_ref, cwt_ref, cb_ref, wout_ref, g_ref, b_ref, o_ref, zc_scr,
                *, tiles_per_seq):
    tm, d = x_ref.shape
    seq_tile = pl.program_id(0) % tiles_per_seq

    x = x_ref[...]
    h = _dot(x.astype(BF16), win_ref[...])
    hh = _dot(xh_ref[...].astype(BF16), win_ref[:, d:])
    zc_scr[0:CONV_HALO, :] = jnp.where(seq_tile == 0, 0.0, hh[:, :d] * hh[:, d:])
    zc_scr[CONV_HALO:CONV_HALO + tm, :] = h[:, d:2 * d] * h[:, 2 * d:]
    conv = cb_ref[...]
    for j in range(CONV_WIDTH):
        off = CONV_HALO - (CONV_WIDTH - 1) + j
        conv = conv + zc_scr[off:off + tm, :] * cwt_ref[j:j + 1, :]
    y = _dot((h[:, :d] * conv).astype(BF16), wout_ref[...])
    o_ref[...] = _layer_norm(ALPHA * x + y, g_ref[...], b_ref[...])


def _odd_mixer(x, seq, w_in, conv_w, conv_b, w_out, g, b):
    n, d = x.shape
    kern = functools.partial(_odd_kernel, tiles_per_seq=seq // TM)
    halo_blocks = TM // CONV_HALO
    return pl.pallas_call(
        kern,
        out_shape=jax.ShapeDtypeStruct((n, d), F32),
        grid=(n // TM,),
        in_specs=[
            pl.BlockSpec((TM, d), lambda i: (i, 0)),
            pl.BlockSpec((CONV_HALO, d), lambda i: (jnp.maximum(i * halo_blocks - 1, 0), 0)),
            _const_spec(w_in.shape),
            _const_spec((CONV_WIDTH, d)),
            _const_spec((1, d)),
            _const_spec(w_out.shape),
            _const_spec((1, d)),
            _const_spec((1, d)),
        ],
        out_specs=pl.BlockSpec((TM, d), lambda i: (i, 0)),
        scratch_shapes=[pltpu.VMEM((CONV_HALO + TM, d), F32)],
        compiler_params=_params(),
        name="odd_mixer",
    )(x, x, w_in.astype(BF16), conv_w.T, conv_b.reshape(1, -1), w_out.astype(BF16),
      g.reshape(1, -1), b.reshape(1, -1))


def _kv_kernel(mem_ref, wk_ref, wv_ref, k_ref, v_ref):
    m = mem_ref[...].astype(BF16)
    k_ref[...] = _dot(m, wk_ref[...]).astype(BF16)
    v_ref[...] = _dot(m, wv_ref[...]).astype(BF16)


def _memory_kv(mem2d, wk, wv):
    nl, d, _ = wk.shape
    rows = mem2d.shape[0]
    out = jax.ShapeDtypeStruct((nl, rows, d), BF16)
    wspec = pl.BlockSpec((None, d, d), lambda l: (l, 0, 0))
    ospec = pl.BlockSpec((None, rows, d), lambda l: (l, 0, 0))
    return pl.pallas_call(
        _kv_kernel,
        out_shape=(out, out),
        grid=(nl,),
        in_specs=[_const_spec((rows, d)), wspec, wspec],
        out_specs=(ospec, ospec),
        compiler_params=_params(),
        name="memory_kv",
    )(mem2d, wk.astype(BF16), wv.astype(BF16))


def _attn_kernel(x_ref, k_ref, v_ref, wq_ref, wo_ref, g_ref, b_ref, o_ref, op_ref, o_scr):
    tm, d = x_ref.shape
    hd_dim = d // N_XHEADS
    sub = tm // SUB_TILES
    for st in range(SUB_TILES):
        rs = slice(st * sub, (st + 1) * sub)
        x = x_ref[rs, :]
        q = _dot(x.astype(BF16), wq_ref[...]) * (1.0 / math.sqrt(hd_dim))
        for hd in range(N_XHEADS):
            hs = slice(hd * hd_dim, (hd + 1) * hd_dim)
            s = lax.dot_general(q[:, hs].astype(BF16), k_ref[:, hs], _NT, preferred_element_type=F32)
            p = jnp.exp(s - jnp.max(s, axis=-1, keepdims=True))
            p = p / jnp.sum(p, axis=-1, keepdims=True)
            o_scr[rs, hs] = _dot(p.astype(BF16), v_ref[:, hs]).astype(BF16)
        xa = _dot(o_scr[rs, :], wo_ref[...])
        out = _layer_norm(ALPHA * x + xa, g_ref[...], b_ref[...])
        o_ref[rs, :] = out
        op_ref[rs, :] = _pack_halves(out)


def _cross_attn(x, seq, k, v, wq, wo, g, b):
    n, d = x.shape
    m = k.shape[1]
    tiles_per_seq = seq // TM
    kvspec = pl.BlockSpec((None, m, d), lambda i: (i // tiles_per_seq, 0, 0))
    return pl.pallas_call(
        _attn_kernel,
        out_shape=(jax.ShapeDtypeStruct((n, d), F32), jax.ShapeDtypeStruct((n, d // 2), I32)),
        grid=(n // TM,),
        in_specs=[
            pl.BlockSpec((TM, d), lambda i: (i, 0)),
            kvspec, kvspec,
            _const_spec((d, d)), _const_spec((d, d)),
            _const_spec((1, d)), _const_spec((1, d)),
        ],
        out_specs=(pl.BlockSpec((TM, d), lambda i: (i, 0)), pl.BlockSpec((TM, d // 2), lambda i: (i, 0))),
        scratch_shapes=[pltpu.VMEM((TM, d), BF16)],
        compiler_params=_params(),
        name="cross_attn",
    )(x, k, v, wq.astype(BF16), wo.astype(BF16), g.reshape(1, -1), b.reshape(1, -1))


def _router_kernel(x_ref, wcat_ref, bias_ref, ri_ref, rg_ref, cnt_ref, carry_scr):
    tm = x_ref.shape[0]
    step = pl.program_id(0)

    @pl.when(step == 0)
    def _():
        carry_scr[...] = jnp.zeros_like(carry_scr)

    x = x_ref[...]
    xh = x.astype(BF16)
    xl = (x - xh.astype(F32)).astype(BF16)
    r1 = _dot(xh, wcat_ref[...])
    r2 = _dot(xl, wcat_ref[:, :ROUTE_COLS])
    logits = r1[:, :ROUTE_COLS] + r1[:, ROUTE_COLS:] + r2 + bias_ref[...]

    lane = lax.broadcasted_iota(I32, (tm, ROUTE_COLS), 1).astype(F32)
    neg = -jnp.inf

    def first_argmax(vals):
        mx = jnp.max(vals, axis=-1, keepdims=True)
        idx = jnp.min(jnp.where(vals == mx, lane, float(ROUTE_COLS)), axis=-1, keepdims=True)
        return mx, idx

    gl = jnp.where(lane < N_GROUPS, logits, neg)
    gmax, g_sel = first_argmax(gl)
    gate_g = 1.0 / jnp.sum(jnp.exp(gl - gmax), axis=-1, keepdims=True)

    lo = N_GROUPS + g_sel * EXPERTS_PER_GROUP
    el = jnp.where((lane >= lo) & (lane < lo + EXPERTS_PER_GROUP), logits, neg)
    m1, i1 = first_argmax(el)
    m2, i2 = first_argmax(jnp.where(lane == i1, neg, el))
    e21 = jnp.exp(m2 - m1)
    w1 = 1.0 / (1.0 + e21)
    w2 = e21 / (1.0 + e21)

    oh1 = lane == i1
    oh2 = lane == i2
    oh = (oh1 | oh2).astype(BF16)
    r = lax.broadcasted_iota(I32, (tm, tm), 0)
    c = lax.broadcasted_iota(I32, (tm, tm), 1)
    before = _dot((r > c).astype(BF16), oh) + carry_scr[...]
    rank1 = jnp.sum(jnp.where(oh1, before, 0.0), axis=-1, keepdims=True)
    rank2 = jnp.sum(jnp.where(oh2, before, 0.0), axis=-1, keepdims=True)
    carry_scr[...] += jnp.sum(oh.astype(F32), axis=0, keepdims=True)

    l4 = lax.broadcasted_iota(I32, ri_ref.shape, 1)
    ri_ref[...] = jnp.where(l4 == 0, i1 - N_GROUPS,
                            jnp.where(l4 == 1, i2 - N_GROUPS,
                                      jnp.where(l4 == 2, rank1, rank2))).astype(I32)
    l2 = lax.broadcasted_iota(I32, rg_ref.shape, 1)
    rg_ref[...] = jnp.where(l2 == 0, gate_g * w1, gate_g * w2)
    cnt_ref[...] = carry_scr[...].astype(I32)


def _router(x, wr_g, br_g, wr_e, br_e):
    n, d = x.shape
    w = jnp.concatenate([wr_g, jnp.transpose(wr_e, (1, 0, 2)).reshape(d, N_EXPERTS)], axis=1)
    w = jnp.pad(w, ((0, 0), (0, ROUTE_COLS - w.shape[1])))
    w_hi = w.astype(BF16)
    w_lo = (w - w_hi.astype(F32)).astype(BF16)
    bias = jnp.pad(jnp.concatenate([br_g, br_e.reshape(-1)]), (0, ROUTE_COLS - N_GROUPS - N_EXPERTS))
    ri, rg, cnt = pl.pallas_call(
        _router_kernel,
        out_shape=(jax.ShapeDtypeStruct((n, 4), I32), jax.ShapeDtypeStruct((n, TOP_K), F32),
                   jax.ShapeDtypeStruct((1, ROUTE_COLS), I32)),
        grid=(n // ROUTER_TM,),
        in_specs=[pl.BlockSpec((ROUTER_TM, d), lambda i: (i, 0)),
                  _const_spec((d, 2 * ROUTE_COLS)), _const_spec((1, ROUTE_COLS))],
        out_specs=(pl.BlockSpec((ROUTER_TM, 4), lambda i: (i, 0)),
                   pl.BlockSpec((ROUTER_TM, TOP_K), lambda i: (i, 0)),
                   _const_spec((1, ROUTE_COLS))),
        scratch_shapes=[pltpu.VMEM((1, ROUTE_COLS), F32)],
        compiler_params=_params(),
        name="router",
    )(x, jnp.concatenate([w_hi, w_lo], axis=1), bias.reshape(1, -1))
    return ri, rg, cnt[0, N_GROUPS:N_GROUPS + N_EXPERTS]


def _sc_gather(table, idx):
    b = idx.shape[0]
    d = table.shape[1]
    per_worker = b // SC_WORKERS
    n_chunks = per_worker // SC_CHUNK
    assert per_worker * SC_WORKERS == b and n_chunks * SC_CHUNK == per_worker and n_chunks % 2 == 0
    mesh = plsc.VectorSubcoreMesh(core_axis_name="c", subcore_axis_name="s")

    @functools.partial(
        pl.kernel, mesh=mesh,
        out_type=jax.ShapeDtypeStruct((b, d), table.dtype),
        scratch_types=[pltpu.VMEM((per_worker,), I32),
                       pltpu.VMEM((2, SC_CHUNK, d), table.dtype),
                       pltpu.SemaphoreType.DMA((2,)),
                       pltpu.SemaphoreType.DMA((2,))],
        name="sc_gather",
    )
    def gather(table_hbm, idx_hbm, out_hbm, idx_v, rows_v, gsem, wsem):
        wid = lax.axis_index("s") * SC_CORES + lax.axis_index("c")
        base = wid * per_worker
        pltpu.sync_copy(idx_hbm.at[pl.ds(base, per_worker)], idx_v)

        def fetch(c, slot):
            off = pl.multiple_of(c * SC_CHUNK, SC_CHUNK)
            return pltpu.make_async_copy(table_hbm.at[idx_v.at[pl.ds(off, SC_CHUNK)]], rows_v.at[slot],
                                         gsem.at[slot])

        def put(c, slot):
            off = pl.multiple_of(c * SC_CHUNK, SC_CHUNK)
            return pltpu.make_async_copy(rows_v.at[slot], out_hbm.at[pl.ds(base + off, SC_CHUNK)],
                                         wsem.at[slot])

        fetch(0, 0).start()

        @pl.loop(0, n_chunks, step=2)
        def _(c0):
            for slot in range(2):
                c = c0 + slot

                @pl.when(c + 1 < n_chunks)
                def _():
                    @pl.when(c >= 1)
                    def _():
                        put(c - 1, 1 - slot).wait()
                    fetch(c + 1, 1 - slot).start()

                fetch(c, slot).wait()
                put(c, slot).start()

        put(n_chunks - 2, 0).wait()
        put(n_chunks - 1, 1).wait()

    return gather(table, idx)


def _sc_row_tokens(dest_flat, rows, n):
    a = dest_flat.shape[0]
    lanes = SC_LANES
    assert a % lanes == 0 and rows % lanes == 0
    mesh = plsc.VectorSubcoreMesh(core_axis_name="c", subcore_axis_name="s")

    @functools.partial(
        pl.kernel, mesh=mesh,
        out_type=jax.ShapeDtypeStruct((rows,), I32),
        scratch_types=[pltpu.VMEM((a,), I32), pltpu.VMEM((rows,), I32)],
        compiler_params=pltpu.CompilerParams(needs_layout_passes=False),
        name="sc_row_tokens",
    )
    def invert(dest_hbm, out_hbm, dest_v, map_v):
        @pl.when((lax.axis_index("s") == 0) & (lax.axis_index("c") == 0))
        def _():
            pltpu.sync_copy(dest_hbm, dest_v)
            lane = lax.iota(I32, lanes)

            @pl.loop(0, rows // lanes)
            def _(i):
                map_v[pl.ds(i * lanes, lanes)] = lax.rem(i * lanes + lane, n)

            @pl.loop(0, a // lanes)
            def _(i):
                plsc.store_scatter(map_v, [dest_v[pl.ds(i * lanes, lanes)]], (i * lanes + lane) // TOP_K)

            pltpu.sync_copy(map_v, out_hbm)

    return invert(dest_flat)


def _expert_kernel(sched_ref, nb_ref, xs_ref, w1_hbm, w3_hbm, w2_hbm, y_ref,
                   w1_buf, w3_buf, w2_buf, w1_scr, w3_scr, w2_scr, sems, *, layer):
    b = pl.program_id(0)
    used = b < nb_ref[0]
    expert, slot, run_start, next_expert = (sched_ref[r, b] for r in range(4))

    def fetch(e, s):
        return [pltpu.make_async_copy(w_hbm.at[layer, e], buf.at[s], sems.at[s, j])
                for j, (w_hbm, buf) in enumerate(((w1_hbm, w1_buf), (w3_hbm, w3_buf), (w2_hbm, w2_buf)))]

    @pl.when(used & (run_start == 1))
    def _():
        @pl.when(b == 0)
        def _():
            for c in fetch(expert, slot):
                c.start()

        for c in fetch(expert, slot):
            c.wait()

        @pl.when(next_expert >= 0)
        def _():
            for c in fetch(next_expert, 1 - slot):
                c.start()

        w1_scr[...] = w1_buf[slot].astype(BF16)
        w3_scr[...] = w3_buf[slot].astype(BF16)
        w2_scr[...] = w2_buf[slot].astype(BF16)

    @pl.when(used)
    def _():
        x_lo, x_hi = _unpack_halves(xs_ref[...])
        xb = jnp.concatenate([x_lo.astype(BF16), x_hi.astype(BF16)], axis=1)
        h1 = _dot(xb, w1_scr[...])
        h3 = _dot(xb, w3_scr[...])
        hid = h1 * (1.0 / (1.0 + jnp.exp(-h1))) * h3
        y_ref[...] = _pack_halves(_dot(hid.astype(BF16), w2_scr[...]))

    @pl.when(jnp.logical_not(used))
    def _():
        y_ref[...] = jnp.zeros_like(y_ref)


def _expert_mlp(xs, schedule, n_used, layer, w1, w3, w2, n_blocks):
    d, de = w1.shape[2], w1.shape[3]

    def row_map(b, sched, nb):
        return (jnp.minimum(b, nb[0] - 1), 0)

    return pl.pallas_call(
        functools.partial(_expert_kernel, layer=layer),
        out_shape=jax.ShapeDtypeStruct((n_blocks * EXPERT_BLOCK, d // 2), I32),
        grid_spec=pltpu.PrefetchScalarGridSpec(
            num_scalar_prefetch=2,
            grid=(n_blocks,),
            in_specs=[pl.BlockSpec((EXPERT_BLOCK, d // 2), row_map),
                      pl.BlockSpec(memory_space=pl.ANY),
                      pl.BlockSpec(memory_space=pl.ANY),
                      pl.BlockSpec(memory_space=pl.ANY)],
            out_specs=pl.BlockSpec((EXPERT_BLOCK, d // 2), lambda b, sched, nb: (b, 0)),
            scratch_shapes=[pltpu.VMEM((2, d, de), F32), pltpu.VMEM((2, d, de), F32), pltpu.VMEM((2, de, d), F32),
                            pltpu.VMEM((d, de), BF16), pltpu.VMEM((d, de), BF16), pltpu.VMEM((de, d), BF16),
                            pltpu.SemaphoreType.DMA((2, 3))],
        ),
        compiler_params=_params(),
        name="expert_mlp",
    )(schedule, n_used, xs, w1, w3, w2)


def _combine_kernel(x_ref, y0_ref, y1_ref, gate_ref, g_ref, b_ref, o_ref):
    gates = gate_ref[...]
    g0, g1 = gates[:, 0:1], gates[:, 1:2]
    y0_lo, y0_hi = _unpack_halves(y0_ref[...])
    y1_lo, y1_hi = _unpack_halves(y1_ref[...])
    ff = jnp.concatenate([g0 * y0_lo + g1 * y1_lo, g0 * y0_hi + g1 * y1_hi], axis=1)
    o_ref[...] = _layer_norm(ALPHA * x_ref[...] + ff, g_ref[...], b_ref[...])


def _combine(x, yg, gates, g, b):
    n, d = x.shape
    tiles = n // TM
    return pl.pallas_call(
        _combine_kernel,
        out_shape=jax.ShapeDtypeStruct((n, d), F32),
        grid=(tiles,),
        in_specs=[pl.BlockSpec((TM, d), lambda i: (i, 0)),
                  pl.BlockSpec((TM, d // 2), lambda i: (i, 0)),
                  pl.BlockSpec((TM, d // 2), lambda i: (i + tiles, 0)),
                  pl.BlockSpec((TM, TOP_K), lambda i: (i, 0)),
                  _const_spec((1, d)), _const_spec((1, d))],
        out_specs=pl.BlockSpec((TM, d), lambda i: (i, 0)),
        compiler_params=_params(),
        name="combine",
    )(x, yg, yg, gates, g.reshape(1, -1), b.reshape(1, -1))


def _moe(x, x_packed, layer, wr_g, br_g, wr_e, br_e, w1, w3, w2, g, b):
    n, d = x.shape
    n_blocks = (n * TOP_K + N_EXPERTS * (EXPERT_BLOCK - 1) + EXPERT_BLOCK - 1) // EXPERT_BLOCK
    ri, gates, counts = _router(x, wr_g, br_g, wr_e, br_e)

    blocks_e = (counts + EXPERT_BLOCK - 1) // EXPERT_BLOCK
    blocks_end = jnp.cumsum(blocks_e)
    run_start = (blocks_end - blocks_e) * EXPERT_BLOCK
    n_used = blocks_end[-1:].astype(I32)
    block_ids = jnp.arange(n_blocks, dtype=I32)
    block_expert = jnp.minimum(jnp.sum(blocks_end[None, :] <= block_ids[:, None], axis=1),
                               N_EXPERTS - 1).astype(I32)
    run_start_flag = jnp.concatenate([jnp.ones((1,), I32),
                                      (block_expert[1:] != block_expert[:-1]).astype(I32)])
    slot = (jnp.cumsum(run_start_flag) - 1) % 2
    next_block = blocks_end[block_expert]
    next_expert = jnp.where(next_block < n_used[0],
                            block_expert[jnp.minimum(next_block, n_blocks - 1)], -1)
    schedule = jnp.stack([block_expert, slot, run_start_flag, next_expert]).astype(I32)
    expert_ids = jnp.arange(N_EXPERTS, dtype=I32)
    start_of = jnp.sum(jnp.where(ri[:, :TOP_K, None] == expert_ids, run_start, 0), axis=-1)
    dest = (start_of + ri[:, TOP_K:]).astype(I32)
    row_token = _sc_row_tokens(dest.reshape(-1), n_blocks * EXPERT_BLOCK, n)

    xs = _sc_gather(x_packed, row_token)
    y = _expert_mlp(xs, schedule, n_used, layer, w1, w3, w2, n_blocks)
    yg = _sc_gather(y, dest.T.reshape(-1))
    return _combine(x, yg, gates, g, b)


def kernel(x, mem, w_in_even, w_pool, pool_scale, ln_v_g, ln_v_b, w_spatial, b_spatial, w_out_even,
           w_in_odd, conv_w, conv_b, w_out_odd, wq_x, wk_x, wv_x, wo_x, ln_g, ln_b, wr_group,
           br_group, wr_expert, br_expert, w1, w3, w2):
    bsz, seq, d = x.shape
    assert seq % TM == 0 and d % LANES == 0
    mlen = mem.shape[1]
    k_all, v_all = _memory_kv(mem.reshape(bsz * mlen, d), wk_x, wv_x)
    k_all = k_all.reshape(DEPTH, bsz, mlen, d)
    v_all = v_all.reshape(DEPTH, bsz, mlen, d)
    h = x.reshape(bsz * seq, d)
    for l in range(DEPTH):
        i = l // 2
        if l % 2 == 0:
            h = _even_mixer(h, seq, w_in_even[i], w_pool[i], pool_scale[i], ln_v_g[i], ln_v_b[i],
                            w_spatial[i], b_spatial[i], w_out_even[i], ln_g[l, 0], ln_b[l, 0])
        else:
            h = _odd_mixer(h, seq, w_in_odd[i], conv_w[i], conv_b[i], w_out_odd[i],
                           ln_g[l, 0], ln_b[l, 0])
        h, hp = _cross_attn(h, seq, k_all[l], v_all[l], wq_x[l], wo_x[l], ln_g[l, 1], ln_b[l, 1])
        h = _moe(h, hp, l, wr_group[l], br_group[l], wr_expert[l], br_expert[l], w1, w3, w2,
                 ln_g[l, 2], ln_b[l, 2])
    return h.reshape(bsz, seq, d)
```

```python
import functools
import math

import jax
import jax.numpy as jnp
from jax import lax
from jax.experimental import pallas as pl
from jax.experimental.pallas import tpu as pltpu
from jax.experimental.pallas import tpu_sc as plsc

F32 = jnp.float32
BF16 = jnp.bfloat16
I32 = jnp.int32

POOL_WINDOWS = (2, 4, 8, 16)
N_SG_HEADS = 4
CHUNK = 128
CONV_WIDTH = 3
N_XHEADS = 4
N_GROUPS = 4
EXPERTS_PER_GROUP = 8
N_EXPERTS = N_GROUPS * EXPERTS_PER_GROUP
TOP_K = 2
DEPTH = 4
ALPHA = (2.0 * DEPTH) ** 0.25
LN_EPS = 1e-5

LANES = 128
SC_CORES = 2
SC_WORKERS = 32
SC_LANES = 16
SC_CHUNK = 64
TM = 1024
SUB_TILES = 2
ROUTE_ROWS = 8
POOL_HALO = 16
CONV_HALO = 8
EXPERT_BLOCK = 512
ROUTE_COLS = 128
VMEM_LIMIT = 48 * 1024 * 1024

_NT = (((1,), (1,)), ((), ()))


def _dot(a, b):
    return jnp.dot(a, b, preferred_element_type=F32)


def _layer_norm(y, g, b):
    mu = jnp.mean(y, axis=-1, keepdims=True)
    yc = y - mu
    var = jnp.mean(yc * yc, axis=-1, keepdims=True)
    return yc * lax.rsqrt(var + LN_EPS) * g + b


def _gelu_tanh(x):
    c = math.sqrt(2.0 / math.pi)
    return 0.5 * x * (1.0 + jnp.tanh(c * (x + 0.044715 * (x * x * x))))


def _pack_halves(v):
    c = v.shape[1] // 2
    lo = pltpu.bitcast(v[:, :c].astype(BF16).astype(F32), jnp.uint32)
    hi = pltpu.bitcast(v[:, c:].astype(BF16).astype(F32), jnp.uint32)
    return pltpu.bitcast((hi & jnp.uint32(0xFFFF0000)) | (lo >> 16), I32)


def _unpack_halves(w):
    u = pltpu.bitcast(w, jnp.uint32)
    return pltpu.bitcast(u << 16, F32), pltpu.bitcast(u & jnp.uint32(0xFFFF0000), F32)


def _const_spec(shape):
    nd = len(shape)
    return pl.BlockSpec(shape, lambda i: (0,) * nd)


def _params():
    return pltpu.CompilerParams(dimension_semantics=("arbitrary",), vmem_limit_bytes=VMEM_LIMIT)


def _even_kernel(x_ref, xh_ref, win_ref, wpool_ref, pscale_ref, lvg_ref, lvb_ref, ws_ref, bst_ref,
                 wout_ref, g_ref, b_ref, o_ref, a_scr, cat_scr, *, tiles_per_seq):
    tm = x_ref.shape[0]
    d_pool = a_scr.shape[1]
    d_sg = lvg_ref.shape[1]
    pgd = d_pool // len(POOL_WINDOWS)
    hd_dim = d_sg // N_SG_HEADS
    seq_tile = pl.program_id(0) % tiles_per_seq

    x = x_ref[...]
    h = _dot(x.astype(BF16), win_ref[...])

    ah = _dot(xh_ref[...].astype(BF16), win_ref[:, :d_pool])
    a_scr[0:POOL_HALO, :] = jnp.where(seq_tile == 0, 0.0, ah)
    a_scr[POOL_HALO:POOL_HALO + tm, :] = h[:, :d_pool]
    pos = seq_tile * tm + lax.broadcasted_iota(I32, (tm, 1), 0)
    for g, w in enumerate(POOL_WINDOWS):
        cs = slice(g * pgd, (g + 1) * pgd)
        tok = a_scr[POOL_HALO:POOL_HALO + tm, cs]
        acc = tok
        for j in range(1, w):
            acc = acc + a_scr[POOL_HALO - j:POOL_HALO - j + tm, cs]
        cnt = jnp.minimum(pos + 1, w).astype(F32)
        d = acc / cnt - tok
        yg = _dot(d.astype(BF16), wpool_ref[g])
        cat_scr[:, cs] = (yg * pscale_ref[:, cs]).astype(BF16)

    z = _gelu_tanh(h[:, d_pool:])
    u = z[:, :d_sg]
    v = _layer_norm(z[:, d_sg:], lvg_ref[...], lvb_ref[...]).astype(BF16)
    row = lax.broadcasted_iota(I32, (CHUNK, CHUNK), 0)
    col = lax.broadcasted_iota(I32, (CHUNK, CHUNK), 1)
    for hd in range(N_SG_HEADS):
        hs = slice(hd * hd_dim, (hd + 1) * hd_dim)
        wsm = jnp.where(row >= col, ws_ref[hd], 0.0).astype(BF16)
        bcol = bst_ref[:, hd:hd + 1]
        for ck in range(tm // CHUNK):
            rs = slice(ck * CHUNK, (ck + 1) * CHUNK)
            sv = _dot(wsm, v[rs, hs]) + bcol
            cat_scr[rs, d_pool + hd * hd_dim:d_pool + (hd + 1) * hd_dim] = (u[rs, hs] * sv).astype(BF16)

    mix = _dot(cat_scr[...], wout_ref[...])
    o_ref[...] = _layer_norm(ALPHA * x + mix, g_ref[...], b_ref[...])


def _even_mixer(x, seq, w_in, w_pool, pool_scale, ln_v_g, ln_v_b, w_spatial, b_spatial, w_out, g, b):
    n, d = x.shape
    d_in = w_in.shape[1]
    d_pool = pool_scale.shape[0]
    d_sg = ln_v_g.shape[0]
    kern = functools.partial(_even_kernel, tiles_per_seq=seq // TM)
    halo_blocks = TM // POOL_HALO
    return pl.pallas_call(
        kern,
        out_shape=jax.ShapeDtypeStruct((n, d), F32),
        grid=(n // TM,),
        in_specs=[
            pl.BlockSpec((TM, d), lambda i: (i, 0)),
            pl.BlockSpec((POOL_HALO, d), lambda i: (jnp.maximum(i * halo_blocks - 1, 0), 0)),
            _const_spec((d, d_in)),
            _const_spec(w_pool.shape),
            _const_spec((1, d_pool)),
            _const_spec((1, d_sg)),
            _const_spec((1, d_sg)),
            _const_spec(w_spatial.shape),
            _const_spec((CHUNK, N_SG_HEADS)),
            _const_spec(w_out.shape),
            _const_spec((1, d)),
            _const_spec((1, d)),
        ],
        out_specs=pl.BlockSpec((TM, d), lambda i: (i, 0)),
        scratch_shapes=[pltpu.VMEM((POOL_HALO + TM, d_pool), F32), pltpu.VMEM((TM, d_pool + d_sg), BF16)],
        compiler_params=_params(),
        name="even_mixer",
    )(x, x, w_in.astype(BF16), w_pool.astype(BF16), pool_scale.reshape(1, -1), ln_v_g.reshape(1, -1),
      ln_v_b.reshape(1, -1), w_spatial, b_spatial.T, w_out.astype(BF16), g.reshape(1, -1), b.reshape(1, -1))


def _odd_kernel(x_ref, xh_ref, win_ref, cwt_ref, cb_ref, wout_ref, g_ref, b_ref, o_ref, zc_scr,
                *, tiles_per_seq):
    tm, d = x_ref.shape
    seq_tile = pl.program_id(0) % tiles_per_seq

    x = x_ref[...]
    h = _dot(x.astype(BF16), win_ref[...])
    hh = _dot(xh_ref[...].astype(BF16), win_ref[:, d:])
    zc_scr[0:CONV_HALO, :] = jnp.where(seq_tile == 0, 0.0, hh[:, :d] * hh[:, d:])
    zc_scr[CONV_HALO:CONV_HALO + tm, :] = h[:, d:2 * d] * h[:, 2 * d:]
    conv = cb_ref[...]
    for j in range(CONV_WIDTH):
        off = CONV_HALO - (CONV_WIDTH - 1) + j
        conv = conv + zc_scr[off:off + tm, :] * cwt_ref[j:j + 1, :]
    y = _dot((h[:, :d] * conv).astype(BF16), wout_ref[...])
    o_ref[...] = _layer_norm(ALPHA * x + y, g_ref[...], b_ref[...])


def _odd_mixer(x, seq, w_in, conv_w, conv_b, w_out, g, b):
    n, d = x.shape
    kern = functools.partial(_odd_kernel, tiles_per_seq=seq // TM)
    halo_blocks = TM // CONV_HALO
    return pl.pallas_call(
        kern,
        out_shape=jax.ShapeDtypeStruct((n, d), F32),
        grid=(n // TM,),
        in_specs=[
            pl.BlockSpec((TM, d), lambda i: (i, 0)),
            pl.BlockSpec((CONV_HALO, d), lambda i: (jnp.maximum(i * halo_blocks - 1, 0), 0)),
            _const_spec(w_in.shape),
            _const_spec((CONV_WIDTH, d)),
            _const_spec((1, d)),
            _const_spec(w_out.shape),
            _const_spec((1, d)),
            _const_spec((1, d)),
        ],
        out_specs=pl.BlockSpec((TM, d), lambda i: (i, 0)),
        scratch_shapes=[pltpu.VMEM((CONV_HALO + TM, d), F32)],
        compiler_params=_params(),
        name="odd_mixer",
    )(x, x, w_in.astype(BF16), conv_w.T, conv_b.reshape(1, -1), w_out.astype(BF16),
      g.reshape(1, -1), b.reshape(1, -1))


def _kv_kernel(mem_ref, wk_ref, wv_ref, k_ref, v_ref):
    m = mem_ref[...].astype(BF16)
    k_ref[...] = _dot(m, wk_ref[...]).astype(BF16)
    v_ref[...] = _dot(m, wv_ref[...]).astype(BF16)


def _memory_kv(mem2d, wk, wv):
    nl, d, _ = wk.shape
    rows = mem2d.shape[0]
    out = jax.ShapeDtypeStruct((nl, rows, d), BF16)
    wspec = pl.BlockSpec((None, d, d), lambda l: (l, 0, 0))
    ospec = pl.BlockSpec((None, rows, d), lambda l: (l, 0, 0))
    return pl.pallas_call(
        _kv_kernel,
        out_shape=(out, out),
        grid=(nl,),
        in_specs=[_const_spec((rows, d)), wspec, wspec],
        out_specs=(ospec, ospec),
        compiler_params=_params(),
        name="memory_kv",
    )(mem2d, wk.astype(BF16), wv.astype(BF16))


def _attn_kernel(x_ref, k_ref, v_ref, wq_ref, wo_ref, g_ref, b_ref, wr_ref, br_ref,
                 o_ref, op_ref, rt_ref, rg_ref, cnt_ref, o_scr, carry_scr):
    tm, d = x_ref.shape
    hd_dim = d // N_XHEADS
    sub = tm // SUB_TILES

    @pl.when(pl.program_id(0) == 0)
    def _():
        carry_scr[...] = jnp.zeros_like(carry_scr)

    for st in range(SUB_TILES):
        rs = slice(st * sub, (st + 1) * sub)
        x = x_ref[rs, :]
        q = _dot(x.astype(BF16), wq_ref[...]) * (1.0 / math.sqrt(hd_dim))
        for hd in range(N_XHEADS):
            hs = slice(hd * hd_dim, (hd + 1) * hd_dim)
            s = lax.dot_general(q[:, hs].astype(BF16), k_ref[:, hs], _NT, preferred_element_type=F32)
            p = jnp.exp(s - jnp.max(s, axis=-1, keepdims=True))
            p = p / jnp.sum(p, axis=-1, keepdims=True)
            o_scr[rs, hs] = _dot(p.astype(BF16), v_ref[:, hs]).astype(BF16)
        xa = _dot(o_scr[rs, :], wo_ref[...])
        out = _layer_norm(ALPHA * x + xa, g_ref[...], b_ref[...])
        o_ref[rs, :] = out
        op_ref[rs, :] = _pack_halves(out)
        table, gates = _route_rows(out, wr_ref, br_ref, carry_scr)
        rt_ref[:, rs] = table
        rg_ref[rs, :] = gates
    cnt_ref[...] = carry_scr[...].astype(I32)


def _cross_attn(x, seq, k, v, wq, wo, g, b, router_w, router_b):
    n, d = x.shape
    m = k.shape[1]
    tiles_per_seq = seq // TM
    kvspec = pl.BlockSpec((None, m, d), lambda i: (i // tiles_per_seq, 0, 0))
    out, packed, table, gates, cnt = pl.pallas_call(
        _attn_kernel,
        out_shape=(jax.ShapeDtypeStruct((n, d), F32), jax.ShapeDtypeStruct((n, d // 2), I32),
                   jax.ShapeDtypeStruct((ROUTE_ROWS, n), F32), jax.ShapeDtypeStruct((n, TOP_K), F32),
                   jax.ShapeDtypeStruct((1, ROUTE_COLS), I32)),
        grid=(n // TM,),
        in_specs=[
            pl.BlockSpec((TM, d), lambda i: (i, 0)),
            kvspec, kvspec,
            _const_spec((d, d)), _const_spec((d, d)),
            _const_spec((1, d)), _const_spec((1, d)),
            _const_spec((d, 2 * ROUTE_COLS)), _const_spec((1, ROUTE_COLS)),
        ],
        out_specs=(pl.BlockSpec((TM, d), lambda i: (i, 0)), pl.BlockSpec((TM, d // 2), lambda i: (i, 0)),
                   pl.BlockSpec((ROUTE_ROWS, TM), lambda i: (0, i)), pl.BlockSpec((TM, TOP_K), lambda i: (i, 0)),
                   _const_spec((1, ROUTE_COLS))),
        scratch_shapes=[pltpu.VMEM((TM, d), BF16), pltpu.VMEM((1, ROUTE_COLS), F32)],
        compiler_params=_params(),
        name="cross_attn",
    )(x, k, v, wq.astype(BF16), wo.astype(BF16), g.reshape(1, -1), b.reshape(1, -1), router_w, router_b)
    return out, packed, table, gates, cnt[0, N_GROUPS:N_GROUPS + N_EXPERTS]


def _route_rows(x, wcat_ref, bias_ref, carry_scr):
    tm = x.shape[0]

    xh = x.astype(BF16)
    xl = (x - xh.astype(F32)).astype(BF16)
    r1 = _dot(xh, wcat_ref[...])
    r2 = _dot(xl, wcat_ref[:, :ROUTE_COLS])
    logits = r1[:, :ROUTE_COLS] + r1[:, ROUTE_COLS:] + r2 + bias_ref[...]

    lane = lax.broadcasted_iota(I32, (tm, ROUTE_COLS), 1).astype(F32)
    neg = -jnp.inf

    def first_argmax(vals):
        mx = jnp.max(vals, axis=-1, keepdims=True)
        idx = jnp.min(jnp.where(vals == mx, lane, float(ROUTE_COLS)), axis=-1, keepdims=True)
        return mx, idx

    gl = jnp.where(lane < N_GROUPS, logits, neg)
    gmax, g_sel = first_argmax(gl)
    gate_g = 1.0 / jnp.sum(jnp.exp(gl - gmax), axis=-1, keepdims=True)

    lo = N_GROUPS + g_sel * EXPERTS_PER_GROUP
    el = jnp.where((lane >= lo) & (lane < lo + EXPERTS_PER_GROUP), logits, neg)
    m1, i1 = first_argmax(el)
    m2, i2 = first_argmax(jnp.where(lane == i1, neg, el))
    e21 = jnp.exp(m2 - m1)
    w1 = 1.0 / (1.0 + e21)
    w2 = e21 / (1.0 + e21)

    oh1 = lane == i1
    oh2 = lane == i2
    oh = (oh1 | oh2).astype(BF16)
    r = lax.broadcasted_iota(I32, (tm, tm), 0)
    c = lax.broadcasted_iota(I32, (tm, tm), 1)
    before = _dot((r > c).astype(BF16), oh) + carry_scr[...]
    rank1 = jnp.sum(jnp.where(oh1, before, 0.0), axis=-1, keepdims=True)
    rank2 = jnp.sum(jnp.where(oh2, before, 0.0), axis=-1, keepdims=True)
    carry_scr[...] += jnp.sum(oh.astype(F32), axis=0, keepdims=True)

    cols = jnp.where(lane == 0.0, i1 - N_GROUPS,
                     jnp.where(lane == 1.0, i2 - N_GROUPS,
                               jnp.where(lane == 2.0, rank1, jnp.where(lane == 3.0, rank2, 0.0))))
    table = jnp.transpose(cols)[:ROUTE_ROWS, :]
    l2 = lax.broadcasted_iota(I32, (tm, TOP_K), 1)
    gates = jnp.where(l2 == 0, gate_g * w1, gate_g * w2)
    return table, gates


def _router_weights(wr_g, br_g, wr_e, br_e):
    d = wr_g.shape[0]
    w = jnp.concatenate([wr_g, jnp.transpose(wr_e, (1, 0, 2)).reshape(d, N_EXPERTS)], axis=1)
    w = jnp.pad(w, ((0, 0), (0, ROUTE_COLS - w.shape[1])))
    w_hi = w.astype(BF16)
    w_lo = (w - w_hi.astype(F32)).astype(BF16)
    bias = jnp.pad(jnp.concatenate([br_g, br_e.reshape(-1)]), (0, ROUTE_COLS - N_GROUPS - N_EXPERTS))
    return jnp.concatenate([w_hi, w_lo], axis=1), bias.reshape(1, -1)


def _sc_gather(table, idx):
    b = idx.shape[0]
    d = table.shape[1]
    per_worker = b // SC_WORKERS
    n_chunks = per_worker // SC_CHUNK
    assert per_worker * SC_WORKERS == b and n_chunks * SC_CHUNK == per_worker and n_chunks % 2 == 0
    mesh = plsc.VectorSubcoreMesh(core_axis_name="c", subcore_axis_name="s")

    @functools.partial(
        pl.kernel, mesh=mesh,
        out_type=jax.ShapeDtypeStruct((b, d), table.dtype),
        scratch_types=[pltpu.VMEM((per_worker,), I32),
                       pltpu.VMEM((2, SC_CHUNK, d), table.dtype),
                       pltpu.SemaphoreType.DMA((2,)),
                       pltpu.SemaphoreType.DMA((2,))],
        name="sc_gather",
    )
    def gather(table_hbm, idx_hbm, out_hbm, idx_v, rows_v, gsem, wsem):
        wid = lax.axis_index("s") * SC_CORES + lax.axis_index("c")
        base = wid * per_worker
        pltpu.sync_copy(idx_hbm.at[pl.ds(base, per_worker)], idx_v)

        def fetch(c, slot):
            off = pl.multiple_of(c * SC_CHUNK, SC_CHUNK)
            return pltpu.make_async_copy(table_hbm.at[idx_v.at[pl.ds(off, SC_CHUNK)]], rows_v.at[slot],
                                         gsem.at[slot])

        def put(c, slot):
            off = pl.multiple_of(c * SC_CHUNK, SC_CHUNK)
            return pltpu.make_async_copy(rows_v.at[slot], out_hbm.at[pl.ds(base + off, SC_CHUNK)],
                                         wsem.at[slot])

        fetch(0, 0).start()

        @pl.loop(0, n_chunks, step=2)
        def _(c0):
            for slot in range(2):
                c = c0 + slot

                @pl.when(c + 1 < n_chunks)
                def _():
                    @pl.when(c >= 1)
                    def _():
                        put(c - 1, 1 - slot).wait()
                    fetch(c + 1, 1 - slot).start()

                fetch(c, slot).wait()
                put(c, slot).start()

        put(n_chunks - 2, 0).wait()
        put(n_chunks - 1, 1).wait()

    return gather(table, idx)


def _sc_row_tokens(dest_flat, rows, n):
    a = dest_flat.shape[0]
    lanes = SC_LANES
    assert a % lanes == 0 and rows % lanes == 0
    mesh = plsc.VectorSubcoreMesh(core_axis_name="c", subcore_axis_name="s")

    @functools.partial(
        pl.kernel, mesh=mesh,
        out_type=jax.ShapeDtypeStruct((rows,), I32),
        scratch_types=[pltpu.VMEM((a,), I32), pltpu.VMEM((rows,), I32)],
        compiler_params=pltpu.CompilerParams(needs_layout_passes=False),
        name="sc_row_tokens",
    )
    def invert(dest_hbm, out_hbm, dest_v, map_v):
        @pl.when((lax.axis_index("s") == 0) & (lax.axis_index("c") == 0))
        def _():
            pltpu.sync_copy(dest_hbm, dest_v)
            lane = lax.iota(I32, lanes)

            @pl.loop(0, rows // lanes)
            def _(i):
                map_v[pl.ds(i * lanes, lanes)] = lax.rem(i * lanes + lane, n)

            @pl.loop(0, a // lanes)
            def _(i):
                plsc.store_scatter(map_v, [dest_v[pl.ds(i * lanes, lanes)]], lax.rem(i * lanes + lane, n))

            pltpu.sync_copy(map_v, out_hbm)

    return invert(dest_flat)


def _expert_kernel(sched_ref, nb_ref, xs_ref, w1_hbm, w3_hbm, w2_hbm, y_ref,
                   w1_buf, w3_buf, w2_buf, w1_scr, w3_scr, w2_scr, sems, *, layer):
    b = pl.program_id(0)
    used = b < nb_ref[0]
    expert, slot, run_start, next_expert = (sched_ref[r, b] for r in range(4))

    def fetch(e, s):
        return [pltpu.make_async_copy(w_hbm.at[layer, e], buf.at[s], sems.at[s, j])
                for j, (w_hbm, buf) in enumerate(((w1_hbm, w1_buf), (w3_hbm, w3_buf), (w2_hbm, w2_buf)))]

    @pl.when(used & (run_start == 1))
    def _():
        @pl.when(b == 0)
        def _():
            for c in fetch(expert, slot):
                c.start()

        for c in fetch(expert, slot):
            c.wait()

        @pl.when(next_expert >= 0)
        def _():
            for c in fetch(next_expert, 1 - slot):
                c.start()

        w1_scr[...] = w1_buf[slot].astype(BF16)
        w3_scr[...] = w3_buf[slot].astype(BF16)
        w2_scr[...] = w2_buf[slot].astype(BF16)

    @pl.when(used)
    def _():
        x_lo, x_hi = _unpack_halves(xs_ref[...])
        xb = jnp.concatenate([x_lo.astype(BF16), x_hi.astype(BF16)], axis=1)
        h1 = _dot(xb, w1_scr[...])
        h3 = _dot(xb, w3_scr[...])
        hid = h1 * (1.0 / (1.0 + jnp.exp(-h1))) * h3
        y_ref[...] = _pack_halves(_dot(hid.astype(BF16), w2_scr[...]))

    @pl.when(jnp.logical_not(used))
    def _():
        y_ref[...] = jnp.zeros_like(y_ref)


def _expert_mlp(xs, schedule, n_used, layer, w1, w3, w2, n_blocks):
    d, de = w1.shape[2], w1.shape[3]

    def row_map(b, sched, nb):
        return (jnp.minimum(b, nb[0] - 1), 0)

    return pl.pallas_call(
        functools.partial(_expert_kernel, layer=layer),
        out_shape=jax.ShapeDtypeStruct((n_blocks * EXPERT_BLOCK, d // 2), I32),
        grid_spec=pltpu.PrefetchScalarGridSpec(
            num_scalar_prefetch=2,
            grid=(n_blocks,),
            in_specs=[pl.BlockSpec((EXPERT_BLOCK, d // 2), row_map),
                      pl.BlockSpec(memory_space=pl.ANY),
                      pl.BlockSpec(memory_space=pl.ANY),
                      pl.BlockSpec(memory_space=pl.ANY)],
            out_specs=pl.BlockSpec((EXPERT_BLOCK, d // 2), lambda b, sched, nb: (b, 0)),
            scratch_shapes=[pltpu.VMEM((2, d, de), F32), pltpu.VMEM((2, d, de), F32), pltpu.VMEM((2, de, d), F32),
                            pltpu.VMEM((d, de), BF16), pltpu.VMEM((d, de), BF16), pltpu.VMEM((de, d), BF16),
                            pltpu.SemaphoreType.DMA((2, 3))],
        ),
        compiler_params=_params(),
        name="expert_mlp",
    )(schedule, n_used, xs, w1, w3, w2)


def _combine_kernel(x_ref, y0_ref, y1_ref, gate_ref, g_ref, b_ref, o_ref):
    gates = gate_ref[...]
    g0, g1 = gates[:, 0:1], gates[:, 1:2]
    y0_lo, y0_hi = _unpack_halves(y0_ref[...])
    y1_lo, y1_hi = _unpack_halves(y1_ref[...])
    ff = jnp.concatenate([g0 * y0_lo + g1 * y1_lo, g0 * y0_hi + g1 * y1_hi], axis=1)
    o_ref[...] = _layer_norm(ALPHA * x_ref[...] + ff, g_ref[...], b_ref[...])


def _combine(x, yg, gates, g, b):
    n, d = x.shape
    tiles = n // TM
    return pl.pallas_call(
        _combine_kernel,
        out_shape=jax.ShapeDtypeStruct((n, d), F32),
        grid=(tiles,),
        in_specs=[pl.BlockSpec((TM, d), lambda i: (i, 0)),
                  pl.BlockSpec((TM, d // 2), lambda i: (i, 0)),
                  pl.BlockSpec((TM, d // 2), lambda i: (i + tiles, 0)),
                  pl.BlockSpec((TM, TOP_K), lambda i: (i, 0)),
                  _const_spec((1, d)), _const_spec((1, d))],
        out_specs=pl.BlockSpec((TM, d), lambda i: (i, 0)),
        compiler_params=_params(),
        name="combine",
    )(x, yg, yg, gates, g.reshape(1, -1), b.reshape(1, -1))


def _moe(x, x_packed, table, gates, counts, layer, w1, w3, w2, g, b):
    n, d = x.shape
    n_blocks = (n * TOP_K + N_EXPERTS * (EXPERT_BLOCK - 1) + EXPERT_BLOCK - 1) // EXPERT_BLOCK
    experts = table[:TOP_K].astype(I32)
    ranks = table[TOP_K:2 * TOP_K].astype(I32)

    blocks_e = (counts + EXPERT_BLOCK - 1) // EXPERT_BLOCK
    blocks_end = jnp.cumsum(blocks_e)
    run_start = (blocks_end - blocks_e) * EXPERT_BLOCK
    n_used = blocks_end[-1:].astype(I32)
    block_ids = jnp.arange(n_blocks, dtype=I32)
    block_expert = jnp.minimum(jnp.sum(blocks_end[None, :] <= block_ids[:, None], axis=1),
                               N_EXPERTS - 1).astype(I32)
    run_start_flag = jnp.concatenate([jnp.ones((1,), I32),
                                      (block_expert[1:] != block_expert[:-1]).astype(I32)])
    slot = (jnp.cumsum(run_start_flag) - 1) % 2
    next_block = blocks_end[block_expert]
    next_expert = jnp.where(next_block < n_used[0],
                            block_expert[jnp.minimum(next_block, n_blocks - 1)], -1)
    schedule = jnp.stack([block_expert, slot, run_start_flag, next_expert]).astype(I32)
    expert_ids = jnp.arange(N_EXPERTS, dtype=I32)
    start_of = jnp.sum(jnp.where(experts[:, :, None] == expert_ids, run_start, 0), axis=-1)
    dest = (start_of + ranks).astype(I32).reshape(-1)
    row_token = _sc_row_tokens(dest, n_blocks * EXPERT_BLOCK, n)

    xs = _sc_gather(x_packed, row_token)
    y = _expert_mlp(xs, schedule, n_used, layer, w1, w3, w2, n_blocks)
    yg = _sc_gather(y, dest)
    return _combine(x, yg, gates, g, b)


def kernel(x, mem, w_in_even, w_pool, pool_scale, ln_v_g, ln_v_b, w_spatial, b_spatial, w_out_even,
           w_in_odd, conv_w, conv_b, w_out_odd, wq_x, wk_x, wv_x, wo_x, ln_g, ln_b, wr_group,
           br_group, wr_expert, br_expert, w1, w3, w2):
    bsz, seq, d = x.shape
    assert seq % TM == 0 and d % LANES == 0
    mlen = mem.shape[1]
    k_all, v_all = _memory_kv(mem.reshape(bsz * mlen, d), wk_x, wv_x)
    k_all = k_all.reshape(DEPTH, bsz, mlen, d)
    v_all = v_all.reshape(DEPTH, bsz, mlen, d)
    h = x.reshape(bsz * seq, d)
    for l in range(DEPTH):
        i = l // 2
        if l % 2 == 0:
            h = _even_mixer(h, seq, w_in_even[i], w_pool[i], pool_scale[i], ln_v_g[i], ln_v_b[i],
                            w_spatial[i], b_spatial[i], w_out_even[i], ln_g[l, 0], ln_b[l, 0])
        else:
            h = _odd_mixer(h, seq, w_in_odd[i], conv_w[i], conv_b[i], w_out_odd[i],
                           ln_g[l, 0], ln_b[l, 0])
        router_w, router_b = _router_weights(wr_group[l], br_group[l], wr_expert[l], br_expert[l])
        h, hp, table, gates, counts = _cross_attn(h, seq, k_all[l], v_all[l], wq_x[l], wo_x[l],
                                                  ln_g[l, 1], ln_b[l, 1], router_w, router_b)
        h = _moe(h, hp, table, gates, counts, l, w1, w3, w2, ln_g[l, 2], ln_b[l, 2])
    return h.reshape(bsz, seq, d)
```

```python
import functools
import math

import jax
import jax.numpy as jnp
from jax import lax
from jax.experimental import pallas as pl
from jax.experimental.pallas import tpu as pltpu
from jax.experimental.pallas import tpu_sc as plsc

F32 = jnp.float32
BF16 = jnp.bfloat16
I32 = jnp.int32

POOL_WINDOWS = (2, 4, 8, 16)
N_SG_HEADS = 4
CHUNK = 128
CONV_WIDTH = 3
N_XHEADS = 4
N_GROUPS = 4
EXPERTS_PER_GROUP = 8
N_EXPERTS = N_GROUPS * EXPERTS_PER_GROUP
TOP_K = 2
DEPTH = 4
ALPHA = (2.0 * DEPTH) ** 0.25
LN_EPS = 1e-5

LANES = 128
SC_CORES = 2
SC_WORKERS = 32
SC_LANES = 16
SC_CHUNK = 64
TM = 1024
SUB_TILES = 2
ROUTE_ROWS = 8
POOL_HALO = 16
CONV_HALO = 8
EXPERT_BLOCK = 512
ROUTE_COLS = 128
STAGE_COLS = 512
VMEM_LIMIT = 48 * 1024 * 1024

_NT = (((1,), (1,)), ((), ()))


def _dot(a, b):
    return jnp.dot(a, b, preferred_element_type=F32)


def _layer_norm(y, g, b):
    mu = jnp.mean(y, axis=-1, keepdims=True)
    yc = y - mu
    var = jnp.mean(yc * yc, axis=-1, keepdims=True)
    return yc * lax.rsqrt(var + LN_EPS) * g + b


def _gelu_tanh(x):
    c = math.sqrt(2.0 / math.pi)
    return 0.5 * x * (1.0 + jnp.tanh(c * (x + 0.044715 * (x * x * x))))


def _pack_halves(v):
    c = v.shape[1] // 2
    lo = pltpu.bitcast(v[:, :c].astype(BF16).astype(F32), jnp.uint32)
    hi = pltpu.bitcast(v[:, c:].astype(BF16).astype(F32), jnp.uint32)
    return pltpu.bitcast((hi & jnp.uint32(0xFFFF0000)) | (lo >> 16), I32)


def _unpack_halves(w):
    u = pltpu.bitcast(w, jnp.uint32)
    return pltpu.bitcast(u << 16, F32), pltpu.bitcast(u & jnp.uint32(0xFFFF0000), F32)


def _load_cast(w_hbm, w_scr, stage, sems):
    chunks = w_scr.shape[1] // STAGE_COLS

    def chunk_copy(c):
        return pltpu.make_async_copy(w_hbm.at[:, pl.ds(c * STAGE_COLS, STAGE_COLS)], stage.at[c % 2],
                                     sems.at[c % 2])

    chunk_copy(0).start()
    for c in range(chunks):
        if c + 1 < chunks:
            chunk_copy(c + 1).start()
        chunk_copy(c).wait()
        w_scr[:, c * STAGE_COLS:(c + 1) * STAGE_COLS] = stage[c % 2].astype(BF16)


def _stage_scratch(rows):
    return [pltpu.VMEM((2, rows, STAGE_COLS), F32), pltpu.SemaphoreType.DMA((2,))]


_HBM = pl.BlockSpec(memory_space=pl.ANY)


def _const_spec(shape):
    nd = len(shape)
    return pl.BlockSpec(shape, lambda i: (0,) * nd)


def _params():
    return pltpu.CompilerParams(dimension_semantics=("arbitrary",), vmem_limit_bytes=VMEM_LIMIT)


def _even_kernel(x_ref, xh_ref, win_hbm, wpool_ref, pscale_ref, lvg_ref, lvb_ref, ws_ref, bst_ref,
                 wout_hbm, g_ref, b_ref, o_ref, a_scr, cat_scr, win_ref, wout_ref, stage, sems,
                 *, tiles_per_seq, layer):
    tm = x_ref.shape[0]
    d_pool = a_scr.shape[1]
    d_sg = lvg_ref.shape[1]
    pgd = d_pool // len(POOL_WINDOWS)
    hd_dim = d_sg // N_SG_HEADS
    seq_tile = pl.program_id(0) % tiles_per_seq

    @pl.when(pl.program_id(0) == 0)
    def _():
        _load_cast(win_hbm.at[layer], win_ref, stage, sems)
        _load_cast(wout_hbm.at[layer], wout_ref, stage, sems)

    x = x_ref[...]
    h = _dot(x.astype(BF16), win_ref[...])

    ah = _dot(xh_ref[...].astype(BF16), win_ref[:, :d_pool])
    a_scr[0:POOL_HALO, :] = jnp.where(seq_tile == 0, 0.0, ah)
    a_scr[POOL_HALO:POOL_HALO + tm, :] = h[:, :d_pool]
    pos = seq_tile * tm + lax.broadcasted_iota(I32, (tm, 1), 0)
    for g, w in enumerate(POOL_WINDOWS):
        cs = slice(g * pgd, (g + 1) * pgd)
        tok = a_scr[POOL_HALO:POOL_HALO + tm, cs]
        acc = tok
        for j in range(1, w):
            acc = acc + a_scr[POOL_HALO - j:POOL_HALO - j + tm, cs]
        cnt = jnp.minimum(pos + 1, w).astype(F32)
        d = acc / cnt - tok
        yg = _dot(d.astype(BF16), wpool_ref[g])
        cat_scr[:, cs] = (yg * pscale_ref[:, cs]).astype(BF16)

    z = _gelu_tanh(h[:, d_pool:])
    u = z[:, :d_sg]
    v = _layer_norm(z[:, d_sg:], lvg_ref[...], lvb_ref[...]).astype(BF16)
    row = lax.broadcasted_iota(I32, (CHUNK, CHUNK), 0)
    col = lax.broadcasted_iota(I32, (CHUNK, CHUNK), 1)
    for hd in range(N_SG_HEADS):
        hs = slice(hd * hd_dim, (hd + 1) * hd_dim)
        wsm = jnp.where(row >= col, ws_ref[hd], 0.0).astype(BF16)
        bcol = bst_ref[:, hd:hd + 1]
        for ck in range(tm // CHUNK):
            rs = slice(ck * CHUNK, (ck + 1) * CHUNK)
            sv = _dot(wsm, v[rs, hs]) + bcol
            cat_scr[rs, d_pool + hd * hd_dim:d_pool + (hd + 1) * hd_dim] = (u[rs, hs] * sv).astype(BF16)

    mix = _dot(cat_scr[...], wout_ref[...])
    o_ref[...] = _layer_norm(ALPHA * x + mix, g_ref[...], b_ref[...])


def _even_mixer(x, seq, layer, w_in, w_pool, pool_scale, ln_v_g, ln_v_b, w_spatial, b_spatial, w_out, g, b):
    n, d = x.shape
    d_in = w_in.shape[2]
    d_pool = pool_scale.shape[0]
    d_sg = ln_v_g.shape[0]
    kern = functools.partial(_even_kernel, tiles_per_seq=seq // TM, layer=layer)
    halo_blocks = TM // POOL_HALO
    return pl.pallas_call(
        kern,
        out_shape=jax.ShapeDtypeStruct((n, d), F32),
        grid=(n // TM,),
        in_specs=[
            pl.BlockSpec((TM, d), lambda i: (i, 0)),
            pl.BlockSpec((POOL_HALO, d), lambda i: (jnp.maximum(i * halo_blocks - 1, 0), 0)),
            _HBM,
            _const_spec(w_pool.shape),
            _const_spec((1, d_pool)),
            _const_spec((1, d_sg)),
            _const_spec((1, d_sg)),
            _const_spec(w_spatial.shape),
            _const_spec((CHUNK, N_SG_HEADS)),
            _HBM,
            _const_spec((1, d)),
            _const_spec((1, d)),
        ],
        out_specs=pl.BlockSpec((TM, d), lambda i: (i, 0)),
        scratch_shapes=[pltpu.VMEM((POOL_HALO + TM, d_pool), F32), pltpu.VMEM((TM, d_pool + d_sg), BF16),
                        pltpu.VMEM((d, d_in), BF16), pltpu.VMEM((d_pool + d_sg, d), BF16)] + _stage_scratch(d),
        compiler_params=_params(),
        name="even_mixer",
    )(x, x, w_in, w_pool.astype(BF16), pool_scale.reshape(1, -1), ln_v_g.reshape(1, -1),
      ln_v_b.reshape(1, -1), w_spatial, b_spatial.T, w_out, g.reshape(1, -1), b.reshape(1, -1))


def _odd_kernel(x_ref, xh_ref, win_hbm, cwt_ref, cb_ref, wout_hbm, g_ref, b_ref, o_ref, zc_scr,
                win_ref, wout_ref, stage, sems, *, tiles_per_seq, layer):
    tm, d = x_ref.shape
    seq_tile = pl.program_id(0) % tiles_per_seq

    @pl.when(pl.program_id(0) == 0)
    def _():
        _load_cast(win_hbm.at[layer], win_ref, stage, sems)
        _load_cast(wout_hbm.at[layer], wout_ref, stage, sems)

    x = x_ref[...]
    xb = x.astype(BF16)
    hc = _dot(xb, win_ref[:, d:2 * d])
    hz = _dot(xb, win_ref[:, 2 * d:])
    hh = _dot(xh_ref[...].astype(BF16), win_ref[:, d:])
    zc_scr[0:CONV_HALO, :] = jnp.where(seq_tile == 0, 0.0, hh[:, :d] * hh[:, d:])
    zc_scr[CONV_HALO:CONV_HALO + tm, :] = hc * hz
    gate = _dot(xb, win_ref[:, :d])
    conv = cb_ref[...]
    for j in range(CONV_WIDTH):
        off = CONV_HALO - (CONV_WIDTH - 1) + j
        conv = conv + zc_scr[off:off + tm, :] * cwt_ref[j:j + 1, :]
    y = _dot((gate * conv).astype(BF16), wout_ref[...])
    o_ref[...] = _layer_norm(ALPHA * x + y, g_ref[...], b_ref[...])


def _odd_mixer(x, seq, layer, w_in, conv_w, conv_b, w_out, g, b):
    n, d = x.shape
    kern = functools.partial(_odd_kernel, tiles_per_seq=seq // TM, layer=layer)
    halo_blocks = TM // CONV_HALO
    return pl.pallas_call(
        kern,
        out_shape=jax.ShapeDtypeStruct((n, d), F32),
        grid=(n // TM,),
        in_specs=[
            pl.BlockSpec((TM, d), lambda i: (i, 0)),
            pl.BlockSpec((CONV_HALO, d), lambda i: (jnp.maximum(i * halo_blocks - 1, 0), 0)),
            _HBM,
            _const_spec((CONV_WIDTH, d)),
            _const_spec((1, d)),
            _HBM,
            _const_spec((1, d)),
            _const_spec((1, d)),
        ],
        out_specs=pl.BlockSpec((TM, d), lambda i: (i, 0)),
        scratch_shapes=[pltpu.VMEM((CONV_HALO + TM, d), F32),
                        pltpu.VMEM(w_in.shape[1:], BF16), pltpu.VMEM(w_out.shape[1:], BF16)] + _stage_scratch(d),
        compiler_params=_params(),
        name="odd_mixer",
    )(x, x, w_in, conv_w.T, conv_b.reshape(1, -1), w_out, g.reshape(1, -1), b.reshape(1, -1))


def _kv_kernel(mem_ref, wk_ref, wv_ref, k_ref, v_ref):
    m = mem_ref[...].astype(BF16)
    k_ref[...] = _dot(m, wk_ref[...].astype(BF16)).astype(BF16)
    v_ref[...] = _dot(m, wv_ref[...].astype(BF16)).astype(BF16)


def _memory_kv(mem2d, wk, wv):
    nl, d, _ = wk.shape
    rows = mem2d.shape[0]
    out = jax.ShapeDtypeStruct((nl, rows, d), BF16)
    wspec = pl.BlockSpec((None, d, d), lambda l: (l, 0, 0))
    ospec = pl.BlockSpec((None, rows, d), lambda l: (l, 0, 0))
    return pl.pallas_call(
        _kv_kernel,
        out_shape=(out, out),
        grid=(nl,),
        in_specs=[_const_spec((rows, d)), wspec, wspec],
        out_specs=(ospec, ospec),
        compiler_params=_params(),
        name="memory_kv",
    )(mem2d, wk, wv)


def _attn_kernel(x_ref, k_ref, v_ref, wq_hbm, wo_hbm, g_ref, b_ref, wr_ref, br_ref,
                 o_ref, op_ref, rt_ref, rg_ref, cnt_ref, o_scr, carry_scr, wq_ref, wo_ref, stage, sems,
                 *, layer):
    tm, d = x_ref.shape
    hd_dim = d // N_XHEADS
    sub = tm // SUB_TILES

    @pl.when(pl.program_id(0) == 0)
    def _():
        carry_scr[...] = jnp.zeros_like(carry_scr)
        _load_cast(wq_hbm.at[layer], wq_ref, stage, sems)
        _load_cast(wo_hbm.at[layer], wo_ref, stage, sems)

    for st in range(SUB_TILES):
        rs = slice(st * sub, (st + 1) * sub)
        x = x_ref[rs, :]
        q = _dot(x.astype(BF16), wq_ref[...]) * (1.0 / math.sqrt(hd_dim))
        for hd in range(N_XHEADS):
            hs = slice(hd * hd_dim, (hd + 1) * hd_dim)
            s = lax.dot_general(q[:, hs].astype(BF16), k_ref[:, hs], _NT, preferred_element_type=F32)
            p = jnp.exp(s - jnp.max(s, axis=-1, keepdims=True))
            p = p / jnp.sum(p, axis=-1, keepdims=True)
            o_scr[rs, hs] = _dot(p.astype(BF16), v_ref[:, hs]).astype(BF16)
        xa = _dot(o_scr[rs, :], wo_ref[...])
        out = _layer_norm(ALPHA * x + xa, g_ref[...], b_ref[...])
        o_ref[rs, :] = out
        op_ref[rs, :] = _pack_halves(out)
        table, gates = _route_rows(out, wr_ref, br_ref, carry_scr)
        rt_ref[:, rs] = table
        rg_ref[rs, :] = gates
    cnt_ref[...] = carry_scr[...].astype(I32)


def _cross_attn(x, seq, layer, k, v, wq, wo, g, b, router_w, router_b):
    n, d = x.shape
    m = k.shape[1]
    tiles_per_seq = seq // TM
    kvspec = pl.BlockSpec((None, m, d), lambda i: (i // tiles_per_seq, 0, 0))
    out, packed, table, gates, cnt = pl.pallas_call(
        functools.partial(_attn_kernel, layer=layer),
        out_shape=(jax.ShapeDtypeStruct((n, d), F32), jax.ShapeDtypeStruct((n, d // 2), I32),
                   jax.ShapeDtypeStruct((ROUTE_ROWS, n), F32), jax.ShapeDtypeStruct((n, TOP_K), F32),
                   jax.ShapeDtypeStruct((1, ROUTE_COLS), I32)),
        grid=(n // TM,),
        in_specs=[
            pl.BlockSpec((TM, d), lambda i: (i, 0)),
            kvspec, kvspec,
            _HBM, _HBM,
            _const_spec((1, d)), _const_spec((1, d)),
            _const_spec((d, 2 * ROUTE_COLS)), _const_spec((1, ROUTE_COLS)),
        ],
        out_specs=(pl.BlockSpec((TM, d), lambda i: (i, 0)), pl.BlockSpec((TM, d // 2), lambda i: (i, 0)),
                   pl.BlockSpec((ROUTE_ROWS, TM), lambda i: (0, i)), pl.BlockSpec((TM, TOP_K), lambda i: (i, 0)),
                   _const_spec((1, ROUTE_COLS))),
        scratch_shapes=[pltpu.VMEM((TM, d), BF16), pltpu.VMEM((1, ROUTE_COLS), F32),
                        pltpu.VMEM((d, d), BF16), pltpu.VMEM((d, d), BF16)] + _stage_scratch(d),
        compiler_params=_params(),
        name="cross_attn",
    )(x, k, v, wq, wo, g.reshape(1, -1), b.reshape(1, -1), router_w, router_b)
    return out, packed, table, gates, cnt[0, N_GROUPS:N_GROUPS + N_EXPERTS]


def _route_rows(x, wcat_ref, bias_ref, carry_scr):
    tm = x.shape[0]

    xh = x.astype(BF16)
    xl = (x - xh.astype(F32)).astype(BF16)
    r1 = _dot(xh, wcat_ref[...])
    r2 = _dot(xl, wcat_ref[:, :ROUTE_COLS])
    logits = r1[:, :ROUTE_COLS] + r1[:, ROUTE_COLS:] + r2 + bias_ref[...]

    lane = lax.broadcasted_iota(I32, (tm, ROUTE_COLS), 1).astype(F32)
    neg = -jnp.inf

    def first_argmax(vals):
        mx = jnp.max(vals, axis=-1, keepdims=True)
        idx = jnp.min(jnp.where(vals == mx, lane, float(ROUTE_COLS)), axis=-1, keepdims=True)
        return mx, idx

    gl = jnp.where(lane < N_GROUPS, logits, neg)
    gmax, g_sel = first_argmax(gl)
    gate_g = 1.0 / jnp.sum(jnp.exp(gl - gmax), axis=-1, keepdims=True)

    lo = N_GROUPS + g_sel * EXPERTS_PER_GROUP
    el = jnp.where((lane >= lo) & (lane < lo + EXPERTS_PER_GROUP), logits, neg)
    m1, i1 = first_argmax(el)
    m2, i2 = first_argmax(jnp.where(lane == i1, neg, el))
    e21 = jnp.exp(m2 - m1)
    w1 = 1.0 / (1.0 + e21)
    w2 = e21 / (1.0 + e21)

    oh1 = lane == i1
    oh2 = lane == i2
    oh = (oh1 | oh2).astype(BF16)
    r = lax.broadcasted_iota(I32, (tm, tm), 0)
    c = lax.broadcasted_iota(I32, (tm, tm), 1)
    before = _dot((r > c).astype(BF16), oh) + carry_scr[...]
    rank1 = jnp.sum(jnp.where(oh1, before, 0.0), axis=-1, keepdims=True)
    rank2 = jnp.sum(jnp.where(oh2, before, 0.0), axis=-1, keepdims=True)
    carry_scr[...] += jnp.sum(oh.astype(F32), axis=0, keepdims=True)

    cols = jnp.where(lane == 0.0, i1 - N_GROUPS,
                     jnp.where(lane == 1.0, i2 - N_GROUPS,
                               jnp.where(lane == 2.0, rank1, jnp.where(lane == 3.0, rank2, 0.0))))
    table = jnp.transpose(cols)[:ROUTE_ROWS, :]
    l2 = lax.broadcasted_iota(I32, (tm, TOP_K), 1)
    gates = jnp.where(l2 == 0, gate_g * w1, gate_g * w2)
    return table, gates


def _router_weights(wr_g, br_g, wr_e, br_e):
    d = wr_g.shape[0]
    w = jnp.concatenate([wr_g, jnp.transpose(wr_e, (1, 0, 2)).reshape(d, N_EXPERTS)], axis=1)
    w = jnp.pad(w, ((0, 0), (0, ROUTE_COLS - w.shape[1])))
    w_hi = w.astype(BF16)
    w_lo = (w - w_hi.astype(F32)).astype(BF16)
    bias = jnp.pad(jnp.concatenate([br_g, br_e.reshape(-1)]), (0, ROUTE_COLS - N_GROUPS - N_EXPERTS))
    return jnp.concatenate([w_hi, w_lo], axis=1), bias.reshape(1, -1)


def _sc_gather(table, idx):
    b = idx.shape[0]
    d = table.shape[1]
    per_worker = b // SC_WORKERS
    n_chunks = per_worker // SC_CHUNK
    assert per_worker * SC_WORKERS == b and n_chunks * SC_CHUNK == per_worker and n_chunks % 2 == 0
    mesh = plsc.VectorSubcoreMesh(core_axis_name="c", subcore_axis_name="s")

    @functools.partial(
        pl.kernel, mesh=mesh,
        out_type=jax.ShapeDtypeStruct((b, d), table.dtype),
        scratch_types=[pltpu.VMEM((per_worker,), I32),
                       pltpu.VMEM((2, SC_CHUNK, d), table.dtype),
                       pltpu.SemaphoreType.DMA((2,)),
                       pltpu.SemaphoreType.DMA((2,))],
        name="sc_gather",
    )
    def gather(table_hbm, idx_hbm, out_hbm, idx_v, rows_v, gsem, wsem):
        wid = lax.axis_index("s") * SC_CORES + lax.axis_index("c")
        base = wid * per_worker
        pltpu.sync_copy(idx_hbm.at[pl.ds(base, per_worker)], idx_v)

        def fetch(c, slot):
            off = pl.multiple_of(c * SC_CHUNK, SC_CHUNK)
            return pltpu.make_async_copy(table_hbm.at[idx_v.at[pl.ds(off, SC_CHUNK)]], rows_v.at[slot],
                                         gsem.at[slot])

        def put(c, slot):
            off = pl.multiple_of(c * SC_CHUNK, SC_CHUNK)
            return pltpu.make_async_copy(rows_v.at[slot], out_hbm.at[pl.ds(base + off, SC_CHUNK)],
                                         wsem.at[slot])

        fetch(0, 0).start()

        @pl.loop(0, n_chunks, step=2)
        def _(c0):
            for slot in range(2):
                c = c0 + slot

                @pl.when(c + 1 < n_chunks)
                def _():
                    @pl.when(c >= 1)
                    def _():
                        put(c - 1, 1 - slot).wait()
                    fetch(c + 1, 1 - slot).start()

                fetch(c, slot).wait()
                put(c, slot).start()

        put(n_chunks - 2, 0).wait()
        put(n_chunks - 1, 1).wait()

    return gather(table, idx)


def _sc_row_tokens(dest_flat, rows, n):
    a = dest_flat.shape[0]
    lanes = SC_LANES
    assert a % lanes == 0 and rows % lanes == 0
    mesh = plsc.VectorSubcoreMesh(core_axis_name="c", subcore_axis_name="s")

    @functools.partial(
        pl.kernel, mesh=mesh,
        out_type=jax.ShapeDtypeStruct((rows,), I32),
        scratch_types=[pltpu.VMEM((a,), I32), pltpu.VMEM((rows,), I32)],
        compiler_params=pltpu.CompilerParams(needs_layout_passes=False),
        name="sc_row_tokens",
    )
    def invert(dest_hbm, out_hbm, dest_v, map_v):
        @pl.when((lax.axis_index("s") == 0) & (lax.axis_index("c") == 0))
        def _():
            pltpu.sync_copy(dest_hbm, dest_v)
            lane = lax.iota(I32, lanes)

            @pl.loop(0, rows // lanes)
            def _(i):
                map_v[pl.ds(i * lanes, lanes)] = lax.rem(i * lanes + lane, n)

            @pl.loop(0, a // lanes)
            def _(i):
                plsc.store_scatter(map_v, [dest_v[pl.ds(i * lanes, lanes)]], lax.rem(i * lanes + lane, n))

            pltpu.sync_copy(map_v, out_hbm)

    return invert(dest_flat)


def _expert_kernel(sched_ref, nb_ref, xs_ref, w1_hbm, w3_hbm, w2_hbm, y_ref,
                   w1_buf, w3_buf, w2_buf, w1_scr, w3_scr, w2_scr, sems, *, layer):
    b = pl.program_id(0)
    used = b < nb_ref[0]
    expert, slot, run_start, next_expert = (sched_ref[r, b] for r in range(4))

    def fetch(e, s):
        return [pltpu.make_async_copy(w_hbm.at[layer, e], buf.at[s], sems.at[s, j])
                for j, (w_hbm, buf) in enumerate(((w1_hbm, w1_buf), (w3_hbm, w3_buf), (w2_hbm, w2_buf)))]

    @pl.when(used & (run_start == 1))
    def _():
        @pl.when(b == 0)
        def _():
            for c in fetch(expert, slot):
                c.start()

        for c in fetch(expert, slot):
            c.wait()

        @pl.when(next_expert >= 0)
        def _():
            for c in fetch(next_expert, 1 - slot):
                c.start()

        w1_scr[...] = w1_buf[slot].astype(BF16)
        w3_scr[...] = w3_buf[slot].astype(BF16)
        w2_scr[...] = w2_buf[slot].astype(BF16)

    @pl.when(used)
    def _():
        x_lo, x_hi = _unpack_halves(xs_ref[...])
        xb = jnp.concatenate([x_lo.astype(BF16), x_hi.astype(BF16)], axis=1)
        h1 = _dot(xb, w1_scr[...])
        h3 = _dot(xb, w3_scr[...])
        hid = h1 * (1.0 / (1.0 + jnp.exp(-h1))) * h3
        y_ref[...] = _pack_halves(_dot(hid.astype(BF16), w2_scr[...]))

    @pl.when(jnp.logical_not(used))
    def _():
        y_ref[...] = jnp.zeros_like(y_ref)


def _expert_mlp(xs, schedule, n_used, layer, w1, w3, w2, n_blocks):
    d, de = w1.shape[2], w1.shape[3]

    def row_map(b, sched, nb):
        return (jnp.minimum(b, nb[0] - 1), 0)

    return pl.pallas_call(
        functools.partial(_expert_kernel, layer=layer),
        out_shape=jax.ShapeDtypeStruct((n_blocks * EXPERT_BLOCK, d // 2), I32),
        grid_spec=pltpu.PrefetchScalarGridSpec(
            num_scalar_prefetch=2,
            grid=(n_blocks,),
            in_specs=[pl.BlockSpec((EXPERT_BLOCK, d // 2), row_map),
                      pl.BlockSpec(memory_space=pl.ANY),
                      pl.BlockSpec(memory_space=pl.ANY),
                      pl.BlockSpec(memory_space=pl.ANY)],
            out_specs=pl.BlockSpec((EXPERT_BLOCK, d // 2), lambda b, sched, nb: (b, 0)),
            scratch_shapes=[pltpu.VMEM((2, d, de), F32), pltpu.VMEM((2, d, de), F32), pltpu.VMEM((2, de, d), F32),
                            pltpu.VMEM((d, de), BF16), pltpu.VMEM((d, de), BF16), pltpu.VMEM((de, d), BF16),
                            pltpu.SemaphoreType.DMA((2, 3))],
        ),
        compiler_params=_params(),
        name="expert_mlp",
    )(schedule, n_used, xs, w1, w3, w2)


def _combine_kernel(x_ref, y0_ref, y1_ref, gate_ref, g_ref, b_ref, o_ref):
    gates = gate_ref[...]
    g0, g1 = gates[:, 0:1], gates[:, 1:2]
    y0_lo, y0_hi = _unpack_halves(y0_ref[...])
    y1_lo, y1_hi = _unpack_halves(y1_ref[...])
    ff = jnp.concatenate([g0 * y0_lo + g1 * y1_lo, g0 * y0_hi + g1 * y1_hi], axis=1)
    o_ref[...] = _layer_norm(ALPHA * x_ref[...] + ff, g_ref[...], b_ref[...])


def _combine(x, yg, gates, g, b):
    n, d = x.shape
    tiles = n // TM
    return pl.pallas_call(
        _combine_kernel,
        out_shape=jax.ShapeDtypeStruct((n, d), F32),
        grid=(tiles,),
        in_specs=[pl.BlockSpec((TM, d), lambda i: (i, 0)),
                  pl.BlockSpec((TM, d // 2), lambda i: (i, 0)),
                  pl.BlockSpec((TM, d // 2), lambda i: (i + tiles, 0)),
                  pl.BlockSpec((TM, TOP_K), lambda i: (i, 0)),
                  _const_spec((1, d)), _const_spec((1, d))],
        out_specs=pl.BlockSpec((TM, d), lambda i: (i, 0)),
        compiler_params=_params(),
        name="combine",
    )(x, yg, yg, gates, g.reshape(1, -1), b.reshape(1, -1))


def _moe(x, x_packed, table, gates, counts, layer, w1, w3, w2, g, b):
    n, d = x.shape
    n_blocks = (n * TOP_K + N_EXPERTS * (EXPERT_BLOCK - 1) + EXPERT_BLOCK - 1) // EXPERT_BLOCK
    experts = table[:TOP_K].astype(I32)
    ranks = table[TOP_K:2 * TOP_K].astype(I32)

    blocks_e = (counts + EXPERT_BLOCK - 1) // EXPERT_BLOCK
    blocks_end = jnp.cumsum(blocks_e)
    run_start = (blocks_end - blocks_e) * EXPERT_BLOCK
    n_used = blocks_end[-1:].astype(I32)
    block_ids = jnp.arange(n_blocks, dtype=I32)
    block_expert = jnp.minimum(jnp.sum(blocks_end[None, :] <= block_ids[:, None], axis=1),
                               N_EXPERTS - 1).astype(I32)
    run_start_flag = jnp.concatenate([jnp.ones((1,), I32),
                                      (block_expert[1:] != block_expert[:-1]).astype(I32)])
    slot = (jnp.cumsum(run_start_flag) - 1) % 2
    next_block = blocks_end[block_expert]
    next_expert = jnp.where(next_block < n_used[0],
                            block_expert[jnp.minimum(next_block, n_blocks - 1)], -1)
    schedule = jnp.stack([block_expert, slot, run_start_flag, next_expert]).astype(I32)
    expert_ids = jnp.arange(N_EXPERTS, dtype=I32)
    start_of = jnp.sum(jnp.where(experts[:, :, None] == expert_ids, run_start, 0), axis=-1)
    dest = (start_of + ranks).astype(I32).reshape(-1)
    row_token = _sc_row_tokens(dest, n_blocks * EXPERT_BLOCK, n)

    xs = _sc_gather(x_packed, row_token)
    y = _expert_mlp(xs, schedule, n_used, layer, w1, w3, w2, n_blocks)
    yg = _sc_gather(y, dest)
    return _combine(x, yg, gates, g, b)


def kernel(x, mem, w_in_even, w_pool, pool_scale, ln_v_g, ln_v_b, w_spatial, b_spatial, w_out_even,
           w_in_odd, conv_w, conv_b, w_out_odd, wq_x, wk_x, wv_x, wo_x, ln_g, ln_b, wr_group,
           br_group, wr_expert, br_expert, w1, w3, w2):
    bsz, seq, d = x.shape
    assert seq % TM == 0 and d % LANES == 0
    mlen = mem.shape[1]
    k_all, v_all = _memory_kv(mem.reshape(bsz * mlen, d), wk_x, wv_x)
    k_all = k_all.reshape(DEPTH, bsz, mlen, d)
    v_all = v_all.reshape(DEPTH, bsz, mlen, d)
    h = x.reshape(bsz * seq, d)
    for l in range(DEPTH):
        i = l // 2
        if l % 2 == 0:
            h = _even_mixer(h, seq, i, w_in_even, w_pool[i], pool_scale[i], ln_v_g[i], ln_v_b[i],
                            w_spatial[i], b_spatial[i], w_out_even, ln_g[l, 0], ln_b[l, 0])
        else:
            h = _odd_mixer(h, seq, i, w_in_odd, conv_w[i], conv_b[i], w_out_odd, ln_g[l, 0], ln_b[l, 0])
        router_w, router_b = _router_weights(wr_group[l], br_group[l], wr_expert[l], br_expert[l])
        h, hp, table, gates, counts = _cross_attn(h, seq, l, k_all[l], v_all[l], wq_x, wo_x,
                                                  ln_g[l, 1], ln_b[l, 1], router_w, router_b)
        h = _moe(h, hp, table, gates, counts, l, w1, w3, w2, ln_g[l, 2], ln_b[l, 2])
    return h.reshape(bsz, seq, d)
```

```python
import functools
import math

import jax
import jax.numpy as jnp
from jax import lax
from jax.experimental import pallas as pl
from jax.experimental.pallas import tpu as pltpu
from jax.experimental.pallas import tpu_sc as plsc

F32 = jnp.float32
BF16 = jnp.bfloat16
I32 = jnp.int32

POOL_WINDOWS = (2, 4, 8, 16)
N_SG_HEADS = 4
CHUNK = 128
CONV_WIDTH = 3
N_XHEADS = 4
N_GROUPS = 4
EXPERTS_PER_GROUP = 8
N_EXPERTS = N_GROUPS * EXPERTS_PER_GROUP
TOP_K = 2
DEPTH = 4
ALPHA = (2.0 * DEPTH) ** 0.25
LN_EPS = 1e-5

LANES = 128
SC_CORES = 2
SC_WORKERS = 32
SC_LANES = 16
SC_CHUNK = 64
TM = 1024
SUB_TILES = 2
ROUTE_ROWS = 8
POOL_HALO = 16
CONV_HALO = 8
EXPERT_BLOCK = 512
ROUTE_COLS = 128
STAGE_COLS = 512
VMEM_LIMIT = 56 * 1024 * 1024

_NT = (((1,), (1,)), ((), ()))


def _dot(a, b):
    return jnp.dot(a, b, preferred_element_type=F32)


def _layer_norm(y, g, b):
    mu = jnp.mean(y, axis=-1, keepdims=True)
    yc = y - mu
    var = jnp.mean(yc * yc, axis=-1, keepdims=True)
    return yc * lax.rsqrt(var + LN_EPS) * g + b


def _gelu_tanh(x):
    c = math.sqrt(2.0 / math.pi)
    return 0.5 * x * (1.0 + jnp.tanh(c * (x + 0.044715 * (x * x * x))))


def _pack_halves(v):
    c = v.shape[1] // 2
    lo = pltpu.bitcast(v[:, :c].astype(BF16).astype(F32), jnp.uint32)
    hi = pltpu.bitcast(v[:, c:].astype(BF16).astype(F32), jnp.uint32)
    return pltpu.bitcast((hi & jnp.uint32(0xFFFF0000)) | (lo >> 16), I32)


def _unpack_halves(w):
    u = pltpu.bitcast(w, jnp.uint32)
    return pltpu.bitcast(u << 16, F32), pltpu.bitcast(u & jnp.uint32(0xFFFF0000), F32)


def _load_cast(w_hbm, w_scr, stage, sems):
    chunks = w_scr.shape[1] // STAGE_COLS

    def chunk_copy(c):
        return pltpu.make_async_copy(w_hbm.at[:, pl.ds(c * STAGE_COLS, STAGE_COLS)], stage.at[c % 2],
                                     sems.at[c % 2])

    chunk_copy(0).start()
    for c in range(chunks):
        if c + 1 < chunks:
            chunk_copy(c + 1).start()
        chunk_copy(c).wait()
        w_scr[:, c * STAGE_COLS:(c + 1) * STAGE_COLS] = stage[c % 2].astype(BF16)


def _stage_scratch(rows):
    return [pltpu.VMEM((2, rows, STAGE_COLS), F32), pltpu.SemaphoreType.DMA((2,))]


_HBM = pl.BlockSpec(memory_space=pl.ANY)


def _const_spec(shape):
    nd = len(shape)
    return pl.BlockSpec(shape, lambda i: (0,) * nd)


def _params():
    return pltpu.CompilerParams(dimension_semantics=("arbitrary",), vmem_limit_bytes=VMEM_LIMIT)


def _moe_output(x_ref, y0_ref, y1_ref, gate_ref, g_ref, b_ref):
    gates = gate_ref[...]
    g0, g1 = gates[:, 0:1], gates[:, 1:2]
    y0_lo, y0_hi = _unpack_halves(y0_ref[...])
    y1_lo, y1_hi = _unpack_halves(y1_ref[...])
    ff = jnp.concatenate([g0 * y0_lo + g1 * y1_lo, g0 * y0_hi + g1 * y1_hi], axis=1)
    return _layer_norm(ALPHA * x_ref[...] + ff, g_ref[...], b_ref[...])


def _mixer_input(src, pending):
    if not pending:
        x_ref, xh_ref = src
        return x_ref[...], xh_ref[...]
    x_ref, xh_ref, y0_ref, y0h_ref, y1_ref, y1h_ref, gate_ref, gateh_ref, g_ref, b_ref = src
    return (_moe_output(x_ref, y0_ref, y1_ref, gate_ref, g_ref, b_ref),
            _moe_output(xh_ref, y0h_ref, y1h_ref, gateh_ref, g_ref, b_ref))


def _mixer_sources(x, pending, halo):
    n, d = x.shape
    tiles = n // TM
    halo_blocks = TM // halo

    def halo_index(i):
        return jnp.maximum(i * halo_blocks - 1, 0)

    specs = [pl.BlockSpec((TM, d), lambda i: (i, 0)), pl.BlockSpec((halo, d), lambda i: (halo_index(i), 0))]
    args = [x, x]
    if pending is not None:
        yg, gates, g, b = pending
        specs += [pl.BlockSpec((TM, d // 2), lambda i: (i, 0)),
                  pl.BlockSpec((halo, d // 2), lambda i: (halo_index(i), 0)),
                  pl.BlockSpec((TM, d // 2), lambda i: (i + tiles, 0)),
                  pl.BlockSpec((halo, d // 2), lambda i: (halo_index(i) + tiles * halo_blocks, 0)),
                  pl.BlockSpec((TM, TOP_K), lambda i: (i, 0)),
                  pl.BlockSpec((halo, TOP_K), lambda i: (halo_index(i), 0)),
                  _const_spec((1, d)), _const_spec((1, d))]
        args += [yg, yg, yg, yg, gates, gates, g.reshape(1, -1), b.reshape(1, -1)]
    return specs, args


def _even_kernel(*refs, tiles_per_seq, layer, pending):
    n_src = 10 if pending else 2
    (win_hbm, wpool_ref, pscale_ref, lvg_ref, lvb_ref, ws_ref, bst_ref, wout_hbm, g_ref, b_ref, o_ref,
     a_scr, cat_scr, win_ref, wout_ref, stage, sems) = refs[n_src:]
    tm = o_ref.shape[0]
    d_pool = a_scr.shape[1]
    d_sg = lvg_ref.shape[1]
    pgd = d_pool // len(POOL_WINDOWS)
    hd_dim = d_sg // N_SG_HEADS
    seq_tile = pl.program_id(0) % tiles_per_seq

    @pl.when(pl.program_id(0) == 0)
    def _():
        _load_cast(win_hbm.at[layer], win_ref, stage, sems)
        _load_cast(wout_hbm.at[layer], wout_ref, stage, sems)

    x, xh = _mixer_input(refs[:n_src], pending)
    h = _dot(x.astype(BF16), win_ref[...])

    ah = _dot(xh.astype(BF16), win_ref[:, :d_pool])
    a_scr[0:POOL_HALO, :] = jnp.where(seq_tile == 0, 0.0, ah)
    a_scr[POOL_HALO:POOL_HALO + tm, :] = h[:, :d_pool]
    pos = seq_tile * tm + lax.broadcasted_iota(I32, (tm, 1), 0)
    for g, w in enumerate(POOL_WINDOWS):
        cs = slice(g * pgd, (g + 1) * pgd)
        tok = a_scr[POOL_HALO:POOL_HALO + tm, cs]
        acc = tok
        for j in range(1, w):
            acc = acc + a_scr[POOL_HALO - j:POOL_HALO - j + tm, cs]
        cnt = jnp.minimum(pos + 1, w).astype(F32)
        d = acc * (1.0 / cnt) - tok
        yg = _dot(d.astype(BF16), wpool_ref[g])
        cat_scr[:, cs] = (yg * pscale_ref[:, cs]).astype(BF16)

    z = _gelu_tanh(h[:, d_pool:])
    u = z[:, :d_sg]
    v = _layer_norm(z[:, d_sg:], lvg_ref[...], lvb_ref[...]).astype(BF16)
    row = lax.broadcasted_iota(I32, (CHUNK, CHUNK), 0)
    col = lax.broadcasted_iota(I32, (CHUNK, CHUNK), 1)
    for hd in range(N_SG_HEADS):
        hs = slice(hd * hd_dim, (hd + 1) * hd_dim)
        wsm = jnp.where(row >= col, ws_ref[hd], 0.0).astype(BF16)
        bcol = bst_ref[:, hd:hd + 1]
        for ck in range(tm // CHUNK):
            rs = slice(ck * CHUNK, (ck + 1) * CHUNK)
            sv = _dot(wsm, v[rs, hs]) + bcol
            cat_scr[rs, d_pool + hd * hd_dim:d_pool + (hd + 1) * hd_dim] = (u[rs, hs] * sv).astype(BF16)

    mix = _dot(cat_scr[...], wout_ref[...])
    o_ref[...] = _layer_norm(ALPHA * x + mix, g_ref[...], b_ref[...])


def _even_mixer(x, pending, seq, layer, w_in, w_pool, pool_scale, ln_v_g, ln_v_b, w_spatial, b_spatial,
                w_out, g, b):
    n, d = x.shape
    d_in = w_in.shape[2]
    d_pool = pool_scale.shape[0]
    d_sg = ln_v_g.shape[0]
    kern = functools.partial(_even_kernel, tiles_per_seq=seq // TM, layer=layer, pending=pending is not None)
    src_specs, src_args = _mixer_sources(x, pending, POOL_HALO)
    return pl.pallas_call(
        kern,
        out_shape=jax.ShapeDtypeStruct((n, d), F32),
        grid=(n // TM,),
        in_specs=src_specs + [
            _HBM,
            _const_spec(w_pool.shape),
            _const_spec((1, d_pool)),
            _const_spec((1, d_sg)),
            _const_spec((1, d_sg)),
            _const_spec(w_spatial.shape),
            _const_spec((CHUNK, N_SG_HEADS)),
            _HBM,
            _const_spec((1, d)),
            _const_spec((1, d)),
        ],
        out_specs=pl.BlockSpec((TM, d), lambda i: (i, 0)),
        scratch_shapes=[pltpu.VMEM((POOL_HALO + TM, d_pool), F32), pltpu.VMEM((TM, d_pool + d_sg), BF16),
                        pltpu.VMEM((d, d_in), BF16), pltpu.VMEM((d_pool + d_sg, d), BF16)] + _stage_scratch(d),
        compiler_params=_params(),
        name="even_mixer",
    )(*src_args, w_in, w_pool.astype(BF16), pool_scale.reshape(1, -1), ln_v_g.reshape(1, -1),
      ln_v_b.reshape(1, -1), w_spatial, b_spatial.T, w_out, g.reshape(1, -1), b.reshape(1, -1))


def _odd_kernel(*refs, tiles_per_seq, layer, pending):
    n_src = 10 if pending else 2
    (win_hbm, cwt_ref, cb_ref, wout_hbm, g_ref, b_ref, o_ref, zc_scr,
     win_ref, wout_ref, stage, sems) = refs[n_src:]
    tm, d = o_ref.shape
    seq_tile = pl.program_id(0) % tiles_per_seq

    @pl.when(pl.program_id(0) == 0)
    def _():
        _load_cast(win_hbm.at[layer], win_ref, stage, sems)
        _load_cast(wout_hbm.at[layer], wout_ref, stage, sems)

    x, xh = _mixer_input(refs[:n_src], pending)
    xb = x.astype(BF16)
    hc = _dot(xb, win_ref[:, d:2 * d])
    hz = _dot(xb, win_ref[:, 2 * d:])
    hh = _dot(xh.astype(BF16), win_ref[:, d:])
    zc_scr[0:CONV_HALO, :] = jnp.where(seq_tile == 0, 0.0, hh[:, :d] * hh[:, d:])
    zc_scr[CONV_HALO:CONV_HALO + tm, :] = hc * hz
    gate = _dot(xb, win_ref[:, :d])
    conv = cb_ref[...]
    for j in range(CONV_WIDTH):
        off = CONV_HALO - (CONV_WIDTH - 1) + j
        conv = conv + zc_scr[off:off + tm, :] * cwt_ref[j:j + 1, :]
    y = _dot((gate * conv).astype(BF16), wout_ref[...])
    o_ref[...] = _layer_norm(ALPHA * x + y, g_ref[...], b_ref[...])


def _odd_mixer(x, pending, seq, layer, w_in, conv_w, conv_b, w_out, g, b):
    n, d = x.shape
    kern = functools.partial(_odd_kernel, tiles_per_seq=seq // TM, layer=layer, pending=pending is not None)
    src_specs, src_args = _mixer_sources(x, pending, CONV_HALO)
    return pl.pallas_call(
        kern,
        out_shape=jax.ShapeDtypeStruct((n, d), F32),
        grid=(n // TM,),
        in_specs=src_specs + [
            _HBM,
            _const_spec((CONV_WIDTH, d)),
            _const_spec((1, d)),
            _HBM,
            _const_spec((1, d)),
            _const_spec((1, d)),
        ],
        out_specs=pl.BlockSpec((TM, d), lambda i: (i, 0)),
        scratch_shapes=[pltpu.VMEM((CONV_HALO + TM, d), F32),
                        pltpu.VMEM(w_in.shape[1:], BF16), pltpu.VMEM(w_out.shape[1:], BF16)] + _stage_scratch(d),
        compiler_params=_params(),
        name="odd_mixer",
    )(*src_args, w_in, conv_w.T, conv_b.reshape(1, -1), w_out, g.reshape(1, -1), b.reshape(1, -1))


def _kv_kernel(mem_ref, wk_ref, wv_ref, k_ref, v_ref):
    m = mem_ref[...].astype(BF16)
    k_ref[...] = _dot(m, wk_ref[...].astype(BF16)).astype(BF16)
    v_ref[...] = _dot(m, wv_ref[...].astype(BF16)).astype(BF16)


def _memory_kv(mem2d, wk, wv):
    nl, d, _ = wk.shape
    rows = mem2d.shape[0]
    out = jax.ShapeDtypeStruct((nl, rows, d), BF16)
    wspec = pl.BlockSpec((None, d, d), lambda l: (l, 0, 0))
    ospec = pl.BlockSpec((None, rows, d), lambda l: (l, 0, 0))
    return pl.pallas_call(
        _kv_kernel,
        out_shape=(out, out),
        grid=(nl,),
        in_specs=[_const_spec((rows, d)), wspec, wspec],
        out_specs=(ospec, ospec),
        compiler_params=_params(),
        name="memory_kv",
    )(mem2d, wk, wv)


def _attn_kernel(x_ref, k_ref, v_ref, wq_hbm, wo_hbm, g_ref, b_ref, wr_ref, br_ref,
                 o_ref, op_ref, rt_ref, rg_ref, cnt_ref, o_scr, carry_scr, wq_ref, wo_ref, stage, sems,
                 *, layer):
    tm, d = x_ref.shape
    hd_dim = d // N_XHEADS
    sub = tm // SUB_TILES

    @pl.when(pl.program_id(0) == 0)
    def _():
        carry_scr[...] = jnp.zeros_like(carry_scr)
        _load_cast(wq_hbm.at[layer], wq_ref, stage, sems)
        _load_cast(wo_hbm.at[layer], wo_ref, stage, sems)

    for st in range(SUB_TILES):
        rs = slice(st * sub, (st + 1) * sub)
        x = x_ref[rs, :]
        q = _dot(x.astype(BF16), wq_ref[...]) * (1.0 / math.sqrt(hd_dim))
        for hd in range(N_XHEADS):
            hs = slice(hd * hd_dim, (hd + 1) * hd_dim)
            s = lax.dot_general(q[:, hs].astype(BF16), k_ref[:, hs], _NT, preferred_element_type=F32)
            p = jnp.exp(s - jnp.max(s, axis=-1, keepdims=True))
            p = p * (1.0 / jnp.sum(p, axis=-1, keepdims=True))
            o_scr[rs, hs] = _dot(p.astype(BF16), v_ref[:, hs]).astype(BF16)
        xa = _dot(o_scr[rs, :], wo_ref[...])
        out = _layer_norm(ALPHA * x + xa, g_ref[...], b_ref[...])
        o_ref[rs, :] = out
        op_ref[rs, :] = _pack_halves(out)
        table, gates = _route_rows(out, wr_ref, br_ref, carry_scr)
        rt_ref[:, rs] = table
        rg_ref[rs, :] = gates
    cnt_ref[...] = carry_scr[...].astype(I32)


def _cross_attn(x, seq, layer, k, v, wq, wo, g, b, router_w, router_b):
    n, d = x.shape
    m = k.shape[1]
    tiles_per_seq = seq // TM
    kvspec = pl.BlockSpec((None, m, d), lambda i: (i // tiles_per_seq, 0, 0))
    out, packed, table, gates, cnt = pl.pallas_call(
        functools.partial(_attn_kernel, layer=layer),
        out_shape=(jax.ShapeDtypeStruct((n, d), F32), jax.ShapeDtypeStruct((n, d // 2), I32),
                   jax.ShapeDtypeStruct((ROUTE_ROWS, n), F32), jax.ShapeDtypeStruct((n, TOP_K), F32),
                   jax.ShapeDtypeStruct((1, ROUTE_COLS), I32)),
        grid=(n // TM,),
        in_specs=[
            pl.BlockSpec((TM, d), lambda i: (i, 0)),
            kvspec, kvspec,
            _HBM, _HBM,
            _const_spec((1, d)), _const_spec((1, d)),
            _const_spec((d, 2 * ROUTE_COLS)), _const_spec((1, ROUTE_COLS)),
        ],
        out_specs=(pl.BlockSpec((TM, d), lambda i: (i, 0)), pl.BlockSpec((TM, d // 2), lambda i: (i, 0)),
                   pl.BlockSpec((ROUTE_ROWS, TM), lambda i: (0, i)), pl.BlockSpec((TM, TOP_K), lambda i: (i, 0)),
                   _const_spec((1, ROUTE_COLS))),
        scratch_shapes=[pltpu.VMEM((TM, d), BF16), pltpu.VMEM((1, ROUTE_COLS), F32),
                        pltpu.VMEM((d, d), BF16), pltpu.VMEM((d, d), BF16)] + _stage_scratch(d),
        compiler_params=_params(),
        name="cross_attn",
    )(x, k, v, wq, wo, g.reshape(1, -1), b.reshape(1, -1), router_w, router_b)
    return out, packed, table, gates, cnt[0, N_GROUPS:N_GROUPS + N_EXPERTS]


def _route_rows(x, wcat_ref, bias_ref, carry_scr):
    tm = x.shape[0]

    xh = x.astype(BF16)
    xl = (x - xh.astype(F32)).astype(BF16)
    r1 = _dot(xh, wcat_ref[...])
    r2 = _dot(xl, wcat_ref[:, :ROUTE_COLS])
    logits = r1[:, :ROUTE_COLS] + r1[:, ROUTE_COLS:] + r2 + bias_ref[...]

    lane = lax.broadcasted_iota(I32, (tm, ROUTE_COLS), 1).astype(F32)
    neg = -jnp.inf

    def first_argmax(vals):
        mx = jnp.max(vals, axis=-1, keepdims=True)
        idx = jnp.min(jnp.where(vals == mx, lane, float(ROUTE_COLS)), axis=-1, keepdims=True)
        return mx, idx

    gl = jnp.where(lane < N_GROUPS, logits, neg)
    gmax, g_sel = first_argmax(gl)
    gate_g = 1.0 / jnp.sum(jnp.exp(gl - gmax), axis=-1, keepdims=True)

    lo = N_GROUPS + g_sel * EXPERTS_PER_GROUP
    el = jnp.where((lane >= lo) & (lane < lo + EXPERTS_PER_GROUP), logits, neg)
    m1, i1 = first_argmax(el)
    m2, i2 = first_argmax(jnp.where(lane == i1, neg, el))
    e21 = jnp.exp(m2 - m1)
    w1 = 1.0 / (1.0 + e21)
    w2 = e21 / (1.0 + e21)

    oh1 = lane == i1
    oh2 = lane == i2
    oh = (oh1 | oh2).astype(BF16)
    r = lax.broadcasted_iota(I32, (tm, tm), 0)
    c = lax.broadcasted_iota(I32, (tm, tm), 1)
    before = _dot((r > c).astype(BF16), oh) + carry_scr[...]
    rank1 = jnp.sum(jnp.where(oh1, before, 0.0), axis=-1, keepdims=True)
    rank2 = jnp.sum(jnp.where(oh2, before, 0.0), axis=-1, keepdims=True)
    carry_scr[...] += jnp.sum(oh.astype(F32), axis=0, keepdims=True)

    cols = jnp.where(lane == 0.0, i1 - N_GROUPS,
                     jnp.where(lane == 1.0, i2 - N_GROUPS,
                               jnp.where(lane == 2.0, rank1, jnp.where(lane == 3.0, rank2, 0.0))))
    table = jnp.transpose(cols)[:ROUTE_ROWS, :]
    l2 = lax.broadcasted_iota(I32, (tm, TOP_K), 1)
    gates = jnp.where(l2 == 0, gate_g * w1, gate_g * w2)
    return table, gates


def _router_weights(wr_g, br_g, wr_e, br_e):
    d = wr_g.shape[0]
    w = jnp.concatenate([wr_g, jnp.transpose(wr_e, (1, 0, 2)).reshape(d, N_EXPERTS)], axis=1)
    w = jnp.pad(w, ((0, 0), (0, ROUTE_COLS - w.shape[1])))
    w_hi = w.astype(BF16)
    w_lo = (w - w_hi.astype(F32)).astype(BF16)
    bias = jnp.pad(jnp.concatenate([br_g, br_e.reshape(-1)]), (0, ROUTE_COLS - N_GROUPS - N_EXPERTS))
    return jnp.concatenate([w_hi, w_lo], axis=1), bias.reshape(1, -1)


def _sc_gather(table, idx):
    b = idx.shape[0]
    d = table.shape[1]
    per_worker = b // SC_WORKERS
    n_chunks = per_worker // SC_CHUNK
    assert per_worker * SC_WORKERS == b and n_chunks * SC_CHUNK == per_worker and n_chunks % 2 == 0
    mesh = plsc.VectorSubcoreMesh(core_axis_name="c", subcore_axis_name="s")

    @functools.partial(
        pl.kernel, mesh=mesh,
        out_type=jax.ShapeDtypeStruct((b, d), table.dtype),
        scratch_types=[pltpu.VMEM((per_worker,), I32),
                       pltpu.VMEM((2, SC_CHUNK, d), table.dtype),
                       pltpu.SemaphoreType.DMA((2,)),
                       pltpu.SemaphoreType.DMA((2,))],
        name="sc_gather",
    )
    def gather(table_hbm, idx_hbm, out_hbm, idx_v, rows_v, gsem, wsem):
        wid = lax.axis_index("s") * SC_CORES + lax.axis_index("c")
        base = wid * per_worker
        pltpu.sync_copy(idx_hbm.at[pl.ds(base, per_worker)], idx_v)

        def fetch(c, slot):
            off = pl.multiple_of(c * SC_CHUNK, SC_CHUNK)
            return pltpu.make_async_copy(table_hbm.at[idx_v.at[pl.ds(off, SC_CHUNK)]], rows_v.at[slot],
                                         gsem.at[slot])

        def put(c, slot):
            off = pl.multiple_of(c * SC_CHUNK, SC_CHUNK)
            return pltpu.make_async_copy(rows_v.at[slot], out_hbm.at[pl.ds(base + off, SC_CHUNK)],
                                         wsem.at[slot])

        fetch(0, 0).start()

        @pl.loop(0, n_chunks, step=2)
        def _(c0):
            for slot in range(2):
                c = c0 + slot

                @pl.when(c + 1 < n_chunks)
                def _():
                    @pl.when(c >= 1)
                    def _():
                        put(c - 1, 1 - slot).wait()
                    fetch(c + 1, 1 - slot).start()

                fetch(c, slot).wait()
                put(c, slot).start()

        put(n_chunks - 2, 0).wait()
        put(n_chunks - 1, 1).wait()

    return gather(table, idx)


def _sc_row_tokens(dest_flat, rows, n):
    a = dest_flat.shape[0]
    lanes = SC_LANES
    assert a % lanes == 0 and rows % lanes == 0
    mesh = plsc.VectorSubcoreMesh(core_axis_name="c", subcore_axis_name="s")

    @functools.partial(
        pl.kernel, mesh=mesh,
        out_type=jax.ShapeDtypeStruct((rows,), I32),
        scratch_types=[pltpu.VMEM((a,), I32), pltpu.VMEM((rows,), I32)],
        compiler_params=pltpu.CompilerParams(needs_layout_passes=False),
        name="sc_row_tokens",
    )
    def invert(dest_hbm, out_hbm, dest_v, map_v):
        @pl.when((lax.axis_index("s") == 0) & (lax.axis_index("c") == 0))
        def _():
            pltpu.sync_copy(dest_hbm, dest_v)
            lane = lax.iota(I32, lanes)

            @pl.loop(0, rows // lanes)
            def _(i):
                map_v[pl.ds(i * lanes, lanes)] = lax.rem(i * lanes + lane, n)

            @pl.loop(0, a // lanes)
            def _(i):
                plsc.store_scatter(map_v, [dest_v[pl.ds(i * lanes, lanes)]], lax.rem(i * lanes + lane, n))

            pltpu.sync_copy(map_v, out_hbm)

    return invert(dest_flat)


def _expert_kernel(sched_ref, nb_ref, xs_ref, w1_hbm, w3_hbm, w2_hbm, y_ref,
                   w1_buf, w3_buf, w2_buf, w1_scr, w3_scr, w2_scr, sems, *, layer):
    b = pl.program_id(0)
    used = b < nb_ref[0]
    expert, slot, run_start, next_expert = (sched_ref[r, b] for r in range(4))

    def fetch(e, s):
        return [pltpu.make_async_copy(w_hbm.at[layer, e], buf.at[s], sems.at[s, j])
                for j, (w_hbm, buf) in enumerate(((w1_hbm, w1_buf), (w3_hbm, w3_buf), (w2_hbm, w2_buf)))]

    @pl.when(used & (run_start == 1))
    def _():
        @pl.when(b == 0)
        def _():
            for c in fetch(expert, slot):
                c.start()

        for c in fetch(expert, slot):
            c.wait()

        @pl.when(next_expert >= 0)
        def _():
            for c in fetch(next_expert, 1 - slot):
                c.start()

        w1_scr[...] = w1_buf[slot].astype(BF16)
        w3_scr[...] = w3_buf[slot].astype(BF16)
        w2_scr[...] = w2_buf[slot].astype(BF16)

    @pl.when(used)
    def _():
        x_lo, x_hi = _unpack_halves(xs_ref[...])
        xb = jnp.concatenate([x_lo.astype(BF16), x_hi.astype(BF16)], axis=1)
        h1 = _dot(xb, w1_scr[...])
        h3 = _dot(xb, w3_scr[...])
        hid = h1 * (1.0 / (1.0 + jnp.exp(-h1))) * h3
        y_ref[...] = _pack_halves(_dot(hid.astype(BF16), w2_scr[...]))

    @pl.when(jnp.logical_not(used))
    def _():
        y_ref[...] = jnp.zeros_like(y_ref)


def _expert_mlp(xs, schedule, n_used, layer, w1, w3, w2, n_blocks):
    d, de = w1.shape[2], w1.shape[3]

    def row_map(b, sched, nb):
        return (jnp.minimum(b, nb[0] - 1), 0)

    return pl.pallas_call(
        functools.partial(_expert_kernel, layer=layer),
        out_shape=jax.ShapeDtypeStruct((n_blocks * EXPERT_BLOCK, d // 2), I32),
        grid_spec=pltpu.PrefetchScalarGridSpec(
            num_scalar_prefetch=2,
            grid=(n_blocks,),
            in_specs=[pl.BlockSpec((EXPERT_BLOCK, d // 2), row_map),
                      pl.BlockSpec(memory_space=pl.ANY),
                      pl.BlockSpec(memory_space=pl.ANY),
                      pl.BlockSpec(memory_space=pl.ANY)],
            out_specs=pl.BlockSpec((EXPERT_BLOCK, d // 2), lambda b, sched, nb: (b, 0)),
            scratch_shapes=[pltpu.VMEM((2, d, de), F32), pltpu.VMEM((2, d, de), F32), pltpu.VMEM((2, de, d), F32),
                            pltpu.VMEM((d, de), BF16), pltpu.VMEM((d, de), BF16), pltpu.VMEM((de, d), BF16),
                            pltpu.SemaphoreType.DMA((2, 3))],
        ),
        compiler_params=_params(),
        name="expert_mlp",
    )(schedule, n_used, xs, w1, w3, w2)


def _combine_kernel(x_ref, y0_ref, y1_ref, gate_ref, g_ref, b_ref, o_ref):
    o_ref[...] = _moe_output(x_ref, y0_ref, y1_ref, gate_ref, g_ref, b_ref)


def _combine(x, yg, gates, g, b):
    n, d = x.shape
    tiles = n // TM
    return pl.pallas_call(
        _combine_kernel,
        out_shape=jax.ShapeDtypeStruct((n, d), F32),
        grid=(tiles,),
        in_specs=[pl.BlockSpec((TM, d), lambda i: (i, 0)),
                  pl.BlockSpec((TM, d // 2), lambda i: (i, 0)),
                  pl.BlockSpec((TM, d // 2), lambda i: (i + tiles, 0)),
                  pl.BlockSpec((TM, TOP_K), lambda i: (i, 0)),
                  _const_spec((1, d)), _const_spec((1, d))],
        out_specs=pl.BlockSpec((TM, d), lambda i: (i, 0)),
        compiler_params=_params(),
        name="combine",
    )(x, yg, yg, gates, g.reshape(1, -1), b.reshape(1, -1))


def _moe_experts(x_packed, table, counts, layer, w1, w3, w2):
    n = x_packed.shape[0]
    n_blocks = (n * TOP_K + N_EXPERTS * (EXPERT_BLOCK - 1) + EXPERT_BLOCK - 1) // EXPERT_BLOCK
    experts = table[:TOP_K].astype(I32)
    ranks = table[TOP_K:2 * TOP_K].astype(I32)

    blocks_e = (counts + EXPERT_BLOCK - 1) // EXPERT_BLOCK
    blocks_end = jnp.cumsum(blocks_e)
    run_start = (blocks_end - blocks_e) * EXPERT_BLOCK
    n_used = blocks_end[-1:].astype(I32)
    block_ids = jnp.arange(n_blocks, dtype=I32)
    block_expert = jnp.minimum(jnp.sum(blocks_end[None, :] <= block_ids[:, None], axis=1),
                               N_EXPERTS - 1).astype(I32)
    run_start_flag = jnp.concatenate([jnp.ones((1,), I32),
                                      (block_expert[1:] != block_expert[:-1]).astype(I32)])
    slot = (jnp.cumsum(run_start_flag) - 1) % 2
    next_block = blocks_end[block_expert]
    next_expert = jnp.where(next_block < n_used[0],
                            block_expert[jnp.minimum(next_block, n_blocks - 1)], -1)
    schedule = jnp.stack([block_expert, slot, run_start_flag, next_expert]).astype(I32)
    expert_ids = jnp.arange(N_EXPERTS, dtype=I32)
    start_of = jnp.sum(jnp.where(experts[:, :, None] == expert_ids, run_start, 0), axis=-1)
    dest = (start_of + ranks).astype(I32).reshape(-1)
    row_token = _sc_row_tokens(dest, n_blocks * EXPERT_BLOCK, n)

    xs = _sc_gather(x_packed, row_token)
    y = _expert_mlp(xs, schedule, n_used, layer, w1, w3, w2, n_blocks)
    return _sc_gather(y, dest)


def kernel(x, mem, w_in_even, w_pool, pool_scale, ln_v_g, ln_v_b, w_spatial, b_spatial, w_out_even,
           w_in_odd, conv_w, conv_b, w_out_odd, wq_x, wk_x, wv_x, wo_x, ln_g, ln_b, wr_group,
           br_group, wr_expert, br_expert, w1, w3, w2):
    bsz, seq, d = x.shape
    assert seq % TM == 0 and d % LANES == 0
    mlen = mem.shape[1]
    k_all, v_all = _memory_kv(mem.reshape(bsz * mlen, d), wk_x, wv_x)
    k_all = k_all.reshape(DEPTH, bsz, mlen, d)
    v_all = v_all.reshape(DEPTH, bsz, mlen, d)
    h = x.reshape(bsz * seq, d)
    pending = None
    for l in range(DEPTH):
        i = l // 2
        if l % 2 == 0:
            h = _even_mixer(h, pending, seq, i, w_in_even, w_pool[i], pool_scale[i], ln_v_g[i], ln_v_b[i],
                            w_spatial[i], b_spatial[i], w_out_even, ln_g[l, 0], ln_b[l, 0])
        else:
            h = _odd_mixer(h, pending, seq, i, w_in_odd, conv_w[i], conv_b[i], w_out_odd,
                           ln_g[l, 0], ln_b[l, 0])
        router_w, router_b = _router_weights(wr_group[l], br_group[l], wr_expert[l], br_expert[l])
        h, hp, table, gates, counts = _cross_attn(h, seq, l, k_all[l], v_all[l], wq_x, wo_x,
                                                  ln_g[l, 1], ln_b[l, 1], router_w, router_b)
        yg = _moe_experts(hp, table, counts, l, w1, w3, w2)
        pending = (yg, gates, ln_g[l, 2], ln_b[l, 2])
    return _combine(h, *pending).reshape(bsz, seq, d)
```

```python
import functools
import math

import jax
import jax.numpy as jnp
from jax import lax
from jax.experimental import pallas as pl
from jax.experimental.pallas import tpu as pltpu
from jax.experimental.pallas import tpu_sc as plsc

F32 = jnp.float32
BF16 = jnp.bfloat16
I32 = jnp.int32

POOL_WINDOWS = (2, 4, 8, 16)
N_SG_HEADS = 4
CHUNK = 128
CONV_WIDTH = 3
N_XHEADS = 4
N_GROUPS = 4
EXPERTS_PER_GROUP = 8
N_EXPERTS = N_GROUPS * EXPERTS_PER_GROUP
TOP_K = 2
DEPTH = 4
ALPHA = (2.0 * DEPTH) ** 0.25
LN_EPS = 1e-5

LANES = 128
SC_CORES = 2
SC_WORKERS = 32
SC_LANES = 16
SC_CHUNK = 64
TM = 1024
SUB_TILES = 2
ROUTE_ROWS = 8
POOL_HALO = 16
CONV_HALO = 8
EXPERT_BLOCK = 512
ROUTE_COLS = 128
STAGE_COLS = 512
VMEM_LIMIT = 56 * 1024 * 1024

_NT = (((1,), (1,)), ((), ()))


def _dot(a, b):
    return jnp.dot(a, b, preferred_element_type=F32)


def _layer_norm(y, g, b):
    mu = jnp.mean(y, axis=-1, keepdims=True)
    yc = y - mu
    var = jnp.mean(yc * yc, axis=-1, keepdims=True)
    return yc * lax.rsqrt(var + LN_EPS) * g + b


def _gelu_tanh(x):
    c = math.sqrt(2.0 / math.pi)
    return 0.5 * x * (1.0 + jnp.tanh(c * (x + 0.044715 * (x * x * x))))


def _pack_halves(v):
    c = v.shape[1] // 2
    lo = pltpu.bitcast(v[:, :c].astype(BF16).astype(F32), jnp.uint32)
    hi = pltpu.bitcast(v[:, c:].astype(BF16).astype(F32), jnp.uint32)
    return pltpu.bitcast((hi & jnp.uint32(0xFFFF0000)) | (lo >> 16), I32)


def _unpack_halves(w):
    u = pltpu.bitcast(w, jnp.uint32)
    return pltpu.bitcast(u << 16, F32), pltpu.bitcast(u & jnp.uint32(0xFFFF0000), F32)


def _load_cast(w_hbm, w_scr, stage, sems):
    chunks = w_scr.shape[1] // STAGE_COLS

    def chunk_copy(c):
        return pltpu.make_async_copy(w_hbm.at[:, pl.ds(c * STAGE_COLS, STAGE_COLS)], stage.at[c % 2],
                                     sems.at[c % 2])

    chunk_copy(0).start()
    for c in range(chunks):
        if c + 1 < chunks:
            chunk_copy(c + 1).start()
        chunk_copy(c).wait()
        w_scr[:, c * STAGE_COLS:(c + 1) * STAGE_COLS] = stage[c % 2].astype(BF16)


def _stage_scratch(rows):
    return [pltpu.VMEM((2, rows, STAGE_COLS), F32), pltpu.SemaphoreType.DMA((2,))]


_HBM = pl.BlockSpec(memory_space=pl.ANY)


def _const_spec(shape):
    nd = len(shape)
    return pl.BlockSpec(shape, lambda i: (0,) * nd)


def _params():
    return pltpu.CompilerParams(dimension_semantics=("arbitrary",), vmem_limit_bytes=VMEM_LIMIT)


def _moe_output(x_ref, y0_ref, y1_ref, gate_ref, g_ref, b_ref):
    gates = gate_ref[...]
    g0, g1 = gates[:, 0:1], gates[:, 1:2]
    y0_lo, y0_hi = _unpack_halves(y0_ref[...])
    y1_lo, y1_hi = _unpack_halves(y1_ref[...])
    ff = jnp.concatenate([g0 * y0_lo + g1 * y1_lo, g0 * y0_hi + g1 * y1_hi], axis=1)
    return _layer_norm(ALPHA * x_ref[...] + ff, g_ref[...], b_ref[...])


def _mixer_input(src, pending):
    if not pending:
        x_ref, xh_ref = src
        return x_ref[...], xh_ref[...]
    x_ref, xh_ref, y0_ref, y0h_ref, y1_ref, y1h_ref, gate_ref, gateh_ref, g_ref, b_ref = src
    return (_moe_output(x_ref, y0_ref, y1_ref, gate_ref, g_ref, b_ref),
            _moe_output(xh_ref, y0h_ref, y1h_ref, gateh_ref, g_ref, b_ref))


def _mixer_sources(x, pending, halo):
    n, d = x.shape
    tiles = n // TM
    halo_blocks = TM // halo

    def halo_index(i):
        return jnp.maximum(i * halo_blocks - 1, 0)

    specs = [pl.BlockSpec((TM, d), lambda i: (i, 0)), pl.BlockSpec((halo, d), lambda i: (halo_index(i), 0))]
    args = [x, x]
    if pending is not None:
        yg, gates, g, b = pending
        specs += [pl.BlockSpec((TM, d // 2), lambda i: (i, 0)),
                  pl.BlockSpec((halo, d // 2), lambda i: (halo_index(i), 0)),
                  pl.BlockSpec((TM, d // 2), lambda i: (i + tiles, 0)),
                  pl.BlockSpec((halo, d // 2), lambda i: (halo_index(i) + tiles * halo_blocks, 0)),
                  pl.BlockSpec((TM, TOP_K), lambda i: (i, 0)),
                  pl.BlockSpec((halo, TOP_K), lambda i: (halo_index(i), 0)),
                  _const_spec((1, d)), _const_spec((1, d))]
        args += [yg, yg, yg, yg, gates, gates, g.reshape(1, -1), b.reshape(1, -1)]
    return specs, args


def _even_kernel(*refs, tiles_per_seq, layer, pending):
    n_src = 10 if pending else 2
    (win_hbm, wpool_ref, pscale_ref, lvg_ref, lvb_ref, ws_ref, bst_ref, wout_hbm, g_ref, b_ref, o_ref,
     a_scr, cat_scr, win_ref, wout_ref, stage, sems) = refs[n_src:]
    tm = o_ref.shape[0]
    d_pool = a_scr.shape[1]
    d_sg = lvg_ref.shape[1]
    pgd = d_pool // len(POOL_WINDOWS)
    hd_dim = d_sg // N_SG_HEADS
    seq_tile = pl.program_id(0) % tiles_per_seq

    @pl.when(pl.program_id(0) == 0)
    def _():
        _load_cast(win_hbm.at[layer], win_ref, stage, sems)
        _load_cast(wout_hbm.at[layer], wout_ref, stage, sems)

    x, xh = _mixer_input(refs[:n_src], pending)
    h = _dot(x.astype(BF16), win_ref[...])

    ah = _dot(xh.astype(BF16), win_ref[:, :d_pool])
    a_scr[0:POOL_HALO, :] = jnp.where(seq_tile == 0, 0.0, ah)
    a_scr[POOL_HALO:POOL_HALO + tm, :] = h[:, :d_pool]
    pos = seq_tile * tm + lax.broadcasted_iota(I32, (tm, 1), 0)
    for g, w in enumerate(POOL_WINDOWS):
        cs = slice(g * pgd, (g + 1) * pgd)
        tok = a_scr[POOL_HALO:POOL_HALO + tm, cs]
        acc = tok
        for j in range(1, w):
            acc = acc + a_scr[POOL_HALO - j:POOL_HALO - j + tm, cs]
        cnt = jnp.minimum(pos + 1, w).astype(F32)
        d = acc * (1.0 / cnt) - tok
        yg = _dot(d.astype(BF16), wpool_ref[g])
        cat_scr[:, cs] = (yg * pscale_ref[:, cs]).astype(BF16)

    z = _gelu_tanh(h[:, d_pool:])
    u = z[:, :d_sg]
    v = _layer_norm(z[:, d_sg:], lvg_ref[...], lvb_ref[...]).astype(BF16)
    row = lax.broadcasted_iota(I32, (CHUNK, CHUNK), 0)
    col = lax.broadcasted_iota(I32, (CHUNK, CHUNK), 1)
    for hd in range(N_SG_HEADS):
        hs = slice(hd * hd_dim, (hd + 1) * hd_dim)
        wsm = jnp.where(row >= col, ws_ref[hd], 0.0).astype(BF16)
        bcol = bst_ref[:, hd:hd + 1]
        for ck in range(tm // CHUNK):
            rs = slice(ck * CHUNK, (ck + 1) * CHUNK)
            sv = _dot(wsm, v[rs, hs]) + bcol
            cat_scr[rs, d_pool + hd * hd_dim:d_pool + (hd + 1) * hd_dim] = (u[rs, hs] * sv).astype(BF16)

    mix = _dot(cat_scr[...], wout_ref[...])
    o_ref[...] = _layer_norm(ALPHA * x + mix, g_ref[...], b_ref[...])


def _even_mixer(x, pending, seq, layer, w_in, w_pool, pool_scale, ln_v_g, ln_v_b, w_spatial, b_spatial,
                w_out, g, b):
    n, d = x.shape
    d_in = w_in.shape[2]
    d_pool = pool_scale.shape[0]
    d_sg = ln_v_g.shape[0]
    kern = functools.partial(_even_kernel, tiles_per_seq=seq // TM, layer=layer, pending=pending is not None)
    src_specs, src_args = _mixer_sources(x, pending, POOL_HALO)
    return pl.pallas_call(
        kern,
        out_shape=jax.ShapeDtypeStruct((n, d), F32),
        grid=(n // TM,),
        in_specs=src_specs + [
            _HBM,
            _const_spec(w_pool.shape),
            _const_spec((1, d_pool)),
            _const_spec((1, d_sg)),
            _const_spec((1, d_sg)),
            _const_spec(w_spatial.shape),
            _const_spec((CHUNK, N_SG_HEADS)),
            _HBM,
            _const_spec((1, d)),
            _const_spec((1, d)),
        ],
        out_specs=pl.BlockSpec((TM, d), lambda i: (i, 0)),
        scratch_shapes=[pltpu.VMEM((POOL_HALO + TM, d_pool), F32), pltpu.VMEM((TM, d_pool + d_sg), BF16),
                        pltpu.VMEM((d, d_in), BF16), pltpu.VMEM((d_pool + d_sg, d), BF16)] + _stage_scratch(d),
        compiler_params=_params(),
        name="even_mixer",
    )(*src_args, w_in, w_pool.astype(BF16), pool_scale.reshape(1, -1), ln_v_g.reshape(1, -1),
      ln_v_b.reshape(1, -1), w_spatial, b_spatial.T, w_out, g.reshape(1, -1), b.reshape(1, -1))


def _odd_kernel(*refs, tiles_per_seq, layer, pending):
    n_src = 10 if pending else 2
    (win_hbm, cwt_ref, cb_ref, wout_hbm, g_ref, b_ref, o_ref, zc_scr,
     win_ref, wout_ref, stage, sems) = refs[n_src:]
    tm, d = o_ref.shape
    seq_tile = pl.program_id(0) % tiles_per_seq

    @pl.when(pl.program_id(0) == 0)
    def _():
        _load_cast(win_hbm.at[layer], win_ref, stage, sems)
        _load_cast(wout_hbm.at[layer], wout_ref, stage, sems)

    x, xh = _mixer_input(refs[:n_src], pending)
    xb = x.astype(BF16)
    hc = _dot(xb, win_ref[:, d:2 * d])
    hz = _dot(xb, win_ref[:, 2 * d:])
    hh = _dot(xh.astype(BF16), win_ref[:, d:])
    zc_scr[0:CONV_HALO, :] = jnp.where(seq_tile == 0, 0.0, hh[:, :d] * hh[:, d:])
    zc_scr[CONV_HALO:CONV_HALO + tm, :] = hc * hz
    gate = _dot(xb, win_ref[:, :d])
    conv = cb_ref[...]
    for j in range(CONV_WIDTH):
        off = CONV_HALO - (CONV_WIDTH - 1) + j
        conv = conv + zc_scr[off:off + tm, :] * cwt_ref[j:j + 1, :]
    y = _dot((gate * conv).astype(BF16), wout_ref[...])
    o_ref[...] = _layer_norm(ALPHA * x + y, g_ref[...], b_ref[...])


def _odd_mixer(x, pending, seq, layer, w_in, conv_w, conv_b, w_out, g, b):
    n, d = x.shape
    kern = functools.partial(_odd_kernel, tiles_per_seq=seq // TM, layer=layer, pending=pending is not None)
    src_specs, src_args = _mixer_sources(x, pending, CONV_HALO)
    return pl.pallas_call(
        kern,
        out_shape=jax.ShapeDtypeStruct((n, d), F32),
        grid=(n // TM,),
        in_specs=src_specs + [
            _HBM,
            _const_spec((CONV_WIDTH, d)),
            _const_spec((1, d)),
            _HBM,
            _const_spec((1, d)),
            _const_spec((1, d)),
        ],
        out_specs=pl.BlockSpec((TM, d), lambda i: (i, 0)),
        scratch_shapes=[pltpu.VMEM((CONV_HALO + TM, d), F32),
                        pltpu.VMEM(w_in.shape[1:], BF16), pltpu.VMEM(w_out.shape[1:], BF16)] + _stage_scratch(d),
        compiler_params=_params(),
        name="odd_mixer",
    )(*src_args, w_in, conv_w.T, conv_b.reshape(1, -1), w_out, g.reshape(1, -1), b.reshape(1, -1))


def _kv_kernel(mem_ref, wk_ref, wv_ref, k_ref, v_ref):
    m = mem_ref[...].astype(BF16)
    k_ref[...] = _dot(m, wk_ref[...].astype(BF16)).astype(BF16)
    v_ref[...] = _dot(m, wv_ref[...].astype(BF16)).astype(BF16)


def _memory_kv(mem2d, wk, wv):
    nl, d, _ = wk.shape
    rows = mem2d.shape[0]
    out = jax.ShapeDtypeStruct((nl, rows, d), BF16)
    wspec = pl.BlockSpec((None, d, d), lambda l: (l, 0, 0))
    ospec = pl.BlockSpec((None, rows, d), lambda l: (l, 0, 0))
    return pl.pallas_call(
        _kv_kernel,
        out_shape=(out, out),
        grid=(nl,),
        in_specs=[_const_spec((rows, d)), wspec, wspec],
        out_specs=(ospec, ospec),
        compiler_params=_params(),
        name="memory_kv",
    )(mem2d, wk, wv)


def _attn_kernel(x_ref, k_ref, v_ref, wq_hbm, wo_hbm, g_ref, b_ref, wr_ref, br_ref,
                 o_ref, op_ref, rt_ref, rg_ref, cnt_ref, o_scr, carry_scr, wq_ref, wo_ref, stage, sems,
                 *, layer):
    tm, d = x_ref.shape
    hd_dim = d // N_XHEADS
    sub = tm // SUB_TILES

    @pl.when(pl.program_id(0) == 0)
    def _():
        carry_scr[...] = jnp.zeros_like(carry_scr)
        _load_cast(wq_hbm.at[layer], wq_ref, stage, sems)
        _load_cast(wo_hbm.at[layer], wo_ref, stage, sems)

    for st in range(SUB_TILES):
        rs = slice(st * sub, (st + 1) * sub)
        x = x_ref[rs, :]
        q = _dot(x.astype(BF16), wq_ref[...]) * (1.0 / math.sqrt(hd_dim))
        for hd in range(N_XHEADS):
            hs = slice(hd * hd_dim, (hd + 1) * hd_dim)
            s = lax.dot_general(q[:, hs].astype(BF16), k_ref[:, hs], _NT, preferred_element_type=F32)
            p = jnp.exp(s - jnp.max(s, axis=-1, keepdims=True))
            p = p * (1.0 / jnp.sum(p, axis=-1, keepdims=True))
            o_scr[rs, hs] = _dot(p.astype(BF16), v_ref[:, hs]).astype(BF16)
        xa = _dot(o_scr[rs, :], wo_ref[...])
        out = _layer_norm(ALPHA * x + xa, g_ref[...], b_ref[...])
        o_ref[rs, :] = out
        op_ref[rs, :] = _pack_halves(out)
        table, gates = _route_rows(out, wr_ref, br_ref, carry_scr)
        rt_ref[:, rs] = table
        rg_ref[rs, :] = gates
    cnt_ref[...] = carry_scr[...].astype(I32)


def _cross_attn(x, seq, layer, k, v, wq, wo, g, b, router_w, router_b):
    n, d = x.shape
    m = k.shape[1]
    tiles_per_seq = seq // TM
    kvspec = pl.BlockSpec((None, m, d), lambda i: (i // tiles_per_seq, 0, 0))
    out, packed, table, gates, cnt = pl.pallas_call(
        functools.partial(_attn_kernel, layer=layer),
        out_shape=(jax.ShapeDtypeStruct((n, d), F32), jax.ShapeDtypeStruct((n, d // 2), I32),
                   jax.ShapeDtypeStruct((ROUTE_ROWS, n), F32), jax.ShapeDtypeStruct((n, TOP_K), F32),
                   jax.ShapeDtypeStruct((1, ROUTE_COLS), I32)),
        grid=(n // TM,),
        in_specs=[
            pl.BlockSpec((TM, d), lambda i: (i, 0)),
            kvspec, kvspec,
            _HBM, _HBM,
            _const_spec((1, d)), _const_spec((1, d)),
            _const_spec((d, 2 * ROUTE_COLS)), _const_spec((1, ROUTE_COLS)),
        ],
        out_specs=(pl.BlockSpec((TM, d), lambda i: (i, 0)), pl.BlockSpec((TM, d // 2), lambda i: (i, 0)),
                   pl.BlockSpec((ROUTE_ROWS, TM), lambda i: (0, i)), pl.BlockSpec((TM, TOP_K), lambda i: (i, 0)),
                   _const_spec((1, ROUTE_COLS))),
        scratch_shapes=[pltpu.VMEM((TM, d), BF16), pltpu.VMEM((1, ROUTE_COLS), F32),
                        pltpu.VMEM((d, d), BF16), pltpu.VMEM((d, d), BF16)] + _stage_scratch(d),
        compiler_params=_params(),
        name="cross_attn",
    )(x, k, v, wq, wo, g.reshape(1, -1), b.reshape(1, -1), router_w, router_b)
    return out, packed, table, gates, cnt[0, N_GROUPS:N_GROUPS + N_EXPERTS]


def _route_rows(x, wcat_ref, bias_ref, carry_scr):
    tm = x.shape[0]

    xh = x.astype(BF16)
    xl = (x - xh.astype(F32)).astype(BF16)
    r1 = _dot(xh, wcat_ref[...])
    r2 = _dot(xl, wcat_ref[:, :ROUTE_COLS])
    logits = r1[:, :ROUTE_COLS] + r1[:, ROUTE_COLS:] + r2 + bias_ref[...]

    lane = lax.broadcasted_iota(I32, (tm, ROUTE_COLS), 1).astype(F32)
    neg = -jnp.inf

    def first_argmax(vals):
        mx = jnp.max(vals, axis=-1, keepdims=True)
        idx = jnp.min(jnp.where(vals == mx, lane, float(ROUTE_COLS)), axis=-1, keepdims=True)
        return mx, idx

    gl = jnp.where(lane < N_GROUPS, logits, neg)
    gmax, g_sel = first_argmax(gl)
    gate_g = 1.0 / jnp.sum(jnp.exp(gl - gmax), axis=-1, keepdims=True)

    lo = N_GROUPS + g_sel * EXPERTS_PER_GROUP
    el = jnp.where((lane >= lo) & (lane < lo + EXPERTS_PER_GROUP), logits, neg)
    m1, i1 = first_argmax(el)
    m2, i2 = first_argmax(jnp.where(lane == i1, neg, el))
    e21 = jnp.exp(m2 - m1)
    w1 = 1.0 / (1.0 + e21)
    w2 = e21 / (1.0 + e21)

    oh1 = lane == i1
    oh2 = lane == i2
    oh = (oh1 | oh2).astype(BF16)
    r = lax.broadcasted_iota(I32, (tm, tm), 0)
    c = lax.broadcasted_iota(I32, (tm, tm), 1)
    before = _dot((r > c).astype(BF16), oh) + carry_scr[...]
    rank1 = jnp.sum(jnp.where(oh1, before, 0.0), axis=-1, keepdims=True)
    rank2 = jnp.sum(jnp.where(oh2, before, 0.0), axis=-1, keepdims=True)
    carry_scr[...] += jnp.sum(oh.astype(F32), axis=0, keepdims=True)

    cols = jnp.where(lane == 0.0, i1 - N_GROUPS,
                     jnp.where(lane == 1.0, i2 - N_GROUPS,
                               jnp.where(lane == 2.0, rank1, jnp.where(lane == 3.0, rank2, 0.0))))
    table = jnp.transpose(cols)[:ROUTE_ROWS, :]
    l2 = lax.broadcasted_iota(I32, (tm, TOP_K), 1)
    gates = jnp.where(l2 == 0, gate_g * w1, gate_g * w2)
    return table, gates


def _router_weights(wr_g, br_g, wr_e, br_e):
    d = wr_g.shape[0]
    w = jnp.concatenate([wr_g, jnp.transpose(wr_e, (1, 0, 2)).reshape(d, N_EXPERTS)], axis=1)
    w = jnp.pad(w, ((0, 0), (0, ROUTE_COLS - w.shape[1])))
    w_hi = w.astype(BF16)
    w_lo = (w - w_hi.astype(F32)).astype(BF16)
    bias = jnp.pad(jnp.concatenate([br_g, br_e.reshape(-1)]), (0, ROUTE_COLS - N_GROUPS - N_EXPERTS))
    return jnp.concatenate([w_hi, w_lo], axis=1), bias.reshape(1, -1)


def _sc_gather(table, idx):
    b = idx.shape[0]
    d = table.shape[1]
    per_worker = b // SC_WORKERS
    n_chunks = per_worker // SC_CHUNK
    assert per_worker * SC_WORKERS == b and n_chunks * SC_CHUNK == per_worker and n_chunks % 2 == 0
    mesh = plsc.VectorSubcoreMesh(core_axis_name="c", subcore_axis_name="s")

    @functools.partial(
        pl.kernel, mesh=mesh,
        out_type=jax.ShapeDtypeStruct((b, d), table.dtype),
        scratch_types=[pltpu.VMEM((per_worker,), I32),
                       pltpu.VMEM((2, SC_CHUNK, d), table.dtype),
                       pltpu.SemaphoreType.DMA((2,)),
                       pltpu.SemaphoreType.DMA((2,))],
        name="sc_gather",
    )
    def gather(table_hbm, idx_hbm, out_hbm, idx_v, rows_v, gsem, wsem):
        wid = lax.axis_index("s") * SC_CORES + lax.axis_index("c")
        base = wid * per_worker
        pltpu.sync_copy(idx_hbm.at[pl.ds(base, per_worker)], idx_v)

        def fetch(c, slot):
            off = pl.multiple_of(c * SC_CHUNK, SC_CHUNK)
            return pltpu.make_async_copy(table_hbm.at[idx_v.at[pl.ds(off, SC_CHUNK)]], rows_v.at[slot],
                                         gsem.at[slot])

        def put(c, slot):
            off = pl.multiple_of(c * SC_CHUNK, SC_CHUNK)
            return pltpu.make_async_copy(rows_v.at[slot], out_hbm.at[pl.ds(base + off, SC_CHUNK)],
                                         wsem.at[slot])

        fetch(0, 0).start()

        @pl.loop(0, n_chunks, step=2)
        def _(c0):
            for slot in range(2):
                c = c0 + slot

                @pl.when(c + 1 < n_chunks)
                def _():
                    @pl.when(c >= 1)
                    def _():
                        put(c - 1, 1 - slot).wait()
                    fetch(c + 1, 1 - slot).start()

                fetch(c, slot).wait()
                put(c, slot).start()

        put(n_chunks - 2, 0).wait()
        put(n_chunks - 1, 1).wait()

    return gather(table, idx)


def _sc_row_tokens(dest_flat, rows, n):
    a = dest_flat.shape[0]
    lanes = SC_LANES
    assert a % lanes == 0 and rows % lanes == 0
    mesh = plsc.VectorSubcoreMesh(core_axis_name="c", subcore_axis_name="s")

    @functools.partial(
        pl.kernel, mesh=mesh,
        out_type=jax.ShapeDtypeStruct((rows,), I32),
        scratch_types=[pltpu.VMEM((a,), I32), pltpu.VMEM((rows,), I32)],
        compiler_params=pltpu.CompilerParams(needs_layout_passes=False),
        name="sc_row_tokens",
    )
    def invert(dest_hbm, out_hbm, dest_v, map_v):
        @pl.when((lax.axis_index("s") == 0) & (lax.axis_index("c") == 0))
        def _():
            pltpu.sync_copy(dest_hbm, dest_v)
            lane = lax.iota(I32, lanes)

            @pl.loop(0, rows // lanes)
            def _(i):
                map_v[pl.ds(i * lanes, lanes)] = lax.rem(i * lanes + lane, n)

            @pl.loop(0, a // lanes)
            def _(i):
                plsc.store_scatter(map_v, [dest_v[pl.ds(i * lanes, lanes)]], lax.rem(i * lanes + lane, n))

            pltpu.sync_copy(map_v, out_hbm)

    return invert(dest_flat)


def _expert_kernel(sched_ref, nb_ref, xs_ref, w1_hbm, w3_hbm, w2_hbm, *rest, layer, first_block, aliased):
    y_ref, w1_buf, w3_buf, w2_buf, w1_scr, w3_scr, w2_scr, sems = rest[1:] if aliased else rest
    step = pl.program_id(0)
    b = step + first_block
    used = b < nb_ref[0]
    expert, slot, run_start, next_expert = (sched_ref[r, step] for r in range(4))

    def fetch(e, s):
        return [pltpu.make_async_copy(w_hbm.at[layer, e], buf.at[s], sems.at[s, j])
                for j, (w_hbm, buf) in enumerate(((w1_hbm, w1_buf), (w3_hbm, w3_buf), (w2_hbm, w2_buf)))]

    @pl.when(used & (run_start == 1))
    def _():
        @pl.when(step == 0)
        def _():
            for c in fetch(expert, slot):
                c.start()

        for c in fetch(expert, slot):
            c.wait()

        @pl.when(next_expert >= 0)
        def _():
            for c in fetch(next_expert, 1 - slot):
                c.start()

        w1_scr[...] = w1_buf[slot].astype(BF16)
        w3_scr[...] = w3_buf[slot].astype(BF16)
        w2_scr[...] = w2_buf[slot].astype(BF16)

    @pl.when(used)
    def _():
        x_lo, x_hi = _unpack_halves(xs_ref[...])
        xb = jnp.concatenate([x_lo.astype(BF16), x_hi.astype(BF16)], axis=1)
        h1 = _dot(xb, w1_scr[...])
        h3 = _dot(xb, w3_scr[...])
        hid = h1 * (1.0 / (1.0 + jnp.exp(-h1))) * h3
        y_ref[...] = _pack_halves(_dot(hid.astype(BF16), w2_scr[...]))

    @pl.when(jnp.logical_not(used))
    def _():
        y_ref[...] = jnp.zeros_like(y_ref)


def _expert_mlp(xs, y_prev, schedule, n_used, layer, w1, w3, w2, first_block, n_blocks):
    d, de = w1.shape[2], w1.shape[3]
    part = schedule.shape[1]
    aliased = y_prev is not None

    def row_map(s, sched, nb):
        return (jnp.clip(jnp.minimum(s + first_block, nb[0] - 1) - first_block, 0, part - 1), 0)

    return pl.pallas_call(
        functools.partial(_expert_kernel, layer=layer, first_block=first_block, aliased=aliased),
        out_shape=jax.ShapeDtypeStruct((n_blocks * EXPERT_BLOCK, d // 2), I32),
        grid_spec=pltpu.PrefetchScalarGridSpec(
            num_scalar_prefetch=2,
            grid=(part,),
            in_specs=[pl.BlockSpec((EXPERT_BLOCK, d // 2), row_map), _HBM, _HBM, _HBM] + [_HBM] * aliased,
            out_specs=pl.BlockSpec((EXPERT_BLOCK, d // 2), lambda s, sched, nb: (s + first_block, 0)),
            scratch_shapes=[pltpu.VMEM((2, d, de), F32), pltpu.VMEM((2, d, de), F32), pltpu.VMEM((2, de, d), F32),
                            pltpu.VMEM((d, de), BF16), pltpu.VMEM((d, de), BF16), pltpu.VMEM((de, d), BF16),
                            pltpu.SemaphoreType.DMA((2, 3))],
        ),
        input_output_aliases={6: 0} if aliased else {},
        compiler_params=_params(),
        name="expert_mlp",
    )(schedule, n_used, xs, w1, w3, w2, *([y_prev] if aliased else []))


def _combine_kernel(x_ref, y0_ref, y1_ref, gate_ref, g_ref, b_ref, o_ref):
    o_ref[...] = _moe_output(x_ref, y0_ref, y1_ref, gate_ref, g_ref, b_ref)


def _combine(x, yg, gates, g, b):
    n, d = x.shape
    tiles = n // TM
    return pl.pallas_call(
        _combine_kernel,
        out_shape=jax.ShapeDtypeStruct((n, d), F32),
        grid=(tiles,),
        in_specs=[pl.BlockSpec((TM, d), lambda i: (i, 0)),
                  pl.BlockSpec((TM, d // 2), lambda i: (i, 0)),
                  pl.BlockSpec((TM, d // 2), lambda i: (i + tiles, 0)),
                  pl.BlockSpec((TM, TOP_K), lambda i: (i, 0)),
                  _const_spec((1, d)), _const_spec((1, d))],
        out_specs=pl.BlockSpec((TM, d), lambda i: (i, 0)),
        compiler_params=_params(),
        name="combine",
    )(x, yg, yg, gates, g.reshape(1, -1), b.reshape(1, -1))


def _moe_experts(x_packed, table, counts, layer, w1, w3, w2):
    n = x_packed.shape[0]
    n_blocks = (n * TOP_K + N_EXPERTS * (EXPERT_BLOCK - 1) + EXPERT_BLOCK - 1) // EXPERT_BLOCK
    experts = table[:TOP_K].astype(I32)
    ranks = table[TOP_K:2 * TOP_K].astype(I32)

    blocks_e = (counts + EXPERT_BLOCK - 1) // EXPERT_BLOCK
    blocks_end = jnp.cumsum(blocks_e)
    run_start = (blocks_end - blocks_e) * EXPERT_BLOCK
    n_used = blocks_end[-1:].astype(I32)
    block_ids = jnp.arange(n_blocks, dtype=I32)
    block_expert = jnp.minimum(jnp.sum(blocks_end[None, :] <= block_ids[:, None], axis=1),
                               N_EXPERTS - 1).astype(I32)
    next_block = blocks_end[block_expert]

    def schedule(lo, hi):
        be = block_expert[lo:hi]
        run_start_flag = jnp.concatenate([jnp.ones((1,), I32), (be[1:] != be[:-1]).astype(I32)])
        slot = (jnp.cumsum(run_start_flag) - 1) % 2
        nxt = next_block[lo:hi]
        next_expert = jnp.where(nxt < jnp.minimum(n_used[0], hi), block_expert[jnp.minimum(nxt, hi - 1)], -1)
        return jnp.stack([be, slot, run_start_flag, next_expert]).astype(I32)

    expert_ids = jnp.arange(N_EXPERTS, dtype=I32)
    start_of = jnp.sum(jnp.where(experts[:, :, None] == expert_ids, run_start, 0), axis=-1)
    dest = (start_of + ranks).astype(I32).reshape(-1)
    row_token = _sc_row_tokens(dest, n_blocks * EXPERT_BLOCK, n)

    half = n_blocks // 2
    y = None
    for lo, hi in ((0, half), (half, n_blocks)):
        xs = _sc_gather(x_packed, row_token[lo * EXPERT_BLOCK:hi * EXPERT_BLOCK])
        y = _expert_mlp(xs, y, schedule(lo, hi), n_used, layer, w1, w3, w2, lo, n_blocks)
    return _sc_gather(y, dest)


def kernel(x, mem, w_in_even, w_pool, pool_scale, ln_v_g, ln_v_b, w_spatial, b_spatial, w_out_even,
           w_in_odd, conv_w, conv_b, w_out_odd, wq_x, wk_x, wv_x, wo_x, ln_g, ln_b, wr_group,
           br_group, wr_expert, br_expert, w1, w3, w2):
    bsz, seq, d = x.shape
    assert seq % TM == 0 and d % LANES == 0
    mlen = mem.shape[1]
    k_all, v_all = _memory_kv(mem.reshape(bsz * mlen, d), wk_x, wv_x)
    k_all = k_all.reshape(DEPTH, bsz, mlen, d)
    v_all = v_all.reshape(DEPTH, bsz, mlen, d)
    h = x.reshape(bsz * seq, d)
    pending = None
    for l in range(DEPTH):
        i = l // 2
        if l % 2 == 0:
            h = _even_mixer(h, pending, seq, i, w_in_even, w_pool[i], pool_scale[i], ln_v_g[i], ln_v_b[i],
                            w_spatial[i], b_spatial[i], w_out_even, ln_g[l, 0], ln_b[l, 0])
        else:
            h = _odd_mixer(h, pending, seq, i, w_in_odd, conv_w[i], conv_b[i], w_out_odd,
                           ln_g[l, 0], ln_b[l, 0])
        router_w, router_b = _router_weights(wr_group[l], br_group[l], wr_expert[l], br_expert[l])
        h, hp, table, gates, counts = _cross_attn(h, seq, l, k_all[l], v_all[l], wq_x, wo_x,
                                                  ln_g[l, 1], ln_b[l, 1], router_w, router_b)
        yg = _moe_experts(hp, table, counts, l, w1, w3, w2)
        pending = (yg, gates, ln_g[l, 2], ln_b[l, 2])
    return _combine(h, *pending).reshape(bsz, seq, d)
```

```python
import functools
import math

import jax
import jax.numpy as jnp
from jax import lax
from jax.experimental import pallas as pl
from jax.experimental.pallas import tpu as pltpu
from jax.experimental.pallas import tpu_sc as plsc

F32 = jnp.float32
BF16 = jnp.bfloat16
I32 = jnp.int32

POOL_WINDOWS = (2, 4, 8, 16)
assert all(w & (w - 1) == 0 for w in POOL_WINDOWS)
N_SG_HEADS = 4
CHUNK = 128
CONV_WIDTH = 3
N_XHEADS = 4
N_GROUPS = 4
EXPERTS_PER_GROUP = 8
N_EXPERTS = N_GROUPS * EXPERTS_PER_GROUP
TOP_K = 2
DEPTH = 4
ALPHA = (2.0 * DEPTH) ** 0.25
LN_EPS = 1e-5

LANES = 128
SC_CORES = 2
SC_WORKERS = 32
SC_LANES = 16
SC_CHUNK = 64
TM = 1024
SUB_TILES = 2
ROUTE_ROWS = 8
POOL_HALO = 16
CONV_HALO = 8
EXPERT_BLOCK = 512
ROUTE_COLS = 128
STAGE_COLS = 512
VMEM_LIMIT = 56 * 1024 * 1024

_NT = (((1,), (1,)), ((), ()))


def _dot(a, b):
    return jnp.dot(a, b, preferred_element_type=F32)


def _layer_norm(y, g, b):
    mu = jnp.mean(y, axis=-1, keepdims=True)
    yc = y - mu
    var = jnp.mean(yc * yc, axis=-1, keepdims=True)
    return yc * lax.rsqrt(var + LN_EPS) * g + b


def _gelu_tanh(x):
    c = math.sqrt(2.0 / math.pi)
    return 0.5 * x * (1.0 + jnp.tanh(c * (x + 0.044715 * (x * x * x))))


def _pack_halves(v):
    c = v.shape[1] // 2
    lo = pltpu.bitcast(v[:, :c].astype(BF16).astype(F32), jnp.uint32)
    hi = pltpu.bitcast(v[:, c:].astype(BF16).astype(F32), jnp.uint32)
    return pltpu.bitcast((hi & jnp.uint32(0xFFFF0000)) | (lo >> 16), I32)


def _unpack_halves(w):
    u = pltpu.bitcast(w, jnp.uint32)
    return pltpu.bitcast(u << 16, F32), pltpu.bitcast(u & jnp.uint32(0xFFFF0000), F32)


def _load_cast(w_hbm, w_scr, stage, sems):
    chunks = w_scr.shape[1] // STAGE_COLS

    def chunk_copy(c):
        return pltpu.make_async_copy(w_hbm.at[:, pl.ds(c * STAGE_COLS, STAGE_COLS)], stage.at[c % 2],
                                     sems.at[c % 2])

    chunk_copy(0).start()
    for c in range(chunks):
        if c + 1 < chunks:
            chunk_copy(c + 1).start()
        chunk_copy(c).wait()
        w_scr[:, c * STAGE_COLS:(c + 1) * STAGE_COLS] = stage[c % 2].astype(BF16)


def _stage_scratch(rows):
    return [pltpu.VMEM((2, rows, STAGE_COLS), F32), pltpu.SemaphoreType.DMA((2,))]


_HBM = pl.BlockSpec(memory_space=pl.ANY)


def _const_spec(shape):
    nd = len(shape)
    return pl.BlockSpec(shape, lambda i: (0,) * nd)


def _params():
    return pltpu.CompilerParams(dimension_semantics=("arbitrary",), vmem_limit_bytes=VMEM_LIMIT)


def _moe_output(x_ref, y0_ref, y1_ref, gate_ref, g_ref, b_ref):
    gates = gate_ref[...]
    g0, g1 = gates[:, 0:1], gates[:, 1:2]
    y0_lo, y0_hi = _unpack_halves(y0_ref[...])
    y1_lo, y1_hi = _unpack_halves(y1_ref[...])
    ff = jnp.concatenate([g0 * y0_lo + g1 * y1_lo, g0 * y0_hi + g1 * y1_hi], axis=1)
    return _layer_norm(ALPHA * x_ref[...] + ff, g_ref[...], b_ref[...])


def _mixer_input(src, pending):
    if not pending:
        x_ref, xh_ref = src
        return x_ref[...], xh_ref[...]
    x_ref, xh_ref, y0_ref, y0h_ref, y1_ref, y1h_ref, gate_ref, gateh_ref, g_ref, b_ref = src
    return (_moe_output(x_ref, y0_ref, y1_ref, gate_ref, g_ref, b_ref),
            _moe_output(xh_ref, y0h_ref, y1h_ref, gateh_ref, g_ref, b_ref))


def _mixer_sources(x, pending, halo):
    n, d = x.shape
    tiles = n // TM
    halo_blocks = TM // halo

    def halo_index(i):
        return jnp.maximum(i * halo_blocks - 1, 0)

    specs = [pl.BlockSpec((TM, d), lambda i: (i, 0)), pl.BlockSpec((halo, d), lambda i: (halo_index(i), 0))]
    args = [x, x]
    if pending is not None:
        yg, gates, g, b = pending
        specs += [pl.BlockSpec((TM, d // 2), lambda i: (i, 0)),
                  pl.BlockSpec((halo, d // 2), lambda i: (halo_index(i), 0)),
                  pl.BlockSpec((TM, d // 2), lambda i: (i + tiles, 0)),
                  pl.BlockSpec((halo, d // 2), lambda i: (halo_index(i) + tiles * halo_blocks, 0)),
                  pl.BlockSpec((TM, TOP_K), lambda i: (i, 0)),
                  pl.BlockSpec((halo, TOP_K), lambda i: (halo_index(i), 0)),
                  _const_spec((1, d)), _const_spec((1, d))]
        args += [yg, yg, yg, yg, gates, gates, g.reshape(1, -1), b.reshape(1, -1)]
    return specs, args


def _even_kernel(*refs, tiles_per_seq, layer, pending):
    n_src = 10 if pending else 2
    (win_hbm, wpool_ref, pscale_ref, lvg_ref, lvb_ref, ws_ref, bst_ref, wout_hbm, g_ref, b_ref, o_ref,
     a_scr, cat_scr, win_ref, wout_ref, stage, sems) = refs[n_src:]
    tm = o_ref.shape[0]
    d_pool = a_scr.shape[1]
    d_sg = lvg_ref.shape[1]
    pgd = d_pool // len(POOL_WINDOWS)
    hd_dim = d_sg // N_SG_HEADS
    seq_tile = pl.program_id(0) % tiles_per_seq

    @pl.when(pl.program_id(0) == 0)
    def _():
        _load_cast(win_hbm.at[layer], win_ref, stage, sems)
        _load_cast(wout_hbm.at[layer], wout_ref, stage, sems)

    x, xh = _mixer_input(refs[:n_src], pending)
    h = _dot(x.astype(BF16), win_ref[...])

    ah = _dot(xh.astype(BF16), win_ref[:, :d_pool])
    a_scr[0:POOL_HALO, :] = jnp.where(seq_tile == 0, 0.0, ah)
    a_scr[POOL_HALO:POOL_HALO + tm, :] = h[:, :d_pool]
    pos = seq_tile * tm + lax.broadcasted_iota(I32, (tm, 1), 0)
    for g, w in enumerate(POOL_WINDOWS):
        cs = slice(g * pgd, (g + 1) * pgd)
        tok = a_scr[POOL_HALO:POOL_HALO + tm, cs]
        acc = a_scr[:, cs]
        span = 1
        while span < w:
            acc = acc[span:, :] + acc[:-span, :]
            span *= 2
        acc = acc[acc.shape[0] - tm:, :]
        cnt = jnp.minimum(pos + 1, w).astype(F32)
        d = acc * (1.0 / cnt) - tok
        yg = _dot(d.astype(BF16), wpool_ref[g])
        cat_scr[:, cs] = (yg * pscale_ref[:, cs]).astype(BF16)

    z = _gelu_tanh(h[:, d_pool:])
    u = z[:, :d_sg]
    v = _layer_norm(z[:, d_sg:], lvg_ref[...], lvb_ref[...]).astype(BF16)
    row = lax.broadcasted_iota(I32, (CHUNK, CHUNK), 0)
    col = lax.broadcasted_iota(I32, (CHUNK, CHUNK), 1)
    for hd in range(N_SG_HEADS):
        hs = slice(hd * hd_dim, (hd + 1) * hd_dim)
        wsm = jnp.where(row >= col, ws_ref[hd], 0.0).astype(BF16)
        bcol = bst_ref[:, hd:hd + 1]
        for ck in range(tm // CHUNK):
            rs = slice(ck * CHUNK, (ck + 1) * CHUNK)
            sv = _dot(wsm, v[rs, hs]) + bcol
            cat_scr[rs, d_pool + hd * hd_dim:d_pool + (hd + 1) * hd_dim] = (u[rs, hs] * sv).astype(BF16)

    mix = _dot(cat_scr[...], wout_ref[...])
    o_ref[...] = _layer_norm(ALPHA * x + mix, g_ref[...], b_ref[...])


def _even_mixer(x, pending, seq, layer, w_in, w_pool, pool_scale, ln_v_g, ln_v_b, w_spatial, b_spatial,
                w_out, g, b):
    n, d = x.shape
    d_in = w_in.shape[2]
    d_pool = pool_scale.shape[0]
    d_sg = ln_v_g.shape[0]
    kern = functools.partial(_even_kernel, tiles_per_seq=seq // TM, layer=layer, pending=pending is not None)
    src_specs, src_args = _mixer_sources(x, pending, POOL_HALO)
    return pl.pallas_call(
        kern,
        out_shape=jax.ShapeDtypeStruct((n, d), F32),
        grid=(n // TM,),
        in_specs=src_specs + [
            _HBM,
            _const_spec(w_pool.shape),
            _const_spec((1, d_pool)),
            _const_spec((1, d_sg)),
            _const_spec((1, d_sg)),
            _const_spec(w_spatial.shape),
            _const_spec((CHUNK, N_SG_HEADS)),
            _HBM,
            _const_spec((1, d)),
            _const_spec((1, d)),
        ],
        out_specs=pl.BlockSpec((TM, d), lambda i: (i, 0)),
        scratch_shapes=[pltpu.VMEM((POOL_HALO + TM, d_pool), F32), pltpu.VMEM((TM, d_pool + d_sg), BF16),
                        pltpu.VMEM((d, d_in), BF16), pltpu.VMEM((d_pool + d_sg, d), BF16)] + _stage_scratch(d),
        compiler_params=_params(),
        name="even_mixer",
    )(*src_args, w_in, w_pool.astype(BF16), pool_scale.reshape(1, -1), ln_v_g.reshape(1, -1),
      ln_v_b.reshape(1, -1), w_spatial, b_spatial.T, w_out, g.reshape(1, -1), b.reshape(1, -1))


def _odd_kernel(*refs, tiles_per_seq, layer, pending):
    n_src = 10 if pending else 2
    (win_hbm, cwt_ref, cb_ref, wout_hbm, g_ref, b_ref, o_ref, zc_scr,
     win_ref, wout_ref, stage, sems) = refs[n_src:]
    tm, d = o_ref.shape
    seq_tile = pl.program_id(0) % tiles_per_seq

    @pl.when(pl.program_id(0) == 0)
    def _():
        _load_cast(win_hbm.at[layer], win_ref, stage, sems)
        _load_cast(wout_hbm.at[layer], wout_ref, stage, sems)

    x, xh = _mixer_input(refs[:n_src], pending)
    xb = x.astype(BF16)
    hc = _dot(xb, win_ref[:, d:2 * d])
    hz = _dot(xb, win_ref[:, 2 * d:])
    hh = _dot(xh.astype(BF16), win_ref[:, d:])
    zc_scr[0:CONV_HALO, :] = jnp.where(seq_tile == 0, 0.0, hh[:, :d] * hh[:, d:])
    zc_scr[CONV_HALO:CONV_HALO + tm, :] = hc * hz
    gate = _dot(xb, win_ref[:, :d])
    conv = cb_ref[...]
    for j in range(CONV_WIDTH):
        off = CONV_HALO - (CONV_WIDTH - 1) + j
        conv = conv + zc_scr[off:off + tm, :] * cwt_ref[j:j + 1, :]
    y = _dot((gate * conv).astype(BF16), wout_ref[...])
    o_ref[...] = _layer_norm(ALPHA * x + y, g_ref[...], b_ref[...])


def _odd_mixer(x, pending, seq, layer, w_in, conv_w, conv_b, w_out, g, b):
    n, d = x.shape
    kern = functools.partial(_odd_kernel, tiles_per_seq=seq // TM, layer=layer, pending=pending is not None)
    src_specs, src_args = _mixer_sources(x, pending, CONV_HALO)
    return pl.pallas_call(
        kern,
        out_shape=jax.ShapeDtypeStruct((n, d), F32),
        grid=(n // TM,),
        in_specs=src_specs + [
            _HBM,
            _const_spec((CONV_WIDTH, d)),
            _const_spec((1, d)),
            _HBM,
            _const_spec((1, d)),
            _const_spec((1, d)),
        ],
        out_specs=pl.BlockSpec((TM, d), lambda i: (i, 0)),
        scratch_shapes=[pltpu.VMEM((CONV_HALO + TM, d), F32),
                        pltpu.VMEM(w_in.shape[1:], BF16), pltpu.VMEM(w_out.shape[1:], BF16)] + _stage_scratch(d),
        compiler_params=_params(),
        name="odd_mixer",
    )(*src_args, w_in, conv_w.T, conv_b.reshape(1, -1), w_out, g.reshape(1, -1), b.reshape(1, -1))


def _kv_kernel(mem_ref, wk_ref, wv_ref, k_ref, v_ref):
    m = mem_ref[...].astype(BF16)
    k_ref[...] = _dot(m, wk_ref[...].astype(BF16)).astype(BF16)
    v_ref[...] = _dot(m, wv_ref[...].astype(BF16)).astype(BF16)


def _memory_kv(mem2d, wk, wv):
    nl, d, _ = wk.shape
    rows = mem2d.shape[0]
    out = jax.ShapeDtypeStruct((nl, rows, d), BF16)
    wspec = pl.BlockSpec((None, d, d), lambda l: (l, 0, 0))
    ospec = pl.BlockSpec((None, rows, d), lambda l: (l, 0, 0))
    return pl.pallas_call(
        _kv_kernel,
        out_shape=(out, out),
        grid=(nl,),
        in_specs=[_const_spec((rows, d)), wspec, wspec],
        out_specs=(ospec, ospec),
        compiler_params=_params(),
        name="memory_kv",
    )(mem2d, wk, wv)


def _attn_kernel(x_ref, k_ref, v_ref, wq_hbm, wo_hbm, g_ref, b_ref, wr_ref, br_ref,
                 o_ref, op_ref, rt_ref, rg_ref, cnt_ref, o_scr, carry_scr, wq_ref, wo_ref, stage, sems,
                 *, layer):
    tm, d = x_ref.shape
    hd_dim = d // N_XHEADS
    sub = tm // SUB_TILES

    @pl.when(pl.program_id(0) == 0)
    def _():
        carry_scr[...] = jnp.zeros_like(carry_scr)
        _load_cast(wq_hbm.at[layer], wq_ref, stage, sems)
        _load_cast(wo_hbm.at[layer], wo_ref, stage, sems)

    for st in range(SUB_TILES):
        rs = slice(st * sub, (st + 1) * sub)
        x = x_ref[rs, :]
        q = _dot(x.astype(BF16), wq_ref[...]) * (1.0 / math.sqrt(hd_dim))
        for hd in range(N_XHEADS):
            hs = slice(hd * hd_dim, (hd + 1) * hd_dim)
            s = lax.dot_general(q[:, hs].astype(BF16), k_ref[:, hs], _NT, preferred_element_type=F32)
            p = jnp.exp(s - jnp.max(s, axis=-1, keepdims=True))
            p = p * (1.0 / jnp.sum(p, axis=-1, keepdims=True))
            o_scr[rs, hs] = _dot(p.astype(BF16), v_ref[:, hs]).astype(BF16)
        xa = _dot(o_scr[rs, :], wo_ref[...])
        out = _layer_norm(ALPHA * x + xa, g_ref[...], b_ref[...])
        o_ref[rs, :] = out
        op_ref[rs, :] = _pack_halves(out)
        table, gates = _route_rows(out, wr_ref, br_ref, carry_scr)
        rt_ref[:, rs] = table
        rg_ref[rs, :] = gates
    cnt_ref[...] = carry_scr[...].astype(I32)


def _cross_attn(x, seq, layer, k, v, wq, wo, g, b, router_w, router_b):
    n, d = x.shape
    m = k.shape[2]
    tiles_per_seq = seq // TM
    kvspec = pl.BlockSpec((None, None, m, d), lambda i: (layer, i // tiles_per_seq, 0, 0))
    out, packed, table, gates, cnt = pl.pallas_call(
        functools.partial(_attn_kernel, layer=layer),
        out_shape=(jax.ShapeDtypeStruct((n, d), F32), jax.ShapeDtypeStruct((n, d // 2), I32),
                   jax.ShapeDtypeStruct((ROUTE_ROWS, n), F32), jax.ShapeDtypeStruct((n, TOP_K), F32),
                   jax.ShapeDtypeStruct((1, ROUTE_COLS), I32)),
        grid=(n // TM,),
        in_specs=[
            pl.BlockSpec((TM, d), lambda i: (i, 0)),
            kvspec, kvspec,
            _HBM, _HBM,
            _const_spec((1, d)), _const_spec((1, d)),
            _const_spec((d, 2 * ROUTE_COLS)), _const_spec((1, ROUTE_COLS)),
        ],
        out_specs=(pl.BlockSpec((TM, d), lambda i: (i, 0)), pl.BlockSpec((TM, d // 2), lambda i: (i, 0)),
                   pl.BlockSpec((ROUTE_ROWS, TM), lambda i: (0, i)), pl.BlockSpec((TM, TOP_K), lambda i: (i, 0)),
                   _const_spec((1, ROUTE_COLS))),
        scratch_shapes=[pltpu.VMEM((TM, d), BF16), pltpu.VMEM((1, ROUTE_COLS), F32),
                        pltpu.VMEM((d, d), BF16), pltpu.VMEM((d, d), BF16)] + _stage_scratch(d),
        compiler_params=_params(),
        name="cross_attn",
    )(x, k, v, wq, wo, g.reshape(1, -1), b.reshape(1, -1), router_w, router_b)
    return out, packed, table, gates, cnt[0, N_GROUPS:N_GROUPS + N_EXPERTS]


def _route_rows(x, wcat_ref, bias_ref, carry_scr):
    tm = x.shape[0]

    xh = x.astype(BF16)
    xl = (x - xh.astype(F32)).astype(BF16)
    r1 = _dot(xh, wcat_ref[...])
    r2 = _dot(xl, wcat_ref[:, :ROUTE_COLS])
    logits = r1[:, :ROUTE_COLS] + r1[:, ROUTE_COLS:] + r2 + bias_ref[...]

    lane = lax.broadcasted_iota(I32, (tm, ROUTE_COLS), 1).astype(F32)
    neg = -jnp.inf

    def first_argmax(vals):
        mx = jnp.max(vals, axis=-1, keepdims=True)
        idx = jnp.min(jnp.where(vals == mx, lane, float(ROUTE_COLS)), axis=-1, keepdims=True)
        return mx, idx

    gl = jnp.where(lane < N_GROUPS, logits, neg)
    gmax, g_sel = first_argmax(gl)
    gate_g = 1.0 / jnp.sum(jnp.exp(gl - gmax), axis=-1, keepdims=True)

    lo = N_GROUPS + g_sel * EXPERTS_PER_GROUP
    el = jnp.where((lane >= lo) & (lane < lo + EXPERTS_PER_GROUP), logits, neg)
    m1, i1 = first_argmax(el)
    m2, i2 = first_argmax(jnp.where(lane == i1, neg, el))
    e21 = jnp.exp(m2 - m1)
    w1 = 1.0 / (1.0 + e21)
    w2 = e21 / (1.0 + e21)

    oh1 = lane == i1
    oh2 = lane == i2
    oh = (oh1 | oh2).astype(BF16)
    r = lax.broadcasted_iota(I32, (tm, tm), 0)
    c = lax.broadcasted_iota(I32, (tm, tm), 1)
    before = _dot((r > c).astype(BF16), oh) + carry_scr[...]
    rank1 = jnp.sum(jnp.where(oh1, before, 0.0), axis=-1, keepdims=True)
    rank2 = jnp.sum(jnp.where(oh2, before, 0.0), axis=-1, keepdims=True)
    carry_scr[...] += jnp.sum(oh.astype(F32), axis=0, keepdims=True)

    cols = jnp.where(lane == 0.0, i1 - N_GROUPS,
                     jnp.where(lane == 1.0, i2 - N_GROUPS,
                               jnp.where(lane == 2.0, rank1, jnp.where(lane == 3.0, rank2, 0.0))))
    table = jnp.transpose(cols)[:ROUTE_ROWS, :]
    l2 = lax.broadcasted_iota(I32, (tm, TOP_K), 1)
    gates = jnp.where(l2 == 0, gate_g * w1, gate_g * w2)
    return table, gates


def _router_weights(wr_g, br_g, wr_e, br_e):
    d = wr_g.shape[0]
    w = jnp.concatenate([wr_g, jnp.transpose(wr_e, (1, 0, 2)).reshape(d, N_EXPERTS)], axis=1)
    w = jnp.pad(w, ((0, 0), (0, ROUTE_COLS - w.shape[1])))
    w_hi = w.astype(BF16)
    w_lo = (w - w_hi.astype(F32)).astype(BF16)
    bias = jnp.pad(jnp.concatenate([br_g, br_e.reshape(-1)]), (0, ROUTE_COLS - N_GROUPS - N_EXPERTS))
    return jnp.concatenate([w_hi, w_lo], axis=1), bias.reshape(1, -1)


def _sc_gather(table, idx):
    b = idx.shape[0]
    d = table.shape[1]
    per_worker = b // SC_WORKERS
    n_chunks = per_worker // SC_CHUNK
    assert per_worker * SC_WORKERS == b and n_chunks * SC_CHUNK == per_worker and n_chunks % 2 == 0
    mesh = plsc.VectorSubcoreMesh(core_axis_name="c", subcore_axis_name="s")

    @functools.partial(
        pl.kernel, mesh=mesh,
        out_type=jax.ShapeDtypeStruct((b, d), table.dtype),
        scratch_types=[pltpu.VMEM((per_worker,), I32),
                       pltpu.VMEM((2, SC_CHUNK, d), table.dtype),
                       pltpu.SemaphoreType.DMA((2,)),
                       pltpu.SemaphoreType.DMA((2,))],
        name="sc_gather",
    )
    def gather(table_hbm, idx_hbm, out_hbm, idx_v, rows_v, gsem, wsem):
        wid = lax.axis_index("s") * SC_CORES + lax.axis_index("c")
        base = wid * per_worker
        pltpu.sync_copy(idx_hbm.at[pl.ds(base, per_worker)], idx_v)

        def fetch(c, slot):
            off = pl.multiple_of(c * SC_CHUNK, SC_CHUNK)
            return pltpu.make_async_copy(table_hbm.at[idx_v.at[pl.ds(off, SC_CHUNK)]], rows_v.at[slot],
                                         gsem.at[slot])

        def put(c, slot):
            off = pl.multiple_of(c * SC_CHUNK, SC_CHUNK)
            return pltpu.make_async_copy(rows_v.at[slot], out_hbm.at[pl.ds(base + off, SC_CHUNK)],
                                         wsem.at[slot])

        fetch(0, 0).start()

        @pl.loop(0, n_chunks, step=2)
        def _(c0):
            for slot in range(2):
                c = c0 + slot

                @pl.when(c + 1 < n_chunks)
                def _():
                    @pl.when(c >= 1)
                    def _():
                        put(c - 1, 1 - slot).wait()
                    fetch(c + 1, 1 - slot).start()

                fetch(c, slot).wait()
                put(c, slot).start()

        put(n_chunks - 2, 0).wait()
        put(n_chunks - 1, 1).wait()

    return gather(table, idx)


def _sc_row_tokens(dest_flat, rows, n):
    a = dest_flat.shape[0]
    lanes = SC_LANES
    assert a % lanes == 0 and rows % lanes == 0
    mesh = plsc.VectorSubcoreMesh(core_axis_name="c", subcore_axis_name="s")

    @functools.partial(
        pl.kernel, mesh=mesh,
        out_type=jax.ShapeDtypeStruct((rows,), I32),
        scratch_types=[pltpu.VMEM((a,), I32), pltpu.VMEM((rows,), I32)],
        compiler_params=pltpu.CompilerParams(needs_layout_passes=False),
        name="sc_row_tokens",
    )
    def invert(dest_hbm, out_hbm, dest_v, map_v):
        @pl.when((lax.axis_index("s") == 0) & (lax.axis_index("c") == 0))
        def _():
            pltpu.sync_copy(dest_hbm, dest_v)
            lane = lax.iota(I32, lanes)

            @pl.loop(0, rows // lanes)
            def _(i):
                map_v[pl.ds(i * lanes, lanes)] = lax.rem(i * lanes + lane, n)

            @pl.loop(0, a // lanes)
            def _(i):
                plsc.store_scatter(map_v, [dest_v[pl.ds(i * lanes, lanes)]], lax.rem(i * lanes + lane, n))

            pltpu.sync_copy(map_v, out_hbm)

    return invert(dest_flat)


def _expert_kernel(sched_ref, nb_ref, xs_ref, w1_hbm, w3_hbm, w2_hbm, y_ref,
                   w1_buf, w3_buf, w2_buf, w1_scr, w3_scr, w2_scr, sems, *, layer):
    b = pl.program_id(0)
    used = b < nb_ref[0]
    expert, slot, run_start, next_expert = (sched_ref[r, b] for r in range(4))

    def fetch(e, s):
        return [pltpu.make_async_copy(w_hbm.at[layer, e], buf.at[s], sems.at[s, j])
                for j, (w_hbm, buf) in enumerate(((w1_hbm, w1_buf), (w3_hbm, w3_buf), (w2_hbm, w2_buf)))]

    @pl.when(used & (run_start == 1))
    def _():
        @pl.when(b == 0)
        def _():
            for c in fetch(expert, slot):
                c.start()

        for c in fetch(expert, slot):
            c.wait()

        @pl.when(next_expert >= 0)
        def _():
            for c in fetch(next_expert, 1 - slot):
                c.start()

        w1_scr[...] = w1_buf[slot].astype(BF16)
        w3_scr[...] = w3_buf[slot].astype(BF16)
        w2_scr[...] = w2_buf[slot].astype(BF16)

    @pl.when(used)
    def _():
        x_lo, x_hi = _unpack_halves(xs_ref[...])
        xb = jnp.concatenate([x_lo.astype(BF16), x_hi.astype(BF16)], axis=1)
        h1 = _dot(xb, w1_scr[...])
        h3 = _dot(xb, w3_scr[...])
        hid = h1 * (1.0 / (1.0 + jnp.exp(-h1))) * h3
        y_ref[...] = _pack_halves(_dot(hid.astype(BF16), w2_scr[...]))

    @pl.when(jnp.logical_not(used))
    def _():
        y_ref[...] = jnp.zeros_like(y_ref)


def _expert_mlp(xs, schedule, n_used, layer, w1, w3, w2, n_blocks):
    d, de = w1.shape[2], w1.shape[3]

    def row_map(b, sched, nb):
        return (jnp.minimum(b, nb[0] - 1), 0)

    return pl.pallas_call(
        functools.partial(_expert_kernel, layer=layer),
        out_shape=jax.ShapeDtypeStruct((n_blocks * EXPERT_BLOCK, d // 2), I32),
        grid_spec=pltpu.PrefetchScalarGridSpec(
            num_scalar_prefetch=2,
            grid=(n_blocks,),
            in_specs=[pl.BlockSpec((EXPERT_BLOCK, d // 2), row_map), _HBM, _HBM, _HBM],
            out_specs=pl.BlockSpec((EXPERT_BLOCK, d // 2), lambda b, sched, nb: (b, 0)),
            scratch_shapes=[pltpu.VMEM((2, d, de), F32), pltpu.VMEM((2, d, de), F32), pltpu.VMEM((2, de, d), F32),
                            pltpu.VMEM((d, de), BF16), pltpu.VMEM((d, de), BF16), pltpu.VMEM((de, d), BF16),
                            pltpu.SemaphoreType.DMA((2, 3))],
        ),
        compiler_params=_params(),
        name="expert_mlp",
    )(schedule, n_used, xs, w1, w3, w2)


def _combine_kernel(x_ref, y0_ref, y1_ref, gate_ref, g_ref, b_ref, o_ref):
    o_ref[...] = _moe_output(x_ref, y0_ref, y1_ref, gate_ref, g_ref, b_ref)


def _combine(x, yg, gates, g, b):
    n, d = x.shape
    tiles = n // TM
    return pl.pallas_call(
        _combine_kernel,
        out_shape=jax.ShapeDtypeStruct((n, d), F32),
        grid=(tiles,),
        in_specs=[pl.BlockSpec((TM, d), lambda i: (i, 0)),
                  pl.BlockSpec((TM, d // 2), lambda i: (i, 0)),
                  pl.BlockSpec((TM, d // 2), lambda i: (i + tiles, 0)),
                  pl.BlockSpec((TM, TOP_K), lambda i: (i, 0)),
                  _const_spec((1, d)), _const_spec((1, d))],
        out_specs=pl.BlockSpec((TM, d), lambda i: (i, 0)),
        compiler_params=_params(),
        name="combine",
    )(x, yg, yg, gates, g.reshape(1, -1), b.reshape(1, -1))


def _moe_experts(x_packed, table, counts, layer, w1, w3, w2):
    n = x_packed.shape[0]
    n_blocks = (n * TOP_K + N_EXPERTS * (EXPERT_BLOCK - 1) + EXPERT_BLOCK - 1) // EXPERT_BLOCK
    experts = table[:TOP_K].astype(I32)
    ranks = table[TOP_K:2 * TOP_K].astype(I32)

    blocks_e = (counts + EXPERT_BLOCK - 1) // EXPERT_BLOCK
    blocks_end = jnp.cumsum(blocks_e)
    run_start = (blocks_end - blocks_e) * EXPERT_BLOCK
    n_used = blocks_end[-1:].astype(I32)
    block_ids = jnp.arange(n_blocks, dtype=I32)
    block_expert = jnp.minimum(jnp.sum(blocks_end[None, :] <= block_ids[:, None], axis=1),
                               N_EXPERTS - 1).astype(I32)
    run_start_flag = jnp.concatenate([jnp.ones((1,), I32),
                                      (block_expert[1:] != block_expert[:-1]).astype(I32)])
    slot = (jnp.cumsum(run_start_flag) - 1) % 2
    next_block = blocks_end[block_expert]
    next_expert = jnp.where(next_block < n_used[0],
                            block_expert[jnp.minimum(next_block, n_blocks - 1)], -1)
    schedule = jnp.stack([block_expert, slot, run_start_flag, next_expert]).astype(I32)
    expert_ids = jnp.arange(N_EXPERTS, dtype=I32)
    start_of = jnp.sum(jnp.where(experts[:, :, None] == expert_ids, run_start, 0), axis=-1)
    dest = (start_of + ranks).astype(I32).reshape(-1)
    row_token = _sc_row_tokens(dest, n_blocks * EXPERT_BLOCK, n)

    xs = _sc_gather(x_packed, row_token)
    y = _expert_mlp(xs, schedule, n_used, layer, w1, w3, w2, n_blocks)
    return _sc_gather(y, dest)


def kernel(x, mem, w_in_even, w_pool, pool_scale, ln_v_g, ln_v_b, w_spatial, b_spatial, w_out_even,
           w_in_odd, conv_w, conv_b, w_out_odd, wq_x, wk_x, wv_x, wo_x, ln_g, ln_b, wr_group,
           br_group, wr_expert, br_expert, w1, w3, w2):
    bsz, seq, d = x.shape
    assert seq % TM == 0 and d % LANES == 0
    mlen = mem.shape[1]
    k_all, v_all = _memory_kv(mem.reshape(bsz * mlen, d), wk_x, wv_x)
    k_all = k_all.reshape(DEPTH, bsz, mlen, d)
    v_all = v_all.reshape(DEPTH, bsz, mlen, d)
    h = x.reshape(bsz * seq, d)
    pending = None
    for l in range(DEPTH):
        i = l // 2
        if l % 2 == 0:
            h = _even_mixer(h, pending, seq, i, w_in_even, w_pool[i], pool_scale[i], ln_v_g[i], ln_v_b[i],
                            w_spatial[i], b_spatial[i], w_out_even, ln_g[l, 0], ln_b[l, 0])
        else:
            h = _odd_mixer(h, pending, seq, i, w_in_odd, conv_w[i], conv_b[i], w_out_odd,
                           ln_g[l, 0], ln_b[l, 0])
        router_w, router_b = _router_weights(wr_group[l], br_group[l], wr_expert[l], br_expert[l])
        h, hp, table, gates, counts = _cross_attn(h, seq, l, k_all, v_all, wq_x, wo_x,
                                                  ln_g[l, 1], ln_b[l, 1], router_w, router_b)
        yg = _moe_experts(hp, table, counts, l, w1, w3, w2)
        pending = (yg, gates, ln_g[l, 2], ln_b[l, 2])
    return _combine(h, *pending).reshape(bsz, seq, d)
```

```python
import functools
import math

import jax
import jax.numpy as jnp
from jax import lax
from jax.experimental import pallas as pl
from jax.experimental.pallas import tpu as pltpu
from jax.experimental.pallas import tpu_sc as plsc

F32 = jnp.float32
BF16 = jnp.bfloat16
I32 = jnp.int32

POOL_WINDOWS = (2, 4, 8, 16)
assert all(w & (w - 1) == 0 for w in POOL_WINDOWS)
N_SG_HEADS = 4
CHUNK = 128
CONV_WIDTH = 3
N_XHEADS = 4
N_GROUPS = 4
EXPERTS_PER_GROUP = 8
N_EXPERTS = N_GROUPS * EXPERTS_PER_GROUP
TOP_K = 2
DEPTH = 4
ALPHA = (2.0 * DEPTH) ** 0.25
LN_EPS = 1e-5

LANES = 128
SC_CORES = 2
SC_WORKERS = 32
SC_LANES = 16
SC_CHUNK = 64
TM = 1024
SUB_TILES = 2
ROUTE_ROWS = 8
POOL_HALO = 16
CONV_HALO = 8
EXPERT_BLOCK = 512
ROUTE_COLS = 128
STAGE_COLS = 512
VMEM_LIMIT = 56 * 1024 * 1024

_NT = (((1,), (1,)), ((), ()))


def _dot(a, b):
    return jnp.dot(a, b, preferred_element_type=F32)


def _layer_norm(y, g, b):
    mu = jnp.mean(y, axis=-1, keepdims=True)
    yc = y - mu
    var = jnp.mean(yc * yc, axis=-1, keepdims=True)
    return yc * lax.rsqrt(var + LN_EPS) * g + b


def _gelu_tanh(x):
    c = math.sqrt(2.0 / math.pi)
    return 0.5 * x * (1.0 + jnp.tanh(c * (x + 0.044715 * (x * x * x))))


def _pack_halves(v):
    c = v.shape[1] // 2
    lo = pltpu.bitcast(v[:, :c].astype(BF16).astype(F32), jnp.uint32)
    hi = pltpu.bitcast(v[:, c:].astype(BF16).astype(F32), jnp.uint32)
    return pltpu.bitcast((hi & jnp.uint32(0xFFFF0000)) | (lo >> 16), I32)


def _unpack_halves(w):
    u = pltpu.bitcast(w, jnp.uint32)
    return pltpu.bitcast(u << 16, F32), pltpu.bitcast(u & jnp.uint32(0xFFFF0000), F32)


def _load_cast(w_hbm, w_scr, stage, sems):
    chunks = w_scr.shape[1] // STAGE_COLS

    def chunk_copy(c):
        return pltpu.make_async_copy(w_hbm.at[:, pl.ds(c * STAGE_COLS, STAGE_COLS)], stage.at[c % 2],
                                     sems.at[c % 2])

    chunk_copy(0).start()
    for c in range(chunks):
        if c + 1 < chunks:
            chunk_copy(c + 1).start()
        chunk_copy(c).wait()
        w_scr[:, c * STAGE_COLS:(c + 1) * STAGE_COLS] = stage[c % 2].astype(BF16)


def _stage_scratch(rows):
    return [pltpu.VMEM((2, rows, STAGE_COLS), F32), pltpu.SemaphoreType.DMA((2,))]


_HBM = pl.BlockSpec(memory_space=pl.ANY)


def _const_spec(shape):
    nd = len(shape)
    return pl.BlockSpec(shape, lambda i: (0,) * nd)


def _params():
    return pltpu.CompilerParams(dimension_semantics=("arbitrary",), vmem_limit_bytes=VMEM_LIMIT)


def _moe_output(x_ref, y0_ref, y1_ref, gate_ref, g_ref, b_ref):
    gates = gate_ref[...]
    g0, g1 = gates[:, 0:1], gates[:, 1:2]
    y0_lo, y0_hi = _unpack_halves(y0_ref[...])
    y1_lo, y1_hi = _unpack_halves(y1_ref[...])
    ff = jnp.concatenate([g0 * y0_lo + g1 * y1_lo, g0 * y0_hi + g1 * y1_hi], axis=1)
    return _layer_norm(ALPHA * x_ref[...] + ff, g_ref[...], b_ref[...])


def _mixer_input(src, pending):
    if not pending:
        x_ref, xh_ref = src
        return x_ref[...], xh_ref[...]
    x_ref, xh_ref, y0_ref, y0h_ref, y1_ref, y1h_ref, gate_ref, gateh_ref, g_ref, b_ref = src
    return (_moe_output(x_ref, y0_ref, y1_ref, gate_ref, g_ref, b_ref),
            _moe_output(xh_ref, y0h_ref, y1h_ref, gateh_ref, g_ref, b_ref))


def _mixer_sources(x, pending, halo):
    n, d = x.shape
    tiles = n // TM
    halo_blocks = TM // halo

    def halo_index(i):
        return jnp.maximum(i * halo_blocks - 1, 0)

    specs = [pl.BlockSpec((TM, d), lambda i: (i, 0)), pl.BlockSpec((halo, d), lambda i: (halo_index(i), 0))]
    args = [x, x]
    if pending is not None:
        yg, gates, g, b = pending
        specs += [pl.BlockSpec((TM, d // 2), lambda i: (i, 0)),
                  pl.BlockSpec((halo, d // 2), lambda i: (halo_index(i), 0)),
                  pl.BlockSpec((TM, d // 2), lambda i: (i + tiles, 0)),
                  pl.BlockSpec((halo, d // 2), lambda i: (halo_index(i) + tiles * halo_blocks, 0)),
                  pl.BlockSpec((TM, TOP_K), lambda i: (i, 0)),
                  pl.BlockSpec((halo, TOP_K), lambda i: (halo_index(i), 0)),
                  _const_spec((1, d)), _const_spec((1, d))]
        args += [yg, yg, yg, yg, gates, gates, g.reshape(1, -1), b.reshape(1, -1)]
    return specs, args


def _even_kernel(*refs, tiles_per_seq, layer, pending):
    n_src = 10 if pending else 2
    (win_hbm, wpool_ref, pscale_ref, lvg_ref, lvb_ref, ws_ref, bst_ref, wout_hbm, g_ref, b_ref, o_ref,
     a_scr, cat_scr, win_ref, wout_ref, stage, sems) = refs[n_src:]
    tm = o_ref.shape[0]
    d_pool = a_scr.shape[1]
    d_sg = lvg_ref.shape[1]
    pgd = d_pool // len(POOL_WINDOWS)
    hd_dim = d_sg // N_SG_HEADS
    seq_tile = pl.program_id(0) % tiles_per_seq

    @pl.when(pl.program_id(0) == 0)
    def _():
        _load_cast(win_hbm.at[layer], win_ref, stage, sems)
        _load_cast(wout_hbm.at[layer], wout_ref, stage, sems)

    x, xh = _mixer_input(refs[:n_src], pending)
    h = _dot(x.astype(BF16), win_ref[...])

    ah = _dot(xh.astype(BF16), win_ref[:, :d_pool])
    a_scr[0:POOL_HALO, :] = jnp.where(seq_tile == 0, 0.0, ah)
    a_scr[POOL_HALO:POOL_HALO + tm, :] = h[:, :d_pool]
    pos = seq_tile * tm + lax.broadcasted_iota(I32, (tm, 1), 0)
    for g, w in enumerate(POOL_WINDOWS):
        cs = slice(g * pgd, (g + 1) * pgd)
        tok = a_scr[POOL_HALO:POOL_HALO + tm, cs]
        acc = a_scr[:, cs]
        span = 1
        while span < w:
            acc = acc[span:, :] + acc[:-span, :]
            span *= 2
        acc = acc[acc.shape[0] - tm:, :]
        cnt = jnp.minimum(pos + 1, w).astype(F32)
        d = acc * (1.0 / cnt) - tok
        yg = _dot(d.astype(BF16), wpool_ref[g])
        cat_scr[:, cs] = (yg * pscale_ref[:, cs]).astype(BF16)

    z = _gelu_tanh(h[:, d_pool:])
    u = z[:, :d_sg]
    v = _layer_norm(z[:, d_sg:], lvg_ref[...], lvb_ref[...]).astype(BF16)
    row = lax.broadcasted_iota(I32, (CHUNK, CHUNK), 0)
    col = lax.broadcasted_iota(I32, (CHUNK, CHUNK), 1)
    for hd in range(N_SG_HEADS):
        hs = slice(hd * hd_dim, (hd + 1) * hd_dim)
        wsm = jnp.where(row >= col, ws_ref[hd], 0.0).astype(BF16)
        bcol = bst_ref[:, hd:hd + 1]
        for ck in range(tm // CHUNK):
            rs = slice(ck * CHUNK, (ck + 1) * CHUNK)
            sv = _dot(wsm, v[rs, hs]) + bcol
            cat_scr[rs, d_pool + hd * hd_dim:d_pool + (hd + 1) * hd_dim] = (u[rs, hs] * sv).astype(BF16)

    mix = _dot(cat_scr[...], wout_ref[...])
    o_ref[...] = _layer_norm(ALPHA * x + mix, g_ref[...], b_ref[...])


def _even_mixer(x, pending, seq, layer, w_in, w_pool, pool_scale, ln_v_g, ln_v_b, w_spatial, b_spatial,
                w_out, g, b):
    n, d = x.shape
    d_in = w_in.shape[2]
    d_pool = pool_scale.shape[0]
    d_sg = ln_v_g.shape[0]
    kern = functools.partial(_even_kernel, tiles_per_seq=seq // TM, layer=layer, pending=pending is not None)
    src_specs, src_args = _mixer_sources(x, pending, POOL_HALO)
    return pl.pallas_call(
        kern,
        out_shape=jax.ShapeDtypeStruct((n, d), F32),
        grid=(n // TM,),
        in_specs=src_specs + [
            _HBM,
            _const_spec(w_pool.shape),
            _const_spec((1, d_pool)),
            _const_spec((1, d_sg)),
            _const_spec((1, d_sg)),
            _const_spec(w_spatial.shape),
            _const_spec((CHUNK, N_SG_HEADS)),
            _HBM,
            _const_spec((1, d)),
            _const_spec((1, d)),
        ],
        out_specs=pl.BlockSpec((TM, d), lambda i: (i, 0)),
        scratch_shapes=[pltpu.VMEM((POOL_HALO + TM, d_pool), F32), pltpu.VMEM((TM, d_pool + d_sg), BF16),
                        pltpu.VMEM((d, d_in), BF16), pltpu.VMEM((d_pool + d_sg, d), BF16)] + _stage_scratch(d),
        compiler_params=_params(),
        name="even_mixer",
    )(*src_args, w_in, w_pool.astype(BF16), pool_scale.reshape(1, -1), ln_v_g.reshape(1, -1),
      ln_v_b.reshape(1, -1), w_spatial, b_spatial.T, w_out, g.reshape(1, -1), b.reshape(1, -1))


def _odd_kernel(*refs, tiles_per_seq, layer, pending):
    n_src = 10 if pending else 2
    (win_hbm, cwt_ref, cb_ref, wout_hbm, g_ref, b_ref, o_ref, zc_scr,
     win_ref, wout_ref, stage, sems) = refs[n_src:]
    tm, d = o_ref.shape
    seq_tile = pl.program_id(0) % tiles_per_seq

    @pl.when(pl.program_id(0) == 0)
    def _():
        _load_cast(win_hbm.at[layer], win_ref, stage, sems)
        _load_cast(wout_hbm.at[layer], wout_ref, stage, sems)

    x, xh = _mixer_input(refs[:n_src], pending)
    xb = x.astype(BF16)
    hc = _dot(xb, win_ref[:, d:2 * d])
    hz = _dot(xb, win_ref[:, 2 * d:])
    hh = _dot(xh.astype(BF16), win_ref[:, d:])
    zc_scr[0:CONV_HALO, :] = jnp.where(seq_tile == 0, 0.0, hh[:, :d] * hh[:, d:])
    zc_scr[CONV_HALO:CONV_HALO + tm, :] = hc * hz
    gate = _dot(xb, win_ref[:, :d])
    conv = cb_ref[...]
    for j in range(CONV_WIDTH):
        off = CONV_HALO - (CONV_WIDTH - 1) + j
        conv = conv + zc_scr[off:off + tm, :] * cwt_ref[j:j + 1, :]
    y = _dot((gate * conv).astype(BF16), wout_ref[...])
    o_ref[...] = _layer_norm(ALPHA * x + y, g_ref[...], b_ref[...])


def _odd_mixer(x, pending, seq, layer, w_in, conv_w, conv_b, w_out, g, b):
    n, d = x.shape
    kern = functools.partial(_odd_kernel, tiles_per_seq=seq // TM, layer=layer, pending=pending is not None)
    src_specs, src_args = _mixer_sources(x, pending, CONV_HALO)
    return pl.pallas_call(
        kern,
        out_shape=jax.ShapeDtypeStruct((n, d), F32),
        grid=(n // TM,),
        in_specs=src_specs + [
            _HBM,
            _const_spec((CONV_WIDTH, d)),
            _const_spec((1, d)),
            _HBM,
            _const_spec((1, d)),
            _const_spec((1, d)),
        ],
        out_specs=pl.BlockSpec((TM, d), lambda i: (i, 0)),
        scratch_shapes=[pltpu.VMEM((CONV_HALO + TM, d), F32),
                        pltpu.VMEM(w_in.shape[1:], BF16), pltpu.VMEM(w_out.shape[1:], BF16)] + _stage_scratch(d),
        compiler_params=_params(),
        name="odd_mixer",
    )(*src_args, w_in, conv_w.T, conv_b.reshape(1, -1), w_out, g.reshape(1, -1), b.reshape(1, -1))


def _kv_kernel(mem_ref, wk_ref, wv_ref, k_ref, v_ref):
    m = mem_ref[...].astype(BF16)
    k_ref[...] = _dot(m, wk_ref[...].astype(BF16)).astype(BF16)
    v_ref[...] = _dot(m, wv_ref[...].astype(BF16)).astype(BF16)


def _memory_kv(mem2d, wk, wv):
    nl, d, _ = wk.shape
    rows = mem2d.shape[0]
    out = jax.ShapeDtypeStruct((nl, rows, d), BF16)
    wspec = pl.BlockSpec((None, d, d), lambda l: (l, 0, 0))
    ospec = pl.BlockSpec((None, rows, d), lambda l: (l, 0, 0))
    return pl.pallas_call(
        _kv_kernel,
        out_shape=(out, out),
        grid=(nl,),
        in_specs=[_const_spec((rows, d)), wspec, wspec],
        out_specs=(ospec, ospec),
        compiler_params=_params(),
        name="memory_kv",
    )(mem2d, wk, wv)


def _attn_kernel(x_ref, k_ref, v_ref, wq_hbm, wo_hbm, g_ref, b_ref, wr_ref, br_ref,
                 o_ref, op_ref, rt_ref, rg_ref, cnt_ref, o_scr, carry_scr, wq_ref, wo_ref, stage, sems,
                 *, layer):
    tm, d = x_ref.shape
    hd_dim = d // N_XHEADS
    sub = tm // SUB_TILES

    @pl.when(pl.program_id(0) == 0)
    def _():
        carry_scr[...] = jnp.zeros_like(carry_scr)
        _load_cast(wq_hbm.at[layer], wq_ref, stage, sems)
        _load_cast(wo_hbm.at[layer], wo_ref, stage, sems)

    for st in range(SUB_TILES):
        rs = slice(st * sub, (st + 1) * sub)
        x = x_ref[rs, :]
        q = _dot(x.astype(BF16), wq_ref[...]) * (1.0 / math.sqrt(hd_dim))
        for hd in range(N_XHEADS):
            hs = slice(hd * hd_dim, (hd + 1) * hd_dim)
            s = lax.dot_general(q[:, hs].astype(BF16), k_ref[:, hs], _NT, preferred_element_type=F32)
            p = jnp.exp(s - jnp.max(s, axis=-1, keepdims=True))
            p = p * (1.0 / jnp.sum(p, axis=-1, keepdims=True))
            o_scr[rs, hs] = _dot(p.astype(BF16), v_ref[:, hs]).astype(BF16)
        xa = _dot(o_scr[rs, :], wo_ref[...])
        out = _layer_norm(ALPHA * x + xa, g_ref[...], b_ref[...])
        o_ref[rs, :] = out
        op_ref[rs, :] = _pack_halves(out)
        table, gates = _route_rows(out, wr_ref, br_ref, carry_scr)
        rt_ref[:, rs] = table
        rg_ref[rs, :] = gates
    cnt_ref[...] = carry_scr[...].astype(I32)


def _cross_attn(x, seq, layer, k, v, wq, wo, g, b, router_w, router_b):
    n, d = x.shape
    m = k.shape[2]
    tiles_per_seq = seq // TM
    kvspec = pl.BlockSpec((None, None, m, d), lambda i: (layer, i // tiles_per_seq, 0, 0))
    out, packed, table, gates, cnt = pl.pallas_call(
        functools.partial(_attn_kernel, layer=layer),
        out_shape=(jax.ShapeDtypeStruct((n, d), F32), jax.ShapeDtypeStruct((n, d // 2), I32),
                   jax.ShapeDtypeStruct((ROUTE_ROWS, n), F32), jax.ShapeDtypeStruct((n, TOP_K), F32),
                   jax.ShapeDtypeStruct((1, ROUTE_COLS), I32)),
        grid=(n // TM,),
        in_specs=[
            pl.BlockSpec((TM, d), lambda i: (i, 0)),
            kvspec, kvspec,
            _HBM, _HBM,
            _const_spec((1, d)), _const_spec((1, d)),
            _const_spec((d, 2 * ROUTE_COLS)), _const_spec((1, ROUTE_COLS)),
        ],
        out_specs=(pl.BlockSpec((TM, d), lambda i: (i, 0)), pl.BlockSpec((TM, d // 2), lambda i: (i, 0)),
                   pl.BlockSpec((ROUTE_ROWS, TM), lambda i: (0, i)), pl.BlockSpec((TM, TOP_K), lambda i: (i, 0)),
                   _const_spec((1, ROUTE_COLS))),
        scratch_shapes=[pltpu.VMEM((TM, d), BF16), pltpu.VMEM((1, ROUTE_COLS), F32),
                        pltpu.VMEM((d, d), BF16), pltpu.VMEM((d, d), BF16)] + _stage_scratch(d),
        compiler_params=_params(),
        name="cross_attn",
    )(x, k, v, wq, wo, g.reshape(1, -1), b.reshape(1, -1), router_w, router_b)
    return out, packed, table, gates, cnt[0, N_GROUPS:N_GROUPS + N_EXPERTS]


def _route_rows(x, wcat_ref, bias_ref, carry_scr):
    tm = x.shape[0]

    xh = x.astype(BF16)
    xl = (x - xh.astype(F32)).astype(BF16)
    r1 = _dot(xh, wcat_ref[...])
    r2 = _dot(xl, wcat_ref[:, :ROUTE_COLS])
    logits = r1[:, :ROUTE_COLS] + r1[:, ROUTE_COLS:] + r2 + bias_ref[...]

    lane = lax.broadcasted_iota(I32, (tm, ROUTE_COLS), 1).astype(F32)
    neg = -jnp.inf

    def first_argmax(vals):
        mx = jnp.max(vals, axis=-1, keepdims=True)
        idx = jnp.min(jnp.where(vals == mx, lane, float(ROUTE_COLS)), axis=-1, keepdims=True)
        return mx, idx

    gl = jnp.where(lane < N_GROUPS, logits, neg)
    gmax, g_sel = first_argmax(gl)
    gate_g = 1.0 / jnp.sum(jnp.exp(gl - gmax), axis=-1, keepdims=True)

    lo = N_GROUPS + g_sel * EXPERTS_PER_GROUP
    el = jnp.where((lane >= lo) & (lane < lo + EXPERTS_PER_GROUP), logits, neg)
    m1, i1 = first_argmax(el)
    m2, i2 = first_argmax(jnp.where(lane == i1, neg, el))
    e21 = jnp.exp(m2 - m1)
    w1 = 1.0 / (1.0 + e21)
    w2 = e21 / (1.0 + e21)

    oh1 = lane == i1
    oh2 = lane == i2
    oh = (oh1 | oh2).astype(BF16)
    r = lax.broadcasted_iota(I32, (tm, tm), 0)
    c = lax.broadcasted_iota(I32, (tm, tm), 1)
    before = _dot((r > c).astype(BF16), oh) + carry_scr[...]
    rank1 = jnp.sum(jnp.where(oh1, before, 0.0), axis=-1, keepdims=True)
    rank2 = jnp.sum(jnp.where(oh2, before, 0.0), axis=-1, keepdims=True)
    carry_scr[...] += jnp.sum(oh.astype(F32), axis=0, keepdims=True)

    cols = jnp.where(lane == 0.0, i1 - N_GROUPS,
                     jnp.where(lane == 1.0, i2 - N_GROUPS,
                               jnp.where(lane == 2.0, rank1, jnp.where(lane == 3.0, rank2, 0.0))))
    table = jnp.transpose(cols)[:ROUTE_ROWS, :]
    l2 = lax.broadcasted_iota(I32, (tm, TOP_K), 1)
    gates = jnp.where(l2 == 0, gate_g * w1, gate_g * w2)
    return table, gates


def _router_weights(wr_g, br_g, wr_e, br_e):
    d = wr_g.shape[0]
    w = jnp.concatenate([wr_g, jnp.transpose(wr_e, (1, 0, 2)).reshape(d, N_EXPERTS)], axis=1)
    w = jnp.pad(w, ((0, 0), (0, ROUTE_COLS - w.shape[1])))
    w_hi = w.astype(BF16)
    w_lo = (w - w_hi.astype(F32)).astype(BF16)
    bias = jnp.pad(jnp.concatenate([br_g, br_e.reshape(-1)]), (0, ROUTE_COLS - N_GROUPS - N_EXPERTS))
    return jnp.concatenate([w_hi, w_lo], axis=1), bias.reshape(1, -1)


def _sc_gather(table, idx):
    b = idx.shape[0]
    d = table.shape[1]
    per_worker = b // SC_WORKERS
    n_chunks = per_worker // SC_CHUNK
    assert per_worker * SC_WORKERS == b and n_chunks * SC_CHUNK == per_worker and n_chunks % 2 == 0
    mesh = plsc.VectorSubcoreMesh(core_axis_name="c", subcore_axis_name="s")

    @functools.partial(
        pl.kernel, mesh=mesh,
        out_type=jax.ShapeDtypeStruct((b, d), table.dtype),
        scratch_types=[pltpu.VMEM((per_worker,), I32),
                       pltpu.VMEM((2, SC_CHUNK, d), table.dtype),
                       pltpu.SemaphoreType.DMA((2,)),
                       pltpu.SemaphoreType.DMA((2,))],
        name="sc_gather",
    )
    def gather(table_hbm, idx_hbm, out_hbm, idx_v, rows_v, gsem, wsem):
        wid = lax.axis_index("s") * SC_CORES + lax.axis_index("c")
        base = wid * per_worker
        pltpu.sync_copy(idx_hbm.at[pl.ds(base, per_worker)], idx_v)

        def fetch(c, slot):
            off = pl.multiple_of(c * SC_CHUNK, SC_CHUNK)
            return pltpu.make_async_copy(table_hbm.at[idx_v.at[pl.ds(off, SC_CHUNK)]], rows_v.at[slot],
                                         gsem.at[slot])

        def put(c, slot):
            off = pl.multiple_of(c * SC_CHUNK, SC_CHUNK)
            return pltpu.make_async_copy(rows_v.at[slot], out_hbm.at[pl.ds(base + off, SC_CHUNK)],
                                         wsem.at[slot])

        fetch(0, 0).start()

        @pl.loop(0, n_chunks, step=2)
        def _(c0):
            for slot in range(2):
                c = c0 + slot

                @pl.when(c + 1 < n_chunks)
                def _():
                    @pl.when(c >= 1)
                    def _():
                        put(c - 1, 1 - slot).wait()
                    fetch(c + 1, 1 - slot).start()

                fetch(c, slot).wait()
                put(c, slot).start()

        put(n_chunks - 2, 0).wait()
        put(n_chunks - 1, 1).wait()

    return gather(table, idx)


def _sc_row_tokens(dest_flat, rows, n):
    a = dest_flat.shape[0]
    lanes = SC_LANES
    per_worker = rows // SC_WORKERS
    assert a % lanes == 0 and per_worker * SC_WORKERS == rows and per_worker % lanes == 0
    mesh = plsc.VectorSubcoreMesh(core_axis_name="c", subcore_axis_name="s")

    @functools.partial(
        pl.kernel, mesh=mesh,
        out_type=jax.ShapeDtypeStruct((rows,), I32),
        scratch_types=[pltpu.VMEM((a,), I32), pltpu.VMEM((per_worker,), I32)],
        compiler_params=pltpu.CompilerParams(needs_layout_passes=False),
        name="sc_row_tokens",
    )
    def invert(dest_hbm, out_hbm, dest_v, map_v):
        wid = lax.axis_index("s") * SC_CORES + lax.axis_index("c")
        lo = wid * per_worker
        pltpu.sync_copy(dest_hbm, dest_v)
        lane = lax.iota(I32, lanes)

        @pl.loop(0, per_worker // lanes)
        def _(i):
            map_v[pl.ds(i * lanes, lanes)] = lax.rem(lo + i * lanes + lane, n)

        @pl.loop(0, a // lanes)
        def _(i):
            local = dest_v[pl.ds(i * lanes, lanes)] - lo
            mine = (local >= 0) & (local < per_worker)
            plsc.store_scatter(map_v, [jnp.where(mine, local, 0)], lax.rem(i * lanes + lane, n), mask=mine)

        pltpu.sync_copy(map_v, out_hbm.at[pl.ds(lo, per_worker)])

    return invert(dest_flat)


def _expert_kernel(sched_ref, nb_ref, xs_ref, w1_hbm, w3_hbm, w2_hbm, y_ref,
                   w1_buf, w3_buf, w2_buf, w1_scr, w3_scr, w2_scr, sems, *, layer):
    b = pl.program_id(0)
    used = b < nb_ref[0]
    expert, slot, run_start, next_expert = (sched_ref[r, b] for r in range(4))

    def fetch(e, s):
        return [pltpu.make_async_copy(w_hbm.at[layer, e], buf.at[s], sems.at[s, j])
                for j, (w_hbm, buf) in enumerate(((w1_hbm, w1_buf), (w3_hbm, w3_buf), (w2_hbm, w2_buf)))]

    @pl.when(used & (run_start == 1))
    def _():
        @pl.when(b == 0)
        def _():
            for c in fetch(expert, slot):
                c.start()

        for c in fetch(expert, slot):
            c.wait()

        @pl.when(next_expert >= 0)
        def _():
            for c in fetch(next_expert, 1 - slot):
                c.start()

        w1_scr[...] = w1_buf[slot].astype(BF16)
        w3_scr[...] = w3_buf[slot].astype(BF16)
        w2_scr[...] = w2_buf[slot].astype(BF16)

    @pl.when(used)
    def _():
        x_lo, x_hi = _unpack_halves(xs_ref[...])
        xb = jnp.concatenate([x_lo.astype(BF16), x_hi.astype(BF16)], axis=1)
        h1 = _dot(xb, w1_scr[...])
        h3 = _dot(xb, w3_scr[...])
        hid = h1 * (1.0 / (1.0 + jnp.exp(-h1))) * h3
        y_ref[...] = _pack_halves(_dot(hid.astype(BF16), w2_scr[...]))

    @pl.when(jnp.logical_not(used))
    def _():
        y_ref[...] = jnp.zeros_like(y_ref)


def _expert_mlp(xs, schedule, n_used, layer, w1, w3, w2, n_blocks):
    d, de = w1.shape[2], w1.shape[3]

    def row_map(b, sched, nb):
        return (jnp.minimum(b, nb[0] - 1), 0)

    return pl.pallas_call(
        functools.partial(_expert_kernel, layer=layer),
        out_shape=jax.ShapeDtypeStruct((n_blocks * EXPERT_BLOCK, d // 2), I32),
        grid_spec=pltpu.PrefetchScalarGridSpec(
            num_scalar_prefetch=2,
            grid=(n_blocks,),
            in_specs=[pl.BlockSpec((EXPERT_BLOCK, d // 2), row_map), _HBM, _HBM, _HBM],
            out_specs=pl.BlockSpec((EXPERT_BLOCK, d // 2), lambda b, sched, nb: (b, 0)),
            scratch_shapes=[pltpu.VMEM((2, d, de), F32), pltpu.VMEM((2, d, de), F32), pltpu.VMEM((2, de, d), F32),
                            pltpu.VMEM((d, de), BF16), pltpu.VMEM((d, de), BF16), pltpu.VMEM((de, d), BF16),
                            pltpu.SemaphoreType.DMA((2, 3))],
        ),
        compiler_params=_params(),
        name="expert_mlp",
    )(schedule, n_used, xs, w1, w3, w2)


def _combine_kernel(x_ref, y0_ref, y1_ref, gate_ref, g_ref, b_ref, o_ref):
    o_ref[...] = _moe_output(x_ref, y0_ref, y1_ref, gate_ref, g_ref, b_ref)


def _combine(x, yg, gates, g, b):
    n, d = x.shape
    tiles = n // TM
    return pl.pallas_call(
        _combine_kernel,
        out_shape=jax.ShapeDtypeStruct((n, d), F32),
        grid=(tiles,),
        in_specs=[pl.BlockSpec((TM, d), lambda i: (i, 0)),
                  pl.BlockSpec((TM, d // 2), lambda i: (i, 0)),
                  pl.BlockSpec((TM, d // 2), lambda i: (i + tiles, 0)),
                  pl.BlockSpec((TM, TOP_K), lambda i: (i, 0)),
                  _const_spec((1, d)), _const_spec((1, d))],
        out_specs=pl.BlockSpec((TM, d), lambda i: (i, 0)),
        compiler_params=_params(),
        name="combine",
    )(x, yg, yg, gates, g.reshape(1, -1), b.reshape(1, -1))


def _moe_experts(x_packed, table, counts, layer, w1, w3, w2):
    n = x_packed.shape[0]
    n_blocks = (n * TOP_K + N_EXPERTS * (EXPERT_BLOCK - 1) + EXPERT_BLOCK - 1) // EXPERT_BLOCK
    experts = table[:TOP_K].astype(I32)
    ranks = table[TOP_K:2 * TOP_K].astype(I32)

    blocks_e = (counts + EXPERT_BLOCK - 1) // EXPERT_BLOCK
    blocks_end = jnp.cumsum(blocks_e)
    run_start = (blocks_end - blocks_e) * EXPERT_BLOCK
    n_used = blocks_end[-1:].astype(I32)
    block_ids = jnp.arange(n_blocks, dtype=I32)
    block_expert = jnp.minimum(jnp.sum(blocks_end[None, :] <= block_ids[:, None], axis=1),
                               N_EXPERTS - 1).astype(I32)
    run_start_flag = jnp.concatenate([jnp.ones((1,), I32),
                                      (block_expert[1:] != block_expert[:-1]).astype(I32)])
    slot = (jnp.cumsum(run_start_flag) - 1) % 2
    next_block = blocks_end[block_expert]
    next_expert = jnp.where(next_block < n_used[0],
                            block_expert[jnp.minimum(next_block, n_blocks - 1)], -1)
    schedule = jnp.stack([block_expert, slot, run_start_flag, next_expert]).astype(I32)
    expert_ids = jnp.arange(N_EXPERTS, dtype=I32)
    start_of = jnp.sum(jnp.where(experts[:, :, None] == expert_ids, run_start, 0), axis=-1)
    dest = (start_of + ranks).astype(I32).reshape(-1)
    row_token = _sc_row_tokens(dest, n_blocks * EXPERT_BLOCK, n)

    xs = _sc_gather(x_packed, row_token)
    y = _expert_mlp(xs, schedule, n_used, layer, w1, w3, w2, n_blocks)
    return _sc_gather(y, dest)


def kernel(x, mem, w_in_even, w_pool, pool_scale, ln_v_g, ln_v_b, w_spatial, b_spatial, w_out_even,
           w_in_odd, conv_w, conv_b, w_out_odd, wq_x, wk_x, wv_x, wo_x, ln_g, ln_b, wr_group,
           br_group, wr_expert, br_expert, w1, w3, w2):
    bsz, seq, d = x.shape
    assert seq % TM == 0 and d % LANES == 0
    mlen = mem.shape[1]
    k_all, v_all = _memory_kv(mem.reshape(bsz * mlen, d), wk_x, wv_x)
    k_all = k_all.reshape(DEPTH, bsz, mlen, d)
    v_all = v_all.reshape(DEPTH, bsz, mlen, d)
    h = x.reshape(bsz * seq, d)
    pending = None
    for l in range(DEPTH):
        i = l // 2
        if l % 2 == 0:
            h = _even_mixer(h, pending, seq, i, w_in_even, w_pool[i], pool_scale[i], ln_v_g[i], ln_v_b[i],
                            w_spatial[i], b_spatial[i], w_out_even, ln_g[l, 0], ln_b[l, 0])
        else:
            h = _odd_mixer(h, pending, seq, i, w_in_odd, conv_w[i], conv_b[i], w_out_odd,
                           ln_g[l, 0], ln_b[l, 0])
        router_w, router_b = _router_weights(wr_group[l], br_group[l], wr_expert[l], br_expert[l])
        h, hp, table, gates, counts = _cross_attn(h, seq, l, k_all, v_all, wq_x, wo_x,
                                                  ln_g[l, 1], ln_b[l, 1], router_w, router_b)
        yg = _moe_experts(hp, table, counts, l, w1, w3, w2)
        pending = (yg, gates, ln_g[l, 2], ln_b[l, 2])
    return _combine(h, *pending).reshape(bsz, seq, d)
```

```python
import functools
import math

import jax
import jax.numpy as jnp
from jax import lax
from jax.experimental import pallas as pl
from jax.experimental.pallas import tpu as pltpu
from jax.experimental.pallas import tpu_sc as plsc

F32 = jnp.float32
BF16 = jnp.bfloat16
I32 = jnp.int32

POOL_WINDOWS = (2, 4, 8, 16)
assert all(w & (w - 1) == 0 for w in POOL_WINDOWS)
N_SG_HEADS = 4
CHUNK = 128
CONV_WIDTH = 3
N_XHEADS = 4
N_GROUPS = 4
EXPERTS_PER_GROUP = 8
N_EXPERTS = N_GROUPS * EXPERTS_PER_GROUP
TOP_K = 2
DEPTH = 4
ALPHA = (2.0 * DEPTH) ** 0.25
LN_EPS = 1e-5

LANES = 128
SC_CORES = 2
SC_WORKERS = 32
SC_LANES = 16
SC_CHUNK = 64
TM = 1024
SUB_TILES = 2
ROUTE_ROWS = 8
POOL_HALO = 16
CONV_HALO = 8
EXPERT_BLOCK = 512
ROUTE_COLS = 128
STAGE_COLS = 512
VMEM_LIMIT = 56 * 1024 * 1024

_NT = (((1,), (1,)), ((), ()))


def _dot(a, b):
    return jnp.dot(a, b, preferred_element_type=F32)


def _layer_norm(y, g, b):
    mu = jnp.mean(y, axis=-1, keepdims=True)
    yc = y - mu
    var = jnp.mean(yc * yc, axis=-1, keepdims=True)
    return yc * lax.rsqrt(var + LN_EPS) * g + b


def _gelu_tanh(x):
    c = math.sqrt(2.0 / math.pi)
    return 0.5 * x * (1.0 + jnp.tanh(c * (x + 0.044715 * (x * x * x))))


def _pack_halves(v):
    c = v.shape[1] // 2
    lo = pltpu.bitcast(v[:, :c].astype(BF16).astype(F32), jnp.uint32)
    hi = pltpu.bitcast(v[:, c:].astype(BF16).astype(F32), jnp.uint32)
    return pltpu.bitcast((hi & jnp.uint32(0xFFFF0000)) | (lo >> 16), I32)


def _unpack_halves(w):
    u = pltpu.bitcast(w, jnp.uint32)
    return pltpu.bitcast(u << 16, F32), pltpu.bitcast(u & jnp.uint32(0xFFFF0000), F32)


def _load_cast(w_hbm, w_scr, stage, sems):
    chunks = w_scr.shape[1] // STAGE_COLS

    def chunk_copy(c):
        return pltpu.make_async_copy(w_hbm.at[:, pl.ds(c * STAGE_COLS, STAGE_COLS)], stage.at[c % 2],
                                     sems.at[c % 2])

    chunk_copy(0).start()
    for c in range(chunks):
        if c + 1 < chunks:
            chunk_copy(c + 1).start()
        chunk_copy(c).wait()
        w_scr[:, c * STAGE_COLS:(c + 1) * STAGE_COLS] = stage[c % 2].astype(BF16)


def _stage_scratch(rows):
    return [pltpu.VMEM((2, rows, STAGE_COLS), F32), pltpu.SemaphoreType.DMA((2,))]


_HBM = pl.BlockSpec(memory_space=pl.ANY)


def _const_spec(shape):
    nd = len(shape)
    return pl.BlockSpec(shape, lambda i: (0,) * nd)


def _params():
    return pltpu.CompilerParams(dimension_semantics=("arbitrary",), vmem_limit_bytes=VMEM_LIMIT)


def _moe_output(x_ref, y0_ref, y1_ref, gate_ref, g_ref, b_ref):
    gates = gate_ref[...]
    g0, g1 = gates[:, 0:1], gates[:, 1:2]
    y0_lo, y0_hi = _unpack_halves(y0_ref[...])
    y1_lo, y1_hi = _unpack_halves(y1_ref[...])
    ff = jnp.concatenate([g0 * y0_lo + g1 * y1_lo, g0 * y0_hi + g1 * y1_hi], axis=1)
    return _layer_norm(ALPHA * x_ref[...] + ff, g_ref[...], b_ref[...])


def _mixer_input(src, pending):
    if not pending:
        x_ref, xh_ref = src
        return x_ref[...], xh_ref[...]
    x_ref, xh_ref, y0_ref, y0h_ref, y1_ref, y1h_ref, gate_ref, gateh_ref, g_ref, b_ref = src
    return (_moe_output(x_ref, y0_ref, y1_ref, gate_ref, g_ref, b_ref),
            _moe_output(xh_ref, y0h_ref, y1h_ref, gateh_ref, g_ref, b_ref))


def _mixer_sources(x, pending, halo):
    n, d = x.shape
    tiles = n // TM
    halo_blocks = TM // halo

    def halo_index(i):
        return jnp.maximum(i * halo_blocks - 1, 0)

    specs = [pl.BlockSpec((TM, d), lambda i: (i, 0)), pl.BlockSpec((halo, d), lambda i: (halo_index(i), 0))]
    args = [x, x]
    if pending is not None:
        yg, gates, g, b = pending
        specs += [pl.BlockSpec((TM, d // 2), lambda i: (i, 0)),
                  pl.BlockSpec((halo, d // 2), lambda i: (halo_index(i), 0)),
                  pl.BlockSpec((TM, d // 2), lambda i: (i + tiles, 0)),
                  pl.BlockSpec((halo, d // 2), lambda i: (halo_index(i) + tiles * halo_blocks, 0)),
                  pl.BlockSpec((TM, TOP_K), lambda i: (i, 0)),
                  pl.BlockSpec((halo, TOP_K), lambda i: (halo_index(i), 0)),
                  _const_spec((1, d)), _const_spec((1, d))]
        args += [yg, yg, yg, yg, gates, gates, g.reshape(1, -1), b.reshape(1, -1)]
    return specs, args


def _even_kernel(*refs, tiles_per_seq, layer, pending):
    n_src = 10 if pending else 2
    (win_hbm, wpool_ref, pscale_ref, lvg_ref, lvb_ref, ws_ref, bst_ref, wout_hbm, g_ref, b_ref, o_ref,
     a_scr, cat_scr, win_ref, wout_ref, stage, sems) = refs[n_src:]
    tm = o_ref.shape[0]
    d_pool = a_scr.shape[1]
    d_sg = lvg_ref.shape[1]
    pgd = d_pool // len(POOL_WINDOWS)
    hd_dim = d_sg // N_SG_HEADS
    seq_tile = pl.program_id(0) % tiles_per_seq

    @pl.when(pl.program_id(0) == 0)
    def _():
        _load_cast(win_hbm.at[layer], win_ref, stage, sems)
        _load_cast(wout_hbm.at[layer], wout_ref, stage, sems)

    x, xh = _mixer_input(refs[:n_src], pending)
    h = _dot(x.astype(BF16), win_ref[...])

    ah = _dot(xh.astype(BF16), win_ref[:, :d_pool])
    a_scr[0:POOL_HALO, :] = jnp.where(seq_tile == 0, 0.0, ah)
    a_scr[POOL_HALO:POOL_HALO + tm, :] = h[:, :d_pool]
    pos = seq_tile * tm + lax.broadcasted_iota(I32, (tm, 1), 0)
    for g, w in enumerate(POOL_WINDOWS):
        cs = slice(g * pgd, (g + 1) * pgd)
        tok = a_scr[POOL_HALO:POOL_HALO + tm, cs]
        acc = a_scr[:, cs]
        span = 1
        while span < w:
            acc = acc[span:, :] + acc[:-span, :]
            span *= 2
        acc = acc[acc.shape[0] - tm:, :]
        cnt = jnp.minimum(pos + 1, w).astype(F32)
        d = acc * (1.0 / cnt) - tok
        yg = _dot(d.astype(BF16), wpool_ref[g])
        cat_scr[:, cs] = (yg * pscale_ref[:, cs]).astype(BF16)

    z = _gelu_tanh(h[:, d_pool:])
    u = z[:, :d_sg]
    v = _layer_norm(z[:, d_sg:], lvg_ref[...], lvb_ref[...]).astype(BF16)
    row = lax.broadcasted_iota(I32, (CHUNK, CHUNK), 0)
    col = lax.broadcasted_iota(I32, (CHUNK, CHUNK), 1)
    for hd in range(N_SG_HEADS):
        hs = slice(hd * hd_dim, (hd + 1) * hd_dim)
        wsm = jnp.where(row >= col, ws_ref[hd], 0.0).astype(BF16)
        bcol = bst_ref[:, hd:hd + 1]
        for ck in range(tm // CHUNK):
            rs = slice(ck * CHUNK, (ck + 1) * CHUNK)
            sv = _dot(wsm, v[rs, hs]) + bcol
            cat_scr[rs, d_pool + hd * hd_dim:d_pool + (hd + 1) * hd_dim] = (u[rs, hs] * sv).astype(BF16)

    mix = _dot(cat_scr[...], wout_ref[...])
    o_ref[...] = _layer_norm(ALPHA * x + mix, g_ref[...], b_ref[...])


def _even_mixer(x, pending, seq, layer, w_in, w_pool, pool_scale, ln_v_g, ln_v_b, w_spatial, b_spatial,
                w_out, g, b):
    n, d = x.shape
    d_in = w_in.shape[2]
    d_pool = pool_scale.shape[0]
    d_sg = ln_v_g.shape[0]
    kern = functools.partial(_even_kernel, tiles_per_seq=seq // TM, layer=layer, pending=pending is not None)
    src_specs, src_args = _mixer_sources(x, pending, POOL_HALO)
    return pl.pallas_call(
        kern,
        out_shape=jax.ShapeDtypeStruct((n, d), F32),
        grid=(n // TM,),
        in_specs=src_specs + [
            _HBM,
            _const_spec(w_pool.shape),
            _const_spec((1, d_pool)),
            _const_spec((1, d_sg)),
            _const_spec((1, d_sg)),
            _const_spec(w_spatial.shape),
            _const_spec((CHUNK, N_SG_HEADS)),
            _HBM,
            _const_spec((1, d)),
            _const_spec((1, d)),
        ],
        out_specs=pl.BlockSpec((TM, d), lambda i: (i, 0)),
        scratch_shapes=[pltpu.VMEM((POOL_HALO + TM, d_pool), F32), pltpu.VMEM((TM, d_pool + d_sg), BF16),
                        pltpu.VMEM((d, d_in), BF16), pltpu.VMEM((d_pool + d_sg, d), BF16)] + _stage_scratch(d),
        compiler_params=_params(),
        name="even_mixer",
    )(*src_args, w_in, w_pool.astype(BF16), pool_scale.reshape(1, -1), ln_v_g.reshape(1, -1),
      ln_v_b.reshape(1, -1), w_spatial, b_spatial.T, w_out, g.reshape(1, -1), b.reshape(1, -1))


def _odd_kernel(*refs, tiles_per_seq, layer, pending):
    n_src = 10 if pending else 2
    (win_hbm, cwt_ref, cb_ref, wout_hbm, g_ref, b_ref, o_ref, zc_scr,
     win_ref, wout_ref, stage, sems) = refs[n_src:]
    tm, d = o_ref.shape
    seq_tile = pl.program_id(0) % tiles_per_seq

    @pl.when(pl.program_id(0) == 0)
    def _():
        _load_cast(win_hbm.at[layer], win_ref, stage, sems)
        _load_cast(wout_hbm.at[layer], wout_ref, stage, sems)

    x, xh = _mixer_input(refs[:n_src], pending)
    xb = x.astype(BF16)
    hc = _dot(xb, win_ref[:, d:2 * d])
    hz = _dot(xb, win_ref[:, 2 * d:])
    hh = _dot(xh.astype(BF16), win_ref[:, d:])
    zc_scr[0:CONV_HALO, :] = jnp.where(seq_tile == 0, 0.0, hh[:, :d] * hh[:, d:])
    zc_scr[CONV_HALO:CONV_HALO + tm, :] = hc * hz
    gate = _dot(xb, win_ref[:, :d])
    conv = cb_ref[...]
    for j in range(CONV_WIDTH):
        off = CONV_HALO - (CONV_WIDTH - 1) + j
        conv = conv + zc_scr[off:off + tm, :] * cwt_ref[j:j + 1, :]
    y = _dot((gate * conv).astype(BF16), wout_ref[...])
    o_ref[...] = _layer_norm(ALPHA * x + y, g_ref[...], b_ref[...])


def _odd_mixer(x, pending, seq, layer, w_in, conv_w, conv_b, w_out, g, b):
    n, d = x.shape
    kern = functools.partial(_odd_kernel, tiles_per_seq=seq // TM, layer=layer, pending=pending is not None)
    src_specs, src_args = _mixer_sources(x, pending, CONV_HALO)
    return pl.pallas_call(
        kern,
        out_shape=jax.ShapeDtypeStruct((n, d), F32),
        grid=(n // TM,),
        in_specs=src_specs + [
            _HBM,
            _const_spec((CONV_WIDTH, d)),
            _const_spec((1, d)),
            _HBM,
            _const_spec((1, d)),
            _const_spec((1, d)),
        ],
        out_specs=pl.BlockSpec((TM, d), lambda i: (i, 0)),
        scratch_shapes=[pltpu.VMEM((CONV_HALO + TM, d), F32),
                        pltpu.VMEM(w_in.shape[1:], BF16), pltpu.VMEM(w_out.shape[1:], BF16)] + _stage_scratch(d),
        compiler_params=_params(),
        name="odd_mixer",
    )(*src_args, w_in, conv_w.T, conv_b.reshape(1, -1), w_out, g.reshape(1, -1), b.reshape(1, -1))


def _kv_kernel(mem_ref, wk_ref, wv_ref, k_ref, v_ref):
    m = mem_ref[...].astype(BF16)
    k_ref[...] = _dot(m, wk_ref[...].astype(BF16)).astype(BF16)
    v_ref[...] = _dot(m, wv_ref[...].astype(BF16)).astype(BF16)


def _memory_kv(mem2d, wk, wv):
    nl, d, _ = wk.shape
    rows = mem2d.shape[0]
    out = jax.ShapeDtypeStruct((nl, rows, d), BF16)
    wspec = pl.BlockSpec((None, d, d), lambda l: (l, 0, 0))
    ospec = pl.BlockSpec((None, rows, d), lambda l: (l, 0, 0))
    return pl.pallas_call(
        _kv_kernel,
        out_shape=(out, out),
        grid=(nl,),
        in_specs=[_const_spec((rows, d)), wspec, wspec],
        out_specs=(ospec, ospec),
        compiler_params=_params(),
        name="memory_kv",
    )(mem2d, wk, wv)


def _attn_kernel(x_ref, k_ref, v_ref, wq_hbm, wo_hbm, g_ref, b_ref, wr_ref, br_ref,
                 o_ref, op_ref, rt_ref, rg_ref, cnt_ref, o_scr, carry_scr, wq_ref, wo_ref, stage, sems,
                 *, layer):
    tm, d = x_ref.shape
    hd_dim = d // N_XHEADS
    sub = tm // SUB_TILES

    @pl.when(pl.program_id(0) == 0)
    def _():
        carry_scr[...] = jnp.zeros_like(carry_scr)
        _load_cast(wq_hbm.at[layer], wq_ref, stage, sems)
        _load_cast(wo_hbm.at[layer], wo_ref, stage, sems)

    for st in range(SUB_TILES):
        rs = slice(st * sub, (st + 1) * sub)
        x = x_ref[rs, :]
        q = _dot(x.astype(BF16), wq_ref[...]) * (1.0 / math.sqrt(hd_dim))
        for hd in range(N_XHEADS):
            hs = slice(hd * hd_dim, (hd + 1) * hd_dim)
            s = lax.dot_general(q[:, hs].astype(BF16), k_ref[:, hs], _NT, preferred_element_type=F32)
            p = jnp.exp(s - jnp.max(s, axis=-1, keepdims=True))
            p = p * (1.0 / jnp.sum(p, axis=-1, keepdims=True))
            o_scr[rs, hs] = _dot(p.astype(BF16), v_ref[:, hs]).astype(BF16)
        xa = _dot(o_scr[rs, :], wo_ref[...])
        out = _layer_norm(ALPHA * x + xa, g_ref[...], b_ref[...])
        o_ref[rs, :] = out
        op_ref[rs, :] = _pack_halves(out)
        table, gates = _route_rows(out, wr_ref, br_ref, carry_scr)
        rt_ref[:, rs] = table
        rg_ref[rs, :] = gates
    cnt_ref[...] = carry_scr[...].astype(I32)


def _cross_attn(x, seq, layer, k, v, wq, wo, g, b, router_w, router_b):
    n, d = x.shape
    m = k.shape[2]
    tiles_per_seq = seq // TM
    kvspec = pl.BlockSpec((None, None, m, d), lambda i: (layer, i // tiles_per_seq, 0, 0))
    out, packed, table, gates, cnt = pl.pallas_call(
        functools.partial(_attn_kernel, layer=layer),
        out_shape=(jax.ShapeDtypeStruct((n, d), F32), jax.ShapeDtypeStruct((n, d // 2), I32),
                   jax.ShapeDtypeStruct((ROUTE_ROWS, n), F32), jax.ShapeDtypeStruct((n, TOP_K), F32),
                   jax.ShapeDtypeStruct((1, ROUTE_COLS), I32)),
        grid=(n // TM,),
        in_specs=[
            pl.BlockSpec((TM, d), lambda i: (i, 0)),
            kvspec, kvspec,
            _HBM, _HBM,
            _const_spec((1, d)), _const_spec((1, d)),
            _const_spec((d, 2 * ROUTE_COLS)), _const_spec((1, ROUTE_COLS)),
        ],
        out_specs=(pl.BlockSpec((TM, d), lambda i: (i, 0)), pl.BlockSpec((TM, d // 2), lambda i: (i, 0)),
                   pl.BlockSpec((ROUTE_ROWS, TM), lambda i: (0, i)), pl.BlockSpec((TM, TOP_K), lambda i: (i, 0)),
                   _const_spec((1, ROUTE_COLS))),
        scratch_shapes=[pltpu.VMEM((TM, d), BF16), pltpu.VMEM((1, ROUTE_COLS), F32),
                        pltpu.VMEM((d, d), BF16), pltpu.VMEM((d, d), BF16)] + _stage_scratch(d),
        compiler_params=_params(),
        name="cross_attn",
    )(x, k, v, wq, wo, g.reshape(1, -1), b.reshape(1, -1), router_w, router_b)
    return out, packed, table, gates, cnt[0, N_GROUPS:N_GROUPS + N_EXPERTS]


def _route_rows(x, wcat_ref, bias_ref, carry_scr):
    tm = x.shape[0]

    xh = x.astype(BF16)
    xl = (x - xh.astype(F32)).astype(BF16)
    r1 = _dot(xh, wcat_ref[...])
    r2 = _dot(xl, wcat_ref[:, :ROUTE_COLS])
    logits = r1[:, :ROUTE_COLS] + r1[:, ROUTE_COLS:] + r2 + bias_ref[...]

    lane = lax.broadcasted_iota(I32, (tm, ROUTE_COLS), 1).astype(F32)
    neg = -jnp.inf

    def first_argmax(vals):
        mx = jnp.max(vals, axis=-1, keepdims=True)
        idx = jnp.min(jnp.where(vals == mx, lane, float(ROUTE_COLS)), axis=-1, keepdims=True)
        return mx, idx

    gl = jnp.where(lane < N_GROUPS, logits, neg)
    gmax, g_sel = first_argmax(gl)
    gate_g = 1.0 / jnp.sum(jnp.exp(gl - gmax), axis=-1, keepdims=True)

    lo = N_GROUPS + g_sel * EXPERTS_PER_GROUP
    el = jnp.where((lane >= lo) & (lane < lo + EXPERTS_PER_GROUP), logits, neg)
    m1, i1 = first_argmax(el)
    m2, i2 = first_argmax(jnp.where(lane == i1, neg, el))
    e21 = jnp.exp(m2 - m1)
    w1 = 1.0 / (1.0 + e21)
    w2 = e21 / (1.0 + e21)

    oh1 = lane == i1
    oh2 = lane == i2
    oh = (oh1 | oh2).astype(BF16)
    r = lax.broadcasted_iota(I32, (tm, tm), 0)
    c = lax.broadcasted_iota(I32, (tm, tm), 1)
    before = _dot((r > c).astype(BF16), oh) + carry_scr[...]
    rank1 = jnp.sum(jnp.where(oh1, before, 0.0), axis=-1, keepdims=True)
    rank2 = jnp.sum(jnp.where(oh2, before, 0.0), axis=-1, keepdims=True)
    carry_scr[...] += jnp.sum(oh.astype(F32), axis=0, keepdims=True)

    cols = jnp.where(lane == 0.0, i1 - N_GROUPS,
                     jnp.where(lane == 1.0, i2 - N_GROUPS,
                               jnp.where(lane == 2.0, rank1, jnp.where(lane == 3.0, rank2, 0.0))))
    table = jnp.transpose(cols)[:ROUTE_ROWS, :]
    l2 = lax.broadcasted_iota(I32, (tm, TOP_K), 1)
    gates = jnp.where(l2 == 0, gate_g * w1, gate_g * w2)
    return table, gates


def _router_weights(wr_g, br_g, wr_e, br_e):
    d = wr_g.shape[0]
    w = jnp.concatenate([wr_g, jnp.transpose(wr_e, (1, 0, 2)).reshape(d, N_EXPERTS)], axis=1)
    w = jnp.pad(w, ((0, 0), (0, ROUTE_COLS - w.shape[1])))
    w_hi = w.astype(BF16)
    w_lo = (w - w_hi.astype(F32)).astype(BF16)
    bias = jnp.pad(jnp.concatenate([br_g, br_e.reshape(-1)]), (0, ROUTE_COLS - N_GROUPS - N_EXPERTS))
    return jnp.concatenate([w_hi, w_lo], axis=1), bias.reshape(1, -1)


def _sc_worker_rows(rows):
    per_worker = rows // SC_WORKERS
    n_chunks = per_worker // SC_CHUNK
    assert per_worker * SC_WORKERS == rows and n_chunks * SC_CHUNK == per_worker and n_chunks % 2 == 0
    return per_worker, n_chunks


def _sc_gather_rows(table_hbm, out_hbm, idx_v, rows_v, gsem, wsem, base, n_chunks):
    def fetch(c, slot):
        off = pl.multiple_of(c * SC_CHUNK, SC_CHUNK)
        return pltpu.make_async_copy(table_hbm.at[idx_v.at[pl.ds(off, SC_CHUNK)]], rows_v.at[slot],
                                     gsem.at[slot])

    def put(c, slot):
        off = pl.multiple_of(c * SC_CHUNK, SC_CHUNK)
        return pltpu.make_async_copy(rows_v.at[slot], out_hbm.at[pl.ds(base + off, SC_CHUNK)], wsem.at[slot])

    fetch(0, 0).start()

    @pl.loop(0, n_chunks, step=2)
    def _(c0):
        for slot in range(2):
            c = c0 + slot

            @pl.when(c + 1 < n_chunks)
            def _():
                @pl.when(c >= 1)
                def _():
                    put(c - 1, 1 - slot).wait()
                fetch(c + 1, 1 - slot).start()

            fetch(c, slot).wait()
            put(c, slot).start()

    put(n_chunks - 2, 0).wait()
    put(n_chunks - 1, 1).wait()


def _sc_row_scratch(per_worker, d, dtype):
    return [pltpu.VMEM((per_worker,), I32), pltpu.VMEM((2, SC_CHUNK, d), dtype),
            pltpu.SemaphoreType.DMA((2,)), pltpu.SemaphoreType.DMA((2,))]


def _sc_gather(table, idx):
    b = idx.shape[0]
    d = table.shape[1]
    per_worker, n_chunks = _sc_worker_rows(b)
    mesh = plsc.VectorSubcoreMesh(core_axis_name="c", subcore_axis_name="s")

    @functools.partial(
        pl.kernel, mesh=mesh,
        out_type=jax.ShapeDtypeStruct((b, d), table.dtype),
        scratch_types=_sc_row_scratch(per_worker, d, table.dtype),
        name="sc_gather",
    )
    def gather(table_hbm, idx_hbm, out_hbm, idx_v, rows_v, gsem, wsem):
        base = (lax.axis_index("s") * SC_CORES + lax.axis_index("c")) * per_worker
        pltpu.sync_copy(idx_hbm.at[pl.ds(base, per_worker)], idx_v)
        _sc_gather_rows(table_hbm, out_hbm, idx_v, rows_v, gsem, wsem, base, n_chunks)

    return gather(table, idx)


def _sc_dispatch(table, dest_flat, rows):
    n, d = table.shape
    a = dest_flat.shape[0]
    lanes = SC_LANES
    per_worker, n_chunks = _sc_worker_rows(rows)
    assert a % lanes == 0 and per_worker % lanes == 0
    mesh = plsc.VectorSubcoreMesh(core_axis_name="c", subcore_axis_name="s")

    @functools.partial(
        pl.kernel, mesh=mesh,
        out_type=jax.ShapeDtypeStruct((rows, d), table.dtype),
        scratch_types=[pltpu.VMEM((a,), I32)] + _sc_row_scratch(per_worker, d, table.dtype),
        compiler_params=pltpu.CompilerParams(needs_layout_passes=False),
        name="sc_dispatch",
    )
    def dispatch(table_hbm, dest_hbm, out_hbm, dest_v, idx_v, rows_v, gsem, wsem):
        base = (lax.axis_index("s") * SC_CORES + lax.axis_index("c")) * per_worker
        pltpu.sync_copy(dest_hbm, dest_v)
        lane = lax.iota(I32, lanes)

        @pl.loop(0, per_worker // lanes)
        def _(i):
            idx_v[pl.ds(i * lanes, lanes)] = lax.rem(base + i * lanes + lane, n)

        @pl.loop(0, a // lanes)
        def _(i):
            local = dest_v[pl.ds(i * lanes, lanes)] - base
            mine = (local >= 0) & (local < per_worker)
            plsc.store_scatter(idx_v, [jnp.where(mine, local, 0)], lax.rem(i * lanes + lane, n), mask=mine)

        _sc_gather_rows(table_hbm, out_hbm, idx_v, rows_v, gsem, wsem, base, n_chunks)

    return dispatch(table, dest_flat)


def _expert_kernel(sched_ref, nb_ref, xs_ref, w1_hbm, w3_hbm, w2_hbm, y_ref,
                   w1_buf, w3_buf, w2_buf, w1_scr, w3_scr, w2_scr, sems, *, layer):
    b = pl.program_id(0)
    used = b < nb_ref[0]
    expert, slot, run_start, next_expert = (sched_ref[r, b] for r in range(4))

    def fetch(e, s):
        return [pltpu.make_async_copy(w_hbm.at[layer, e], buf.at[s], sems.at[s, j])
                for j, (w_hbm, buf) in enumerate(((w1_hbm, w1_buf), (w3_hbm, w3_buf), (w2_hbm, w2_buf)))]

    @pl.when(used & (run_start == 1))
    def _():
        @pl.when(b == 0)
        def _():
            for c in fetch(expert, slot):
                c.start()

        for c in fetch(expert, slot):
            c.wait()

        @pl.when(next_expert >= 0)
        def _():
            for c in fetch(next_expert, 1 - slot):
                c.start()

        w1_scr[...] = w1_buf[slot].astype(BF16)
        w3_scr[...] = w3_buf[slot].astype(BF16)
        w2_scr[...] = w2_buf[slot].astype(BF16)

    @pl.when(used)
    def _():
        x_lo, x_hi = _unpack_halves(xs_ref[...])
        xb = jnp.concatenate([x_lo.astype(BF16), x_hi.astype(BF16)], axis=1)
        h1 = _dot(xb, w1_scr[...])
        h3 = _dot(xb, w3_scr[...])
        hid = h1 * (1.0 / (1.0 + jnp.exp(-h1))) * h3
        y_ref[...] = _pack_halves(_dot(hid.astype(BF16), w2_scr[...]))

    @pl.when(jnp.logical_not(used))
    def _():
        y_ref[...] = jnp.zeros_like(y_ref)


def _expert_mlp(xs, schedule, n_used, layer, w1, w3, w2, n_blocks):
    d, de = w1.shape[2], w1.shape[3]

    def row_map(b, sched, nb):
        return (jnp.minimum(b, nb[0] - 1), 0)

    return pl.pallas_call(
        functools.partial(_expert_kernel, layer=layer),
        out_shape=jax.ShapeDtypeStruct((n_blocks * EXPERT_BLOCK, d // 2), I32),
        grid_spec=pltpu.PrefetchScalarGridSpec(
            num_scalar_prefetch=2,
            grid=(n_blocks,),
            in_specs=[pl.BlockSpec((EXPERT_BLOCK, d // 2), row_map), _HBM, _HBM, _HBM],
            out_specs=pl.BlockSpec((EXPERT_BLOCK, d // 2), lambda b, sched, nb: (b, 0)),
            scratch_shapes=[pltpu.VMEM((2, d, de), F32), pltpu.VMEM((2, d, de), F32), pltpu.VMEM((2, de, d), F32),
                            pltpu.VMEM((d, de), BF16), pltpu.VMEM((d, de), BF16), pltpu.VMEM((de, d), BF16),
                            pltpu.SemaphoreType.DMA((2, 3))],
        ),
        compiler_params=_params(),
        name="expert_mlp",
    )(schedule, n_used, xs, w1, w3, w2)


def _combine_kernel(x_ref, y0_ref, y1_ref, gate_ref, g_ref, b_ref, o_ref):
    o_ref[...] = _moe_output(x_ref, y0_ref, y1_ref, gate_ref, g_ref, b_ref)


def _combine(x, yg, gates, g, b):
    n, d = x.shape
    tiles = n // TM
    return pl.pallas_call(
        _combine_kernel,
        out_shape=jax.ShapeDtypeStruct((n, d), F32),
        grid=(tiles,),
        in_specs=[pl.BlockSpec((TM, d), lambda i: (i, 0)),
                  pl.BlockSpec((TM, d // 2), lambda i: (i, 0)),
                  pl.BlockSpec((TM, d // 2), lambda i: (i + tiles, 0)),
                  pl.BlockSpec((TM, TOP_K), lambda i: (i, 0)),
                  _const_spec((1, d)), _const_spec((1, d))],
        out_specs=pl.BlockSpec((TM, d), lambda i: (i, 0)),
        compiler_params=_params(),
        name="combine",
    )(x, yg, yg, gates, g.reshape(1, -1), b.reshape(1, -1))


def _moe_experts(x_packed, table, counts, layer, w1, w3, w2):
    n = x_packed.shape[0]
    n_blocks = (n * TOP_K + N_EXPERTS * (EXPERT_BLOCK - 1) + EXPERT_BLOCK - 1) // EXPERT_BLOCK
    experts = table[:TOP_K].astype(I32)
    ranks = table[TOP_K:2 * TOP_K].astype(I32)

    blocks_e = (counts + EXPERT_BLOCK - 1) // EXPERT_BLOCK
    blocks_end = jnp.cumsum(blocks_e)
    run_start = (blocks_end - blocks_e) * EXPERT_BLOCK
    n_used = blocks_end[-1:].astype(I32)
    block_ids = jnp.arange(n_blocks, dtype=I32)
    block_expert = jnp.minimum(jnp.sum(blocks_end[None, :] <= block_ids[:, None], axis=1),
                               N_EXPERTS - 1).astype(I32)
    run_start_flag = jnp.concatenate([jnp.ones((1,), I32),
                                      (block_expert[1:] != block_expert[:-1]).astype(I32)])
    slot = (jnp.cumsum(run_start_flag) - 1) % 2
    next_block = blocks_end[block_expert]
    next_expert = jnp.where(next_block < n_used[0],
                            block_expert[jnp.minimum(next_block, n_blocks - 1)], -1)
    schedule = jnp.stack([block_expert, slot, run_start_flag, next_expert]).astype(I32)
    expert_ids = jnp.arange(N_EXPERTS, dtype=I32)
    start_of = jnp.sum(jnp.where(experts[:, :, None] == expert_ids, run_start, 0), axis=-1)
    dest = (start_of + ranks).astype(I32).reshape(-1)
    xs = _sc_dispatch(x_packed, dest, n_blocks * EXPERT_BLOCK)
    y = _expert_mlp(xs, schedule, n_used, layer, w1, w3, w2, n_blocks)
    return _sc_gather(y, dest)


def kernel(x, mem, w_in_even, w_pool, pool_scale, ln_v_g, ln_v_b, w_spatial, b_spatial, w_out_even,
           w_in_odd, conv_w, conv_b, w_out_odd, wq_x, wk_x, wv_x, wo_x, ln_g, ln_b, wr_group,
           br_group, wr_expert, br_expert, w1, w3, w2):
    bsz, seq, d = x.shape
    assert seq % TM == 0 and d % LANES == 0
    mlen = mem.shape[1]
    k_all, v_all = _memory_kv(mem.reshape(bsz * mlen, d), wk_x, wv_x)
    k_all = k_all.reshape(DEPTH, bsz, mlen, d)
    v_all = v_all.reshape(DEPTH, bsz, mlen, d)
    h = x.reshape(bsz * seq, d)
    pending = None
    for l in range(DEPTH):
        i = l // 2
        if l % 2 == 0:
            h = _even_mixer(h, pending, seq, i, w_in_even, w_pool[i], pool_scale[i], ln_v_g[i], ln_v_b[i],
                            w_spatial[i], b_spatial[i], w_out_even, ln_g[l, 0], ln_b[l, 0])
        else:
            h = _odd_mixer(h, pending, seq, i, w_in_odd, conv_w[i], conv_b[i], w_out_odd,
                           ln_g[l, 0], ln_b[l, 0])
        router_w, router_b = _router_weights(wr_group[l], br_group[l], wr_expert[l], br_expert[l])
        h, hp, table, gates, counts = _cross_attn(h, seq, l, k_all, v_all, wq_x, wo_x,
                                                  ln_g[l, 1], ln_b[l, 1], router_w, router_b)
        yg = _moe_experts(hp, table, counts, l, w1, w3, w2)
        pending = (yg, gates, ln_g[l, 2], ln_b[l, 2])
    return _combine(h, *pending).reshape(bsz, seq, d)
```

```python
import functools
import math

import jax
import jax.numpy as jnp
from jax import lax
from jax.experimental import pallas as pl
from jax.experimental.pallas import tpu as pltpu
from jax.experimental.pallas import tpu_sc as plsc

F32 = jnp.float32
BF16 = jnp.bfloat16
I32 = jnp.int32

POOL_WINDOWS = (2, 4, 8, 16)
assert all(w & (w - 1) == 0 for w in POOL_WINDOWS)
N_SG_HEADS = 4
CHUNK = 128
CONV_WIDTH = 3
N_XHEADS = 4
N_GROUPS = 4
EXPERTS_PER_GROUP = 8
N_EXPERTS = N_GROUPS * EXPERTS_PER_GROUP
TOP_K = 2
DEPTH = 4
ALPHA = (2.0 * DEPTH) ** 0.25
LN_EPS = 1e-5

LANES = 128
SC_CORES = 2
SC_WORKERS = 32
SC_LANES = 16
SC_CHUNK = 64
TM = 1024
SUB_TILES = 2
ROUTE_ROWS = 8
POOL_HALO = 16
CONV_HALO = 8
EXPERT_BLOCK = 512
ROUTE_COLS = 128
STAGE_COLS = 512
VMEM_LIMIT = 56 * 1024 * 1024

_NT = (((1,), (1,)), ((), ()))


def _dot(a, b):
    return jnp.dot(a, b, preferred_element_type=F32)


def _layer_norm(y, g, b):
    mu = jnp.mean(y, axis=-1, keepdims=True)
    yc = y - mu
    var = jnp.mean(yc * yc, axis=-1, keepdims=True)
    return yc * lax.rsqrt(var + LN_EPS) * g + b


def _gelu_tanh(x):
    c = math.sqrt(2.0 / math.pi)
    return 0.5 * x * (1.0 + jnp.tanh(c * (x + 0.044715 * (x * x * x))))


def _pack_halves(v):
    c = v.shape[1] // 2
    lo = pltpu.bitcast(v[:, :c].astype(BF16).astype(F32), jnp.uint32)
    hi = pltpu.bitcast(v[:, c:].astype(BF16).astype(F32), jnp.uint32)
    return pltpu.bitcast((hi & jnp.uint32(0xFFFF0000)) | (lo >> 16), I32)


def _unpack_halves(w):
    u = pltpu.bitcast(w, jnp.uint32)
    return pltpu.bitcast(u << 16, F32), pltpu.bitcast(u & jnp.uint32(0xFFFF0000), F32)


def _load_cast(w_hbm, w_scr, stage, sems):
    chunks = w_scr.shape[1] // STAGE_COLS

    def chunk_copy(c):
        return pltpu.make_async_copy(w_hbm.at[:, pl.ds(c * STAGE_COLS, STAGE_COLS)], stage.at[c % 2],
                                     sems.at[c % 2])

    chunk_copy(0).start()
    for c in range(chunks):
        if c + 1 < chunks:
            chunk_copy(c + 1).start()
        chunk_copy(c).wait()
        w_scr[:, c * STAGE_COLS:(c + 1) * STAGE_COLS] = stage[c % 2].astype(BF16)


def _stage_scratch(rows):
    return [pltpu.VMEM((2, rows, STAGE_COLS), F32), pltpu.SemaphoreType.DMA((2,))]


_HBM = pl.BlockSpec(memory_space=pl.ANY)


def _const_spec(shape):
    nd = len(shape)
    return pl.BlockSpec(shape, lambda i: (0,) * nd)


def _params():
    return pltpu.CompilerParams(dimension_semantics=("arbitrary",), vmem_limit_bytes=VMEM_LIMIT)


def _moe_output(x_ref, y0_ref, y1_ref, gate_ref, g_ref, b_ref):
    gates = gate_ref[...]
    g0, g1 = gates[:, 0:1], gates[:, 1:2]
    y0_lo, y0_hi = _unpack_halves(y0_ref[...])
    y1_lo, y1_hi = _unpack_halves(y1_ref[...])
    ff = jnp.concatenate([g0 * y0_lo + g1 * y1_lo, g0 * y0_hi + g1 * y1_hi], axis=1)
    return _layer_norm(ALPHA * x_ref[...] + ff, g_ref[...], b_ref[...])


def _mixer_input(src, pending):
    if not pending:
        x_ref, xh_ref = src
        return x_ref[...], xh_ref[...]
    x_ref, xh_ref, y0_ref, y0h_ref, y1_ref, y1h_ref, gate_ref, gateh_ref, g_ref, b_ref = src
    return (_moe_output(x_ref, y0_ref, y1_ref, gate_ref, g_ref, b_ref),
            _moe_output(xh_ref, y0h_ref, y1h_ref, gateh_ref, g_ref, b_ref))


def _mixer_sources(x, pending, halo):
    n, d = x.shape
    tiles = n // TM
    halo_blocks = TM // halo

    def halo_index(i):
        return jnp.maximum(i * halo_blocks - 1, 0)

    specs = [pl.BlockSpec((TM, d), lambda i: (i, 0)), pl.BlockSpec((halo, d), lambda i: (halo_index(i), 0))]
    args = [x, x]
    if pending is not None:
        yg, gates, g, b = pending
        specs += [pl.BlockSpec((TM, d // 2), lambda i: (i, 0)),
                  pl.BlockSpec((halo, d // 2), lambda i: (halo_index(i), 0)),
                  pl.BlockSpec((TM, d // 2), lambda i: (i + tiles, 0)),
                  pl.BlockSpec((halo, d // 2), lambda i: (halo_index(i) + tiles * halo_blocks, 0)),
                  pl.BlockSpec((TM, TOP_K), lambda i: (i, 0)),
                  pl.BlockSpec((halo, TOP_K), lambda i: (halo_index(i), 0)),
                  _const_spec((1, d)), _const_spec((1, d))]
        args += [yg, yg, yg, yg, gates, gates, g.reshape(1, -1), b.reshape(1, -1)]
    return specs, args


def _even_kernel(*refs, tiles_per_seq, layer, pending):
    n_src = 10 if pending else 2
    (win_hbm, wpool_ref, pscale_ref, lvg_ref, lvb_ref, ws_ref, bst_ref, wout_hbm, g_ref, b_ref, o_ref,
     a_scr, cat_scr, win_ref, wout_ref, stage, sems) = refs[n_src:]
    tm = o_ref.shape[0]
    d_pool = a_scr.shape[1]
    d_sg = lvg_ref.shape[1]
    pgd = d_pool // len(POOL_WINDOWS)
    hd_dim = d_sg // N_SG_HEADS
    seq_tile = pl.program_id(0) % tiles_per_seq

    @pl.when(pl.program_id(0) == 0)
    def _():
        _load_cast(win_hbm.at[layer], win_ref, stage, sems)
        _load_cast(wout_hbm.at[layer], wout_ref, stage, sems)

    x, xh = _mixer_input(refs[:n_src], pending)
    h = _dot(x.astype(BF16), win_ref[...])

    ah = _dot(xh.astype(BF16), win_ref[:, :d_pool])
    a_scr[0:POOL_HALO, :] = jnp.where(seq_tile == 0, 0.0, ah)
    a_scr[POOL_HALO:POOL_HALO + tm, :] = h[:, :d_pool]
    pos = seq_tile * tm + lax.broadcasted_iota(I32, (tm, 1), 0)
    for g, w in enumerate(POOL_WINDOWS):
        cs = slice(g * pgd, (g + 1) * pgd)
        tok = a_scr[POOL_HALO:POOL_HALO + tm, cs]
        acc = a_scr[:, cs]
        span = 1
        while span < w:
            acc = acc[span:, :] + acc[:-span, :]
            span *= 2
        acc = acc[acc.shape[0] - tm:, :]
        cnt = jnp.minimum(pos + 1, w).astype(F32)
        d = acc * (1.0 / cnt) - tok
        yg = _dot(d.astype(BF16), wpool_ref[g])
        cat_scr[:, cs] = (yg * pscale_ref[:, cs]).astype(BF16)

    z = _gelu_tanh(h[:, d_pool:])
    u = z[:, :d_sg]
    v = _layer_norm(z[:, d_sg:], lvg_ref[...], lvb_ref[...]).astype(BF16)
    row = lax.broadcasted_iota(I32, (CHUNK, CHUNK), 0)
    col = lax.broadcasted_iota(I32, (CHUNK, CHUNK), 1)
    for hd in range(N_SG_HEADS):
        hs = slice(hd * hd_dim, (hd + 1) * hd_dim)
        wsm = jnp.where(row >= col, ws_ref[hd], 0.0).astype(BF16)
        bcol = bst_ref[:, hd:hd + 1]
        for ck in range(tm // CHUNK):
            rs = slice(ck * CHUNK, (ck + 1) * CHUNK)
            sv = _dot(wsm, v[rs, hs]) + bcol
            cat_scr[rs, d_pool + hd * hd_dim:d_pool + (hd + 1) * hd_dim] = (u[rs, hs] * sv).astype(BF16)

    mix = _dot(cat_scr[...], wout_ref[...])
    o_ref[...] = _layer_norm(ALPHA * x + mix, g_ref[...], b_ref[...])


def _even_mixer(x, pending, seq, layer, w_in, w_pool, pool_scale, ln_v_g, ln_v_b, w_spatial, b_spatial,
                w_out, g, b):
    n, d = x.shape
    d_in = w_in.shape[2]
    d_pool = pool_scale.shape[0]
    d_sg = ln_v_g.shape[0]
    kern = functools.partial(_even_kernel, tiles_per_seq=seq // TM, layer=layer, pending=pending is not None)
    src_specs, src_args = _mixer_sources(x, pending, POOL_HALO)
    return pl.pallas_call(
        kern,
        out_shape=jax.ShapeDtypeStruct((n, d), F32),
        grid=(n // TM,),
        in_specs=src_specs + [
            _HBM,
            _const_spec(w_pool.shape),
            _const_spec((1, d_pool)),
            _const_spec((1, d_sg)),
            _const_spec((1, d_sg)),
            _const_spec(w_spatial.shape),
            _const_spec((CHUNK, N_SG_HEADS)),
            _HBM,
            _const_spec((1, d)),
            _const_spec((1, d)),
        ],
        out_specs=pl.BlockSpec((TM, d), lambda i: (i, 0)),
        scratch_shapes=[pltpu.VMEM((POOL_HALO + TM, d_pool), F32), pltpu.VMEM((TM, d_pool + d_sg), BF16),
                        pltpu.VMEM((d, d_in), BF16), pltpu.VMEM((d_pool + d_sg, d), BF16)] + _stage_scratch(d),
        compiler_params=_params(),
        name="even_mixer",
    )(*src_args, w_in, w_pool.astype(BF16), pool_scale.reshape(1, -1), ln_v_g.reshape(1, -1),
      ln_v_b.reshape(1, -1), w_spatial, b_spatial.T, w_out, g.reshape(1, -1), b.reshape(1, -1))


def _odd_kernel(*refs, tiles_per_seq, layer, pending):
    n_src = 10 if pending else 2
    (win_hbm, cwt_ref, cb_ref, wout_hbm, g_ref, b_ref, o_ref, zc_scr,
     win_ref, wout_ref, stage, sems) = refs[n_src:]
    tm, d = o_ref.shape
    seq_tile = pl.program_id(0) % tiles_per_seq

    @pl.when(pl.program_id(0) == 0)
    def _():
        _load_cast(win_hbm.at[layer], win_ref, stage, sems)
        _load_cast(wout_hbm.at[layer], wout_ref, stage, sems)

    x, xh = _mixer_input(refs[:n_src], pending)
    xb = x.astype(BF16)
    hc = _dot(xb, win_ref[:, d:2 * d])
    hz = _dot(xb, win_ref[:, 2 * d:])
    hh = _dot(xh.astype(BF16), win_ref[:, d:])
    zc_scr[0:CONV_HALO, :] = jnp.where(seq_tile == 0, 0.0, hh[:, :d] * hh[:, d:])
    zc_scr[CONV_HALO:CONV_HALO + tm, :] = hc * hz
    gate = _dot(xb, win_ref[:, :d])
    conv = cb_ref[...]
    for j in range(CONV_WIDTH):
        off = CONV_HALO - (CONV_WIDTH - 1) + j
        conv = conv + zc_scr[off:off + tm, :] * cwt_ref[j:j + 1, :]
    y = _dot((gate * conv).astype(BF16), wout_ref[...])
    o_ref[...] = _layer_norm(ALPHA * x + y, g_ref[...], b_ref[...])


def _odd_mixer(x, pending, seq, layer, w_in, conv_w, conv_b, w_out, g, b):
    n, d = x.shape
    kern = functools.partial(_odd_kernel, tiles_per_seq=seq // TM, layer=layer, pending=pending is not None)
    src_specs, src_args = _mixer_sources(x, pending, CONV_HALO)
    return pl.pallas_call(
        kern,
        out_shape=jax.ShapeDtypeStruct((n, d), F32),
        grid=(n // TM,),
        in_specs=src_specs + [
            _HBM,
            _const_spec((CONV_WIDTH, d)),
            _const_spec((1, d)),
            _HBM,
            _const_spec((1, d)),
            _const_spec((1, d)),
        ],
        out_specs=pl.BlockSpec((TM, d), lambda i: (i, 0)),
        scratch_shapes=[pltpu.VMEM((CONV_HALO + TM, d), F32),
                        pltpu.VMEM(w_in.shape[1:], BF16), pltpu.VMEM(w_out.shape[1:], BF16)] + _stage_scratch(d),
        compiler_params=_params(),
        name="odd_mixer",
    )(*src_args, w_in, conv_w.T, conv_b.reshape(1, -1), w_out, g.reshape(1, -1), b.reshape(1, -1))


def _kv_kernel(mem_ref, wk_ref, wv_ref, k_ref, v_ref):
    m = mem_ref[...].astype(BF16)
    k_ref[...] = _dot(m, wk_ref[...].astype(BF16)).astype(BF16)
    v_ref[...] = _dot(m, wv_ref[...].astype(BF16)).astype(BF16)


def _memory_kv(mem2d, wk, wv):
    nl, d, _ = wk.shape
    rows = mem2d.shape[0]
    out = jax.ShapeDtypeStruct((nl, rows, d), BF16)
    wspec = pl.BlockSpec((None, d, d), lambda l: (l, 0, 0))
    ospec = pl.BlockSpec((None, rows, d), lambda l: (l, 0, 0))
    return pl.pallas_call(
        _kv_kernel,
        out_shape=(out, out),
        grid=(nl,),
        in_specs=[_const_spec((rows, d)), wspec, wspec],
        out_specs=(ospec, ospec),
        compiler_params=_params(),
        name="memory_kv",
    )(mem2d, wk, wv)


def _attn_kernel(x_ref, k_ref, v_ref, wq_hbm, wo_hbm, g_ref, b_ref, wr_ref, br_ref,
                 o_ref, op_ref, rt_ref, rg_ref, cnt_ref, o_scr, carry_scr, wq_ref, wo_ref, stage, sems,
                 *, layer):
    tm, d = x_ref.shape
    hd_dim = d // N_XHEADS
    sub = tm // SUB_TILES

    @pl.when(pl.program_id(0) == 0)
    def _():
        carry_scr[...] = jnp.zeros_like(carry_scr)
        _load_cast(wq_hbm.at[layer], wq_ref, stage, sems)
        _load_cast(wo_hbm.at[layer], wo_ref, stage, sems)

    for st in range(SUB_TILES):
        rs = slice(st * sub, (st + 1) * sub)
        x = x_ref[rs, :]
        q = _dot(x.astype(BF16), wq_ref[...]) * (1.0 / math.sqrt(hd_dim))
        for hd in range(N_XHEADS):
            hs = slice(hd * hd_dim, (hd + 1) * hd_dim)
            s = lax.dot_general(q[:, hs].astype(BF16), k_ref[:, hs], _NT, preferred_element_type=F32)
            p = jnp.exp(s - jnp.max(s, axis=-1, keepdims=True))
            p = p * (1.0 / jnp.sum(p, axis=-1, keepdims=True))
            o_scr[rs, hs] = _dot(p.astype(BF16), v_ref[:, hs]).astype(BF16)
        xa = _dot(o_scr[rs, :], wo_ref[...])
        out = _layer_norm(ALPHA * x + xa, g_ref[...], b_ref[...])
        o_ref[rs, :] = out
        op_ref[rs, :] = _pack_halves(out)
        table, gates = _route_rows(out, wr_ref, br_ref, carry_scr)
        rt_ref[:, rs] = table
        rg_ref[rs, :] = gates
    cnt_ref[...] = carry_scr[...].astype(I32)


def _cross_attn(x, seq, layer, k, v, wq, wo, g, b, router_w, router_b):
    n, d = x.shape
    m = k.shape[2]
    tiles_per_seq = seq // TM
    kvspec = pl.BlockSpec((None, None, m, d), lambda i: (layer, i // tiles_per_seq, 0, 0))
    out, packed, table, gates, cnt = pl.pallas_call(
        functools.partial(_attn_kernel, layer=layer),
        out_shape=(jax.ShapeDtypeStruct((n, d), F32), jax.ShapeDtypeStruct((n, d // 2), I32),
                   jax.ShapeDtypeStruct((ROUTE_ROWS, n), F32), jax.ShapeDtypeStruct((n, TOP_K), F32),
                   jax.ShapeDtypeStruct((1, ROUTE_COLS), I32)),
        grid=(n // TM,),
        in_specs=[
            pl.BlockSpec((TM, d), lambda i: (i, 0)),
            kvspec, kvspec,
            _HBM, _HBM,
            _const_spec((1, d)), _const_spec((1, d)),
            _const_spec((d, 2 * ROUTE_COLS)), _const_spec((1, ROUTE_COLS)),
        ],
        out_specs=(pl.BlockSpec((TM, d), lambda i: (i, 0)), pl.BlockSpec((TM, d // 2), lambda i: (i, 0)),
                   pl.BlockSpec((ROUTE_ROWS, TM), lambda i: (0, i)), pl.BlockSpec((TM, TOP_K), lambda i: (i, 0)),
                   _const_spec((1, ROUTE_COLS))),
        scratch_shapes=[pltpu.VMEM((TM, d), BF16), pltpu.VMEM((1, ROUTE_COLS), F32),
                        pltpu.VMEM((d, d), BF16), pltpu.VMEM((d, d), BF16)] + _stage_scratch(d),
        compiler_params=_params(),
        name="cross_attn",
    )(x, k, v, wq, wo, g.reshape(1, -1), b.reshape(1, -1), router_w, router_b)
    return out, packed, table, gates, cnt[0, N_GROUPS:N_GROUPS + N_EXPERTS]


def _route_rows(x, wcat_ref, bias_ref, carry_scr):
    tm = x.shape[0]

    xh = x.astype(BF16)
    xl = (x - xh.astype(F32)).astype(BF16)
    r1 = _dot(xh, wcat_ref[...])
    r2 = _dot(xl, wcat_ref[:, :ROUTE_COLS])
    logits = r1[:, :ROUTE_COLS] + r1[:, ROUTE_COLS:] + r2 + bias_ref[...]

    lane = lax.broadcasted_iota(I32, (tm, ROUTE_COLS), 1).astype(F32)
    neg = -jnp.inf

    def first_argmax(vals):
        mx = jnp.max(vals, axis=-1, keepdims=True)
        idx = jnp.min(jnp.where(vals == mx, lane, float(ROUTE_COLS)), axis=-1, keepdims=True)
        return mx, idx

    gl = jnp.where(lane < N_GROUPS, logits, neg)
    gmax, g_sel = first_argmax(gl)
    gate_g = 1.0 / jnp.sum(jnp.exp(gl - gmax), axis=-1, keepdims=True)

    lo = N_GROUPS + g_sel * EXPERTS_PER_GROUP
    el = jnp.where((lane >= lo) & (lane < lo + EXPERTS_PER_GROUP), logits, neg)
    m1, i1 = first_argmax(el)
    m2, i2 = first_argmax(jnp.where(lane == i1, neg, el))
    e21 = jnp.exp(m2 - m1)
    w1 = 1.0 / (1.0 + e21)
    w2 = e21 / (1.0 + e21)

    oh1 = lane == i1
    oh2 = lane == i2
    oh = (oh1 | oh2).astype(BF16)
    r = lax.broadcasted_iota(I32, (tm, tm), 0)
    c = lax.broadcasted_iota(I32, (tm, tm), 1)
    before = _dot((r > c).astype(BF16), oh) + carry_scr[...]
    rank1 = jnp.sum(jnp.where(oh1, before, 0.0), axis=-1, keepdims=True)
    rank2 = jnp.sum(jnp.where(oh2, before, 0.0), axis=-1, keepdims=True)
    carry_scr[...] += jnp.sum(oh.astype(F32), axis=0, keepdims=True)

    cols = jnp.where(lane == 0.0, i1 - N_GROUPS,
                     jnp.where(lane == 1.0, i2 - N_GROUPS,
                               jnp.where(lane == 2.0, rank1, jnp.where(lane == 3.0, rank2, 0.0))))
    table = jnp.transpose(cols)[:ROUTE_ROWS, :]
    l2 = lax.broadcasted_iota(I32, (tm, TOP_K), 1)
    gates = jnp.where(l2 == 0, gate_g * w1, gate_g * w2)
    return table, gates


def _router_weights(wr_g, br_g, wr_e, br_e):
    d = wr_g.shape[0]
    w = jnp.concatenate([wr_g, jnp.transpose(wr_e, (1, 0, 2)).reshape(d, N_EXPERTS)], axis=1)
    w = jnp.pad(w, ((0, 0), (0, ROUTE_COLS - w.shape[1])))
    w_hi = w.astype(BF16)
    w_lo = (w - w_hi.astype(F32)).astype(BF16)
    bias = jnp.pad(jnp.concatenate([br_g, br_e.reshape(-1)]), (0, ROUTE_COLS - N_GROUPS - N_EXPERTS))
    return jnp.concatenate([w_hi, w_lo], axis=1), bias.reshape(1, -1)


def _sc_worker_rows(rows):
    per_worker = rows // SC_WORKERS
    n_chunks = per_worker // SC_CHUNK
    assert per_worker * SC_WORKERS == rows and n_chunks * SC_CHUNK == per_worker and n_chunks % 2 == 0
    return per_worker, n_chunks


def _sc_gather_rows(table_hbm, out_hbm, idx_v, rows_v, gsem, wsem, base, n_chunks):
    def fetch(c, slot):
        off = pl.multiple_of(c * SC_CHUNK, SC_CHUNK)
        return pltpu.make_async_copy(table_hbm.at[idx_v.at[pl.ds(off, SC_CHUNK)]], rows_v.at[slot],
                                     gsem.at[slot])

    def put(c, slot):
        off = pl.multiple_of(c * SC_CHUNK, SC_CHUNK)
        return pltpu.make_async_copy(rows_v.at[slot], out_hbm.at[pl.ds(base + off, SC_CHUNK)], wsem.at[slot])

    fetch(0, 0).start()

    @pl.loop(0, n_chunks, step=2)
    def _(c0):
        for slot in range(2):
            c = c0 + slot

            @pl.when(c + 1 < n_chunks)
            def _():
                @pl.when(c >= 1)
                def _():
                    put(c - 1, 1 - slot).wait()
                fetch(c + 1, 1 - slot).start()

            fetch(c, slot).wait()
            put(c, slot).start()

    put(n_chunks - 2, 0).wait()
    put(n_chunks - 1, 1).wait()


def _sc_row_scratch(per_worker, d, dtype):
    return [pltpu.VMEM((per_worker,), I32), pltpu.VMEM((2, SC_CHUNK, d), dtype),
            pltpu.SemaphoreType.DMA((2,)), pltpu.SemaphoreType.DMA((2,))]


def _sc_gather(table, idx):
    b = idx.shape[0]
    d = table.shape[1]
    per_worker, n_chunks = _sc_worker_rows(b)
    mesh = plsc.VectorSubcoreMesh(core_axis_name="c", subcore_axis_name="s")

    @functools.partial(
        pl.kernel, mesh=mesh,
        out_type=jax.ShapeDtypeStruct((b, d), table.dtype),
        scratch_types=_sc_row_scratch(per_worker, d, table.dtype),
        name="sc_gather",
    )
    def gather(table_hbm, idx_hbm, out_hbm, idx_v, rows_v, gsem, wsem):
        base = (lax.axis_index("s") * SC_CORES + lax.axis_index("c")) * per_worker
        pltpu.sync_copy(idx_hbm.at[pl.ds(base, per_worker)], idx_v)
        _sc_gather_rows(table_hbm, out_hbm, idx_v, rows_v, gsem, wsem, base, n_chunks)

    return gather(table, idx)


def _sc_dispatch(table, dest_flat, rows):
    n, d = table.shape
    a = dest_flat.shape[0]
    lanes = SC_LANES
    per_worker, n_chunks = _sc_worker_rows(rows)
    assert a % lanes == 0 and per_worker % lanes == 0
    mesh = plsc.VectorSubcoreMesh(core_axis_name="c", subcore_axis_name="s")

    @functools.partial(
        pl.kernel, mesh=mesh,
        out_type=jax.ShapeDtypeStruct((rows, d), table.dtype),
        scratch_types=[pltpu.VMEM((a,), I32)] + _sc_row_scratch(per_worker, d, table.dtype),
        compiler_params=pltpu.CompilerParams(needs_layout_passes=False),
        name="sc_dispatch",
    )
    def dispatch(table_hbm, dest_hbm, out_hbm, dest_v, idx_v, rows_v, gsem, wsem):
        base = (lax.axis_index("s") * SC_CORES + lax.axis_index("c")) * per_worker
        pltpu.sync_copy(dest_hbm, dest_v)
        lane = lax.iota(I32, lanes)

        @pl.loop(0, per_worker // lanes)
        def _(i):
            idx_v[pl.ds(i * lanes, lanes)] = lax.rem(base + i * lanes + lane, n)

        @plsc.parallel_loop(0, a // lanes, unroll=4)
        def _(i):
            local = dest_v[pl.ds(i * lanes, lanes)] - base
            mine = (local >= 0) & (local < per_worker)
            plsc.store_scatter(idx_v, [jnp.where(mine, local, 0)], lax.rem(i * lanes + lane, n), mask=mine)

        _sc_gather_rows(table_hbm, out_hbm, idx_v, rows_v, gsem, wsem, base, n_chunks)

    return dispatch(table, dest_flat)


def _expert_kernel(sched_ref, nb_ref, xs_ref, w1_hbm, w3_hbm, w2_hbm, y_ref,
                   w1_buf, w3_buf, w2_buf, w1_scr, w3_scr, w2_scr, sems, *, layer):
    b = pl.program_id(0)
    used = b < nb_ref[0]
    expert, slot, run_start, next_expert = (sched_ref[r, b] for r in range(4))

    def fetch(e, s):
        return [pltpu.make_async_copy(w_hbm.at[layer, e], buf.at[s], sems.at[s, j])
                for j, (w_hbm, buf) in enumerate(((w1_hbm, w1_buf), (w3_hbm, w3_buf), (w2_hbm, w2_buf)))]

    @pl.when(used & (run_start == 1))
    def _():
        @pl.when(b == 0)
        def _():
            for c in fetch(expert, slot):
                c.start()

        for c in fetch(expert, slot):
            c.wait()

        @pl.when(next_expert >= 0)
        def _():
            for c in fetch(next_expert, 1 - slot):
                c.start()

        w1_scr[...] = w1_buf[slot].astype(BF16)
        w3_scr[...] = w3_buf[slot].astype(BF16)
        w2_scr[...] = w2_buf[slot].astype(BF16)

    @pl.when(used)
    def _():
        x_lo, x_hi = _unpack_halves(xs_ref[...])
        xb = jnp.concatenate([x_lo.astype(BF16), x_hi.astype(BF16)], axis=1)
        h1 = _dot(xb, w1_scr[...])
        h3 = _dot(xb, w3_scr[...])
        hid = h1 * (1.0 / (1.0 + jnp.exp(-h1))) * h3
        y_ref[...] = _pack_halves(_dot(hid.astype(BF16), w2_scr[...]))

    @pl.when(jnp.logical_not(used))
    def _():
        y_ref[...] = jnp.zeros_like(y_ref)


def _expert_mlp(xs, schedule, n_used, layer, w1, w3, w2, n_blocks):
    d, de = w1.shape[2], w1.shape[3]

    def row_map(b, sched, nb):
        return (jnp.minimum(b, nb[0] - 1), 0)

    return pl.pallas_call(
        functools.partial(_expert_kernel, layer=layer),
        out_shape=jax.ShapeDtypeStruct((n_blocks * EXPERT_BLOCK, d // 2), I32),
        grid_spec=pltpu.PrefetchScalarGridSpec(
            num_scalar_prefetch=2,
            grid=(n_blocks,),
            in_specs=[pl.BlockSpec((EXPERT_BLOCK, d // 2), row_map), _HBM, _HBM, _HBM],
            out_specs=pl.BlockSpec((EXPERT_BLOCK, d // 2), lambda b, sched, nb: (b, 0)),
            scratch_shapes=[pltpu.VMEM((2, d, de), F32), pltpu.VMEM((2, d, de), F32), pltpu.VMEM((2, de, d), F32),
                            pltpu.VMEM((d, de), BF16), pltpu.VMEM((d, de), BF16), pltpu.VMEM((de, d), BF16),
                            pltpu.SemaphoreType.DMA((2, 3))],
        ),
        compiler_params=_params(),
        name="expert_mlp",
    )(schedule, n_used, xs, w1, w3, w2)


def _combine_kernel(x_ref, y0_ref, y1_ref, gate_ref, g_ref, b_ref, o_ref):
    o_ref[...] = _moe_output(x_ref, y0_ref, y1_ref, gate_ref, g_ref, b_ref)


def _combine(x, yg, gates, g, b):
    n, d = x.shape
    tiles = n // TM
    return pl.pallas_call(
        _combine_kernel,
        out_shape=jax.ShapeDtypeStruct((n, d), F32),
        grid=(tiles,),
        in_specs=[pl.BlockSpec((TM, d), lambda i: (i, 0)),
                  pl.BlockSpec((TM, d // 2), lambda i: (i, 0)),
                  pl.BlockSpec((TM, d // 2), lambda i: (i + tiles, 0)),
                  pl.BlockSpec((TM, TOP_K), lambda i: (i, 0)),
                  _const_spec((1, d)), _const_spec((1, d))],
        out_specs=pl.BlockSpec((TM, d), lambda i: (i, 0)),
        compiler_params=_params(),
        name="combine",
    )(x, yg, yg, gates, g.reshape(1, -1), b.reshape(1, -1))


def _moe_experts(x_packed, table, counts, layer, w1, w3, w2):
    n = x_packed.shape[0]
    n_blocks = (n * TOP_K + N_EXPERTS * (EXPERT_BLOCK - 1) + EXPERT_BLOCK - 1) // EXPERT_BLOCK
    experts = table[:TOP_K].astype(I32)
    ranks = table[TOP_K:2 * TOP_K].astype(I32)

    blocks_e = (counts + EXPERT_BLOCK - 1) // EXPERT_BLOCK
    blocks_end = jnp.cumsum(blocks_e)
    run_start = (blocks_end - blocks_e) * EXPERT_BLOCK
    n_used = blocks_end[-1:].astype(I32)
    block_ids = jnp.arange(n_blocks, dtype=I32)
    block_expert = jnp.minimum(jnp.sum(blocks_end[None, :] <= block_ids[:, None], axis=1),
                               N_EXPERTS - 1).astype(I32)
    run_start_flag = jnp.concatenate([jnp.ones((1,), I32),
                                      (block_expert[1:] != block_expert[:-1]).astype(I32)])
    slot = (jnp.cumsum(run_start_flag) - 1) % 2
    next_block = blocks_end[block_expert]
    next_expert = jnp.where(next_block < n_used[0],
                            block_expert[jnp.minimum(next_block, n_blocks - 1)], -1)
    schedule = jnp.stack([block_expert, slot, run_start_flag, next_expert]).astype(I32)
    expert_ids = jnp.arange(N_EXPERTS, dtype=I32)
    start_of = jnp.sum(jnp.where(experts[:, :, None] == expert_ids, run_start, 0), axis=-1)
    dest = (start_of + ranks).astype(I32).reshape(-1)
    xs = _sc_dispatch(x_packed, dest, n_blocks * EXPERT_BLOCK)
    y = _expert_mlp(xs, schedule, n_used, layer, w1, w3, w2, n_blocks)
    return _sc_gather(y, dest)


def kernel(x, mem, w_in_even, w_pool, pool_scale, ln_v_g, ln_v_b, w_spatial, b_spatial, w_out_even,
           w_in_odd, conv_w, conv_b, w_out_odd, wq_x, wk_x, wv_x, wo_x, ln_g, ln_b, wr_group,
           br_group, wr_expert, br_expert, w1, w3, w2):
    bsz, seq, d = x.shape
    assert seq % TM == 0 and d % LANES == 0
    mlen = mem.shape[1]
    k_all, v_all = _memory_kv(mem.reshape(bsz * mlen, d), wk_x, wv_x)
    k_all = k_all.reshape(DEPTH, bsz, mlen, d)
    v_all = v_all.reshape(DEPTH, bsz, mlen, d)
    h = x.reshape(bsz * seq, d)
    pending = None
    for l in range(DEPTH):
        i = l // 2
        if l % 2 == 0:
            h = _even_mixer(h, pending, seq, i, w_in_even, w_pool[i], pool_scale[i], ln_v_g[i], ln_v_b[i],
                            w_spatial[i], b_spatial[i], w_out_even, ln_g[l, 0], ln_b[l, 0])
        else:
            h = _odd_mixer(h, pending, seq, i, w_in_odd, conv_w[i], conv_b[i], w_out_odd,
                           ln_g[l, 0], ln_b[l, 0])
        router_w, router_b = _router_weights(wr_group[l], br_group[l], wr_expert[l], br_expert[l])
        h, hp, table, gates, counts = _cross_attn(h, seq, l, k_all, v_all, wq_x, wo_x,
                                                  ln_g[l, 1], ln_b[l, 1], router_w, router_b)
        yg = _moe_experts(hp, table, counts, l, w1, w3, w2)
        pending = (yg, gates, ln_g[l, 2], ln_b[l, 2])
    return _combine(h, *pending).reshape(bsz, seq, d)
```

```python
import functools
import math

import jax
import jax.numpy as jnp
from jax import lax
from jax.experimental import pallas as pl
from jax.experimental.pallas import tpu as pltpu
from jax.experimental.pallas import tpu_sc as plsc

F32 = jnp.float32
BF16 = jnp.bfloat16
I32 = jnp.int32

POOL_WINDOWS = (2, 4, 8, 16)
assert all(w & (w - 1) == 0 for w in POOL_WINDOWS)
N_SG_HEADS = 4
CHUNK = 128
CONV_WIDTH = 3
N_XHEADS = 4
N_GROUPS = 4
EXPERTS_PER_GROUP = 8
N_EXPERTS = N_GROUPS * EXPERTS_PER_GROUP
TOP_K = 2
DEPTH = 4
ALPHA = (2.0 * DEPTH) ** 0.25
LN_EPS = 1e-5

LANES = 128
SC_CORES = 2
SC_WORKERS = 32
SC_LANES = 16
SC_CHUNK = 64
TM = 1024
SUB_TILES = 2
ROUTE_ROWS = 8
POOL_HALO = 16
CONV_HALO = 8
EXPERT_BLOCK = 512
ROUTE_COLS = 128
STAGE_COLS = 512
VMEM_LIMIT = 56 * 1024 * 1024

_NT = (((1,), (1,)), ((), ()))


def _dot(a, b):
    return jnp.dot(a, b, preferred_element_type=F32)


def _layer_norm(y, g, b):
    mu = jnp.mean(y, axis=-1, keepdims=True)
    yc = y - mu
    var = jnp.mean(yc * yc, axis=-1, keepdims=True)
    return yc * lax.rsqrt(var + LN_EPS) * g + b


def _gelu_tanh(x):
    c = math.sqrt(2.0 / math.pi)
    return 0.5 * x * (1.0 + jnp.tanh(c * (x + 0.044715 * (x * x * x))))


def _pack_halves(v):
    c = v.shape[1] // 2
    lo = pltpu.bitcast(v[:, :c].astype(BF16).astype(F32), jnp.uint32)
    hi = pltpu.bitcast(v[:, c:].astype(BF16).astype(F32), jnp.uint32)
    return pltpu.bitcast((hi & jnp.uint32(0xFFFF0000)) | (lo >> 16), I32)


def _unpack_halves(w):
    u = pltpu.bitcast(w, jnp.uint32)
    return pltpu.bitcast(u << 16, F32), pltpu.bitcast(u & jnp.uint32(0xFFFF0000), F32)


def _load_cast(w_hbm, w_scr, stage, sems):
    chunks = w_scr.shape[1] // STAGE_COLS

    def chunk_copy(c):
        return pltpu.make_async_copy(w_hbm.at[:, pl.ds(c * STAGE_COLS, STAGE_COLS)], stage.at[c % 2],
                                     sems.at[c % 2])

    chunk_copy(0).start()
    for c in range(chunks):
        if c + 1 < chunks:
            chunk_copy(c + 1).start()
        chunk_copy(c).wait()
        w_scr[:, c * STAGE_COLS:(c + 1) * STAGE_COLS] = stage[c % 2].astype(BF16)


def _stage_scratch(rows):
    return [pltpu.VMEM((2, rows, STAGE_COLS), F32), pltpu.SemaphoreType.DMA((2,))]


_HBM = pl.BlockSpec(memory_space=pl.ANY)


def _const_spec(shape):
    nd = len(shape)
    return pl.BlockSpec(shape, lambda i: (0,) * nd)


def _params():
    return pltpu.CompilerParams(dimension_semantics=("arbitrary",), vmem_limit_bytes=VMEM_LIMIT)


def _moe_output(x_ref, y0_ref, y1_ref, gate_ref, g_ref, b_ref):
    gates = gate_ref[...]
    g0, g1 = gates[:, 0:1], gates[:, 1:2]
    y0_lo, y0_hi = _unpack_halves(y0_ref[...])
    y1_lo, y1_hi = _unpack_halves(y1_ref[...])
    ff = jnp.concatenate([g0 * y0_lo + g1 * y1_lo, g0 * y0_hi + g1 * y1_hi], axis=1)
    return _layer_norm(ALPHA * x_ref[...] + ff, g_ref[...], b_ref[...])


def _mixer_input(src, pending):
    if not pending:
        x_ref, xh_ref = src
        return x_ref[...], xh_ref[...]
    x_ref, xh_ref, y0_ref, y0h_ref, y1_ref, y1h_ref, gate_ref, gateh_ref, g_ref, b_ref = src
    return (_moe_output(x_ref, y0_ref, y1_ref, gate_ref, g_ref, b_ref),
            _moe_output(xh_ref, y0h_ref, y1h_ref, gateh_ref, g_ref, b_ref))


def _mixer_sources(x, pending, halo):
    n, d = x.shape
    tiles = n // TM
    halo_blocks = TM // halo

    def halo_index(i):
        return jnp.maximum(i * halo_blocks - 1, 0)

    specs = [pl.BlockSpec((TM, d), lambda i: (i, 0)), pl.BlockSpec((halo, d), lambda i: (halo_index(i), 0))]
    args = [x, x]
    if pending is not None:
        yg, gates, g, b = pending
        specs += [pl.BlockSpec((TM, d // 2), lambda i: (i, 0)),
                  pl.BlockSpec((halo, d // 2), lambda i: (halo_index(i), 0)),
                  pl.BlockSpec((TM, d // 2), lambda i: (i + tiles, 0)),
                  pl.BlockSpec((halo, d // 2), lambda i: (halo_index(i) + tiles * halo_blocks, 0)),
                  pl.BlockSpec((TM, TOP_K), lambda i: (i, 0)),
                  pl.BlockSpec((halo, TOP_K), lambda i: (halo_index(i), 0)),
                  _const_spec((1, d)), _const_spec((1, d))]
        args += [yg, yg, yg, yg, gates, gates, g.reshape(1, -1), b.reshape(1, -1)]
    return specs, args


def _even_kernel(*refs, tiles_per_seq, layer, pending):
    n_src = 10 if pending else 2
    (win_hbm, wpool_ref, pscale_ref, lvg_ref, lvb_ref, ws_ref, bst_ref, wout_hbm, g_ref, b_ref, o_ref,
     a_scr, cat_scr, win_ref, wout_ref, stage, sems) = refs[n_src:]
    tm = o_ref.shape[0]
    d_pool = a_scr.shape[1]
    d_sg = lvg_ref.shape[1]
    pgd = d_pool // len(POOL_WINDOWS)
    hd_dim = d_sg // N_SG_HEADS
    seq_tile = pl.program_id(0) % tiles_per_seq

    @pl.when(pl.program_id(0) == 0)
    def _():
        _load_cast(win_hbm.at[layer], win_ref, stage, sems)
        _load_cast(wout_hbm.at[layer], wout_ref, stage, sems)

    x, xh = _mixer_input(refs[:n_src], pending)
    sub = tm // SUB_TILES

    ah = _dot(xh.astype(BF16), win_ref[:, :d_pool])
    a_scr[0:POOL_HALO, :] = jnp.where(seq_tile == 0, 0.0, ah)
    row = lax.broadcasted_iota(I32, (CHUNK, CHUNK), 0)
    col = lax.broadcasted_iota(I32, (CHUNK, CHUNK), 1)
    ws_masked = [jnp.where(row >= col, ws_ref[hd], 0.0).astype(BF16) for hd in range(N_SG_HEADS)]

    def in_proj(st):
        h = _dot(x[st * sub:(st + 1) * sub, :].astype(BF16), win_ref[...])
        a_scr[POOL_HALO + st * sub:POOL_HALO + (st + 1) * sub, :] = h[:, :d_pool]
        return h[:, d_pool:]

    def branches_out_proj(st, hz):
        base = st * sub
        pos = seq_tile * tm + base + lax.broadcasted_iota(I32, (sub, 1), 0)
        for g, w in enumerate(POOL_WINDOWS):
            cs = slice(g * pgd, (g + 1) * pgd)
            tok = a_scr[POOL_HALO + base:POOL_HALO + base + sub, cs]
            acc = a_scr[base:POOL_HALO + base + sub, cs]
            span = 1
            while span < w:
                acc = acc[span:, :] + acc[:-span, :]
                span *= 2
            acc = acc[acc.shape[0] - sub:, :]
            cnt = jnp.minimum(pos + 1, w).astype(F32)
            dev = acc * (1.0 / cnt) - tok
            yg = _dot(dev.astype(BF16), wpool_ref[g])
            cat_scr[base:base + sub, cs] = (yg * pscale_ref[:, cs]).astype(BF16)

        z = _gelu_tanh(hz)
        u = z[:, :d_sg]
        v = _layer_norm(z[:, d_sg:], lvg_ref[...], lvb_ref[...]).astype(BF16)
        for hd in range(N_SG_HEADS):
            hs = slice(hd * hd_dim, (hd + 1) * hd_dim)
            bcol = bst_ref[:, hd:hd + 1]
            for ck in range(sub // CHUNK):
                rs = slice(ck * CHUNK, (ck + 1) * CHUNK)
                sv = _dot(ws_masked[hd], v[rs, hs]) + bcol
                cat_scr[base + ck * CHUNK:base + (ck + 1) * CHUNK,
                        d_pool + hd * hd_dim:d_pool + (hd + 1) * hd_dim] = (u[rs, hs] * sv).astype(BF16)
        return _dot(cat_scr[base:base + sub, :], wout_ref[...])

    hzs = [in_proj(st) for st in range(SUB_TILES)]
    mixes = [branches_out_proj(st, hzs[st]) for st in range(SUB_TILES)]
    for st in range(SUB_TILES):
        rs = slice(st * sub, (st + 1) * sub)
        o_ref[rs, :] = _layer_norm(ALPHA * x[rs, :] + mixes[st], g_ref[...], b_ref[...])


def _even_mixer(x, pending, seq, layer, w_in, w_pool, pool_scale, ln_v_g, ln_v_b, w_spatial, b_spatial,
                w_out, g, b):
    n, d = x.shape
    d_in = w_in.shape[2]
    d_pool = pool_scale.shape[0]
    d_sg = ln_v_g.shape[0]
    kern = functools.partial(_even_kernel, tiles_per_seq=seq // TM, layer=layer, pending=pending is not None)
    src_specs, src_args = _mixer_sources(x, pending, POOL_HALO)
    return pl.pallas_call(
        kern,
        out_shape=jax.ShapeDtypeStruct((n, d), F32),
        grid=(n // TM,),
        in_specs=src_specs + [
            _HBM,
            _const_spec(w_pool.shape),
            _const_spec((1, d_pool)),
            _const_spec((1, d_sg)),
            _const_spec((1, d_sg)),
            _const_spec(w_spatial.shape),
            _const_spec((CHUNK, N_SG_HEADS)),
            _HBM,
            _const_spec((1, d)),
            _const_spec((1, d)),
        ],
        out_specs=pl.BlockSpec((TM, d), lambda i: (i, 0)),
        scratch_shapes=[pltpu.VMEM((POOL_HALO + TM, d_pool), F32), pltpu.VMEM((TM, d_pool + d_sg), BF16),
                        pltpu.VMEM((d, d_in), BF16), pltpu.VMEM((d_pool + d_sg, d), BF16)] + _stage_scratch(d),
        compiler_params=_params(),
        name="even_mixer",
    )(*src_args, w_in, w_pool.astype(BF16), pool_scale.reshape(1, -1), ln_v_g.reshape(1, -1),
      ln_v_b.reshape(1, -1), w_spatial, b_spatial.T, w_out, g.reshape(1, -1), b.reshape(1, -1))


def _odd_kernel(*refs, tiles_per_seq, layer, pending):
    n_src = 10 if pending else 2
    (win_hbm, cwt_ref, cb_ref, wout_hbm, g_ref, b_ref, o_ref, zc_scr,
     win_ref, wout_ref, stage, sems) = refs[n_src:]
    tm, d = o_ref.shape
    seq_tile = pl.program_id(0) % tiles_per_seq

    @pl.when(pl.program_id(0) == 0)
    def _():
        _load_cast(win_hbm.at[layer], win_ref, stage, sems)
        _load_cast(wout_hbm.at[layer], wout_ref, stage, sems)

    x, xh = _mixer_input(refs[:n_src], pending)
    hh = _dot(xh.astype(BF16), win_ref[:, d:])
    zc_scr[0:CONV_HALO, :] = jnp.where(seq_tile == 0, 0.0, hh[:, :d] * hh[:, d:])
    sub = tm // SUB_TILES

    def in_proj(st):
        xb = x[st * sub:(st + 1) * sub, :].astype(BF16)
        hc = _dot(xb, win_ref[:, d:2 * d])
        hz = _dot(xb, win_ref[:, 2 * d:])
        zc_scr[CONV_HALO + st * sub:CONV_HALO + (st + 1) * sub, :] = hc * hz
        return _dot(xb, win_ref[:, :d])

    def conv_out_proj(st, gate):
        conv = cb_ref[...]
        for j in range(CONV_WIDTH):
            off = CONV_HALO + st * sub - (CONV_WIDTH - 1) + j
            conv = conv + zc_scr[off:off + sub, :] * cwt_ref[j:j + 1, :]
        return _dot((gate * conv).astype(BF16), wout_ref[...])

    gates = [in_proj(st) for st in range(SUB_TILES)]
    ys = [conv_out_proj(st, gates[st]) for st in range(SUB_TILES)]
    for st in range(SUB_TILES):
        rs = slice(st * sub, (st + 1) * sub)
        o_ref[rs, :] = _layer_norm(ALPHA * x[rs, :] + ys[st], g_ref[...], b_ref[...])


def _odd_mixer(x, pending, seq, layer, w_in, conv_w, conv_b, w_out, g, b):
    n, d = x.shape
    kern = functools.partial(_odd_kernel, tiles_per_seq=seq // TM, layer=layer, pending=pending is not None)
    src_specs, src_args = _mixer_sources(x, pending, CONV_HALO)
    return pl.pallas_call(
        kern,
        out_shape=jax.ShapeDtypeStruct((n, d), F32),
        grid=(n // TM,),
        in_specs=src_specs + [
            _HBM,
            _const_spec((CONV_WIDTH, d)),
            _const_spec((1, d)),
            _HBM,
            _const_spec((1, d)),
            _const_spec((1, d)),
        ],
        out_specs=pl.BlockSpec((TM, d), lambda i: (i, 0)),
        scratch_shapes=[pltpu.VMEM((CONV_HALO + TM, d), F32),
                        pltpu.VMEM(w_in.shape[1:], BF16), pltpu.VMEM(w_out.shape[1:], BF16)] + _stage_scratch(d),
        compiler_params=_params(),
        name="odd_mixer",
    )(*src_args, w_in, conv_w.T, conv_b.reshape(1, -1), w_out, g.reshape(1, -1), b.reshape(1, -1))


def _kv_kernel(mem_ref, wk_ref, wv_ref, k_ref, v_ref):
    m = mem_ref[...].astype(BF16)
    k_ref[...] = _dot(m, wk_ref[...].astype(BF16)).astype(BF16)
    v_ref[...] = _dot(m, wv_ref[...].astype(BF16)).astype(BF16)


def _memory_kv(mem2d, wk, wv):
    nl, d, _ = wk.shape
    rows = mem2d.shape[0]
    out = jax.ShapeDtypeStruct((nl, rows, d), BF16)
    wspec = pl.BlockSpec((None, d, d), lambda l: (l, 0, 0))
    ospec = pl.BlockSpec((None, rows, d), lambda l: (l, 0, 0))
    return pl.pallas_call(
        _kv_kernel,
        out_shape=(out, out),
        grid=(nl,),
        in_specs=[_const_spec((rows, d)), wspec, wspec],
        out_specs=(ospec, ospec),
        compiler_params=_params(),
        name="memory_kv",
    )(mem2d, wk, wv)


def _attn_kernel(x_ref, k_ref, v_ref, wq_hbm, wo_hbm, g_ref, b_ref, wr_ref, br_ref,
                 o_ref, op_ref, rt_ref, rg_ref, cnt_ref, o_scr, carry_scr, wq_ref, wo_ref, stage, sems,
                 *, layer):
    tm, d = x_ref.shape
    hd_dim = d // N_XHEADS
    sub = tm // SUB_TILES

    @pl.when(pl.program_id(0) == 0)
    def _():
        carry_scr[...] = jnp.zeros_like(carry_scr)
        _load_cast(wq_hbm.at[layer], wq_ref, stage, sems)
        _load_cast(wo_hbm.at[layer], wo_ref, stage, sems)

    row_slices = [slice(st * sub, (st + 1) * sub) for st in range(SUB_TILES)]
    qs = [_dot(x_ref[rs, :].astype(BF16), wq_ref[...]) * (1.0 / math.sqrt(hd_dim)) for rs in row_slices]
    for rs, q in zip(row_slices, qs):
        for hd in range(N_XHEADS):
            hs = slice(hd * hd_dim, (hd + 1) * hd_dim)
            s = lax.dot_general(q[:, hs].astype(BF16), k_ref[:, hs], _NT, preferred_element_type=F32)
            p = jnp.exp(s - jnp.max(s, axis=-1, keepdims=True))
            p = p * (1.0 / jnp.sum(p, axis=-1, keepdims=True))
            o_scr[rs, hs] = _dot(p.astype(BF16), v_ref[:, hs]).astype(BF16)
    xas = [_dot(o_scr[rs, :], wo_ref[...]) for rs in row_slices]
    for rs, xa in zip(row_slices, xas):
        out = _layer_norm(ALPHA * x_ref[rs, :] + xa, g_ref[...], b_ref[...])
        o_ref[rs, :] = out
        op_ref[rs, :] = _pack_halves(out)
        table, gates = _route_rows(out, wr_ref, br_ref, carry_scr)
        rt_ref[:, rs] = table
        rg_ref[rs, :] = gates
    cnt_ref[...] = carry_scr[...].astype(I32)


def _cross_attn(x, seq, layer, k, v, wq, wo, g, b, router_w, router_b):
    n, d = x.shape
    m = k.shape[2]
    tiles_per_seq = seq // TM
    kvspec = pl.BlockSpec((None, None, m, d), lambda i: (layer, i // tiles_per_seq, 0, 0))
    out, packed, table, gates, cnt = pl.pallas_call(
        functools.partial(_attn_kernel, layer=layer),
        out_shape=(jax.ShapeDtypeStruct((n, d), F32), jax.ShapeDtypeStruct((n, d // 2), I32),
                   jax.ShapeDtypeStruct((ROUTE_ROWS, n), F32), jax.ShapeDtypeStruct((n, TOP_K), F32),
                   jax.ShapeDtypeStruct((1, ROUTE_COLS), I32)),
        grid=(n // TM,),
        in_specs=[
            pl.BlockSpec((TM, d), lambda i: (i, 0)),
            kvspec, kvspec,
            _HBM, _HBM,
            _const_spec((1, d)), _const_spec((1, d)),
            _const_spec((d, 2 * ROUTE_COLS)), _const_spec((1, ROUTE_COLS)),
        ],
        out_specs=(pl.BlockSpec((TM, d), lambda i: (i, 0)), pl.BlockSpec((TM, d // 2), lambda i: (i, 0)),
                   pl.BlockSpec((ROUTE_ROWS, TM), lambda i: (0, i)), pl.BlockSpec((TM, TOP_K), lambda i: (i, 0)),
                   _const_spec((1, ROUTE_COLS))),
        scratch_shapes=[pltpu.VMEM((TM, d), BF16), pltpu.VMEM((1, ROUTE_COLS), F32),
                        pltpu.VMEM((d, d), BF16), pltpu.VMEM((d, d), BF16)] + _stage_scratch(d),
        compiler_params=_params(),
        name="cross_attn",
    )(x, k, v, wq, wo, g.reshape(1, -1), b.reshape(1, -1), router_w, router_b)
    return out, packed, table, gates, cnt[0, N_GROUPS:N_GROUPS + N_EXPERTS]


def _route_rows(x, wcat_ref, bias_ref, carry_scr):
    tm = x.shape[0]

    xh = x.astype(BF16)
    xl = (x - xh.astype(F32)).astype(BF16)
    r1 = _dot(xh, wcat_ref[...])
    r2 = _dot(xl, wcat_ref[:, :ROUTE_COLS])
    logits = r1[:, :ROUTE_COLS] + r1[:, ROUTE_COLS:] + r2 + bias_ref[...]

    lane = lax.broadcasted_iota(I32, (tm, ROUTE_COLS), 1).astype(F32)
    neg = -jnp.inf

    def first_argmax(vals):
        mx = jnp.max(vals, axis=-1, keepdims=True)
        idx = jnp.min(jnp.where(vals == mx, lane, float(ROUTE_COLS)), axis=-1, keepdims=True)
        return mx, idx

    gl = jnp.where(lane < N_GROUPS, logits, neg)
    gmax, g_sel = first_argmax(gl)
    gate_g = 1.0 / jnp.sum(jnp.exp(gl - gmax), axis=-1, keepdims=True)

    lo = N_GROUPS + g_sel * EXPERTS_PER_GROUP
    el = jnp.where((lane >= lo) & (lane < lo + EXPERTS_PER_GROUP), logits, neg)
    m1, i1 = first_argmax(el)
    m2, i2 = first_argmax(jnp.where(lane == i1, neg, el))
    e21 = jnp.exp(m2 - m1)
    w1 = 1.0 / (1.0 + e21)
    w2 = e21 / (1.0 + e21)

    oh1 = lane == i1
    oh2 = lane == i2
    oh = (oh1 | oh2).astype(BF16)
    r = lax.broadcasted_iota(I32, (tm, tm), 0)
    c = lax.broadcasted_iota(I32, (tm, tm), 1)
    before = _dot((r > c).astype(BF16), oh) + carry_scr[...]
    rank1 = jnp.sum(jnp.where(oh1, before, 0.0), axis=-1, keepdims=True)
    rank2 = jnp.sum(jnp.where(oh2, before, 0.0), axis=-1, keepdims=True)
    carry_scr[...] += jnp.sum(oh.astype(F32), axis=0, keepdims=True)

    cols = jnp.where(lane == 0.0, i1 - N_GROUPS,
                     jnp.where(lane == 1.0, i2 - N_GROUPS,
                               jnp.where(lane == 2.0, rank1, jnp.where(lane == 3.0, rank2, 0.0))))
    table = jnp.transpose(cols)[:ROUTE_ROWS, :]
    l2 = lax.broadcasted_iota(I32, (tm, TOP_K), 1)
    gates = jnp.where(l2 == 0, gate_g * w1, gate_g * w2)
    return table, gates


def _router_weights(wr_g, br_g, wr_e, br_e):
    d = wr_g.shape[0]
    w = jnp.concatenate([wr_g, jnp.transpose(wr_e, (1, 0, 2)).reshape(d, N_EXPERTS)], axis=1)
    w = jnp.pad(w, ((0, 0), (0, ROUTE_COLS - w.shape[1])))
    w_hi = w.astype(BF16)
    w_lo = (w - w_hi.astype(F32)).astype(BF16)
    bias = jnp.pad(jnp.concatenate([br_g, br_e.reshape(-1)]), (0, ROUTE_COLS - N_GROUPS - N_EXPERTS))
    return jnp.concatenate([w_hi, w_lo], axis=1), bias.reshape(1, -1)


def _sc_worker_rows(rows):
    per_worker = rows // SC_WORKERS
    n_chunks = per_worker // SC_CHUNK
    assert per_worker * SC_WORKERS == rows and n_chunks * SC_CHUNK == per_worker and n_chunks % 2 == 0
    return per_worker, n_chunks


def _sc_gather_rows(table_hbm, out_hbm, idx_v, rows_v, gsem, wsem, base, n_chunks):
    def fetch(c, slot):
        off = pl.multiple_of(c * SC_CHUNK, SC_CHUNK)
        return pltpu.make_async_copy(table_hbm.at[idx_v.at[pl.ds(off, SC_CHUNK)]], rows_v.at[slot],
                                     gsem.at[slot])

    def put(c, slot):
        off = pl.multiple_of(c * SC_CHUNK, SC_CHUNK)
        return pltpu.make_async_copy(rows_v.at[slot], out_hbm.at[pl.ds(base + off, SC_CHUNK)], wsem.at[slot])

    fetch(0, 0).start()

    @pl.loop(0, n_chunks, step=2)
    def _(c0):
        for slot in range(2):
            c = c0 + slot

            @pl.when(c + 1 < n_chunks)
            def _():
                @pl.when(c >= 1)
                def _():
                    put(c - 1, 1 - slot).wait()
                fetch(c + 1, 1 - slot).start()

            fetch(c, slot).wait()
            put(c, slot).start()

    put(n_chunks - 2, 0).wait()
    put(n_chunks - 1, 1).wait()


def _sc_row_scratch(per_worker, d, dtype):
    return [pltpu.VMEM((per_worker,), I32), pltpu.VMEM((2, SC_CHUNK, d), dtype),
            pltpu.SemaphoreType.DMA((2,)), pltpu.SemaphoreType.DMA((2,))]


def _sc_gather(table, idx):
    b = idx.shape[0]
    d = table.shape[1]
    per_worker, n_chunks = _sc_worker_rows(b)
    mesh = plsc.VectorSubcoreMesh(core_axis_name="c", subcore_axis_name="s")

    @functools.partial(
        pl.kernel, mesh=mesh,
        out_type=jax.ShapeDtypeStruct((b, d), table.dtype),
        scratch_types=_sc_row_scratch(per_worker, d, table.dtype),
        name="sc_gather",
    )
    def gather(table_hbm, idx_hbm, out_hbm, idx_v, rows_v, gsem, wsem):
        base = (lax.axis_index("s") * SC_CORES + lax.axis_index("c")) * per_worker
        pltpu.sync_copy(idx_hbm.at[pl.ds(base, per_worker)], idx_v)
        _sc_gather_rows(table_hbm, out_hbm, idx_v, rows_v, gsem, wsem, base, n_chunks)

    return gather(table, idx)


def _sc_dispatch(table, dest_flat, rows):
    n, d = table.shape
    a = dest_flat.shape[0]
    lanes = SC_LANES
    per_worker, n_chunks = _sc_worker_rows(rows)
    assert a % lanes == 0 and per_worker % lanes == 0
    mesh = plsc.VectorSubcoreMesh(core_axis_name="c", subcore_axis_name="s")

    @functools.partial(
        pl.kernel, mesh=mesh,
        out_type=jax.ShapeDtypeStruct((rows, d), table.dtype),
        scratch_types=[pltpu.VMEM((a,), I32)] + _sc_row_scratch(per_worker, d, table.dtype),
        compiler_params=pltpu.CompilerParams(needs_layout_passes=False),
        name="sc_dispatch",
    )
    def dispatch(table_hbm, dest_hbm, out_hbm, dest_v, idx_v, rows_v, gsem, wsem):
        base = (lax.axis_index("s") * SC_CORES + lax.axis_index("c")) * per_worker
        pltpu.sync_copy(dest_hbm, dest_v)
        lane = lax.iota(I32, lanes)

        @pl.loop(0, per_worker // lanes)
        def _(i):
            idx_v[pl.ds(i * lanes, lanes)] = lax.rem(base + i * lanes + lane, n)

        @plsc.parallel_loop(0, a // lanes, unroll=4)
        def _(i):
            local = dest_v[pl.ds(i * lanes, lanes)] - base
            mine = (local >= 0) & (local < per_worker)
            plsc.store_scatter(idx_v, [jnp.where(mine, local, 0)], lax.rem(i * lanes + lane, n), mask=mine)

        _sc_gather_rows(table_hbm, out_hbm, idx_v, rows_v, gsem, wsem, base, n_chunks)

    return dispatch(table, dest_flat)


def _expert_kernel(sched_ref, nb_ref, xs_ref, w1_hbm, w3_hbm, w2_hbm, y_ref,
                   w1_buf, w3_buf, w2_buf, w1_scr, w3_scr, w2_scr, sems, *, layer):
    b = pl.program_id(0)
    used = b < nb_ref[0]
    expert, slot, run_start, next_expert = (sched_ref[r, b] for r in range(4))

    def fetch(e, s):
        return [pltpu.make_async_copy(w_hbm.at[layer, e], buf.at[s], sems.at[s, j])
                for j, (w_hbm, buf) in enumerate(((w1_hbm, w1_buf), (w3_hbm, w3_buf), (w2_hbm, w2_buf)))]

    @pl.when(used & (run_start == 1))
    def _():
        @pl.when(b == 0)
        def _():
            for c in fetch(expert, slot):
                c.start()

        for c in fetch(expert, slot):
            c.wait()

        @pl.when(next_expert >= 0)
        def _():
            for c in fetch(next_expert, 1 - slot):
                c.start()

        w1_scr[...] = w1_buf[slot].astype(BF16)
        w3_scr[...] = w3_buf[slot].astype(BF16)
        w2_scr[...] = w2_buf[slot].astype(BF16)

    @pl.when(used)
    def _():
        x_lo, x_hi = _unpack_halves(xs_ref[...])
        xb = jnp.concatenate([x_lo.astype(BF16), x_hi.astype(BF16)], axis=1)
        h1 = _dot(xb, w1_scr[...])
        h3 = _dot(xb, w3_scr[...])
        hid = h1 * (1.0 / (1.0 + jnp.exp(-h1))) * h3
        y_ref[...] = _pack_halves(_dot(hid.astype(BF16), w2_scr[...]))

    @pl.when(jnp.logical_not(used))
    def _():
        y_ref[...] = jnp.zeros_like(y_ref)


def _expert_mlp(xs, schedule, n_used, layer, w1, w3, w2, n_blocks):
    d, de = w1.shape[2], w1.shape[3]

    def row_map(b, sched, nb):
        return (jnp.minimum(b, nb[0] - 1), 0)

    return pl.pallas_call(
        functools.partial(_expert_kernel, layer=layer),
        out_shape=jax.ShapeDtypeStruct((n_blocks * EXPERT_BLOCK, d // 2), I32),
        grid_spec=pltpu.PrefetchScalarGridSpec(
            num_scalar_prefetch=2,
            grid=(n_blocks,),
            in_specs=[pl.BlockSpec((EXPERT_BLOCK, d // 2), row_map), _HBM, _HBM, _HBM],
            out_specs=pl.BlockSpec((EXPERT_BLOCK, d // 2), lambda b, sched, nb: (b, 0)),
            scratch_shapes=[pltpu.VMEM((2, d, de), F32), pltpu.VMEM((2, d, de), F32), pltpu.VMEM((2, de, d), F32),
                            pltpu.VMEM((d, de), BF16), pltpu.VMEM((d, de), BF16), pltpu.VMEM((de, d), BF16),
                            pltpu.SemaphoreType.DMA((2, 3))],
        ),
        compiler_params=_params(),
        name="expert_mlp",
    )(schedule, n_used, xs, w1, w3, w2)


def _combine_kernel(x_ref, y0_ref, y1_ref, gate_ref, g_ref, b_ref, o_ref):
    o_ref[...] = _moe_output(x_ref, y0_ref, y1_ref, gate_ref, g_ref, b_ref)


def _combine(x, yg, gates, g, b):
    n, d = x.shape
    tiles = n // TM
    return pl.pallas_call(
        _combine_kernel,
        out_shape=jax.ShapeDtypeStruct((n, d), F32),
        grid=(tiles,),
        in_specs=[pl.BlockSpec((TM, d), lambda i: (i, 0)),
                  pl.BlockSpec((TM, d // 2), lambda i: (i, 0)),
                  pl.BlockSpec((TM, d // 2), lambda i: (i + tiles, 0)),
                  pl.BlockSpec((TM, TOP_K), lambda i: (i, 0)),
                  _const_spec((1, d)), _const_spec((1, d))],
        out_specs=pl.BlockSpec((TM, d), lambda i: (i, 0)),
        compiler_params=_params(),
        name="combine",
    )(x, yg, yg, gates, g.reshape(1, -1), b.reshape(1, -1))


def _moe_experts(x_packed, table, counts, layer, w1, w3, w2):
    n = x_packed.shape[0]
    n_blocks = (n * TOP_K + N_EXPERTS * (EXPERT_BLOCK - 1) + EXPERT_BLOCK - 1) // EXPERT_BLOCK
    experts = table[:TOP_K].astype(I32)
    ranks = table[TOP_K:2 * TOP_K].astype(I32)

    blocks_e = (counts + EXPERT_BLOCK - 1) // EXPERT_BLOCK
    blocks_end = jnp.cumsum(blocks_e)
    run_start = (blocks_end - blocks_e) * EXPERT_BLOCK
    n_used = blocks_end[-1:].astype(I32)
    block_ids = jnp.arange(n_blocks, dtype=I32)
    block_expert = jnp.minimum(jnp.sum(blocks_end[None, :] <= block_ids[:, None], axis=1),
                               N_EXPERTS - 1).astype(I32)
    run_start_flag = jnp.concatenate([jnp.ones((1,), I32),
                                      (block_expert[1:] != block_expert[:-1]).astype(I32)])
    slot = (jnp.cumsum(run_start_flag) - 1) % 2
    next_block = blocks_end[block_expert]
    next_expert = jnp.where(next_block < n_used[0],
                            block_expert[jnp.minimum(next_block, n_blocks - 1)], -1)
    schedule = jnp.stack([block_expert, slot, run_start_flag, next_expert]).astype(I32)
    expert_ids = jnp.arange(N_EXPERTS, dtype=I32)
    start_of = jnp.sum(jnp.where(experts[:, :, None] == expert_ids, run_start, 0), axis=-1)
    dest = (start_of + ranks).astype(I32).reshape(-1)
    xs = _sc_dispatch(x_packed, dest, n_blocks * EXPERT_BLOCK)
    y = _expert_mlp(xs, schedule, n_used, layer, w1, w3, w2, n_blocks)
    return _sc_gather(y, dest)


def kernel(x, mem, w_in_even, w_pool, pool_scale, ln_v_g, ln_v_b, w_spatial, b_spatial, w_out_even,
           w_in_odd, conv_w, conv_b, w_out_odd, wq_x, wk_x, wv_x, wo_x, ln_g, ln_b, wr_group,
           br_group, wr_expert, br_expert, w1, w3, w2):
    bsz, seq, d = x.shape
    assert seq % TM == 0 and d % LANES == 0
    mlen = mem.shape[1]
    k_all, v_all = _memory_kv(mem.reshape(bsz * mlen, d), wk_x, wv_x)
    k_all = k_all.reshape(DEPTH, bsz, mlen, d)
    v_all = v_all.reshape(DEPTH, bsz, mlen, d)
    h = x.reshape(bsz * seq, d)
    pending = None
    for l in range(DEPTH):
        i = l // 2
        if l % 2 == 0:
            h = _even_mixer(h, pending, seq, i, w_in_even, w_pool[i], pool_scale[i], ln_v_g[i], ln_v_b[i],
                            w_spatial[i], b_spatial[i], w_out_even, ln_g[l, 0], ln_b[l, 0])
        else:
            h = _odd_mixer(h, pending, seq, i, w_in_odd, conv_w[i], conv_b[i], w_out_odd,
                           ln_g[l, 0], ln_b[l, 0])
        router_w, router_b = _router_weights(wr_group[l], br_group[l], wr_expert[l], br_expert[l])
        h, hp, table, gates, counts = _cross_attn(h, seq, l, k_all, v_all, wq_x, wo_x,
                                                  ln_g[l, 1], ln_b[l, 1], router_w, router_b)
        yg = _moe_experts(hp, table, counts, l, w1, w3, w2)
        pending = (yg, gates, ln_g[l, 2], ln_b[l, 2])
    return _combine(h, *pending).reshape(bsz, seq, d)
```

```python
import functools
import math

import jax
import jax.numpy as jnp
from jax import lax
from jax.experimental import pallas as pl
from jax.experimental.pallas import tpu as pltpu
from jax.experimental.pallas import tpu_sc as plsc

F32 = jnp.float32
BF16 = jnp.bfloat16
I32 = jnp.int32

POOL_WINDOWS = (2, 4, 8, 16)
assert all(w & (w - 1) == 0 for w in POOL_WINDOWS)
N_SG_HEADS = 4
CHUNK = 128
CONV_WIDTH = 3
N_XHEADS = 4
N_GROUPS = 4
EXPERTS_PER_GROUP = 8
N_EXPERTS = N_GROUPS * EXPERTS_PER_GROUP
TOP_K = 2
DEPTH = 4
ALPHA = (2.0 * DEPTH) ** 0.25
LN_EPS = 1e-5

LANES = 128
SC_CORES = 2
SC_WORKERS = 32
SC_LANES = 16
VMEM_LIMIT = 56 * 1024 * 1024

SC_CHUNK = 64
TM = 1024
SUB_TILES = 2
ROUTE_ROWS = 8
POOL_HALO = 16
CONV_HALO = 8
EXPERT_BLOCK = 512
ROUTE_COLS = 128
STAGE_COLS = 512

_NT = (((1,), (1,)), ((), ()))


def _dot(a, b):
    return jnp.dot(a, b, preferred_element_type=F32)


def _layer_norm(y, g, b):
    mu = jnp.mean(y, axis=-1, keepdims=True)
    yc = y - mu
    var = jnp.mean(yc * yc, axis=-1, keepdims=True)
    return yc * lax.rsqrt(var + LN_EPS) * g + b


def _gelu_tanh(x):
    c = math.sqrt(2.0 / math.pi)
    return 0.5 * x * (1.0 + jnp.tanh(c * (x + 0.044715 * (x * x * x))))


def _pack_halves(v):
    c = v.shape[1] // 2
    lo = pltpu.bitcast(v[:, :c].astype(BF16).astype(F32), jnp.uint32)
    hi = pltpu.bitcast(v[:, c:].astype(BF16).astype(F32), jnp.uint32)
    return pltpu.bitcast((hi & jnp.uint32(0xFFFF0000)) | (lo >> 16), I32)


def _unpack_halves(w):
    u = pltpu.bitcast(w, jnp.uint32)
    return pltpu.bitcast(u << 16, F32), pltpu.bitcast(u & jnp.uint32(0xFFFF0000), F32)


def _load_cast(w_hbm, w_scr, stage, sems):
    chunks = w_scr.shape[1] // STAGE_COLS

    def chunk_copy(c):
        return pltpu.make_async_copy(w_hbm.at[:, pl.ds(c * STAGE_COLS, STAGE_COLS)], stage.at[c % 2],
                                     sems.at[c % 2])

    chunk_copy(0).start()
    for c in range(chunks):
        if c + 1 < chunks:
            chunk_copy(c + 1).start()
        chunk_copy(c).wait()
        w_scr[:, c * STAGE_COLS:(c + 1) * STAGE_COLS] = stage[c % 2].astype(BF16)


def _stage_scratch(rows):
    return [pltpu.VMEM((2, rows, STAGE_COLS), F32), pltpu.SemaphoreType.DMA((2,))]


_HBM = pl.BlockSpec(memory_space=pl.ANY)


def _const_spec(shape):
    nd = len(shape)
    return pl.BlockSpec(shape, lambda i: (0,) * nd)


def _params():
    return pltpu.CompilerParams(dimension_semantics=("arbitrary",), vmem_limit_bytes=VMEM_LIMIT)


def _moe_output(x_ref, y0_ref, y1_ref, gate_ref, g_ref, b_ref):
    gates = gate_ref[...]
    g0, g1 = gates[:, 0:1], gates[:, 1:2]
    y0_lo, y0_hi = _unpack_halves(y0_ref[...])
    y1_lo, y1_hi = _unpack_halves(y1_ref[...])
    ff = jnp.concatenate([g0 * y0_lo + g1 * y1_lo, g0 * y0_hi + g1 * y1_hi], axis=1)
    return _layer_norm(ALPHA * x_ref[...] + ff, g_ref[...], b_ref[...])


def _mixer_input(src, pending):
    if not pending:
        x_ref, xh_ref = src
        return x_ref[...], xh_ref[...]
    x_ref, xh_ref, y0_ref, y0h_ref, y1_ref, y1h_ref, gate_ref, gateh_ref, g_ref, b_ref = src
    return (_moe_output(x_ref, y0_ref, y1_ref, gate_ref, g_ref, b_ref),
            _moe_output(xh_ref, y0h_ref, y1h_ref, gateh_ref, g_ref, b_ref))


def _mixer_sources(x, pending, halo):
    n, d = x.shape
    tiles = n // TM
    halo_blocks = TM // halo

    def halo_index(i):
        return jnp.maximum(i * halo_blocks - 1, 0)

    specs = [pl.BlockSpec((TM, d), lambda i: (i, 0)), pl.BlockSpec((halo, d), lambda i: (halo_index(i), 0))]
    args = [x, x]
    if pending is not None:
        yg, gates, g, b = pending
        specs += [pl.BlockSpec((TM, d // 2), lambda i: (i, 0)),
                  pl.BlockSpec((halo, d // 2), lambda i: (halo_index(i), 0)),
                  pl.BlockSpec((TM, d // 2), lambda i: (i + tiles, 0)),
                  pl.BlockSpec((halo, d // 2), lambda i: (halo_index(i) + tiles * halo_blocks, 0)),
                  pl.BlockSpec((TM, TOP_K), lambda i: (i, 0)),
                  pl.BlockSpec((halo, TOP_K), lambda i: (halo_index(i), 0)),
                  _const_spec((1, d)), _const_spec((1, d))]
        args += [yg, yg, yg, yg, gates, gates, g.reshape(1, -1), b.reshape(1, -1)]
    return specs, args


def _even_kernel(*refs, tiles_per_seq, layer, pending):
    n_src = 10 if pending else 2
    (win_hbm, wpool_ref, pscale_ref, lvg_ref, lvb_ref, ws_ref, bst_ref, wout_hbm, g_ref, b_ref, o_ref,
     a_scr, cat_scr, win_ref, wout_ref, stage, sems) = refs[n_src:]
    tm = o_ref.shape[0]
    d_pool = a_scr.shape[1]
    d_sg = lvg_ref.shape[1]
    pgd = d_pool // len(POOL_WINDOWS)
    hd_dim = d_sg // N_SG_HEADS
    seq_tile = pl.program_id(0) % tiles_per_seq

    @pl.when(pl.program_id(0) == 0)
    def _():
        _load_cast(win_hbm.at[layer], win_ref, stage, sems)
        _load_cast(wout_hbm.at[layer], wout_ref, stage, sems)

    x, xh = _mixer_input(refs[:n_src], pending)
    sub = tm // SUB_TILES

    ah = _dot(xh.astype(BF16), win_ref[:, :d_pool])
    a_scr[0:POOL_HALO, :] = jnp.where(seq_tile == 0, 0.0, ah)
    row = lax.broadcasted_iota(I32, (CHUNK, CHUNK), 0)
    col = lax.broadcasted_iota(I32, (CHUNK, CHUNK), 1)
    ws_masked = [jnp.where(row >= col, ws_ref[hd], 0.0).astype(BF16) for hd in range(N_SG_HEADS)]

    def in_proj(st):
        h = _dot(x[st * sub:(st + 1) * sub, :].astype(BF16), win_ref[...])
        a_scr[POOL_HALO + st * sub:POOL_HALO + (st + 1) * sub, :] = h[:, :d_pool]
        return h[:, d_pool:]

    def branches_out_proj(st, hz):
        base = st * sub
        pos = seq_tile * tm + base + lax.broadcasted_iota(I32, (sub, 1), 0)
        for g, w in enumerate(POOL_WINDOWS):
            cs = slice(g * pgd, (g + 1) * pgd)
            tok = a_scr[POOL_HALO + base:POOL_HALO + base + sub, cs]
            acc = a_scr[base:POOL_HALO + base + sub, cs]
            span = 1
            while span < w:
                acc = acc[span:, :] + acc[:-span, :]
                span *= 2
            acc = acc[acc.shape[0] - sub:, :]
            cnt = jnp.minimum(pos + 1, w).astype(F32)
            dev = acc * (1.0 / cnt) - tok
            yg = _dot(dev.astype(BF16), wpool_ref[g])
            cat_scr[base:base + sub, cs] = (yg * pscale_ref[:, cs]).astype(BF16)

        z = _gelu_tanh(hz)
        u = z[:, :d_sg]
        v = _layer_norm(z[:, d_sg:], lvg_ref[...], lvb_ref[...]).astype(BF16)
        for hd in range(N_SG_HEADS):
            hs = slice(hd * hd_dim, (hd + 1) * hd_dim)
            bcol = bst_ref[:, hd:hd + 1]
            for ck in range(sub // CHUNK):
                rs = slice(ck * CHUNK, (ck + 1) * CHUNK)
                sv = _dot(ws_masked[hd], v[rs, hs]) + bcol
                cat_scr[base + ck * CHUNK:base + (ck + 1) * CHUNK,
                        d_pool + hd * hd_dim:d_pool + (hd + 1) * hd_dim] = (u[rs, hs] * sv).astype(BF16)
        return _dot(cat_scr[base:base + sub, :], wout_ref[...])

    hzs = [in_proj(st) for st in range(SUB_TILES)]
    mixes = [branches_out_proj(st, hzs[st]) for st in range(SUB_TILES)]
    for st in range(SUB_TILES):
        rs = slice(st * sub, (st + 1) * sub)
        o_ref[rs, :] = _layer_norm(ALPHA * x[rs, :] + mixes[st], g_ref[...], b_ref[...])


def _even_mixer(x, pending, seq, layer, w_in, w_pool, pool_scale, ln_v_g, ln_v_b, w_spatial, b_spatial,
                w_out, g, b):
    n, d = x.shape
    d_in = w_in.shape[2]
    d_pool = pool_scale.shape[0]
    d_sg = ln_v_g.shape[0]
    kern = functools.partial(_even_kernel, tiles_per_seq=seq // TM, layer=layer, pending=pending is not None)
    src_specs, src_args = _mixer_sources(x, pending, POOL_HALO)
    return pl.pallas_call(
        kern,
        out_shape=jax.ShapeDtypeStruct((n, d), F32),
        grid=(n // TM,),
        in_specs=src_specs + [
            _HBM,
            _const_spec(w_pool.shape),
            _const_spec((1, d_pool)),
            _const_spec((1, d_sg)),
            _const_spec((1, d_sg)),
            _const_spec(w_spatial.shape),
            _const_spec((CHUNK, N_SG_HEADS)),
            _HBM,
            _const_spec((1, d)),
            _const_spec((1, d)),
        ],
        out_specs=pl.BlockSpec((TM, d), lambda i: (i, 0)),
        scratch_shapes=[pltpu.VMEM((POOL_HALO + TM, d_pool), F32), pltpu.VMEM((TM, d_pool + d_sg), BF16),
                        pltpu.VMEM((d, d_in), BF16), pltpu.VMEM((d_pool + d_sg, d), BF16)] + _stage_scratch(d),
        compiler_params=_params(),
        name="even_mixer",
    )(*src_args, w_in, w_pool.astype(BF16), pool_scale.reshape(1, -1), ln_v_g.reshape(1, -1),
      ln_v_b.reshape(1, -1), w_spatial, b_spatial.T, w_out, g.reshape(1, -1), b.reshape(1, -1))


def _odd_kernel(*refs, tiles_per_seq, layer, pending):
    n_src = 10 if pending else 2
    (win_hbm, cwt_ref, cb_ref, wout_hbm, g_ref, b_ref, o_ref, zc_scr,
     win_ref, wout_ref, stage, sems) = refs[n_src:]
    tm, d = o_ref.shape
    seq_tile = pl.program_id(0) % tiles_per_seq

    @pl.when(pl.program_id(0) == 0)
    def _():
        _load_cast(win_hbm.at[layer], win_ref, stage, sems)
        _load_cast(wout_hbm.at[layer], wout_ref, stage, sems)

    x, xh = _mixer_input(refs[:n_src], pending)
    hh = _dot(xh.astype(BF16), win_ref[:, d:])
    zc_scr[0:CONV_HALO, :] = jnp.where(seq_tile == 0, 0.0, hh[:, :d] * hh[:, d:])
    sub = tm // SUB_TILES

    def in_proj(st):
        xb = x[st * sub:(st + 1) * sub, :].astype(BF16)
        hc = _dot(xb, win_ref[:, d:2 * d])
        hz = _dot(xb, win_ref[:, 2 * d:])
        zc_scr[CONV_HALO + st * sub:CONV_HALO + (st + 1) * sub, :] = hc * hz
        return _dot(xb, win_ref[:, :d])

    def conv_out_proj(st, gate):
        conv = cb_ref[...]
        for j in range(CONV_WIDTH):
            off = CONV_HALO + st * sub - (CONV_WIDTH - 1) + j
            conv = conv + zc_scr[off:off + sub, :] * cwt_ref[j:j + 1, :]
        return _dot((gate * conv).astype(BF16), wout_ref[...])

    gates = [in_proj(st) for st in range(SUB_TILES)]
    ys = [conv_out_proj(st, gates[st]) for st in range(SUB_TILES)]
    for st in range(SUB_TILES):
        rs = slice(st * sub, (st + 1) * sub)
        o_ref[rs, :] = _layer_norm(ALPHA * x[rs, :] + ys[st], g_ref[...], b_ref[...])


def _odd_mixer(x, pending, seq, layer, w_in, conv_w, conv_b, w_out, g, b):
    n, d = x.shape
    kern = functools.partial(_odd_kernel, tiles_per_seq=seq // TM, layer=layer, pending=pending is not None)
    src_specs, src_args = _mixer_sources(x, pending, CONV_HALO)
    return pl.pallas_call(
        kern,
        out_shape=jax.ShapeDtypeStruct((n, d), F32),
        grid=(n // TM,),
        in_specs=src_specs + [
            _HBM,
            _const_spec((CONV_WIDTH, d)),
            _const_spec((1, d)),
            _HBM,
            _const_spec((1, d)),
            _const_spec((1, d)),
        ],
        out_specs=pl.BlockSpec((TM, d), lambda i: (i, 0)),
        scratch_shapes=[pltpu.VMEM((CONV_HALO + TM, d), F32),
                        pltpu.VMEM(w_in.shape[1:], BF16), pltpu.VMEM(w_out.shape[1:], BF16)] + _stage_scratch(d),
        compiler_params=_params(),
        name="odd_mixer",
    )(*src_args, w_in, conv_w.T, conv_b.reshape(1, -1), w_out, g.reshape(1, -1), b.reshape(1, -1))


def _kv_kernel(mem_ref, wk_ref, wv_ref, k_ref, v_ref):
    m = mem_ref[...].astype(BF16)
    k_ref[...] = _dot(m, wk_ref[...].astype(BF16)).astype(BF16)
    v_ref[...] = _dot(m, wv_ref[...].astype(BF16)).astype(BF16)


def _memory_kv(mem2d, wk, wv):
    nl, d, _ = wk.shape
    rows = mem2d.shape[0]
    out = jax.ShapeDtypeStruct((nl, rows, d), BF16)
    wspec = pl.BlockSpec((None, d, d), lambda l: (l, 0, 0))
    ospec = pl.BlockSpec((None, rows, d), lambda l: (l, 0, 0))
    return pl.pallas_call(
        _kv_kernel,
        out_shape=(out, out),
        grid=(nl,),
        in_specs=[_const_spec((rows, d)), wspec, wspec],
        out_specs=(ospec, ospec),
        compiler_params=_params(),
        name="memory_kv",
    )(mem2d, wk, wv)


def _attn_kernel(x_ref, k_ref, v_ref, wq_hbm, wo_hbm, g_ref, b_ref, wr_ref, br_ref,
                 o_ref, op_ref, rt_ref, rg_ref, cnt_ref, o_scr, carry_scr, wq_ref, wo_ref, stage, sems,
                 *, layer):
    tm, d = x_ref.shape
    hd_dim = d // N_XHEADS
    sub = tm // SUB_TILES

    @pl.when(pl.program_id(0) == 0)
    def _():
        carry_scr[...] = jnp.zeros_like(carry_scr)
        _load_cast(wq_hbm.at[layer], wq_ref, stage, sems)
        _load_cast(wo_hbm.at[layer], wo_ref, stage, sems)

    row_slices = [slice(st * sub, (st + 1) * sub) for st in range(SUB_TILES)]
    qs = [_dot(x_ref[rs, :].astype(BF16), wq_ref[...]) * (1.0 / math.sqrt(hd_dim)) for rs in row_slices]
    for rs, q in zip(row_slices, qs):
        for hd in range(N_XHEADS):
            hs = slice(hd * hd_dim, (hd + 1) * hd_dim)
            s = lax.dot_general(q[:, hs].astype(BF16), k_ref[:, hs], _NT, preferred_element_type=F32)
            p = jnp.exp(s - jnp.max(s, axis=-1, keepdims=True))
            p = p * (1.0 / jnp.sum(p, axis=-1, keepdims=True))
            o_scr[rs, hs] = _dot(p.astype(BF16), v_ref[:, hs]).astype(BF16)
    xas = [_dot(o_scr[rs, :], wo_ref[...]) for rs in row_slices]
    for rs, xa in zip(row_slices, xas):
        out = _layer_norm(ALPHA * x_ref[rs, :] + xa, g_ref[...], b_ref[...])
        o_ref[rs, :] = out
        op_ref[rs, :] = _pack_halves(out)
        table, gates = _route_rows(out, wr_ref, br_ref, carry_scr)
        rt_ref[:, rs] = table
        rg_ref[rs, :] = gates
    cnt_ref[...] = carry_scr[...].astype(I32)


def _cross_attn(x, seq, layer, k, v, wq, wo, g, b, router_w, router_b):
    n, d = x.shape
    m = k.shape[2]
    tiles_per_seq = seq // TM
    kvspec = pl.BlockSpec((None, None, m, d), lambda i: (layer, i // tiles_per_seq, 0, 0))
    out, packed, table, gates, cnt = pl.pallas_call(
        functools.partial(_attn_kernel, layer=layer),
        out_shape=(jax.ShapeDtypeStruct((n, d), F32), jax.ShapeDtypeStruct((n, d // 2), I32),
                   jax.ShapeDtypeStruct((ROUTE_ROWS, n), F32), jax.ShapeDtypeStruct((n, TOP_K), F32),
                   jax.ShapeDtypeStruct((1, ROUTE_COLS), I32)),
        grid=(n // TM,),
        in_specs=[
            pl.BlockSpec((TM, d), lambda i: (i, 0)),
            kvspec, kvspec,
            _HBM, _HBM,
            _const_spec((1, d)), _const_spec((1, d)),
            _const_spec((d, 2 * ROUTE_COLS)), _const_spec((1, ROUTE_COLS)),
        ],
        out_specs=(pl.BlockSpec((TM, d), lambda i: (i, 0)), pl.BlockSpec((TM, d // 2), lambda i: (i, 0)),
                   pl.BlockSpec((ROUTE_ROWS, TM), lambda i: (0, i)), pl.BlockSpec((TM, TOP_K), lambda i: (i, 0)),
                   _const_spec((1, ROUTE_COLS))),
        scratch_shapes=[pltpu.VMEM((TM, d), BF16), pltpu.VMEM((1, ROUTE_COLS), F32),
                        pltpu.VMEM((d, d), BF16), pltpu.VMEM((d, d), BF16)] + _stage_scratch(d),
        compiler_params=_params(),
        name="cross_attn",
    )(x, k, v, wq, wo, g.reshape(1, -1), b.reshape(1, -1), router_w, router_b)
    return out, packed, table, gates, cnt[0, N_GROUPS:N_GROUPS + N_EXPERTS]


def _route_rows(x, wcat_ref, bias_ref, carry_scr):
    tm = x.shape[0]

    xh = x.astype(BF16)
    xl = (x - xh.astype(F32)).astype(BF16)
    r1 = _dot(xh, wcat_ref[...])
    r2 = _dot(xl, wcat_ref[:, :ROUTE_COLS])
    logits = r1[:, :ROUTE_COLS] + r1[:, ROUTE_COLS:] + r2 + bias_ref[...]

    lane = lax.broadcasted_iota(I32, (tm, ROUTE_COLS), 1).astype(F32)
    neg = -jnp.inf

    def first_argmax(vals):
        mx = jnp.max(vals, axis=-1, keepdims=True)
        idx = jnp.min(jnp.where(vals == mx, lane, float(ROUTE_COLS)), axis=-1, keepdims=True)
        return mx, idx

    gl = jnp.where(lane < N_GROUPS, logits, neg)
    gmax, g_sel = first_argmax(gl)
    gate_g = 1.0 / jnp.sum(jnp.exp(gl - gmax), axis=-1, keepdims=True)

    lo = N_GROUPS + g_sel * EXPERTS_PER_GROUP
    el = jnp.where((lane >= lo) & (lane < lo + EXPERTS_PER_GROUP), logits, neg)
    m1, i1 = first_argmax(el)
    m2, i2 = first_argmax(jnp.where(lane == i1, neg, el))
    e21 = jnp.exp(m2 - m1)
    w1 = 1.0 / (1.0 + e21)
    w2 = e21 / (1.0 + e21)

    oh1 = lane == i1
    oh2 = lane == i2
    oh = (oh1 | oh2).astype(BF16)
    r = lax.broadcasted_iota(I32, (tm, tm), 0)
    c = lax.broadcasted_iota(I32, (tm, tm), 1)
    before = _dot((r > c).astype(BF16), oh) + carry_scr[...]
    rank1 = jnp.sum(jnp.where(oh1, before, 0.0), axis=-1, keepdims=True)
    rank2 = jnp.sum(jnp.where(oh2, before, 0.0), axis=-1, keepdims=True)
    carry_scr[...] += jnp.sum(oh.astype(F32), axis=0, keepdims=True)

    cols = jnp.where(lane == 0.0, i1 - N_GROUPS,
                     jnp.where(lane == 1.0, i2 - N_GROUPS,
                               jnp.where(lane == 2.0, rank1, jnp.where(lane == 3.0, rank2, 0.0))))
    table = jnp.transpose(cols)[:ROUTE_ROWS, :]
    l2 = lax.broadcasted_iota(I32, (tm, TOP_K), 1)
    gates = jnp.where(l2 == 0, gate_g * w1, gate_g * w2)
    return table, gates


def _router_weights(wr_g, br_g, wr_e, br_e):
    d = wr_g.shape[0]
    w = jnp.concatenate([wr_g, jnp.transpose(wr_e, (1, 0, 2)).reshape(d, N_EXPERTS)], axis=1)
    w = jnp.pad(w, ((0, 0), (0, ROUTE_COLS - w.shape[1])))
    w_hi = w.astype(BF16)
    w_lo = (w - w_hi.astype(F32)).astype(BF16)
    bias = jnp.pad(jnp.concatenate([br_g, br_e.reshape(-1)]), (0, ROUTE_COLS - N_GROUPS - N_EXPERTS))
    return jnp.concatenate([w_hi, w_lo], axis=1), bias.reshape(1, -1)


def _sc_worker_rows(rows):
    per_worker = rows // SC_WORKERS
    n_chunks = per_worker // SC_CHUNK
    assert per_worker * SC_WORKERS == rows and n_chunks * SC_CHUNK == per_worker and n_chunks % 2 == 0
    return per_worker, n_chunks


def _sc_gather_rows(table_hbm, out_hbm, idx_v, rows_v, gsem, wsem, base, n_chunks):
    def fetch(c, slot):
        off = pl.multiple_of(c * SC_CHUNK, SC_CHUNK)
        return pltpu.make_async_copy(table_hbm.at[idx_v.at[pl.ds(off, SC_CHUNK)]], rows_v.at[slot],
                                     gsem.at[slot])

    def put(c, slot):
        off = pl.multiple_of(c * SC_CHUNK, SC_CHUNK)
        return pltpu.make_async_copy(rows_v.at[slot], out_hbm.at[pl.ds(base + off, SC_CHUNK)], wsem.at[slot])

    fetch(0, 0).start()

    @pl.loop(0, n_chunks, step=2)
    def _(c0):
        for slot in range(2):
            c = c0 + slot

            @pl.when(c + 1 < n_chunks)
            def _():
                @pl.when(c >= 1)
                def _():
                    put(c - 1, 1 - slot).wait()
                fetch(c + 1, 1 - slot).start()

            fetch(c, slot).wait()
            put(c, slot).start()

    put(n_chunks - 2, 0).wait()
    put(n_chunks - 1, 1).wait()


def _sc_row_scratch(per_worker, d, dtype):
    return [pltpu.VMEM((per_worker,), I32), pltpu.VMEM((2, SC_CHUNK, d), dtype),
            pltpu.SemaphoreType.DMA((2,)), pltpu.SemaphoreType.DMA((2,))]


def _sc_gather(table, idx):
    b = idx.shape[0]
    d = table.shape[1]
    per_worker, n_chunks = _sc_worker_rows(b)
    mesh = plsc.VectorSubcoreMesh(core_axis_name="c", subcore_axis_name="s")

    @functools.partial(
        pl.kernel, mesh=mesh,
        out_type=jax.ShapeDtypeStruct((b, d), table.dtype),
        scratch_types=_sc_row_scratch(per_worker, d, table.dtype),
        name="sc_gather",
    )
    def gather(table_hbm, idx_hbm, out_hbm, idx_v, rows_v, gsem, wsem):
        base = (lax.axis_index("s") * SC_CORES + lax.axis_index("c")) * per_worker
        pltpu.sync_copy(idx_hbm.at[pl.ds(base, per_worker)], idx_v)
        _sc_gather_rows(table_hbm, out_hbm, idx_v, rows_v, gsem, wsem, base, n_chunks)

    return gather(table, idx)


def _sc_dispatch(table, dest_flat, rows):
    n, d = table.shape
    a = dest_flat.shape[0]
    lanes = SC_LANES
    per_worker, n_chunks = _sc_worker_rows(rows)
    assert a % lanes == 0 and per_worker % lanes == 0
    mesh = plsc.VectorSubcoreMesh(core_axis_name="c", subcore_axis_name="s")

    @functools.partial(
        pl.kernel, mesh=mesh,
        out_type=jax.ShapeDtypeStruct((rows, d), table.dtype),
        scratch_types=[pltpu.VMEM((a,), I32)] + _sc_row_scratch(per_worker, d, table.dtype),
        compiler_params=pltpu.CompilerParams(needs_layout_passes=False),
        name="sc_dispatch",
    )
    def dispatch(table_hbm, dest_hbm, out_hbm, dest_v, idx_v, rows_v, gsem, wsem):
        base = (lax.axis_index("s") * SC_CORES + lax.axis_index("c")) * per_worker
        pltpu.sync_copy(dest_hbm, dest_v)
        lane = lax.iota(I32, lanes)

        @pl.loop(0, per_worker // lanes)
        def _(i):
            idx_v[pl.ds(i * lanes, lanes)] = lax.rem(base + i * lanes + lane, n)

        @plsc.parallel_loop(0, a // lanes, unroll=8)
        def _(i):
            local = dest_v[pl.ds(i * lanes, lanes)] - base
            mine = (local >= 0) & (local < per_worker)
            plsc.store_scatter(idx_v, [jnp.where(mine, local, 0)], lax.rem(i * lanes + lane, n), mask=mine)

        _sc_gather_rows(table_hbm, out_hbm, idx_v, rows_v, gsem, wsem, base, n_chunks)

    return dispatch(table, dest_flat)


def _expert_kernel(sched_ref, nb_ref, xs_ref, w1_hbm, w3_hbm, w2_hbm, y_ref,
                   w1_buf, w3_buf, w2_buf, w1_scr, w3_scr, w2_scr, sems, *, layer):
    b = pl.program_id(0)
    used = b < nb_ref[0]
    expert, slot, run_start, next_expert = (sched_ref[r, b] for r in range(4))

    def fetch(e, s):
        return [pltpu.make_async_copy(w_hbm.at[layer, e], buf.at[s], sems.at[s, j])
                for j, (w_hbm, buf) in enumerate(((w1_hbm, w1_buf), (w3_hbm, w3_buf), (w2_hbm, w2_buf)))]

    @pl.when(used & (run_start == 1))
    def _():
        @pl.when(b == 0)
        def _():
            for c in fetch(expert, slot):
                c.start()

        for c in fetch(expert, slot):
            c.wait()

        @pl.when(next_expert >= 0)
        def _():
            for c in fetch(next_expert, 1 - slot):
                c.start()

        w1_scr[...] = w1_buf[slot].astype(BF16)
        w3_scr[...] = w3_buf[slot].astype(BF16)
        w2_scr[...] = w2_buf[slot].astype(BF16)

    @pl.when(used)
    def _():
        x_lo, x_hi = _unpack_halves(xs_ref[...])
        xb = jnp.concatenate([x_lo.astype(BF16), x_hi.astype(BF16)], axis=1)
        h1 = _dot(xb, w1_scr[...])
        h3 = _dot(xb, w3_scr[...])
        hid = h1 * (1.0 / (1.0 + jnp.exp(-h1))) * h3
        y_ref[...] = _pack_halves(_dot(hid.astype(BF16), w2_scr[...]))

    @pl.when(jnp.logical_not(used))
    def _():
        y_ref[...] = jnp.zeros_like(y_ref)


def _expert_mlp(xs, schedule, n_used, layer, w1, w3, w2, n_blocks):
    d, de = w1.shape[2], w1.shape[3]

    def row_map(b, sched, nb):
        return (jnp.minimum(b, nb[0] - 1), 0)

    return pl.pallas_call(
        functools.partial(_expert_kernel, layer=layer),
        out_shape=jax.ShapeDtypeStruct((n_blocks * EXPERT_BLOCK, d // 2), I32),
        grid_spec=pltpu.PrefetchScalarGridSpec(
            num_scalar_prefetch=2,
            grid=(n_blocks,),
            in_specs=[pl.BlockSpec((EXPERT_BLOCK, d // 2), row_map), _HBM, _HBM, _HBM],
            out_specs=pl.BlockSpec((EXPERT_BLOCK, d // 2), lambda b, sched, nb: (b, 0)),
            scratch_shapes=[pltpu.VMEM((2, d, de), F32), pltpu.VMEM((2, d, de), F32), pltpu.VMEM((2, de, d), F32),
                            pltpu.VMEM((d, de), BF16), pltpu.VMEM((d, de), BF16), pltpu.VMEM((de, d), BF16),
                            pltpu.SemaphoreType.DMA((2, 3))],
        ),
        compiler_params=_params(),
        name="expert_mlp",
    )(schedule, n_used, xs, w1, w3, w2)


def _combine_kernel(x_ref, y0_ref, y1_ref, gate_ref, g_ref, b_ref, o_ref):
    o_ref[...] = _moe_output(x_ref, y0_ref, y1_ref, gate_ref, g_ref, b_ref)


def _combine(x, yg, gates, g, b):
    n, d = x.shape
    tiles = n // TM
    return pl.pallas_call(
        _combine_kernel,
        out_shape=jax.ShapeDtypeStruct((n, d), F32),
        grid=(tiles,),
        in_specs=[pl.BlockSpec((TM, d), lambda i: (i, 0)),
                  pl.BlockSpec((TM, d // 2), lambda i: (i, 0)),
                  pl.BlockSpec((TM, d // 2), lambda i: (i + tiles, 0)),
                  pl.BlockSpec((TM, TOP_K), lambda i: (i, 0)),
                  _const_spec((1, d)), _const_spec((1, d))],
        out_specs=pl.BlockSpec((TM, d), lambda i: (i, 0)),
        compiler_params=_params(),
        name="combine",
    )(x, yg, yg, gates, g.reshape(1, -1), b.reshape(1, -1))


def _moe_experts(x_packed, table, counts, layer, w1, w3, w2):
    n = x_packed.shape[0]
    n_blocks = (n * TOP_K + N_EXPERTS * (EXPERT_BLOCK - 1) + EXPERT_BLOCK - 1) // EXPERT_BLOCK
    experts = table[:TOP_K].astype(I32)
    ranks = table[TOP_K:2 * TOP_K].astype(I32)

    blocks_e = (counts + EXPERT_BLOCK - 1) // EXPERT_BLOCK
    blocks_end = jnp.cumsum(blocks_e)
    run_start = (blocks_end - blocks_e) * EXPERT_BLOCK
    n_used = blocks_end[-1:].astype(I32)
    block_ids = jnp.arange(n_blocks, dtype=I32)
    block_expert = jnp.minimum(jnp.sum(blocks_end[None, :] <= block_ids[:, None], axis=1),
                               N_EXPERTS - 1).astype(I32)
    run_start_flag = jnp.concatenate([jnp.ones((1,), I32),
                                      (block_expert[1:] != block_expert[:-1]).astype(I32)])
    slot = (jnp.cumsum(run_start_flag) - 1) % 2
    next_block = blocks_end[block_expert]
    next_expert = jnp.where(next_block < n_used[0],
                            block_expert[jnp.minimum(next_block, n_blocks - 1)], -1)
    schedule = jnp.stack([block_expert, slot, run_start_flag, next_expert]).astype(I32)
    expert_ids = jnp.arange(N_EXPERTS, dtype=I32)
    start_of = jnp.sum(jnp.where(experts[:, :, None] == expert_ids, run_start, 0), axis=-1)
    dest = (start_of + ranks).astype(I32).reshape(-1)
    xs = _sc_dispatch(x_packed, dest, n_blocks * EXPERT_BLOCK)
    y = _expert_mlp(xs, schedule, n_used, layer, w1, w3, w2, n_blocks)
    return _sc_gather(y, dest)


def kernel(x, mem, w_in_even, w_pool, pool_scale, ln_v_g, ln_v_b, w_spatial, b_spatial, w_out_even,
           w_in_odd, conv_w, conv_b, w_out_odd, wq_x, wk_x, wv_x, wo_x, ln_g, ln_b, wr_group,
           br_group, wr_expert, br_expert, w1, w3, w2):
    bsz, seq, d = x.shape
    assert seq % TM == 0 and d % LANES == 0
    mlen = mem.shape[1]
    k_all, v_all = _memory_kv(mem.reshape(bsz * mlen, d), wk_x, wv_x)
    k_all = k_all.reshape(DEPTH, bsz, mlen, d)
    v_all = v_all.reshape(DEPTH, bsz, mlen, d)
    h = x.reshape(bsz * seq, d)
    pending = None
    for l in range(DEPTH):
        i = l // 2
        if l % 2 == 0:
            h = _even_mixer(h, pending, seq, i, w_in_even, w_pool[i], pool_scale[i], ln_v_g[i], ln_v_b[i],
                            w_spatial[i], b_spatial[i], w_out_even, ln_g[l, 0], ln_b[l, 0])
        else:
            h = _odd_mixer(h, pending, seq, i, w_in_odd, conv_w[i], conv_b[i], w_out_odd,
                           ln_g[l, 0], ln_b[l, 0])
        router_w, router_b = _router_weights(wr_group[l], br_group[l], wr_expert[l], br_expert[l])
        h, hp, table, gates, counts = _cross_attn(h, seq, l, k_all, v_all, wq_x, wo_x,
                                                  ln_g[l, 1], ln_b[l, 1], router_w, router_b)
        yg = _moe_experts(hp, table, counts, l, w1, w3, w2)
        pending = (yg, gates, ln_g[l, 2], ln_b[l, 2])
    return _combine(h, *pending).reshape(bsz, seq, d)
```

```python
import functools
import math

import jax
import jax.numpy as jnp
from jax import lax
from jax.experimental import pallas as pl
from jax.experimental.pallas import tpu as pltpu
from jax.experimental.pallas import tpu_sc as plsc

F32 = jnp.float32
BF16 = jnp.bfloat16
I32 = jnp.int32

POOL_WINDOWS = (2, 4, 8, 16)
assert all(w & (w - 1) == 0 for w in POOL_WINDOWS)
N_SG_HEADS = 4
CHUNK = 128
CONV_WIDTH = 3
N_XHEADS = 4
N_GROUPS = 4
EXPERTS_PER_GROUP = 8
N_EXPERTS = N_GROUPS * EXPERTS_PER_GROUP
TOP_K = 2
DEPTH = 4
ALPHA = (2.0 * DEPTH) ** 0.25
LN_EPS = 1e-5

LANES = 128
SC_CORES = 2
SC_WORKERS = 32
SC_LANES = 16
VMEM_LIMIT = 56 * 1024 * 1024

SC_CHUNK = 64
TM = 1024
SUB_TILES = 2
ROUTE_ROWS = 8
POOL_HALO = 16
CONV_HALO = 8
EXPERT_UNIT = 256
ROUTE_COLS = 128
STAGE_COLS = 512

_NT = (((1,), (1,)), ((), ()))


def _dot(a, b):
    return jnp.dot(a, b, preferred_element_type=F32)


def _layer_norm(y, g, b):
    mu = jnp.mean(y, axis=-1, keepdims=True)
    yc = y - mu
    var = jnp.mean(yc * yc, axis=-1, keepdims=True)
    return yc * lax.rsqrt(var + LN_EPS) * g + b


def _gelu_tanh(x):
    c = math.sqrt(2.0 / math.pi)
    return 0.5 * x * (1.0 + jnp.tanh(c * (x + 0.044715 * (x * x * x))))


def _pack_halves(v):
    c = v.shape[1] // 2
    lo = pltpu.bitcast(v[:, :c].astype(BF16).astype(F32), jnp.uint32)
    hi = pltpu.bitcast(v[:, c:].astype(BF16).astype(F32), jnp.uint32)
    return pltpu.bitcast((hi & jnp.uint32(0xFFFF0000)) | (lo >> 16), I32)


def _unpack_halves(w):
    u = pltpu.bitcast(w, jnp.uint32)
    return pltpu.bitcast(u << 16, F32), pltpu.bitcast(u & jnp.uint32(0xFFFF0000), F32)


def _load_cast(w_hbm, w_scr, stage, sems):
    chunks = w_scr.shape[1] // STAGE_COLS

    def chunk_copy(c):
        return pltpu.make_async_copy(w_hbm.at[:, pl.ds(c * STAGE_COLS, STAGE_COLS)], stage.at[c % 2],
                                     sems.at[c % 2])

    chunk_copy(0).start()
    for c in range(chunks):
        if c + 1 < chunks:
            chunk_copy(c + 1).start()
        chunk_copy(c).wait()
        w_scr[:, c * STAGE_COLS:(c + 1) * STAGE_COLS] = stage[c % 2].astype(BF16)


def _stage_scratch(rows):
    return [pltpu.VMEM((2, rows, STAGE_COLS), F32), pltpu.SemaphoreType.DMA((2,))]


_HBM = pl.BlockSpec(memory_space=pl.ANY)


def _const_spec(shape):
    nd = len(shape)
    return pl.BlockSpec(shape, lambda i: (0,) * nd)


def _params():
    return pltpu.CompilerParams(dimension_semantics=("arbitrary",), vmem_limit_bytes=VMEM_LIMIT)


def _moe_output(x_ref, y0_ref, y1_ref, gate_ref, g_ref, b_ref):
    gates = gate_ref[...]
    g0, g1 = gates[:, 0:1], gates[:, 1:2]
    y0_lo, y0_hi = _unpack_halves(y0_ref[...])
    y1_lo, y1_hi = _unpack_halves(y1_ref[...])
    ff = jnp.concatenate([g0 * y0_lo + g1 * y1_lo, g0 * y0_hi + g1 * y1_hi], axis=1)
    return _layer_norm(ALPHA * x_ref[...] + ff, g_ref[...], b_ref[...])


def _mixer_input(src, pending):
    if not pending:
        x_ref, xh_ref = src
        return x_ref[...], xh_ref[...]
    x_ref, xh_ref, y0_ref, y0h_ref, y1_ref, y1h_ref, gate_ref, gateh_ref, g_ref, b_ref = src
    return (_moe_output(x_ref, y0_ref, y1_ref, gate_ref, g_ref, b_ref),
            _moe_output(xh_ref, y0h_ref, y1h_ref, gateh_ref, g_ref, b_ref))


def _mixer_sources(x, pending, halo):
    n, d = x.shape
    tiles = n // TM
    halo_blocks = TM // halo

    def halo_index(i):
        return jnp.maximum(i * halo_blocks - 1, 0)

    specs = [pl.BlockSpec((TM, d), lambda i: (i, 0)), pl.BlockSpec((halo, d), lambda i: (halo_index(i), 0))]
    args = [x, x]
    if pending is not None:
        yg, gates, g, b = pending
        specs += [pl.BlockSpec((TM, d // 2), lambda i: (i, 0)),
                  pl.BlockSpec((halo, d // 2), lambda i: (halo_index(i), 0)),
                  pl.BlockSpec((TM, d // 2), lambda i: (i + tiles, 0)),
                  pl.BlockSpec((halo, d // 2), lambda i: (halo_index(i) + tiles * halo_blocks, 0)),
                  pl.BlockSpec((TM, TOP_K), lambda i: (i, 0)),
                  pl.BlockSpec((halo, TOP_K), lambda i: (halo_index(i), 0)),
                  _const_spec((1, d)), _const_spec((1, d))]
        args += [yg, yg, yg, yg, gates, gates, g.reshape(1, -1), b.reshape(1, -1)]
    return specs, args


def _even_kernel(*refs, tiles_per_seq, layer, pending):
    n_src = 10 if pending else 2
    (win_hbm, wpool_ref, pscale_ref, lvg_ref, lvb_ref, ws_ref, bst_ref, wout_hbm, g_ref, b_ref, o_ref,
     a_scr, cat_scr, win_ref, wout_ref, stage, sems) = refs[n_src:]
    tm = o_ref.shape[0]
    d_pool = a_scr.shape[1]
    d_sg = lvg_ref.shape[1]
    pgd = d_pool // len(POOL_WINDOWS)
    hd_dim = d_sg // N_SG_HEADS
    seq_tile = pl.program_id(0) % tiles_per_seq

    @pl.when(pl.program_id(0) == 0)
    def _():
        _load_cast(win_hbm.at[layer], win_ref, stage, sems)
        _load_cast(wout_hbm.at[layer], wout_ref, stage, sems)

    x, xh = _mixer_input(refs[:n_src], pending)
    sub = tm // SUB_TILES

    ah = _dot(xh.astype(BF16), win_ref[:, :d_pool])
    a_scr[0:POOL_HALO, :] = jnp.where(seq_tile == 0, 0.0, ah)
    row = lax.broadcasted_iota(I32, (CHUNK, CHUNK), 0)
    col = lax.broadcasted_iota(I32, (CHUNK, CHUNK), 1)
    ws_masked = [jnp.where(row >= col, ws_ref[hd], 0.0).astype(BF16) for hd in range(N_SG_HEADS)]

    def in_proj(st):
        h = _dot(x[st * sub:(st + 1) * sub, :].astype(BF16), win_ref[...])
        a_scr[POOL_HALO + st * sub:POOL_HALO + (st + 1) * sub, :] = h[:, :d_pool]
        return h[:, d_pool:]

    def branches_out_proj(st, hz):
        base = st * sub
        pos = seq_tile * tm + base + lax.broadcasted_iota(I32, (sub, 1), 0)
        for g, w in enumerate(POOL_WINDOWS):
            cs = slice(g * pgd, (g + 1) * pgd)
            tok = a_scr[POOL_HALO + base:POOL_HALO + base + sub, cs]
            acc = a_scr[base:POOL_HALO + base + sub, cs]
            span = 1
            while span < w:
                acc = acc[span:, :] + acc[:-span, :]
                span *= 2
            acc = acc[acc.shape[0] - sub:, :]
            cnt = jnp.minimum(pos + 1, w).astype(F32)
            dev = acc * (1.0 / cnt) - tok
            yg = _dot(dev.astype(BF16), wpool_ref[g])
            cat_scr[base:base + sub, cs] = (yg * pscale_ref[:, cs]).astype(BF16)

        z = _gelu_tanh(hz)
        u = z[:, :d_sg]
        v = _layer_norm(z[:, d_sg:], lvg_ref[...], lvb_ref[...]).astype(BF16)
        for hd in range(N_SG_HEADS):
            hs = slice(hd * hd_dim, (hd + 1) * hd_dim)
            bcol = bst_ref[:, hd:hd + 1]
            for ck in range(sub // CHUNK):
                rs = slice(ck * CHUNK, (ck + 1) * CHUNK)
                sv = _dot(ws_masked[hd], v[rs, hs]) + bcol
                cat_scr[base + ck * CHUNK:base + (ck + 1) * CHUNK,
                        d_pool + hd * hd_dim:d_pool + (hd + 1) * hd_dim] = (u[rs, hs] * sv).astype(BF16)
        return _dot(cat_scr[base:base + sub, :], wout_ref[...])

    hzs = [in_proj(st) for st in range(SUB_TILES)]
    mixes = [branches_out_proj(st, hzs[st]) for st in range(SUB_TILES)]
    for st in range(SUB_TILES):
        rs = slice(st * sub, (st + 1) * sub)
        o_ref[rs, :] = _layer_norm(ALPHA * x[rs, :] + mixes[st], g_ref[...], b_ref[...])


def _even_mixer(x, pending, seq, layer, w_in, w_pool, pool_scale, ln_v_g, ln_v_b, w_spatial, b_spatial,
                w_out, g, b):
    n, d = x.shape
    d_in = w_in.shape[2]
    d_pool = pool_scale.shape[0]
    d_sg = ln_v_g.shape[0]
    kern = functools.partial(_even_kernel, tiles_per_seq=seq // TM, layer=layer, pending=pending is not None)
    src_specs, src_args = _mixer_sources(x, pending, POOL_HALO)
    return pl.pallas_call(
        kern,
        out_shape=jax.ShapeDtypeStruct((n, d), F32),
        grid=(n // TM,),
        in_specs=src_specs + [
            _HBM,
            _const_spec(w_pool.shape),
            _const_spec((1, d_pool)),
            _const_spec((1, d_sg)),
            _const_spec((1, d_sg)),
            _const_spec(w_spatial.shape),
            _const_spec((CHUNK, N_SG_HEADS)),
            _HBM,
            _const_spec((1, d)),
            _const_spec((1, d)),
        ],
        out_specs=pl.BlockSpec((TM, d), lambda i: (i, 0)),
        scratch_shapes=[pltpu.VMEM((POOL_HALO + TM, d_pool), F32), pltpu.VMEM((TM, d_pool + d_sg), BF16),
                        pltpu.VMEM((d, d_in), BF16), pltpu.VMEM((d_pool + d_sg, d), BF16)] + _stage_scratch(d),
        compiler_params=_params(),
        name="even_mixer",
    )(*src_args, w_in, w_pool.astype(BF16), pool_scale.reshape(1, -1), ln_v_g.reshape(1, -1),
      ln_v_b.reshape(1, -1), w_spatial, b_spatial.T, w_out, g.reshape(1, -1), b.reshape(1, -1))


def _odd_kernel(*refs, tiles_per_seq, layer, pending):
    n_src = 10 if pending else 2
    (win_hbm, cwt_ref, cb_ref, wout_hbm, g_ref, b_ref, o_ref, zc_scr,
     win_ref, wout_ref, stage, sems) = refs[n_src:]
    tm, d = o_ref.shape
    seq_tile = pl.program_id(0) % tiles_per_seq

    @pl.when(pl.program_id(0) == 0)
    def _():
        _load_cast(win_hbm.at[layer], win_ref, stage, sems)
        _load_cast(wout_hbm.at[layer], wout_ref, stage, sems)

    x, xh = _mixer_input(refs[:n_src], pending)
    hh = _dot(xh.astype(BF16), win_ref[:, d:])
    zc_scr[0:CONV_HALO, :] = jnp.where(seq_tile == 0, 0.0, hh[:, :d] * hh[:, d:])
    sub = tm // SUB_TILES

    def in_proj(st):
        xb = x[st * sub:(st + 1) * sub, :].astype(BF16)
        hc = _dot(xb, win_ref[:, d:2 * d])
        hz = _dot(xb, win_ref[:, 2 * d:])
        zc_scr[CONV_HALO + st * sub:CONV_HALO + (st + 1) * sub, :] = hc * hz
        return _dot(xb, win_ref[:, :d])

    def conv_out_proj(st, gate):
        conv = cb_ref[...]
        for j in range(CONV_WIDTH):
            off = CONV_HALO + st * sub - (CONV_WIDTH - 1) + j
            conv = conv + zc_scr[off:off + sub, :] * cwt_ref[j:j + 1, :]
        return _dot((gate * conv).astype(BF16), wout_ref[...])

    gates = [in_proj(st) for st in range(SUB_TILES)]
    ys = [conv_out_proj(st, gates[st]) for st in range(SUB_TILES)]
    for st in range(SUB_TILES):
        rs = slice(st * sub, (st + 1) * sub)
        o_ref[rs, :] = _layer_norm(ALPHA * x[rs, :] + ys[st], g_ref[...], b_ref[...])


def _odd_mixer(x, pending, seq, layer, w_in, conv_w, conv_b, w_out, g, b):
    n, d = x.shape
    kern = functools.partial(_odd_kernel, tiles_per_seq=seq // TM, layer=layer, pending=pending is not None)
    src_specs, src_args = _mixer_sources(x, pending, CONV_HALO)
    return pl.pallas_call(
        kern,
        out_shape=jax.ShapeDtypeStruct((n, d), F32),
        grid=(n // TM,),
        in_specs=src_specs + [
            _HBM,
            _const_spec((CONV_WIDTH, d)),
            _const_spec((1, d)),
            _HBM,
            _const_spec((1, d)),
            _const_spec((1, d)),
        ],
        out_specs=pl.BlockSpec((TM, d), lambda i: (i, 0)),
        scratch_shapes=[pltpu.VMEM((CONV_HALO + TM, d), F32),
                        pltpu.VMEM(w_in.shape[1:], BF16), pltpu.VMEM(w_out.shape[1:], BF16)] + _stage_scratch(d),
        compiler_params=_params(),
        name="odd_mixer",
    )(*src_args, w_in, conv_w.T, conv_b.reshape(1, -1), w_out, g.reshape(1, -1), b.reshape(1, -1))


def _kv_kernel(mem_ref, wk_ref, wv_ref, k_ref, v_ref):
    m = mem_ref[...].astype(BF16)
    k_ref[...] = _dot(m, wk_ref[...].astype(BF16)).astype(BF16)
    v_ref[...] = _dot(m, wv_ref[...].astype(BF16)).astype(BF16)


def _memory_kv(mem2d, wk, wv):
    nl, d, _ = wk.shape
    rows = mem2d.shape[0]
    out = jax.ShapeDtypeStruct((nl, rows, d), BF16)
    wspec = pl.BlockSpec((None, d, d), lambda l: (l, 0, 0))
    ospec = pl.BlockSpec((None, rows, d), lambda l: (l, 0, 0))
    return pl.pallas_call(
        _kv_kernel,
        out_shape=(out, out),
        grid=(nl,),
        in_specs=[_const_spec((rows, d)), wspec, wspec],
        out_specs=(ospec, ospec),
        compiler_params=_params(),
        name="memory_kv",
    )(mem2d, wk, wv)


def _attn_kernel(x_ref, k_ref, v_ref, wq_hbm, wo_hbm, g_ref, b_ref, wr_ref, br_ref,
                 o_ref, op_ref, rt_ref, rg_ref, cnt_ref, o_scr, carry_scr, wq_ref, wo_ref, stage, sems,
                 *, layer):
    tm, d = x_ref.shape
    hd_dim = d // N_XHEADS
    sub = tm // SUB_TILES

    @pl.when(pl.program_id(0) == 0)
    def _():
        carry_scr[...] = jnp.zeros_like(carry_scr)
        _load_cast(wq_hbm.at[layer], wq_ref, stage, sems)
        _load_cast(wo_hbm.at[layer], wo_ref, stage, sems)

    row_slices = [slice(st * sub, (st + 1) * sub) for st in range(SUB_TILES)]
    qs = [_dot(x_ref[rs, :].astype(BF16), wq_ref[...]) * (1.0 / math.sqrt(hd_dim)) for rs in row_slices]
    for rs, q in zip(row_slices, qs):
        for hd in range(N_XHEADS):
            hs = slice(hd * hd_dim, (hd + 1) * hd_dim)
            s = lax.dot_general(q[:, hs].astype(BF16), k_ref[:, hs], _NT, preferred_element_type=F32)
            p = jnp.exp(s - jnp.max(s, axis=-1, keepdims=True))
            p = p * (1.0 / jnp.sum(p, axis=-1, keepdims=True))
            o_scr[rs, hs] = _dot(p.astype(BF16), v_ref[:, hs]).astype(BF16)
    xas = [_dot(o_scr[rs, :], wo_ref[...]) for rs in row_slices]
    for rs, xa in zip(row_slices, xas):
        out = _layer_norm(ALPHA * x_ref[rs, :] + xa, g_ref[...], b_ref[...])
        o_ref[rs, :] = out
        op_ref[rs, :] = _pack_halves(out)
        table, gates = _route_rows(out, wr_ref, br_ref, carry_scr)
        rt_ref[:, rs] = table
        rg_ref[rs, :] = gates
    cnt_ref[...] = carry_scr[...].astype(I32)


def _cross_attn(x, seq, layer, k, v, wq, wo, g, b, router_w, router_b):
    n, d = x.shape
    m = k.shape[2]
    tiles_per_seq = seq // TM
    kvspec = pl.BlockSpec((None, None, m, d), lambda i: (layer, i // tiles_per_seq, 0, 0))
    out, packed, table, gates, cnt = pl.pallas_call(
        functools.partial(_attn_kernel, layer=layer),
        out_shape=(jax.ShapeDtypeStruct((n, d), F32), jax.ShapeDtypeStruct((n, d // 2), I32),
                   jax.ShapeDtypeStruct((ROUTE_ROWS, n), F32), jax.ShapeDtypeStruct((n, TOP_K), F32),
                   jax.ShapeDtypeStruct((1, ROUTE_COLS), I32)),
        grid=(n // TM,),
        in_specs=[
            pl.BlockSpec((TM, d), lambda i: (i, 0)),
            kvspec, kvspec,
            _HBM, _HBM,
            _const_spec((1, d)), _const_spec((1, d)),
            _const_spec((d, 2 * ROUTE_COLS)), _const_spec((1, ROUTE_COLS)),
        ],
        out_specs=(pl.BlockSpec((TM, d), lambda i: (i, 0)), pl.BlockSpec((TM, d // 2), lambda i: (i, 0)),
                   pl.BlockSpec((ROUTE_ROWS, TM), lambda i: (0, i)), pl.BlockSpec((TM, TOP_K), lambda i: (i, 0)),
                   _const_spec((1, ROUTE_COLS))),
        scratch_shapes=[pltpu.VMEM((TM, d), BF16), pltpu.VMEM((1, ROUTE_COLS), F32),
                        pltpu.VMEM((d, d), BF16), pltpu.VMEM((d, d), BF16)] + _stage_scratch(d),
        compiler_params=_params(),
        name="cross_attn",
    )(x, k, v, wq, wo, g.reshape(1, -1), b.reshape(1, -1), router_w, router_b)
    return out, packed, table, gates, cnt[0, N_GROUPS:N_GROUPS + N_EXPERTS]


def _route_rows(x, wcat_ref, bias_ref, carry_scr):
    tm = x.shape[0]

    xh = x.astype(BF16)
    xl = (x - xh.astype(F32)).astype(BF16)
    r1 = _dot(xh, wcat_ref[...])
    r2 = _dot(xl, wcat_ref[:, :ROUTE_COLS])
    logits = r1[:, :ROUTE_COLS] + r1[:, ROUTE_COLS:] + r2 + bias_ref[...]

    lane = lax.broadcasted_iota(I32, (tm, ROUTE_COLS), 1).astype(F32)
    neg = -jnp.inf

    def first_argmax(vals):
        mx = jnp.max(vals, axis=-1, keepdims=True)
        idx = jnp.min(jnp.where(vals == mx, lane, float(ROUTE_COLS)), axis=-1, keepdims=True)
        return mx, idx

    gl = jnp.where(lane < N_GROUPS, logits, neg)
    gmax, g_sel = first_argmax(gl)
    gate_g = 1.0 / jnp.sum(jnp.exp(gl - gmax), axis=-1, keepdims=True)

    lo = N_GROUPS + g_sel * EXPERTS_PER_GROUP
    el = jnp.where((lane >= lo) & (lane < lo + EXPERTS_PER_GROUP), logits, neg)
    m1, i1 = first_argmax(el)
    m2, i2 = first_argmax(jnp.where(lane == i1, neg, el))
    e21 = jnp.exp(m2 - m1)
    w1 = 1.0 / (1.0 + e21)
    w2 = e21 / (1.0 + e21)

    oh1 = lane == i1
    oh2 = lane == i2
    oh = (oh1 | oh2).astype(BF16)
    r = lax.broadcasted_iota(I32, (tm, tm), 0)
    c = lax.broadcasted_iota(I32, (tm, tm), 1)
    before = _dot((r > c).astype(BF16), oh) + carry_scr[...]
    rank1 = jnp.sum(jnp.where(oh1, before, 0.0), axis=-1, keepdims=True)
    rank2 = jnp.sum(jnp.where(oh2, before, 0.0), axis=-1, keepdims=True)
    carry_scr[...] += jnp.sum(oh.astype(F32), axis=0, keepdims=True)

    cols = jnp.where(lane == 0.0, i1 - N_GROUPS,
                     jnp.where(lane == 1.0, i2 - N_GROUPS,
                               jnp.where(lane == 2.0, rank1, jnp.where(lane == 3.0, rank2, 0.0))))
    table = jnp.transpose(cols)[:ROUTE_ROWS, :]
    l2 = lax.broadcasted_iota(I32, (tm, TOP_K), 1)
    gates = jnp.where(l2 == 0, gate_g * w1, gate_g * w2)
    return table, gates


def _router_weights(wr_g, br_g, wr_e, br_e):
    d = wr_g.shape[0]
    w = jnp.concatenate([wr_g, jnp.transpose(wr_e, (1, 0, 2)).reshape(d, N_EXPERTS)], axis=1)
    w = jnp.pad(w, ((0, 0), (0, ROUTE_COLS - w.shape[1])))
    w_hi = w.astype(BF16)
    w_lo = (w - w_hi.astype(F32)).astype(BF16)
    bias = jnp.pad(jnp.concatenate([br_g, br_e.reshape(-1)]), (0, ROUTE_COLS - N_GROUPS - N_EXPERTS))
    return jnp.concatenate([w_hi, w_lo], axis=1), bias.reshape(1, -1)


def _sc_worker_rows(rows):
    per_worker = rows // SC_WORKERS
    n_chunks = per_worker // SC_CHUNK
    assert per_worker * SC_WORKERS == rows and n_chunks * SC_CHUNK == per_worker and n_chunks % 2 == 0
    return per_worker, n_chunks


def _sc_gather_rows(table_hbm, out_hbm, idx_v, rows_v, gsem, wsem, base, n_chunks):
    def fetch(c, slot):
        off = pl.multiple_of(c * SC_CHUNK, SC_CHUNK)
        return pltpu.make_async_copy(table_hbm.at[idx_v.at[pl.ds(off, SC_CHUNK)]], rows_v.at[slot],
                                     gsem.at[slot])

    def put(c, slot):
        off = pl.multiple_of(c * SC_CHUNK, SC_CHUNK)
        return pltpu.make_async_copy(rows_v.at[slot], out_hbm.at[pl.ds(base + off, SC_CHUNK)], wsem.at[slot])

    fetch(0, 0).start()

    @pl.loop(0, n_chunks, step=2)
    def _(c0):
        for slot in range(2):
            c = c0 + slot

            @pl.when(c + 1 < n_chunks)
            def _():
                @pl.when(c >= 1)
                def _():
                    put(c - 1, 1 - slot).wait()
                fetch(c + 1, 1 - slot).start()

            fetch(c, slot).wait()
            put(c, slot).start()

    put(n_chunks - 2, 0).wait()
    put(n_chunks - 1, 1).wait()


def _sc_row_scratch(per_worker, d, dtype):
    return [pltpu.VMEM((per_worker,), I32), pltpu.VMEM((2, SC_CHUNK, d), dtype),
            pltpu.SemaphoreType.DMA((2,)), pltpu.SemaphoreType.DMA((2,))]


def _sc_gather(table, idx):
    b = idx.shape[0]
    d = table.shape[1]
    per_worker, n_chunks = _sc_worker_rows(b)
    mesh = plsc.VectorSubcoreMesh(core_axis_name="c", subcore_axis_name="s")

    @functools.partial(
        pl.kernel, mesh=mesh,
        out_type=jax.ShapeDtypeStruct((b, d), table.dtype),
        scratch_types=_sc_row_scratch(per_worker, d, table.dtype),
        name="sc_gather",
    )
    def gather(table_hbm, idx_hbm, out_hbm, idx_v, rows_v, gsem, wsem):
        base = (lax.axis_index("s") * SC_CORES + lax.axis_index("c")) * per_worker
        pltpu.sync_copy(idx_hbm.at[pl.ds(base, per_worker)], idx_v)
        _sc_gather_rows(table_hbm, out_hbm, idx_v, rows_v, gsem, wsem, base, n_chunks)

    return gather(table, idx)


def _sc_dispatch(table, dest_flat, rows):
    n, d = table.shape
    a = dest_flat.shape[0]
    lanes = SC_LANES
    per_worker, n_chunks = _sc_worker_rows(rows)
    assert a % lanes == 0 and per_worker % lanes == 0
    mesh = plsc.VectorSubcoreMesh(core_axis_name="c", subcore_axis_name="s")

    @functools.partial(
        pl.kernel, mesh=mesh,
        out_type=jax.ShapeDtypeStruct((rows, d), table.dtype),
        scratch_types=[pltpu.VMEM((a,), I32)] + _sc_row_scratch(per_worker, d, table.dtype),
        compiler_params=pltpu.CompilerParams(needs_layout_passes=False),
        name="sc_dispatch",
    )
    def dispatch(table_hbm, dest_hbm, out_hbm, dest_v, idx_v, rows_v, gsem, wsem):
        base = (lax.axis_index("s") * SC_CORES + lax.axis_index("c")) * per_worker
        pltpu.sync_copy(dest_hbm, dest_v)
        lane = lax.iota(I32, lanes)

        @pl.loop(0, per_worker // lanes)
        def _(i):
            idx_v[pl.ds(i * lanes, lanes)] = lax.rem(base + i * lanes + lane, n)

        @plsc.parallel_loop(0, a // lanes, unroll=8)
        def _(i):
            local = dest_v[pl.ds(i * lanes, lanes)] - base
            mine = (local >= 0) & (local < per_worker)
            plsc.store_scatter(idx_v, [jnp.where(mine, local, 0)], lax.rem(i * lanes + lane, n), mask=mine)

        _sc_gather_rows(table_hbm, out_hbm, idx_v, rows_v, gsem, wsem, base, n_chunks)

    return dispatch(table, dest_flat)


def _expert_kernel(sched_ref, ni_ref, xs_hbm, w1_hbm, w3_hbm, w2_hbm, y_hbm,
                   x_buf, y_buf, w1_buf, w3_buf, w2_buf, w1_scr, w3_scr, w2_scr, wsems, xsems, ysems,
                   *, layer):
    i = pl.program_id(0)
    n_items = ni_ref[0]
    expert, wslot, run_start, next_expert = (sched_ref[r, i] for r in range(4))
    slot = i % 2

    def for_units(item, fn):
        for units in (1, 2):
            @pl.when(sched_ref[5, item] == units)
            def _():
                fn(units * EXPERT_UNIT)

    def x_copy(item, s, rows):
        row0 = pl.multiple_of(sched_ref[4, item], EXPERT_UNIT)
        return pltpu.make_async_copy(xs_hbm.at[pl.ds(row0, rows)], x_buf.at[s, pl.ds(0, rows)], xsems.at[s])

    def y_copy(item, s, rows):
        row0 = pl.multiple_of(sched_ref[4, item], EXPERT_UNIT)
        return pltpu.make_async_copy(y_buf.at[s, pl.ds(0, rows)], y_hbm.at[pl.ds(row0, rows)], ysems.at[s])

    def fetch(e, s):
        return [pltpu.make_async_copy(w_hbm.at[layer, e], buf.at[s], wsems.at[s, j])
                for j, (w_hbm, buf) in enumerate(((w1_hbm, w1_buf), (w3_hbm, w3_buf), (w2_hbm, w2_buf)))]

    def mlp(rows):
        x_lo, x_hi = _unpack_halves(x_buf[slot, 0:rows, :])
        xb = jnp.concatenate([x_lo.astype(BF16), x_hi.astype(BF16)], axis=1)
        h1 = _dot(xb, w1_scr[...])
        h3 = _dot(xb, w3_scr[...])
        hid = h1 * (1.0 / (1.0 + jnp.exp(-h1))) * h3
        y_buf[slot, 0:rows, :] = _pack_halves(_dot(hid.astype(BF16), w2_scr[...]))
        y_copy(i, slot, rows).start()

    @pl.when(i < n_items)
    def _():
        @pl.when(i == 0)
        def _():
            for_units(0, lambda rows: x_copy(0, 0, rows).start())

        for_units(i, lambda rows: x_copy(i, slot, rows).wait())

        @pl.when(i + 1 < n_items)
        def _():
            for_units(i + 1, lambda rows: x_copy(i + 1, 1 - slot, rows).start())

        @pl.when(run_start == 1)
        def _():
            @pl.when(i == 0)
            def _():
                for c in fetch(expert, wslot):
                    c.start()

            for c in fetch(expert, wslot):
                c.wait()

            @pl.when(next_expert >= 0)
            def _():
                for c in fetch(next_expert, 1 - wslot):
                    c.start()

            w1_scr[...] = w1_buf[wslot].astype(BF16)
            w3_scr[...] = w3_buf[wslot].astype(BF16)
            w2_scr[...] = w2_buf[wslot].astype(BF16)

        @pl.when(i >= 2)
        def _():
            for_units(i - 2, lambda rows: y_copy(i - 2, slot, rows).wait())

        for_units(i, mlp)

        @pl.when(i == n_items - 1)
        def _():
            @pl.when(i >= 1)
            def _():
                for_units(i - 1, lambda rows: y_copy(i - 1, 1 - slot, rows).wait())

            for_units(i, lambda rows: y_copy(i, slot, rows).wait())
            y_buf[0, 0:EXPERT_UNIT, :] = jnp.zeros((EXPERT_UNIT, y_buf.shape[2]), y_buf.dtype)
            first_free = (sched_ref[4, i] + sched_ref[5, i] * EXPERT_UNIT) // EXPERT_UNIT

            def zero_unit(u, _):
                c = pltpu.make_async_copy(y_buf.at[0, pl.ds(0, EXPERT_UNIT)],
                                          y_hbm.at[pl.ds(pl.multiple_of(u * EXPERT_UNIT, EXPERT_UNIT), EXPERT_UNIT)],
                                          ysems.at[0])
                c.start()
                c.wait()
                return 0

            lax.fori_loop(first_free, y_hbm.shape[0] // EXPERT_UNIT, zero_unit, 0)


def _expert_mlp(xs, schedule, n_items, layer, w1, w3, w2):
    d, de = w1.shape[2], w1.shape[3]
    max_rows = 2 * EXPERT_UNIT
    return pl.pallas_call(
        functools.partial(_expert_kernel, layer=layer),
        out_shape=jax.ShapeDtypeStruct(xs.shape, I32),
        grid_spec=pltpu.PrefetchScalarGridSpec(
            num_scalar_prefetch=2,
            grid=(schedule.shape[1],),
            in_specs=[_HBM, _HBM, _HBM, _HBM],
            out_specs=_HBM,
            scratch_shapes=[pltpu.VMEM((2, max_rows, d // 2), I32), pltpu.VMEM((2, max_rows, d // 2), I32),
                            pltpu.VMEM((2, d, de), F32), pltpu.VMEM((2, d, de), F32), pltpu.VMEM((2, de, d), F32),
                            pltpu.VMEM((d, de), BF16), pltpu.VMEM((d, de), BF16), pltpu.VMEM((de, d), BF16),
                            pltpu.SemaphoreType.DMA((2, 3)), pltpu.SemaphoreType.DMA((2,)),
                            pltpu.SemaphoreType.DMA((2,))],
        ),
        compiler_params=_params(),
        name="expert_mlp",
    )(schedule, n_items, xs, w1, w3, w2)


def _combine_kernel(x_ref, y0_ref, y1_ref, gate_ref, g_ref, b_ref, o_ref):
    o_ref[...] = _moe_output(x_ref, y0_ref, y1_ref, gate_ref, g_ref, b_ref)


def _combine(x, yg, gates, g, b):
    n, d = x.shape
    tiles = n // TM
    return pl.pallas_call(
        _combine_kernel,
        out_shape=jax.ShapeDtypeStruct((n, d), F32),
        grid=(tiles,),
        in_specs=[pl.BlockSpec((TM, d), lambda i: (i, 0)),
                  pl.BlockSpec((TM, d // 2), lambda i: (i, 0)),
                  pl.BlockSpec((TM, d // 2), lambda i: (i + tiles, 0)),
                  pl.BlockSpec((TM, TOP_K), lambda i: (i, 0)),
                  _const_spec((1, d)), _const_spec((1, d))],
        out_specs=pl.BlockSpec((TM, d), lambda i: (i, 0)),
        compiler_params=_params(),
        name="combine",
    )(x, yg, yg, gates, g.reshape(1, -1), b.reshape(1, -1))


def _moe_experts(x_packed, table, counts, layer, w1, w3, w2):
    n = x_packed.shape[0]
    max_units = (n * TOP_K + N_EXPERTS * (EXPERT_UNIT - 1) + EXPERT_UNIT - 1) // EXPERT_UNIT
    max_items = (max_units + N_EXPERTS + 1) // 2
    experts = table[:TOP_K].astype(I32)
    ranks = table[TOP_K:2 * TOP_K].astype(I32)

    units_e = (counts + EXPERT_UNIT - 1) // EXPERT_UNIT
    units_start = jnp.cumsum(units_e) - units_e
    items_e = (units_e + 1) // 2
    items_end = jnp.cumsum(items_e)
    n_items = items_end[-1:].astype(I32)
    item_ids = jnp.arange(max_items, dtype=I32)
    item_expert = jnp.minimum(jnp.sum(items_end[None, :] <= item_ids[:, None], axis=1), N_EXPERTS - 1)
    within = item_ids - (items_end - items_e)[item_expert]
    item_row0 = (units_start[item_expert] + 2 * within) * EXPERT_UNIT
    item_units = jnp.clip(units_e[item_expert] - 2 * within, 1, 2)
    run_start_flag = ((within == 0) & (item_ids < n_items[0])).astype(I32)
    slot = (jnp.cumsum(run_start_flag) - 1) % 2
    next_item = items_end[item_expert]
    next_expert = jnp.where(next_item < n_items[0], item_expert[jnp.minimum(next_item, max_items - 1)], -1)
    schedule = jnp.stack([item_expert, slot, run_start_flag, next_expert, item_row0, item_units]).astype(I32)
    expert_ids = jnp.arange(N_EXPERTS, dtype=I32)
    start_of = jnp.sum(jnp.where(experts[:, :, None] == expert_ids, units_start * EXPERT_UNIT, 0), axis=-1)
    dest = (start_of + ranks).astype(I32).reshape(-1)
    xs = _sc_dispatch(x_packed, dest, max_units * EXPERT_UNIT)
    y = _expert_mlp(xs, schedule, n_items, layer, w1, w3, w2)
    return _sc_gather(y, dest)


def kernel(x, mem, w_in_even, w_pool, pool_scale, ln_v_g, ln_v_b, w_spatial, b_spatial, w_out_even,
           w_in_odd, conv_w, conv_b, w_out_odd, wq_x, wk_x, wv_x, wo_x, ln_g, ln_b, wr_group,
           br_group, wr_expert, br_expert, w1, w3, w2):
    bsz, seq, d = x.shape
    assert seq % TM == 0 and d % LANES == 0
    mlen = mem.shape[1]
    k_all, v_all = _memory_kv(mem.reshape(bsz * mlen, d), wk_x, wv_x)
    k_all = k_all.reshape(DEPTH, bsz, mlen, d)
    v_all = v_all.reshape(DEPTH, bsz, mlen, d)
    h = x.reshape(bsz * seq, d)
    pending = None
    for l in range(DEPTH):
        i = l // 2
        if l % 2 == 0:
            h = _even_mixer(h, pending, seq, i, w_in_even, w_pool[i], pool_scale[i], ln_v_g[i], ln_v_b[i],
                            w_spatial[i], b_spatial[i], w_out_even, ln_g[l, 0], ln_b[l, 0])
        else:
            h = _odd_mixer(h, pending, seq, i, w_in_odd, conv_w[i], conv_b[i], w_out_odd,
                           ln_g[l, 0], ln_b[l, 0])
        router_w, router_b = _router_weights(wr_group[l], br_group[l], wr_expert[l], br_expert[l])
        h, hp, table, gates, counts = _cross_attn(h, seq, l, k_all, v_all, wq_x, wo_x,
                                                  ln_g[l, 1], ln_b[l, 1], router_w, router_b)
        yg = _moe_experts(hp, table, counts, l, w1, w3, w2)
        pending = (yg, gates, ln_g[l, 2], ln_b[l, 2])
    return _combine(h, *pending).reshape(bsz, seq, d)
```

```python
import functools
import math

import jax
import jax.numpy as jnp
from jax import lax
from jax.experimental import pallas as pl
from jax.experimental.pallas import tpu as pltpu
from jax.experimental.pallas import tpu_sc as plsc

F32 = jnp.float32
BF16 = jnp.bfloat16
I32 = jnp.int32

POOL_WINDOWS = (2, 4, 8, 16)
assert all(w & (w - 1) == 0 for w in POOL_WINDOWS)
N_SG_HEADS = 4
CHUNK = 128
CONV_WIDTH = 3
N_XHEADS = 4
N_GROUPS = 4
EXPERTS_PER_GROUP = 8
N_EXPERTS = N_GROUPS * EXPERTS_PER_GROUP
TOP_K = 2
DEPTH = 4
ALPHA = (2.0 * DEPTH) ** 0.25
LN_EPS = 1e-5

LANES = 128
SC_CORES = 2
SC_WORKERS = 32
SC_LANES = 16
VMEM_LIMIT = 56 * 1024 * 1024

SC_CHUNK = 64
TM = 1024
SUB_TILES = 2
ROUTE_ROWS = 8
POOL_HALO = 16
CONV_HALO = 8
EXPERT_UNIT = 256
ROUTE_COLS = 128
STAGE_COLS = 512

_NT = (((1,), (1,)), ((), ()))


def _dot(a, b):
    return jnp.dot(a, b, preferred_element_type=F32)


def _layer_norm(y, g, b):
    mu = jnp.mean(y, axis=-1, keepdims=True)
    yc = y - mu
    var = jnp.mean(yc * yc, axis=-1, keepdims=True)
    return yc * lax.rsqrt(var + LN_EPS) * g + b


def _gelu_tanh(x):
    c = math.sqrt(2.0 / math.pi)
    return 0.5 * x * (1.0 + jnp.tanh(c * (x + 0.044715 * (x * x * x))))


def _pack_halves(v):
    c = v.shape[1] // 2
    lo = pltpu.bitcast(v[:, :c].astype(BF16).astype(F32), jnp.uint32)
    hi = pltpu.bitcast(v[:, c:].astype(BF16).astype(F32), jnp.uint32)
    return pltpu.bitcast((hi & jnp.uint32(0xFFFF0000)) | (lo >> 16), I32)


def _unpack_halves(w):
    u = pltpu.bitcast(w, jnp.uint32)
    return pltpu.bitcast(u << 16, F32), pltpu.bitcast(u & jnp.uint32(0xFFFF0000), F32)


def _load_cast(w_hbm, w_scr, stage, sems):
    chunks = w_scr.shape[1] // STAGE_COLS

    def chunk_copy(c):
        return pltpu.make_async_copy(w_hbm.at[:, pl.ds(c * STAGE_COLS, STAGE_COLS)], stage.at[c % 2],
                                     sems.at[c % 2])

    chunk_copy(0).start()
    for c in range(chunks):
        if c + 1 < chunks:
            chunk_copy(c + 1).start()
        chunk_copy(c).wait()
        w_scr[:, c * STAGE_COLS:(c + 1) * STAGE_COLS] = stage[c % 2].astype(BF16)


def _stage_scratch(rows):
    return [pltpu.VMEM((2, rows, STAGE_COLS), F32), pltpu.SemaphoreType.DMA((2,))]


_HBM = pl.BlockSpec(memory_space=pl.ANY)


def _const_spec(shape):
    nd = len(shape)
    return pl.BlockSpec(shape, lambda i: (0,) * nd)


def _params():
    return pltpu.CompilerParams(dimension_semantics=("arbitrary",), vmem_limit_bytes=VMEM_LIMIT)


def _moe_output(x_ref, y0_ref, y1_ref, gate_ref, g_ref, b_ref):
    gates = gate_ref[...]
    g0, g1 = gates[:, 0:1], gates[:, 1:2]
    y0_lo, y0_hi = _unpack_halves(y0_ref[...])
    y1_lo, y1_hi = _unpack_halves(y1_ref[...])
    ff = jnp.concatenate([g0 * y0_lo + g1 * y1_lo, g0 * y0_hi + g1 * y1_hi], axis=1)
    return _layer_norm(ALPHA * x_ref[...] + ff, g_ref[...], b_ref[...])


def _mixer_input(src, pending):
    if not pending:
        x_ref, xh_ref = src
        return x_ref[...], xh_ref[...]
    x_ref, xh_ref, y0_ref, y0h_ref, y1_ref, y1h_ref, gate_ref, gateh_ref, g_ref, b_ref = src
    return (_moe_output(x_ref, y0_ref, y1_ref, gate_ref, g_ref, b_ref),
            _moe_output(xh_ref, y0h_ref, y1h_ref, gateh_ref, g_ref, b_ref))


def _mixer_sources(x, pending, halo):
    n, d = x.shape
    tiles = n // TM
    halo_blocks = TM // halo

    def halo_index(i):
        return jnp.maximum(i * halo_blocks - 1, 0)

    specs = [pl.BlockSpec((TM, d), lambda i: (i, 0)), pl.BlockSpec((halo, d), lambda i: (halo_index(i), 0))]
    args = [x, x]
    if pending is not None:
        yg, gates, g, b = pending
        specs += [pl.BlockSpec((TM, d // 2), lambda i: (i, 0)),
                  pl.BlockSpec((halo, d // 2), lambda i: (halo_index(i), 0)),
                  pl.BlockSpec((TM, d // 2), lambda i: (i + tiles, 0)),
                  pl.BlockSpec((halo, d // 2), lambda i: (halo_index(i) + tiles * halo_blocks, 0)),
                  pl.BlockSpec((TM, TOP_K), lambda i: (i, 0)),
                  pl.BlockSpec((halo, TOP_K), lambda i: (halo_index(i), 0)),
                  _const_spec((1, d)), _const_spec((1, d))]
        args += [yg, yg, yg, yg, gates, gates, g.reshape(1, -1), b.reshape(1, -1)]
    return specs, args


def _even_kernel(*refs, tiles_per_seq, layer, pending):
    n_src = 10 if pending else 2
    (win_hbm, wpool_ref, pscale_ref, lvg_ref, lvb_ref, ws_ref, bst_ref, wout_hbm, g_ref, b_ref, o_ref,
     a_scr, cat_scr, win_ref, wout_ref, stage, sems) = refs[n_src:]
    tm = o_ref.shape[0]
    d_pool = a_scr.shape[1]
    d_sg = lvg_ref.shape[1]
    pgd = d_pool // len(POOL_WINDOWS)
    hd_dim = d_sg // N_SG_HEADS
    seq_tile = pl.program_id(0) % tiles_per_seq

    @pl.when(pl.program_id(0) == 0)
    def _():
        _load_cast(win_hbm.at[layer], win_ref, stage, sems)
        _load_cast(wout_hbm.at[layer], wout_ref, stage, sems)

    x, xh = _mixer_input(refs[:n_src], pending)
    sub = tm // SUB_TILES

    ah = _dot(xh.astype(BF16), win_ref[:, :d_pool])
    a_scr[0:POOL_HALO, :] = jnp.where(seq_tile == 0, 0.0, ah)
    row = lax.broadcasted_iota(I32, (CHUNK, CHUNK), 0)
    col = lax.broadcasted_iota(I32, (CHUNK, CHUNK), 1)
    ws_masked = [jnp.where(row >= col, ws_ref[hd], 0.0).astype(BF16) for hd in range(N_SG_HEADS)]

    def in_proj(st):
        h = _dot(x[st * sub:(st + 1) * sub, :].astype(BF16), win_ref[...])
        a_scr[POOL_HALO + st * sub:POOL_HALO + (st + 1) * sub, :] = h[:, :d_pool]
        return h[:, d_pool:]

    def branches_out_proj(st, hz):
        base = st * sub
        pos = seq_tile * tm + base + lax.broadcasted_iota(I32, (sub, 1), 0)
        for g, w in enumerate(POOL_WINDOWS):
            cs = slice(g * pgd, (g + 1) * pgd)
            tok = a_scr[POOL_HALO + base:POOL_HALO + base + sub, cs]
            acc = a_scr[base:POOL_HALO + base + sub, cs]
            span = 1
            while span < w:
                acc = acc[span:, :] + acc[:-span, :]
                span *= 2
            acc = acc[acc.shape[0] - sub:, :]
            cnt = jnp.minimum(pos + 1, w).astype(F32)
            dev = acc * (1.0 / cnt) - tok
            yg = _dot(dev.astype(BF16), wpool_ref[g])
            cat_scr[base:base + sub, cs] = (yg * pscale_ref[:, cs]).astype(BF16)

        z = _gelu_tanh(hz)
        u = z[:, :d_sg]
        v = _layer_norm(z[:, d_sg:], lvg_ref[...], lvb_ref[...]).astype(BF16)
        for hd in range(N_SG_HEADS):
            hs = slice(hd * hd_dim, (hd + 1) * hd_dim)
            bcol = bst_ref[:, hd:hd + 1]
            for ck in range(sub // CHUNK):
                rs = slice(ck * CHUNK, (ck + 1) * CHUNK)
                sv = _dot(ws_masked[hd], v[rs, hs]) + bcol
                cat_scr[base + ck * CHUNK:base + (ck + 1) * CHUNK,
                        d_pool + hd * hd_dim:d_pool + (hd + 1) * hd_dim] = (u[rs, hs] * sv).astype(BF16)
        return _dot(cat_scr[base:base + sub, :], wout_ref[...])

    hzs = [in_proj(st) for st in range(SUB_TILES)]
    mixes = [branches_out_proj(st, hzs[st]) for st in range(SUB_TILES)]
    for st in range(SUB_TILES):
        rs = slice(st * sub, (st + 1) * sub)
        o_ref[rs, :] = _layer_norm(ALPHA * x[rs, :] + mixes[st], g_ref[...], b_ref[...])


def _even_mixer(x, pending, seq, layer, w_in, w_pool, pool_scale, ln_v_g, ln_v_b, w_spatial, b_spatial,
                w_out, g, b):
    n, d = x.shape
    d_in = w_in.shape[2]
    d_pool = pool_scale.shape[0]
    d_sg = ln_v_g.shape[0]
    kern = functools.partial(_even_kernel, tiles_per_seq=seq // TM, layer=layer, pending=pending is not None)
    src_specs, src_args = _mixer_sources(x, pending, POOL_HALO)
    return pl.pallas_call(
        kern,
        out_shape=jax.ShapeDtypeStruct((n, d), F32),
        grid=(n // TM,),
        in_specs=src_specs + [
            _HBM,
            _const_spec(w_pool.shape),
            _const_spec((1, d_pool)),
            _const_spec((1, d_sg)),
            _const_spec((1, d_sg)),
            _const_spec(w_spatial.shape),
            _const_spec((CHUNK, N_SG_HEADS)),
            _HBM,
            _const_spec((1, d)),
            _const_spec((1, d)),
        ],
        out_specs=pl.BlockSpec((TM, d), lambda i: (i, 0)),
        scratch_shapes=[pltpu.VMEM((POOL_HALO + TM, d_pool), F32), pltpu.VMEM((TM, d_pool + d_sg), BF16),
                        pltpu.VMEM((d, d_in), BF16), pltpu.VMEM((d_pool + d_sg, d), BF16)] + _stage_scratch(d),
        compiler_params=_params(),
        name="even_mixer",
    )(*src_args, w_in, w_pool.astype(BF16), pool_scale.reshape(1, -1), ln_v_g.reshape(1, -1),
      ln_v_b.reshape(1, -1), w_spatial, b_spatial.T, w_out, g.reshape(1, -1), b.reshape(1, -1))


def _odd_kernel(*refs, tiles_per_seq, layer, pending):
    n_src = 10 if pending else 2
    (win_hbm, cwt_ref, cb_ref, wout_hbm, g_ref, b_ref, o_ref, zc_scr,
     win_ref, wout_ref, stage, sems) = refs[n_src:]
    tm, d = o_ref.shape
    seq_tile = pl.program_id(0) % tiles_per_seq

    @pl.when(pl.program_id(0) == 0)
    def _():
        _load_cast(win_hbm.at[layer], win_ref, stage, sems)
        _load_cast(wout_hbm.at[layer], wout_ref, stage, sems)

    x, xh = _mixer_input(refs[:n_src], pending)
    hh = _dot(xh.astype(BF16), win_ref[:, d:])
    zc_scr[0:CONV_HALO, :] = jnp.where(seq_tile == 0, 0.0, hh[:, :d] * hh[:, d:])
    sub = tm // SUB_TILES

    def in_proj(st):
        xb = x[st * sub:(st + 1) * sub, :].astype(BF16)
        hc = _dot(xb, win_ref[:, d:2 * d])
        hz = _dot(xb, win_ref[:, 2 * d:])
        zc_scr[CONV_HALO + st * sub:CONV_HALO + (st + 1) * sub, :] = hc * hz
        return _dot(xb, win_ref[:, :d])

    def conv_out_proj(st, gate):
        conv = cb_ref[...]
        for j in range(CONV_WIDTH):
            off = CONV_HALO + st * sub - (CONV_WIDTH - 1) + j
            conv = conv + zc_scr[off:off + sub, :] * cwt_ref[j:j + 1, :]
        return _dot((gate * conv).astype(BF16), wout_ref[...])

    gates = [in_proj(st) for st in range(SUB_TILES)]
    ys = [conv_out_proj(st, gates[st]) for st in range(SUB_TILES)]
    for st in range(SUB_TILES):
        rs = slice(st * sub, (st + 1) * sub)
        o_ref[rs, :] = _layer_norm(ALPHA * x[rs, :] + ys[st], g_ref[...], b_ref[...])


def _odd_mixer(x, pending, seq, layer, w_in, conv_w, conv_b, w_out, g, b):
    n, d = x.shape
    kern = functools.partial(_odd_kernel, tiles_per_seq=seq // TM, layer=layer, pending=pending is not None)
    src_specs, src_args = _mixer_sources(x, pending, CONV_HALO)
    return pl.pallas_call(
        kern,
        out_shape=jax.ShapeDtypeStruct((n, d), F32),
        grid=(n // TM,),
        in_specs=src_specs + [
            _HBM,
            _const_spec((CONV_WIDTH, d)),
            _const_spec((1, d)),
            _HBM,
            _const_spec((1, d)),
            _const_spec((1, d)),
        ],
        out_specs=pl.BlockSpec((TM, d), lambda i: (i, 0)),
        scratch_shapes=[pltpu.VMEM((CONV_HALO + TM, d), F32),
                        pltpu.VMEM(w_in.shape[1:], BF16), pltpu.VMEM(w_out.shape[1:], BF16)] + _stage_scratch(d),
        compiler_params=_params(),
        name="odd_mixer",
    )(*src_args, w_in, conv_w.T, conv_b.reshape(1, -1), w_out, g.reshape(1, -1), b.reshape(1, -1))


def _kv_kernel(mem_ref, wk_ref, wv_ref, k_ref, v_ref):
    m = mem_ref[...].astype(BF16)
    k_ref[...] = _dot(m, wk_ref[...].astype(BF16)).astype(BF16)
    v_ref[...] = _dot(m, wv_ref[...].astype(BF16)).astype(BF16)


def _memory_kv(mem2d, wk, wv):
    nl, d, _ = wk.shape
    rows = mem2d.shape[0]
    out = jax.ShapeDtypeStruct((nl, rows, d), BF16)
    wspec = pl.BlockSpec((None, d, d), lambda l: (l, 0, 0))
    ospec = pl.BlockSpec((None, rows, d), lambda l: (l, 0, 0))
    return pl.pallas_call(
        _kv_kernel,
        out_shape=(out, out),
        grid=(nl,),
        in_specs=[_const_spec((rows, d)), wspec, wspec],
        out_specs=(ospec, ospec),
        compiler_params=_params(),
        name="memory_kv",
    )(mem2d, wk, wv)


def _attn_kernel(x_ref, k_ref, v_ref, wq_hbm, wo_hbm, g_ref, b_ref, wr_ref, br_ref,
                 o_ref, op_ref, rt_ref, rg_ref, cnt_ref, o_scr, carry_scr, wq_ref, wo_ref, stage, sems,
                 *, layer):
    tm, d = x_ref.shape
    hd_dim = d // N_XHEADS
    sub = tm // SUB_TILES

    @pl.when(pl.program_id(0) == 0)
    def _():
        carry_scr[...] = jnp.zeros_like(carry_scr)
        _load_cast(wq_hbm.at[layer], wq_ref, stage, sems)
        _load_cast(wo_hbm.at[layer], wo_ref, stage, sems)

    row_slices = [slice(st * sub, (st + 1) * sub) for st in range(SUB_TILES)]
    qs = [_dot(x_ref[rs, :].astype(BF16), wq_ref[...]) * (1.0 / math.sqrt(hd_dim)) for rs in row_slices]
    for rs, q in zip(row_slices, qs):
        for hd in range(N_XHEADS):
            hs = slice(hd * hd_dim, (hd + 1) * hd_dim)
            s = lax.dot_general(q[:, hs].astype(BF16), k_ref[:, hs], _NT, preferred_element_type=F32)
            p = jnp.exp(s - jnp.max(s, axis=-1, keepdims=True))
            p = p * (1.0 / jnp.sum(p, axis=-1, keepdims=True))
            o_scr[rs, hs] = _dot(p.astype(BF16), v_ref[:, hs]).astype(BF16)
    xas = [_dot(o_scr[rs, :], wo_ref[...]) for rs in row_slices]
    for rs, xa in zip(row_slices, xas):
        out = _layer_norm(ALPHA * x_ref[rs, :] + xa, g_ref[...], b_ref[...])
        o_ref[rs, :] = out
        op_ref[rs, :] = _pack_halves(out)
        table, gates = _route_rows(out, wr_ref, br_ref, carry_scr)
        rt_ref[:, rs] = table
        rg_ref[rs, :] = gates
    cnt_ref[...] = carry_scr[...].astype(I32)


def _cross_attn(x, seq, layer, k, v, wq, wo, g, b, router_w, router_b):
    n, d = x.shape
    m = k.shape[2]
    tiles_per_seq = seq // TM
    kvspec = pl.BlockSpec((None, None, m, d), lambda i: (layer, i // tiles_per_seq, 0, 0))
    out, packed, table, gates, cnt = pl.pallas_call(
        functools.partial(_attn_kernel, layer=layer),
        out_shape=(jax.ShapeDtypeStruct((n, d), F32), jax.ShapeDtypeStruct((n, d // 2), I32),
                   jax.ShapeDtypeStruct((ROUTE_ROWS, n), F32), jax.ShapeDtypeStruct((n, TOP_K), F32),
                   jax.ShapeDtypeStruct((1, ROUTE_COLS), I32)),
        grid=(n // TM,),
        in_specs=[
            pl.BlockSpec((TM, d), lambda i: (i, 0)),
            kvspec, kvspec,
            _HBM, _HBM,
            _const_spec((1, d)), _const_spec((1, d)),
            _const_spec((d, 2 * ROUTE_COLS)), _const_spec((1, ROUTE_COLS)),
        ],
        out_specs=(pl.BlockSpec((TM, d), lambda i: (i, 0)), pl.BlockSpec((TM, d // 2), lambda i: (i, 0)),
                   pl.BlockSpec((ROUTE_ROWS, TM), lambda i: (0, i)), pl.BlockSpec((TM, TOP_K), lambda i: (i, 0)),
                   _const_spec((1, ROUTE_COLS))),
        scratch_shapes=[pltpu.VMEM((TM, d), BF16), pltpu.VMEM((1, ROUTE_COLS), F32),
                        pltpu.VMEM((d, d), BF16), pltpu.VMEM((d, d), BF16)] + _stage_scratch(d),
        compiler_params=_params(),
        name="cross_attn",
    )(x, k, v, wq, wo, g.reshape(1, -1), b.reshape(1, -1), router_w, router_b)
    return out, packed, table, gates, cnt[0, N_GROUPS:N_GROUPS + N_EXPERTS]


def _route_rows(x, wcat_ref, bias_ref, carry_scr):
    tm = x.shape[0]

    xh = x.astype(BF16)
    xl = (x - xh.astype(F32)).astype(BF16)
    r1 = _dot(xh, wcat_ref[...])
    r2 = _dot(xl, wcat_ref[:, :ROUTE_COLS])
    logits = r1[:, :ROUTE_COLS] + r1[:, ROUTE_COLS:] + r2 + bias_ref[...]

    lane = lax.broadcasted_iota(I32, (tm, ROUTE_COLS), 1).astype(F32)
    neg = -jnp.inf

    def first_argmax(vals):
        mx = jnp.max(vals, axis=-1, keepdims=True)
        idx = jnp.min(jnp.where(vals == mx, lane, float(ROUTE_COLS)), axis=-1, keepdims=True)
        return mx, idx

    gl = jnp.where(lane < N_GROUPS, logits, neg)
    gmax, g_sel = first_argmax(gl)
    gate_g = 1.0 / jnp.sum(jnp.exp(gl - gmax), axis=-1, keepdims=True)

    lo = N_GROUPS + g_sel * EXPERTS_PER_GROUP
    el = jnp.where((lane >= lo) & (lane < lo + EXPERTS_PER_GROUP), logits, neg)
    m1, i1 = first_argmax(el)
    m2, i2 = first_argmax(jnp.where(lane == i1, neg, el))
    e21 = jnp.exp(m2 - m1)
    w1 = 1.0 / (1.0 + e21)
    w2 = e21 / (1.0 + e21)

    oh1 = lane == i1
    oh2 = lane == i2
    oh = (oh1 | oh2).astype(BF16)
    r = lax.broadcasted_iota(I32, (tm, tm), 0)
    c = lax.broadcasted_iota(I32, (tm, tm), 1)
    before = _dot((r > c).astype(BF16), oh) + carry_scr[...]
    rank1 = jnp.sum(jnp.where(oh1, before, 0.0), axis=-1, keepdims=True)
    rank2 = jnp.sum(jnp.where(oh2, before, 0.0), axis=-1, keepdims=True)
    carry_scr[...] += jnp.sum(oh.astype(F32), axis=0, keepdims=True)

    cols = jnp.where(lane == 0.0, i1 - N_GROUPS,
                     jnp.where(lane == 1.0, i2 - N_GROUPS,
                               jnp.where(lane == 2.0, rank1, jnp.where(lane == 3.0, rank2, 0.0))))
    table = jnp.transpose(cols)[:ROUTE_ROWS, :]
    l2 = lax.broadcasted_iota(I32, (tm, TOP_K), 1)
    gates = jnp.where(l2 == 0, gate_g * w1, gate_g * w2)
    return table, gates


def _router_weights(wr_g, br_g, wr_e, br_e):
    d = wr_g.shape[0]
    w = jnp.concatenate([wr_g, jnp.transpose(wr_e, (1, 0, 2)).reshape(d, N_EXPERTS)], axis=1)
    w = jnp.pad(w, ((0, 0), (0, ROUTE_COLS - w.shape[1])))
    w_hi = w.astype(BF16)
    w_lo = (w - w_hi.astype(F32)).astype(BF16)
    bias = jnp.pad(jnp.concatenate([br_g, br_e.reshape(-1)]), (0, ROUTE_COLS - N_GROUPS - N_EXPERTS))
    return jnp.concatenate([w_hi, w_lo], axis=1), bias.reshape(1, -1)


def _sc_worker_rows(rows):
    per_worker = rows // SC_WORKERS
    n_chunks = per_worker // SC_CHUNK
    assert per_worker * SC_WORKERS == rows and n_chunks * SC_CHUNK == per_worker and n_chunks % 2 == 0
    return per_worker, n_chunks


def _sc_gather_rows(table_hbm, out_hbm, idx_v, rows_v, gsem, wsem, base, n_chunks):
    def fetch(c, slot):
        off = pl.multiple_of(c * SC_CHUNK, SC_CHUNK)
        return pltpu.make_async_copy(table_hbm.at[idx_v.at[pl.ds(off, SC_CHUNK)]], rows_v.at[slot],
                                     gsem.at[slot])

    def put(c, slot):
        off = pl.multiple_of(c * SC_CHUNK, SC_CHUNK)
        return pltpu.make_async_copy(rows_v.at[slot], out_hbm.at[pl.ds(base + off, SC_CHUNK)], wsem.at[slot])

    fetch(0, 0).start()

    @pl.loop(0, n_chunks, step=2)
    def _(c0):
        for slot in range(2):
            c = c0 + slot

            @pl.when(c + 1 < n_chunks)
            def _():
                @pl.when(c >= 1)
                def _():
                    put(c - 1, 1 - slot).wait()
                fetch(c + 1, 1 - slot).start()

            fetch(c, slot).wait()
            put(c, slot).start()

    put(n_chunks - 2, 0).wait()
    put(n_chunks - 1, 1).wait()


def _sc_row_scratch(per_worker, d, dtype):
    return [pltpu.VMEM((per_worker,), I32), pltpu.VMEM((2, SC_CHUNK, d), dtype),
            pltpu.SemaphoreType.DMA((2,)), pltpu.SemaphoreType.DMA((2,))]


def _sc_gather(table, idx):
    b = idx.shape[0]
    d = table.shape[1]
    per_worker, n_chunks = _sc_worker_rows(b)
    mesh = plsc.VectorSubcoreMesh(core_axis_name="c", subcore_axis_name="s")

    @functools.partial(
        pl.kernel, mesh=mesh,
        out_type=jax.ShapeDtypeStruct((b, d), table.dtype),
        scratch_types=_sc_row_scratch(per_worker, d, table.dtype),
        name="sc_gather",
    )
    def gather(table_hbm, idx_hbm, out_hbm, idx_v, rows_v, gsem, wsem):
        base = (lax.axis_index("s") * SC_CORES + lax.axis_index("c")) * per_worker
        pltpu.sync_copy(idx_hbm.at[pl.ds(base, per_worker)], idx_v)
        _sc_gather_rows(table_hbm, out_hbm, idx_v, rows_v, gsem, wsem, base, n_chunks)

    return gather(table, idx)


def _sc_dispatch(table, dest_flat, rows):
    n, d = table.shape
    a = dest_flat.shape[0]
    lanes = SC_LANES
    per_worker, n_chunks = _sc_worker_rows(rows)
    assert a % lanes == 0 and per_worker % lanes == 0
    mesh = plsc.VectorSubcoreMesh(core_axis_name="c", subcore_axis_name="s")

    @functools.partial(
        pl.kernel, mesh=mesh,
        out_type=jax.ShapeDtypeStruct((rows, d), table.dtype),
        scratch_types=[pltpu.VMEM((a,), I32)] + _sc_row_scratch(per_worker, d, table.dtype),
        compiler_params=pltpu.CompilerParams(needs_layout_passes=False),
        name="sc_dispatch",
    )
    def dispatch(table_hbm, dest_hbm, out_hbm, dest_v, idx_v, rows_v, gsem, wsem):
        base = (lax.axis_index("s") * SC_CORES + lax.axis_index("c")) * per_worker
        pltpu.sync_copy(dest_hbm, dest_v)
        lane = lax.iota(I32, lanes)

        @pl.loop(0, per_worker // lanes)
        def _(i):
            idx_v[pl.ds(i * lanes, lanes)] = lax.rem(base + i * lanes + lane, n)

        @plsc.parallel_loop(0, a // lanes, unroll=8)
        def _(i):
            local = dest_v[pl.ds(i * lanes, lanes)] - base
            mine = (local >= 0) & (local < per_worker)
            plsc.store_scatter(idx_v, [jnp.where(mine, local, 0)], lax.rem(i * lanes + lane, n), mask=mine)

        _sc_gather_rows(table_hbm, out_hbm, idx_v, rows_v, gsem, wsem, base, n_chunks)

    return dispatch(table, dest_flat)


def _expert_kernel(sched_ref, ni_ref, xs_hbm, w1_hbm, w3_hbm, w2_hbm, y_hbm,
                   x_buf, y_buf, w1_buf, w3_buf, w2_buf, w1_scr, w3_scr, w2_scr, wsems, xsems, ysems,
                   *, layer):
    i = pl.program_id(0)
    n_items = ni_ref[0]
    expert, wslot, run_start, next_expert = (sched_ref[r, i] for r in range(4))
    slot = i % 2

    def for_units(item, fn):
        for units in (1, 2):
            @pl.when(sched_ref[5, item] == units)
            def _():
                fn(units * EXPERT_UNIT)

    def x_copy(item, s, rows):
        row0 = pl.multiple_of(sched_ref[4, item], EXPERT_UNIT)
        return pltpu.make_async_copy(xs_hbm.at[pl.ds(row0, rows)], x_buf.at[s, pl.ds(0, rows)], xsems.at[s])

    def y_copy(item, s, rows):
        row0 = pl.multiple_of(sched_ref[4, item], EXPERT_UNIT)
        return pltpu.make_async_copy(y_buf.at[s, pl.ds(0, rows)], y_hbm.at[pl.ds(row0, rows)], ysems.at[s])

    def fetch(e, s):
        return [pltpu.make_async_copy(w_hbm.at[layer, e], buf.at[s], wsems.at[s, j])
                for j, (w_hbm, buf) in enumerate(((w1_hbm, w1_buf), (w3_hbm, w3_buf), (w2_hbm, w2_buf)))]

    def mlp(rows):
        x_lo, x_hi = _unpack_halves(x_buf[slot, 0:rows, :])
        xb = jnp.concatenate([x_lo.astype(BF16), x_hi.astype(BF16)], axis=1)
        h1 = _dot(xb, w1_scr[...])
        h3 = _dot(xb, w3_scr[...])
        hid = h1 * (1.0 / (1.0 + jnp.exp(-h1))) * h3
        y_buf[slot, 0:rows, :] = _pack_halves(_dot(hid.astype(BF16), w2_scr[...]))
        y_copy(i, slot, rows).start()

    @pl.when(i < n_items)
    def _():
        @pl.when(i == 0)
        def _():
            for_units(0, lambda rows: x_copy(0, 0, rows).start())

        for_units(i, lambda rows: x_copy(i, slot, rows).wait())

        @pl.when(i + 1 < n_items)
        def _():
            for_units(i + 1, lambda rows: x_copy(i + 1, 1 - slot, rows).start())

        @pl.when(run_start == 1)
        def _():
            @pl.when(i == 0)
            def _():
                for c in fetch(expert, wslot):
                    c.start()

            for c in fetch(expert, wslot):
                c.wait()

            @pl.when(next_expert >= 0)
            def _():
                for c in fetch(next_expert, 1 - wslot):
                    c.start()

            w1_scr[...] = w1_buf[wslot].astype(BF16)
            w3_scr[...] = w3_buf[wslot].astype(BF16)
            w2_scr[...] = w2_buf[wslot].astype(BF16)

        @pl.when(i >= 2)
        def _():
            for_units(i - 2, lambda rows: y_copy(i - 2, slot, rows).wait())

        for_units(i, mlp)

        @pl.when(i == n_items - 1)
        def _():
            @pl.when(i >= 1)
            def _():
                for_units(i - 1, lambda rows: y_copy(i - 1, 1 - slot, rows).wait())

            for_units(i, lambda rows: y_copy(i, slot, rows).wait())
            y_buf[0, 0:EXPERT_UNIT, :] = jnp.zeros((EXPERT_UNIT, y_buf.shape[2]), y_buf.dtype)
            first_free = (sched_ref[4, i] + sched_ref[5, i] * EXPERT_UNIT) // EXPERT_UNIT

            def zero_copy(u):
                return pltpu.make_async_copy(
                    y_buf.at[0, pl.ds(0, EXPERT_UNIT)],
                    y_hbm.at[pl.ds(pl.multiple_of(u * EXPERT_UNIT, EXPERT_UNIT), EXPERT_UNIT)], ysems.at[0])

            def start_zero(u, _):
                zero_copy(u).start()
                return 0

            def wait_zero(u, _):
                zero_copy(u).wait()
                return 0

            lax.fori_loop(first_free, y_hbm.shape[0] // EXPERT_UNIT, start_zero, 0)
            lax.fori_loop(first_free, y_hbm.shape[0] // EXPERT_UNIT, wait_zero, 0)


def _expert_mlp(xs, schedule, n_items, layer, w1, w3, w2):
    d, de = w1.shape[2], w1.shape[3]
    max_rows = 2 * EXPERT_UNIT
    return pl.pallas_call(
        functools.partial(_expert_kernel, layer=layer),
        out_shape=jax.ShapeDtypeStruct(xs.shape, I32),
        grid_spec=pltpu.PrefetchScalarGridSpec(
            num_scalar_prefetch=2,
            grid=(schedule.shape[1],),
            in_specs=[_HBM, _HBM, _HBM, _HBM],
            out_specs=_HBM,
            scratch_shapes=[pltpu.VMEM((2, max_rows, d // 2), I32), pltpu.VMEM((2, max_rows, d // 2), I32),
                            pltpu.VMEM((2, d, de), F32), pltpu.VMEM((2, d, de), F32), pltpu.VMEM((2, de, d), F32),
                            pltpu.VMEM((d, de), BF16), pltpu.VMEM((d, de), BF16), pltpu.VMEM((de, d), BF16),
                            pltpu.SemaphoreType.DMA((2, 3)), pltpu.SemaphoreType.DMA((2,)),
                            pltpu.SemaphoreType.DMA((2,))],
        ),
        compiler_params=_params(),
        name="expert_mlp",
    )(schedule, n_items, xs, w1, w3, w2)


def _combine_kernel(x_ref, y0_ref, y1_ref, gate_ref, g_ref, b_ref, o_ref):
    o_ref[...] = _moe_output(x_ref, y0_ref, y1_ref, gate_ref, g_ref, b_ref)


def _combine(x, yg, gates, g, b):
    n, d = x.shape
    tiles = n // TM
    return pl.pallas_call(
        _combine_kernel,
        out_shape=jax.ShapeDtypeStruct((n, d), F32),
        grid=(tiles,),
        in_specs=[pl.BlockSpec((TM, d), lambda i: (i, 0)),
                  pl.BlockSpec((TM, d // 2), lambda i: (i, 0)),
                  pl.BlockSpec((TM, d // 2), lambda i: (i + tiles, 0)),
                  pl.BlockSpec((TM, TOP_K), lambda i: (i, 0)),
                  _const_spec((1, d)), _const_spec((1, d))],
        out_specs=pl.BlockSpec((TM, d), lambda i: (i, 0)),
        compiler_params=_params(),
        name="combine",
    )(x, yg, yg, gates, g.reshape(1, -1), b.reshape(1, -1))


def _moe_experts(x_packed, table, counts, layer, w1, w3, w2):
    n = x_packed.shape[0]
    max_units = (n * TOP_K + N_EXPERTS * (EXPERT_UNIT - 1) + EXPERT_UNIT - 1) // EXPERT_UNIT
    max_items = (max_units + N_EXPERTS + 1) // 2
    experts = table[:TOP_K].astype(I32)
    ranks = table[TOP_K:2 * TOP_K].astype(I32)

    units_e = (counts + EXPERT_UNIT - 1) // EXPERT_UNIT
    units_start = jnp.cumsum(units_e) - units_e
    items_e = (units_e + 1) // 2
    items_end = jnp.cumsum(items_e)
    n_items = items_end[-1:].astype(I32)
    item_ids = jnp.arange(max_items, dtype=I32)
    item_expert = jnp.minimum(jnp.sum(items_end[None, :] <= item_ids[:, None], axis=1), N_EXPERTS - 1)
    within = item_ids - (items_end - items_e)[item_expert]
    item_row0 = (units_start[item_expert] + 2 * within) * EXPERT_UNIT
    item_units = jnp.clip(units_e[item_expert] - 2 * within, 1, 2)
    run_start_flag = ((within == 0) & (item_ids < n_items[0])).astype(I32)
    slot = (jnp.cumsum(run_start_flag) - 1) % 2
    next_item = items_end[item_expert]
    next_expert = jnp.where(next_item < n_items[0], item_expert[jnp.minimum(next_item, max_items - 1)], -1)
    schedule = jnp.stack([item_expert, slot, run_start_flag, next_expert, item_row0, item_units]).astype(I32)
    expert_ids = jnp.arange(N_EXPERTS, dtype=I32)
    start_of = jnp.sum(jnp.where(experts[:, :, None] == expert_ids, units_start * EXPERT_UNIT, 0), axis=-1)
    dest = (start_of + ranks).astype(I32).reshape(-1)
    xs = _sc_dispatch(x_packed, dest, max_units * EXPERT_UNIT)
    y = _expert_mlp(xs, schedule, n_items, layer, w1, w3, w2)
    return _sc_gather(y, dest)


def kernel(x, mem, w_in_even, w_pool, pool_scale, ln_v_g, ln_v_b, w_spatial, b_spatial, w_out_even,
           w_in_odd, conv_w, conv_b, w_out_odd, wq_x, wk_x, wv_x, wo_x, ln_g, ln_b, wr_group,
           br_group, wr_expert, br_expert, w1, w3, w2):
    bsz, seq, d = x.shape
    assert seq % TM == 0 and d % LANES == 0
    mlen = mem.shape[1]
    k_all, v_all = _memory_kv(mem.reshape(bsz * mlen, d), wk_x, wv_x)
    k_all = k_all.reshape(DEPTH, bsz, mlen, d)
    v_all = v_all.reshape(DEPTH, bsz, mlen, d)
    h = x.reshape(bsz * seq, d)
    pending = None
    for l in range(DEPTH):
        i = l // 2
        if l % 2 == 0:
            h = _even_mixer(h, pending, seq, i, w_in_even, w_pool[i], pool_scale[i], ln_v_g[i], ln_v_b[i],
                            w_spatial[i], b_spatial[i], w_out_even, ln_g[l, 0], ln_b[l, 0])
        else:
            h = _odd_mixer(h, pending, seq, i, w_in_odd, conv_w[i], conv_b[i], w_out_odd,
                           ln_g[l, 0], ln_b[l, 0])
        router_w, router_b = _router_weights(wr_group[l], br_group[l], wr_expert[l], br_expert[l])
        h, hp, table, gates, counts = _cross_attn(h, seq, l, k_all, v_all, wq_x, wo_x,
                                                  ln_g[l, 1], ln_b[l, 1], router_w, router_b)
        yg = _moe_experts(hp, table, counts, l, w1, w3, w2)
        pending = (yg, gates, ln_g[l, 2], ln_b[l, 2])
    return _combine(h, *pending).reshape(bsz, seq, d)
```

```python
import functools
import math

import jax
import jax.numpy as jnp
from jax import lax
from jax.experimental import pallas as pl
from jax.experimental.pallas import tpu as pltpu
from jax.experimental.pallas import tpu_sc as plsc

F32 = jnp.float32
BF16 = jnp.bfloat16
I32 = jnp.int32

POOL_WINDOWS = (2, 4, 8, 16)
assert all(w & (w - 1) == 0 for w in POOL_WINDOWS)
N_SG_HEADS = 4
CHUNK = 128
CONV_WIDTH = 3
N_XHEADS = 4
N_GROUPS = 4
EXPERTS_PER_GROUP = 8
N_EXPERTS = N_GROUPS * EXPERTS_PER_GROUP
TOP_K = 2
DEPTH = 4
ALPHA = (2.0 * DEPTH) ** 0.25
LN_EPS = 1e-5

LANES = 128
SC_CORES = 2
SC_WORKERS = 32
SC_LANES = 16
VMEM_LIMIT = 56 * 1024 * 1024

SC_CHUNK = 64
TM = 1024
SUB_TILES = 2
ROUTE_ROWS = 8
POOL_HALO = 16
CONV_HALO = 8
EXPERT_UNIT = 128
ITEM_UNITS = 4
ROUTE_COLS = 128
STAGE_COLS = 512

_NT = (((1,), (1,)), ((), ()))


def _dot(a, b):
    return jnp.dot(a, b, preferred_element_type=F32)


def _layer_norm(y, g, b):
    mu = jnp.mean(y, axis=-1, keepdims=True)
    yc = y - mu
    var = jnp.mean(yc * yc, axis=-1, keepdims=True)
    return yc * lax.rsqrt(var + LN_EPS) * g + b


def _gelu_tanh(x):
    c = math.sqrt(2.0 / math.pi)
    return 0.5 * x * (1.0 + jnp.tanh(c * (x + 0.044715 * (x * x * x))))


def _pack_halves(v):
    c = v.shape[1] // 2
    lo = pltpu.bitcast(v[:, :c].astype(BF16).astype(F32), jnp.uint32)
    hi = pltpu.bitcast(v[:, c:].astype(BF16).astype(F32), jnp.uint32)
    return pltpu.bitcast((hi & jnp.uint32(0xFFFF0000)) | (lo >> 16), I32)


def _unpack_halves(w):
    u = pltpu.bitcast(w, jnp.uint32)
    return pltpu.bitcast(u << 16, F32), pltpu.bitcast(u & jnp.uint32(0xFFFF0000), F32)


def _load_cast(w_hbm, w_scr, stage, sems):
    chunks = w_scr.shape[1] // STAGE_COLS

    def chunk_copy(c):
        return pltpu.make_async_copy(w_hbm.at[:, pl.ds(c * STAGE_COLS, STAGE_COLS)], stage.at[c % 2],
                                     sems.at[c % 2])

    chunk_copy(0).start()
    for c in range(chunks):
        if c + 1 < chunks:
            chunk_copy(c + 1).start()
        chunk_copy(c).wait()
        w_scr[:, c * STAGE_COLS:(c + 1) * STAGE_COLS] = stage[c % 2].astype(BF16)


def _stage_scratch(rows):
    return [pltpu.VMEM((2, rows, STAGE_COLS), F32), pltpu.SemaphoreType.DMA((2,))]


_HBM = pl.BlockSpec(memory_space=pl.ANY)


def _const_spec(shape):
    nd = len(shape)
    return pl.BlockSpec(shape, lambda i: (0,) * nd)


def _params():
    return pltpu.CompilerParams(dimension_semantics=("arbitrary",), vmem_limit_bytes=VMEM_LIMIT)


def _moe_output(x_ref, y0_ref, y1_ref, gate_ref, g_ref, b_ref):
    gates = gate_ref[...]
    g0, g1 = gates[:, 0:1], gates[:, 1:2]
    y0_lo, y0_hi = _unpack_halves(y0_ref[...])
    y1_lo, y1_hi = _unpack_halves(y1_ref[...])
    ff = jnp.concatenate([g0 * y0_lo + g1 * y1_lo, g0 * y0_hi + g1 * y1_hi], axis=1)
    return _layer_norm(ALPHA * x_ref[...] + ff, g_ref[...], b_ref[...])


def _mixer_input(src, pending):
    if not pending:
        x_ref, xh_ref = src
        return x_ref[...], xh_ref[...]
    x_ref, xh_ref, y0_ref, y0h_ref, y1_ref, y1h_ref, gate_ref, gateh_ref, g_ref, b_ref = src
    return (_moe_output(x_ref, y0_ref, y1_ref, gate_ref, g_ref, b_ref),
            _moe_output(xh_ref, y0h_ref, y1h_ref, gateh_ref, g_ref, b_ref))


def _mixer_sources(x, pending, halo):
    n, d = x.shape
    tiles = n // TM
    halo_blocks = TM // halo

    def halo_index(i):
        return jnp.maximum(i * halo_blocks - 1, 0)

    specs = [pl.BlockSpec((TM, d), lambda i: (i, 0)), pl.BlockSpec((halo, d), lambda i: (halo_index(i), 0))]
    args = [x, x]
    if pending is not None:
        yg, gates, g, b = pending
        specs += [pl.BlockSpec((TM, d // 2), lambda i: (i, 0)),
                  pl.BlockSpec((halo, d // 2), lambda i: (halo_index(i), 0)),
                  pl.BlockSpec((TM, d // 2), lambda i: (i + tiles, 0)),
                  pl.BlockSpec((halo, d // 2), lambda i: (halo_index(i) + tiles * halo_blocks, 0)),
                  pl.BlockSpec((TM, TOP_K), lambda i: (i, 0)),
                  pl.BlockSpec((halo, TOP_K), lambda i: (halo_index(i), 0)),
                  _const_spec((1, d)), _const_spec((1, d))]
        args += [yg, yg, yg, yg, gates, gates, g.reshape(1, -1), b.reshape(1, -1)]
    return specs, args


def _even_kernel(*refs, tiles_per_seq, layer, pending):
    n_src = 10 if pending else 2
    (win_hbm, wpool_ref, pscale_ref, lvg_ref, lvb_ref, ws_ref, bst_ref, wout_hbm, g_ref, b_ref, o_ref,
     a_scr, cat_scr, win_ref, wout_ref, stage, sems) = refs[n_src:]
    tm = o_ref.shape[0]
    d_pool = a_scr.shape[1]
    d_sg = lvg_ref.shape[1]
    pgd = d_pool // len(POOL_WINDOWS)
    hd_dim = d_sg // N_SG_HEADS
    seq_tile = pl.program_id(0) % tiles_per_seq

    @pl.when(pl.program_id(0) == 0)
    def _():
        _load_cast(win_hbm.at[layer], win_ref, stage, sems)
        _load_cast(wout_hbm.at[layer], wout_ref, stage, sems)

    x, xh = _mixer_input(refs[:n_src], pending)
    sub = tm // SUB_TILES

    ah = _dot(xh.astype(BF16), win_ref[:, :d_pool])
    a_scr[0:POOL_HALO, :] = jnp.where(seq_tile == 0, 0.0, ah)
    row = lax.broadcasted_iota(I32, (CHUNK, CHUNK), 0)
    col = lax.broadcasted_iota(I32, (CHUNK, CHUNK), 1)
    ws_masked = [jnp.where(row >= col, ws_ref[hd], 0.0).astype(BF16) for hd in range(N_SG_HEADS)]

    def in_proj(st):
        h = _dot(x[st * sub:(st + 1) * sub, :].astype(BF16), win_ref[...])
        a_scr[POOL_HALO + st * sub:POOL_HALO + (st + 1) * sub, :] = h[:, :d_pool]
        return h[:, d_pool:]

    def branches_out_proj(st, hz):
        base = st * sub
        pos = seq_tile * tm + base + lax.broadcasted_iota(I32, (sub, 1), 0)
        for g, w in enumerate(POOL_WINDOWS):
            cs = slice(g * pgd, (g + 1) * pgd)
            tok = a_scr[POOL_HALO + base:POOL_HALO + base + sub, cs]
            acc = a_scr[base:POOL_HALO + base + sub, cs]
            span = 1
            while span < w:
                acc = acc[span:, :] + acc[:-span, :]
                span *= 2
            acc = acc[acc.shape[0] - sub:, :]
            cnt = jnp.minimum(pos + 1, w).astype(F32)
            dev = acc * (1.0 / cnt) - tok
            yg = _dot(dev.astype(BF16), wpool_ref[g])
            cat_scr[base:base + sub, cs] = (yg * pscale_ref[:, cs]).astype(BF16)

        z = _gelu_tanh(hz)
        u = z[:, :d_sg]
        v = _layer_norm(z[:, d_sg:], lvg_ref[...], lvb_ref[...]).astype(BF16)
        for hd in range(N_SG_HEADS):
            hs = slice(hd * hd_dim, (hd + 1) * hd_dim)
            bcol = bst_ref[:, hd:hd + 1]
            for ck in range(sub // CHUNK):
                rs = slice(ck * CHUNK, (ck + 1) * CHUNK)
                sv = _dot(ws_masked[hd], v[rs, hs]) + bcol
                cat_scr[base + ck * CHUNK:base + (ck + 1) * CHUNK,
                        d_pool + hd * hd_dim:d_pool + (hd + 1) * hd_dim] = (u[rs, hs] * sv).astype(BF16)
        return _dot(cat_scr[base:base + sub, :], wout_ref[...])

    hzs = [in_proj(st) for st in range(SUB_TILES)]
    mixes = [branches_out_proj(st, hzs[st]) for st in range(SUB_TILES)]
    for st in range(SUB_TILES):
        rs = slice(st * sub, (st + 1) * sub)
        o_ref[rs, :] = _layer_norm(ALPHA * x[rs, :] + mixes[st], g_ref[...], b_ref[...])


def _even_mixer(x, pending, seq, layer, w_in, w_pool, pool_scale, ln_v_g, ln_v_b, w_spatial, b_spatial,
                w_out, g, b):
    n, d = x.shape
    d_in = w_in.shape[2]
    d_pool = pool_scale.shape[0]
    d_sg = ln_v_g.shape[0]
    kern = functools.partial(_even_kernel, tiles_per_seq=seq // TM, layer=layer, pending=pending is not None)
    src_specs, src_args = _mixer_sources(x, pending, POOL_HALO)
    return pl.pallas_call(
        kern,
        out_shape=jax.ShapeDtypeStruct((n, d), F32),
        grid=(n // TM,),
        in_specs=src_specs + [
            _HBM,
            _const_spec(w_pool.shape),
            _const_spec((1, d_pool)),
            _const_spec((1, d_sg)),
            _const_spec((1, d_sg)),
            _const_spec(w_spatial.shape),
            _const_spec((CHUNK, N_SG_HEADS)),
            _HBM,
            _const_spec((1, d)),
            _const_spec((1, d)),
        ],
        out_specs=pl.BlockSpec((TM, d), lambda i: (i, 0)),
        scratch_shapes=[pltpu.VMEM((POOL_HALO + TM, d_pool), F32), pltpu.VMEM((TM, d_pool + d_sg), BF16),
                        pltpu.VMEM((d, d_in), BF16), pltpu.VMEM((d_pool + d_sg, d), BF16)] + _stage_scratch(d),
        compiler_params=_params(),
        name="even_mixer",
    )(*src_args, w_in, w_pool.astype(BF16), pool_scale.reshape(1, -1), ln_v_g.reshape(1, -1),
      ln_v_b.reshape(1, -1), w_spatial, b_spatial.T, w_out, g.reshape(1, -1), b.reshape(1, -1))


def _odd_kernel(*refs, tiles_per_seq, layer, pending):
    n_src = 10 if pending else 2
    (win_hbm, cwt_ref, cb_ref, wout_hbm, g_ref, b_ref, o_ref, zc_scr,
     win_ref, wout_ref, stage, sems) = refs[n_src:]
    tm, d = o_ref.shape
    seq_tile = pl.program_id(0) % tiles_per_seq

    @pl.when(pl.program_id(0) == 0)
    def _():
        _load_cast(win_hbm.at[layer], win_ref, stage, sems)
        _load_cast(wout_hbm.at[layer], wout_ref, stage, sems)

    x, xh = _mixer_input(refs[:n_src], pending)
    hh = _dot(xh.astype(BF16), win_ref[:, d:])
    zc_scr[0:CONV_HALO, :] = jnp.where(seq_tile == 0, 0.0, hh[:, :d] * hh[:, d:])
    sub = tm // SUB_TILES

    def in_proj(st):
        xb = x[st * sub:(st + 1) * sub, :].astype(BF16)
        hc = _dot(xb, win_ref[:, d:2 * d])
        hz = _dot(xb, win_ref[:, 2 * d:])
        zc_scr[CONV_HALO + st * sub:CONV_HALO + (st + 1) * sub, :] = hc * hz
        return _dot(xb, win_ref[:, :d])

    def conv_out_proj(st, gate):
        conv = cb_ref[...]
        for j in range(CONV_WIDTH):
            off = CONV_HALO + st * sub - (CONV_WIDTH - 1) + j
            conv = conv + zc_scr[off:off + sub, :] * cwt_ref[j:j + 1, :]
        return _dot((gate * conv).astype(BF16), wout_ref[...])

    gates = [in_proj(st) for st in range(SUB_TILES)]
    ys = [conv_out_proj(st, gates[st]) for st in range(SUB_TILES)]
    for st in range(SUB_TILES):
        rs = slice(st * sub, (st + 1) * sub)
        o_ref[rs, :] = _layer_norm(ALPHA * x[rs, :] + ys[st], g_ref[...], b_ref[...])


def _odd_mixer(x, pending, seq, layer, w_in, conv_w, conv_b, w_out, g, b):
    n, d = x.shape
    kern = functools.partial(_odd_kernel, tiles_per_seq=seq // TM, layer=layer, pending=pending is not None)
    src_specs, src_args = _mixer_sources(x, pending, CONV_HALO)
    return pl.pallas_call(
        kern,
        out_shape=jax.ShapeDtypeStruct((n, d), F32),
        grid=(n // TM,),
        in_specs=src_specs + [
            _HBM,
            _const_spec((CONV_WIDTH, d)),
            _const_spec((1, d)),
            _HBM,
            _const_spec((1, d)),
            _const_spec((1, d)),
        ],
        out_specs=pl.BlockSpec((TM, d), lambda i: (i, 0)),
        scratch_shapes=[pltpu.VMEM((CONV_HALO + TM, d), F32),
                        pltpu.VMEM(w_in.shape[1:], BF16), pltpu.VMEM(w_out.shape[1:], BF16)] + _stage_scratch(d),
        compiler_params=_params(),
        name="odd_mixer",
    )(*src_args, w_in, conv_w.T, conv_b.reshape(1, -1), w_out, g.reshape(1, -1), b.reshape(1, -1))


def _kv_kernel(mem_ref, wk_ref, wv_ref, k_ref, v_ref):
    m = mem_ref[...].astype(BF16)
    k_ref[...] = _dot(m, wk_ref[...].astype(BF16)).astype(BF16)
    v_ref[...] = _dot(m, wv_ref[...].astype(BF16)).astype(BF16)


def _memory_kv(mem2d, wk, wv):
    nl, d, _ = wk.shape
    rows = mem2d.shape[0]
    out = jax.ShapeDtypeStruct((nl, rows, d), BF16)
    wspec = pl.BlockSpec((None, d, d), lambda l: (l, 0, 0))
    ospec = pl.BlockSpec((None, rows, d), lambda l: (l, 0, 0))
    return pl.pallas_call(
        _kv_kernel,
        out_shape=(out, out),
        grid=(nl,),
        in_specs=[_const_spec((rows, d)), wspec, wspec],
        out_specs=(ospec, ospec),
        compiler_params=_params(),
        name="memory_kv",
    )(mem2d, wk, wv)


def _attn_kernel(x_ref, k_ref, v_ref, wq_hbm, wo_hbm, g_ref, b_ref, wr_ref, br_ref,
                 o_ref, op_ref, rt_ref, rg_ref, cnt_ref, o_scr, carry_scr, wq_ref, wo_ref, stage, sems,
                 *, layer):
    tm, d = x_ref.shape
    hd_dim = d // N_XHEADS
    sub = tm // SUB_TILES

    @pl.when(pl.program_id(0) == 0)
    def _():
        carry_scr[...] = jnp.zeros_like(carry_scr)
        _load_cast(wq_hbm.at[layer], wq_ref, stage, sems)
        _load_cast(wo_hbm.at[layer], wo_ref, stage, sems)

    row_slices = [slice(st * sub, (st + 1) * sub) for st in range(SUB_TILES)]
    qs = [_dot(x_ref[rs, :].astype(BF16), wq_ref[...]) * (1.0 / math.sqrt(hd_dim)) for rs in row_slices]
    for rs, q in zip(row_slices, qs):
        for hd in range(N_XHEADS):
            hs = slice(hd * hd_dim, (hd + 1) * hd_dim)
            s = lax.dot_general(q[:, hs].astype(BF16), k_ref[:, hs], _NT, preferred_element_type=F32)
            p = jnp.exp(s - jnp.max(s, axis=-1, keepdims=True))
            p = p * (1.0 / jnp.sum(p, axis=-1, keepdims=True))
            o_scr[rs, hs] = _dot(p.astype(BF16), v_ref[:, hs]).astype(BF16)
    xas = [_dot(o_scr[rs, :], wo_ref[...]) for rs in row_slices]
    for rs, xa in zip(row_slices, xas):
        out = _layer_norm(ALPHA * x_ref[rs, :] + xa, g_ref[...], b_ref[...])
        o_ref[rs, :] = out
        op_ref[rs, :] = _pack_halves(out)
        table, gates = _route_rows(out, wr_ref, br_ref, carry_scr)
        rt_ref[:, rs] = table
        rg_ref[rs, :] = gates
    cnt_ref[...] = carry_scr[...].astype(I32)


def _cross_attn(x, seq, layer, k, v, wq, wo, g, b, router_w, router_b):
    n, d = x.shape
    m = k.shape[2]
    tiles_per_seq = seq // TM
    kvspec = pl.BlockSpec((None, None, m, d), lambda i: (layer, i // tiles_per_seq, 0, 0))
    out, packed, table, gates, cnt = pl.pallas_call(
        functools.partial(_attn_kernel, layer=layer),
        out_shape=(jax.ShapeDtypeStruct((n, d), F32), jax.ShapeDtypeStruct((n, d // 2), I32),
                   jax.ShapeDtypeStruct((ROUTE_ROWS, n), F32), jax.ShapeDtypeStruct((n, TOP_K), F32),
                   jax.ShapeDtypeStruct((1, ROUTE_COLS), I32)),
        grid=(n // TM,),
        in_specs=[
            pl.BlockSpec((TM, d), lambda i: (i, 0)),
            kvspec, kvspec,
            _HBM, _HBM,
            _const_spec((1, d)), _const_spec((1, d)),
            _const_spec((d, 2 * ROUTE_COLS)), _const_spec((1, ROUTE_COLS)),
        ],
        out_specs=(pl.BlockSpec((TM, d), lambda i: (i, 0)), pl.BlockSpec((TM, d // 2), lambda i: (i, 0)),
                   pl.BlockSpec((ROUTE_ROWS, TM), lambda i: (0, i)), pl.BlockSpec((TM, TOP_K), lambda i: (i, 0)),
                   _const_spec((1, ROUTE_COLS))),
        scratch_shapes=[pltpu.VMEM((TM, d), BF16), pltpu.VMEM((1, ROUTE_COLS), F32),
                        pltpu.VMEM((d, d), BF16), pltpu.VMEM((d, d), BF16)] + _stage_scratch(d),
        compiler_params=_params(),
        name="cross_attn",
    )(x, k, v, wq, wo, g.reshape(1, -1), b.reshape(1, -1), router_w, router_b)
    return out, packed, table, gates, cnt[0, N_GROUPS:N_GROUPS + N_EXPERTS]


def _route_rows(x, wcat_ref, bias_ref, carry_scr):
    tm = x.shape[0]

    xh = x.astype(BF16)
    xl = (x - xh.astype(F32)).astype(BF16)
    r1 = _dot(xh, wcat_ref[...])
    r2 = _dot(xl, wcat_ref[:, :ROUTE_COLS])
    logits = r1[:, :ROUTE_COLS] + r1[:, ROUTE_COLS:] + r2 + bias_ref[...]

    lane = lax.broadcasted_iota(I32, (tm, ROUTE_COLS), 1).astype(F32)
    neg = -jnp.inf

    def first_argmax(vals):
        mx = jnp.max(vals, axis=-1, keepdims=True)
        idx = jnp.min(jnp.where(vals == mx, lane, float(ROUTE_COLS)), axis=-1, keepdims=True)
        return mx, idx

    gl = jnp.where(lane < N_GROUPS, logits, neg)
    gmax, g_sel = first_argmax(gl)
    gate_g = 1.0 / jnp.sum(jnp.exp(gl - gmax), axis=-1, keepdims=True)

    lo = N_GROUPS + g_sel * EXPERTS_PER_GROUP
    el = jnp.where((lane >= lo) & (lane < lo + EXPERTS_PER_GROUP), logits, neg)
    m1, i1 = first_argmax(el)
    m2, i2 = first_argmax(jnp.where(lane == i1, neg, el))
    e21 = jnp.exp(m2 - m1)
    w1 = 1.0 / (1.0 + e21)
    w2 = e21 / (1.0 + e21)

    oh1 = lane == i1
    oh2 = lane == i2
    oh = (oh1 | oh2).astype(BF16)
    r = lax.broadcasted_iota(I32, (tm, tm), 0)
    c = lax.broadcasted_iota(I32, (tm, tm), 1)
    before = _dot((r > c).astype(BF16), oh) + carry_scr[...]
    rank1 = jnp.sum(jnp.where(oh1, before, 0.0), axis=-1, keepdims=True)
    rank2 = jnp.sum(jnp.where(oh2, before, 0.0), axis=-1, keepdims=True)
    carry_scr[...] += jnp.sum(oh.astype(F32), axis=0, keepdims=True)

    cols = jnp.where(lane == 0.0, i1 - N_GROUPS,
                     jnp.where(lane == 1.0, i2 - N_GROUPS,
                               jnp.where(lane == 2.0, rank1, jnp.where(lane == 3.0, rank2, 0.0))))
    table = jnp.transpose(cols)[:ROUTE_ROWS, :]
    l2 = lax.broadcasted_iota(I32, (tm, TOP_K), 1)
    gates = jnp.where(l2 == 0, gate_g * w1, gate_g * w2)
    return table, gates


def _router_weights(wr_g, br_g, wr_e, br_e):
    d = wr_g.shape[0]
    w = jnp.concatenate([wr_g, jnp.transpose(wr_e, (1, 0, 2)).reshape(d, N_EXPERTS)], axis=1)
    w = jnp.pad(w, ((0, 0), (0, ROUTE_COLS - w.shape[1])))
    w_hi = w.astype(BF16)
    w_lo = (w - w_hi.astype(F32)).astype(BF16)
    bias = jnp.pad(jnp.concatenate([br_g, br_e.reshape(-1)]), (0, ROUTE_COLS - N_GROUPS - N_EXPERTS))
    return jnp.concatenate([w_hi, w_lo], axis=1), bias.reshape(1, -1)


def _sc_worker_rows(rows):
    per_worker = rows // SC_WORKERS
    n_chunks = per_worker // SC_CHUNK
    assert per_worker * SC_WORKERS == rows and n_chunks * SC_CHUNK == per_worker and n_chunks % 2 == 0
    return per_worker, n_chunks


def _sc_gather_rows(table_hbm, out_hbm, idx_v, rows_v, gsem, wsem, base, n_chunks):
    def fetch(c, slot):
        off = pl.multiple_of(c * SC_CHUNK, SC_CHUNK)
        return pltpu.make_async_copy(table_hbm.at[idx_v.at[pl.ds(off, SC_CHUNK)]], rows_v.at[slot],
                                     gsem.at[slot])

    def put(c, slot):
        off = pl.multiple_of(c * SC_CHUNK, SC_CHUNK)
        return pltpu.make_async_copy(rows_v.at[slot], out_hbm.at[pl.ds(base + off, SC_CHUNK)], wsem.at[slot])

    fetch(0, 0).start()

    @pl.loop(0, n_chunks, step=2)
    def _(c0):
        for slot in range(2):
            c = c0 + slot

            @pl.when(c + 1 < n_chunks)
            def _():
                @pl.when(c >= 1)
                def _():
                    put(c - 1, 1 - slot).wait()
                fetch(c + 1, 1 - slot).start()

            fetch(c, slot).wait()
            put(c, slot).start()

    put(n_chunks - 2, 0).wait()
    put(n_chunks - 1, 1).wait()


def _sc_row_scratch(per_worker, d, dtype):
    return [pltpu.VMEM((per_worker,), I32), pltpu.VMEM((2, SC_CHUNK, d), dtype),
            pltpu.SemaphoreType.DMA((2,)), pltpu.SemaphoreType.DMA((2,))]


def _sc_gather(table, idx):
    b = idx.shape[0]
    d = table.shape[1]
    per_worker, n_chunks = _sc_worker_rows(b)
    mesh = plsc.VectorSubcoreMesh(core_axis_name="c", subcore_axis_name="s")

    @functools.partial(
        pl.kernel, mesh=mesh,
        out_type=jax.ShapeDtypeStruct((b, d), table.dtype),
        scratch_types=_sc_row_scratch(per_worker, d, table.dtype),
        name="sc_gather",
    )
    def gather(table_hbm, idx_hbm, out_hbm, idx_v, rows_v, gsem, wsem):
        base = (lax.axis_index("s") * SC_CORES + lax.axis_index("c")) * per_worker
        pltpu.sync_copy(idx_hbm.at[pl.ds(base, per_worker)], idx_v)
        _sc_gather_rows(table_hbm, out_hbm, idx_v, rows_v, gsem, wsem, base, n_chunks)

    return gather(table, idx)


def _sc_dispatch(table, dest_flat, rows):
    n, d = table.shape
    a = dest_flat.shape[0]
    lanes = SC_LANES
    per_worker, n_chunks = _sc_worker_rows(rows)
    assert a % lanes == 0 and per_worker % lanes == 0
    mesh = plsc.VectorSubcoreMesh(core_axis_name="c", subcore_axis_name="s")

    @functools.partial(
        pl.kernel, mesh=mesh,
        out_type=jax.ShapeDtypeStruct((rows, d), table.dtype),
        scratch_types=[pltpu.VMEM((a,), I32)] + _sc_row_scratch(per_worker, d, table.dtype),
        compiler_params=pltpu.CompilerParams(needs_layout_passes=False),
        name="sc_dispatch",
    )
    def dispatch(table_hbm, dest_hbm, out_hbm, dest_v, idx_v, rows_v, gsem, wsem):
        base = (lax.axis_index("s") * SC_CORES + lax.axis_index("c")) * per_worker
        pltpu.sync_copy(dest_hbm, dest_v)
        lane = lax.iota(I32, lanes)

        @pl.loop(0, per_worker // lanes)
        def _(i):
            idx_v[pl.ds(i * lanes, lanes)] = lax.rem(base + i * lanes + lane, n)

        @plsc.parallel_loop(0, a // lanes, unroll=8)
        def _(i):
            local = dest_v[pl.ds(i * lanes, lanes)] - base
            mine = (local >= 0) & (local < per_worker)
            plsc.store_scatter(idx_v, [jnp.where(mine, local, 0)], lax.rem(i * lanes + lane, n), mask=mine)

        _sc_gather_rows(table_hbm, out_hbm, idx_v, rows_v, gsem, wsem, base, n_chunks)

    return dispatch(table, dest_flat)


def _expert_kernel(sched_ref, ni_ref, xs_hbm, w1_hbm, w3_hbm, w2_hbm, y_hbm,
                   x_buf, y_buf, w1_buf, w3_buf, w2_buf, w1_scr, w3_scr, w2_scr, wsems, xsems, ysems,
                   *, layer):
    i = pl.program_id(0)
    n_items = ni_ref[0]
    expert, wslot, run_start, next_expert = (sched_ref[r, i] for r in range(4))
    slot = i % 2

    def for_units(item, fn):
        for units in range(1, ITEM_UNITS + 1):
            @pl.when(sched_ref[5, item] == units)
            def _():
                fn(units * EXPERT_UNIT)

    def x_copy(item, s, rows):
        row0 = pl.multiple_of(sched_ref[4, item], EXPERT_UNIT)
        return pltpu.make_async_copy(xs_hbm.at[pl.ds(row0, rows)], x_buf.at[s, pl.ds(0, rows)], xsems.at[s])

    def y_copy(item, s, rows):
        row0 = pl.multiple_of(sched_ref[4, item], EXPERT_UNIT)
        return pltpu.make_async_copy(y_buf.at[s, pl.ds(0, rows)], y_hbm.at[pl.ds(row0, rows)], ysems.at[s])

    def fetch(e, s):
        return [pltpu.make_async_copy(w_hbm.at[layer, e], buf.at[s], wsems.at[s, j])
                for j, (w_hbm, buf) in enumerate(((w1_hbm, w1_buf), (w3_hbm, w3_buf), (w2_hbm, w2_buf)))]

    def mlp(rows):
        x_lo, x_hi = _unpack_halves(x_buf[slot, 0:rows, :])
        xb = jnp.concatenate([x_lo.astype(BF16), x_hi.astype(BF16)], axis=1)
        h1 = _dot(xb, w1_scr[...])
        h3 = _dot(xb, w3_scr[...])
        hid = h1 * (1.0 / (1.0 + jnp.exp(-h1))) * h3
        y_buf[slot, 0:rows, :] = _pack_halves(_dot(hid.astype(BF16), w2_scr[...]))
        y_copy(i, slot, rows).start()

    @pl.when(i < n_items)
    def _():
        @pl.when(i == 0)
        def _():
            for_units(0, lambda rows: x_copy(0, 0, rows).start())

        for_units(i, lambda rows: x_copy(i, slot, rows).wait())

        @pl.when(i + 1 < n_items)
        def _():
            for_units(i + 1, lambda rows: x_copy(i + 1, 1 - slot, rows).start())

        @pl.when(run_start == 1)
        def _():
            @pl.when(i == 0)
            def _():
                for c in fetch(expert, wslot):
                    c.start()

            for c in fetch(expert, wslot):
                c.wait()

            @pl.when(next_expert >= 0)
            def _():
                for c in fetch(next_expert, 1 - wslot):
                    c.start()

            w1_scr[...] = w1_buf[wslot].astype(BF16)
            w3_scr[...] = w3_buf[wslot].astype(BF16)
            w2_scr[...] = w2_buf[wslot].astype(BF16)

        @pl.when(i >= 2)
        def _():
            for_units(i - 2, lambda rows: y_copy(i - 2, slot, rows).wait())

        for_units(i, mlp)

        @pl.when(i == n_items - 1)
        def _():
            @pl.when(i >= 1)
            def _():
                for_units(i - 1, lambda rows: y_copy(i - 1, 1 - slot, rows).wait())

            for_units(i, lambda rows: y_copy(i, slot, rows).wait())
            y_buf[0, 0:EXPERT_UNIT, :] = jnp.zeros((EXPERT_UNIT, y_buf.shape[2]), y_buf.dtype)
            first_free = (sched_ref[4, i] + sched_ref[5, i] * EXPERT_UNIT) // EXPERT_UNIT

            def zero_copy(u):
                return pltpu.make_async_copy(
                    y_buf.at[0, pl.ds(0, EXPERT_UNIT)],
                    y_hbm.at[pl.ds(pl.multiple_of(u * EXPERT_UNIT, EXPERT_UNIT), EXPERT_UNIT)], ysems.at[0])

            def start_zero(u, _):
                zero_copy(u).start()
                return 0

            def wait_zero(u, _):
                zero_copy(u).wait()
                return 0

            lax.fori_loop(first_free, y_hbm.shape[0] // EXPERT_UNIT, start_zero, 0)
            lax.fori_loop(first_free, y_hbm.shape[0] // EXPERT_UNIT, wait_zero, 0)


def _expert_mlp(xs, schedule, n_items, layer, w1, w3, w2):
    d, de = w1.shape[2], w1.shape[3]
    max_rows = ITEM_UNITS * EXPERT_UNIT
    return pl.pallas_call(
        functools.partial(_expert_kernel, layer=layer),
        out_shape=jax.ShapeDtypeStruct(xs.shape, I32),
        grid_spec=pltpu.PrefetchScalarGridSpec(
            num_scalar_prefetch=2,
            grid=(schedule.shape[1],),
            in_specs=[_HBM, _HBM, _HBM, _HBM],
            out_specs=_HBM,
            scratch_shapes=[pltpu.VMEM((2, max_rows, d // 2), I32), pltpu.VMEM((2, max_rows, d // 2), I32),
                            pltpu.VMEM((2, d, de), F32), pltpu.VMEM((2, d, de), F32), pltpu.VMEM((2, de, d), F32),
                            pltpu.VMEM((d, de), BF16), pltpu.VMEM((d, de), BF16), pltpu.VMEM((de, d), BF16),
                            pltpu.SemaphoreType.DMA((2, 3)), pltpu.SemaphoreType.DMA((2,)),
                            pltpu.SemaphoreType.DMA((2,))],
        ),
        compiler_params=_params(),
        name="expert_mlp",
    )(schedule, n_items, xs, w1, w3, w2)


def _combine_kernel(x_ref, y0_ref, y1_ref, gate_ref, g_ref, b_ref, o_ref):
    o_ref[...] = _moe_output(x_ref, y0_ref, y1_ref, gate_ref, g_ref, b_ref)


def _combine(x, yg, gates, g, b):
    n, d = x.shape
    tiles = n // TM
    return pl.pallas_call(
        _combine_kernel,
        out_shape=jax.ShapeDtypeStruct((n, d), F32),
        grid=(tiles,),
        in_specs=[pl.BlockSpec((TM, d), lambda i: (i, 0)),
                  pl.BlockSpec((TM, d // 2), lambda i: (i, 0)),
                  pl.BlockSpec((TM, d // 2), lambda i: (i + tiles, 0)),
                  pl.BlockSpec((TM, TOP_K), lambda i: (i, 0)),
                  _const_spec((1, d)), _const_spec((1, d))],
        out_specs=pl.BlockSpec((TM, d), lambda i: (i, 0)),
        compiler_params=_params(),
        name="combine",
    )(x, yg, yg, gates, g.reshape(1, -1), b.reshape(1, -1))


def _moe_experts(x_packed, table, counts, layer, w1, w3, w2):
    n = x_packed.shape[0]
    max_units = (n * TOP_K + N_EXPERTS * (EXPERT_UNIT - 1) + EXPERT_UNIT - 1) // EXPERT_UNIT
    max_items = (max_units + N_EXPERTS * (ITEM_UNITS - 1) + ITEM_UNITS - 1) // ITEM_UNITS
    experts = table[:TOP_K].astype(I32)
    ranks = table[TOP_K:2 * TOP_K].astype(I32)

    units_e = (counts + EXPERT_UNIT - 1) // EXPERT_UNIT
    units_start = jnp.cumsum(units_e) - units_e
    items_e = (units_e + ITEM_UNITS - 1) // ITEM_UNITS
    items_end = jnp.cumsum(items_e)
    n_items = items_end[-1:].astype(I32)
    item_ids = jnp.arange(max_items, dtype=I32)
    item_expert = jnp.minimum(jnp.sum(items_end[None, :] <= item_ids[:, None], axis=1), N_EXPERTS - 1)
    within = item_ids - (items_end - items_e)[item_expert]
    item_row0 = (units_start[item_expert] + ITEM_UNITS * within) * EXPERT_UNIT
    item_units = jnp.clip(units_e[item_expert] - ITEM_UNITS * within, 1, ITEM_UNITS)
    run_start_flag = ((within == 0) & (item_ids < n_items[0])).astype(I32)
    slot = (jnp.cumsum(run_start_flag) - 1) % 2
    next_item = items_end[item_expert]
    next_expert = jnp.where(next_item < n_items[0], item_expert[jnp.minimum(next_item, max_items - 1)], -1)
    schedule = jnp.stack([item_expert, slot, run_start_flag, next_expert, item_row0, item_units]).astype(I32)
    expert_ids = jnp.arange(N_EXPERTS, dtype=I32)
    start_of = jnp.sum(jnp.where(experts[:, :, None] == expert_ids, units_start * EXPERT_UNIT, 0), axis=-1)
    dest = (start_of + ranks).astype(I32).reshape(-1)
    xs = _sc_dispatch(x_packed, dest, max_units * EXPERT_UNIT)
    y = _expert_mlp(xs, schedule, n_items, layer, w1, w3, w2)
    return _sc_gather(y, dest)


def kernel(x, mem, w_in_even, w_pool, pool_scale, ln_v_g, ln_v_b, w_spatial, b_spatial, w_out_even,
           w_in_odd, conv_w, conv_b, w_out_odd, wq_x, wk_x, wv_x, wo_x, ln_g, ln_b, wr_group,
           br_group, wr_expert, br_expert, w1, w3, w2):
    bsz, seq, d = x.shape
    assert seq % TM == 0 and d % LANES == 0
    mlen = mem.shape[1]
    k_all, v_all = _memory_kv(mem.reshape(bsz * mlen, d), wk_x, wv_x)
    k_all = k_all.reshape(DEPTH, bsz, mlen, d)
    v_all = v_all.reshape(DEPTH, bsz, mlen, d)
    h = x.reshape(bsz * seq, d)
    pending = None
    for l in range(DEPTH):
        i = l // 2
        if l % 2 == 0:
            h = _even_mixer(h, pending, seq, i, w_in_even, w_pool[i], pool_scale[i], ln_v_g[i], ln_v_b[i],
                            w_spatial[i], b_spatial[i], w_out_even, ln_g[l, 0], ln_b[l, 0])
        else:
            h = _odd_mixer(h, pending, seq, i, w_in_odd, conv_w[i], conv_b[i], w_out_odd,
                           ln_g[l, 0], ln_b[l, 0])
        router_w, router_b = _router_weights(wr_group[l], br_group[l], wr_expert[l], br_expert[l])
        h, hp, table, gates, counts = _cross_attn(h, seq, l, k_all, v_all, wq_x, wo_x,
                                                  ln_g[l, 1], ln_b[l, 1], router_w, router_b)
        yg = _moe_experts(hp, table, counts, l, w1, w3, w2)
        pending = (yg, gates, ln_g[l, 2], ln_b[l, 2])
    return _combine(h, *pending).reshape(bsz, seq, d)
```

```python
import functools
import math

import jax
import jax.numpy as jnp
from jax import lax
from jax.experimental import pallas as pl
from jax.experimental.pallas import tpu as pltpu
from jax.experimental.pallas import tpu_sc as plsc

F32 = jnp.float32
BF16 = jnp.bfloat16
I32 = jnp.int32

POOL_WINDOWS = (2, 4, 8, 16)
assert all(w & (w - 1) == 0 for w in POOL_WINDOWS)
N_SG_HEADS = 4
CHUNK = 128
CONV_WIDTH = 3
N_XHEADS = 4
N_GROUPS = 4
EXPERTS_PER_GROUP = 8
N_EXPERTS = N_GROUPS * EXPERTS_PER_GROUP
TOP_K = 2
DEPTH = 4
ALPHA = (2.0 * DEPTH) ** 0.25
LN_EPS = 1e-5

LANES = 128
SC_CORES = 2
SC_WORKERS = 32
SC_LANES = 16
VMEM_LIMIT = 56 * 1024 * 1024

SC_CHUNK = 64
TM = 1024
SUB_TILES = 2
ROUTE_ROWS = 8
POOL_HALO = 16
CONV_HALO = 8
EXPERT_UNIT = 128
ITEM_UNITS = 4
ROUTE_COLS = 128
STAGE_COLS = 512

_NT = (((1,), (1,)), ((), ()))


def _dot(a, b):
    return jnp.dot(a, b, preferred_element_type=F32)


def _layer_norm(y, g, b):
    mu = jnp.mean(y, axis=-1, keepdims=True)
    yc = y - mu
    var = jnp.mean(yc * yc, axis=-1, keepdims=True)
    return yc * lax.rsqrt(var + LN_EPS) * g + b


def _gelu_tanh(x):
    c = math.sqrt(2.0 / math.pi)
    return 0.5 * x * (1.0 + jnp.tanh(c * (x + 0.044715 * (x * x * x))))


def _pack_halves(v):
    c = v.shape[1] // 2
    lo = pltpu.bitcast(v[:, :c].astype(BF16).astype(F32), jnp.uint32)
    hi = pltpu.bitcast(v[:, c:].astype(BF16).astype(F32), jnp.uint32)
    return pltpu.bitcast((hi & jnp.uint32(0xFFFF0000)) | (lo >> 16), I32)


def _unpack_halves(w):
    u = pltpu.bitcast(w, jnp.uint32)
    return pltpu.bitcast(u << 16, F32), pltpu.bitcast(u & jnp.uint32(0xFFFF0000), F32)


def _load_cast(w_hbm, w_scr, stage, sems):
    chunks = w_scr.shape[1] // STAGE_COLS

    def chunk_copy(c):
        return pltpu.make_async_copy(w_hbm.at[:, pl.ds(c * STAGE_COLS, STAGE_COLS)], stage.at[c % 2],
                                     sems.at[c % 2])

    chunk_copy(0).start()
    for c in range(chunks):
        if c + 1 < chunks:
            chunk_copy(c + 1).start()
        chunk_copy(c).wait()
        w_scr[:, c * STAGE_COLS:(c + 1) * STAGE_COLS] = stage[c % 2].astype(BF16)


def _stage_scratch(rows):
    return [pltpu.VMEM((2, rows, STAGE_COLS), F32), pltpu.SemaphoreType.DMA((2,))]


_HBM = pl.BlockSpec(memory_space=pl.ANY)


def _const_spec(shape):
    nd = len(shape)
    return pl.BlockSpec(shape, lambda i: (0,) * nd)


def _pick_spec(stacked, index):
    rest = stacked.shape[1:]
    return pl.BlockSpec((None,) + rest, lambda i: (index,) + (0,) * len(rest))


def _params():
    return pltpu.CompilerParams(dimension_semantics=("arbitrary",), vmem_limit_bytes=VMEM_LIMIT)


def _moe_output(x_ref, y0_ref, y1_ref, gate_ref, g_ref, b_ref):
    gates = gate_ref[...]
    g0, g1 = gates[:, 0:1], gates[:, 1:2]
    y0_lo, y0_hi = _unpack_halves(y0_ref[...])
    y1_lo, y1_hi = _unpack_halves(y1_ref[...])
    ff = jnp.concatenate([g0 * y0_lo + g1 * y1_lo, g0 * y0_hi + g1 * y1_hi], axis=1)
    return _layer_norm(ALPHA * x_ref[...] + ff, g_ref[...], b_ref[...])


def _mixer_input(src, pending):
    if not pending:
        x_ref, xh_ref = src
        return x_ref[...], xh_ref[...]
    x_ref, xh_ref, y0_ref, y0h_ref, y1_ref, y1h_ref, gate_ref, gateh_ref, g_ref, b_ref = src
    return (_moe_output(x_ref, y0_ref, y1_ref, gate_ref, g_ref, b_ref),
            _moe_output(xh_ref, y0h_ref, y1h_ref, gateh_ref, g_ref, b_ref))


def _mixer_sources(x, pending, halo):
    n, d = x.shape
    tiles = n // TM
    halo_blocks = TM // halo

    def halo_index(i):
        return jnp.maximum(i * halo_blocks - 1, 0)

    specs = [pl.BlockSpec((TM, d), lambda i: (i, 0)), pl.BlockSpec((halo, d), lambda i: (halo_index(i), 0))]
    args = [x, x]
    if pending is not None:
        yg, gates, ln_g, ln_b, ln_row = pending
        specs += [pl.BlockSpec((TM, d // 2), lambda i: (i, 0)),
                  pl.BlockSpec((halo, d // 2), lambda i: (halo_index(i), 0)),
                  pl.BlockSpec((TM, d // 2), lambda i: (i + tiles, 0)),
                  pl.BlockSpec((halo, d // 2), lambda i: (halo_index(i) + tiles * halo_blocks, 0)),
                  pl.BlockSpec((TM, TOP_K), lambda i: (i, 0)),
                  pl.BlockSpec((halo, TOP_K), lambda i: (halo_index(i), 0)),
                  _pick_spec(ln_g, ln_row), _pick_spec(ln_b, ln_row)]
        args += [yg, yg, yg, yg, gates, gates, ln_g, ln_b]
    return specs, args


def _even_kernel(*refs, tiles_per_seq, layer, pending):
    n_src = 10 if pending else 2
    (win_hbm, wpool_ref, pscale_ref, lvg_ref, lvb_ref, ws_ref, bst_ref, wout_hbm, g_ref, b_ref, o_ref,
     a_scr, cat_scr, win_ref, wout_ref, stage, sems) = refs[n_src:]
    tm = o_ref.shape[0]
    d_pool = a_scr.shape[1]
    d_sg = lvg_ref.shape[1]
    pgd = d_pool // len(POOL_WINDOWS)
    hd_dim = d_sg // N_SG_HEADS
    seq_tile = pl.program_id(0) % tiles_per_seq

    @pl.when(pl.program_id(0) == 0)
    def _():
        _load_cast(win_hbm.at[layer], win_ref, stage, sems)
        _load_cast(wout_hbm.at[layer], wout_ref, stage, sems)

    x, xh = _mixer_input(refs[:n_src], pending)
    sub = tm // SUB_TILES

    ah = _dot(xh.astype(BF16), win_ref[:, :d_pool])
    a_scr[0:POOL_HALO, :] = jnp.where(seq_tile == 0, 0.0, ah)
    row = lax.broadcasted_iota(I32, (CHUNK, CHUNK), 0)
    col = lax.broadcasted_iota(I32, (CHUNK, CHUNK), 1)
    ws_masked = [jnp.where(row >= col, ws_ref[hd], 0.0).astype(BF16) for hd in range(N_SG_HEADS)]

    def in_proj(st):
        h = _dot(x[st * sub:(st + 1) * sub, :].astype(BF16), win_ref[...])
        a_scr[POOL_HALO + st * sub:POOL_HALO + (st + 1) * sub, :] = h[:, :d_pool]
        return h[:, d_pool:]

    def branches_out_proj(st, hz):
        base = st * sub
        pos = seq_tile * tm + base + lax.broadcasted_iota(I32, (sub, 1), 0)
        for g, w in enumerate(POOL_WINDOWS):
            cs = slice(g * pgd, (g + 1) * pgd)
            tok = a_scr[POOL_HALO + base:POOL_HALO + base + sub, cs]
            acc = a_scr[base:POOL_HALO + base + sub, cs]
            span = 1
            while span < w:
                acc = acc[span:, :] + acc[:-span, :]
                span *= 2
            acc = acc[acc.shape[0] - sub:, :]
            cnt = jnp.minimum(pos + 1, w).astype(F32)
            dev = acc * (1.0 / cnt) - tok
            yg = _dot(dev.astype(BF16), wpool_ref[g])
            cat_scr[base:base + sub, cs] = (yg * pscale_ref[:, cs]).astype(BF16)

        z = _gelu_tanh(hz)
        u = z[:, :d_sg]
        v = _layer_norm(z[:, d_sg:], lvg_ref[...], lvb_ref[...]).astype(BF16)
        for hd in range(N_SG_HEADS):
            hs = slice(hd * hd_dim, (hd + 1) * hd_dim)
            bcol = bst_ref[:, hd:hd + 1]
            for ck in range(sub // CHUNK):
                rs = slice(ck * CHUNK, (ck + 1) * CHUNK)
                sv = _dot(ws_masked[hd], v[rs, hs]) + bcol
                cat_scr[base + ck * CHUNK:base + (ck + 1) * CHUNK,
                        d_pool + hd * hd_dim:d_pool + (hd + 1) * hd_dim] = (u[rs, hs] * sv).astype(BF16)
        return _dot(cat_scr[base:base + sub, :], wout_ref[...])

    hzs = [in_proj(st) for st in range(SUB_TILES)]
    mixes = [branches_out_proj(st, hzs[st]) for st in range(SUB_TILES)]
    for st in range(SUB_TILES):
        rs = slice(st * sub, (st + 1) * sub)
        o_ref[rs, :] = _layer_norm(ALPHA * x[rs, :] + mixes[st], g_ref[...], b_ref[...])


def _even_mixer(x, pending, seq, layer, w_in, w_pool, pool_scale, ln_v_g, ln_v_b, w_spatial, b_spatial_t,
                w_out, ln_g, ln_b, ln_row):
    n, d = x.shape
    d_in = w_in.shape[2]
    d_pool = pool_scale.shape[2]
    d_sg = ln_v_g.shape[2]
    kern = functools.partial(_even_kernel, tiles_per_seq=seq // TM, layer=layer, pending=pending is not None)
    src_specs, src_args = _mixer_sources(x, pending, POOL_HALO)
    return pl.pallas_call(
        kern,
        out_shape=jax.ShapeDtypeStruct((n, d), F32),
        grid=(n // TM,),
        in_specs=src_specs + [
            _HBM,
            _pick_spec(w_pool, layer),
            _pick_spec(pool_scale, layer),
            _pick_spec(ln_v_g, layer),
            _pick_spec(ln_v_b, layer),
            _pick_spec(w_spatial, layer),
            _pick_spec(b_spatial_t, layer),
            _HBM,
            _pick_spec(ln_g, ln_row),
            _pick_spec(ln_b, ln_row),
        ],
        out_specs=pl.BlockSpec((TM, d), lambda i: (i, 0)),
        scratch_shapes=[pltpu.VMEM((POOL_HALO + TM, d_pool), F32), pltpu.VMEM((TM, d_pool + d_sg), BF16),
                        pltpu.VMEM((d, d_in), BF16), pltpu.VMEM((d_pool + d_sg, d), BF16)] + _stage_scratch(d),
        compiler_params=_params(),
        name="even_mixer",
    )(*src_args, w_in, w_pool, pool_scale, ln_v_g, ln_v_b, w_spatial, b_spatial_t, w_out, ln_g, ln_b)


def _odd_kernel(*refs, tiles_per_seq, layer, pending):
    n_src = 10 if pending else 2
    (win_hbm, cwt_ref, cb_ref, wout_hbm, g_ref, b_ref, o_ref, zc_scr,
     win_ref, wout_ref, stage, sems) = refs[n_src:]
    tm, d = o_ref.shape
    seq_tile = pl.program_id(0) % tiles_per_seq

    @pl.when(pl.program_id(0) == 0)
    def _():
        _load_cast(win_hbm.at[layer], win_ref, stage, sems)
        _load_cast(wout_hbm.at[layer], wout_ref, stage, sems)

    x, xh = _mixer_input(refs[:n_src], pending)
    hh = _dot(xh.astype(BF16), win_ref[:, d:])
    zc_scr[0:CONV_HALO, :] = jnp.where(seq_tile == 0, 0.0, hh[:, :d] * hh[:, d:])
    sub = tm // SUB_TILES

    def in_proj(st):
        xb = x[st * sub:(st + 1) * sub, :].astype(BF16)
        hc = _dot(xb, win_ref[:, d:2 * d])
        hz = _dot(xb, win_ref[:, 2 * d:])
        zc_scr[CONV_HALO + st * sub:CONV_HALO + (st + 1) * sub, :] = hc * hz
        return _dot(xb, win_ref[:, :d])

    def conv_out_proj(st, gate):
        conv = cb_ref[...]
        for j in range(CONV_WIDTH):
            off = CONV_HALO + st * sub - (CONV_WIDTH - 1) + j
            conv = conv + zc_scr[off:off + sub, :] * cwt_ref[j:j + 1, :]
        return _dot((gate * conv).astype(BF16), wout_ref[...])

    gates = [in_proj(st) for st in range(SUB_TILES)]
    ys = [conv_out_proj(st, gates[st]) for st in range(SUB_TILES)]
    for st in range(SUB_TILES):
        rs = slice(st * sub, (st + 1) * sub)
        o_ref[rs, :] = _layer_norm(ALPHA * x[rs, :] + ys[st], g_ref[...], b_ref[...])


def _odd_mixer(x, pending, seq, layer, w_in, conv_w_t, conv_b, w_out, ln_g, ln_b, ln_row):
    n, d = x.shape
    kern = functools.partial(_odd_kernel, tiles_per_seq=seq // TM, layer=layer, pending=pending is not None)
    src_specs, src_args = _mixer_sources(x, pending, CONV_HALO)
    return pl.pallas_call(
        kern,
        out_shape=jax.ShapeDtypeStruct((n, d), F32),
        grid=(n // TM,),
        in_specs=src_specs + [
            _HBM,
            _pick_spec(conv_w_t, layer),
            _pick_spec(conv_b, layer),
            _HBM,
            _pick_spec(ln_g, ln_row),
            _pick_spec(ln_b, ln_row),
        ],
        out_specs=pl.BlockSpec((TM, d), lambda i: (i, 0)),
        scratch_shapes=[pltpu.VMEM((CONV_HALO + TM, d), F32),
                        pltpu.VMEM(w_in.shape[1:], BF16), pltpu.VMEM(w_out.shape[1:], BF16)] + _stage_scratch(d),
        compiler_params=_params(),
        name="odd_mixer",
    )(*src_args, w_in, conv_w_t, conv_b, w_out, ln_g, ln_b)


def _kv_kernel(mem_ref, wk_ref, wv_ref, k_ref, v_ref):
    m = mem_ref[...].astype(BF16)
    k_ref[...] = _dot(m, wk_ref[...].astype(BF16)).astype(BF16)
    v_ref[...] = _dot(m, wv_ref[...].astype(BF16)).astype(BF16)


def _memory_kv(mem2d, wk, wv):
    nl, d, _ = wk.shape
    rows = mem2d.shape[0]
    out = jax.ShapeDtypeStruct((nl, rows, d), BF16)
    wspec = pl.BlockSpec((None, d, d), lambda l: (l, 0, 0))
    ospec = pl.BlockSpec((None, rows, d), lambda l: (l, 0, 0))
    return pl.pallas_call(
        _kv_kernel,
        out_shape=(out, out),
        grid=(nl,),
        in_specs=[_const_spec((rows, d)), wspec, wspec],
        out_specs=(ospec, ospec),
        compiler_params=_params(),
        name="memory_kv",
    )(mem2d, wk, wv)


def _attn_kernel(x_ref, k_ref, v_ref, wq_hbm, wo_hbm, g_ref, b_ref, wr_ref, br_ref,
                 o_ref, op_ref, rt_ref, rg_ref, cnt_ref, o_scr, carry_scr, wq_ref, wo_ref, stage, sems,
                 *, layer):
    tm, d = x_ref.shape
    hd_dim = d // N_XHEADS
    sub = tm // SUB_TILES

    @pl.when(pl.program_id(0) == 0)
    def _():
        carry_scr[...] = jnp.zeros_like(carry_scr)
        _load_cast(wq_hbm.at[layer], wq_ref, stage, sems)
        _load_cast(wo_hbm.at[layer], wo_ref, stage, sems)

    row_slices = [slice(st * sub, (st + 1) * sub) for st in range(SUB_TILES)]
    qs = [_dot(x_ref[rs, :].astype(BF16), wq_ref[...]) * (1.0 / math.sqrt(hd_dim)) for rs in row_slices]
    for rs, q in zip(row_slices, qs):
        for hd in range(N_XHEADS):
            hs = slice(hd * hd_dim, (hd + 1) * hd_dim)
            s = lax.dot_general(q[:, hs].astype(BF16), k_ref[:, hs], _NT, preferred_element_type=F32)
            p = jnp.exp(s - jnp.max(s, axis=-1, keepdims=True))
            p = p * (1.0 / jnp.sum(p, axis=-1, keepdims=True))
            o_scr[rs, hs] = _dot(p.astype(BF16), v_ref[:, hs]).astype(BF16)
    xas = [_dot(o_scr[rs, :], wo_ref[...]) for rs in row_slices]
    for rs, xa in zip(row_slices, xas):
        out = _layer_norm(ALPHA * x_ref[rs, :] + xa, g_ref[...], b_ref[...])
        o_ref[rs, :] = out
        op_ref[rs, :] = _pack_halves(out)
        table, gates = _route_rows(out, wr_ref, br_ref, carry_scr)
        rt_ref[:, rs] = table
        rg_ref[rs, :] = gates
    cnt_ref[...] = carry_scr[...].astype(I32)


def _cross_attn(x, seq, layer, k, v, wq, wo, ln_g, ln_b, ln_row, router_w, router_b):
    n, d = x.shape
    m = k.shape[2]
    tiles_per_seq = seq // TM
    kvspec = pl.BlockSpec((None, None, m, d), lambda i: (layer, i // tiles_per_seq, 0, 0))
    out, packed, table, gates, cnt = pl.pallas_call(
        functools.partial(_attn_kernel, layer=layer),
        out_shape=(jax.ShapeDtypeStruct((n, d), F32), jax.ShapeDtypeStruct((n, d // 2), I32),
                   jax.ShapeDtypeStruct((ROUTE_ROWS, n), F32), jax.ShapeDtypeStruct((n, TOP_K), F32),
                   jax.ShapeDtypeStruct((1, ROUTE_COLS), I32)),
        grid=(n // TM,),
        in_specs=[
            pl.BlockSpec((TM, d), lambda i: (i, 0)),
            kvspec, kvspec,
            _HBM, _HBM,
            _pick_spec(ln_g, ln_row), _pick_spec(ln_b, ln_row),
            _pick_spec(router_w, layer), _pick_spec(router_b, layer),
        ],
        out_specs=(pl.BlockSpec((TM, d), lambda i: (i, 0)), pl.BlockSpec((TM, d // 2), lambda i: (i, 0)),
                   pl.BlockSpec((ROUTE_ROWS, TM), lambda i: (0, i)), pl.BlockSpec((TM, TOP_K), lambda i: (i, 0)),
                   _const_spec((1, ROUTE_COLS))),
        scratch_shapes=[pltpu.VMEM((TM, d), BF16), pltpu.VMEM((1, ROUTE_COLS), F32),
                        pltpu.VMEM((d, d), BF16), pltpu.VMEM((d, d), BF16)] + _stage_scratch(d),
        compiler_params=_params(),
        name="cross_attn",
    )(x, k, v, wq, wo, ln_g, ln_b, router_w, router_b)
    return out, packed, table, gates, cnt[0, N_GROUPS:N_GROUPS + N_EXPERTS]


def _route_rows(x, wcat_ref, bias_ref, carry_scr):
    tm = x.shape[0]

    xh = x.astype(BF16)
    xl = (x - xh.astype(F32)).astype(BF16)
    r1 = _dot(xh, wcat_ref[...])
    r2 = _dot(xl, wcat_ref[:, :ROUTE_COLS])
    logits = r1[:, :ROUTE_COLS] + r1[:, ROUTE_COLS:] + r2 + bias_ref[...]

    lane = lax.broadcasted_iota(I32, (tm, ROUTE_COLS), 1).astype(F32)
    neg = -jnp.inf

    def first_argmax(vals):
        mx = jnp.max(vals, axis=-1, keepdims=True)
        idx = jnp.min(jnp.where(vals == mx, lane, float(ROUTE_COLS)), axis=-1, keepdims=True)
        return mx, idx

    gl = jnp.where(lane < N_GROUPS, logits, neg)
    gmax, g_sel = first_argmax(gl)
    gate_g = 1.0 / jnp.sum(jnp.exp(gl - gmax), axis=-1, keepdims=True)

    lo = N_GROUPS + g_sel * EXPERTS_PER_GROUP
    el = jnp.where((lane >= lo) & (lane < lo + EXPERTS_PER_GROUP), logits, neg)
    m1, i1 = first_argmax(el)
    m2, i2 = first_argmax(jnp.where(lane == i1, neg, el))
    e21 = jnp.exp(m2 - m1)
    w1 = 1.0 / (1.0 + e21)
    w2 = e21 / (1.0 + e21)

    oh1 = lane == i1
    oh2 = lane == i2
    oh = (oh1 | oh2).astype(BF16)
    r = lax.broadcasted_iota(I32, (tm, tm), 0)
    c = lax.broadcasted_iota(I32, (tm, tm), 1)
    before = _dot((r > c).astype(BF16), oh) + carry_scr[...]
    rank1 = jnp.sum(jnp.where(oh1, before, 0.0), axis=-1, keepdims=True)
    rank2 = jnp.sum(jnp.where(oh2, before, 0.0), axis=-1, keepdims=True)
    carry_scr[...] += jnp.sum(oh.astype(F32), axis=0, keepdims=True)

    cols = jnp.where(lane == 0.0, i1 - N_GROUPS,
                     jnp.where(lane == 1.0, i2 - N_GROUPS,
                               jnp.where(lane == 2.0, rank1, jnp.where(lane == 3.0, rank2, 0.0))))
    table = jnp.transpose(cols)[:ROUTE_ROWS, :]
    l2 = lax.broadcasted_iota(I32, (tm, TOP_K), 1)
    gates = jnp.where(l2 == 0, gate_g * w1, gate_g * w2)
    return table, gates


def _router_weights(wr_g, br_g, wr_e, br_e):
    nl, d, _ = wr_g.shape
    w = jnp.concatenate([wr_g, jnp.transpose(wr_e, (0, 2, 1, 3)).reshape(nl, d, N_EXPERTS)], axis=2)
    w = jnp.pad(w, ((0, 0), (0, 0), (0, ROUTE_COLS - w.shape[2])))
    w_hi = w.astype(BF16)
    w_lo = (w - w_hi.astype(F32)).astype(BF16)
    bias = jnp.pad(jnp.concatenate([br_g, br_e.reshape(nl, -1)], axis=1),
                   ((0, 0), (0, ROUTE_COLS - N_GROUPS - N_EXPERTS)))
    return jnp.concatenate([w_hi, w_lo], axis=2), bias[:, None, :]


def _sc_worker_rows(rows):
    per_worker = rows // SC_WORKERS
    n_chunks = per_worker // SC_CHUNK
    assert per_worker * SC_WORKERS == rows and n_chunks * SC_CHUNK == per_worker and n_chunks % 2 == 0
    return per_worker, n_chunks


def _sc_gather_rows(table_hbm, out_hbm, idx_v, rows_v, gsem, wsem, base, n_chunks):
    def fetch(c, slot):
        off = pl.multiple_of(c * SC_CHUNK, SC_CHUNK)
        return pltpu.make_async_copy(table_hbm.at[idx_v.at[pl.ds(off, SC_CHUNK)]], rows_v.at[slot],
                                     gsem.at[slot])

    def put(c, slot):
        off = pl.multiple_of(c * SC_CHUNK, SC_CHUNK)
        return pltpu.make_async_copy(rows_v.at[slot], out_hbm.at[pl.ds(base + off, SC_CHUNK)], wsem.at[slot])

    fetch(0, 0).start()

    @pl.loop(0, n_chunks, step=2)
    def _(c0):
        for slot in range(2):
            c = c0 + slot

            @pl.when(c + 1 < n_chunks)
            def _():
                @pl.when(c >= 1)
                def _():
                    put(c - 1, 1 - slot).wait()
                fetch(c + 1, 1 - slot).start()

            fetch(c, slot).wait()
            put(c, slot).start()

    put(n_chunks - 2, 0).wait()
    put(n_chunks - 1, 1).wait()


def _sc_row_scratch(per_worker, d, dtype):
    return [pltpu.VMEM((per_worker,), I32), pltpu.VMEM((2, SC_CHUNK, d), dtype),
            pltpu.SemaphoreType.DMA((2,)), pltpu.SemaphoreType.DMA((2,))]


def _sc_gather(table, idx):
    b = idx.shape[0]
    d = table.shape[1]
    per_worker, n_chunks = _sc_worker_rows(b)
    mesh = plsc.VectorSubcoreMesh(core_axis_name="c", subcore_axis_name="s")

    @functools.partial(
        pl.kernel, mesh=mesh,
        out_type=jax.ShapeDtypeStruct((b, d), table.dtype),
        scratch_types=_sc_row_scratch(per_worker, d, table.dtype),
        name="sc_gather",
    )
    def gather(table_hbm, idx_hbm, out_hbm, idx_v, rows_v, gsem, wsem):
        base = (lax.axis_index("s") * SC_CORES + lax.axis_index("c")) * per_worker
        pltpu.sync_copy(idx_hbm.at[pl.ds(base, per_worker)], idx_v)
        _sc_gather_rows(table_hbm, out_hbm, idx_v, rows_v, gsem, wsem, base, n_chunks)

    return gather(table, idx)


def _sc_dispatch(table, dest_flat, rows):
    n, d = table.shape
    a = dest_flat.shape[0]
    lanes = SC_LANES
    per_worker, n_chunks = _sc_worker_rows(rows)
    assert a % lanes == 0 and per_worker % lanes == 0
    mesh = plsc.VectorSubcoreMesh(core_axis_name="c", subcore_axis_name="s")

    @functools.partial(
        pl.kernel, mesh=mesh,
        out_type=jax.ShapeDtypeStruct((rows, d), table.dtype),
        scratch_types=[pltpu.VMEM((a,), I32)] + _sc_row_scratch(per_worker, d, table.dtype),
        compiler_params=pltpu.CompilerParams(needs_layout_passes=False),
        name="sc_dispatch",
    )
    def dispatch(table_hbm, dest_hbm, out_hbm, dest_v, idx_v, rows_v, gsem, wsem):
        base = (lax.axis_index("s") * SC_CORES + lax.axis_index("c")) * per_worker
        pltpu.sync_copy(dest_hbm, dest_v)
        lane = lax.iota(I32, lanes)

        @pl.loop(0, per_worker // lanes)
        def _(i):
            idx_v[pl.ds(i * lanes, lanes)] = lax.rem(base + i * lanes + lane, n)

        @plsc.parallel_loop(0, a // lanes, unroll=8)
        def _(i):
            local = dest_v[pl.ds(i * lanes, lanes)] - base
            mine = (local >= 0) & (local < per_worker)
            plsc.store_scatter(idx_v, [jnp.where(mine, local, 0)], lax.rem(i * lanes + lane, n), mask=mine)

        _sc_gather_rows(table_hbm, out_hbm, idx_v, rows_v, gsem, wsem, base, n_chunks)

    return dispatch(table, dest_flat)


def _expert_kernel(sched_ref, ni_ref, xs_hbm, w1_hbm, w3_hbm, w2_hbm, y_hbm,
                   x_buf, y_buf, w1_buf, w3_buf, w2_buf, w1_scr, w3_scr, w2_scr, wsems, xsems, ysems,
                   *, layer):
    i = pl.program_id(0)
    n_items = ni_ref[0]
    expert, wslot, run_start, next_expert = (sched_ref[r, i] for r in range(4))
    slot = i % 2

    def for_units(item, fn):
        for units in range(1, ITEM_UNITS + 1):
            @pl.when(sched_ref[5, item] == units)
            def _():
                fn(units * EXPERT_UNIT)

    def x_copy(item, s, rows):
        row0 = pl.multiple_of(sched_ref[4, item], EXPERT_UNIT)
        return pltpu.make_async_copy(xs_hbm.at[pl.ds(row0, rows)], x_buf.at[s, pl.ds(0, rows)], xsems.at[s])

    def y_copy(item, s, rows):
        row0 = pl.multiple_of(sched_ref[4, item], EXPERT_UNIT)
        return pltpu.make_async_copy(y_buf.at[s, pl.ds(0, rows)], y_hbm.at[pl.ds(row0, rows)], ysems.at[s])

    def fetch(e, s):
        return [pltpu.make_async_copy(w_hbm.at[layer, e], buf.at[s], wsems.at[s, j])
                for j, (w_hbm, buf) in enumerate(((w1_hbm, w1_buf), (w3_hbm, w3_buf), (w2_hbm, w2_buf)))]

    def mlp(rows):
        x_lo, x_hi = _unpack_halves(x_buf[slot, 0:rows, :])
        xb = jnp.concatenate([x_lo.astype(BF16), x_hi.astype(BF16)], axis=1)
        h1 = _dot(xb, w1_scr[...])
        h3 = _dot(xb, w3_scr[...])
        hid = h1 * (1.0 / (1.0 + jnp.exp(-h1))) * h3
        y_buf[slot, 0:rows, :] = _pack_halves(_dot(hid.astype(BF16), w2_scr[...]))
        y_copy(i, slot, rows).start()

    @pl.when(i < n_items)
    def _():
        @pl.when(i == 0)
        def _():
            for_units(0, lambda rows: x_copy(0, 0, rows).start())

        for_units(i, lambda rows: x_copy(i, slot, rows).wait())

        @pl.when(i + 1 < n_items)
        def _():
            for_units(i + 1, lambda rows: x_copy(i + 1, 1 - slot, rows).start())

        @pl.when(run_start == 1)
        def _():
            @pl.when(i == 0)
            def _():
                for c in fetch(expert, wslot):
                    c.start()

            for c in fetch(expert, wslot):
                c.wait()

            @pl.when(next_expert >= 0)
            def _():
                for c in fetch(next_expert, 1 - wslot):
                    c.start()

            w1_scr[...] = w1_buf[wslot].astype(BF16)
            w3_scr[...] = w3_buf[wslot].astype(BF16)
            w2_scr[...] = w2_buf[wslot].astype(BF16)

        @pl.when(i >= 2)
        def _():
            for_units(i - 2, lambda rows: y_copy(i - 2, slot, rows).wait())

        for_units(i, mlp)

        @pl.when(i == n_items - 1)
        def _():
            @pl.when(i >= 1)
            def _():
                for_units(i - 1, lambda rows: y_copy(i - 1, 1 - slot, rows).wait())

            for_units(i, lambda rows: y_copy(i, slot, rows).wait())
            y_buf[0, 0:EXPERT_UNIT, :] = jnp.zeros((EXPERT_UNIT, y_buf.shape[2]), y_buf.dtype)
            first_free = (sched_ref[4, i] + sched_ref[5, i] * EXPERT_UNIT) // EXPERT_UNIT

            def zero_copy(u):
                return pltpu.make_async_copy(
                    y_buf.at[0, pl.ds(0, EXPERT_UNIT)],
                    y_hbm.at[pl.ds(pl.multiple_of(u * EXPERT_UNIT, EXPERT_UNIT), EXPERT_UNIT)], ysems.at[0])

            def start_zero(u, _):
                zero_copy(u).start()
                return 0

            def wait_zero(u, _):
                zero_copy(u).wait()
                return 0

            lax.fori_loop(first_free, y_hbm.shape[0] // EXPERT_UNIT, start_zero, 0)
            lax.fori_loop(first_free, y_hbm.shape[0] // EXPERT_UNIT, wait_zero, 0)


def _expert_mlp(xs, schedule, n_items, layer, w1, w3, w2):
    d, de = w1.shape[2], w1.shape[3]
    max_rows = ITEM_UNITS * EXPERT_UNIT
    return pl.pallas_call(
        functools.partial(_expert_kernel, layer=layer),
        out_shape=jax.ShapeDtypeStruct(xs.shape, I32),
        grid_spec=pltpu.PrefetchScalarGridSpec(
            num_scalar_prefetch=2,
            grid=(schedule.shape[1],),
            in_specs=[_HBM, _HBM, _HBM, _HBM],
            out_specs=_HBM,
            scratch_shapes=[pltpu.VMEM((2, max_rows, d // 2), I32), pltpu.VMEM((2, max_rows, d // 2), I32),
                            pltpu.VMEM((2, d, de), F32), pltpu.VMEM((2, d, de), F32), pltpu.VMEM((2, de, d), F32),
                            pltpu.VMEM((d, de), BF16), pltpu.VMEM((d, de), BF16), pltpu.VMEM((de, d), BF16),
                            pltpu.SemaphoreType.DMA((2, 3)), pltpu.SemaphoreType.DMA((2,)),
                            pltpu.SemaphoreType.DMA((2,))],
        ),
        compiler_params=_params(),
        name="expert_mlp",
    )(schedule, n_items, xs, w1, w3, w2)


def _combine_kernel(x_ref, y0_ref, y1_ref, gate_ref, g_ref, b_ref, o_ref):
    o_ref[...] = _moe_output(x_ref, y0_ref, y1_ref, gate_ref, g_ref, b_ref)


def _combine(x, yg, gates, ln_g, ln_b, ln_row):
    n, d = x.shape
    tiles = n // TM
    return pl.pallas_call(
        _combine_kernel,
        out_shape=jax.ShapeDtypeStruct((n, d), F32),
        grid=(tiles,),
        in_specs=[pl.BlockSpec((TM, d), lambda i: (i, 0)),
                  pl.BlockSpec((TM, d // 2), lambda i: (i, 0)),
                  pl.BlockSpec((TM, d // 2), lambda i: (i + tiles, 0)),
                  pl.BlockSpec((TM, TOP_K), lambda i: (i, 0)),
                  _pick_spec(ln_g, ln_row), _pick_spec(ln_b, ln_row)],
        out_specs=pl.BlockSpec((TM, d), lambda i: (i, 0)),
        compiler_params=_params(),
        name="combine",
    )(x, yg, yg, gates, ln_g, ln_b)


def _moe_experts(x_packed, table, counts, layer, w1, w3, w2):
    n = x_packed.shape[0]
    max_units = (n * TOP_K + N_EXPERTS * (EXPERT_UNIT - 1) + EXPERT_UNIT - 1) // EXPERT_UNIT
    max_items = (max_units + N_EXPERTS * (ITEM_UNITS - 1) + ITEM_UNITS - 1) // ITEM_UNITS
    experts = table[:TOP_K].astype(I32)
    ranks = table[TOP_K:2 * TOP_K].astype(I32)

    units_e = (counts + EXPERT_UNIT - 1) // EXPERT_UNIT
    units_start = jnp.cumsum(units_e) - units_e
    items_e = (units_e + ITEM_UNITS - 1) // ITEM_UNITS
    items_end = jnp.cumsum(items_e)
    n_items = items_end[-1:].astype(I32)
    item_ids = jnp.arange(max_items, dtype=I32)
    expert_ids = jnp.arange(N_EXPERTS, dtype=I32)
    item_expert = jnp.minimum(jnp.sum(items_end[None, :] <= item_ids[:, None], axis=1), N_EXPERTS - 1)
    later = (expert_ids[None, :] > expert_ids[:, None]) & (items_e[None, :] > 0)
    next_run = jnp.min(jnp.where(later, expert_ids[None, :], N_EXPERTS), axis=1)
    next_run = jnp.where(next_run == N_EXPERTS, -1, next_run)
    per_expert = jnp.stack([items_end - items_e, units_start, units_e, next_run], axis=1)
    mine = (item_expert[:, None] == expert_ids[None, :])[:, :, None]
    first_item, unit0, units, next_expert = jnp.sum(jnp.where(mine, per_expert[None], 0), axis=1).T
    within = item_ids - first_item
    item_row0 = (unit0 + ITEM_UNITS * within) * EXPERT_UNIT
    item_units = jnp.clip(units - ITEM_UNITS * within, 1, ITEM_UNITS)
    run_start_flag = ((within == 0) & (item_ids < n_items[0])).astype(I32)
    slot = (jnp.cumsum(run_start_flag) - 1) % 2
    schedule = jnp.stack([item_expert, slot, run_start_flag, next_expert, item_row0, item_units]).astype(I32)
    start_of = jnp.sum(jnp.where(experts[:, :, None] == expert_ids, units_start * EXPERT_UNIT, 0), axis=-1)
    dest = (start_of + ranks).astype(I32).reshape(-1)
    xs = _sc_dispatch(x_packed, dest, max_units * EXPERT_UNIT)
    y = _expert_mlp(xs, schedule, n_items, layer, w1, w3, w2)
    return _sc_gather(y, dest)


def kernel(x, mem, w_in_even, w_pool, pool_scale, ln_v_g, ln_v_b, w_spatial, b_spatial, w_out_even,
           w_in_odd, conv_w, conv_b, w_out_odd, wq_x, wk_x, wv_x, wo_x, ln_g, ln_b, wr_group,
           br_group, wr_expert, br_expert, w1, w3, w2):
    bsz, seq, d = x.shape
    assert seq % TM == 0 and d % LANES == 0
    mlen = mem.shape[1]
    k_all, v_all = _memory_kv(mem.reshape(bsz * mlen, d), wk_x, wv_x)
    k_all = k_all.reshape(DEPTH, bsz, mlen, d)
    v_all = v_all.reshape(DEPTH, bsz, mlen, d)
    ln_g = ln_g.reshape(DEPTH * 3, 1, d)
    ln_b = ln_b.reshape(DEPTH * 3, 1, d)
    w_pool = w_pool.astype(BF16)
    pool_scale, ln_v_g, ln_v_b, conv_b = (p[:, None, :] for p in (pool_scale, ln_v_g, ln_v_b, conv_b))
    b_spatial_t = jnp.swapaxes(b_spatial, 1, 2)
    conv_w_t = jnp.swapaxes(conv_w, 1, 2)
    router_w, router_b = _router_weights(wr_group, br_group, wr_expert, br_expert)

    h = x.reshape(bsz * seq, d)
    pending = None
    for l in range(DEPTH):
        i = l // 2
        if l % 2 == 0:
            h = _even_mixer(h, pending, seq, i, w_in_even, w_pool, pool_scale, ln_v_g, ln_v_b,
                            w_spatial, b_spatial_t, w_out_even, ln_g, ln_b, 3 * l)
        else:
            h = _odd_mixer(h, pending, seq, i, w_in_odd, conv_w_t, conv_b, w_out_odd, ln_g, ln_b, 3 * l)
        h, hp, table, gates, counts = _cross_attn(h, seq, l, k_all, v_all, wq_x, wo_x,
                                                  ln_g, ln_b, 3 * l + 1, router_w, router_b)
        yg = _moe_experts(hp, table, counts, l, w1, w3, w2)
        pending = (yg, gates, ln_g, ln_b, 3 * l + 2)
    return _combine(h, *pending).reshape(bsz, seq, d)
```

```python
import functools
import math

import jax
import jax.numpy as jnp
from jax import lax
from jax.experimental import pallas as pl
from jax.experimental.pallas import tpu as pltpu
from jax.experimental.pallas import tpu_sc as plsc

F32 = jnp.float32
BF16 = jnp.bfloat16
I32 = jnp.int32

POOL_WINDOWS = (2, 4, 8, 16)
assert all(w & (w - 1) == 0 for w in POOL_WINDOWS)
N_SG_HEADS = 4
CHUNK = 128
CONV_WIDTH = 3
N_XHEADS = 4
N_GROUPS = 4
EXPERTS_PER_GROUP = 8
N_EXPERTS = N_GROUPS * EXPERTS_PER_GROUP
TOP_K = 2
DEPTH = 4
ALPHA = (2.0 * DEPTH) ** 0.25
LN_EPS = 1e-5

LANES = 128
SC_CORES = 2
SC_WORKERS = 32
SC_LANES = 16
VMEM_LIMIT = 56 * 1024 * 1024

SC_CHUNK = 64
TM = 1024
SUB_TILES = 2
ROUTE_ROWS = 8
POOL_HALO = 16
CONV_HALO = 8
EXPERT_UNIT = 128
ITEM_UNITS = 4
ROUTE_COLS = 128
STAGE_COLS = 512

_NT = (((1,), (1,)), ((), ()))


def _dot(a, b):
    return jnp.dot(a, b, preferred_element_type=F32)


def _layer_norm(y, g, b):
    mu = jnp.mean(y, axis=-1, keepdims=True)
    yc = y - mu
    var = jnp.mean(yc * yc, axis=-1, keepdims=True)
    return yc * lax.rsqrt(var + LN_EPS) * g + b


def _gelu_tanh(x):
    c = math.sqrt(2.0 / math.pi)
    return 0.5 * x * (1.0 + jnp.tanh(c * (x + 0.044715 * (x * x * x))))


def _pack_halves(v):
    c = v.shape[1] // 2
    lo = pltpu.bitcast(v[:, :c].astype(BF16).astype(F32), jnp.uint32)
    hi = pltpu.bitcast(v[:, c:].astype(BF16).astype(F32), jnp.uint32)
    return pltpu.bitcast((hi & jnp.uint32(0xFFFF0000)) | (lo >> 16), I32)


def _unpack_halves(w):
    u = pltpu.bitcast(w, jnp.uint32)
    return pltpu.bitcast(u << 16, F32), pltpu.bitcast(u & jnp.uint32(0xFFFF0000), F32)


def _load_cast(w_hbm, w_scr, stage, sems):
    chunks = w_scr.shape[1] // STAGE_COLS

    def chunk_copy(c):
        return pltpu.make_async_copy(w_hbm.at[:, pl.ds(c * STAGE_COLS, STAGE_COLS)], stage.at[c % 2],
                                     sems.at[c % 2])

    chunk_copy(0).start()
    for c in range(chunks):
        if c + 1 < chunks:
            chunk_copy(c + 1).start()
        chunk_copy(c).wait()
        w_scr[:, c * STAGE_COLS:(c + 1) * STAGE_COLS] = stage[c % 2].astype(BF16)


def _stage_scratch(rows):
    return [pltpu.VMEM((2, rows, STAGE_COLS), F32), pltpu.SemaphoreType.DMA((2,))]


_HBM = pl.BlockSpec(memory_space=pl.ANY)


def _const_spec(shape):
    nd = len(shape)
    return pl.BlockSpec(shape, lambda i: (0,) * nd)


def _pick_spec(stacked, index):
    rest = stacked.shape[1:]
    return pl.BlockSpec((None,) + rest, lambda i: (index,) + (0,) * len(rest))


def _params():
    return pltpu.CompilerParams(dimension_semantics=("arbitrary",), vmem_limit_bytes=VMEM_LIMIT)


def _moe_output(x_ref, y0_ref, y1_ref, gate_ref, g_ref, b_ref):
    gates = gate_ref[...]
    g0, g1 = gates[:, 0:1], gates[:, 1:2]
    y0_lo, y0_hi = _unpack_halves(y0_ref[...])
    y1_lo, y1_hi = _unpack_halves(y1_ref[...])
    ff = jnp.concatenate([g0 * y0_lo + g1 * y1_lo, g0 * y0_hi + g1 * y1_hi], axis=1)
    return _layer_norm(ALPHA * x_ref[...] + ff, g_ref[...], b_ref[...])


def _mixer_input(src, pending):
    if not pending:
        x_ref, xh_ref = src
        return x_ref[...], xh_ref[...]
    x_ref, xh_ref, y0_ref, y0h_ref, y1_ref, y1h_ref, gate_ref, gateh_ref, g_ref, b_ref = src
    return (_moe_output(x_ref, y0_ref, y1_ref, gate_ref, g_ref, b_ref),
            _moe_output(xh_ref, y0h_ref, y1h_ref, gateh_ref, g_ref, b_ref))


def _mixer_sources(x, pending, halo):
    n, d = x.shape
    tiles = n // TM
    halo_blocks = TM // halo

    def halo_index(i):
        return jnp.maximum(i * halo_blocks - 1, 0)

    specs = [pl.BlockSpec((TM, d), lambda i: (i, 0)), pl.BlockSpec((halo, d), lambda i: (halo_index(i), 0))]
    args = [x, x]
    if pending is not None:
        yg, gates, ln_g, ln_b, ln_row = pending
        specs += [pl.BlockSpec((TM, d // 2), lambda i: (i, 0)),
                  pl.BlockSpec((halo, d // 2), lambda i: (halo_index(i), 0)),
                  pl.BlockSpec((TM, d // 2), lambda i: (i + tiles, 0)),
                  pl.BlockSpec((halo, d // 2), lambda i: (halo_index(i) + tiles * halo_blocks, 0)),
                  pl.BlockSpec((TM, TOP_K), lambda i: (i, 0)),
                  pl.BlockSpec((halo, TOP_K), lambda i: (halo_index(i), 0)),
                  _pick_spec(ln_g, ln_row), _pick_spec(ln_b, ln_row)]
        args += [yg, yg, yg, yg, gates, gates, ln_g, ln_b]
    return specs, args


def _even_kernel(*refs, tiles_per_seq, layer, pending):
    n_src = 10 if pending else 2
    (win_hbm, wpool_ref, pscale_ref, lvg_ref, lvb_ref, ws_ref, bst_ref, wout_hbm, g_ref, b_ref, o_ref,
     a_scr, cat_scr, win_ref, wout_ref, stage, sems) = refs[n_src:]
    tm = o_ref.shape[0]
    d_pool = a_scr.shape[1]
    d_sg = lvg_ref.shape[1]
    pgd = d_pool // len(POOL_WINDOWS)
    hd_dim = d_sg // N_SG_HEADS
    seq_tile = pl.program_id(0) % tiles_per_seq

    @pl.when(pl.program_id(0) == 0)
    def _():
        _load_cast(win_hbm.at[layer], win_ref, stage, sems)
        _load_cast(wout_hbm.at[layer], wout_ref, stage, sems)

    x, xh = _mixer_input(refs[:n_src], pending)
    sub = tm // SUB_TILES

    ah = _dot(xh.astype(BF16), win_ref[:, :d_pool])
    a_scr[0:POOL_HALO, :] = jnp.where(seq_tile == 0, 0.0, ah)
    row = lax.broadcasted_iota(I32, (CHUNK, CHUNK), 0)
    col = lax.broadcasted_iota(I32, (CHUNK, CHUNK), 1)
    ws_masked = [jnp.where(row >= col, ws_ref[hd], 0.0).astype(BF16) for hd in range(N_SG_HEADS)]

    def in_proj(st):
        h = _dot(x[st * sub:(st + 1) * sub, :].astype(BF16), win_ref[...])
        a_scr[POOL_HALO + st * sub:POOL_HALO + (st + 1) * sub, :] = h[:, :d_pool]
        return h[:, d_pool:]

    def branches_out_proj(st, hz):
        base = st * sub
        pos = seq_tile * tm + base + lax.broadcasted_iota(I32, (sub, 1), 0)
        for g, w in enumerate(POOL_WINDOWS):
            cs = slice(g * pgd, (g + 1) * pgd)
            tok = a_scr[POOL_HALO + base:POOL_HALO + base + sub, cs]
            acc = a_scr[base:POOL_HALO + base + sub, cs]
            span = 1
            while span < w:
                acc = acc[span:, :] + acc[:-span, :]
                span *= 2
            acc = acc[acc.shape[0] - sub:, :]
            cnt = jnp.minimum(pos + 1, w).astype(F32)
            dev = acc * (1.0 / cnt) - tok
            yg = _dot(dev.astype(BF16), wpool_ref[g])
            cat_scr[base:base + sub, cs] = (yg * pscale_ref[:, cs]).astype(BF16)

        z = _gelu_tanh(hz)
        u = z[:, :d_sg]
        v = _layer_norm(z[:, d_sg:], lvg_ref[...], lvb_ref[...]).astype(BF16)
        for hd in range(N_SG_HEADS):
            hs = slice(hd * hd_dim, (hd + 1) * hd_dim)
            bcol = bst_ref[:, hd:hd + 1]
            for ck in range(sub // CHUNK):
                rs = slice(ck * CHUNK, (ck + 1) * CHUNK)
                sv = _dot(ws_masked[hd], v[rs, hs]) + bcol
                cat_scr[base + ck * CHUNK:base + (ck + 1) * CHUNK,
                        d_pool + hd * hd_dim:d_pool + (hd + 1) * hd_dim] = (u[rs, hs] * sv).astype(BF16)
        return _dot(cat_scr[base:base + sub, :], wout_ref[...])

    hzs = [in_proj(st) for st in range(SUB_TILES)]
    mixes = [branches_out_proj(st, hzs[st]) for st in range(SUB_TILES)]
    for st in range(SUB_TILES):
        rs = slice(st * sub, (st + 1) * sub)
        o_ref[rs, :] = _layer_norm(ALPHA * x[rs, :] + mixes[st], g_ref[...], b_ref[...])


def _even_mixer(x, pending, seq, layer, w_in, w_pool, pool_scale, ln_v_g, ln_v_b, w_spatial, b_spatial_t,
                w_out, ln_g, ln_b, ln_row):
    n, d = x.shape
    d_in = w_in.shape[2]
    d_pool = pool_scale.shape[2]
    d_sg = ln_v_g.shape[2]
    kern = functools.partial(_even_kernel, tiles_per_seq=seq // TM, layer=layer, pending=pending is not None)
    src_specs, src_args = _mixer_sources(x, pending, POOL_HALO)
    return pl.pallas_call(
        kern,
        out_shape=jax.ShapeDtypeStruct((n, d), F32),
        grid=(n // TM,),
        in_specs=src_specs + [
            _HBM,
            _pick_spec(w_pool, layer),
            _pick_spec(pool_scale, layer),
            _pick_spec(ln_v_g, layer),
            _pick_spec(ln_v_b, layer),
            _pick_spec(w_spatial, layer),
            _pick_spec(b_spatial_t, layer),
            _HBM,
            _pick_spec(ln_g, ln_row),
            _pick_spec(ln_b, ln_row),
        ],
        out_specs=pl.BlockSpec((TM, d), lambda i: (i, 0)),
        scratch_shapes=[pltpu.VMEM((POOL_HALO + TM, d_pool), F32), pltpu.VMEM((TM, d_pool + d_sg), BF16),
                        pltpu.VMEM((d, d_in), BF16), pltpu.VMEM((d_pool + d_sg, d), BF16)] + _stage_scratch(d),
        compiler_params=_params(),
        name="even_mixer",
    )(*src_args, w_in, w_pool, pool_scale, ln_v_g, ln_v_b, w_spatial, b_spatial_t, w_out, ln_g, ln_b)


def _odd_kernel(*refs, tiles_per_seq, layer, pending):
    n_src = 10 if pending else 2
    (win_hbm, cwt_ref, cb_ref, wout_hbm, g_ref, b_ref, o_ref, zc_scr,
     win_ref, wout_ref, stage, sems) = refs[n_src:]
    tm, d = o_ref.shape
    seq_tile = pl.program_id(0) % tiles_per_seq

    @pl.when(pl.program_id(0) == 0)
    def _():
        _load_cast(win_hbm.at[layer], win_ref, stage, sems)
        _load_cast(wout_hbm.at[layer], wout_ref, stage, sems)

    x, xh = _mixer_input(refs[:n_src], pending)
    hh = _dot(xh.astype(BF16), win_ref[:, d:])
    zc_scr[0:CONV_HALO, :] = jnp.where(seq_tile == 0, 0.0, hh[:, :d] * hh[:, d:])
    sub = tm // SUB_TILES

    def in_proj(st):
        xb = x[st * sub:(st + 1) * sub, :].astype(BF16)
        hc = _dot(xb, win_ref[:, d:2 * d])
        hz = _dot(xb, win_ref[:, 2 * d:])
        zc_scr[CONV_HALO + st * sub:CONV_HALO + (st + 1) * sub, :] = hc * hz
        return _dot(xb, win_ref[:, :d])

    def conv_out_proj(st, gate):
        conv = cb_ref[...]
        for j in range(CONV_WIDTH):
            off = CONV_HALO + st * sub - (CONV_WIDTH - 1) + j
            conv = conv + zc_scr[off:off + sub, :] * cwt_ref[j:j + 1, :]
        return _dot((gate * conv).astype(BF16), wout_ref[...])

    gates = [in_proj(st) for st in range(SUB_TILES)]
    ys = [conv_out_proj(st, gates[st]) for st in range(SUB_TILES)]
    for st in range(SUB_TILES):
        rs = slice(st * sub, (st + 1) * sub)
        o_ref[rs, :] = _layer_norm(ALPHA * x[rs, :] + ys[st], g_ref[...], b_ref[...])


def _odd_mixer(x, pending, seq, layer, w_in, conv_w_t, conv_b, w_out, ln_g, ln_b, ln_row):
    n, d = x.shape
    kern = functools.partial(_odd_kernel, tiles_per_seq=seq // TM, layer=layer, pending=pending is not None)
    src_specs, src_args = _mixer_sources(x, pending, CONV_HALO)
    return pl.pallas_call(
        kern,
        out_shape=jax.ShapeDtypeStruct((n, d), F32),
        grid=(n // TM,),
        in_specs=src_specs + [
            _HBM,
            _pick_spec(conv_w_t, layer),
            _pick_spec(conv_b, layer),
            _HBM,
            _pick_spec(ln_g, ln_row),
            _pick_spec(ln_b, ln_row),
        ],
        out_specs=pl.BlockSpec((TM, d), lambda i: (i, 0)),
        scratch_shapes=[pltpu.VMEM((CONV_HALO + TM, d), F32),
                        pltpu.VMEM(w_in.shape[1:], BF16), pltpu.VMEM(w_out.shape[1:], BF16)] + _stage_scratch(d),
        compiler_params=_params(),
        name="odd_mixer",
    )(*src_args, w_in, conv_w_t, conv_b, w_out, ln_g, ln_b)


def _kv_kernel(mem_ref, wk_ref, wv_ref, k_ref, v_ref):
    m = mem_ref[...].astype(BF16)
    k_ref[...] = _dot(m, wk_ref[...].astype(BF16)).astype(BF16)
    v_ref[...] = _dot(m, wv_ref[...].astype(BF16)).astype(BF16)


def _memory_kv(mem2d, wk, wv, layer):
    rows, d = mem2d.shape
    out = jax.ShapeDtypeStruct((rows, d), BF16)
    return pl.pallas_call(
        _kv_kernel,
        out_shape=(out, out),
        grid=(1,),
        in_specs=[_const_spec((rows, d)), _pick_spec(wk, layer), _pick_spec(wv, layer)],
        out_specs=(_const_spec((rows, d)), _const_spec((rows, d))),
        compiler_params=_params(),
        name="memory_kv",
    )(mem2d, wk, wv)


def _attn_kernel(x_ref, k_ref, v_ref, wq_hbm, wo_hbm, g_ref, b_ref, wr_ref, br_ref,
                 o_ref, op_ref, rt_ref, rg_ref, cnt_ref, o_scr, carry_scr, wq_ref, wo_ref, stage, sems,
                 *, layer):
    tm, d = x_ref.shape
    hd_dim = d // N_XHEADS
    sub = tm // SUB_TILES

    @pl.when(pl.program_id(0) == 0)
    def _():
        carry_scr[...] = jnp.zeros_like(carry_scr)
        _load_cast(wq_hbm.at[layer], wq_ref, stage, sems)
        _load_cast(wo_hbm.at[layer], wo_ref, stage, sems)

    row_slices = [slice(st * sub, (st + 1) * sub) for st in range(SUB_TILES)]
    qs = [_dot(x_ref[rs, :].astype(BF16), wq_ref[...]) * (1.0 / math.sqrt(hd_dim)) for rs in row_slices]
    for rs, q in zip(row_slices, qs):
        for hd in range(N_XHEADS):
            hs = slice(hd * hd_dim, (hd + 1) * hd_dim)
            s = lax.dot_general(q[:, hs].astype(BF16), k_ref[:, hs], _NT, preferred_element_type=F32)
            p = jnp.exp(s - jnp.max(s, axis=-1, keepdims=True))
            p = p * (1.0 / jnp.sum(p, axis=-1, keepdims=True))
            o_scr[rs, hs] = _dot(p.astype(BF16), v_ref[:, hs]).astype(BF16)
    xas = [_dot(o_scr[rs, :], wo_ref[...]) for rs in row_slices]
    for rs, xa in zip(row_slices, xas):
        out = _layer_norm(ALPHA * x_ref[rs, :] + xa, g_ref[...], b_ref[...])
        o_ref[rs, :] = out
        op_ref[rs, :] = _pack_halves(out)
        table, gates = _route_rows(out, wr_ref, br_ref, carry_scr)
        rt_ref[:, rs] = table
        rg_ref[rs, :] = gates
    cnt_ref[...] = carry_scr[...].astype(I32)


def _cross_attn(x, seq, layer, k, v, wq, wo, ln_g, ln_b, ln_row, router_w, router_b):
    n, d = x.shape
    m = k.shape[1]
    tiles_per_seq = seq // TM
    kvspec = pl.BlockSpec((None, m, d), lambda i: (i // tiles_per_seq, 0, 0))
    out, packed, table, gates, cnt = pl.pallas_call(
        functools.partial(_attn_kernel, layer=layer),
        out_shape=(jax.ShapeDtypeStruct((n, d), F32), jax.ShapeDtypeStruct((n, d // 2), I32),
                   jax.ShapeDtypeStruct((ROUTE_ROWS, n), F32), jax.ShapeDtypeStruct((n, TOP_K), F32),
                   jax.ShapeDtypeStruct((1, ROUTE_COLS), I32)),
        grid=(n // TM,),
        in_specs=[
            pl.BlockSpec((TM, d), lambda i: (i, 0)),
            kvspec, kvspec,
            _HBM, _HBM,
            _pick_spec(ln_g, ln_row), _pick_spec(ln_b, ln_row),
            _pick_spec(router_w, layer), _pick_spec(router_b, layer),
        ],
        out_specs=(pl.BlockSpec((TM, d), lambda i: (i, 0)), pl.BlockSpec((TM, d // 2), lambda i: (i, 0)),
                   pl.BlockSpec((ROUTE_ROWS, TM), lambda i: (0, i)), pl.BlockSpec((TM, TOP_K), lambda i: (i, 0)),
                   _const_spec((1, ROUTE_COLS))),
        scratch_shapes=[pltpu.VMEM((TM, d), BF16), pltpu.VMEM((1, ROUTE_COLS), F32),
                        pltpu.VMEM((d, d), BF16), pltpu.VMEM((d, d), BF16)] + _stage_scratch(d),
        compiler_params=_params(),
        name="cross_attn",
    )(x, k, v, wq, wo, ln_g, ln_b, router_w, router_b)
    return out, packed, table, gates, cnt[0, N_GROUPS:N_GROUPS + N_EXPERTS]


def _route_rows(x, wcat_ref, bias_ref, carry_scr):
    tm = x.shape[0]

    xh = x.astype(BF16)
    xl = (x - xh.astype(F32)).astype(BF16)
    r1 = _dot(xh, wcat_ref[...])
    r2 = _dot(xl, wcat_ref[:, :ROUTE_COLS])
    logits = r1[:, :ROUTE_COLS] + r1[:, ROUTE_COLS:] + r2 + bias_ref[...]

    lane = lax.broadcasted_iota(I32, (tm, ROUTE_COLS), 1).astype(F32)
    neg = -jnp.inf

    def first_argmax(vals):
        mx = jnp.max(vals, axis=-1, keepdims=True)
        idx = jnp.min(jnp.where(vals == mx, lane, float(ROUTE_COLS)), axis=-1, keepdims=True)
        return mx, idx

    gl = jnp.where(lane < N_GROUPS, logits, neg)
    gmax, g_sel = first_argmax(gl)
    gate_g = 1.0 / jnp.sum(jnp.exp(gl - gmax), axis=-1, keepdims=True)

    lo = N_GROUPS + g_sel * EXPERTS_PER_GROUP
    el = jnp.where((lane >= lo) & (lane < lo + EXPERTS_PER_GROUP), logits, neg)
    m1, i1 = first_argmax(el)
    m2, i2 = first_argmax(jnp.where(lane == i1, neg, el))
    e21 = jnp.exp(m2 - m1)
    w1 = 1.0 / (1.0 + e21)
    w2 = e21 / (1.0 + e21)

    oh1 = lane == i1
    oh2 = lane == i2
    oh = (oh1 | oh2).astype(BF16)
    r = lax.broadcasted_iota(I32, (tm, tm), 0)
    c = lax.broadcasted_iota(I32, (tm, tm), 1)
    before = _dot((r > c).astype(BF16), oh) + carry_scr[...]
    rank1 = jnp.sum(jnp.where(oh1, before, 0.0), axis=-1, keepdims=True)
    rank2 = jnp.sum(jnp.where(oh2, before, 0.0), axis=-1, keepdims=True)
    carry_scr[...] += jnp.sum(oh.astype(F32), axis=0, keepdims=True)

    cols = jnp.where(lane == 0.0, i1 - N_GROUPS,
                     jnp.where(lane == 1.0, i2 - N_GROUPS,
                               jnp.where(lane == 2.0, rank1, jnp.where(lane == 3.0, rank2, 0.0))))
    table = jnp.transpose(cols)[:ROUTE_ROWS, :]
    l2 = lax.broadcasted_iota(I32, (tm, TOP_K), 1)
    gates = jnp.where(l2 == 0, gate_g * w1, gate_g * w2)
    return table, gates


def _router_weights(wr_g, br_g, wr_e, br_e):
    nl, d, _ = wr_g.shape
    w = jnp.concatenate([wr_g, jnp.transpose(wr_e, (0, 2, 1, 3)).reshape(nl, d, N_EXPERTS)], axis=2)
    w = jnp.pad(w, ((0, 0), (0, 0), (0, ROUTE_COLS - w.shape[2])))
    w_hi = w.astype(BF16)
    w_lo = (w - w_hi.astype(F32)).astype(BF16)
    bias = jnp.pad(jnp.concatenate([br_g, br_e.reshape(nl, -1)], axis=1),
                   ((0, 0), (0, ROUTE_COLS - N_GROUPS - N_EXPERTS)))
    return jnp.concatenate([w_hi, w_lo], axis=2), bias[:, None, :]


def _sc_worker_rows(rows):
    per_worker = rows // SC_WORKERS
    n_chunks = per_worker // SC_CHUNK
    assert per_worker * SC_WORKERS == rows and n_chunks * SC_CHUNK == per_worker and n_chunks % 2 == 0
    return per_worker, n_chunks


def _sc_gather_rows(table_hbm, out_hbm, idx_v, rows_v, gsem, wsem, base, n_chunks):
    def fetch(c, slot):
        off = pl.multiple_of(c * SC_CHUNK, SC_CHUNK)
        return pltpu.make_async_copy(table_hbm.at[idx_v.at[pl.ds(off, SC_CHUNK)]], rows_v.at[slot],
                                     gsem.at[slot])

    def put(c, slot):
        off = pl.multiple_of(c * SC_CHUNK, SC_CHUNK)
        return pltpu.make_async_copy(rows_v.at[slot], out_hbm.at[pl.ds(base + off, SC_CHUNK)], wsem.at[slot])

    fetch(0, 0).start()

    @pl.loop(0, n_chunks, step=2)
    def _(c0):
        for slot in range(2):
            c = c0 + slot

            @pl.when(c + 1 < n_chunks)
            def _():
                @pl.when(c >= 1)
                def _():
                    put(c - 1, 1 - slot).wait()
                fetch(c + 1, 1 - slot).start()

            fetch(c, slot).wait()
            put(c, slot).start()

    put(n_chunks - 2, 0).wait()
    put(n_chunks - 1, 1).wait()


def _sc_row_scratch(per_worker, d, dtype):
    return [pltpu.VMEM((per_worker,), I32), pltpu.VMEM((2, SC_CHUNK, d), dtype),
            pltpu.SemaphoreType.DMA((2,)), pltpu.SemaphoreType.DMA((2,))]


def _sc_gather(table, idx):
    b = idx.shape[0]
    d = table.shape[1]
    per_worker, n_chunks = _sc_worker_rows(b)
    mesh = plsc.VectorSubcoreMesh(core_axis_name="c", subcore_axis_name="s")

    @functools.partial(
        pl.kernel, mesh=mesh,
        out_type=jax.ShapeDtypeStruct((b, d), table.dtype),
        scratch_types=_sc_row_scratch(per_worker, d, table.dtype),
        name="sc_gather",
    )
    def gather(table_hbm, idx_hbm, out_hbm, idx_v, rows_v, gsem, wsem):
        base = (lax.axis_index("s") * SC_CORES + lax.axis_index("c")) * per_worker
        pltpu.sync_copy(idx_hbm.at[pl.ds(base, per_worker)], idx_v)
        _sc_gather_rows(table_hbm, out_hbm, idx_v, rows_v, gsem, wsem, base, n_chunks)

    return gather(table, idx)


def _sc_dispatch(table, dest_flat, rows):
    n, d = table.shape
    a = dest_flat.shape[0]
    lanes = SC_LANES
    per_worker, n_chunks = _sc_worker_rows(rows)
    assert a % lanes == 0 and per_worker % lanes == 0
    mesh = plsc.VectorSubcoreMesh(core_axis_name="c", subcore_axis_name="s")

    @functools.partial(
        pl.kernel, mesh=mesh,
        out_type=jax.ShapeDtypeStruct((rows, d), table.dtype),
        scratch_types=[pltpu.VMEM((a,), I32)] + _sc_row_scratch(per_worker, d, table.dtype),
        compiler_params=pltpu.CompilerParams(needs_layout_passes=False),
        name="sc_dispatch",
    )
    def dispatch(table_hbm, dest_hbm, out_hbm, dest_v, idx_v, rows_v, gsem, wsem):
        base = (lax.axis_index("s") * SC_CORES + lax.axis_index("c")) * per_worker
        pltpu.sync_copy(dest_hbm, dest_v)
        lane = lax.iota(I32, lanes)

        @pl.loop(0, per_worker // lanes)
        def _(i):
            idx_v[pl.ds(i * lanes, lanes)] = lax.rem(base + i * lanes + lane, n)

        @plsc.parallel_loop(0, a // lanes, unroll=8)
        def _(i):
            local = dest_v[pl.ds(i * lanes, lanes)] - base
            mine = (local >= 0) & (local < per_worker)
            plsc.store_scatter(idx_v, [jnp.where(mine, local, 0)], lax.rem(i * lanes + lane, n), mask=mine)

        _sc_gather_rows(table_hbm, out_hbm, idx_v, rows_v, gsem, wsem, base, n_chunks)

    return dispatch(table, dest_flat)


def _expert_kernel(sched_ref, ni_ref, xs_hbm, w1_hbm, w3_hbm, w2_hbm, y_hbm,
                   x_buf, y_buf, w1_buf, w3_buf, w2_buf, w1_scr, w3_scr, w2_scr, wsems, xsems, ysems,
                   *, layer):
    i = pl.program_id(0)
    n_items = ni_ref[0]
    expert, wslot, run_start, next_expert = (sched_ref[r, i] for r in range(4))
    slot = i % 2

    def for_units(item, fn):
        for units in range(1, ITEM_UNITS + 1):
            @pl.when(sched_ref[5, item] == units)
            def _():
                fn(units * EXPERT_UNIT)

    def x_copy(item, s, rows):
        row0 = pl.multiple_of(sched_ref[4, item], EXPERT_UNIT)
        return pltpu.make_async_copy(xs_hbm.at[pl.ds(row0, rows)], x_buf.at[s, pl.ds(0, rows)], xsems.at[s])

    def y_copy(item, s, rows):
        row0 = pl.multiple_of(sched_ref[4, item], EXPERT_UNIT)
        return pltpu.make_async_copy(y_buf.at[s, pl.ds(0, rows)], y_hbm.at[pl.ds(row0, rows)], ysems.at[s])

    def fetch(e, s):
        return [pltpu.make_async_copy(w_hbm.at[layer, e], buf.at[s], wsems.at[s, j])
                for j, (w_hbm, buf) in enumerate(((w1_hbm, w1_buf), (w3_hbm, w3_buf), (w2_hbm, w2_buf)))]

    def mlp(rows):
        x_lo, x_hi = _unpack_halves(x_buf[slot, 0:rows, :])
        xb = jnp.concatenate([x_lo.astype(BF16), x_hi.astype(BF16)], axis=1)
        h1 = _dot(xb, w1_scr[...])
        h3 = _dot(xb, w3_scr[...])
        hid = h1 * (1.0 / (1.0 + jnp.exp(-h1))) * h3
        y_buf[slot, 0:rows, :] = _pack_halves(_dot(hid.astype(BF16), w2_scr[...]))
        y_copy(i, slot, rows).start()

    @pl.when(i < n_items)
    def _():
        @pl.when(i == 0)
        def _():
            for_units(0, lambda rows: x_copy(0, 0, rows).start())

        for_units(i, lambda rows: x_copy(i, slot, rows).wait())

        @pl.when(i + 1 < n_items)
        def _():
            for_units(i + 1, lambda rows: x_copy(i + 1, 1 - slot, rows).start())

        @pl.when(run_start == 1)
        def _():
            @pl.when(i == 0)
            def _():
                for c in fetch(expert, wslot):
                    c.start()

            for c in fetch(expert, wslot):
                c.wait()

            @pl.when(next_expert >= 0)
            def _():
                for c in fetch(next_expert, 1 - wslot):
                    c.start()

            w1_scr[...] = w1_buf[wslot].astype(BF16)
            w3_scr[...] = w3_buf[wslot].astype(BF16)
            w2_scr[...] = w2_buf[wslot].astype(BF16)

        @pl.when(i >= 2)
        def _():
            for_units(i - 2, lambda rows: y_copy(i - 2, slot, rows).wait())

        for_units(i, mlp)

        @pl.when(i == n_items - 1)
        def _():
            @pl.when(i >= 1)
            def _():
                for_units(i - 1, lambda rows: y_copy(i - 1, 1 - slot, rows).wait())

            for_units(i, lambda rows: y_copy(i, slot, rows).wait())
            y_buf[0, 0:EXPERT_UNIT, :] = jnp.zeros((EXPERT_UNIT, y_buf.shape[2]), y_buf.dtype)
            first_free = (sched_ref[4, i] + sched_ref[5, i] * EXPERT_UNIT) // EXPERT_UNIT

            def zero_copy(u):
                return pltpu.make_async_copy(
                    y_buf.at[0, pl.ds(0, EXPERT_UNIT)],
                    y_hbm.at[pl.ds(pl.multiple_of(u * EXPERT_UNIT, EXPERT_UNIT), EXPERT_UNIT)], ysems.at[0])

            def start_zero(u, _):
                zero_copy(u).start()
                return 0

            def wait_zero(u, _):
                zero_copy(u).wait()
                return 0

            lax.fori_loop(first_free, y_hbm.shape[0] // EXPERT_UNIT, start_zero, 0)
            lax.fori_loop(first_free, y_hbm.shape[0] // EXPERT_UNIT, wait_zero, 0)


def _expert_mlp(xs, schedule, n_items, layer, w1, w3, w2):
    d, de = w1.shape[2], w1.shape[3]
    max_rows = ITEM_UNITS * EXPERT_UNIT
    return pl.pallas_call(
        functools.partial(_expert_kernel, layer=layer),
        out_shape=jax.ShapeDtypeStruct(xs.shape, I32),
        grid_spec=pltpu.PrefetchScalarGridSpec(
            num_scalar_prefetch=2,
            grid=(schedule.shape[1],),
            in_specs=[_HBM, _HBM, _HBM, _HBM],
            out_specs=_HBM,
            scratch_shapes=[pltpu.VMEM((2, max_rows, d // 2), I32), pltpu.VMEM((2, max_rows, d // 2), I32),
                            pltpu.VMEM((2, d, de), F32), pltpu.VMEM((2, d, de), F32), pltpu.VMEM((2, de, d), F32),
                            pltpu.VMEM((d, de), BF16), pltpu.VMEM((d, de), BF16), pltpu.VMEM((de, d), BF16),
                            pltpu.SemaphoreType.DMA((2, 3)), pltpu.SemaphoreType.DMA((2,)),
                            pltpu.SemaphoreType.DMA((2,))],
        ),
        compiler_params=_params(),
        name="expert_mlp",
    )(schedule, n_items, xs, w1, w3, w2)


def _combine_kernel(x_ref, y0_ref, y1_ref, gate_ref, g_ref, b_ref, o_ref):
    o_ref[...] = _moe_output(x_ref, y0_ref, y1_ref, gate_ref, g_ref, b_ref)


def _combine(x, yg, gates, ln_g, ln_b, ln_row):
    n, d = x.shape
    tiles = n // TM
    return pl.pallas_call(
        _combine_kernel,
        out_shape=jax.ShapeDtypeStruct((n, d), F32),
        grid=(tiles,),
        in_specs=[pl.BlockSpec((TM, d), lambda i: (i, 0)),
                  pl.BlockSpec((TM, d // 2), lambda i: (i, 0)),
                  pl.BlockSpec((TM, d // 2), lambda i: (i + tiles, 0)),
                  pl.BlockSpec((TM, TOP_K), lambda i: (i, 0)),
                  _pick_spec(ln_g, ln_row), _pick_spec(ln_b, ln_row)],
        out_specs=pl.BlockSpec((TM, d), lambda i: (i, 0)),
        compiler_params=_params(),
        name="combine",
    )(x, yg, yg, gates, ln_g, ln_b)


def _moe_experts(x_packed, table, counts, layer, w1, w3, w2):
    n = x_packed.shape[0]
    max_units = (n * TOP_K + N_EXPERTS * (EXPERT_UNIT - 1) + EXPERT_UNIT - 1) // EXPERT_UNIT
    max_items = (max_units + N_EXPERTS * (ITEM_UNITS - 1) + ITEM_UNITS - 1) // ITEM_UNITS
    experts = table[:TOP_K].astype(I32)
    ranks = table[TOP_K:2 * TOP_K].astype(I32)

    units_e = (counts + EXPERT_UNIT - 1) // EXPERT_UNIT
    units_start = jnp.cumsum(units_e) - units_e
    items_e = (units_e + ITEM_UNITS - 1) // ITEM_UNITS
    items_end = jnp.cumsum(items_e)
    n_items = items_end[-1:].astype(I32)
    item_ids = jnp.arange(max_items, dtype=I32)
    expert_ids = jnp.arange(N_EXPERTS, dtype=I32)
    item_expert = jnp.minimum(jnp.sum(items_end[None, :] <= item_ids[:, None], axis=1), N_EXPERTS - 1)
    later = (expert_ids[None, :] > expert_ids[:, None]) & (items_e[None, :] > 0)
    next_run = jnp.min(jnp.where(later, expert_ids[None, :], N_EXPERTS), axis=1)
    next_run = jnp.where(next_run == N_EXPERTS, -1, next_run)
    per_expert = jnp.stack([items_end - items_e, units_start, units_e, next_run], axis=1)
    mine = (item_expert[:, None] == expert_ids[None, :])[:, :, None]
    first_item, unit0, units, next_expert = jnp.sum(jnp.where(mine, per_expert[None], 0), axis=1).T
    within = item_ids - first_item
    item_row0 = (unit0 + ITEM_UNITS * within) * EXPERT_UNIT
    item_units = jnp.clip(units - ITEM_UNITS * within, 1, ITEM_UNITS)
    run_start_flag = ((within == 0) & (item_ids < n_items[0])).astype(I32)
    slot = (jnp.cumsum(run_start_flag) - 1) % 2
    schedule = jnp.stack([item_expert, slot, run_start_flag, next_expert, item_row0, item_units]).astype(I32)
    start_of = jnp.sum(jnp.where(experts[:, :, None] == expert_ids, units_start * EXPERT_UNIT, 0), axis=-1)
    dest = (start_of + ranks).astype(I32).reshape(-1)
    xs = _sc_dispatch(x_packed, dest, max_units * EXPERT_UNIT)
    y = _expert_mlp(xs, schedule, n_items, layer, w1, w3, w2)
    return _sc_gather(y, dest)


def kernel(x, mem, w_in_even, w_pool, pool_scale, ln_v_g, ln_v_b, w_spatial, b_spatial, w_out_even,
           w_in_odd, conv_w, conv_b, w_out_odd, wq_x, wk_x, wv_x, wo_x, ln_g, ln_b, wr_group,
           br_group, wr_expert, br_expert, w1, w3, w2):
    bsz, seq, d = x.shape
    assert seq % TM == 0 and d % LANES == 0
    mlen = mem.shape[1]
    mem2d = mem.reshape(bsz * mlen, d)
    ln_g = ln_g.reshape(DEPTH * 3, 1, d)
    ln_b = ln_b.reshape(DEPTH * 3, 1, d)
    w_pool = w_pool.astype(BF16)
    pool_scale, ln_v_g, ln_v_b, conv_b = (p[:, None, :] for p in (pool_scale, ln_v_g, ln_v_b, conv_b))
    b_spatial_t = jnp.swapaxes(b_spatial, 1, 2)
    conv_w_t = jnp.swapaxes(conv_w, 1, 2)
    router_w, router_b = _router_weights(wr_group, br_group, wr_expert, br_expert)

    h = x.reshape(bsz * seq, d)
    pending = None
    kv = _memory_kv(mem2d, wk_x, wv_x, 0)
    for l in range(DEPTH):
        i = l // 2
        k, v = (a.reshape(bsz, mlen, d) for a in kv)
        if l % 2 == 0:
            h = _even_mixer(h, pending, seq, i, w_in_even, w_pool, pool_scale, ln_v_g, ln_v_b,
                            w_spatial, b_spatial_t, w_out_even, ln_g, ln_b, 3 * l)
        else:
            h = _odd_mixer(h, pending, seq, i, w_in_odd, conv_w_t, conv_b, w_out_odd, ln_g, ln_b, 3 * l)
        h, hp, table, gates, counts = _cross_attn(h, seq, l, k, v, wq_x, wo_x,
                                                  ln_g, ln_b, 3 * l + 1, router_w, router_b)
        if l + 1 < DEPTH:
            kv = _memory_kv(mem2d, wk_x, wv_x, l + 1)
        yg = _moe_experts(hp, table, counts, l, w1, w3, w2)
        pending = (yg, gates, ln_g, ln_b, 3 * l + 2)
    return _combine(h, *pending).reshape(bsz, seq, d)
```

```python
import functools
import math

import jax
import jax.numpy as jnp
from jax import lax
from jax.experimental import pallas as pl
from jax.experimental.pallas import tpu as pltpu
from jax.experimental.pallas import tpu_sc as plsc

F32 = jnp.float32
BF16 = jnp.bfloat16
I32 = jnp.int32

POOL_WINDOWS = (2, 4, 8, 16)
assert all(w & (w - 1) == 0 for w in POOL_WINDOWS)
N_SG_HEADS = 4
CHUNK = 128
CONV_WIDTH = 3
N_XHEADS = 4
N_GROUPS = 4
EXPERTS_PER_GROUP = 8
N_EXPERTS = N_GROUPS * EXPERTS_PER_GROUP
TOP_K = 2
DEPTH = 4
ALPHA = (2.0 * DEPTH) ** 0.25
LN_EPS = 1e-5

LANES = 128
SC_CORES = 2
SC_WORKERS = 32
SC_LANES = 16
VMEM_LIMIT = 56 * 1024 * 1024

SC_CHUNK = 64
TM = 1024
SUB_TILES = 2
ROUTE_ROWS = 8
POOL_HALO = 16
CONV_HALO = 8
EXPERT_UNIT = 128
ITEM_UNITS = 4
ROUTE_COLS = 128
STAGE_COLS = 512

_NT = (((1,), (1,)), ((), ()))


def _dot(a, b):
    return jnp.dot(a, b, preferred_element_type=F32)


def _layer_norm(y, g, b):
    mu = jnp.mean(y, axis=-1, keepdims=True)
    yc = y - mu
    var = jnp.mean(yc * yc, axis=-1, keepdims=True)
    return yc * lax.rsqrt(var + LN_EPS) * g + b


def _gelu_tanh(x):
    c = math.sqrt(2.0 / math.pi)
    return 0.5 * x * (1.0 + jnp.tanh(c * (x + 0.044715 * (x * x * x))))


def _pack_halves(v):
    c = v.shape[1] // 2
    lo = pltpu.bitcast(v[:, :c].astype(BF16).astype(F32), jnp.uint32)
    hi = pltpu.bitcast(v[:, c:].astype(BF16).astype(F32), jnp.uint32)
    return pltpu.bitcast((hi & jnp.uint32(0xFFFF0000)) | (lo >> 16), I32)


def _unpack_halves(w):
    u = pltpu.bitcast(w, jnp.uint32)
    return pltpu.bitcast(u << 16, F32), pltpu.bitcast(u & jnp.uint32(0xFFFF0000), F32)


def _load_cast(w_hbm, w_scr, stage, sems):
    chunks = w_scr.shape[1] // STAGE_COLS

    def chunk_copy(c):
        return pltpu.make_async_copy(w_hbm.at[:, pl.ds(c * STAGE_COLS, STAGE_COLS)], stage.at[c % 2],
                                     sems.at[c % 2])

    chunk_copy(0).start()
    for c in range(chunks):
        if c + 1 < chunks:
            chunk_copy(c + 1).start()
        chunk_copy(c).wait()
        w_scr[:, c * STAGE_COLS:(c + 1) * STAGE_COLS] = stage[c % 2].astype(BF16)


def _stage_scratch(rows):
    return [pltpu.VMEM((2, rows, STAGE_COLS), F32), pltpu.SemaphoreType.DMA((2,))]


_HBM = pl.BlockSpec(memory_space=pl.ANY)


def _const_spec(shape):
    nd = len(shape)
    return pl.BlockSpec(shape, lambda i: (0,) * nd)


def _pick_spec(stacked, index):
    rest = stacked.shape[1:]
    return pl.BlockSpec((None,) + rest, lambda i: (index,) + (0,) * len(rest))


def _params():
    return pltpu.CompilerParams(dimension_semantics=("arbitrary",), vmem_limit_bytes=VMEM_LIMIT)


def _moe_output(x_ref, y0_ref, y1_ref, gate_ref, g_ref, b_ref):
    gates = gate_ref[...]
    g0, g1 = gates[:, 0:1], gates[:, 1:2]
    y0_lo, y0_hi = _unpack_halves(y0_ref[...])
    y1_lo, y1_hi = _unpack_halves(y1_ref[...])
    ff = jnp.concatenate([g0 * y0_lo + g1 * y1_lo, g0 * y0_hi + g1 * y1_hi], axis=1)
    return _layer_norm(ALPHA * x_ref[...] + ff, g_ref[...], b_ref[...])


def _mixer_input(src, pending):
    if not pending:
        x_ref, xh_ref = src
        return x_ref[...], xh_ref[...]
    x_ref, xh_ref, y0_ref, y0h_ref, y1_ref, y1h_ref, gate_ref, gateh_ref, g_ref, b_ref = src
    return (_moe_output(x_ref, y0_ref, y1_ref, gate_ref, g_ref, b_ref),
            _moe_output(xh_ref, y0h_ref, y1h_ref, gateh_ref, g_ref, b_ref))


def _mixer_sources(x, pending, halo):
    n, d = x.shape
    tiles = n // TM
    halo_blocks = TM // halo

    def halo_index(i):
        return jnp.maximum(i * halo_blocks - 1, 0)

    specs = [pl.BlockSpec((TM, d), lambda i: (i, 0)), pl.BlockSpec((halo, d), lambda i: (halo_index(i), 0))]
    args = [x, x]
    if pending is not None:
        yg, gates, ln_g, ln_b, ln_row = pending
        specs += [pl.BlockSpec((TM, d // 2), lambda i: (i, 0)),
                  pl.BlockSpec((halo, d // 2), lambda i: (halo_index(i), 0)),
                  pl.BlockSpec((TM, d // 2), lambda i: (i + tiles, 0)),
                  pl.BlockSpec((halo, d // 2), lambda i: (halo_index(i) + tiles * halo_blocks, 0)),
                  pl.BlockSpec((TM, TOP_K), lambda i: (i, 0)),
                  pl.BlockSpec((halo, TOP_K), lambda i: (halo_index(i), 0)),
                  _pick_spec(ln_g, ln_row), _pick_spec(ln_b, ln_row)]
        args += [yg, yg, yg, yg, gates, gates, ln_g, ln_b]
    return specs, args


def _even_kernel(*refs, tiles_per_seq, layer, pending):
    n_src = 10 if pending else 2
    (win_hbm, wpool_ref, pscale_ref, lvg_ref, lvb_ref, ws_ref, bst_ref, wout_hbm, g_ref, b_ref, o_ref,
     a_scr, cat_scr, win_ref, wout_ref, stage, sems) = refs[n_src:]
    tm = o_ref.shape[0]
    d_pool = a_scr.shape[1]
    d_sg = lvg_ref.shape[1]
    pgd = d_pool // len(POOL_WINDOWS)
    hd_dim = d_sg // N_SG_HEADS
    seq_tile = pl.program_id(0) % tiles_per_seq

    @pl.when(pl.program_id(0) == 0)
    def _():
        _load_cast(win_hbm.at[layer], win_ref, stage, sems)
        _load_cast(wout_hbm.at[layer], wout_ref, stage, sems)

    x, xh = _mixer_input(refs[:n_src], pending)
    sub = tm // SUB_TILES

    ah = _dot(xh.astype(BF16), win_ref[:, :d_pool])
    a_scr[0:POOL_HALO, :] = jnp.where(seq_tile == 0, 0.0, ah)
    row = lax.broadcasted_iota(I32, (CHUNK, CHUNK), 0)
    col = lax.broadcasted_iota(I32, (CHUNK, CHUNK), 1)
    ws_masked = [jnp.where(row >= col, ws_ref[hd], 0.0).astype(BF16) for hd in range(N_SG_HEADS)]

    def in_proj(st):
        h = _dot(x[st * sub:(st + 1) * sub, :].astype(BF16), win_ref[...])
        a_scr[POOL_HALO + st * sub:POOL_HALO + (st + 1) * sub, :] = h[:, :d_pool]
        return h[:, d_pool:]

    def branches_out_proj(st, hz):
        base = st * sub
        pos = seq_tile * tm + base + lax.broadcasted_iota(I32, (sub, 1), 0)
        for g, w in enumerate(POOL_WINDOWS):
            cs = slice(g * pgd, (g + 1) * pgd)
            tok = a_scr[POOL_HALO + base:POOL_HALO + base + sub, cs]
            acc = a_scr[base:POOL_HALO + base + sub, cs]
            span = 1
            while span < w:
                acc = acc[span:, :] + acc[:-span, :]
                span *= 2
            acc = acc[acc.shape[0] - sub:, :]
            cnt = jnp.minimum(pos + 1, w).astype(F32)
            dev = acc * (1.0 / cnt) - tok
            yg = _dot(dev.astype(BF16), wpool_ref[g])
            cat_scr[base:base + sub, cs] = (yg * pscale_ref[:, cs]).astype(BF16)

        z = _gelu_tanh(hz)
        u = z[:, :d_sg]
        v = _layer_norm(z[:, d_sg:], lvg_ref[...], lvb_ref[...]).astype(BF16)
        for hd in range(N_SG_HEADS):
            hs = slice(hd * hd_dim, (hd + 1) * hd_dim)
            bcol = bst_ref[:, hd:hd + 1]
            for ck in range(sub // CHUNK):
                rs = slice(ck * CHUNK, (ck + 1) * CHUNK)
                sv = _dot(ws_masked[hd], v[rs, hs]) + bcol
                cat_scr[base + ck * CHUNK:base + (ck + 1) * CHUNK,
                        d_pool + hd * hd_dim:d_pool + (hd + 1) * hd_dim] = (u[rs, hs] * sv).astype(BF16)
        return _dot(cat_scr[base:base + sub, :], wout_ref[...])

    hzs = [in_proj(st) for st in range(SUB_TILES)]
    mixes = [branches_out_proj(st, hzs[st]) for st in range(SUB_TILES)]
    for st in range(SUB_TILES):
        rs = slice(st * sub, (st + 1) * sub)
        o_ref[rs, :] = _layer_norm(ALPHA * x[rs, :] + mixes[st], g_ref[...], b_ref[...])


def _even_mixer(x, pending, seq, layer, w_in, w_pool, pool_scale, ln_v_g, ln_v_b, w_spatial, b_spatial_t,
                w_out, ln_g, ln_b, ln_row):
    n, d = x.shape
    d_in = w_in.shape[2]
    d_pool = pool_scale.shape[2]
    d_sg = ln_v_g.shape[2]
    kern = functools.partial(_even_kernel, tiles_per_seq=seq // TM, layer=layer, pending=pending is not None)
    src_specs, src_args = _mixer_sources(x, pending, POOL_HALO)
    return pl.pallas_call(
        kern,
        out_shape=jax.ShapeDtypeStruct((n, d), F32),
        grid=(n // TM,),
        in_specs=src_specs + [
            _HBM,
            _pick_spec(w_pool, layer),
            _pick_spec(pool_scale, layer),
            _pick_spec(ln_v_g, layer),
            _pick_spec(ln_v_b, layer),
            _pick_spec(w_spatial, layer),
            _pick_spec(b_spatial_t, layer),
            _HBM,
            _pick_spec(ln_g, ln_row),
            _pick_spec(ln_b, ln_row),
        ],
        out_specs=pl.BlockSpec((TM, d), lambda i: (i, 0)),
        scratch_shapes=[pltpu.VMEM((POOL_HALO + TM, d_pool), F32), pltpu.VMEM((TM, d_pool + d_sg), BF16),
                        pltpu.VMEM((d, d_in), BF16), pltpu.VMEM((d_pool + d_sg, d), BF16)] + _stage_scratch(d),
        compiler_params=_params(),
        name="even_mixer",
    )(*src_args, w_in, w_pool, pool_scale, ln_v_g, ln_v_b, w_spatial, b_spatial_t, w_out, ln_g, ln_b)


def _odd_kernel(*refs, tiles_per_seq, layer, pending):
    n_src = 10 if pending else 2
    (win_hbm, cwt_ref, cb_ref, wout_hbm, g_ref, b_ref, o_ref, zc_scr,
     win_ref, wout_ref, stage, sems) = refs[n_src:]
    tm, d = o_ref.shape
    seq_tile = pl.program_id(0) % tiles_per_seq

    @pl.when(pl.program_id(0) == 0)
    def _():
        _load_cast(win_hbm.at[layer], win_ref, stage, sems)
        _load_cast(wout_hbm.at[layer], wout_ref, stage, sems)

    x, xh = _mixer_input(refs[:n_src], pending)
    hh = _dot(xh.astype(BF16), win_ref[:, d:])
    zc_scr[0:CONV_HALO, :] = jnp.where(seq_tile == 0, 0.0, hh[:, :d] * hh[:, d:])
    sub = tm // SUB_TILES

    def in_proj(st):
        xb = x[st * sub:(st + 1) * sub, :].astype(BF16)
        hc = _dot(xb, win_ref[:, d:2 * d])
        hz = _dot(xb, win_ref[:, 2 * d:])
        zc_scr[CONV_HALO + st * sub:CONV_HALO + (st + 1) * sub, :] = hc * hz
        return _dot(xb, win_ref[:, :d])

    def conv_out_proj(st, gate):
        conv = cb_ref[...]
        for j in range(CONV_WIDTH):
            off = CONV_HALO + st * sub - (CONV_WIDTH - 1) + j
            conv = conv + zc_scr[off:off + sub, :] * cwt_ref[j:j + 1, :]
        return _dot((gate * conv).astype(BF16), wout_ref[...])

    gates = [in_proj(st) for st in range(SUB_TILES)]
    ys = [conv_out_proj(st, gates[st]) for st in range(SUB_TILES)]
    for st in range(SUB_TILES):
        rs = slice(st * sub, (st + 1) * sub)
        o_ref[rs, :] = _layer_norm(ALPHA * x[rs, :] + ys[st], g_ref[...], b_ref[...])


def _odd_mixer(x, pending, seq, layer, w_in, conv_w_t, conv_b, w_out, ln_g, ln_b, ln_row):
    n, d = x.shape
    kern = functools.partial(_odd_kernel, tiles_per_seq=seq // TM, layer=layer, pending=pending is not None)
    src_specs, src_args = _mixer_sources(x, pending, CONV_HALO)
    return pl.pallas_call(
        kern,
        out_shape=jax.ShapeDtypeStruct((n, d), F32),
        grid=(n // TM,),
        in_specs=src_specs + [
            _HBM,
            _pick_spec(conv_w_t, layer),
            _pick_spec(conv_b, layer),
            _HBM,
            _pick_spec(ln_g, ln_row),
            _pick_spec(ln_b, ln_row),
        ],
        out_specs=pl.BlockSpec((TM, d), lambda i: (i, 0)),
        scratch_shapes=[pltpu.VMEM((CONV_HALO + TM, d), F32),
                        pltpu.VMEM(w_in.shape[1:], BF16), pltpu.VMEM(w_out.shape[1:], BF16)] + _stage_scratch(d),
        compiler_params=_params(),
        name="odd_mixer",
    )(*src_args, w_in, conv_w_t, conv_b, w_out, ln_g, ln_b)


def _kv_kernel(mem_ref, wk_ref, wv_ref, k_ref, v_ref):
    m = mem_ref[...].astype(BF16)
    k_ref[...] = _dot(m, wk_ref[...].astype(BF16)).astype(BF16)
    v_ref[...] = _dot(m, wv_ref[...].astype(BF16)).astype(BF16)


def _memory_kv(mem2d, wk, wv, layer):
    rows, d = mem2d.shape
    out = jax.ShapeDtypeStruct((rows, d), BF16)
    return pl.pallas_call(
        _kv_kernel,
        out_shape=(out, out),
        grid=(1,),
        in_specs=[_const_spec((rows, d)), _pick_spec(wk, layer), _pick_spec(wv, layer)],
        out_specs=(_const_spec((rows, d)), _const_spec((rows, d))),
        compiler_params=_params(),
        name="memory_kv",
    )(mem2d, wk, wv)


def _attn_kernel(x_ref, k_ref, v_ref, wq_hbm, wo_hbm, g_ref, b_ref, wr_ref, br_ref,
                 o_ref, op_ref, rt_ref, rg_ref, cnt_ref, o_scr, carry_scr, wq_ref, wo_ref, stage, sems,
                 *, layer):
    tm, d = x_ref.shape
    hd_dim = d // N_XHEADS
    sub = tm // SUB_TILES

    @pl.when(pl.program_id(0) == 0)
    def _():
        carry_scr[...] = jnp.zeros_like(carry_scr)
        _load_cast(wq_hbm.at[layer], wq_ref, stage, sems)
        _load_cast(wo_hbm.at[layer], wo_ref, stage, sems)

    row_slices = [slice(st * sub, (st + 1) * sub) for st in range(SUB_TILES)]
    qs = [_dot(x_ref[rs, :].astype(BF16), wq_ref[...]) * (1.0 / math.sqrt(hd_dim)) for rs in row_slices]
    for rs, q in zip(row_slices, qs):
        for hd in range(N_XHEADS):
            hs = slice(hd * hd_dim, (hd + 1) * hd_dim)
            s = lax.dot_general(q[:, hs].astype(BF16), k_ref[:, hs], _NT, preferred_element_type=F32)
            p = jnp.exp(s - jnp.max(s, axis=-1, keepdims=True))
            p = p * (1.0 / jnp.sum(p, axis=-1, keepdims=True))
            o_scr[rs, hs] = _dot(p.astype(BF16), v_ref[:, hs]).astype(BF16)
    xas = [_dot(o_scr[rs, :], wo_ref[...]) for rs in row_slices]
    for rs, xa in zip(row_slices, xas):
        out = _layer_norm(ALPHA * x_ref[rs, :] + xa, g_ref[...], b_ref[...])
        o_ref[rs, :] = out
        op_ref[rs, :] = _pack_halves(out)
        table, gates = _route_rows(out, wr_ref, br_ref, carry_scr)
        rt_ref[:, rs] = table
        rg_ref[rs, :] = gates
    cnt_ref[...] = carry_scr[...].astype(I32)


def _cross_attn(x, seq, layer, k, v, wq, wo, ln_g, ln_b, ln_row, router_w, router_b):
    n, d = x.shape
    m = k.shape[1]
    tiles_per_seq = seq // TM
    kvspec = pl.BlockSpec((None, m, d), lambda i: (i // tiles_per_seq, 0, 0))
    out, packed, table, gates, cnt = pl.pallas_call(
        functools.partial(_attn_kernel, layer=layer),
        out_shape=(jax.ShapeDtypeStruct((n, d), F32), jax.ShapeDtypeStruct((n, d // 2), I32),
                   jax.ShapeDtypeStruct((ROUTE_ROWS, n), F32), jax.ShapeDtypeStruct((n, TOP_K), F32),
                   jax.ShapeDtypeStruct((1, ROUTE_COLS), I32)),
        grid=(n // TM,),
        in_specs=[
            pl.BlockSpec((TM, d), lambda i: (i, 0)),
            kvspec, kvspec,
            _HBM, _HBM,
            _pick_spec(ln_g, ln_row), _pick_spec(ln_b, ln_row),
            _pick_spec(router_w, layer), _pick_spec(router_b, layer),
        ],
        out_specs=(pl.BlockSpec((TM, d), lambda i: (i, 0)), pl.BlockSpec((TM, d // 2), lambda i: (i, 0)),
                   pl.BlockSpec((ROUTE_ROWS, TM), lambda i: (0, i)), pl.BlockSpec((TM, TOP_K), lambda i: (i, 0)),
                   _const_spec((1, ROUTE_COLS))),
        scratch_shapes=[pltpu.VMEM((TM, d), BF16), pltpu.VMEM((1, ROUTE_COLS), F32),
                        pltpu.VMEM((d, d), BF16), pltpu.VMEM((d, d), BF16)] + _stage_scratch(d),
        compiler_params=_params(),
        name="cross_attn",
    )(x, k, v, wq, wo, ln_g, ln_b, router_w, router_b)
    return out, packed, table, gates, cnt[0, N_GROUPS:N_GROUPS + N_EXPERTS]


def _route_rows(x, w_ref, bias_ref, carry_scr):
    tm = x.shape[0]

    logits = _dot(x.astype(BF16), w_ref[...]) + bias_ref[...]

    lane = lax.broadcasted_iota(I32, (tm, ROUTE_COLS), 1).astype(F32)
    neg = -jnp.inf

    def first_argmax(vals):
        mx = jnp.max(vals, axis=-1, keepdims=True)
        idx = jnp.min(jnp.where(vals == mx, lane, float(ROUTE_COLS)), axis=-1, keepdims=True)
        return mx, idx

    gl = jnp.where(lane < N_GROUPS, logits, neg)
    gmax, g_sel = first_argmax(gl)
    gate_g = 1.0 / jnp.sum(jnp.exp(gl - gmax), axis=-1, keepdims=True)

    lo = N_GROUPS + g_sel * EXPERTS_PER_GROUP
    el = jnp.where((lane >= lo) & (lane < lo + EXPERTS_PER_GROUP), logits, neg)
    m1, i1 = first_argmax(el)
    m2, i2 = first_argmax(jnp.where(lane == i1, neg, el))
    e21 = jnp.exp(m2 - m1)
    w1 = 1.0 / (1.0 + e21)
    w2 = e21 / (1.0 + e21)

    oh1 = lane == i1
    oh2 = lane == i2
    oh = (oh1 | oh2).astype(BF16)
    r = lax.broadcasted_iota(I32, (tm, tm), 0)
    c = lax.broadcasted_iota(I32, (tm, tm), 1)
    before = _dot((r > c).astype(BF16), oh) + carry_scr[...]
    rank1 = jnp.sum(jnp.where(oh1, before, 0.0), axis=-1, keepdims=True)
    rank2 = jnp.sum(jnp.where(oh2, before, 0.0), axis=-1, keepdims=True)
    carry_scr[...] += jnp.sum(oh.astype(F32), axis=0, keepdims=True)

    cols = jnp.where(lane == 0.0, i1 - N_GROUPS,
                     jnp.where(lane == 1.0, i2 - N_GROUPS,
                               jnp.where(lane == 2.0, rank1, jnp.where(lane == 3.0, rank2, 0.0))))
    table = jnp.transpose(cols)[:ROUTE_ROWS, :]
    l2 = lax.broadcasted_iota(I32, (tm, TOP_K), 1)
    gates = jnp.where(l2 == 0, gate_g * w1, gate_g * w2)
    return table, gates


def _router_weights(wr_g, br_g, wr_e, br_e):
    nl, d, _ = wr_g.shape
    w = jnp.concatenate([wr_g, jnp.transpose(wr_e, (0, 2, 1, 3)).reshape(nl, d, N_EXPERTS)], axis=2)
    w = jnp.pad(w, ((0, 0), (0, 0), (0, ROUTE_COLS - w.shape[2]))).astype(BF16)
    bias = jnp.pad(jnp.concatenate([br_g, br_e.reshape(nl, -1)], axis=1),
                   ((0, 0), (0, ROUTE_COLS - N_GROUPS - N_EXPERTS)))
    return w, bias[:, None, :]


def _sc_worker_rows(rows):
    per_worker = rows // SC_WORKERS
    n_chunks = per_worker // SC_CHUNK
    assert per_worker * SC_WORKERS == rows and n_chunks * SC_CHUNK == per_worker and n_chunks % 2 == 0
    return per_worker, n_chunks


def _sc_gather_rows(table_hbm, out_hbm, idx_v, rows_v, gsem, wsem, base, n_chunks):
    def fetch(c, slot):
        off = pl.multiple_of(c * SC_CHUNK, SC_CHUNK)
        return pltpu.make_async_copy(table_hbm.at[idx_v.at[pl.ds(off, SC_CHUNK)]], rows_v.at[slot],
                                     gsem.at[slot])

    def put(c, slot):
        off = pl.multiple_of(c * SC_CHUNK, SC_CHUNK)
        return pltpu.make_async_copy(rows_v.at[slot], out_hbm.at[pl.ds(base + off, SC_CHUNK)], wsem.at[slot])

    fetch(0, 0).start()

    @pl.loop(0, n_chunks, step=2)
    def _(c0):
        for slot in range(2):
            c = c0 + slot

            @pl.when(c + 1 < n_chunks)
            def _():
                @pl.when(c >= 1)
                def _():
                    put(c - 1, 1 - slot).wait()
                fetch(c + 1, 1 - slot).start()

            fetch(c, slot).wait()
            put(c, slot).start()

    put(n_chunks - 2, 0).wait()
    put(n_chunks - 1, 1).wait()


def _sc_row_scratch(per_worker, d, dtype):
    return [pltpu.VMEM((per_worker,), I32), pltpu.VMEM((2, SC_CHUNK, d), dtype),
            pltpu.SemaphoreType.DMA((2,)), pltpu.SemaphoreType.DMA((2,))]


def _sc_gather(table, idx):
    b = idx.shape[0]
    d = table.shape[1]
    per_worker, n_chunks = _sc_worker_rows(b)
    mesh = plsc.VectorSubcoreMesh(core_axis_name="c", subcore_axis_name="s")

    @functools.partial(
        pl.kernel, mesh=mesh,
        out_type=jax.ShapeDtypeStruct((b, d), table.dtype),
        scratch_types=_sc_row_scratch(per_worker, d, table.dtype),
        name="sc_gather",
    )
    def gather(table_hbm, idx_hbm, out_hbm, idx_v, rows_v, gsem, wsem):
        base = (lax.axis_index("s") * SC_CORES + lax.axis_index("c")) * per_worker
        pltpu.sync_copy(idx_hbm.at[pl.ds(base, per_worker)], idx_v)
        _sc_gather_rows(table_hbm, out_hbm, idx_v, rows_v, gsem, wsem, base, n_chunks)

    return gather(table, idx)


def _sc_dispatch(table, dest_flat, rows):
    n, d = table.shape
    a = dest_flat.shape[0]
    lanes = SC_LANES
    per_worker, n_chunks = _sc_worker_rows(rows)
    assert a % lanes == 0 and per_worker % lanes == 0
    mesh = plsc.VectorSubcoreMesh(core_axis_name="c", subcore_axis_name="s")

    @functools.partial(
        pl.kernel, mesh=mesh,
        out_type=jax.ShapeDtypeStruct((rows, d), table.dtype),
        scratch_types=[pltpu.VMEM((a,), I32)] + _sc_row_scratch(per_worker, d, table.dtype),
        compiler_params=pltpu.CompilerParams(needs_layout_passes=False),
        name="sc_dispatch",
    )
    def dispatch(table_hbm, dest_hbm, out_hbm, dest_v, idx_v, rows_v, gsem, wsem):
        base = (lax.axis_index("s") * SC_CORES + lax.axis_index("c")) * per_worker
        pltpu.sync_copy(dest_hbm, dest_v)
        lane = lax.iota(I32, lanes)

        @pl.loop(0, per_worker // lanes)
        def _(i):
            idx_v[pl.ds(i * lanes, lanes)] = lax.rem(base + i * lanes + lane, n)

        @plsc.parallel_loop(0, a // lanes, unroll=8)
        def _(i):
            local = dest_v[pl.ds(i * lanes, lanes)] - base
            mine = (local >= 0) & (local < per_worker)
            plsc.store_scatter(idx_v, [jnp.where(mine, local, 0)], lax.rem(i * lanes + lane, n), mask=mine)

        _sc_gather_rows(table_hbm, out_hbm, idx_v, rows_v, gsem, wsem, base, n_chunks)

    return dispatch(table, dest_flat)


def _expert_kernel(sched_ref, ni_ref, xs_hbm, w1_hbm, w3_hbm, w2_hbm, y_hbm,
                   x_buf, y_buf, w1_buf, w3_buf, w2_buf, w1_scr, w3_scr, w2_scr, wsems, xsems, ysems,
                   *, layer):
    i = pl.program_id(0)
    n_items = ni_ref[0]
    expert, wslot, run_start, next_expert = (sched_ref[r, i] for r in range(4))
    slot = i % 2

    def for_units(item, fn):
        for units in range(1, ITEM_UNITS + 1):
            @pl.when(sched_ref[5, item] == units)
            def _():
                fn(units * EXPERT_UNIT)

    def x_copy(item, s, rows):
        row0 = pl.multiple_of(sched_ref[4, item], EXPERT_UNIT)
        return pltpu.make_async_copy(xs_hbm.at[pl.ds(row0, rows)], x_buf.at[s, pl.ds(0, rows)], xsems.at[s])

    def y_copy(item, s, rows):
        row0 = pl.multiple_of(sched_ref[4, item], EXPERT_UNIT)
        return pltpu.make_async_copy(y_buf.at[s, pl.ds(0, rows)], y_hbm.at[pl.ds(row0, rows)], ysems.at[s])

    def fetch(e, s):
        return [pltpu.make_async_copy(w_hbm.at[layer, e], buf.at[s], wsems.at[s, j])
                for j, (w_hbm, buf) in enumerate(((w1_hbm, w1_buf), (w3_hbm, w3_buf), (w2_hbm, w2_buf)))]

    def mlp(rows):
        x_lo, x_hi = _unpack_halves(x_buf[slot, 0:rows, :])
        xb = jnp.concatenate([x_lo.astype(BF16), x_hi.astype(BF16)], axis=1)
        h1 = _dot(xb, w1_scr[...])
        h3 = _dot(xb, w3_scr[...])
        hid = h1 * (1.0 / (1.0 + jnp.exp(-h1))) * h3
        y_buf[slot, 0:rows, :] = _pack_halves(_dot(hid.astype(BF16), w2_scr[...]))
        y_copy(i, slot, rows).start()

    @pl.when(i < n_items)
    def _():
        @pl.when(i == 0)
        def _():
            for_units(0, lambda rows: x_copy(0, 0, rows).start())

        for_units(i, lambda rows: x_copy(i, slot, rows).wait())

        @pl.when(i + 1 < n_items)
        def _():
            for_units(i + 1, lambda rows: x_copy(i + 1, 1 - slot, rows).start())

        @pl.when(run_start == 1)
        def _():
            @pl.when(i == 0)
            def _():
                for c in fetch(expert, wslot):
                    c.start()

            for c in fetch(expert, wslot):
                c.wait()

            @pl.when(next_expert >= 0)
            def _():
                for c in fetch(next_expert, 1 - wslot):
                    c.start()

            w1_scr[...] = w1_buf[wslot].astype(BF16)
            w3_scr[...] = w3_buf[wslot].astype(BF16)
            w2_scr[...] = w2_buf[wslot].astype(BF16)

        @pl.when(i >= 2)
        def _():
            for_units(i - 2, lambda rows: y_copy(i - 2, slot, rows).wait())

        for_units(i, mlp)

        @pl.when(i == n_items - 1)
        def _():
            @pl.when(i >= 1)
            def _():
                for_units(i - 1, lambda rows: y_copy(i - 1, 1 - slot, rows).wait())

            for_units(i, lambda rows: y_copy(i, slot, rows).wait())
            y_buf[0, 0:EXPERT_UNIT, :] = jnp.zeros((EXPERT_UNIT, y_buf.shape[2]), y_buf.dtype)
            first_free = (sched_ref[4, i] + sched_ref[5, i] * EXPERT_UNIT) // EXPERT_UNIT

            def zero_copy(u):
                return pltpu.make_async_copy(
                    y_buf.at[0, pl.ds(0, EXPERT_UNIT)],
                    y_hbm.at[pl.ds(pl.multiple_of(u * EXPERT_UNIT, EXPERT_UNIT), EXPERT_UNIT)], ysems.at[0])

            def start_zero(u, _):
                zero_copy(u).start()
                return 0

            def wait_zero(u, _):
                zero_copy(u).wait()
                return 0

            lax.fori_loop(first_free, y_hbm.shape[0] // EXPERT_UNIT, start_zero, 0)
            lax.fori_loop(first_free, y_hbm.shape[0] // EXPERT_UNIT, wait_zero, 0)


def _expert_mlp(xs, schedule, n_items, layer, w1, w3, w2):
    d, de = w1.shape[2], w1.shape[3]
    max_rows = ITEM_UNITS * EXPERT_UNIT
    return pl.pallas_call(
        functools.partial(_expert_kernel, layer=layer),
        out_shape=jax.ShapeDtypeStruct(xs.shape, I32),
        grid_spec=pltpu.PrefetchScalarGridSpec(
            num_scalar_prefetch=2,
            grid=(schedule.shape[1],),
            in_specs=[_HBM, _HBM, _HBM, _HBM],
            out_specs=_HBM,
            scratch_shapes=[pltpu.VMEM((2, max_rows, d // 2), I32), pltpu.VMEM((2, max_rows, d // 2), I32),
                            pltpu.VMEM((2, d, de), F32), pltpu.VMEM((2, d, de), F32), pltpu.VMEM((2, de, d), F32),
                            pltpu.VMEM((d, de), BF16), pltpu.VMEM((d, de), BF16), pltpu.VMEM((de, d), BF16),
                            pltpu.SemaphoreType.DMA((2, 3)), pltpu.SemaphoreType.DMA((2,)),
                            pltpu.SemaphoreType.DMA((2,))],
        ),
        compiler_params=_params(),
        name="expert_mlp",
    )(schedule, n_items, xs, w1, w3, w2)


def _combine_kernel(x_ref, y0_ref, y1_ref, gate_ref, g_ref, b_ref, o_ref):
    o_ref[...] = _moe_output(x_ref, y0_ref, y1_ref, gate_ref, g_ref, b_ref)


def _combine(x, yg, gates, ln_g, ln_b, ln_row):
    n, d = x.shape
    tiles = n // TM
    return pl.pallas_call(
        _combine_kernel,
        out_shape=jax.ShapeDtypeStruct((n, d), F32),
        grid=(tiles,),
        in_specs=[pl.BlockSpec((TM, d), lambda i: (i, 0)),
                  pl.BlockSpec((TM, d // 2), lambda i: (i, 0)),
                  pl.BlockSpec((TM, d // 2), lambda i: (i + tiles, 0)),
                  pl.BlockSpec((TM, TOP_K), lambda i: (i, 0)),
                  _pick_spec(ln_g, ln_row), _pick_spec(ln_b, ln_row)],
        out_specs=pl.BlockSpec((TM, d), lambda i: (i, 0)),
        compiler_params=_params(),
        name="combine",
    )(x, yg, yg, gates, ln_g, ln_b)


def _moe_experts(x_packed, table, counts, layer, w1, w3, w2):
    n = x_packed.shape[0]
    max_units = (n * TOP_K + N_EXPERTS * (EXPERT_UNIT - 1) + EXPERT_UNIT - 1) // EXPERT_UNIT
    max_items = (max_units + N_EXPERTS * (ITEM_UNITS - 1) + ITEM_UNITS - 1) // ITEM_UNITS
    experts = table[:TOP_K].astype(I32)
    ranks = table[TOP_K:2 * TOP_K].astype(I32)

    units_e = (counts + EXPERT_UNIT - 1) // EXPERT_UNIT
    units_start = jnp.cumsum(units_e) - units_e
    items_e = (units_e + ITEM_UNITS - 1) // ITEM_UNITS
    items_end = jnp.cumsum(items_e)
    n_items = items_end[-1:].astype(I32)
    item_ids = jnp.arange(max_items, dtype=I32)
    expert_ids = jnp.arange(N_EXPERTS, dtype=I32)
    item_expert = jnp.minimum(jnp.sum(items_end[None, :] <= item_ids[:, None], axis=1), N_EXPERTS - 1)
    later = (expert_ids[None, :] > expert_ids[:, None]) & (items_e[None, :] > 0)
    next_run = jnp.min(jnp.where(later, expert_ids[None, :], N_EXPERTS), axis=1)
    next_run = jnp.where(next_run == N_EXPERTS, -1, next_run)
    per_expert = jnp.stack([items_end - items_e, units_start, units_e, next_run], axis=1)
    mine = (item_expert[:, None] == expert_ids[None, :])[:, :, None]
    first_item, unit0, units, next_expert = jnp.sum(jnp.where(mine, per_expert[None], 0), axis=1).T
    within = item_ids - first_item
    item_row0 = (unit0 + ITEM_UNITS * within) * EXPERT_UNIT
    item_units = jnp.clip(units - ITEM_UNITS * within, 1, ITEM_UNITS)
    run_start_flag = ((within == 0) & (item_ids < n_items[0])).astype(I32)
    slot = (jnp.cumsum(run_start_flag) - 1) % 2
    schedule = jnp.stack([item_expert, slot, run_start_flag, next_expert, item_row0, item_units]).astype(I32)
    start_of = jnp.sum(jnp.where(experts[:, :, None] == expert_ids, units_start * EXPERT_UNIT, 0), axis=-1)
    dest = (start_of + ranks).astype(I32).reshape(-1)
    xs = _sc_dispatch(x_packed, dest, max_units * EXPERT_UNIT)
    y = _expert_mlp(xs, schedule, n_items, layer, w1, w3, w2)
    return _sc_gather(y, dest)


def kernel(x, mem, w_in_even, w_pool, pool_scale, ln_v_g, ln_v_b, w_spatial, b_spatial, w_out_even,
           w_in_odd, conv_w, conv_b, w_out_odd, wq_x, wk_x, wv_x, wo_x, ln_g, ln_b, wr_group,
           br_group, wr_expert, br_expert, w1, w3, w2):
    bsz, seq, d = x.shape
    assert seq % TM == 0 and d % LANES == 0
    mlen = mem.shape[1]
    mem2d = mem.reshape(bsz * mlen, d)
    ln_g = ln_g.reshape(DEPTH * 3, 1, d)
    ln_b = ln_b.reshape(DEPTH * 3, 1, d)
    w_pool = w_pool.astype(BF16)
    pool_scale, ln_v_g, ln_v_b, conv_b = (p[:, None, :] for p in (pool_scale, ln_v_g, ln_v_b, conv_b))
    b_spatial_t = jnp.swapaxes(b_spatial, 1, 2)
    conv_w_t = jnp.swapaxes(conv_w, 1, 2)
    router_w, router_b = _router_weights(wr_group, br_group, wr_expert, br_expert)

    h = x.reshape(bsz * seq, d)
    pending = None
    kv = _memory_kv(mem2d, wk_x, wv_x, 0)
    for l in range(DEPTH):
        i = l // 2
        k, v = (a.reshape(bsz, mlen, d) for a in kv)
        if l % 2 == 0:
            h = _even_mixer(h, pending, seq, i, w_in_even, w_pool, pool_scale, ln_v_g, ln_v_b,
                            w_spatial, b_spatial_t, w_out_even, ln_g, ln_b, 3 * l)
        else:
            h = _odd_mixer(h, pending, seq, i, w_in_odd, conv_w_t, conv_b, w_out_odd, ln_g, ln_b, 3 * l)
        h, hp, table, gates, counts = _cross_attn(h, seq, l, k, v, wq_x, wo_x,
                                                  ln_g, ln_b, 3 * l + 1, router_w, router_b)
        if l + 1 < DEPTH:
            kv = _memory_kv(mem2d, wk_x, wv_x, l + 1)
        yg = _moe_experts(hp, table, counts, l, w1, w3, w2)
        pending = (yg, gates, ln_g, ln_b, 3 * l + 2)
    return _combine(h, *pending).reshape(bsz, seq, d)
```

```python
import functools
import math

import jax
import jax.numpy as jnp
from jax import lax
from jax.experimental import pallas as pl
from jax.experimental.pallas import tpu as pltpu
from jax.experimental.pallas import tpu_sc as plsc

F32 = jnp.float32
BF16 = jnp.bfloat16
I32 = jnp.int32

POOL_WINDOWS = (2, 4, 8, 16)
assert all(w & (w - 1) == 0 for w in POOL_WINDOWS)
N_SG_HEADS = 4
CHUNK = 128
CONV_WIDTH = 3
N_XHEADS = 4
N_GROUPS = 4
EXPERTS_PER_GROUP = 8
N_EXPERTS = N_GROUPS * EXPERTS_PER_GROUP
TOP_K = 2
DEPTH = 4
ALPHA = (2.0 * DEPTH) ** 0.25
LN_EPS = 1e-5

LANES = 128
SC_CORES = 2
SC_WORKERS = 32
SC_LANES = 16
VMEM_LIMIT = 56 * 1024 * 1024

SC_CHUNK = 64
TM = 1024
SUB_TILES = 2
ROUTE_ROWS = 8
POOL_HALO = 16
CONV_HALO = 8
EXPERT_UNIT = 128
ITEM_UNITS = 4
ROUTE_COLS = 128
EXPERT_ROW0 = 8
STAGE_COLS = 512

_NT = (((1,), (1,)), ((), ()))


def _dot(a, b):
    return jnp.dot(a, b, preferred_element_type=F32)


def _layer_norm(y, g, b):
    mu = jnp.mean(y, axis=-1, keepdims=True)
    yc = y - mu
    var = jnp.mean(yc * yc, axis=-1, keepdims=True)
    return yc * lax.rsqrt(var + LN_EPS) * g + b


def _gelu_tanh(x):
    c = math.sqrt(2.0 / math.pi)
    return 0.5 * x * (1.0 + jnp.tanh(c * (x + 0.044715 * (x * x * x))))


def _pack_halves(v):
    c = v.shape[1] // 2
    lo = pltpu.bitcast(v[:, :c].astype(BF16).astype(F32), jnp.uint32)
    hi = pltpu.bitcast(v[:, c:].astype(BF16).astype(F32), jnp.uint32)
    return pltpu.bitcast((hi & jnp.uint32(0xFFFF0000)) | (lo >> 16), I32)


def _unpack_halves(w):
    u = pltpu.bitcast(w, jnp.uint32)
    return pltpu.bitcast(u << 16, F32), pltpu.bitcast(u & jnp.uint32(0xFFFF0000), F32)


def _load_cast(w_hbm, w_scr, stage, sems):
    chunks = w_scr.shape[1] // STAGE_COLS

    def chunk_copy(c):
        return pltpu.make_async_copy(w_hbm.at[:, pl.ds(c * STAGE_COLS, STAGE_COLS)], stage.at[c % 2],
                                     sems.at[c % 2])

    chunk_copy(0).start()
    for c in range(chunks):
        if c + 1 < chunks:
            chunk_copy(c + 1).start()
        chunk_copy(c).wait()
        w_scr[:, c * STAGE_COLS:(c + 1) * STAGE_COLS] = stage[c % 2].astype(BF16)


def _stage_scratch(rows):
    return [pltpu.VMEM((2, rows, STAGE_COLS), F32), pltpu.SemaphoreType.DMA((2,))]


_HBM = pl.BlockSpec(memory_space=pl.ANY)


def _const_spec(shape):
    nd = len(shape)
    return pl.BlockSpec(shape, lambda i: (0,) * nd)


def _pick_spec(stacked, index):
    rest = stacked.shape[1:]
    return pl.BlockSpec((None,) + rest, lambda i: (index,) + (0,) * len(rest))


def _params():
    return pltpu.CompilerParams(dimension_semantics=("arbitrary",), vmem_limit_bytes=VMEM_LIMIT)


def _moe_output(x_ref, y0_ref, y1_ref, gate_ref, g_ref, b_ref):
    gates = gate_ref[...]
    g0, g1 = gates[:, 0:1], gates[:, 1:2]
    y0_lo, y0_hi = _unpack_halves(y0_ref[...])
    y1_lo, y1_hi = _unpack_halves(y1_ref[...])
    ff = jnp.concatenate([g0 * y0_lo + g1 * y1_lo, g0 * y0_hi + g1 * y1_hi], axis=1)
    return _layer_norm(ALPHA * x_ref[...] + ff, g_ref[...], b_ref[...])


def _mixer_input(src, pending):
    if not pending:
        x_ref, xh_ref = src
        return x_ref[...], xh_ref[...]
    x_ref, xh_ref, y0_ref, y0h_ref, y1_ref, y1h_ref, gate_ref, gateh_ref, g_ref, b_ref = src
    return (_moe_output(x_ref, y0_ref, y1_ref, gate_ref, g_ref, b_ref),
            _moe_output(xh_ref, y0h_ref, y1h_ref, gateh_ref, g_ref, b_ref))


def _mixer_sources(x, pending, halo):
    n, d = x.shape
    tiles = n // TM
    halo_blocks = TM // halo

    def halo_index(i):
        return jnp.maximum(i * halo_blocks - 1, 0)

    specs = [pl.BlockSpec((TM, d), lambda i: (i, 0)), pl.BlockSpec((halo, d), lambda i: (halo_index(i), 0))]
    args = [x, x]
    if pending is not None:
        yg, gates, ln_g, ln_b, ln_row = pending
        specs += [pl.BlockSpec((TM, d // 2), lambda i: (i, 0)),
                  pl.BlockSpec((halo, d // 2), lambda i: (halo_index(i), 0)),
                  pl.BlockSpec((TM, d // 2), lambda i: (i + tiles, 0)),
                  pl.BlockSpec((halo, d // 2), lambda i: (halo_index(i) + tiles * halo_blocks, 0)),
                  pl.BlockSpec((TM, TOP_K), lambda i: (i, 0)),
                  pl.BlockSpec((halo, TOP_K), lambda i: (halo_index(i), 0)),
                  _pick_spec(ln_g, ln_row), _pick_spec(ln_b, ln_row)]
        args += [yg, yg, yg, yg, gates, gates, ln_g, ln_b]
    return specs, args


def _even_kernel(*refs, tiles_per_seq, layer, pending):
    n_src = 10 if pending else 2
    (win_hbm, wpool_ref, pscale_ref, lvg_ref, lvb_ref, ws_ref, bst_ref, wout_hbm, g_ref, b_ref, o_ref,
     a_scr, cat_scr, win_ref, wout_ref, stage, sems) = refs[n_src:]
    tm = o_ref.shape[0]
    d_pool = a_scr.shape[1]
    d_sg = lvg_ref.shape[1]
    pgd = d_pool // len(POOL_WINDOWS)
    hd_dim = d_sg // N_SG_HEADS
    seq_tile = pl.program_id(0) % tiles_per_seq

    @pl.when(pl.program_id(0) == 0)
    def _():
        _load_cast(win_hbm.at[layer], win_ref, stage, sems)
        _load_cast(wout_hbm.at[layer], wout_ref, stage, sems)

    x, xh = _mixer_input(refs[:n_src], pending)
    sub = tm // SUB_TILES

    ah = _dot(xh.astype(BF16), win_ref[:, :d_pool])
    a_scr[0:POOL_HALO, :] = jnp.where(seq_tile == 0, 0.0, ah)
    row = lax.broadcasted_iota(I32, (CHUNK, CHUNK), 0)
    col = lax.broadcasted_iota(I32, (CHUNK, CHUNK), 1)
    ws_masked = [jnp.where(row >= col, ws_ref[hd], 0.0).astype(BF16) for hd in range(N_SG_HEADS)]

    def in_proj(st):
        h = _dot(x[st * sub:(st + 1) * sub, :].astype(BF16), win_ref[...])
        a_scr[POOL_HALO + st * sub:POOL_HALO + (st + 1) * sub, :] = h[:, :d_pool]
        return h[:, d_pool:]

    def branches_out_proj(st, hz):
        base = st * sub
        pos = seq_tile * tm + base + lax.broadcasted_iota(I32, (sub, 1), 0)
        for g, w in enumerate(POOL_WINDOWS):
            cs = slice(g * pgd, (g + 1) * pgd)
            tok = a_scr[POOL_HALO + base:POOL_HALO + base + sub, cs]
            acc = a_scr[base:POOL_HALO + base + sub, cs]
            span = 1
            while span < w:
                acc = acc[span:, :] + acc[:-span, :]
                span *= 2
            acc = acc[acc.shape[0] - sub:, :]
            cnt = jnp.minimum(pos + 1, w).astype(F32)
            dev = acc * (1.0 / cnt) - tok
            yg = _dot(dev.astype(BF16), wpool_ref[g])
            cat_scr[base:base + sub, cs] = (yg * pscale_ref[:, cs]).astype(BF16)

        z = _gelu_tanh(hz)
        u = z[:, :d_sg]
        v = _layer_norm(z[:, d_sg:], lvg_ref[...], lvb_ref[...]).astype(BF16)
        for hd in range(N_SG_HEADS):
            hs = slice(hd * hd_dim, (hd + 1) * hd_dim)
            bcol = bst_ref[:, hd:hd + 1]
            for ck in range(sub // CHUNK):
                rs = slice(ck * CHUNK, (ck + 1) * CHUNK)
                sv = _dot(ws_masked[hd], v[rs, hs]) + bcol
                cat_scr[base + ck * CHUNK:base + (ck + 1) * CHUNK,
                        d_pool + hd * hd_dim:d_pool + (hd + 1) * hd_dim] = (u[rs, hs] * sv).astype(BF16)
        return _dot(cat_scr[base:base + sub, :], wout_ref[...])

    hzs = [in_proj(st) for st in range(SUB_TILES)]
    mixes = [branches_out_proj(st, hzs[st]) for st in range(SUB_TILES)]
    for st in range(SUB_TILES):
        rs = slice(st * sub, (st + 1) * sub)
        o_ref[rs, :] = _layer_norm(ALPHA * x[rs, :] + mixes[st], g_ref[...], b_ref[...])


def _even_mixer(x, pending, seq, layer, w_in, w_pool, pool_scale, ln_v_g, ln_v_b, w_spatial, b_spatial_t,
                w_out, ln_g, ln_b, ln_row):
    n, d = x.shape
    d_in = w_in.shape[2]
    d_pool = pool_scale.shape[2]
    d_sg = ln_v_g.shape[2]
    kern = functools.partial(_even_kernel, tiles_per_seq=seq // TM, layer=layer, pending=pending is not None)
    src_specs, src_args = _mixer_sources(x, pending, POOL_HALO)
    return pl.pallas_call(
        kern,
        out_shape=jax.ShapeDtypeStruct((n, d), F32),
        grid=(n // TM,),
        in_specs=src_specs + [
            _HBM,
            _pick_spec(w_pool, layer),
            _pick_spec(pool_scale, layer),
            _pick_spec(ln_v_g, layer),
            _pick_spec(ln_v_b, layer),
            _pick_spec(w_spatial, layer),
            _pick_spec(b_spatial_t, layer),
            _HBM,
            _pick_spec(ln_g, ln_row),
            _pick_spec(ln_b, ln_row),
        ],
        out_specs=pl.BlockSpec((TM, d), lambda i: (i, 0)),
        scratch_shapes=[pltpu.VMEM((POOL_HALO + TM, d_pool), F32), pltpu.VMEM((TM, d_pool + d_sg), BF16),
                        pltpu.VMEM((d, d_in), BF16), pltpu.VMEM((d_pool + d_sg, d), BF16)] + _stage_scratch(d),
        compiler_params=_params(),
        name="even_mixer",
    )(*src_args, w_in, w_pool, pool_scale, ln_v_g, ln_v_b, w_spatial, b_spatial_t, w_out, ln_g, ln_b)


def _odd_kernel(*refs, tiles_per_seq, layer, pending):
    n_src = 10 if pending else 2
    (win_hbm, cwt_ref, cb_ref, wout_hbm, g_ref, b_ref, o_ref, zc_scr,
     win_ref, wout_ref, stage, sems) = refs[n_src:]
    tm, d = o_ref.shape
    seq_tile = pl.program_id(0) % tiles_per_seq

    @pl.when(pl.program_id(0) == 0)
    def _():
        _load_cast(win_hbm.at[layer], win_ref, stage, sems)
        _load_cast(wout_hbm.at[layer], wout_ref, stage, sems)

    x, xh = _mixer_input(refs[:n_src], pending)
    hh = _dot(xh.astype(BF16), win_ref[:, d:])
    zc_scr[0:CONV_HALO, :] = jnp.where(seq_tile == 0, 0.0, hh[:, :d] * hh[:, d:])
    sub = tm // SUB_TILES

    def in_proj(st):
        xb = x[st * sub:(st + 1) * sub, :].astype(BF16)
        hc = _dot(xb, win_ref[:, d:2 * d])
        hz = _dot(xb, win_ref[:, 2 * d:])
        zc_scr[CONV_HALO + st * sub:CONV_HALO + (st + 1) * sub, :] = hc * hz
        return _dot(xb, win_ref[:, :d])

    def conv_out_proj(st, gate):
        conv = cb_ref[...]
        for j in range(CONV_WIDTH):
            off = CONV_HALO + st * sub - (CONV_WIDTH - 1) + j
            conv = conv + zc_scr[off:off + sub, :] * cwt_ref[j:j + 1, :]
        return _dot((gate * conv).astype(BF16), wout_ref[...])

    gates = [in_proj(st) for st in range(SUB_TILES)]
    ys = [conv_out_proj(st, gates[st]) for st in range(SUB_TILES)]
    for st in range(SUB_TILES):
        rs = slice(st * sub, (st + 1) * sub)
        o_ref[rs, :] = _layer_norm(ALPHA * x[rs, :] + ys[st], g_ref[...], b_ref[...])


def _odd_mixer(x, pending, seq, layer, w_in, conv_w_t, conv_b, w_out, ln_g, ln_b, ln_row):
    n, d = x.shape
    kern = functools.partial(_odd_kernel, tiles_per_seq=seq // TM, layer=layer, pending=pending is not None)
    src_specs, src_args = _mixer_sources(x, pending, CONV_HALO)
    return pl.pallas_call(
        kern,
        out_shape=jax.ShapeDtypeStruct((n, d), F32),
        grid=(n // TM,),
        in_specs=src_specs + [
            _HBM,
            _pick_spec(conv_w_t, layer),
            _pick_spec(conv_b, layer),
            _HBM,
            _pick_spec(ln_g, ln_row),
            _pick_spec(ln_b, ln_row),
        ],
        out_specs=pl.BlockSpec((TM, d), lambda i: (i, 0)),
        scratch_shapes=[pltpu.VMEM((CONV_HALO + TM, d), F32),
                        pltpu.VMEM(w_in.shape[1:], BF16), pltpu.VMEM(w_out.shape[1:], BF16)] + _stage_scratch(d),
        compiler_params=_params(),
        name="odd_mixer",
    )(*src_args, w_in, conv_w_t, conv_b, w_out, ln_g, ln_b)


def _kv_kernel(mem_ref, wk_ref, wv_ref, k_ref, v_ref):
    m = mem_ref[...].astype(BF16)
    k_ref[...] = _dot(m, wk_ref[...].astype(BF16)).astype(BF16)
    v_ref[...] = _dot(m, wv_ref[...].astype(BF16)).astype(BF16)


def _memory_kv(mem2d, wk, wv, layer):
    rows, d = mem2d.shape
    out = jax.ShapeDtypeStruct((rows, d), BF16)
    return pl.pallas_call(
        _kv_kernel,
        out_shape=(out, out),
        grid=(1,),
        in_specs=[_const_spec((rows, d)), _pick_spec(wk, layer), _pick_spec(wv, layer)],
        out_specs=(_const_spec((rows, d)), _const_spec((rows, d))),
        compiler_params=_params(),
        name="memory_kv",
    )(mem2d, wk, wv)


def _attn_kernel(x_ref, k_ref, v_ref, wq_hbm, wo_hbm, g_ref, b_ref, wr_ref, br_ref,
                 o_ref, op_ref, rt_ref, rg_ref, cnt_ref, o_scr, carry_scr, wq_ref, wo_ref, stage, sems,
                 *, layer):
    tm, d = x_ref.shape
    hd_dim = d // N_XHEADS
    sub = tm // SUB_TILES

    @pl.when(pl.program_id(0) == 0)
    def _():
        carry_scr[...] = jnp.zeros_like(carry_scr)
        _load_cast(wq_hbm.at[layer], wq_ref, stage, sems)
        _load_cast(wo_hbm.at[layer], wo_ref, stage, sems)

    row_slices = [slice(st * sub, (st + 1) * sub) for st in range(SUB_TILES)]
    qs = [_dot(x_ref[rs, :].astype(BF16), wq_ref[...]) * (1.0 / math.sqrt(hd_dim)) for rs in row_slices]
    for rs, q in zip(row_slices, qs):
        for hd in range(N_XHEADS):
            hs = slice(hd * hd_dim, (hd + 1) * hd_dim)
            s = lax.dot_general(q[:, hs].astype(BF16), k_ref[:, hs], _NT, preferred_element_type=F32)
            p = jnp.exp(s - jnp.max(s, axis=-1, keepdims=True))
            p = p * (1.0 / jnp.sum(p, axis=-1, keepdims=True))
            o_scr[rs, hs] = _dot(p.astype(BF16), v_ref[:, hs]).astype(BF16)
    xas = [_dot(o_scr[rs, :], wo_ref[...]) for rs in row_slices]
    for rs, xa in zip(row_slices, xas):
        out = _layer_norm(ALPHA * x_ref[rs, :] + xa, g_ref[...], b_ref[...])
        o_ref[rs, :] = out
        op_ref[rs, :] = _pack_halves(out)
        table, gates = _route_rows(out, wr_ref, br_ref, carry_scr)
        rt_ref[:, rs] = table
        rg_ref[rs, :] = gates
    cnt_ref[...] = carry_scr[...].astype(I32)


def _cross_attn(x, seq, layer, k, v, wq, wo, ln_g, ln_b, ln_row, router_w, router_b):
    n, d = x.shape
    m = k.shape[1]
    tiles_per_seq = seq // TM
    kvspec = pl.BlockSpec((None, m, d), lambda i: (i // tiles_per_seq, 0, 0))
    out, packed, table, gates, cnt = pl.pallas_call(
        functools.partial(_attn_kernel, layer=layer),
        out_shape=(jax.ShapeDtypeStruct((n, d), F32), jax.ShapeDtypeStruct((n, d // 2), I32),
                   jax.ShapeDtypeStruct((ROUTE_ROWS, n), F32), jax.ShapeDtypeStruct((n, TOP_K), F32),
                   jax.ShapeDtypeStruct((N_EXPERTS, LANES), I32)),
        grid=(n // TM,),
        in_specs=[
            pl.BlockSpec((TM, d), lambda i: (i, 0)),
            kvspec, kvspec,
            _HBM, _HBM,
            _pick_spec(ln_g, ln_row), _pick_spec(ln_b, ln_row),
            _pick_spec(router_w, layer), _pick_spec(router_b, layer),
        ],
        out_specs=(pl.BlockSpec((TM, d), lambda i: (i, 0)), pl.BlockSpec((TM, d // 2), lambda i: (i, 0)),
                   pl.BlockSpec((ROUTE_ROWS, TM), lambda i: (0, i)), pl.BlockSpec((TM, TOP_K), lambda i: (i, 0)),
                   _const_spec((N_EXPERTS, LANES))),
        scratch_shapes=[pltpu.VMEM((TM, d), BF16), pltpu.VMEM((N_EXPERTS, LANES), F32),
                        pltpu.VMEM((d, d), BF16), pltpu.VMEM((d, d), BF16)] + _stage_scratch(d),
        compiler_params=_params(),
        name="cross_attn",
    )(x, k, v, wq, wo, ln_g, ln_b, router_w, router_b)
    return out, packed, table, gates, cnt[:, 0]


def _route_rows(x, w_ref, bias_ref, carry_scr):
    tm = x.shape[0]
    neg = -jnp.inf
    logits = _dot(x.astype(BF16), w_ref[...]) + bias_ref[...]
    lt = jnp.transpose(logits)

    def first_argmax(vals):
        rows = lax.broadcasted_iota(I32, vals.shape, 0).astype(F32)
        mx = jnp.max(vals, axis=0, keepdims=True)
        idx = jnp.min(jnp.where(vals == mx, rows, float(vals.shape[0])), axis=0, keepdims=True)
        return mx, idx, rows

    gl = lt[0:N_GROUPS, :]
    gmax, g_sel, _ = first_argmax(gl)
    gate_g = 1.0 / jnp.sum(jnp.exp(gl - gmax), axis=0, keepdims=True)

    el = lt[EXPERT_ROW0:EXPERT_ROW0 + EXPERTS_PER_GROUP, :]
    for g in range(1, N_GROUPS):
        lo = EXPERT_ROW0 + g * EXPERTS_PER_GROUP
        el = jnp.where(g_sel == float(g), lt[lo:lo + EXPERTS_PER_GROUP, :], el)
    m1, i1, erow = first_argmax(el)
    m2, i2, _ = first_argmax(jnp.where(erow == i1, neg, el))
    e21 = jnp.exp(m2 - m1)
    w1 = 1.0 / (1.0 + e21)
    w2 = e21 / (1.0 + e21)
    e1 = g_sel * EXPERTS_PER_GROUP + i1
    e2 = g_sel * EXPERTS_PER_GROUP + i2

    xrow = lax.broadcasted_iota(I32, (N_EXPERTS, tm), 0).astype(F32)
    oh1 = xrow == e1
    oh2 = xrow == e2
    oh = (oh1 | oh2).astype(BF16)
    r = lax.broadcasted_iota(I32, (tm, tm), 0)
    c = lax.broadcasted_iota(I32, (tm, tm), 1)
    before = _dot(oh, (r < c).astype(BF16)) + carry_scr[:, 0:1]
    rank1 = jnp.sum(jnp.where(oh1, before, 0.0), axis=0, keepdims=True)
    rank2 = jnp.sum(jnp.where(oh2, before, 0.0), axis=0, keepdims=True)
    carry_scr[...] += jnp.sum(oh.astype(F32), axis=1, keepdims=True)

    trow = lax.broadcasted_iota(I32, (ROUTE_ROWS, tm), 0)
    table = jnp.where(trow == 0, e1, jnp.where(trow == 1, e2, jnp.where(trow == 2, rank1,
                                                                      jnp.where(trow == 3, rank2, 0.0))))
    grow = lax.broadcasted_iota(I32, (LANES, tm), 0)
    gates = jnp.transpose(jnp.where(grow == 0, gate_g * w1, jnp.where(grow == 1, gate_g * w2, 0.0)))[:, :TOP_K]
    return table, gates


def _router_weights(wr_g, br_g, wr_e, br_e):
    nl, d, _ = wr_g.shape
    gap = EXPERT_ROW0 - N_GROUPS
    w = jnp.concatenate([wr_g, jnp.zeros((nl, d, gap), F32),
                         jnp.transpose(wr_e, (0, 2, 1, 3)).reshape(nl, d, N_EXPERTS)], axis=2)
    w = jnp.pad(w, ((0, 0), (0, 0), (0, ROUTE_COLS - w.shape[2]))).astype(BF16)
    bias = jnp.concatenate([br_g, jnp.zeros((nl, gap), F32), br_e.reshape(nl, -1)], axis=1)
    bias = jnp.pad(bias, ((0, 0), (0, ROUTE_COLS - bias.shape[1])))
    return w, bias[:, None, :]


def _sc_worker_rows(rows):
    per_worker = rows // SC_WORKERS
    n_chunks = per_worker // SC_CHUNK
    assert per_worker * SC_WORKERS == rows and n_chunks * SC_CHUNK == per_worker and n_chunks % 2 == 0
    return per_worker, n_chunks


def _sc_gather_rows(table_hbm, out_hbm, idx_v, rows_v, gsem, wsem, base, n_chunks):
    def fetch(c, slot):
        off = pl.multiple_of(c * SC_CHUNK, SC_CHUNK)
        return pltpu.make_async_copy(table_hbm.at[idx_v.at[pl.ds(off, SC_CHUNK)]], rows_v.at[slot],
                                     gsem.at[slot])

    def put(c, slot):
        off = pl.multiple_of(c * SC_CHUNK, SC_CHUNK)
        return pltpu.make_async_copy(rows_v.at[slot], out_hbm.at[pl.ds(base + off, SC_CHUNK)], wsem.at[slot])

    fetch(0, 0).start()

    @pl.loop(0, n_chunks, step=2)
    def _(c0):
        for slot in range(2):
            c = c0 + slot

            @pl.when(c + 1 < n_chunks)
            def _():
                @pl.when(c >= 1)
                def _():
                    put(c - 1, 1 - slot).wait()
                fetch(c + 1, 1 - slot).start()

            fetch(c, slot).wait()
            put(c, slot).start()

    put(n_chunks - 2, 0).wait()
    put(n_chunks - 1, 1).wait()


def _sc_row_scratch(per_worker, d, dtype):
    return [pltpu.VMEM((per_worker,), I32), pltpu.VMEM((2, SC_CHUNK, d), dtype),
            pltpu.SemaphoreType.DMA((2,)), pltpu.SemaphoreType.DMA((2,))]


def _sc_gather(table, idx):
    b = idx.shape[0]
    d = table.shape[1]
    per_worker, n_chunks = _sc_worker_rows(b)
    mesh = plsc.VectorSubcoreMesh(core_axis_name="c", subcore_axis_name="s")

    @functools.partial(
        pl.kernel, mesh=mesh,
        out_type=jax.ShapeDtypeStruct((b, d), table.dtype),
        scratch_types=_sc_row_scratch(per_worker, d, table.dtype),
        name="sc_gather",
    )
    def gather(table_hbm, idx_hbm, out_hbm, idx_v, rows_v, gsem, wsem):
        base = (lax.axis_index("s") * SC_CORES + lax.axis_index("c")) * per_worker
        pltpu.sync_copy(idx_hbm.at[pl.ds(base, per_worker)], idx_v)
        _sc_gather_rows(table_hbm, out_hbm, idx_v, rows_v, gsem, wsem, base, n_chunks)

    return gather(table, idx)


def _sc_dispatch(table, dest_flat, rows):
    n, d = table.shape
    a = dest_flat.shape[0]
    lanes = SC_LANES
    per_worker, n_chunks = _sc_worker_rows(rows)
    assert a % lanes == 0 and per_worker % lanes == 0
    mesh = plsc.VectorSubcoreMesh(core_axis_name="c", subcore_axis_name="s")

    @functools.partial(
        pl.kernel, mesh=mesh,
        out_type=jax.ShapeDtypeStruct((rows, d), table.dtype),
        scratch_types=[pltpu.VMEM((a,), I32)] + _sc_row_scratch(per_worker, d, table.dtype),
        compiler_params=pltpu.CompilerParams(needs_layout_passes=False),
        name="sc_dispatch",
    )
    def dispatch(table_hbm, dest_hbm, out_hbm, dest_v, idx_v, rows_v, gsem, wsem):
        base = (lax.axis_index("s") * SC_CORES + lax.axis_index("c")) * per_worker
        pltpu.sync_copy(dest_hbm, dest_v)
        lane = lax.iota(I32, lanes)

        @pl.loop(0, per_worker // lanes)
        def _(i):
            idx_v[pl.ds(i * lanes, lanes)] = lax.rem(base + i * lanes + lane, n)

        @plsc.parallel_loop(0, a // lanes, unroll=8)
        def _(i):
            local = dest_v[pl.ds(i * lanes, lanes)] - base
            mine = (local >= 0) & (local < per_worker)
            plsc.store_scatter(idx_v, [jnp.where(mine, local, 0)], lax.rem(i * lanes + lane, n), mask=mine)

        _sc_gather_rows(table_hbm, out_hbm, idx_v, rows_v, gsem, wsem, base, n_chunks)

    return dispatch(table, dest_flat)


def _expert_kernel(sched_ref, ni_ref, xs_hbm, w1_hbm, w3_hbm, w2_hbm, y_hbm,
                   x_buf, y_buf, w1_buf, w3_buf, w2_buf, w1_scr, w3_scr, w2_scr, wsems, xsems, ysems,
                   *, layer):
    i = pl.program_id(0)
    n_items = ni_ref[0]
    expert, wslot, run_start, next_expert = (sched_ref[r, i] for r in range(4))
    slot = i % 2

    def for_units(item, fn):
        for units in range(1, ITEM_UNITS + 1):
            @pl.when(sched_ref[5, item] == units)
            def _():
                fn(units * EXPERT_UNIT)

    def x_copy(item, s, rows):
        row0 = pl.multiple_of(sched_ref[4, item], EXPERT_UNIT)
        return pltpu.make_async_copy(xs_hbm.at[pl.ds(row0, rows)], x_buf.at[s, pl.ds(0, rows)], xsems.at[s])

    def y_copy(item, s, rows):
        row0 = pl.multiple_of(sched_ref[4, item], EXPERT_UNIT)
        return pltpu.make_async_copy(y_buf.at[s, pl.ds(0, rows)], y_hbm.at[pl.ds(row0, rows)], ysems.at[s])

    def fetch(e, s):
        return [pltpu.make_async_copy(w_hbm.at[layer, e], buf.at[s], wsems.at[s, j])
                for j, (w_hbm, buf) in enumerate(((w1_hbm, w1_buf), (w3_hbm, w3_buf), (w2_hbm, w2_buf)))]

    def mlp(rows):
        x_lo, x_hi = _unpack_halves(x_buf[slot, 0:rows, :])
        xb = jnp.concatenate([x_lo.astype(BF16), x_hi.astype(BF16)], axis=1)
        h1 = _dot(xb, w1_scr[...])
        h3 = _dot(xb, w3_scr[...])
        hid = h1 * (1.0 / (1.0 + jnp.exp(-h1))) * h3
        y_buf[slot, 0:rows, :] = _pack_halves(_dot(hid.astype(BF16), w2_scr[...]))
        y_copy(i, slot, rows).start()

    @pl.when(i < n_items)
    def _():
        @pl.when(i == 0)
        def _():
            for_units(0, lambda rows: x_copy(0, 0, rows).start())

        for_units(i, lambda rows: x_copy(i, slot, rows).wait())

        @pl.when(i + 1 < n_items)
        def _():
            for_units(i + 1, lambda rows: x_copy(i + 1, 1 - slot, rows).start())

        @pl.when(run_start == 1)
        def _():
            @pl.when(i == 0)
            def _():
                for c in fetch(expert, wslot):
                    c.start()

            for c in fetch(expert, wslot):
                c.wait()

            @pl.when(next_expert >= 0)
            def _():
                for c in fetch(next_expert, 1 - wslot):
                    c.start()

            w1_scr[...] = w1_buf[wslot].astype(BF16)
            w3_scr[...] = w3_buf[wslot].astype(BF16)
            w2_scr[...] = w2_buf[wslot].astype(BF16)

        @pl.when(i >= 2)
        def _():
            for_units(i - 2, lambda rows: y_copy(i - 2, slot, rows).wait())

        for_units(i, mlp)

        @pl.when(i == n_items - 1)
        def _():
            @pl.when(i >= 1)
            def _():
                for_units(i - 1, lambda rows: y_copy(i - 1, 1 - slot, rows).wait())

            for_units(i, lambda rows: y_copy(i, slot, rows).wait())
            y_buf[0, 0:EXPERT_UNIT, :] = jnp.zeros((EXPERT_UNIT, y_buf.shape[2]), y_buf.dtype)
            first_free = (sched_ref[4, i] + sched_ref[5, i] * EXPERT_UNIT) // EXPERT_UNIT

            def zero_copy(u):
                return pltpu.make_async_copy(
                    y_buf.at[0, pl.ds(0, EXPERT_UNIT)],
                    y_hbm.at[pl.ds(pl.multiple_of(u * EXPERT_UNIT, EXPERT_UNIT), EXPERT_UNIT)], ysems.at[0])

            def start_zero(u, _):
                zero_copy(u).start()
                return 0

            def wait_zero(u, _):
                zero_copy(u).wait()
                return 0

            lax.fori_loop(first_free, y_hbm.shape[0] // EXPERT_UNIT, start_zero, 0)
            lax.fori_loop(first_free, y_hbm.shape[0] // EXPERT_UNIT, wait_zero, 0)


def _expert_mlp(xs, schedule, n_items, layer, w1, w3, w2):
    d, de = w1.shape[2], w1.shape[3]
    max_rows = ITEM_UNITS * EXPERT_UNIT
    return pl.pallas_call(
        functools.partial(_expert_kernel, layer=layer),
        out_shape=jax.ShapeDtypeStruct(xs.shape, I32),
        grid_spec=pltpu.PrefetchScalarGridSpec(
            num_scalar_prefetch=2,
            grid=(schedule.shape[1],),
            in_specs=[_HBM, _HBM, _HBM, _HBM],
            out_specs=_HBM,
            scratch_shapes=[pltpu.VMEM((2, max_rows, d // 2), I32), pltpu.VMEM((2, max_rows, d // 2), I32),
                            pltpu.VMEM((2, d, de), F32), pltpu.VMEM((2, d, de), F32), pltpu.VMEM((2, de, d), F32),
                            pltpu.VMEM((d, de), BF16), pltpu.VMEM((d, de), BF16), pltpu.VMEM((de, d), BF16),
                            pltpu.SemaphoreType.DMA((2, 3)), pltpu.SemaphoreType.DMA((2,)),
                            pltpu.SemaphoreType.DMA((2,))],
        ),
        compiler_params=_params(),
        name="expert_mlp",
    )(schedule, n_items, xs, w1, w3, w2)


def _combine_kernel(x_ref, y0_ref, y1_ref, gate_ref, g_ref, b_ref, o_ref):
    o_ref[...] = _moe_output(x_ref, y0_ref, y1_ref, gate_ref, g_ref, b_ref)


def _combine(x, yg, gates, ln_g, ln_b, ln_row):
    n, d = x.shape
    tiles = n // TM
    return pl.pallas_call(
        _combine_kernel,
        out_shape=jax.ShapeDtypeStruct((n, d), F32),
        grid=(tiles,),
        in_specs=[pl.BlockSpec((TM, d), lambda i: (i, 0)),
                  pl.BlockSpec((TM, d // 2), lambda i: (i, 0)),
                  pl.BlockSpec((TM, d // 2), lambda i: (i + tiles, 0)),
                  pl.BlockSpec((TM, TOP_K), lambda i: (i, 0)),
                  _pick_spec(ln_g, ln_row), _pick_spec(ln_b, ln_row)],
        out_specs=pl.BlockSpec((TM, d), lambda i: (i, 0)),
        compiler_params=_params(),
        name="combine",
    )(x, yg, yg, gates, ln_g, ln_b)


def _moe_experts(x_packed, table, counts, layer, w1, w3, w2):
    n = x_packed.shape[0]
    max_units = (n * TOP_K + N_EXPERTS * (EXPERT_UNIT - 1) + EXPERT_UNIT - 1) // EXPERT_UNIT
    max_items = (max_units + N_EXPERTS * (ITEM_UNITS - 1) + ITEM_UNITS - 1) // ITEM_UNITS
    experts = table[:TOP_K].astype(I32)
    ranks = table[TOP_K:2 * TOP_K].astype(I32)

    units_e = (counts + EXPERT_UNIT - 1) // EXPERT_UNIT
    units_start = jnp.cumsum(units_e) - units_e
    items_e = (units_e + ITEM_UNITS - 1) // ITEM_UNITS
    items_end = jnp.cumsum(items_e)
    n_items = items_end[-1:].astype(I32)
    item_ids = jnp.arange(max_items, dtype=I32)
    expert_ids = jnp.arange(N_EXPERTS, dtype=I32)
    item_expert = jnp.minimum(jnp.sum(items_end[None, :] <= item_ids[:, None], axis=1), N_EXPERTS - 1)
    later = (expert_ids[None, :] > expert_ids[:, None]) & (items_e[None, :] > 0)
    next_run = jnp.min(jnp.where(later, expert_ids[None, :], N_EXPERTS), axis=1)
    next_run = jnp.where(next_run == N_EXPERTS, -1, next_run)
    per_expert = jnp.stack([items_end - items_e, units_start, units_e, next_run], axis=1)
    mine = (item_expert[:, None] == expert_ids[None, :])[:, :, None]
    first_item, unit0, units, next_expert = jnp.sum(jnp.where(mine, per_expert[None], 0), axis=1).T
    within = item_ids - first_item
    item_row0 = (unit0 + ITEM_UNITS * within) * EXPERT_UNIT
    item_units = jnp.clip(units - ITEM_UNITS * within, 1, ITEM_UNITS)
    run_start_flag = ((within == 0) & (item_ids < n_items[0])).astype(I32)
    slot = (jnp.cumsum(run_start_flag) - 1) % 2
    schedule = jnp.stack([item_expert, slot, run_start_flag, next_expert, item_row0, item_units]).astype(I32)
    start_of = jnp.sum(jnp.where(experts[:, :, None] == expert_ids, units_start * EXPERT_UNIT, 0), axis=-1)
    dest = (start_of + ranks).astype(I32).reshape(-1)
    xs = _sc_dispatch(x_packed, dest, max_units * EXPERT_UNIT)
    y = _expert_mlp(xs, schedule, n_items, layer, w1, w3, w2)
    return _sc_gather(y, dest)


def kernel(x, mem, w_in_even, w_pool, pool_scale, ln_v_g, ln_v_b, w_spatial, b_spatial, w_out_even,
           w_in_odd, conv_w, conv_b, w_out_odd, wq_x, wk_x, wv_x, wo_x, ln_g, ln_b, wr_group,
           br_group, wr_expert, br_expert, w1, w3, w2):
    bsz, seq, d = x.shape
    assert seq % TM == 0 and d % LANES == 0
    mlen = mem.shape[1]
    mem2d = mem.reshape(bsz * mlen, d)
    ln_g = ln_g.reshape(DEPTH * 3, 1, d)
    ln_b = ln_b.reshape(DEPTH * 3, 1, d)
    w_pool = w_pool.astype(BF16)
    pool_scale, ln_v_g, ln_v_b, conv_b = (p[:, None, :] for p in (pool_scale, ln_v_g, ln_v_b, conv_b))
    b_spatial_t = jnp.swapaxes(b_spatial, 1, 2)
    conv_w_t = jnp.swapaxes(conv_w, 1, 2)
    router_w, router_b = _router_weights(wr_group, br_group, wr_expert, br_expert)

    h = x.reshape(bsz * seq, d)
    pending = None
    kv = _memory_kv(mem2d, wk_x, wv_x, 0)
    for l in range(DEPTH):
        i = l // 2
        k, v = (a.reshape(bsz, mlen, d) for a in kv)
        if l % 2 == 0:
            h = _even_mixer(h, pending, seq, i, w_in_even, w_pool, pool_scale, ln_v_g, ln_v_b,
                            w_spatial, b_spatial_t, w_out_even, ln_g, ln_b, 3 * l)
        else:
            h = _odd_mixer(h, pending, seq, i, w_in_odd, conv_w_t, conv_b, w_out_odd, ln_g, ln_b, 3 * l)
        h, hp, table, gates, counts = _cross_attn(h, seq, l, k, v, wq_x, wo_x,
                                                  ln_g, ln_b, 3 * l + 1, router_w, router_b)
        if l + 1 < DEPTH:
            kv = _memory_kv(mem2d, wk_x, wv_x, l + 1)
        yg = _moe_experts(hp, table, counts, l, w1, w3, w2)
        pending = (yg, gates, ln_g, ln_b, 3 * l + 2)
    return _combine(h, *pending).reshape(bsz, seq, d)
```

```python
import functools
import math

import jax
import jax.numpy as jnp
from jax import lax
from jax.experimental import pallas as pl
from jax.experimental.pallas import tpu as pltpu
from jax.experimental.pallas import tpu_sc as plsc

F32 = jnp.float32
BF16 = jnp.bfloat16
I32 = jnp.int32

POOL_WINDOWS = (2, 4, 8, 16)
assert all(w & (w - 1) == 0 for w in POOL_WINDOWS)
N_SG_HEADS = 4
CHUNK = 128
CONV_WIDTH = 3
N_XHEADS = 4
N_GROUPS = 4
EXPERTS_PER_GROUP = 8
N_EXPERTS = N_GROUPS * EXPERTS_PER_GROUP
TOP_K = 2
DEPTH = 4
ALPHA = (2.0 * DEPTH) ** 0.25
LN_EPS = 1e-5

LANES = 128
SC_CORES = 2
SC_WORKERS = 32
SC_LANES = 16
VMEM_LIMIT = 56 * 1024 * 1024

SC_CHUNK = 64
TM = 1024
SUB_TILES = 2
ROUTE_ROWS = 8
POOL_HALO = 16
CONV_HALO = 8
EXPERT_UNIT = 128
ITEM_UNITS = 8
ROUTE_COLS = 128
EXPERT_ROW0 = 8
STAGE_COLS = 512

_NT = (((1,), (1,)), ((), ()))


def _dot(a, b):
    return jnp.dot(a, b, preferred_element_type=F32)


def _layer_norm(y, g, b):
    mu = jnp.mean(y, axis=-1, keepdims=True)
    yc = y - mu
    var = jnp.mean(yc * yc, axis=-1, keepdims=True)
    return yc * lax.rsqrt(var + LN_EPS) * g + b


def _gelu_tanh(x):
    c = math.sqrt(2.0 / math.pi)
    return 0.5 * x * (1.0 + jnp.tanh(c * (x + 0.044715 * (x * x * x))))


def _pack_halves(v):
    c = v.shape[1] // 2
    lo = pltpu.bitcast(v[:, :c].astype(BF16).astype(F32), jnp.uint32)
    hi = pltpu.bitcast(v[:, c:].astype(BF16).astype(F32), jnp.uint32)
    return pltpu.bitcast((hi & jnp.uint32(0xFFFF0000)) | (lo >> 16), I32)


def _unpack_halves(w):
    u = pltpu.bitcast(w, jnp.uint32)
    return pltpu.bitcast(u << 16, F32), pltpu.bitcast(u & jnp.uint32(0xFFFF0000), F32)


def _load_cast(w_hbm, w_scr, stage, sems):
    chunks = w_scr.shape[1] // STAGE_COLS

    def chunk_copy(c):
        return pltpu.make_async_copy(w_hbm.at[:, pl.ds(c * STAGE_COLS, STAGE_COLS)], stage.at[c % 2],
                                     sems.at[c % 2])

    chunk_copy(0).start()
    for c in range(chunks):
        if c + 1 < chunks:
            chunk_copy(c + 1).start()
        chunk_copy(c).wait()
        w_scr[:, c * STAGE_COLS:(c + 1) * STAGE_COLS] = stage[c % 2].astype(BF16)


def _stage_scratch(rows):
    return [pltpu.VMEM((2, rows, STAGE_COLS), F32), pltpu.SemaphoreType.DMA((2,))]


_HBM = pl.BlockSpec(memory_space=pl.ANY)


def _const_spec(shape):
    nd = len(shape)
    return pl.BlockSpec(shape, lambda i: (0,) * nd)


def _pick_spec(stacked, index):
    rest = stacked.shape[1:]
    return pl.BlockSpec((None,) + rest, lambda i: (index,) + (0,) * len(rest))


def _params():
    return pltpu.CompilerParams(dimension_semantics=("arbitrary",), vmem_limit_bytes=VMEM_LIMIT)


def _moe_output(x_ref, y0_ref, y1_ref, gate_ref, g_ref, b_ref):
    gates = gate_ref[...]
    g0, g1 = gates[:, 0:1], gates[:, 1:2]
    y0_lo, y0_hi = _unpack_halves(y0_ref[...])
    y1_lo, y1_hi = _unpack_halves(y1_ref[...])
    ff = jnp.concatenate([g0 * y0_lo + g1 * y1_lo, g0 * y0_hi + g1 * y1_hi], axis=1)
    return _layer_norm(ALPHA * x_ref[...] + ff, g_ref[...], b_ref[...])


def _mixer_input(src, pending):
    if not pending:
        x_ref, xh_ref = src
        return x_ref[...], xh_ref[...]
    x_ref, xh_ref, y0_ref, y0h_ref, y1_ref, y1h_ref, gate_ref, gateh_ref, g_ref, b_ref = src
    return (_moe_output(x_ref, y0_ref, y1_ref, gate_ref, g_ref, b_ref),
            _moe_output(xh_ref, y0h_ref, y1h_ref, gateh_ref, g_ref, b_ref))


def _mixer_sources(x, pending, halo):
    n, d = x.shape
    tiles = n // TM
    halo_blocks = TM // halo

    def halo_index(i):
        return jnp.maximum(i * halo_blocks - 1, 0)

    specs = [pl.BlockSpec((TM, d), lambda i: (i, 0)), pl.BlockSpec((halo, d), lambda i: (halo_index(i), 0))]
    args = [x, x]
    if pending is not None:
        yg, gates, ln_g, ln_b, ln_row = pending
        specs += [pl.BlockSpec((TM, d // 2), lambda i: (i, 0)),
                  pl.BlockSpec((halo, d // 2), lambda i: (halo_index(i), 0)),
                  pl.BlockSpec((TM, d // 2), lambda i: (i + tiles, 0)),
                  pl.BlockSpec((halo, d // 2), lambda i: (halo_index(i) + tiles * halo_blocks, 0)),
                  pl.BlockSpec((TM, TOP_K), lambda i: (i, 0)),
                  pl.BlockSpec((halo, TOP_K), lambda i: (halo_index(i), 0)),
                  _pick_spec(ln_g, ln_row), _pick_spec(ln_b, ln_row)]
        args += [yg, yg, yg, yg, gates, gates, ln_g, ln_b]
    return specs, args


def _even_kernel(*refs, tiles_per_seq, layer, pending):
    n_src = 10 if pending else 2
    (win_hbm, wpool_ref, pscale_ref, lvg_ref, lvb_ref, ws_ref, bst_ref, wout_hbm, g_ref, b_ref, o_ref,
     a_scr, cat_scr, win_ref, wout_ref, stage, sems) = refs[n_src:]
    tm = o_ref.shape[0]
    d_pool = a_scr.shape[1]
    d_sg = lvg_ref.shape[1]
    pgd = d_pool // len(POOL_WINDOWS)
    hd_dim = d_sg // N_SG_HEADS
    seq_tile = pl.program_id(0) % tiles_per_seq

    @pl.when(pl.program_id(0) == 0)
    def _():
        _load_cast(win_hbm.at[layer], win_ref, stage, sems)
        _load_cast(wout_hbm.at[layer], wout_ref, stage, sems)

    x, xh = _mixer_input(refs[:n_src], pending)
    sub = tm // SUB_TILES

    ah = _dot(xh.astype(BF16), win_ref[:, :d_pool])
    a_scr[0:POOL_HALO, :] = jnp.where(seq_tile == 0, 0.0, ah)
    row = lax.broadcasted_iota(I32, (CHUNK, CHUNK), 0)
    col = lax.broadcasted_iota(I32, (CHUNK, CHUNK), 1)
    ws_masked = [jnp.where(row >= col, ws_ref[hd], 0.0).astype(BF16) for hd in range(N_SG_HEADS)]

    def in_proj(st):
        h = _dot(x[st * sub:(st + 1) * sub, :].astype(BF16), win_ref[...])
        a_scr[POOL_HALO + st * sub:POOL_HALO + (st + 1) * sub, :] = h[:, :d_pool]
        return h[:, d_pool:]

    def branches_out_proj(st, hz):
        base = st * sub
        pos = seq_tile * tm + base + lax.broadcasted_iota(I32, (sub, 1), 0)
        for g, w in enumerate(POOL_WINDOWS):
            cs = slice(g * pgd, (g + 1) * pgd)
            tok = a_scr[POOL_HALO + base:POOL_HALO + base + sub, cs]
            acc = a_scr[base:POOL_HALO + base + sub, cs]
            span = 1
            while span < w:
                acc = acc[span:, :] + acc[:-span, :]
                span *= 2
            acc = acc[acc.shape[0] - sub:, :]
            cnt = jnp.minimum(pos + 1, w).astype(F32)
            dev = acc * (1.0 / cnt) - tok
            yg = _dot(dev.astype(BF16), wpool_ref[g])
            cat_scr[base:base + sub, cs] = (yg * pscale_ref[:, cs]).astype(BF16)

        z = _gelu_tanh(hz)
        u = z[:, :d_sg]
        v = _layer_norm(z[:, d_sg:], lvg_ref[...], lvb_ref[...]).astype(BF16)
        for hd in range(N_SG_HEADS):
            hs = slice(hd * hd_dim, (hd + 1) * hd_dim)
            bcol = bst_ref[:, hd:hd + 1]
            for ck in range(sub // CHUNK):
                rs = slice(ck * CHUNK, (ck + 1) * CHUNK)
                sv = _dot(ws_masked[hd], v[rs, hs]) + bcol
                cat_scr[base + ck * CHUNK:base + (ck + 1) * CHUNK,
                        d_pool + hd * hd_dim:d_pool + (hd + 1) * hd_dim] = (u[rs, hs] * sv).astype(BF16)
        return _dot(cat_scr[base:base + sub, :], wout_ref[...])

    hzs = [in_proj(st) for st in range(SUB_TILES)]
    mixes = [branches_out_proj(st, hzs[st]) for st in range(SUB_TILES)]
    for st in range(SUB_TILES):
        rs = slice(st * sub, (st + 1) * sub)
        o_ref[rs, :] = _layer_norm(ALPHA * x[rs, :] + mixes[st], g_ref[...], b_ref[...])


def _even_mixer(x, pending, seq, layer, w_in, w_pool, pool_scale, ln_v_g, ln_v_b, w_spatial, b_spatial_t,
                w_out, ln_g, ln_b, ln_row):
    n, d = x.shape
    d_in = w_in.shape[2]
    d_pool = pool_scale.shape[2]
    d_sg = ln_v_g.shape[2]
    kern = functools.partial(_even_kernel, tiles_per_seq=seq // TM, layer=layer, pending=pending is not None)
    src_specs, src_args = _mixer_sources(x, pending, POOL_HALO)
    return pl.pallas_call(
        kern,
        out_shape=jax.ShapeDtypeStruct((n, d), F32),
        grid=(n // TM,),
        in_specs=src_specs + [
            _HBM,
            _pick_spec(w_pool, layer),
            _pick_spec(pool_scale, layer),
            _pick_spec(ln_v_g, layer),
            _pick_spec(ln_v_b, layer),
            _pick_spec(w_spatial, layer),
            _pick_spec(b_spatial_t, layer),
            _HBM,
            _pick_spec(ln_g, ln_row),
            _pick_spec(ln_b, ln_row),
        ],
        out_specs=pl.BlockSpec((TM, d), lambda i: (i, 0)),
        scratch_shapes=[pltpu.VMEM((POOL_HALO + TM, d_pool), F32), pltpu.VMEM((TM, d_pool + d_sg), BF16),
                        pltpu.VMEM((d, d_in), BF16), pltpu.VMEM((d_pool + d_sg, d), BF16)] + _stage_scratch(d),
        compiler_params=_params(),
        name="even_mixer",
    )(*src_args, w_in, w_pool, pool_scale, ln_v_g, ln_v_b, w_spatial, b_spatial_t, w_out, ln_g, ln_b)


def _odd_kernel(*refs, tiles_per_seq, layer, pending):
    n_src = 10 if pending else 2
    (win_hbm, cwt_ref, cb_ref, wout_hbm, g_ref, b_ref, o_ref, zc_scr,
     win_ref, wout_ref, stage, sems) = refs[n_src:]
    tm, d = o_ref.shape
    seq_tile = pl.program_id(0) % tiles_per_seq

    @pl.when(pl.program_id(0) == 0)
    def _():
        _load_cast(win_hbm.at[layer], win_ref, stage, sems)
        _load_cast(wout_hbm.at[layer], wout_ref, stage, sems)

    x, xh = _mixer_input(refs[:n_src], pending)
    hh = _dot(xh.astype(BF16), win_ref[:, d:])
    zc_scr[0:CONV_HALO, :] = jnp.where(seq_tile == 0, 0.0, hh[:, :d] * hh[:, d:])
    sub = tm // SUB_TILES

    def in_proj(st):
        xb = x[st * sub:(st + 1) * sub, :].astype(BF16)
        hc = _dot(xb, win_ref[:, d:2 * d])
        hz = _dot(xb, win_ref[:, 2 * d:])
        zc_scr[CONV_HALO + st * sub:CONV_HALO + (st + 1) * sub, :] = hc * hz
        return _dot(xb, win_ref[:, :d])

    def conv_out_proj(st, gate):
        conv = cb_ref[...]
        for j in range(CONV_WIDTH):
            off = CONV_HALO + st * sub - (CONV_WIDTH - 1) + j
            conv = conv + zc_scr[off:off + sub, :] * cwt_ref[j:j + 1, :]
        return _dot((gate * conv).astype(BF16), wout_ref[...])

    gates = [in_proj(st) for st in range(SUB_TILES)]
    ys = [conv_out_proj(st, gates[st]) for st in range(SUB_TILES)]
    for st in range(SUB_TILES):
        rs = slice(st * sub, (st + 1) * sub)
        o_ref[rs, :] = _layer_norm(ALPHA * x[rs, :] + ys[st], g_ref[...], b_ref[...])


def _odd_mixer(x, pending, seq, layer, w_in, conv_w_t, conv_b, w_out, ln_g, ln_b, ln_row):
    n, d = x.shape
    kern = functools.partial(_odd_kernel, tiles_per_seq=seq // TM, layer=layer, pending=pending is not None)
    src_specs, src_args = _mixer_sources(x, pending, CONV_HALO)
    return pl.pallas_call(
        kern,
        out_shape=jax.ShapeDtypeStruct((n, d), F32),
        grid=(n // TM,),
        in_specs=src_specs + [
            _HBM,
            _pick_spec(conv_w_t, layer),
            _pick_spec(conv_b, layer),
            _HBM,
            _pick_spec(ln_g, ln_row),
            _pick_spec(ln_b, ln_row),
        ],
        out_specs=pl.BlockSpec((TM, d), lambda i: (i, 0)),
        scratch_shapes=[pltpu.VMEM((CONV_HALO + TM, d), F32),
                        pltpu.VMEM(w_in.shape[1:], BF16), pltpu.VMEM(w_out.shape[1:], BF16)] + _stage_scratch(d),
        compiler_params=_params(),
        name="odd_mixer",
    )(*src_args, w_in, conv_w_t, conv_b, w_out, ln_g, ln_b)


def _kv_kernel(mem_ref, wk_ref, wv_ref, k_ref, v_ref):
    m = mem_ref[...].astype(BF16)
    k_ref[...] = _dot(m, wk_ref[...].astype(BF16)).astype(BF16)
    v_ref[...] = _dot(m, wv_ref[...].astype(BF16)).astype(BF16)


def _memory_kv(mem2d, wk, wv, layer):
    rows, d = mem2d.shape
    out = jax.ShapeDtypeStruct((rows, d), BF16)
    return pl.pallas_call(
        _kv_kernel,
        out_shape=(out, out),
        grid=(1,),
        in_specs=[_const_spec((rows, d)), _pick_spec(wk, layer), _pick_spec(wv, layer)],
        out_specs=(_const_spec((rows, d)), _const_spec((rows, d))),
        compiler_params=_params(),
        name="memory_kv",
    )(mem2d, wk, wv)


def _attn_kernel(x_ref, k_ref, v_ref, wq_hbm, wo_hbm, g_ref, b_ref, wr_ref, br_ref,
                 o_ref, op_ref, rt_ref, rg_ref, cnt_ref, o_scr, carry_scr, wq_ref, wo_ref, stage, sems,
                 *, layer):
    tm, d = x_ref.shape
    hd_dim = d // N_XHEADS
    sub = tm // SUB_TILES

    @pl.when(pl.program_id(0) == 0)
    def _():
        carry_scr[...] = jnp.zeros_like(carry_scr)
        _load_cast(wq_hbm.at[layer], wq_ref, stage, sems)
        _load_cast(wo_hbm.at[layer], wo_ref, stage, sems)

    row_slices = [slice(st * sub, (st + 1) * sub) for st in range(SUB_TILES)]
    qs = [_dot(x_ref[rs, :].astype(BF16), wq_ref[...]) * (1.0 / math.sqrt(hd_dim)) for rs in row_slices]
    for rs, q in zip(row_slices, qs):
        for hd in range(N_XHEADS):
            hs = slice(hd * hd_dim, (hd + 1) * hd_dim)
            s = lax.dot_general(q[:, hs].astype(BF16), k_ref[:, hs], _NT, preferred_element_type=F32)
            p = jnp.exp(s - jnp.max(s, axis=-1, keepdims=True))
            p = p * (1.0 / jnp.sum(p, axis=-1, keepdims=True))
            o_scr[rs, hs] = _dot(p.astype(BF16), v_ref[:, hs]).astype(BF16)
    xas = [_dot(o_scr[rs, :], wo_ref[...]) for rs in row_slices]
    for rs, xa in zip(row_slices, xas):
        out = _layer_norm(ALPHA * x_ref[rs, :] + xa, g_ref[...], b_ref[...])
        o_ref[rs, :] = out
        op_ref[rs, :] = _pack_halves(out)
        table, gates = _route_rows(out, wr_ref, br_ref, carry_scr)
        rt_ref[:, rs] = table
        rg_ref[rs, :] = gates
    cnt_ref[...] = carry_scr[...].astype(I32)


def _cross_attn(x, seq, layer, k, v, wq, wo, ln_g, ln_b, ln_row, router_w, router_b):
    n, d = x.shape
    m = k.shape[1]
    tiles_per_seq = seq // TM
    kvspec = pl.BlockSpec((None, m, d), lambda i: (i // tiles_per_seq, 0, 0))
    out, packed, table, gates, cnt = pl.pallas_call(
        functools.partial(_attn_kernel, layer=layer),
        out_shape=(jax.ShapeDtypeStruct((n, d), F32), jax.ShapeDtypeStruct((n, d // 2), I32),
                   jax.ShapeDtypeStruct((ROUTE_ROWS, n), F32), jax.ShapeDtypeStruct((n, TOP_K), F32),
                   jax.ShapeDtypeStruct((N_EXPERTS, LANES), I32)),
        grid=(n // TM,),
        in_specs=[
            pl.BlockSpec((TM, d), lambda i: (i, 0)),
            kvspec, kvspec,
            _HBM, _HBM,
            _pick_spec(ln_g, ln_row), _pick_spec(ln_b, ln_row),
            _pick_spec(router_w, layer), _pick_spec(router_b, layer),
        ],
        out_specs=(pl.BlockSpec((TM, d), lambda i: (i, 0)), pl.BlockSpec((TM, d // 2), lambda i: (i, 0)),
                   pl.BlockSpec((ROUTE_ROWS, TM), lambda i: (0, i)), pl.BlockSpec((TM, TOP_K), lambda i: (i, 0)),
                   _const_spec((N_EXPERTS, LANES))),
        scratch_shapes=[pltpu.VMEM((TM, d), BF16), pltpu.VMEM((N_EXPERTS, LANES), F32),
                        pltpu.VMEM((d, d), BF16), pltpu.VMEM((d, d), BF16)] + _stage_scratch(d),
        compiler_params=_params(),
        name="cross_attn",
    )(x, k, v, wq, wo, ln_g, ln_b, router_w, router_b)
    return out, packed, table, gates, cnt[:, 0]


def _route_rows(x, w_ref, bias_ref, carry_scr):
    tm = x.shape[0]
    neg = -jnp.inf
    logits = _dot(x.astype(BF16), w_ref[...]) + bias_ref[...]
    lt = jnp.transpose(logits)

    def first_argmax(vals):
        rows = lax.broadcasted_iota(I32, vals.shape, 0).astype(F32)
        mx = jnp.max(vals, axis=0, keepdims=True)
        idx = jnp.min(jnp.where(vals == mx, rows, float(vals.shape[0])), axis=0, keepdims=True)
        return mx, idx, rows

    gl = lt[0:N_GROUPS, :]
    gmax, g_sel, _ = first_argmax(gl)
    gate_g = 1.0 / jnp.sum(jnp.exp(gl - gmax), axis=0, keepdims=True)

    el = lt[EXPERT_ROW0:EXPERT_ROW0 + EXPERTS_PER_GROUP, :]
    for g in range(1, N_GROUPS):
        lo = EXPERT_ROW0 + g * EXPERTS_PER_GROUP
        el = jnp.where(g_sel == float(g), lt[lo:lo + EXPERTS_PER_GROUP, :], el)
    m1, i1, erow = first_argmax(el)
    m2, i2, _ = first_argmax(jnp.where(erow == i1, neg, el))
    e21 = jnp.exp(m2 - m1)
    w1 = 1.0 / (1.0 + e21)
    w2 = e21 / (1.0 + e21)
    e1 = g_sel * EXPERTS_PER_GROUP + i1
    e2 = g_sel * EXPERTS_PER_GROUP + i2

    xrow = lax.broadcasted_iota(I32, (N_EXPERTS, tm), 0).astype(F32)
    oh1 = xrow == e1
    oh2 = xrow == e2
    oh = (oh1 | oh2).astype(BF16)
    r = lax.broadcasted_iota(I32, (tm, tm), 0)
    c = lax.broadcasted_iota(I32, (tm, tm), 1)
    before = _dot(oh, (r < c).astype(BF16)) + carry_scr[:, 0:1]
    rank1 = jnp.sum(jnp.where(oh1, before, 0.0), axis=0, keepdims=True)
    rank2 = jnp.sum(jnp.where(oh2, before, 0.0), axis=0, keepdims=True)
    carry_scr[...] += jnp.sum(oh.astype(F32), axis=1, keepdims=True)

    trow = lax.broadcasted_iota(I32, (ROUTE_ROWS, tm), 0)
    table = jnp.where(trow == 0, e1, jnp.where(trow == 1, e2, jnp.where(trow == 2, rank1,
                                                                      jnp.where(trow == 3, rank2, 0.0))))
    grow = lax.broadcasted_iota(I32, (LANES, tm), 0)
    gates = jnp.transpose(jnp.where(grow == 0, gate_g * w1, jnp.where(grow == 1, gate_g * w2, 0.0)))[:, :TOP_K]
    return table, gates


def _router_weights(wr_g, br_g, wr_e, br_e):
    nl, d, _ = wr_g.shape
    gap = EXPERT_ROW0 - N_GROUPS
    w = jnp.concatenate([wr_g, jnp.zeros((nl, d, gap), F32),
                         jnp.transpose(wr_e, (0, 2, 1, 3)).reshape(nl, d, N_EXPERTS)], axis=2)
    w = jnp.pad(w, ((0, 0), (0, 0), (0, ROUTE_COLS - w.shape[2]))).astype(BF16)
    bias = jnp.concatenate([br_g, jnp.zeros((nl, gap), F32), br_e.reshape(nl, -1)], axis=1)
    bias = jnp.pad(bias, ((0, 0), (0, ROUTE_COLS - bias.shape[1])))
    return w, bias[:, None, :]


def _sc_worker_rows(rows):
    per_worker = rows // SC_WORKERS
    n_chunks = per_worker // SC_CHUNK
    assert per_worker * SC_WORKERS == rows and n_chunks * SC_CHUNK == per_worker and n_chunks % 2 == 0
    return per_worker, n_chunks


def _sc_gather_rows(table_hbm, out_hbm, idx_v, rows_v, gsem, wsem, base, n_chunks):
    def fetch(c, slot):
        off = pl.multiple_of(c * SC_CHUNK, SC_CHUNK)
        return pltpu.make_async_copy(table_hbm.at[idx_v.at[pl.ds(off, SC_CHUNK)]], rows_v.at[slot],
                                     gsem.at[slot])

    def put(c, slot):
        off = pl.multiple_of(c * SC_CHUNK, SC_CHUNK)
        return pltpu.make_async_copy(rows_v.at[slot], out_hbm.at[pl.ds(base + off, SC_CHUNK)], wsem.at[slot])

    fetch(0, 0).start()

    @pl.loop(0, n_chunks, step=2)
    def _(c0):
        for slot in range(2):
            c = c0 + slot

            @pl.when(c + 1 < n_chunks)
            def _():
                @pl.when(c >= 1)
                def _():
                    put(c - 1, 1 - slot).wait()
                fetch(c + 1, 1 - slot).start()

            fetch(c, slot).wait()
            put(c, slot).start()

    put(n_chunks - 2, 0).wait()
    put(n_chunks - 1, 1).wait()


def _sc_row_scratch(per_worker, d, dtype):
    return [pltpu.VMEM((per_worker,), I32), pltpu.VMEM((2, SC_CHUNK, d), dtype),
            pltpu.SemaphoreType.DMA((2,)), pltpu.SemaphoreType.DMA((2,))]


def _sc_gather(table, idx):
    b = idx.shape[0]
    d = table.shape[1]
    per_worker, n_chunks = _sc_worker_rows(b)
    mesh = plsc.VectorSubcoreMesh(core_axis_name="c", subcore_axis_name="s")

    @functools.partial(
        pl.kernel, mesh=mesh,
        out_type=jax.ShapeDtypeStruct((b, d), table.dtype),
        scratch_types=_sc_row_scratch(per_worker, d, table.dtype),
        name="sc_gather",
    )
    def gather(table_hbm, idx_hbm, out_hbm, idx_v, rows_v, gsem, wsem):
        base = (lax.axis_index("s") * SC_CORES + lax.axis_index("c")) * per_worker
        pltpu.sync_copy(idx_hbm.at[pl.ds(base, per_worker)], idx_v)
        _sc_gather_rows(table_hbm, out_hbm, idx_v, rows_v, gsem, wsem, base, n_chunks)

    return gather(table, idx)


def _sc_dispatch(table, dest_flat, rows):
    n, d = table.shape
    a = dest_flat.shape[0]
    lanes = SC_LANES
    per_worker, n_chunks = _sc_worker_rows(rows)
    assert a % lanes == 0 and per_worker % lanes == 0
    mesh = plsc.VectorSubcoreMesh(core_axis_name="c", subcore_axis_name="s")

    @functools.partial(
        pl.kernel, mesh=mesh,
        out_type=jax.ShapeDtypeStruct((rows, d), table.dtype),
        scratch_types=[pltpu.VMEM((a,), I32)] + _sc_row_scratch(per_worker, d, table.dtype),
        compiler_params=pltpu.CompilerParams(needs_layout_passes=False),
        name="sc_dispatch",
    )
    def dispatch(table_hbm, dest_hbm, out_hbm, dest_v, idx_v, rows_v, gsem, wsem):
        base = (lax.axis_index("s") * SC_CORES + lax.axis_index("c")) * per_worker
        pltpu.sync_copy(dest_hbm, dest_v)
        lane = lax.iota(I32, lanes)

        @pl.loop(0, per_worker // lanes)
        def _(i):
            idx_v[pl.ds(i * lanes, lanes)] = lax.rem(base + i * lanes + lane, n)

        @plsc.parallel_loop(0, a // lanes, unroll=8)
        def _(i):
            local = dest_v[pl.ds(i * lanes, lanes)] - base
            mine = (local >= 0) & (local < per_worker)
            plsc.store_scatter(idx_v, [jnp.where(mine, local, 0)], lax.rem(i * lanes + lane, n), mask=mine)

        _sc_gather_rows(table_hbm, out_hbm, idx_v, rows_v, gsem, wsem, base, n_chunks)

    return dispatch(table, dest_flat)


def _expert_kernel(sched_ref, ni_ref, xs_hbm, w1_hbm, w3_hbm, w2_hbm, y_hbm,
                   x_buf, y_buf, w1_buf, w3_buf, w2_buf, w1_scr, w3_scr, w2_scr, wsems, xsems, ysems,
                   *, layer):
    i = pl.program_id(0)
    n_items = ni_ref[0]
    expert, wslot, run_start, next_expert = (sched_ref[r, i] for r in range(4))
    slot = i % 2

    def for_units(item, fn):
        for units in range(1, ITEM_UNITS + 1):
            @pl.when(sched_ref[5, item] == units)
            def _():
                fn(units * EXPERT_UNIT)

    def x_copy(item, s, rows):
        row0 = pl.multiple_of(sched_ref[4, item], EXPERT_UNIT)
        return pltpu.make_async_copy(xs_hbm.at[pl.ds(row0, rows)], x_buf.at[s, pl.ds(0, rows)], xsems.at[s])

    def y_copy(item, s, rows):
        row0 = pl.multiple_of(sched_ref[4, item], EXPERT_UNIT)
        return pltpu.make_async_copy(y_buf.at[s, pl.ds(0, rows)], y_hbm.at[pl.ds(row0, rows)], ysems.at[s])

    def fetch(e, s):
        return [pltpu.make_async_copy(w_hbm.at[layer, e], buf.at[s], wsems.at[s, j])
                for j, (w_hbm, buf) in enumerate(((w1_hbm, w1_buf), (w3_hbm, w3_buf), (w2_hbm, w2_buf)))]

    def mlp(rows):
        x_lo, x_hi = _unpack_halves(x_buf[slot, 0:rows, :])
        xb = jnp.concatenate([x_lo.astype(BF16), x_hi.astype(BF16)], axis=1)
        h1 = _dot(xb, w1_scr[...])
        h3 = _dot(xb, w3_scr[...])
        hid = h1 * (1.0 / (1.0 + jnp.exp(-h1))) * h3
        y_buf[slot, 0:rows, :] = _pack_halves(_dot(hid.astype(BF16), w2_scr[...]))
        y_copy(i, slot, rows).start()

    @pl.when(i < n_items)
    def _():
        @pl.when(i == 0)
        def _():
            for_units(0, lambda rows: x_copy(0, 0, rows).start())

        for_units(i, lambda rows: x_copy(i, slot, rows).wait())

        @pl.when(i + 1 < n_items)
        def _():
            for_units(i + 1, lambda rows: x_copy(i + 1, 1 - slot, rows).start())

        @pl.when(run_start == 1)
        def _():
            @pl.when(i == 0)
            def _():
                for c in fetch(expert, wslot):
                    c.start()

            for c in fetch(expert, wslot):
                c.wait()

            @pl.when(next_expert >= 0)
            def _():
                for c in fetch(next_expert, 1 - wslot):
                    c.start()

            w1_scr[...] = w1_buf[wslot].astype(BF16)
            w3_scr[...] = w3_buf[wslot].astype(BF16)
            w2_scr[...] = w2_buf[wslot].astype(BF16)

        @pl.when(i >= 2)
        def _():
            for_units(i - 2, lambda rows: y_copy(i - 2, slot, rows).wait())

        for_units(i, mlp)

        @pl.when(i == n_items - 1)
        def _():
            @pl.when(i >= 1)
            def _():
                for_units(i - 1, lambda rows: y_copy(i - 1, 1 - slot, rows).wait())

            for_units(i, lambda rows: y_copy(i, slot, rows).wait())
            y_buf[0, 0:EXPERT_UNIT, :] = jnp.zeros((EXPERT_UNIT, y_buf.shape[2]), y_buf.dtype)
            first_free = (sched_ref[4, i] + sched_ref[5, i] * EXPERT_UNIT) // EXPERT_UNIT

            def zero_copy(u):
                return pltpu.make_async_copy(
                    y_buf.at[0, pl.ds(0, EXPERT_UNIT)],
                    y_hbm.at[pl.ds(pl.multiple_of(u * EXPERT_UNIT, EXPERT_UNIT), EXPERT_UNIT)], ysems.at[0])

            def start_zero(u, _):
                zero_copy(u).start()
                return 0

            def wait_zero(u, _):
                zero_copy(u).wait()
                return 0

            lax.fori_loop(first_free, y_hbm.shape[0] // EXPERT_UNIT, start_zero, 0)
            lax.fori_loop(first_free, y_hbm.shape[0] // EXPERT_UNIT, wait_zero, 0)


def _expert_mlp(xs, schedule, n_items, layer, w1, w3, w2):
    d, de = w1.shape[2], w1.shape[3]
    max_rows = ITEM_UNITS * EXPERT_UNIT
    return pl.pallas_call(
        functools.partial(_expert_kernel, layer=layer),
        out_shape=jax.ShapeDtypeStruct(xs.shape, I32),
        grid_spec=pltpu.PrefetchScalarGridSpec(
            num_scalar_prefetch=2,
            grid=(schedule.shape[1],),
            in_specs=[_HBM, _HBM, _HBM, _HBM],
            out_specs=_HBM,
            scratch_shapes=[pltpu.VMEM((2, max_rows, d // 2), I32), pltpu.VMEM((2, max_rows, d // 2), I32),
                            pltpu.VMEM((2, d, de), F32), pltpu.VMEM((2, d, de), F32), pltpu.VMEM((2, de, d), F32),
                            pltpu.VMEM((d, de), BF16), pltpu.VMEM((d, de), BF16), pltpu.VMEM((de, d), BF16),
                            pltpu.SemaphoreType.DMA((2, 3)), pltpu.SemaphoreType.DMA((2,)),
                            pltpu.SemaphoreType.DMA((2,))],
        ),
        compiler_params=_params(),
        name="expert_mlp",
    )(schedule, n_items, xs, w1, w3, w2)


def _combine_kernel(x_ref, y0_ref, y1_ref, gate_ref, g_ref, b_ref, o_ref):
    o_ref[...] = _moe_output(x_ref, y0_ref, y1_ref, gate_ref, g_ref, b_ref)


def _combine(x, yg, gates, ln_g, ln_b, ln_row):
    n, d = x.shape
    tiles = n // TM
    return pl.pallas_call(
        _combine_kernel,
        out_shape=jax.ShapeDtypeStruct((n, d), F32),
        grid=(tiles,),
        in_specs=[pl.BlockSpec((TM, d), lambda i: (i, 0)),
                  pl.BlockSpec((TM, d // 2), lambda i: (i, 0)),
                  pl.BlockSpec((TM, d // 2), lambda i: (i + tiles, 0)),
                  pl.BlockSpec((TM, TOP_K), lambda i: (i, 0)),
                  _pick_spec(ln_g, ln_row), _pick_spec(ln_b, ln_row)],
        out_specs=pl.BlockSpec((TM, d), lambda i: (i, 0)),
        compiler_params=_params(),
        name="combine",
    )(x, yg, yg, gates, ln_g, ln_b)


def _moe_experts(x_packed, table, counts, layer, w1, w3, w2):
    n = x_packed.shape[0]
    max_units = (n * TOP_K + N_EXPERTS * (EXPERT_UNIT - 1) + EXPERT_UNIT - 1) // EXPERT_UNIT
    max_items = (max_units + N_EXPERTS * (ITEM_UNITS - 1) + ITEM_UNITS - 1) // ITEM_UNITS
    experts = table[:TOP_K].astype(I32)
    ranks = table[TOP_K:2 * TOP_K].astype(I32)

    units_e = (counts + EXPERT_UNIT - 1) // EXPERT_UNIT
    units_start = jnp.cumsum(units_e) - units_e
    items_e = (units_e + ITEM_UNITS - 1) // ITEM_UNITS
    items_end = jnp.cumsum(items_e)
    n_items = items_end[-1:].astype(I32)
    item_ids = jnp.arange(max_items, dtype=I32)
    expert_ids = jnp.arange(N_EXPERTS, dtype=I32)
    item_expert = jnp.minimum(jnp.sum(items_end[None, :] <= item_ids[:, None], axis=1), N_EXPERTS - 1)
    later = (expert_ids[None, :] > expert_ids[:, None]) & (items_e[None, :] > 0)
    next_run = jnp.min(jnp.where(later, expert_ids[None, :], N_EXPERTS), axis=1)
    next_run = jnp.where(next_run == N_EXPERTS, -1, next_run)
    per_expert = jnp.stack([items_end - items_e, units_start, units_e, next_run], axis=1)
    mine = (item_expert[:, None] == expert_ids[None, :])[:, :, None]
    first_item, unit0, units, next_expert = jnp.sum(jnp.where(mine, per_expert[None], 0), axis=1).T
    within = item_ids - first_item
    item_row0 = (unit0 + ITEM_UNITS * within) * EXPERT_UNIT
    item_units = jnp.clip(units - ITEM_UNITS * within, 1, ITEM_UNITS)
    run_start_flag = ((within == 0) & (item_ids < n_items[0])).astype(I32)
    slot = (jnp.cumsum(run_start_flag) - 1) % 2
    schedule = jnp.stack([item_expert, slot, run_start_flag, next_expert, item_row0, item_units]).astype(I32)
    start_of = jnp.sum(jnp.where(experts[:, :, None] == expert_ids, units_start * EXPERT_UNIT, 0), axis=-1)
    dest = (start_of + ranks).astype(I32).reshape(-1)
    xs = _sc_dispatch(x_packed, dest, max_units * EXPERT_UNIT)
    y = _expert_mlp(xs, schedule, n_items, layer, w1, w3, w2)
    return _sc_gather(y, dest)


def kernel(x, mem, w_in_even, w_pool, pool_scale, ln_v_g, ln_v_b, w_spatial, b_spatial, w_out_even,
           w_in_odd, conv_w, conv_b, w_out_odd, wq_x, wk_x, wv_x, wo_x, ln_g, ln_b, wr_group,
           br_group, wr_expert, br_expert, w1, w3, w2):
    bsz, seq, d = x.shape
    assert seq % TM == 0 and d % LANES == 0
    mlen = mem.shape[1]
    mem2d = mem.reshape(bsz * mlen, d)
    ln_g = ln_g.reshape(DEPTH * 3, 1, d)
    ln_b = ln_b.reshape(DEPTH * 3, 1, d)
    w_pool = w_pool.astype(BF16)
    pool_scale, ln_v_g, ln_v_b, conv_b = (p[:, None, :] for p in (pool_scale, ln_v_g, ln_v_b, conv_b))
    b_spatial_t = jnp.swapaxes(b_spatial, 1, 2)
    conv_w_t = jnp.swapaxes(conv_w, 1, 2)
    router_w, router_b = _router_weights(wr_group, br_group, wr_expert, br_expert)

    h = x.reshape(bsz * seq, d)
    pending = None
    kv = _memory_kv(mem2d, wk_x, wv_x, 0)
    for l in range(DEPTH):
        i = l // 2
        k, v = (a.reshape(bsz, mlen, d) for a in kv)
        if l % 2 == 0:
            h = _even_mixer(h, pending, seq, i, w_in_even, w_pool, pool_scale, ln_v_g, ln_v_b,
                            w_spatial, b_spatial_t, w_out_even, ln_g, ln_b, 3 * l)
        else:
            h = _odd_mixer(h, pending, seq, i, w_in_odd, conv_w_t, conv_b, w_out_odd, ln_g, ln_b, 3 * l)
        h, hp, table, gates, counts = _cross_attn(h, seq, l, k, v, wq_x, wo_x,
                                                  ln_g, ln_b, 3 * l + 1, router_w, router_b)
        if l + 1 < DEPTH:
            kv = _memory_kv(mem2d, wk_x, wv_x, l + 1)
        yg = _moe_experts(hp, table, counts, l, w1, w3, w2)
        pending = (yg, gates, ln_g, ln_b, 3 * l + 2)
    return _combine(h, *pending).reshape(bsz, seq, d)
```

```python
import functools
import math

import jax
import jax.numpy as jnp
from jax import lax
from jax.experimental import pallas as pl
from jax.experimental.pallas import tpu as pltpu
from jax.experimental.pallas import tpu_sc as plsc

F32 = jnp.float32
BF16 = jnp.bfloat16
I32 = jnp.int32

POOL_WINDOWS = (2, 4, 8, 16)
assert all(w & (w - 1) == 0 for w in POOL_WINDOWS)
N_SG_HEADS = 4
CHUNK = 128
CONV_WIDTH = 3
N_XHEADS = 4
N_GROUPS = 4
EXPERTS_PER_GROUP = 8
N_EXPERTS = N_GROUPS * EXPERTS_PER_GROUP
TOP_K = 2
DEPTH = 4
ALPHA = (2.0 * DEPTH) ** 0.25
LN_EPS = 1e-5

LANES = 128
SC_CORES = 2
SC_WORKERS = 32
SC_LANES = 16
VMEM_LIMIT = 56 * 1024 * 1024

SC_CHUNK = 64
TM = 1024
SUB_TILES = 2
ROUTE_ROWS = 8
POOL_HALO = 16
CONV_HALO = 8
EXPERT_UNIT = 128
ITEM_UNITS = 4
ROUTE_COLS = 128
EXPERT_ROW0 = 8
STAGE_COLS = 512

_NT = (((1,), (1,)), ((), ()))


def _dot(a, b):
    return jnp.dot(a, b, preferred_element_type=F32)


def _layer_norm(y, g, b):
    mu = jnp.mean(y, axis=-1, keepdims=True)
    yc = y - mu
    var = jnp.mean(yc * yc, axis=-1, keepdims=True)
    return yc * lax.rsqrt(var + LN_EPS) * g + b


def _gelu_tanh(x):
    c = math.sqrt(2.0 / math.pi)
    return 0.5 * x * (1.0 + jnp.tanh(c * (x + 0.044715 * (x * x * x))))


def _pack_halves(v):
    c = v.shape[1] // 2
    lo = pltpu.bitcast(v[:, :c].astype(BF16).astype(F32), jnp.uint32)
    hi = pltpu.bitcast(v[:, c:].astype(BF16).astype(F32), jnp.uint32)
    return pltpu.bitcast((hi & jnp.uint32(0xFFFF0000)) | (lo >> 16), I32)


def _unpack_halves(w):
    u = pltpu.bitcast(w, jnp.uint32)
    return pltpu.bitcast(u << 16, F32), pltpu.bitcast(u & jnp.uint32(0xFFFF0000), F32)


def _load_cast(w_hbm, w_scr, stage, sems):
    chunks = w_scr.shape[1] // STAGE_COLS

    def chunk_copy(c):
        return pltpu.make_async_copy(w_hbm.at[:, pl.ds(c * STAGE_COLS, STAGE_COLS)], stage.at[c % 2],
                                     sems.at[c % 2])

    chunk_copy(0).start()
    for c in range(chunks):
        if c + 1 < chunks:
            chunk_copy(c + 1).start()
        chunk_copy(c).wait()
        w_scr[:, c * STAGE_COLS:(c + 1) * STAGE_COLS] = stage[c % 2].astype(BF16)


def _stage_scratch(rows):
    return [pltpu.VMEM((2, rows, STAGE_COLS), F32), pltpu.SemaphoreType.DMA((2,))]


_HBM = pl.BlockSpec(memory_space=pl.ANY)


def _const_spec(shape):
    nd = len(shape)
    return pl.BlockSpec(shape, lambda i: (0,) * nd)


def _pick_spec(stacked, index):
    rest = stacked.shape[1:]
    return pl.BlockSpec((None,) + rest, lambda i: (index,) + (0,) * len(rest))


def _params():
    return pltpu.CompilerParams(dimension_semantics=("arbitrary",), vmem_limit_bytes=VMEM_LIMIT)


def _moe_output(x_ref, y0_ref, y1_ref, gate_ref, g_ref, b_ref):
    gates = gate_ref[...]
    g0, g1 = gates[:, 0:1], gates[:, 1:2]
    y0_lo, y0_hi = _unpack_halves(y0_ref[...])
    y1_lo, y1_hi = _unpack_halves(y1_ref[...])
    ff = jnp.concatenate([g0 * y0_lo + g1 * y1_lo, g0 * y0_hi + g1 * y1_hi], axis=1)
    return _layer_norm(ALPHA * x_ref[...] + ff, g_ref[...], b_ref[...])


def _mixer_input(src, pending):
    if not pending:
        x_ref, xh_ref = src
        return x_ref[...], xh_ref[...]
    x_ref, xh_ref, y0_ref, y0h_ref, y1_ref, y1h_ref, gate_ref, gateh_ref, g_ref, b_ref = src
    return (_moe_output(x_ref, y0_ref, y1_ref, gate_ref, g_ref, b_ref),
            _moe_output(xh_ref, y0h_ref, y1h_ref, gateh_ref, g_ref, b_ref))


def _mixer_sources(x, pending, halo):
    n, d = x.shape
    tiles = n // TM
    halo_blocks = TM // halo

    def halo_index(i):
        return jnp.maximum(i * halo_blocks - 1, 0)

    specs = [pl.BlockSpec((TM, d), lambda i: (i, 0)), pl.BlockSpec((halo, d), lambda i: (halo_index(i), 0))]
    args = [x, x]
    if pending is not None:
        yg, gates, ln_g, ln_b, ln_row = pending
        specs += [pl.BlockSpec((TM, d // 2), lambda i: (i, 0)),
                  pl.BlockSpec((halo, d // 2), lambda i: (halo_index(i), 0)),
                  pl.BlockSpec((TM, d // 2), lambda i: (i + tiles, 0)),
                  pl.BlockSpec((halo, d // 2), lambda i: (halo_index(i) + tiles * halo_blocks, 0)),
                  pl.BlockSpec((TM, TOP_K), lambda i: (i, 0)),
                  pl.BlockSpec((halo, TOP_K), lambda i: (halo_index(i), 0)),
                  _pick_spec(ln_g, ln_row), _pick_spec(ln_b, ln_row)]
        args += [yg, yg, yg, yg, gates, gates, ln_g, ln_b]
    return specs, args


def _even_kernel(*refs, tiles_per_seq, layer, pending):
    n_src = 10 if pending else 2
    (win_hbm, wpool_ref, pscale_ref, lvg_ref, lvb_ref, ws_ref, bst_ref, wout_hbm, g_ref, b_ref, o_ref,
     a_scr, cat_scr, win_ref, wout_ref, stage, sems) = refs[n_src:]
    tm = o_ref.shape[0]
    d_pool = a_scr.shape[1]
    d_sg = lvg_ref.shape[1]
    pgd = d_pool // len(POOL_WINDOWS)
    hd_dim = d_sg // N_SG_HEADS
    seq_tile = pl.program_id(0) % tiles_per_seq

    @pl.when(pl.program_id(0) == 0)
    def _():
        _load_cast(win_hbm.at[layer], win_ref, stage, sems)
        _load_cast(wout_hbm.at[layer], wout_ref, stage, sems)

    x, xh = _mixer_input(refs[:n_src], pending)
    sub = tm // SUB_TILES

    ah = _dot(xh.astype(BF16), win_ref[:, :d_pool])
    a_scr[0:POOL_HALO, :] = jnp.where(seq_tile == 0, 0.0, ah)
    row = lax.broadcasted_iota(I32, (CHUNK, CHUNK), 0)
    col = lax.broadcasted_iota(I32, (CHUNK, CHUNK), 1)
    ws_masked = [jnp.where(row >= col, ws_ref[hd], 0.0).astype(BF16) for hd in range(N_SG_HEADS)]

    def in_proj(st):
        h = _dot(x[st * sub:(st + 1) * sub, :].astype(BF16), win_ref[...])
        a_scr[POOL_HALO + st * sub:POOL_HALO + (st + 1) * sub, :] = h[:, :d_pool]
        return h[:, d_pool:]

    def branches_out_proj(st, hz):
        base = st * sub
        pos = seq_tile * tm + base + lax.broadcasted_iota(I32, (sub, 1), 0)
        for g, w in enumerate(POOL_WINDOWS):
            cs = slice(g * pgd, (g + 1) * pgd)
            tok = a_scr[POOL_HALO + base:POOL_HALO + base + sub, cs]
            acc = a_scr[base:POOL_HALO + base + sub, cs]
            span = 1
            while span < w:
                acc = acc[span:, :] + acc[:-span, :]
                span *= 2
            acc = acc[acc.shape[0] - sub:, :]
            cnt = jnp.minimum(pos + 1, w).astype(F32)
            dev = acc * (1.0 / cnt) - tok
            yg = _dot(dev.astype(BF16), wpool_ref[g])
            cat_scr[base:base + sub, cs] = (yg * pscale_ref[:, cs]).astype(BF16)

        z = _gelu_tanh(hz)
        u = z[:, :d_sg]
        v = _layer_norm(z[:, d_sg:], lvg_ref[...], lvb_ref[...]).astype(BF16)
        for hd in range(N_SG_HEADS):
            hs = slice(hd * hd_dim, (hd + 1) * hd_dim)
            bcol = bst_ref[:, hd:hd + 1]
            for ck in range(sub // CHUNK):
                rs = slice(ck * CHUNK, (ck + 1) * CHUNK)
                sv = _dot(ws_masked[hd], v[rs, hs]) + bcol
                cat_scr[base + ck * CHUNK:base + (ck + 1) * CHUNK,
                        d_pool + hd * hd_dim:d_pool + (hd + 1) * hd_dim] = (u[rs, hs] * sv).astype(BF16)
        return _dot(cat_scr[base:base + sub, :], wout_ref[...])

    hzs = [in_proj(st) for st in range(SUB_TILES)]
    mixes = [branches_out_proj(st, hzs[st]) for st in range(SUB_TILES)]
    for st in range(SUB_TILES):
        rs = slice(st * sub, (st + 1) * sub)
        o_ref[rs, :] = _layer_norm(ALPHA * x[rs, :] + mixes[st], g_ref[...], b_ref[...])


def _even_mixer(x, pending, seq, layer, w_in, w_pool, pool_scale, ln_v_g, ln_v_b, w_spatial, b_spatial_t,
                w_out, ln_g, ln_b, ln_row):
    n, d = x.shape
    d_in = w_in.shape[2]
    d_pool = pool_scale.shape[2]
    d_sg = ln_v_g.shape[2]
    kern = functools.partial(_even_kernel, tiles_per_seq=seq // TM, layer=layer, pending=pending is not None)
    src_specs, src_args = _mixer_sources(x, pending, POOL_HALO)
    return pl.pallas_call(
        kern,
        out_shape=jax.ShapeDtypeStruct((n, d), F32),
        grid=(n // TM,),
        in_specs=src_specs + [
            _HBM,
            _pick_spec(w_pool, layer),
            _pick_spec(pool_scale, layer),
            _pick_spec(ln_v_g, layer),
            _pick_spec(ln_v_b, layer),
            _pick_spec(w_spatial, layer),
            _pick_spec(b_spatial_t, layer),
            _HBM,
            _pick_spec(ln_g, ln_row),
            _pick_spec(ln_b, ln_row),
        ],
        out_specs=pl.BlockSpec((TM, d), lambda i: (i, 0)),
        scratch_shapes=[pltpu.VMEM((POOL_HALO + TM, d_pool), F32), pltpu.VMEM((TM, d_pool + d_sg), BF16),
                        pltpu.VMEM((d, d_in), BF16), pltpu.VMEM((d_pool + d_sg, d), BF16)] + _stage_scratch(d),
        compiler_params=_params(),
        name="even_mixer",
    )(*src_args, w_in, w_pool, pool_scale, ln_v_g, ln_v_b, w_spatial, b_spatial_t, w_out, ln_g, ln_b)


def _odd_kernel(*refs, tiles_per_seq, layer, pending):
    n_src = 10 if pending else 2
    (win_hbm, cwt_ref, cb_ref, wout_hbm, g_ref, b_ref, o_ref, zc_scr,
     win_ref, wout_ref, stage, sems) = refs[n_src:]
    tm, d = o_ref.shape
    seq_tile = pl.program_id(0) % tiles_per_seq

    @pl.when(pl.program_id(0) == 0)
    def _():
        _load_cast(win_hbm.at[layer], win_ref, stage, sems)
        _load_cast(wout_hbm.at[layer], wout_ref, stage, sems)

    x, xh = _mixer_input(refs[:n_src], pending)
    hh = _dot(xh.astype(BF16), win_ref[:, d:])
    zc_scr[0:CONV_HALO, :] = jnp.where(seq_tile == 0, 0.0, hh[:, :d] * hh[:, d:])
    sub = tm // SUB_TILES

    def in_proj(st):
        xb = x[st * sub:(st + 1) * sub, :].astype(BF16)
        hc = _dot(xb, win_ref[:, d:2 * d])
        hz = _dot(xb, win_ref[:, 2 * d:])
        zc_scr[CONV_HALO + st * sub:CONV_HALO + (st + 1) * sub, :] = hc * hz
        return _dot(xb, win_ref[:, :d])

    def conv_out_proj(st, gate):
        conv = cb_ref[...]
        for j in range(CONV_WIDTH):
            off = CONV_HALO + st * sub - (CONV_WIDTH - 1) + j
            conv = conv + zc_scr[off:off + sub, :] * cwt_ref[j:j + 1, :]
        return _dot((gate * conv).astype(BF16), wout_ref[...])

    gates = [in_proj(st) for st in range(SUB_TILES)]
    ys = [conv_out_proj(st, gates[st]) for st in range(SUB_TILES)]
    for st in range(SUB_TILES):
        rs = slice(st * sub, (st + 1) * sub)
        o_ref[rs, :] = _layer_norm(ALPHA * x[rs, :] + ys[st], g_ref[...], b_ref[...])


def _odd_mixer(x, pending, seq, layer, w_in, conv_w_t, conv_b, w_out, ln_g, ln_b, ln_row):
    n, d = x.shape
    kern = functools.partial(_odd_kernel, tiles_per_seq=seq // TM, layer=layer, pending=pending is not None)
    src_specs, src_args = _mixer_sources(x, pending, CONV_HALO)
    return pl.pallas_call(
        kern,
        out_shape=jax.ShapeDtypeStruct((n, d), F32),
        grid=(n // TM,),
        in_specs=src_specs + [
            _HBM,
            _pick_spec(conv_w_t, layer),
            _pick_spec(conv_b, layer),
            _HBM,
            _pick_spec(ln_g, ln_row),
            _pick_spec(ln_b, ln_row),
        ],
        out_specs=pl.BlockSpec((TM, d), lambda i: (i, 0)),
        scratch_shapes=[pltpu.VMEM((CONV_HALO + TM, d), F32),
                        pltpu.VMEM(w_in.shape[1:], BF16), pltpu.VMEM(w_out.shape[1:], BF16)] + _stage_scratch(d),
        compiler_params=_params(),
        name="odd_mixer",
    )(*src_args, w_in, conv_w_t, conv_b, w_out, ln_g, ln_b)


def _kv_kernel(mem_ref, wk_ref, wv_ref, k_ref, v_ref):
    m = mem_ref[...].astype(BF16)
    k_ref[...] = _dot(m, wk_ref[...].astype(BF16)).astype(BF16)
    v_ref[...] = _dot(m, wv_ref[...].astype(BF16)).astype(BF16)


def _memory_kv(mem2d, wk, wv, layer):
    rows, d = mem2d.shape
    out = jax.ShapeDtypeStruct((rows, d), BF16)
    return pl.pallas_call(
        _kv_kernel,
        out_shape=(out, out),
        grid=(1,),
        in_specs=[_const_spec((rows, d)), _pick_spec(wk, layer), _pick_spec(wv, layer)],
        out_specs=(_const_spec((rows, d)), _const_spec((rows, d))),
        compiler_params=_params(),
        name="memory_kv",
    )(mem2d, wk, wv)


def _attn_kernel(x_ref, k_ref, v_ref, wq_hbm, wo_hbm, g_ref, b_ref, wr_ref, br_ref,
                 o_ref, op_ref, rt_ref, rg_ref, cnt_ref, o_scr, carry_scr, wq_ref, wo_ref, stage, sems,
                 *, layer):
    tm, d = x_ref.shape
    hd_dim = d // N_XHEADS
    sub = tm // SUB_TILES

    @pl.when(pl.program_id(0) == 0)
    def _():
        carry_scr[...] = jnp.zeros_like(carry_scr)
        _load_cast(wq_hbm.at[layer], wq_ref, stage, sems)
        _load_cast(wo_hbm.at[layer], wo_ref, stage, sems)

    row_slices = [slice(st * sub, (st + 1) * sub) for st in range(SUB_TILES)]
    qs = [_dot(x_ref[rs, :].astype(BF16), wq_ref[...]) * (1.0 / math.sqrt(hd_dim)) for rs in row_slices]
    for rs, q in zip(row_slices, qs):
        for hd in range(N_XHEADS):
            hs = slice(hd * hd_dim, (hd + 1) * hd_dim)
            s = lax.dot_general(q[:, hs].astype(BF16), k_ref[:, hs], _NT, preferred_element_type=F32)
            p = jnp.exp(s - jnp.max(s, axis=-1, keepdims=True))
            p = p * (1.0 / jnp.sum(p, axis=-1, keepdims=True))
            o_scr[rs, hs] = _dot(p.astype(BF16), v_ref[:, hs]).astype(BF16)
    xas = [_dot(o_scr[rs, :], wo_ref[...]) for rs in row_slices]
    for rs, xa in zip(row_slices, xas):
        out = _layer_norm(ALPHA * x_ref[rs, :] + xa, g_ref[...], b_ref[...])
        o_ref[rs, :] = out
        op_ref[rs, :] = _pack_halves(out)
        table, gates = _route_rows(out, wr_ref, br_ref, carry_scr)
        rt_ref[:, rs] = table
        rg_ref[rs, :] = gates
    cnt_ref[...] = carry_scr[...].astype(I32)


def _cross_attn(x, seq, layer, k, v, wq, wo, ln_g, ln_b, ln_row, router_w, router_b):
    n, d = x.shape
    m = k.shape[1]
    tiles_per_seq = seq // TM
    kvspec = pl.BlockSpec((None, m, d), lambda i: (i // tiles_per_seq, 0, 0))
    out, packed, table, gates, cnt = pl.pallas_call(
        functools.partial(_attn_kernel, layer=layer),
        out_shape=(jax.ShapeDtypeStruct((n, d), F32), jax.ShapeDtypeStruct((n, d // 2), I32),
                   jax.ShapeDtypeStruct((ROUTE_ROWS, n), F32), jax.ShapeDtypeStruct((n, TOP_K), F32),
                   jax.ShapeDtypeStruct((N_EXPERTS, LANES), I32)),
        grid=(n // TM,),
        in_specs=[
            pl.BlockSpec((TM, d), lambda i: (i, 0)),
            kvspec, kvspec,
            _HBM, _HBM,
            _pick_spec(ln_g, ln_row), _pick_spec(ln_b, ln_row),
            _pick_spec(router_w, layer), _pick_spec(router_b, layer),
        ],
        out_specs=(pl.BlockSpec((TM, d), lambda i: (i, 0)), pl.BlockSpec((TM, d // 2), lambda i: (i, 0)),
                   pl.BlockSpec((ROUTE_ROWS, TM), lambda i: (0, i)), pl.BlockSpec((TM, TOP_K), lambda i: (i, 0)),
                   _const_spec((N_EXPERTS, LANES))),
        scratch_shapes=[pltpu.VMEM((TM, d), BF16), pltpu.VMEM((N_EXPERTS, LANES), F32),
                        pltpu.VMEM((d, d), BF16), pltpu.VMEM((d, d), BF16)] + _stage_scratch(d),
        compiler_params=_params(),
        name="cross_attn",
    )(x, k, v, wq, wo, ln_g, ln_b, router_w, router_b)
    return out, packed, table, gates, cnt[:, 0]


def _route_rows(x, w_ref, bias_ref, carry_scr):
    tm = x.shape[0]
    neg = -jnp.inf
    logits = _dot(x.astype(BF16), w_ref[...]) + bias_ref[...]
    lt = jnp.transpose(logits)

    def first_argmax(vals):
        rows = lax.broadcasted_iota(I32, vals.shape, 0).astype(F32)
        mx = jnp.max(vals, axis=0, keepdims=True)
        idx = jnp.min(jnp.where(vals == mx, rows, float(vals.shape[0])), axis=0, keepdims=True)
        return mx, idx, rows

    gl = lt[0:N_GROUPS, :]
    gmax, g_sel, _ = first_argmax(gl)
    gate_g = 1.0 / jnp.sum(jnp.exp(gl - gmax), axis=0, keepdims=True)

    el = lt[EXPERT_ROW0:EXPERT_ROW0 + EXPERTS_PER_GROUP, :]
    for g in range(1, N_GROUPS):
        lo = EXPERT_ROW0 + g * EXPERTS_PER_GROUP
        el = jnp.where(g_sel == float(g), lt[lo:lo + EXPERTS_PER_GROUP, :], el)
    m1, i1, erow = first_argmax(el)
    m2, i2, _ = first_argmax(jnp.where(erow == i1, neg, el))
    e21 = jnp.exp(m2 - m1)
    w1 = 1.0 / (1.0 + e21)
    w2 = e21 / (1.0 + e21)
    e1 = g_sel * EXPERTS_PER_GROUP + i1
    e2 = g_sel * EXPERTS_PER_GROUP + i2

    xrow = lax.broadcasted_iota(I32, (N_EXPERTS, tm), 0).astype(F32)
    oh1 = xrow == e1
    oh2 = xrow == e2
    oh = (oh1 | oh2).astype(BF16)
    r = lax.broadcasted_iota(I32, (tm, tm), 0)
    c = lax.broadcasted_iota(I32, (tm, tm), 1)
    before = _dot(oh, (r < c).astype(BF16)) + carry_scr[:, 0:1]
    rank1 = jnp.sum(jnp.where(oh1, before, 0.0), axis=0, keepdims=True)
    rank2 = jnp.sum(jnp.where(oh2, before, 0.0), axis=0, keepdims=True)
    carry_scr[...] += jnp.sum(oh.astype(F32), axis=1, keepdims=True)

    trow = lax.broadcasted_iota(I32, (ROUTE_ROWS, tm), 0)
    table = jnp.where(trow == 0, e1, jnp.where(trow == 1, e2, jnp.where(trow == 2, rank1,
                                                                      jnp.where(trow == 3, rank2, 0.0))))
    grow = lax.broadcasted_iota(I32, (LANES, tm), 0)
    gates = jnp.transpose(jnp.where(grow == 0, gate_g * w1, jnp.where(grow == 1, gate_g * w2, 0.0)))[:, :TOP_K]
    return table, gates


def _router_weights(wr_g, br_g, wr_e, br_e):
    nl, d, _ = wr_g.shape
    gap = EXPERT_ROW0 - N_GROUPS
    w = jnp.concatenate([wr_g, jnp.zeros((nl, d, gap), F32),
                         jnp.transpose(wr_e, (0, 2, 1, 3)).reshape(nl, d, N_EXPERTS)], axis=2)
    w = jnp.pad(w, ((0, 0), (0, 0), (0, ROUTE_COLS - w.shape[2]))).astype(BF16)
    bias = jnp.concatenate([br_g, jnp.zeros((nl, gap), F32), br_e.reshape(nl, -1)], axis=1)
    bias = jnp.pad(bias, ((0, 0), (0, ROUTE_COLS - bias.shape[1])))
    return w, bias[:, None, :]


def _sc_worker_rows(rows):
    per_worker = rows // SC_WORKERS
    n_chunks = per_worker // SC_CHUNK
    assert per_worker * SC_WORKERS == rows and n_chunks * SC_CHUNK == per_worker and n_chunks % 2 == 0
    return per_worker, n_chunks


def _sc_gather_rows(table_hbm, out_hbm, idx_v, rows_v, gsem, wsem, base, n_chunks):
    def fetch(c, slot):
        off = pl.multiple_of(c * SC_CHUNK, SC_CHUNK)
        return pltpu.make_async_copy(table_hbm.at[idx_v.at[pl.ds(off, SC_CHUNK)]], rows_v.at[slot],
                                     gsem.at[slot])

    def put(c, slot):
        off = pl.multiple_of(c * SC_CHUNK, SC_CHUNK)
        return pltpu.make_async_copy(rows_v.at[slot], out_hbm.at[pl.ds(base + off, SC_CHUNK)], wsem.at[slot])

    fetch(0, 0).start()

    @pl.loop(0, n_chunks, step=2)
    def _(c0):
        for slot in range(2):
            c = c0 + slot

            @pl.when(c + 1 < n_chunks)
            def _():
                @pl.when(c >= 1)
                def _():
                    put(c - 1, 1 - slot).wait()
                fetch(c + 1, 1 - slot).start()

            fetch(c, slot).wait()
            put(c, slot).start()

    put(n_chunks - 2, 0).wait()
    put(n_chunks - 1, 1).wait()


def _sc_row_scratch(per_worker, d, dtype):
    return [pltpu.VMEM((per_worker,), I32), pltpu.VMEM((2, SC_CHUNK, d), dtype),
            pltpu.SemaphoreType.DMA((2,)), pltpu.SemaphoreType.DMA((2,))]


def _sc_gather(table, idx):
    b = idx.shape[0]
    d = table.shape[1]
    per_worker, n_chunks = _sc_worker_rows(b)
    mesh = plsc.VectorSubcoreMesh(core_axis_name="c", subcore_axis_name="s")

    @functools.partial(
        pl.kernel, mesh=mesh,
        out_type=jax.ShapeDtypeStruct((b, d), table.dtype),
        scratch_types=_sc_row_scratch(per_worker, d, table.dtype),
        name="sc_gather",
    )
    def gather(table_hbm, idx_hbm, out_hbm, idx_v, rows_v, gsem, wsem):
        base = (lax.axis_index("s") * SC_CORES + lax.axis_index("c")) * per_worker
        pltpu.sync_copy(idx_hbm.at[pl.ds(base, per_worker)], idx_v)
        _sc_gather_rows(table_hbm, out_hbm, idx_v, rows_v, gsem, wsem, base, n_chunks)

    return gather(table, idx)


def _sc_dispatch(table, dest_flat, rows):
    n, d = table.shape
    a = dest_flat.shape[0]
    lanes = SC_LANES
    per_worker, n_chunks = _sc_worker_rows(rows)
    assert a % lanes == 0 and per_worker % lanes == 0
    mesh = plsc.VectorSubcoreMesh(core_axis_name="c", subcore_axis_name="s")

    @functools.partial(
        pl.kernel, mesh=mesh,
        out_type=jax.ShapeDtypeStruct((rows, d), table.dtype),
        scratch_types=[pltpu.VMEM((a,), I32)] + _sc_row_scratch(per_worker, d, table.dtype),
        compiler_params=pltpu.CompilerParams(needs_layout_passes=False),
        name="sc_dispatch",
    )
    def dispatch(table_hbm, dest_hbm, out_hbm, dest_v, idx_v, rows_v, gsem, wsem):
        base = (lax.axis_index("s") * SC_CORES + lax.axis_index("c")) * per_worker
        pltpu.sync_copy(dest_hbm, dest_v)
        lane = lax.iota(I32, lanes)

        @pl.loop(0, per_worker // lanes)
        def _(i):
            idx_v[pl.ds(i * lanes, lanes)] = lax.rem(base + i * lanes + lane, n)

        for k in range(a // n):
            @plsc.parallel_loop(0, n // lanes, unroll=8)
            def _(i):
                local = dest_v[pl.ds(k * n + i * lanes, lanes)] - base
                mine = (local >= 0) & (local < per_worker)
                plsc.store_scatter(idx_v, [jnp.where(mine, local, 0)], i * lanes + lane, mask=mine)

        _sc_gather_rows(table_hbm, out_hbm, idx_v, rows_v, gsem, wsem, base, n_chunks)

    return dispatch(table, dest_flat)


def _expert_kernel(sched_ref, ni_ref, xs_hbm, w1_hbm, w3_hbm, w2_hbm, y_hbm,
                   x_buf, y_buf, w1_buf, w3_buf, w2_buf, w1_scr, w3_scr, w2_scr, wsems, xsems, ysems,
                   *, layer):
    i = pl.program_id(0)
    n_items = ni_ref[0]
    expert, wslot, run_start, next_expert = (sched_ref[r, i] for r in range(4))
    slot = i % 2

    def for_units(item, fn):
        for units in range(1, ITEM_UNITS + 1):
            @pl.when(sched_ref[5, item] == units)
            def _():
                fn(units * EXPERT_UNIT)

    def x_copy(item, s, rows):
        row0 = pl.multiple_of(sched_ref[4, item], EXPERT_UNIT)
        return pltpu.make_async_copy(xs_hbm.at[pl.ds(row0, rows)], x_buf.at[s, pl.ds(0, rows)], xsems.at[s])

    def y_copy(item, s, rows):
        row0 = pl.multiple_of(sched_ref[4, item], EXPERT_UNIT)
        return pltpu.make_async_copy(y_buf.at[s, pl.ds(0, rows)], y_hbm.at[pl.ds(row0, rows)], ysems.at[s])

    def fetch(e, s):
        return [pltpu.make_async_copy(w_hbm.at[layer, e], buf.at[s], wsems.at[s, j])
                for j, (w_hbm, buf) in enumerate(((w1_hbm, w1_buf), (w3_hbm, w3_buf), (w2_hbm, w2_buf)))]

    def mlp(rows):
        x_lo, x_hi = _unpack_halves(x_buf[slot, 0:rows, :])
        xb = jnp.concatenate([x_lo.astype(BF16), x_hi.astype(BF16)], axis=1)
        h1 = _dot(xb, w1_scr[...])
        h3 = _dot(xb, w3_scr[...])
        hid = h1 * (1.0 / (1.0 + jnp.exp(-h1))) * h3
        y_buf[slot, 0:rows, :] = _pack_halves(_dot(hid.astype(BF16), w2_scr[...]))
        y_copy(i, slot, rows).start()

    @pl.when(i < n_items)
    def _():
        @pl.when(i == 0)
        def _():
            for_units(0, lambda rows: x_copy(0, 0, rows).start())

        for_units(i, lambda rows: x_copy(i, slot, rows).wait())

        @pl.when(i + 1 < n_items)
        def _():
            for_units(i + 1, lambda rows: x_copy(i + 1, 1 - slot, rows).start())

        @pl.when(run_start == 1)
        def _():
            @pl.when(i == 0)
            def _():
                for c in fetch(expert, wslot):
                    c.start()

            for c in fetch(expert, wslot):
                c.wait()

            @pl.when(next_expert >= 0)
            def _():
                for c in fetch(next_expert, 1 - wslot):
                    c.start()

            w1_scr[...] = w1_buf[wslot].astype(BF16)
            w3_scr[...] = w3_buf[wslot].astype(BF16)
            w2_scr[...] = w2_buf[wslot].astype(BF16)

        @pl.when(i >= 2)
        def _():
            for_units(i - 2, lambda rows: y_copy(i - 2, slot, rows).wait())

        for_units(i, mlp)

        @pl.when(i == n_items - 1)
        def _():
            @pl.when(i >= 1)
            def _():
                for_units(i - 1, lambda rows: y_copy(i - 1, 1 - slot, rows).wait())

            for_units(i, lambda rows: y_copy(i, slot, rows).wait())
            y_buf[0, 0:EXPERT_UNIT, :] = jnp.zeros((EXPERT_UNIT, y_buf.shape[2]), y_buf.dtype)
            first_free = (sched_ref[4, i] + sched_ref[5, i] * EXPERT_UNIT) // EXPERT_UNIT

            def zero_copy(u):
                return pltpu.make_async_copy(
                    y_buf.at[0, pl.ds(0, EXPERT_UNIT)],
                    y_hbm.at[pl.ds(pl.multiple_of(u * EXPERT_UNIT, EXPERT_UNIT), EXPERT_UNIT)], ysems.at[0])

            def start_zero(u, _):
                zero_copy(u).start()
                return 0

            def wait_zero(u, _):
                zero_copy(u).wait()
                return 0

            lax.fori_loop(first_free, y_hbm.shape[0] // EXPERT_UNIT, start_zero, 0)
            lax.fori_loop(first_free, y_hbm.shape[0] // EXPERT_UNIT, wait_zero, 0)


def _expert_mlp(xs, schedule, n_items, layer, w1, w3, w2):
    d, de = w1.shape[2], w1.shape[3]
    max_rows = ITEM_UNITS * EXPERT_UNIT
    return pl.pallas_call(
        functools.partial(_expert_kernel, layer=layer),
        out_shape=jax.ShapeDtypeStruct(xs.shape, I32),
        grid_spec=pltpu.PrefetchScalarGridSpec(
            num_scalar_prefetch=2,
            grid=(schedule.shape[1],),
            in_specs=[_HBM, _HBM, _HBM, _HBM],
            out_specs=_HBM,
            scratch_shapes=[pltpu.VMEM((2, max_rows, d // 2), I32), pltpu.VMEM((2, max_rows, d // 2), I32),
                            pltpu.VMEM((2, d, de), F32), pltpu.VMEM((2, d, de), F32), pltpu.VMEM((2, de, d), F32),
                            pltpu.VMEM((d, de), BF16), pltpu.VMEM((d, de), BF16), pltpu.VMEM((de, d), BF16),
                            pltpu.SemaphoreType.DMA((2, 3)), pltpu.SemaphoreType.DMA((2,)),
                            pltpu.SemaphoreType.DMA((2,))],
        ),
        compiler_params=_params(),
        name="expert_mlp",
    )(schedule, n_items, xs, w1, w3, w2)


def _combine_kernel(x_ref, y0_ref, y1_ref, gate_ref, g_ref, b_ref, o_ref):
    o_ref[...] = _moe_output(x_ref, y0_ref, y1_ref, gate_ref, g_ref, b_ref)


def _combine(x, yg, gates, ln_g, ln_b, ln_row):
    n, d = x.shape
    tiles = n // TM
    return pl.pallas_call(
        _combine_kernel,
        out_shape=jax.ShapeDtypeStruct((n, d), F32),
        grid=(tiles,),
        in_specs=[pl.BlockSpec((TM, d), lambda i: (i, 0)),
                  pl.BlockSpec((TM, d // 2), lambda i: (i, 0)),
                  pl.BlockSpec((TM, d // 2), lambda i: (i + tiles, 0)),
                  pl.BlockSpec((TM, TOP_K), lambda i: (i, 0)),
                  _pick_spec(ln_g, ln_row), _pick_spec(ln_b, ln_row)],
        out_specs=pl.BlockSpec((TM, d), lambda i: (i, 0)),
        compiler_params=_params(),
        name="combine",
    )(x, yg, yg, gates, ln_g, ln_b)


def _moe_experts(x_packed, table, counts, layer, w1, w3, w2):
    n = x_packed.shape[0]
    max_units = (n * TOP_K + N_EXPERTS * (EXPERT_UNIT - 1) + EXPERT_UNIT - 1) // EXPERT_UNIT
    max_items = (max_units + N_EXPERTS * (ITEM_UNITS - 1) + ITEM_UNITS - 1) // ITEM_UNITS
    experts = table[:TOP_K].astype(I32)
    ranks = table[TOP_K:2 * TOP_K].astype(I32)

    units_e = (counts + EXPERT_UNIT - 1) // EXPERT_UNIT
    units_start = jnp.cumsum(units_e) - units_e
    items_e = (units_e + ITEM_UNITS - 1) // ITEM_UNITS
    items_end = jnp.cumsum(items_e)
    n_items = items_end[-1:].astype(I32)
    item_ids = jnp.arange(max_items, dtype=I32)
    expert_ids = jnp.arange(N_EXPERTS, dtype=I32)
    item_expert = jnp.minimum(jnp.sum(items_end[None, :] <= item_ids[:, None], axis=1), N_EXPERTS - 1)
    later = (expert_ids[None, :] > expert_ids[:, None]) & (items_e[None, :] > 0)
    next_run = jnp.min(jnp.where(later, expert_ids[None, :], N_EXPERTS), axis=1)
    next_run = jnp.where(next_run == N_EXPERTS, -1, next_run)
    per_expert = jnp.stack([items_end - items_e, units_start, units_e, next_run], axis=1)
    mine = (item_expert[:, None] == expert_ids[None, :])[:, :, None]
    first_item, unit0, units, next_expert = jnp.sum(jnp.where(mine, per_expert[None], 0), axis=1).T
    within = item_ids - first_item
    item_row0 = (unit0 + ITEM_UNITS * within) * EXPERT_UNIT
    item_units = jnp.clip(units - ITEM_UNITS * within, 1, ITEM_UNITS)
    run_start_flag = ((within == 0) & (item_ids < n_items[0])).astype(I32)
    slot = (jnp.cumsum(run_start_flag) - 1) % 2
    schedule = jnp.stack([item_expert, slot, run_start_flag, next_expert, item_row0, item_units]).astype(I32)
    start_of = jnp.sum(jnp.where(experts[:, :, None] == expert_ids, units_start * EXPERT_UNIT, 0), axis=-1)
    dest = (start_of + ranks).astype(I32).reshape(-1)
    xs = _sc_dispatch(x_packed, dest, max_units * EXPERT_UNIT)
    y = _expert_mlp(xs, schedule, n_items, layer, w1, w3, w2)
    return _sc_gather(y, dest)


def kernel(x, mem, w_in_even, w_pool, pool_scale, ln_v_g, ln_v_b, w_spatial, b_spatial, w_out_even,
           w_in_odd, conv_w, conv_b, w_out_odd, wq_x, wk_x, wv_x, wo_x, ln_g, ln_b, wr_group,
           br_group, wr_expert, br_expert, w1, w3, w2):
    bsz, seq, d = x.shape
    assert seq % TM == 0 and d % LANES == 0
    mlen = mem.shape[1]
    mem2d = mem.reshape(bsz * mlen, d)
    ln_g = ln_g.reshape(DEPTH * 3, 1, d)
    ln_b = ln_b.reshape(DEPTH * 3, 1, d)
    w_pool = w_pool.astype(BF16)
    pool_scale, ln_v_g, ln_v_b, conv_b = (p[:, None, :] for p in (pool_scale, ln_v_g, ln_v_b, conv_b))
    b_spatial_t = jnp.swapaxes(b_spatial, 1, 2)
    conv_w_t = jnp.swapaxes(conv_w, 1, 2)
    router_w, router_b = _router_weights(wr_group, br_group, wr_expert, br_expert)

    h = x.reshape(bsz * seq, d)
    pending = None
    kv = _memory_kv(mem2d, wk_x, wv_x, 0)
    for l in range(DEPTH):
        i = l // 2
        k, v = (a.reshape(bsz, mlen, d) for a in kv)
        if l % 2 == 0:
            h = _even_mixer(h, pending, seq, i, w_in_even, w_pool, pool_scale, ln_v_g, ln_v_b,
                            w_spatial, b_spatial_t, w_out_even, ln_g, ln_b, 3 * l)
        else:
            h = _odd_mixer(h, pending, seq, i, w_in_odd, conv_w_t, conv_b, w_out_odd, ln_g, ln_b, 3 * l)
        h, hp, table, gates, counts = _cross_attn(h, seq, l, k, v, wq_x, wo_x,
                                                  ln_g, ln_b, 3 * l + 1, router_w, router_b)
        if l + 1 < DEPTH:
            kv = _memory_kv(mem2d, wk_x, wv_x, l + 1)
        yg = _moe_experts(hp, table, counts, l, w1, w3, w2)
        pending = (yg, gates, ln_g, ln_b, 3 * l + 2)
    return _combine(h, *pending).reshape(bsz, seq, d)
```

```python
import functools
import math

import jax
import jax.numpy as jnp
from jax import lax
from jax.experimental import pallas as pl
from jax.experimental.pallas import tpu as pltpu
from jax.experimental.pallas import tpu_sc as plsc

F32 = jnp.float32
BF16 = jnp.bfloat16
I32 = jnp.int32

POOL_WINDOWS = (2, 4, 8, 16)
assert all(w & (w - 1) == 0 for w in POOL_WINDOWS)
N_SG_HEADS = 4
CHUNK = 128
CONV_WIDTH = 3
N_XHEADS = 4
N_GROUPS = 4
EXPERTS_PER_GROUP = 8
N_EXPERTS = N_GROUPS * EXPERTS_PER_GROUP
TOP_K = 2
DEPTH = 4
ALPHA = (2.0 * DEPTH) ** 0.25
LN_EPS = 1e-5

LANES = 128
SC_CORES = 2
SC_WORKERS = 32
SC_LANES = 16
VMEM_LIMIT = 56 * 1024 * 1024

SC_CHUNK = 64
TM = 1024
SUB_TILES = 2
ROUTE_ROWS = 8
POOL_HALO = 16
CONV_HALO = 8
EXPERT_UNIT = 128
ITEM_UNITS = 4
ROUTE_COLS = 128
EXPERT_ROW0 = 8
STAGE_COLS = 512

_NT = (((1,), (1,)), ((), ()))


def _dot(a, b):
    return jnp.dot(a, b, preferred_element_type=F32)


def _layer_norm(y, g, b):
    mu = jnp.mean(y, axis=-1, keepdims=True)
    yc = y - mu
    var = jnp.mean(yc * yc, axis=-1, keepdims=True)
    return yc * lax.rsqrt(var + LN_EPS) * g + b


def _gelu_tanh(x):
    c = math.sqrt(2.0 / math.pi)
    return 0.5 * x * (1.0 + jnp.tanh(c * (x + 0.044715 * (x * x * x))))


def _pack_halves(v):
    c = v.shape[1] // 2
    lo = pltpu.bitcast(v[:, :c].astype(BF16).astype(F32), jnp.uint32)
    hi = pltpu.bitcast(v[:, c:].astype(BF16).astype(F32), jnp.uint32)
    return pltpu.bitcast((hi & jnp.uint32(0xFFFF0000)) | (lo >> 16), I32)


def _unpack_halves(w):
    u = pltpu.bitcast(w, jnp.uint32)
    return pltpu.bitcast(u << 16, F32), pltpu.bitcast(u & jnp.uint32(0xFFFF0000), F32)


def _load_cast(w_hbm, w_scr, stage, sems):
    chunks = w_scr.shape[1] // STAGE_COLS

    def chunk_copy(c):
        return pltpu.make_async_copy(w_hbm.at[:, pl.ds(c * STAGE_COLS, STAGE_COLS)], stage.at[c % 2],
                                     sems.at[c % 2])

    chunk_copy(0).start()
    for c in range(chunks):
        if c + 1 < chunks:
            chunk_copy(c + 1).start()
        chunk_copy(c).wait()
        w_scr[:, c * STAGE_COLS:(c + 1) * STAGE_COLS] = stage[c % 2].astype(BF16)


def _stage_scratch(rows):
    return [pltpu.VMEM((2, rows, STAGE_COLS), F32), pltpu.SemaphoreType.DMA((2,))]


_HBM = pl.BlockSpec(memory_space=pl.ANY)


def _const_spec(shape):
    nd = len(shape)
    return pl.BlockSpec(shape, lambda i: (0,) * nd)


def _pick_spec(stacked, index):
    rest = stacked.shape[1:]
    return pl.BlockSpec((None,) + rest, lambda i: (index,) + (0,) * len(rest))


def _params():
    return pltpu.CompilerParams(dimension_semantics=("arbitrary",), vmem_limit_bytes=VMEM_LIMIT)


def _moe_output(x_ref, y0_ref, y1_ref, gate_ref, g_ref, b_ref):
    gates = gate_ref[...]
    g0, g1 = gates[:, 0:1], gates[:, 1:2]
    y0_lo, y0_hi = _unpack_halves(y0_ref[...])
    y1_lo, y1_hi = _unpack_halves(y1_ref[...])
    ff = jnp.concatenate([g0 * y0_lo + g1 * y1_lo, g0 * y0_hi + g1 * y1_hi], axis=1)
    return _layer_norm(ALPHA * x_ref[...] + ff, g_ref[...], b_ref[...])


def _mixer_input(src, pending):
    if not pending:
        x_ref, xh_ref = src
        return x_ref[...], xh_ref[...]
    x_ref, xh_ref, y0_ref, y0h_ref, y1_ref, y1h_ref, gate_ref, gateh_ref, g_ref, b_ref = src
    return (_moe_output(x_ref, y0_ref, y1_ref, gate_ref, g_ref, b_ref),
            _moe_output(xh_ref, y0h_ref, y1h_ref, gateh_ref, g_ref, b_ref))


def _mixer_sources(x, pending, halo):
    n, d = x.shape
    tiles = n // TM
    halo_blocks = TM // halo

    def halo_index(i):
        return jnp.maximum(i * halo_blocks - 1, 0)

    specs = [pl.BlockSpec((TM, d), lambda i: (i, 0)), pl.BlockSpec((halo, d), lambda i: (halo_index(i), 0))]
    args = [x, x]
    if pending is not None:
        yg, gates, ln_g, ln_b, ln_row = pending
        specs += [pl.BlockSpec((TM, d // 2), lambda i: (i, 0)),
                  pl.BlockSpec((halo, d // 2), lambda i: (halo_index(i), 0)),
                  pl.BlockSpec((TM, d // 2), lambda i: (i + tiles, 0)),
                  pl.BlockSpec((halo, d // 2), lambda i: (halo_index(i) + tiles * halo_blocks, 0)),
                  pl.BlockSpec((TM, TOP_K), lambda i: (i, 0)),
                  pl.BlockSpec((halo, TOP_K), lambda i: (halo_index(i), 0)),
                  _pick_spec(ln_g, ln_row), _pick_spec(ln_b, ln_row)]
        args += [yg, yg, yg, yg, gates, gates, ln_g, ln_b]
    return specs, args


def _even_kernel(*refs, tiles_per_seq, layer, pending):
    n_src = 10 if pending else 2
    (win_hbm, wpool_ref, pscale_ref, lvg_ref, lvb_ref, ws_ref, bst_ref, wout_hbm, g_ref, b_ref, o_ref,
     a_scr, cat_scr, win_ref, wout_ref, stage, sems) = refs[n_src:]
    tm = o_ref.shape[0]
    d_pool = a_scr.shape[1]
    d_sg = lvg_ref.shape[1]
    pgd = d_pool // len(POOL_WINDOWS)
    hd_dim = d_sg // N_SG_HEADS
    seq_tile = pl.program_id(0) % tiles_per_seq

    @pl.when(pl.program_id(0) == 0)
    def _():
        _load_cast(win_hbm.at[layer], win_ref, stage, sems)
        _load_cast(wout_hbm.at[layer], wout_ref, stage, sems)

    x, xh = _mixer_input(refs[:n_src], pending)
    sub = tm // SUB_TILES

    ah = _dot(xh.astype(BF16), win_ref[:, :d_pool])
    a_scr[0:POOL_HALO, :] = jnp.where(seq_tile == 0, 0.0, ah)
    row = lax.broadcasted_iota(I32, (CHUNK, CHUNK), 0)
    col = lax.broadcasted_iota(I32, (CHUNK, CHUNK), 1)
    ws_masked = [jnp.where(row >= col, ws_ref[hd], 0.0).astype(BF16) for hd in range(N_SG_HEADS)]

    def in_proj(st):
        h = _dot(x[st * sub:(st + 1) * sub, :].astype(BF16), win_ref[...])
        a_scr[POOL_HALO + st * sub:POOL_HALO + (st + 1) * sub, :] = h[:, :d_pool]
        return h[:, d_pool:]

    def branches_out_proj(st, hz):
        base = st * sub
        pos = seq_tile * tm + base + lax.broadcasted_iota(I32, (sub, 1), 0)
        for g, w in enumerate(POOL_WINDOWS):
            cs = slice(g * pgd, (g + 1) * pgd)
            tok = a_scr[POOL_HALO + base:POOL_HALO + base + sub, cs]
            acc = a_scr[base:POOL_HALO + base + sub, cs]
            span = 1
            while span < w:
                acc = acc[span:, :] + acc[:-span, :]
                span *= 2
            acc = acc[acc.shape[0] - sub:, :]
            cnt = jnp.minimum(pos + 1, w).astype(F32)
            dev = acc * (1.0 / cnt) - tok
            yg = _dot(dev.astype(BF16), wpool_ref[g])
            cat_scr[base:base + sub, cs] = (yg * pscale_ref[:, cs]).astype(BF16)

        z = _gelu_tanh(hz)
        u = z[:, :d_sg]
        v = _layer_norm(z[:, d_sg:], lvg_ref[...], lvb_ref[...]).astype(BF16)
        for hd in range(N_SG_HEADS):
            hs = slice(hd * hd_dim, (hd + 1) * hd_dim)
            bcol = bst_ref[:, hd:hd + 1]
            for ck in range(sub // CHUNK):
                rs = slice(ck * CHUNK, (ck + 1) * CHUNK)
                sv = _dot(ws_masked[hd], v[rs, hs]) + bcol
                cat_scr[base + ck * CHUNK:base + (ck + 1) * CHUNK,
                        d_pool + hd * hd_dim:d_pool + (hd + 1) * hd_dim] = (u[rs, hs] * sv).astype(BF16)
        return _dot(cat_scr[base:base + sub, :], wout_ref[...])

    hzs = [in_proj(st) for st in range(SUB_TILES)]
    mixes = [branches_out_proj(st, hzs[st]) for st in range(SUB_TILES)]
    for st in range(SUB_TILES):
        rs = slice(st * sub, (st + 1) * sub)
        o_ref[rs, :] = _layer_norm(ALPHA * x[rs, :] + mixes[st], g_ref[...], b_ref[...])


def _even_mixer(x, pending, seq, layer, w_in, w_pool, pool_scale, ln_v_g, ln_v_b, w_spatial, b_spatial_t,
                w_out, ln_g, ln_b, ln_row):
    n, d = x.shape
    d_in = w_in.shape[2]
    d_pool = pool_scale.shape[2]
    d_sg = ln_v_g.shape[2]
    kern = functools.partial(_even_kernel, tiles_per_seq=seq // TM, layer=layer, pending=pending is not None)
    src_specs, src_args = _mixer_sources(x, pending, POOL_HALO)
    return pl.pallas_call(
        kern,
        out_shape=jax.ShapeDtypeStruct((n, d), F32),
        grid=(n // TM,),
        in_specs=src_specs + [
            _HBM,
            _pick_spec(w_pool, layer),
            _pick_spec(pool_scale, layer),
            _pick_spec(ln_v_g, layer),
            _pick_spec(ln_v_b, layer),
            _pick_spec(w_spatial, layer),
            _pick_spec(b_spatial_t, layer),
            _HBM,
            _pick_spec(ln_g, ln_row),
            _pick_spec(ln_b, ln_row),
        ],
        out_specs=pl.BlockSpec((TM, d), lambda i: (i, 0)),
        scratch_shapes=[pltpu.VMEM((POOL_HALO + TM, d_pool), F32), pltpu.VMEM((TM, d_pool + d_sg), BF16),
                        pltpu.VMEM((d, d_in), BF16), pltpu.VMEM((d_pool + d_sg, d), BF16)] + _stage_scratch(d),
        compiler_params=_params(),
        name="even_mixer",
    )(*src_args, w_in, w_pool, pool_scale, ln_v_g, ln_v_b, w_spatial, b_spatial_t, w_out, ln_g, ln_b)


def _odd_kernel(*refs, tiles_per_seq, layer, pending):
    n_src = 10 if pending else 2
    (win_hbm, cwt_ref, cb_ref, wout_hbm, g_ref, b_ref, o_ref, zc_scr,
     win_ref, wout_ref, stage, sems) = refs[n_src:]
    tm, d = o_ref.shape
    seq_tile = pl.program_id(0) % tiles_per_seq

    @pl.when(pl.program_id(0) == 0)
    def _():
        _load_cast(win_hbm.at[layer], win_ref, stage, sems)
        _load_cast(wout_hbm.at[layer], wout_ref, stage, sems)

    x, xh = _mixer_input(refs[:n_src], pending)
    hh = _dot(xh.astype(BF16), win_ref[:, d:])
    zc_scr[0:CONV_HALO, :] = jnp.where(seq_tile == 0, 0.0, hh[:, :d] * hh[:, d:])
    sub = tm // SUB_TILES

    def in_proj(st):
        xb = x[st * sub:(st + 1) * sub, :].astype(BF16)
        hc = _dot(xb, win_ref[:, d:2 * d])
        hz = _dot(xb, win_ref[:, 2 * d:])
        zc_scr[CONV_HALO + st * sub:CONV_HALO + (st + 1) * sub, :] = hc * hz
        return _dot(xb, win_ref[:, :d])

    def conv_out_proj(st, gate):
        conv = cb_ref[...]
        for j in range(CONV_WIDTH):
            off = CONV_HALO + st * sub - (CONV_WIDTH - 1) + j
            conv = conv + zc_scr[off:off + sub, :] * cwt_ref[j:j + 1, :]
        return _dot((gate * conv).astype(BF16), wout_ref[...])

    gates = [in_proj(st) for st in range(SUB_TILES)]
    ys = [conv_out_proj(st, gates[st]) for st in range(SUB_TILES)]
    for st in range(SUB_TILES):
        rs = slice(st * sub, (st + 1) * sub)
        o_ref[rs, :] = _layer_norm(ALPHA * x[rs, :] + ys[st], g_ref[...], b_ref[...])


def _odd_mixer(x, pending, seq, layer, w_in, conv_w_t, conv_b, w_out, ln_g, ln_b, ln_row):
    n, d = x.shape
    kern = functools.partial(_odd_kernel, tiles_per_seq=seq // TM, layer=layer, pending=pending is not None)
    src_specs, src_args = _mixer_sources(x, pending, CONV_HALO)
    return pl.pallas_call(
        kern,
        out_shape=jax.ShapeDtypeStruct((n, d), F32),
        grid=(n // TM,),
        in_specs=src_specs + [
            _HBM,
            _pick_spec(conv_w_t, layer),
            _pick_spec(conv_b, layer),
            _HBM,
            _pick_spec(ln_g, ln_row),
            _pick_spec(ln_b, ln_row),
        ],
        out_specs=pl.BlockSpec((TM, d), lambda i: (i, 0)),
        scratch_shapes=[pltpu.VMEM((CONV_HALO + TM, d), F32),
                        pltpu.VMEM(w_in.shape[1:], BF16), pltpu.VMEM(w_out.shape[1:], BF16)] + _stage_scratch(d),
        compiler_params=_params(),
        name="odd_mixer",
    )(*src_args, w_in, conv_w_t, conv_b, w_out, ln_g, ln_b)


def _kv_kernel(mem_ref, wk_ref, wv_ref, k_ref, v_ref):
    m = mem_ref[...].astype(BF16)
    k_ref[...] = _dot(m, wk_ref[...].astype(BF16)).astype(BF16)
    v_ref[...] = _dot(m, wv_ref[...].astype(BF16)).astype(BF16)


def _memory_kv(mem2d, wk, wv, layer):
    rows, d = mem2d.shape
    out = jax.ShapeDtypeStruct((rows, d), BF16)
    return pl.pallas_call(
        _kv_kernel,
        out_shape=(out, out),
        grid=(1,),
        in_specs=[_const_spec((rows, d)), _pick_spec(wk, layer), _pick_spec(wv, layer)],
        out_specs=(_const_spec((rows, d)), _const_spec((rows, d))),
        compiler_params=_params(),
        name="memory_kv",
    )(mem2d, wk, wv)


def _attn_kernel(x_ref, k_ref, v_ref, wq_hbm, wo_hbm, g_ref, b_ref, wr_ref, br_ref,
                 o_ref, op_ref, rt_ref, rg_ref, cnt_ref, o_scr, carry_scr, wq_ref, wo_ref, stage, sems,
                 *, layer):
    tm, d = x_ref.shape
    hd_dim = d // N_XHEADS
    sub = tm // SUB_TILES

    @pl.when(pl.program_id(0) == 0)
    def _():
        carry_scr[...] = jnp.zeros_like(carry_scr)
        _load_cast(wq_hbm.at[layer], wq_ref, stage, sems)
        _load_cast(wo_hbm.at[layer], wo_ref, stage, sems)

    row_slices = [slice(st * sub, (st + 1) * sub) for st in range(SUB_TILES)]
    qs = [_dot(x_ref[rs, :].astype(BF16), wq_ref[...]) * (1.0 / math.sqrt(hd_dim)) for rs in row_slices]
    for rs, q in zip(row_slices, qs):
        for hd in range(N_XHEADS):
            hs = slice(hd * hd_dim, (hd + 1) * hd_dim)
            s = lax.dot_general(q[:, hs].astype(BF16), k_ref[:, hs], _NT, preferred_element_type=F32)
            p = jnp.exp(s - jnp.max(s, axis=-1, keepdims=True))
            p = p * (1.0 / jnp.sum(p, axis=-1, keepdims=True))
            o_scr[rs, hs] = _dot(p.astype(BF16), v_ref[:, hs]).astype(BF16)
    xas = [_dot(o_scr[rs, :], wo_ref[...]) for rs in row_slices]
    for rs, xa in zip(row_slices, xas):
        out = _layer_norm(ALPHA * x_ref[rs, :] + xa, g_ref[...], b_ref[...])
        o_ref[rs, :] = out
        op_ref[rs, :] = _pack_halves(out)
        table, gates = _route_rows(out, wr_ref, br_ref, carry_scr)
        rt_ref[:, rs] = table
        rg_ref[rs, :] = gates
    cnt_ref[...] = carry_scr[...].astype(I32)


def _cross_attn(x, seq, layer, k, v, wq, wo, ln_g, ln_b, ln_row, router_w, router_b):
    n, d = x.shape
    m = k.shape[1]
    tiles_per_seq = seq // TM
    kvspec = pl.BlockSpec((None, m, d), lambda i: (i // tiles_per_seq, 0, 0))
    out, packed, table, gates, cnt = pl.pallas_call(
        functools.partial(_attn_kernel, layer=layer),
        out_shape=(jax.ShapeDtypeStruct((n, d), F32), jax.ShapeDtypeStruct((n, d // 2), I32),
                   jax.ShapeDtypeStruct((ROUTE_ROWS, n), F32), jax.ShapeDtypeStruct((n, TOP_K), F32),
                   jax.ShapeDtypeStruct((N_EXPERTS, LANES), I32)),
        grid=(n // TM,),
        in_specs=[
            pl.BlockSpec((TM, d), lambda i: (i, 0)),
            kvspec, kvspec,
            _HBM, _HBM,
            _pick_spec(ln_g, ln_row), _pick_spec(ln_b, ln_row),
            _pick_spec(router_w, layer), _pick_spec(router_b, layer),
        ],
        out_specs=(pl.BlockSpec((TM, d), lambda i: (i, 0)), pl.BlockSpec((TM, d // 2), lambda i: (i, 0)),
                   pl.BlockSpec((ROUTE_ROWS, TM), lambda i: (0, i)), pl.BlockSpec((TM, TOP_K), lambda i: (i, 0)),
                   _const_spec((N_EXPERTS, LANES))),
        scratch_shapes=[pltpu.VMEM((TM, d), BF16), pltpu.VMEM((N_EXPERTS, LANES), F32),
                        pltpu.VMEM((d, d), BF16), pltpu.VMEM((d, d), BF16)] + _stage_scratch(d),
        compiler_params=_params(),
        name="cross_attn",
    )(x, k, v, wq, wo, ln_g, ln_b, router_w, router_b)
    return out, packed, table, gates, cnt[:, 0]


def _route_rows(x, w_ref, bias_ref, carry_scr):
    tm = x.shape[0]
    neg = -jnp.inf
    logits = _dot(x.astype(BF16), w_ref[...]) + bias_ref[...]
    lt = jnp.transpose(logits)

    def first_argmax(vals):
        rows = lax.broadcasted_iota(I32, vals.shape, 0).astype(F32)
        mx = jnp.max(vals, axis=0, keepdims=True)
        idx = jnp.min(jnp.where(vals == mx, rows, float(vals.shape[0])), axis=0, keepdims=True)
        return mx, idx, rows

    gl = lt[0:N_GROUPS, :]
    gmax, g_sel, _ = first_argmax(gl)
    gate_g = 1.0 / jnp.sum(jnp.exp(gl - gmax), axis=0, keepdims=True)

    el = lt[EXPERT_ROW0:EXPERT_ROW0 + EXPERTS_PER_GROUP, :]
    for g in range(1, N_GROUPS):
        lo = EXPERT_ROW0 + g * EXPERTS_PER_GROUP
        el = jnp.where(g_sel == float(g), lt[lo:lo + EXPERTS_PER_GROUP, :], el)
    m1, i1, erow = first_argmax(el)
    m2, i2, _ = first_argmax(jnp.where(erow == i1, neg, el))
    e21 = jnp.exp(m2 - m1)
    w1 = 1.0 / (1.0 + e21)
    w2 = e21 / (1.0 + e21)
    e1 = g_sel * EXPERTS_PER_GROUP + i1
    e2 = g_sel * EXPERTS_PER_GROUP + i2

    xrow = lax.broadcasted_iota(I32, (N_EXPERTS, tm), 0).astype(F32)
    oh1 = xrow == e1
    oh2 = xrow == e2
    oh = (oh1 | oh2).astype(BF16)
    r = lax.broadcasted_iota(I32, (tm, tm), 0)
    c = lax.broadcasted_iota(I32, (tm, tm), 1)
    before = _dot(oh, (r < c).astype(BF16)) + carry_scr[:, 0:1]
    rank1 = jnp.sum(jnp.where(oh1, before, 0.0), axis=0, keepdims=True)
    rank2 = jnp.sum(jnp.where(oh2, before, 0.0), axis=0, keepdims=True)
    carry_scr[...] += jnp.sum(oh.astype(F32), axis=1, keepdims=True)

    trow = lax.broadcasted_iota(I32, (ROUTE_ROWS, tm), 0)
    table = jnp.where(trow == 0, e1, jnp.where(trow == 1, e2, jnp.where(trow == 2, rank1,
                                                                      jnp.where(trow == 3, rank2, 0.0))))
    grow = lax.broadcasted_iota(I32, (LANES, tm), 0)
    gates = jnp.transpose(jnp.where(grow == 0, gate_g * w1, jnp.where(grow == 1, gate_g * w2, 0.0)))[:, :TOP_K]
    return table, gates


def _router_weights(wr_g, br_g, wr_e, br_e):
    nl, d, _ = wr_g.shape
    gap = EXPERT_ROW0 - N_GROUPS
    w = jnp.concatenate([wr_g, jnp.zeros((nl, d, gap), F32),
                         jnp.transpose(wr_e, (0, 2, 1, 3)).reshape(nl, d, N_EXPERTS)], axis=2)
    w = jnp.pad(w, ((0, 0), (0, 0), (0, ROUTE_COLS - w.shape[2]))).astype(BF16)
    bias = jnp.concatenate([br_g, jnp.zeros((nl, gap), F32), br_e.reshape(nl, -1)], axis=1)
    bias = jnp.pad(bias, ((0, 0), (0, ROUTE_COLS - bias.shape[1])))
    return w, bias[:, None, :]


def _sc_worker_rows(rows):
    per_worker = rows // SC_WORKERS
    n_chunks = per_worker // SC_CHUNK
    assert per_worker * SC_WORKERS == rows and n_chunks * SC_CHUNK == per_worker and n_chunks % 2 == 0
    return per_worker, n_chunks


def _sc_gather_rows(table_hbm, out_hbm, idx_v, rows_v, gsem, wsem, base, n_chunks):
    def fetch(c, slot):
        off = pl.multiple_of(c * SC_CHUNK, SC_CHUNK)
        return pltpu.make_async_copy(table_hbm.at[idx_v.at[pl.ds(off, SC_CHUNK)]], rows_v.at[slot],
                                     gsem.at[slot])

    def put(c, slot):
        off = pl.multiple_of(c * SC_CHUNK, SC_CHUNK)
        return pltpu.make_async_copy(rows_v.at[slot], out_hbm.at[pl.ds(base + off, SC_CHUNK)], wsem.at[slot])

    fetch(0, 0).start()

    @pl.loop(0, n_chunks, step=2)
    def _(c0):
        for slot in range(2):
            c = c0 + slot

            @pl.when(c + 1 < n_chunks)
            def _():
                @pl.when(c >= 1)
                def _():
                    put(c - 1, 1 - slot).wait()
                fetch(c + 1, 1 - slot).start()

            fetch(c, slot).wait()
            put(c, slot).start()

    put(n_chunks - 2, 0).wait()
    put(n_chunks - 1, 1).wait()


def _sc_row_scratch(per_worker, d, dtype):
    return [pltpu.VMEM((per_worker,), I32), pltpu.VMEM((2, SC_CHUNK, d), dtype),
            pltpu.SemaphoreType.DMA((2,)), pltpu.SemaphoreType.DMA((2,))]


def _sc_gather(table, idx):
    b = idx.shape[0]
    d = table.shape[1]
    per_worker, n_chunks = _sc_worker_rows(b)
    mesh = plsc.VectorSubcoreMesh(core_axis_name="c", subcore_axis_name="s")

    @functools.partial(
        pl.kernel, mesh=mesh,
        out_type=jax.ShapeDtypeStruct((b, d), table.dtype),
        scratch_types=_sc_row_scratch(per_worker, d, table.dtype),
        name="sc_gather",
    )
    def gather(table_hbm, idx_hbm, out_hbm, idx_v, rows_v, gsem, wsem):
        base = (lax.axis_index("s") * SC_CORES + lax.axis_index("c")) * per_worker
        pltpu.sync_copy(idx_hbm.at[pl.ds(base, per_worker)], idx_v)
        _sc_gather_rows(table_hbm, out_hbm, idx_v, rows_v, gsem, wsem, base, n_chunks)

    return gather(table, idx)


def _sc_dispatch(table, dest_flat, rows):
    n, d = table.shape
    a = dest_flat.shape[0]
    lanes = SC_LANES
    per_worker, n_chunks = _sc_worker_rows(rows)
    assert a % lanes == 0 and per_worker % lanes == 0
    mesh = plsc.VectorSubcoreMesh(core_axis_name="c", subcore_axis_name="s")

    @functools.partial(
        pl.kernel, mesh=mesh,
        out_type=jax.ShapeDtypeStruct((rows, d), table.dtype),
        scratch_types=[pltpu.VMEM((a,), I32)] + _sc_row_scratch(per_worker, d, table.dtype),
        compiler_params=pltpu.CompilerParams(needs_layout_passes=False),
        name="sc_dispatch",
    )
    def dispatch(table_hbm, dest_hbm, out_hbm, dest_v, idx_v, rows_v, gsem, wsem):
        base = (lax.axis_index("s") * SC_CORES + lax.axis_index("c")) * per_worker
        pltpu.sync_copy(dest_hbm, dest_v)
        lane = lax.iota(I32, lanes)

        @pl.loop(0, per_worker // lanes)
        def _(i):
            idx_v[pl.ds(i * lanes, lanes)] = lax.rem(base + i * lanes + lane, n)

        for k in range(a // n):
            @plsc.parallel_loop(0, n // lanes, unroll=8)
            def _(i):
                local = dest_v[pl.ds(k * n + i * lanes, lanes)] - base
                mine = (local >= 0) & (local < per_worker)
                plsc.store_scatter(idx_v, [jnp.where(mine, local, 0)], i * lanes + lane, mask=mine)

        _sc_gather_rows(table_hbm, out_hbm, idx_v, rows_v, gsem, wsem, base, n_chunks)

    return dispatch(table, dest_flat)


def _expert_kernel(sched_ref, ni_ref, xs_hbm, w1_hbm, w3_hbm, w2_hbm, y_hbm,
                   x_buf, y_buf, w1_buf, w3_buf, w2_buf, w1_scr, w3_scr, w2_scr, wsems, xsems, ysems,
                   *, layer):
    i = pl.program_id(0)
    n_items = ni_ref[0]
    expert, wslot, run_start, next_expert = (sched_ref[r, i] for r in range(4))
    slot = i % 2

    def for_units(item, fn):
        for units in range(1, ITEM_UNITS + 1):
            @pl.when(sched_ref[5, item] == units)
            def _():
                fn(units * EXPERT_UNIT)

    def x_copy(item, s, rows):
        row0 = pl.multiple_of(sched_ref[4, item], EXPERT_UNIT)
        return pltpu.make_async_copy(xs_hbm.at[pl.ds(row0, rows)], x_buf.at[s, pl.ds(0, rows)], xsems.at[s])

    def y_copy(item, s, rows):
        row0 = pl.multiple_of(sched_ref[4, item], EXPERT_UNIT)
        return pltpu.make_async_copy(y_buf.at[s, pl.ds(0, rows)], y_hbm.at[pl.ds(row0, rows)], ysems.at[s])

    def fetch(e, s):
        return [pltpu.make_async_copy(w_hbm.at[layer, e], buf.at[s], wsems.at[s, j])
                for j, (w_hbm, buf) in enumerate(((w1_hbm, w1_buf), (w3_hbm, w3_buf), (w2_hbm, w2_buf)))]

    def mlp(rows):
        x_lo, x_hi = _unpack_halves(x_buf[slot, 0:rows, :])
        xb = jnp.concatenate([x_lo.astype(BF16), x_hi.astype(BF16)], axis=1)
        h1 = _dot(xb, w1_scr[...])
        h3 = _dot(xb, w3_scr[...])
        hid = h1 * (1.0 / (1.0 + jnp.exp(-h1))) * h3
        y_buf[slot, 0:rows, :] = _pack_halves(_dot(hid.astype(BF16), w2_scr[...]))
        y_copy(i, slot, rows).start()

    @pl.when(i < n_items)
    def _():
        @pl.when(i == 0)
        def _():
            for_units(0, lambda rows: x_copy(0, 0, rows).start())

        for_units(i, lambda rows: x_copy(i, slot, rows).wait())

        @pl.when(i + 1 < n_items)
        def _():
            for_units(i + 1, lambda rows: x_copy(i + 1, 1 - slot, rows).start())

        @pl.when(run_start == 1)
        def _():
            @pl.when(i == 0)
            def _():
                for c in fetch(expert, wslot):
                    c.start()

            for c in fetch(expert, wslot):
                c.wait()

            @pl.when(next_expert >= 0)
            def _():
                for c in fetch(next_expert, 1 - wslot):
                    c.start(priority=1)

            w1_scr[...] = w1_buf[wslot].astype(BF16)
            w3_scr[...] = w3_buf[wslot].astype(BF16)
            w2_scr[...] = w2_buf[wslot].astype(BF16)

        @pl.when(i >= 2)
        def _():
            for_units(i - 2, lambda rows: y_copy(i - 2, slot, rows).wait())

        for_units(i, mlp)

        @pl.when(i == n_items - 1)
        def _():
            @pl.when(i >= 1)
            def _():
                for_units(i - 1, lambda rows: y_copy(i - 1, 1 - slot, rows).wait())

            for_units(i, lambda rows: y_copy(i, slot, rows).wait())
            y_buf[0, 0:EXPERT_UNIT, :] = jnp.zeros((EXPERT_UNIT, y_buf.shape[2]), y_buf.dtype)
            first_free = (sched_ref[4, i] + sched_ref[5, i] * EXPERT_UNIT) // EXPERT_UNIT

            def zero_copy(u):
                return pltpu.make_async_copy(
                    y_buf.at[0, pl.ds(0, EXPERT_UNIT)],
                    y_hbm.at[pl.ds(pl.multiple_of(u * EXPERT_UNIT, EXPERT_UNIT), EXPERT_UNIT)], ysems.at[0])

            def start_zero(u, _):
                zero_copy(u).start()
                return 0

            def wait_zero(u, _):
                zero_copy(u).wait()
                return 0

            lax.fori_loop(first_free, y_hbm.shape[0] // EXPERT_UNIT, start_zero, 0)
            lax.fori_loop(first_free, y_hbm.shape[0] // EXPERT_UNIT, wait_zero, 0)


def _expert_mlp(xs, schedule, n_items, layer, w1, w3, w2):
    d, de = w1.shape[2], w1.shape[3]
    max_rows = ITEM_UNITS * EXPERT_UNIT
    return pl.pallas_call(
        functools.partial(_expert_kernel, layer=layer),
        out_shape=jax.ShapeDtypeStruct(xs.shape, I32),
        grid_spec=pltpu.PrefetchScalarGridSpec(
            num_scalar_prefetch=2,
            grid=(schedule.shape[1],),
            in_specs=[_HBM, _HBM, _HBM, _HBM],
            out_specs=_HBM,
            scratch_shapes=[pltpu.VMEM((2, max_rows, d // 2), I32), pltpu.VMEM((2, max_rows, d // 2), I32),
                            pltpu.VMEM((2, d, de), F32), pltpu.VMEM((2, d, de), F32), pltpu.VMEM((2, de, d), F32),
                            pltpu.VMEM((d, de), BF16), pltpu.VMEM((d, de), BF16), pltpu.VMEM((de, d), BF16),
                            pltpu.SemaphoreType.DMA((2, 3)), pltpu.SemaphoreType.DMA((2,)),
                            pltpu.SemaphoreType.DMA((2,))],
        ),
        compiler_params=_params(),
        name="expert_mlp",
    )(schedule, n_items, xs, w1, w3, w2)


def _combine_kernel(x_ref, y0_ref, y1_ref, gate_ref, g_ref, b_ref, o_ref):
    o_ref[...] = _moe_output(x_ref, y0_ref, y1_ref, gate_ref, g_ref, b_ref)


def _combine(x, yg, gates, ln_g, ln_b, ln_row):
    n, d = x.shape
    tiles = n // TM
    return pl.pallas_call(
        _combine_kernel,
        out_shape=jax.ShapeDtypeStruct((n, d), F32),
        grid=(tiles,),
        in_specs=[pl.BlockSpec((TM, d), lambda i: (i, 0)),
                  pl.BlockSpec((TM, d // 2), lambda i: (i, 0)),
                  pl.BlockSpec((TM, d // 2), lambda i: (i + tiles, 0)),
                  pl.BlockSpec((TM, TOP_K), lambda i: (i, 0)),
                  _pick_spec(ln_g, ln_row), _pick_spec(ln_b, ln_row)],
        out_specs=pl.BlockSpec((TM, d), lambda i: (i, 0)),
        compiler_params=_params(),
        name="combine",
    )(x, yg, yg, gates, ln_g, ln_b)


def _moe_experts(x_packed, table, counts, layer, w1, w3, w2):
    n = x_packed.shape[0]
    max_units = (n * TOP_K + N_EXPERTS * (EXPERT_UNIT - 1) + EXPERT_UNIT - 1) // EXPERT_UNIT
    max_items = (max_units + N_EXPERTS * (ITEM_UNITS - 1) + ITEM_UNITS - 1) // ITEM_UNITS
    experts = table[:TOP_K].astype(I32)
    ranks = table[TOP_K:2 * TOP_K].astype(I32)

    units_e = (counts + EXPERT_UNIT - 1) // EXPERT_UNIT
    units_start = jnp.cumsum(units_e) - units_e
    items_e = (units_e + ITEM_UNITS - 1) // ITEM_UNITS
    items_end = jnp.cumsum(items_e)
    n_items = items_end[-1:].astype(I32)
    item_ids = jnp.arange(max_items, dtype=I32)
    expert_ids = jnp.arange(N_EXPERTS, dtype=I32)
    item_expert = jnp.minimum(jnp.sum(items_end[None, :] <= item_ids[:, None], axis=1), N_EXPERTS - 1)
    later = (expert_ids[None, :] > expert_ids[:, None]) & (items_e[None, :] > 0)
    next_run = jnp.min(jnp.where(later, expert_ids[None, :], N_EXPERTS), axis=1)
    next_run = jnp.where(next_run == N_EXPERTS, -1, next_run)
    per_expert = jnp.stack([items_end - items_e, units_start, units_e, next_run], axis=1)
    mine = (item_expert[:, None] == expert_ids[None, :])[:, :, None]
    first_item, unit0, units, next_expert = jnp.sum(jnp.where(mine, per_expert[None], 0), axis=1).T
    within = item_ids - first_item
    item_row0 = (unit0 + ITEM_UNITS * within) * EXPERT_UNIT
    item_units = jnp.clip(units - ITEM_UNITS * within, 1, ITEM_UNITS)
    run_start_flag = ((within == 0) & (item_ids < n_items[0])).astype(I32)
    slot = (jnp.cumsum(run_start_flag) - 1) % 2
    schedule = jnp.stack([item_expert, slot, run_start_flag, next_expert, item_row0, item_units]).astype(I32)
    start_of = jnp.sum(jnp.where(experts[:, :, None] == expert_ids, units_start * EXPERT_UNIT, 0), axis=-1)
    dest = (start_of + ranks).astype(I32).reshape(-1)
    xs = _sc_dispatch(x_packed, dest, max_units * EXPERT_UNIT)
    y = _expert_mlp(xs, schedule, n_items, layer, w1, w3, w2)
    return _sc_gather(y, dest)


def kernel(x, mem, w_in_even, w_pool, pool_scale, ln_v_g, ln_v_b, w_spatial, b_spatial, w_out_even,
           w_in_odd, conv_w, conv_b, w_out_odd, wq_x, wk_x, wv_x, wo_x, ln_g, ln_b, wr_group,
           br_group, wr_expert, br_expert, w1, w3, w2):
    bsz, seq, d = x.shape
    assert seq % TM == 0 and d % LANES == 0
    mlen = mem.shape[1]
    mem2d = mem.reshape(bsz * mlen, d)
    ln_g = ln_g.reshape(DEPTH * 3, 1, d)
    ln_b = ln_b.reshape(DEPTH * 3, 1, d)
    w_pool = w_pool.astype(BF16)
    pool_scale, ln_v_g, ln_v_b, conv_b = (p[:, None, :] for p in (pool_scale, ln_v_g, ln_v_b, conv_b))
    b_spatial_t = jnp.swapaxes(b_spatial, 1, 2)
    conv_w_t = jnp.swapaxes(conv_w, 1, 2)
    router_w, router_b = _router_weights(wr_group, br_group, wr_expert, br_expert)

    h = x.reshape(bsz * seq, d)
    pending = None
    kv = _memory_kv(mem2d, wk_x, wv_x, 0)
    for l in range(DEPTH):
        i = l // 2
        k, v = (a.reshape(bsz, mlen, d) for a in kv)
        if l % 2 == 0:
            h = _even_mixer(h, pending, seq, i, w_in_even, w_pool, pool_scale, ln_v_g, ln_v_b,
                            w_spatial, b_spatial_t, w_out_even, ln_g, ln_b, 3 * l)
        else:
            h = _odd_mixer(h, pending, seq, i, w_in_odd, conv_w_t, conv_b, w_out_odd, ln_g, ln_b, 3 * l)
        h, hp, table, gates, counts = _cross_attn(h, seq, l, k, v, wq_x, wo_x,
                                                  ln_g, ln_b, 3 * l + 1, router_w, router_b)
        if l + 1 < DEPTH:
            kv = _memory_kv(mem2d, wk_x, wv_x, l + 1)
        yg = _moe_experts(hp, table, counts, l, w1, w3, w2)
        pending = (yg, gates, ln_g, ln_b, 3 * l + 2)
    return _combine(h, *pending).reshape(bsz, seq, d)
```

```python
import functools
import math

import jax
import jax.numpy as jnp
from jax import lax
from jax.experimental import pallas as pl
from jax.experimental.pallas import tpu as pltpu
from jax.experimental.pallas import tpu_sc as plsc

F32 = jnp.float32
BF16 = jnp.bfloat16
I32 = jnp.int32

POOL_WINDOWS = (2, 4, 8, 16)
assert all(w & (w - 1) == 0 for w in POOL_WINDOWS)
N_SG_HEADS = 4
CHUNK = 128
CONV_WIDTH = 3
N_XHEADS = 4
N_GROUPS = 4
EXPERTS_PER_GROUP = 8
N_EXPERTS = N_GROUPS * EXPERTS_PER_GROUP
TOP_K = 2
DEPTH = 4
ALPHA = (2.0 * DEPTH) ** 0.25
LN_EPS = 1e-5

LANES = 128
SC_CORES = 2
SC_WORKERS = 32
SC_LANES = 16
VMEM_LIMIT = 56 * 1024 * 1024

SC_CHUNK = 64
TM = 1024
SUB_TILES = 2
ROUTE_ROWS = 8
POOL_HALO = 16
CONV_HALO = 8
EXPERT_UNIT = 128
ITEM_UNITS = 4
ROUTE_COLS = 128
EXPERT_ROW0 = 8
STAGE_COLS = 512

_NT = (((1,), (1,)), ((), ()))


def _dot(a, b):
    return jnp.dot(a, b, preferred_element_type=F32)


def _layer_norm(y, g, b):
    mu = jnp.mean(y, axis=-1, keepdims=True)
    yc = y - mu
    var = jnp.mean(yc * yc, axis=-1, keepdims=True)
    return yc * lax.rsqrt(var + LN_EPS) * g + b


def _gelu_tanh(x):
    c = math.sqrt(2.0 / math.pi)
    return 0.5 * x * (1.0 + jnp.tanh(c * (x + 0.044715 * (x * x * x))))


def _pack_halves(v):
    c = v.shape[1] // 2
    lo = pltpu.bitcast(v[:, :c].astype(BF16).astype(F32), jnp.uint32)
    hi = pltpu.bitcast(v[:, c:].astype(BF16).astype(F32), jnp.uint32)
    return pltpu.bitcast((hi & jnp.uint32(0xFFFF0000)) | (lo >> 16), I32)


def _unpack_halves(w):
    u = pltpu.bitcast(w, jnp.uint32)
    return pltpu.bitcast(u << 16, F32), pltpu.bitcast(u & jnp.uint32(0xFFFF0000), F32)


def _load_cast(w_hbm, w_scr, stage, sems):
    chunks = w_scr.shape[1] // STAGE_COLS

    def chunk_copy(c):
        return pltpu.make_async_copy(w_hbm.at[:, pl.ds(c * STAGE_COLS, STAGE_COLS)], stage.at[c % 2],
                                     sems.at[c % 2])

    chunk_copy(0).start()
    for c in range(chunks):
        if c + 1 < chunks:
            chunk_copy(c + 1).start()
        chunk_copy(c).wait()
        w_scr[:, c * STAGE_COLS:(c + 1) * STAGE_COLS] = stage[c % 2].astype(BF16)


def _stage_scratch(rows):
    return [pltpu.VMEM((2, rows, STAGE_COLS), F32), pltpu.SemaphoreType.DMA((2,))]


_HBM = pl.BlockSpec(memory_space=pl.ANY)


def _const_spec(shape):
    nd = len(shape)
    return pl.BlockSpec(shape, lambda i: (0,) * nd)


def _pick_spec(stacked, index):
    rest = stacked.shape[1:]
    return pl.BlockSpec((None,) + rest, lambda i: (index,) + (0,) * len(rest))


def _params():
    return pltpu.CompilerParams(dimension_semantics=("arbitrary",), vmem_limit_bytes=VMEM_LIMIT)


def _moe_output(x_ref, y0_ref, y1_ref, gate_ref, g_ref, b_ref):
    gates = gate_ref[...]
    g0, g1 = gates[:, 0:1], gates[:, 1:2]
    y0_lo, y0_hi = _unpack_halves(y0_ref[...])
    y1_lo, y1_hi = _unpack_halves(y1_ref[...])
    ff = jnp.concatenate([g0 * y0_lo + g1 * y1_lo, g0 * y0_hi + g1 * y1_hi], axis=1)
    return _layer_norm(ALPHA * x_ref[...] + ff, g_ref[...], b_ref[...])


def _mixer_input(src, pending):
    if not pending:
        x_ref, xh_ref = src
        return x_ref[...], xh_ref[...]
    x_ref, xh_ref, y0_ref, y0h_ref, y1_ref, y1h_ref, gate_ref, gateh_ref, g_ref, b_ref = src
    return (_moe_output(x_ref, y0_ref, y1_ref, gate_ref, g_ref, b_ref),
            _moe_output(xh_ref, y0h_ref, y1h_ref, gateh_ref, g_ref, b_ref))


def _mixer_sources(x, pending, halo):
    n, d = x.shape
    tiles = n // TM
    halo_blocks = TM // halo

    def halo_index(i):
        return jnp.maximum(i * halo_blocks - 1, 0)

    specs = [pl.BlockSpec((TM, d), lambda i: (i, 0)), pl.BlockSpec((halo, d), lambda i: (halo_index(i), 0))]
    args = [x, x]
    if pending is not None:
        yg, gates, ln_g, ln_b, ln_row = pending
        specs += [pl.BlockSpec((TM, d // 2), lambda i: (i, 0)),
                  pl.BlockSpec((halo, d // 2), lambda i: (halo_index(i), 0)),
                  pl.BlockSpec((TM, d // 2), lambda i: (i + tiles, 0)),
                  pl.BlockSpec((halo, d // 2), lambda i: (halo_index(i) + tiles * halo_blocks, 0)),
                  pl.BlockSpec((TM, TOP_K), lambda i: (i, 0)),
                  pl.BlockSpec((halo, TOP_K), lambda i: (halo_index(i), 0)),
                  _pick_spec(ln_g, ln_row), _pick_spec(ln_b, ln_row)]
        args += [yg, yg, yg, yg, gates, gates, ln_g, ln_b]
    return specs, args


def _even_kernel(*refs, tiles_per_seq, layer, pending):
    n_src = 10 if pending else 2
    (win_hbm, wpool_ref, pscale_ref, lvg_ref, lvb_ref, ws_ref, bst_ref, wout_hbm, g_ref, b_ref, o_ref,
     a_scr, cat_scr, win_ref, wout_ref, stage, sems) = refs[n_src:]
    tm = o_ref.shape[0]
    d_pool = a_scr.shape[1]
    d_sg = lvg_ref.shape[1]
    pgd = d_pool // len(POOL_WINDOWS)
    hd_dim = d_sg // N_SG_HEADS
    seq_tile = pl.program_id(0) % tiles_per_seq

    @pl.when(pl.program_id(0) == 0)
    def _():
        _load_cast(win_hbm.at[layer], win_ref, stage, sems)
        _load_cast(wout_hbm.at[layer], wout_ref, stage, sems)

    x, xh = _mixer_input(refs[:n_src], pending)
    sub = tm // SUB_TILES

    ah = _dot(xh.astype(BF16), win_ref[:, :d_pool])
    a_scr[0:POOL_HALO, :] = jnp.where(seq_tile == 0, 0.0, ah)
    row = lax.broadcasted_iota(I32, (CHUNK, CHUNK), 0)
    col = lax.broadcasted_iota(I32, (CHUNK, CHUNK), 1)
    ws_masked = [jnp.where(row >= col, ws_ref[hd], 0.0).astype(BF16) for hd in range(N_SG_HEADS)]

    def in_proj(st):
        h = _dot(x[st * sub:(st + 1) * sub, :].astype(BF16), win_ref[...])
        a_scr[POOL_HALO + st * sub:POOL_HALO + (st + 1) * sub, :] = h[:, :d_pool]
        return h[:, d_pool:]

    def branches_out_proj(st, hz):
        base = st * sub
        pos = seq_tile * tm + base + lax.broadcasted_iota(I32, (sub, 1), 0)
        for g, w in enumerate(POOL_WINDOWS):
            cs = slice(g * pgd, (g + 1) * pgd)
            tok = a_scr[POOL_HALO + base:POOL_HALO + base + sub, cs]
            acc = a_scr[base:POOL_HALO + base + sub, cs]
            span = 1
            while span < w:
                acc = acc[span:, :] + acc[:-span, :]
                span *= 2
            acc = acc[acc.shape[0] - sub:, :]
            cnt = jnp.minimum(pos + 1, w).astype(F32)
            dev = acc * (1.0 / cnt) - tok
            yg = _dot(dev.astype(BF16), wpool_ref[g])
            cat_scr[base:base + sub, cs] = (yg * pscale_ref[:, cs]).astype(BF16)

        z = _gelu_tanh(hz)
        u = z[:, :d_sg]
        v = _layer_norm(z[:, d_sg:], lvg_ref[...], lvb_ref[...]).astype(BF16)
        for hd in range(N_SG_HEADS):
            hs = slice(hd * hd_dim, (hd + 1) * hd_dim)
            bcol = bst_ref[:, hd:hd + 1]
            for ck in range(sub // CHUNK):
                rs = slice(ck * CHUNK, (ck + 1) * CHUNK)
                sv = _dot(ws_masked[hd], v[rs, hs]) + bcol
                cat_scr[base + ck * CHUNK:base + (ck + 1) * CHUNK,
                        d_pool + hd * hd_dim:d_pool + (hd + 1) * hd_dim] = (u[rs, hs] * sv).astype(BF16)
        return _dot(cat_scr[base:base + sub, :], wout_ref[...])

    hzs = [in_proj(st) for st in range(SUB_TILES)]
    mixes = [branches_out_proj(st, hzs[st]) for st in range(SUB_TILES)]
    for st in range(SUB_TILES):
        rs = slice(st * sub, (st + 1) * sub)
        o_ref[rs, :] = _layer_norm(ALPHA * x[rs, :] + mixes[st], g_ref[...], b_ref[...])


def _even_mixer(x, pending, seq, layer, w_in, w_pool, pool_scale, ln_v_g, ln_v_b, w_spatial, b_spatial_t,
                w_out, ln_g, ln_b, ln_row):
    n, d = x.shape
    d_in = w_in.shape[2]
    d_pool = pool_scale.shape[2]
    d_sg = ln_v_g.shape[2]
    kern = functools.partial(_even_kernel, tiles_per_seq=seq // TM, layer=layer, pending=pending is not None)
    src_specs, src_args = _mixer_sources(x, pending, POOL_HALO)
    return pl.pallas_call(
        kern,
        out_shape=jax.ShapeDtypeStruct((n, d), F32),
        grid=(n // TM,),
        in_specs=src_specs + [
            _HBM,
            _pick_spec(w_pool, layer),
            _pick_spec(pool_scale, layer),
            _pick_spec(ln_v_g, layer),
            _pick_spec(ln_v_b, layer),
            _pick_spec(w_spatial, layer),
            _pick_spec(b_spatial_t, layer),
            _HBM,
            _pick_spec(ln_g, ln_row),
            _pick_spec(ln_b, ln_row),
        ],
        out_specs=pl.BlockSpec((TM, d), lambda i: (i, 0)),
        scratch_shapes=[pltpu.VMEM((POOL_HALO + TM, d_pool), F32), pltpu.VMEM((TM, d_pool + d_sg), BF16),
                        pltpu.VMEM((d, d_in), BF16), pltpu.VMEM((d_pool + d_sg, d), BF16)] + _stage_scratch(d),
        compiler_params=_params(),
        name="even_mixer",
    )(*src_args, w_in, w_pool, pool_scale, ln_v_g, ln_v_b, w_spatial, b_spatial_t, w_out, ln_g, ln_b)


def _odd_kernel(*refs, tiles_per_seq, layer, pending):
    n_src = 10 if pending else 2
    (win_hbm, cwt_ref, cb_ref, wout_hbm, g_ref, b_ref, o_ref, zc_scr,
     win_ref, wout_ref, stage, sems) = refs[n_src:]
    tm, d = o_ref.shape
    seq_tile = pl.program_id(0) % tiles_per_seq

    @pl.when(pl.program_id(0) == 0)
    def _():
        _load_cast(win_hbm.at[layer], win_ref, stage, sems)
        _load_cast(wout_hbm.at[layer], wout_ref, stage, sems)

    x, xh = _mixer_input(refs[:n_src], pending)
    hh = _dot(xh.astype(BF16), win_ref[:, d:])
    zc_scr[0:CONV_HALO, :] = jnp.where(seq_tile == 0, 0.0, hh[:, :d] * hh[:, d:])
    sub = tm // SUB_TILES

    def in_proj(st):
        xb = x[st * sub:(st + 1) * sub, :].astype(BF16)
        hc = _dot(xb, win_ref[:, d:2 * d])
        hz = _dot(xb, win_ref[:, 2 * d:])
        zc_scr[CONV_HALO + st * sub:CONV_HALO + (st + 1) * sub, :] = hc * hz
        return _dot(xb, win_ref[:, :d])

    def conv_out_proj(st, gate):
        conv = cb_ref[...]
        for j in range(CONV_WIDTH):
            off = CONV_HALO + st * sub - (CONV_WIDTH - 1) + j
            conv = conv + zc_scr[off:off + sub, :] * cwt_ref[j:j + 1, :]
        return _dot((gate * conv).astype(BF16), wout_ref[...])

    gates = [in_proj(st) for st in range(SUB_TILES)]
    ys = [conv_out_proj(st, gates[st]) for st in range(SUB_TILES)]
    for st in range(SUB_TILES):
        rs = slice(st * sub, (st + 1) * sub)
        o_ref[rs, :] = _layer_norm(ALPHA * x[rs, :] + ys[st], g_ref[...], b_ref[...])


def _odd_mixer(x, pending, seq, layer, w_in, conv_w_t, conv_b, w_out, ln_g, ln_b, ln_row):
    n, d = x.shape
    kern = functools.partial(_odd_kernel, tiles_per_seq=seq // TM, layer=layer, pending=pending is not None)
    src_specs, src_args = _mixer_sources(x, pending, CONV_HALO)
    return pl.pallas_call(
        kern,
        out_shape=jax.ShapeDtypeStruct((n, d), F32),
        grid=(n // TM,),
        in_specs=src_specs + [
            _HBM,
            _pick_spec(conv_w_t, layer),
            _pick_spec(conv_b, layer),
            _HBM,
            _pick_spec(ln_g, ln_row),
            _pick_spec(ln_b, ln_row),
        ],
        out_specs=pl.BlockSpec((TM, d), lambda i: (i, 0)),
        scratch_shapes=[pltpu.VMEM((CONV_HALO + TM, d), F32),
                        pltpu.VMEM(w_in.shape[1:], BF16), pltpu.VMEM(w_out.shape[1:], BF16)] + _stage_scratch(d),
        compiler_params=_params(),
        name="odd_mixer",
    )(*src_args, w_in, conv_w_t, conv_b, w_out, ln_g, ln_b)


def _kv_kernel(mem_ref, wk_ref, wv_ref, k_ref, v_ref):
    m = mem_ref[...].astype(BF16)
    k_ref[...] = _dot(m, wk_ref[...].astype(BF16)).astype(BF16)
    v_ref[...] = _dot(m, wv_ref[...].astype(BF16)).astype(BF16)


def _memory_kv(mem2d, wk, wv, layer):
    rows, d = mem2d.shape
    out = jax.ShapeDtypeStruct((rows, d), BF16)
    return pl.pallas_call(
        _kv_kernel,
        out_shape=(out, out),
        grid=(1,),
        in_specs=[_const_spec((rows, d)), _pick_spec(wk, layer), _pick_spec(wv, layer)],
        out_specs=(_const_spec((rows, d)), _const_spec((rows, d))),
        compiler_params=_params(),
        name="memory_kv",
    )(mem2d, wk, wv)


def _attn_kernel(x_ref, k_ref, v_ref, wq_hbm, wo_hbm, g_ref, b_ref, wr_ref, br_ref,
                 o_ref, op_ref, rt_ref, rg_ref, cnt_ref, o_scr, carry_scr, wq_ref, wo_ref, stage, sems,
                 *, layer):
    tm, d = x_ref.shape
    hd_dim = d // N_XHEADS
    sub = tm // SUB_TILES

    @pl.when(pl.program_id(0) == 0)
    def _():
        carry_scr[...] = jnp.zeros_like(carry_scr)
        _load_cast(wq_hbm.at[layer], wq_ref, stage, sems)
        _load_cast(wo_hbm.at[layer], wo_ref, stage, sems)

    row_slices = [slice(st * sub, (st + 1) * sub) for st in range(SUB_TILES)]
    qs = [_dot(x_ref[rs, :].astype(BF16), wq_ref[...]) * (1.0 / math.sqrt(hd_dim)) for rs in row_slices]
    for rs, q in zip(row_slices, qs):
        for hd in range(N_XHEADS):
            hs = slice(hd * hd_dim, (hd + 1) * hd_dim)
            s = lax.dot_general(q[:, hs].astype(BF16), k_ref[:, hs], _NT, preferred_element_type=F32)
            p = jnp.exp(s - jnp.max(s, axis=-1, keepdims=True))
            p = p * (1.0 / jnp.sum(p, axis=-1, keepdims=True))
            o_scr[rs, hs] = _dot(p.astype(BF16), v_ref[:, hs]).astype(BF16)
    xas = [_dot(o_scr[rs, :], wo_ref[...]) for rs in row_slices]
    for rs, xa in zip(row_slices, xas):
        out = _layer_norm(ALPHA * x_ref[rs, :] + xa, g_ref[...], b_ref[...])
        o_ref[rs, :] = out
        op_ref[rs, :] = _pack_halves(out)
        table, gates = _route_rows(out, wr_ref, br_ref, carry_scr)
        rt_ref[:, rs] = table
        rg_ref[rs, :] = gates
    cnt_ref[...] = carry_scr[...].astype(I32)


def _cross_attn(x, seq, layer, k, v, wq, wo, ln_g, ln_b, ln_row, router_w, router_b):
    n, d = x.shape
    m = k.shape[1]
    tiles_per_seq = seq // TM
    kvspec = pl.BlockSpec((None, m, d), lambda i: (i // tiles_per_seq, 0, 0))
    out, packed, table, gates, cnt = pl.pallas_call(
        functools.partial(_attn_kernel, layer=layer),
        out_shape=(jax.ShapeDtypeStruct((n, d), F32), jax.ShapeDtypeStruct((n, d // 2), I32),
                   jax.ShapeDtypeStruct((ROUTE_ROWS, n), F32), jax.ShapeDtypeStruct((n, TOP_K), F32),
                   jax.ShapeDtypeStruct((N_EXPERTS, LANES), I32)),
        grid=(n // TM,),
        in_specs=[
            pl.BlockSpec((TM, d), lambda i: (i, 0)),
            kvspec, kvspec,
            _HBM, _HBM,
            _pick_spec(ln_g, ln_row), _pick_spec(ln_b, ln_row),
            _pick_spec(router_w, layer), _pick_spec(router_b, layer),
        ],
        out_specs=(pl.BlockSpec((TM, d), lambda i: (i, 0)), pl.BlockSpec((TM, d // 2), lambda i: (i, 0)),
                   pl.BlockSpec((ROUTE_ROWS, TM), lambda i: (0, i)), pl.BlockSpec((TM, TOP_K), lambda i: (i, 0)),
                   _const_spec((N_EXPERTS, LANES))),
        scratch_shapes=[pltpu.VMEM((TM, d), BF16), pltpu.VMEM((N_EXPERTS, LANES), F32),
                        pltpu.VMEM((d, d), BF16), pltpu.VMEM((d, d), BF16)] + _stage_scratch(d),
        compiler_params=_params(),
        name="cross_attn",
    )(x, k, v, wq, wo, ln_g, ln_b, router_w, router_b)
    return out, packed, table, gates, cnt[:, 0]


def _route_rows(x, w_ref, bias_ref, carry_scr):
    tm = x.shape[0]
    neg = -jnp.inf
    logits = _dot(x.astype(BF16), w_ref[...]) + bias_ref[...]
    lt = jnp.transpose(logits)

    def first_argmax(vals):
        rows = lax.broadcasted_iota(I32, vals.shape, 0).astype(F32)
        mx = jnp.max(vals, axis=0, keepdims=True)
        idx = jnp.min(jnp.where(vals == mx, rows, float(vals.shape[0])), axis=0, keepdims=True)
        return mx, idx, rows

    gl = lt[0:N_GROUPS, :]
    gmax, g_sel, _ = first_argmax(gl)
    gate_g = 1.0 / jnp.sum(jnp.exp(gl - gmax), axis=0, keepdims=True)

    el = lt[EXPERT_ROW0:EXPERT_ROW0 + EXPERTS_PER_GROUP, :]
    for g in range(1, N_GROUPS):
        lo = EXPERT_ROW0 + g * EXPERTS_PER_GROUP
        el = jnp.where(g_sel == float(g), lt[lo:lo + EXPERTS_PER_GROUP, :], el)
    m1, i1, erow = first_argmax(el)
    m2, i2, _ = first_argmax(jnp.where(erow == i1, neg, el))
    e21 = jnp.exp(m2 - m1)
    w1 = 1.0 / (1.0 + e21)
    w2 = e21 / (1.0 + e21)
    e1 = g_sel * EXPERTS_PER_GROUP + i1
    e2 = g_sel * EXPERTS_PER_GROUP + i2

    xrow = lax.broadcasted_iota(I32, (N_EXPERTS, tm), 0).astype(F32)
    oh1 = xrow == e1
    oh2 = xrow == e2
    oh = (oh1 | oh2).astype(BF16)
    r = lax.broadcasted_iota(I32, (tm, tm), 0)
    c = lax.broadcasted_iota(I32, (tm, tm), 1)
    before = _dot(oh, (r < c).astype(BF16)) + carry_scr[:, 0:1]
    rank1 = jnp.sum(jnp.where(oh1, before, 0.0), axis=0, keepdims=True)
    rank2 = jnp.sum(jnp.where(oh2, before, 0.0), axis=0, keepdims=True)
    carry_scr[...] += jnp.sum(oh.astype(F32), axis=1, keepdims=True)

    trow = lax.broadcasted_iota(I32, (ROUTE_ROWS, tm), 0)
    table = jnp.where(trow == 0, e1, jnp.where(trow == 1, e2, jnp.where(trow == 2, rank1,
                                                                      jnp.where(trow == 3, rank2, 0.0))))
    grow = lax.broadcasted_iota(I32, (LANES, tm), 0)
    gates = jnp.transpose(jnp.where(grow == 0, gate_g * w1, jnp.where(grow == 1, gate_g * w2, 0.0)))[:, :TOP_K]
    return table, gates


def _router_weights(wr_g, br_g, wr_e, br_e):
    nl, d, _ = wr_g.shape
    gap = EXPERT_ROW0 - N_GROUPS
    w = jnp.concatenate([wr_g, jnp.zeros((nl, d, gap), F32),
                         jnp.transpose(wr_e, (0, 2, 1, 3)).reshape(nl, d, N_EXPERTS)], axis=2)
    w = jnp.pad(w, ((0, 0), (0, 0), (0, ROUTE_COLS - w.shape[2]))).astype(BF16)
    bias = jnp.concatenate([br_g, jnp.zeros((nl, gap), F32), br_e.reshape(nl, -1)], axis=1)
    bias = jnp.pad(bias, ((0, 0), (0, ROUTE_COLS - bias.shape[1])))
    return w, bias[:, None, :]


def _sc_worker_rows(rows):
    per_worker = rows // SC_WORKERS
    n_chunks = per_worker // SC_CHUNK
    assert per_worker * SC_WORKERS == rows and n_chunks * SC_CHUNK == per_worker and n_chunks % 2 == 0
    return per_worker, n_chunks


def _sc_gather_rows(table_hbm, out_hbm, idx_v, rows_v, gsem, wsem, base, n_chunks):
    def fetch(c, slot):
        off = pl.multiple_of(c * SC_CHUNK, SC_CHUNK)
        return pltpu.make_async_copy(table_hbm.at[idx_v.at[pl.ds(off, SC_CHUNK)]], rows_v.at[slot],
                                     gsem.at[slot])

    def put(c, slot):
        off = pl.multiple_of(c * SC_CHUNK, SC_CHUNK)
        return pltpu.make_async_copy(rows_v.at[slot], out_hbm.at[pl.ds(base + off, SC_CHUNK)], wsem.at[slot])

    fetch(0, 0).start()

    @pl.loop(0, n_chunks, step=2)
    def _(c0):
        for slot in range(2):
            c = c0 + slot

            @pl.when(c + 1 < n_chunks)
            def _():
                @pl.when(c >= 1)
                def _():
                    put(c - 1, 1 - slot).wait()
                fetch(c + 1, 1 - slot).start()

            fetch(c, slot).wait()
            put(c, slot).start()

    put(n_chunks - 2, 0).wait()
    put(n_chunks - 1, 1).wait()


def _sc_row_scratch(per_worker, d, dtype):
    return [pltpu.VMEM((per_worker,), I32), pltpu.VMEM((2, SC_CHUNK, d), dtype),
            pltpu.SemaphoreType.DMA((2,)), pltpu.SemaphoreType.DMA((2,))]


def _sc_gather(table, idx):
    b = idx.shape[0]
    d = table.shape[1]
    per_worker, n_chunks = _sc_worker_rows(b)
    mesh = plsc.VectorSubcoreMesh(core_axis_name="c", subcore_axis_name="s")

    @functools.partial(
        pl.kernel, mesh=mesh,
        out_type=jax.ShapeDtypeStruct((b, d), table.dtype),
        scratch_types=_sc_row_scratch(per_worker, d, table.dtype),
        name="sc_gather",
    )
    def gather(table_hbm, idx_hbm, out_hbm, idx_v, rows_v, gsem, wsem):
        base = (lax.axis_index("s") * SC_CORES + lax.axis_index("c")) * per_worker
        pltpu.sync_copy(idx_hbm.at[pl.ds(base, per_worker)], idx_v)
        _sc_gather_rows(table_hbm, out_hbm, idx_v, rows_v, gsem, wsem, base, n_chunks)

    return gather(table, idx)


def _sc_dispatch(table, dest_flat, rows):
    n, d = table.shape
    a = dest_flat.shape[0]
    lanes = SC_LANES
    per_worker, n_chunks = _sc_worker_rows(rows)
    assert a % lanes == 0 and per_worker % lanes == 0
    mesh = plsc.VectorSubcoreMesh(core_axis_name="c", subcore_axis_name="s")

    @functools.partial(
        pl.kernel, mesh=mesh,
        out_type=jax.ShapeDtypeStruct((rows, d), table.dtype),
        scratch_types=[pltpu.VMEM((a,), I32)] + _sc_row_scratch(per_worker, d, table.dtype),
        compiler_params=pltpu.CompilerParams(needs_layout_passes=False),
        name="sc_dispatch",
    )
    def dispatch(table_hbm, dest_hbm, out_hbm, dest_v, idx_v, rows_v, gsem, wsem):
        base = (lax.axis_index("s") * SC_CORES + lax.axis_index("c")) * per_worker
        pltpu.sync_copy(dest_hbm, dest_v)
        lane = lax.iota(I32, lanes)

        @pl.loop(0, per_worker // lanes)
        def _(i):
            idx_v[pl.ds(i * lanes, lanes)] = lax.rem(base + i * lanes + lane, n)

        for k in range(a // n):
            @plsc.parallel_loop(0, n // lanes, unroll=8)
            def _(i):
                local = dest_v[pl.ds(k * n + i * lanes, lanes)] - base
                mine = (local >= 0) & (local < per_worker)
                plsc.store_scatter(idx_v, [jnp.where(mine, local, 0)], i * lanes + lane, mask=mine)

        _sc_gather_rows(table_hbm, out_hbm, idx_v, rows_v, gsem, wsem, base, n_chunks)

    return dispatch(table, dest_flat)


def _expert_kernel(sched_ref, ni_ref, xs_hbm, w1_hbm, w3_hbm, w2_hbm, y_hbm,
                   x_buf, y_buf, w1_buf, w3_buf, w2_buf, w1_scr, w3_scr, w2_scr, wsems, xsems, ysems,
                   *, layer):
    i = pl.program_id(0)
    n_items = ni_ref[0]
    expert, wslot, run_start, next_expert = (sched_ref[r, i] for r in range(4))
    slot = i % 2

    def for_units(item, fn):
        for units in range(1, ITEM_UNITS + 1):
            @pl.when(sched_ref[5, item] == units)
            def _():
                fn(units * EXPERT_UNIT)

    def x_copy(item, s, rows):
        row0 = pl.multiple_of(sched_ref[4, item], EXPERT_UNIT)
        return pltpu.make_async_copy(xs_hbm.at[pl.ds(row0, rows)], x_buf.at[s, pl.ds(0, rows)], xsems.at[s])

    def y_copy(item, s, rows):
        row0 = pl.multiple_of(sched_ref[4, item], EXPERT_UNIT)
        return pltpu.make_async_copy(y_buf.at[s, pl.ds(0, rows)], y_hbm.at[pl.ds(row0, rows)], ysems.at[s])

    def fetch(e, s):
        return [pltpu.make_async_copy(w_hbm.at[layer, e], buf.at[s], wsems.at[s, j])
                for j, (w_hbm, buf) in enumerate(((w1_hbm, w1_buf), (w3_hbm, w3_buf), (w2_hbm, w2_buf)))]

    def mlp(rows):
        x_lo, x_hi = _unpack_halves(x_buf[slot, 0:rows, :])
        xb = jnp.concatenate([x_lo.astype(BF16), x_hi.astype(BF16)], axis=1)
        h1 = _dot(xb, w1_scr[...])
        h3 = _dot(xb, w3_scr[...])
        hid = h1 * (1.0 / (1.0 + jnp.exp(-h1))) * h3
        y_buf[slot, 0:rows, :] = _pack_halves(_dot(hid.astype(BF16), w2_scr[...]))
        y_copy(i, slot, rows).start(priority=1)

    @pl.when(i < n_items)
    def _():
        @pl.when(i == 0)
        def _():
            for_units(0, lambda rows: x_copy(0, 0, rows).start())

        for_units(i, lambda rows: x_copy(i, slot, rows).wait())

        @pl.when(i + 1 < n_items)
        def _():
            for_units(i + 1, lambda rows: x_copy(i + 1, 1 - slot, rows).start())

        @pl.when(run_start == 1)
        def _():
            @pl.when(i == 0)
            def _():
                for c in fetch(expert, wslot):
                    c.start()

            for c in fetch(expert, wslot):
                c.wait()

            @pl.when(next_expert >= 0)
            def _():
                for c in fetch(next_expert, 1 - wslot):
                    c.start(priority=1)

            w1_scr[...] = w1_buf[wslot].astype(BF16)
            w3_scr[...] = w3_buf[wslot].astype(BF16)
            w2_scr[...] = w2_buf[wslot].astype(BF16)

        @pl.when(i >= 2)
        def _():
            for_units(i - 2, lambda rows: y_copy(i - 2, slot, rows).wait())

        for_units(i, mlp)

        @pl.when(i == n_items - 1)
        def _():
            @pl.when(i >= 1)
            def _():
                for_units(i - 1, lambda rows: y_copy(i - 1, 1 - slot, rows).wait())

            for_units(i, lambda rows: y_copy(i, slot, rows).wait())
            y_buf[0, 0:EXPERT_UNIT, :] = jnp.zeros((EXPERT_UNIT, y_buf.shape[2]), y_buf.dtype)
            first_free = (sched_ref[4, i] + sched_ref[5, i] * EXPERT_UNIT) // EXPERT_UNIT

            def zero_copy(u):
                return pltpu.make_async_copy(
                    y_buf.at[0, pl.ds(0, EXPERT_UNIT)],
                    y_hbm.at[pl.ds(pl.multiple_of(u * EXPERT_UNIT, EXPERT_UNIT), EXPERT_UNIT)], ysems.at[0])

            def start_zero(u, _):
                zero_copy(u).start()
                return 0

            def wait_zero(u, _):
                zero_copy(u).wait()
                return 0

            lax.fori_loop(first_free, y_hbm.shape[0] // EXPERT_UNIT, start_zero, 0)
            lax.fori_loop(first_free, y_hbm.shape[0] // EXPERT_UNIT, wait_zero, 0)


def _expert_mlp(xs, schedule, n_items, layer, w1, w3, w2):
    d, de = w1.shape[2], w1.shape[3]
    max_rows = ITEM_UNITS * EXPERT_UNIT
    return pl.pallas_call(
        functools.partial(_expert_kernel, layer=layer),
        out_shape=jax.ShapeDtypeStruct(xs.shape, I32),
        grid_spec=pltpu.PrefetchScalarGridSpec(
            num_scalar_prefetch=2,
            grid=(schedule.shape[1],),
            in_specs=[_HBM, _HBM, _HBM, _HBM],
            out_specs=_HBM,
            scratch_shapes=[pltpu.VMEM((2, max_rows, d // 2), I32), pltpu.VMEM((2, max_rows, d // 2), I32),
                            pltpu.VMEM((2, d, de), F32), pltpu.VMEM((2, d, de), F32), pltpu.VMEM((2, de, d), F32),
                            pltpu.VMEM((d, de), BF16), pltpu.VMEM((d, de), BF16), pltpu.VMEM((de, d), BF16),
                            pltpu.SemaphoreType.DMA((2, 3)), pltpu.SemaphoreType.DMA((2,)),
                            pltpu.SemaphoreType.DMA((2,))],
        ),
        compiler_params=_params(),
        name="expert_mlp",
    )(schedule, n_items, xs, w1, w3, w2)


def _combine_kernel(x_ref, y0_ref, y1_ref, gate_ref, g_ref, b_ref, o_ref):
    o_ref[...] = _moe_output(x_ref, y0_ref, y1_ref, gate_ref, g_ref, b_ref)


def _combine(x, yg, gates, ln_g, ln_b, ln_row):
    n, d = x.shape
    tiles = n // TM
    return pl.pallas_call(
        _combine_kernel,
        out_shape=jax.ShapeDtypeStruct((n, d), F32),
        grid=(tiles,),
        in_specs=[pl.BlockSpec((TM, d), lambda i: (i, 0)),
                  pl.BlockSpec((TM, d // 2), lambda i: (i, 0)),
                  pl.BlockSpec((TM, d // 2), lambda i: (i + tiles, 0)),
                  pl.BlockSpec((TM, TOP_K), lambda i: (i, 0)),
                  _pick_spec(ln_g, ln_row), _pick_spec(ln_b, ln_row)],
        out_specs=pl.BlockSpec((TM, d), lambda i: (i, 0)),
        compiler_params=_params(),
        name="combine",
    )(x, yg, yg, gates, ln_g, ln_b)


def _moe_experts(x_packed, table, counts, layer, w1, w3, w2):
    n = x_packed.shape[0]
    max_units = (n * TOP_K + N_EXPERTS * (EXPERT_UNIT - 1) + EXPERT_UNIT - 1) // EXPERT_UNIT
    max_items = (max_units + N_EXPERTS * (ITEM_UNITS - 1) + ITEM_UNITS - 1) // ITEM_UNITS
    experts = table[:TOP_K].astype(I32)
    ranks = table[TOP_K:2 * TOP_K].astype(I32)

    units_e = (counts + EXPERT_UNIT - 1) // EXPERT_UNIT
    units_start = jnp.cumsum(units_e) - units_e
    items_e = (units_e + ITEM_UNITS - 1) // ITEM_UNITS
    items_end = jnp.cumsum(items_e)
    n_items = items_end[-1:].astype(I32)
    item_ids = jnp.arange(max_items, dtype=I32)
    expert_ids = jnp.arange(N_EXPERTS, dtype=I32)
    item_expert = jnp.minimum(jnp.sum(items_end[None, :] <= item_ids[:, None], axis=1), N_EXPERTS - 1)
    later = (expert_ids[None, :] > expert_ids[:, None]) & (items_e[None, :] > 0)
    next_run = jnp.min(jnp.where(later, expert_ids[None, :], N_EXPERTS), axis=1)
    next_run = jnp.where(next_run == N_EXPERTS, -1, next_run)
    per_expert = jnp.stack([items_end - items_e, units_start, units_e, next_run], axis=1)
    mine = (item_expert[:, None] == expert_ids[None, :])[:, :, None]
    first_item, unit0, units, next_expert = jnp.sum(jnp.where(mine, per_expert[None], 0), axis=1).T
    within = item_ids - first_item
    item_row0 = (unit0 + ITEM_UNITS * within) * EXPERT_UNIT
    item_units = jnp.clip(units - ITEM_UNITS * within, 1, ITEM_UNITS)
    run_start_flag = ((within == 0) & (item_ids < n_items[0])).astype(I32)
    slot = (jnp.cumsum(run_start_flag) - 1) % 2
    schedule = jnp.stack([item_expert, slot, run_start_flag, next_expert, item_row0, item_units]).astype(I32)
    start_of = jnp.sum(jnp.where(experts[:, :, None] == expert_ids, units_start * EXPERT_UNIT, 0), axis=-1)
    dest = (start_of + ranks).astype(I32).reshape(-1)
    xs = _sc_dispatch(x_packed, dest, max_units * EXPERT_UNIT)
    y = _expert_mlp(xs, schedule, n_items, layer, w1, w3, w2)
    return _sc_gather(y, dest)


def kernel(x, mem, w_in_even, w_pool, pool_scale, ln_v_g, ln_v_b, w_spatial, b_spatial, w_out_even,
           w_in_odd, conv_w, conv_b, w_out_odd, wq_x, wk_x, wv_x, wo_x, ln_g, ln_b, wr_group,
           br_group, wr_expert, br_expert, w1, w3, w2):
    bsz, seq, d = x.shape
    assert seq % TM == 0 and d % LANES == 0
    mlen = mem.shape[1]
    mem2d = mem.reshape(bsz * mlen, d)
    ln_g = ln_g.reshape(DEPTH * 3, 1, d)
    ln_b = ln_b.reshape(DEPTH * 3, 1, d)
    w_pool = w_pool.astype(BF16)
    pool_scale, ln_v_g, ln_v_b, conv_b = (p[:, None, :] for p in (pool_scale, ln_v_g, ln_v_b, conv_b))
    b_spatial_t = jnp.swapaxes(b_spatial, 1, 2)
    conv_w_t = jnp.swapaxes(conv_w, 1, 2)
    router_w, router_b = _router_weights(wr_group, br_group, wr_expert, br_expert)

    h = x.reshape(bsz * seq, d)
    pending = None
    kv = _memory_kv(mem2d, wk_x, wv_x, 0)
    for l in range(DEPTH):
        i = l // 2
        k, v = (a.reshape(bsz, mlen, d) for a in kv)
        if l % 2 == 0:
            h = _even_mixer(h, pending, seq, i, w_in_even, w_pool, pool_scale, ln_v_g, ln_v_b,
                            w_spatial, b_spatial_t, w_out_even, ln_g, ln_b, 3 * l)
        else:
            h = _odd_mixer(h, pending, seq, i, w_in_odd, conv_w_t, conv_b, w_out_odd, ln_g, ln_b, 3 * l)
        h, hp, table, gates, counts = _cross_attn(h, seq, l, k, v, wq_x, wo_x,
                                                  ln_g, ln_b, 3 * l + 1, router_w, router_b)
        if l + 1 < DEPTH:
            kv = _memory_kv(mem2d, wk_x, wv_x, l + 1)
        yg = _moe_experts(hp, table, counts, l, w1, w3, w2)
        pending = (yg, gates, ln_g, ln_b, 3 * l + 2)
    return _combine(h, *pending).reshape(bsz, seq, d)
```

```python
import functools
import math

import jax
import jax.numpy as jnp
from jax import lax
from jax.experimental import pallas as pl
from jax.experimental.pallas import tpu as pltpu
from jax.experimental.pallas import tpu_sc as plsc

F32 = jnp.float32
BF16 = jnp.bfloat16
I32 = jnp.int32

POOL_WINDOWS = (2, 4, 8, 16)
assert all(w & (w - 1) == 0 for w in POOL_WINDOWS)
N_SG_HEADS = 4
CHUNK = 128
CONV_WIDTH = 3
N_XHEADS = 4
N_GROUPS = 4
EXPERTS_PER_GROUP = 8
N_EXPERTS = N_GROUPS * EXPERTS_PER_GROUP
TOP_K = 2
DEPTH = 4
ALPHA = (2.0 * DEPTH) ** 0.25
LN_EPS = 1e-5

LANES = 128
SC_CORES = 2
SC_WORKERS = 32
SC_LANES = 16
VMEM_LIMIT = 56 * 1024 * 1024

SC_CHUNK = 64
TM = 1024
SUB_TILES = 2
ROUTE_ROWS = 8
POOL_HALO = 16
CONV_HALO = 8
EXPERT_UNIT = 128
ITEM_UNITS = 4
ROUTE_COLS = 128
EXPERT_ROW0 = 8
STAGE_COLS = 512

_NT = (((1,), (1,)), ((), ()))


def _dot(a, b):
    return jnp.dot(a, b, preferred_element_type=F32)


def _layer_norm(y, g, b):
    mu = jnp.mean(y, axis=-1, keepdims=True)
    yc = y - mu
    var = jnp.mean(yc * yc, axis=-1, keepdims=True)
    return yc * lax.rsqrt(var + LN_EPS) * g + b


def _gelu_tanh(x):
    c = math.sqrt(2.0 / math.pi)
    return 0.5 * x * (1.0 + jnp.tanh(c * (x + 0.044715 * (x * x * x))))


def _pack_halves(v):
    c = v.shape[1] // 2
    lo = pltpu.bitcast(v[:, :c].astype(BF16).astype(F32), jnp.uint32)
    hi = pltpu.bitcast(v[:, c:].astype(BF16).astype(F32), jnp.uint32)
    return pltpu.bitcast((hi & jnp.uint32(0xFFFF0000)) | (lo >> 16), I32)


def _unpack_halves(w):
    u = pltpu.bitcast(w, jnp.uint32)
    return pltpu.bitcast(u << 16, F32), pltpu.bitcast(u & jnp.uint32(0xFFFF0000), F32)


def _load_cast(w_hbm, w_scr, stage, sems):
    chunks = w_scr.shape[1] // STAGE_COLS

    def chunk_copy(c):
        return pltpu.make_async_copy(w_hbm.at[:, pl.ds(c * STAGE_COLS, STAGE_COLS)], stage.at[c % 2],
                                     sems.at[c % 2])

    chunk_copy(0).start()
    for c in range(chunks):
        if c + 1 < chunks:
            chunk_copy(c + 1).start()
        chunk_copy(c).wait()
        w_scr[:, c * STAGE_COLS:(c + 1) * STAGE_COLS] = stage[c % 2].astype(BF16)


def _stage_scratch(rows):
    return [pltpu.VMEM((2, rows, STAGE_COLS), F32), pltpu.SemaphoreType.DMA((2,))]


_HBM = pl.BlockSpec(memory_space=pl.ANY)


def _const_spec(shape):
    nd = len(shape)
    return pl.BlockSpec(shape, lambda i: (0,) * nd)


def _pick_spec(stacked, index):
    rest = stacked.shape[1:]
    return pl.BlockSpec((None,) + rest, lambda i: (index,) + (0,) * len(rest))


def _params():
    return pltpu.CompilerParams(dimension_semantics=("arbitrary",), vmem_limit_bytes=VMEM_LIMIT)


def _moe_output(x_ref, y0_ref, y1_ref, gate_ref, g_ref, b_ref):
    gates = gate_ref[...]
    g0, g1 = gates[:, 0:1], gates[:, 1:2]
    y0_lo, y0_hi = _unpack_halves(y0_ref[...])
    y1_lo, y1_hi = _unpack_halves(y1_ref[...])
    ff = jnp.concatenate([g0 * y0_lo + g1 * y1_lo, g0 * y0_hi + g1 * y1_hi], axis=1)
    return _layer_norm(ALPHA * x_ref[...] + ff, g_ref[...], b_ref[...])


def _mixer_input(src, pending):
    if not pending:
        return src[0][...]
    return _moe_output(*src)


def _mixer_sources(x, pending):
    n, d = x.shape
    tiles = n // TM
    specs = [pl.BlockSpec((TM, d), lambda i: (i, 0))]
    args = [x]
    if pending is not None:
        yg, gates, ln_g, ln_b, ln_row = pending
        specs += [pl.BlockSpec((TM, d // 2), lambda i: (i, 0)),
                  pl.BlockSpec((TM, d // 2), lambda i: (i + tiles, 0)),
                  pl.BlockSpec((TM, TOP_K), lambda i: (i, 0)),
                  _pick_spec(ln_g, ln_row), _pick_spec(ln_b, ln_row)]
        args += [yg, yg, gates, ln_g, ln_b]
    return specs, args


def _even_kernel(*refs, tiles_per_seq, layer, pending):
    n_src = 6 if pending else 1
    (win_hbm, wpool_ref, pscale_ref, lvg_ref, lvb_ref, ws_ref, bst_ref, wout_hbm, g_ref, b_ref, o_ref,
     a_scr, cat_scr, win_ref, wout_ref, stage, sems) = refs[n_src:]
    tm = o_ref.shape[0]
    d_pool = a_scr.shape[1]
    d_sg = lvg_ref.shape[1]
    pgd = d_pool // len(POOL_WINDOWS)
    hd_dim = d_sg // N_SG_HEADS
    seq_tile = pl.program_id(0) % tiles_per_seq

    @pl.when(pl.program_id(0) == 0)
    def _():
        _load_cast(win_hbm.at[layer], win_ref, stage, sems)
        _load_cast(wout_hbm.at[layer], wout_ref, stage, sems)

    @pl.when(seq_tile == 0)
    def _():
        a_scr[0:POOL_HALO, :] = jnp.zeros((POOL_HALO, d_pool), F32)

    x = _mixer_input(refs[:n_src], pending)
    sub = tm // SUB_TILES
    row = lax.broadcasted_iota(I32, (CHUNK, CHUNK), 0)
    col = lax.broadcasted_iota(I32, (CHUNK, CHUNK), 1)
    ws_masked = [jnp.where(row >= col, ws_ref[hd], 0.0).astype(BF16) for hd in range(N_SG_HEADS)]

    def in_proj(st):
        h = _dot(x[st * sub:(st + 1) * sub, :].astype(BF16), win_ref[...])
        a_scr[POOL_HALO + st * sub:POOL_HALO + (st + 1) * sub, :] = h[:, :d_pool]
        return h[:, d_pool:]

    def branches_out_proj(st, hz):
        base = st * sub
        pos = seq_tile * tm + base + lax.broadcasted_iota(I32, (sub, 1), 0)
        for g, w in enumerate(POOL_WINDOWS):
            cs = slice(g * pgd, (g + 1) * pgd)
            tok = a_scr[POOL_HALO + base:POOL_HALO + base + sub, cs]
            acc = a_scr[base:POOL_HALO + base + sub, cs]
            span = 1
            while span < w:
                acc = acc[span:, :] + acc[:-span, :]
                span *= 2
            acc = acc[acc.shape[0] - sub:, :]
            cnt = jnp.minimum(pos + 1, w).astype(F32)
            dev = acc * (1.0 / cnt) - tok
            yg = _dot(dev.astype(BF16), wpool_ref[g])
            cat_scr[base:base + sub, cs] = (yg * pscale_ref[:, cs]).astype(BF16)

        z = _gelu_tanh(hz)
        u = z[:, :d_sg]
        v = _layer_norm(z[:, d_sg:], lvg_ref[...], lvb_ref[...]).astype(BF16)
        for hd in range(N_SG_HEADS):
            hs = slice(hd * hd_dim, (hd + 1) * hd_dim)
            bcol = bst_ref[:, hd:hd + 1]
            for ck in range(sub // CHUNK):
                rs = slice(ck * CHUNK, (ck + 1) * CHUNK)
                sv = _dot(ws_masked[hd], v[rs, hs]) + bcol
                cat_scr[base + ck * CHUNK:base + (ck + 1) * CHUNK,
                        d_pool + hd * hd_dim:d_pool + (hd + 1) * hd_dim] = (u[rs, hs] * sv).astype(BF16)
        return _dot(cat_scr[base:base + sub, :], wout_ref[...])

    hzs = [in_proj(st) for st in range(SUB_TILES)]
    mixes = [branches_out_proj(st, hzs[st]) for st in range(SUB_TILES)]
    for st in range(SUB_TILES):
        rs = slice(st * sub, (st + 1) * sub)
        o_ref[rs, :] = _layer_norm(ALPHA * x[rs, :] + mixes[st], g_ref[...], b_ref[...])
    a_scr[0:POOL_HALO, :] = a_scr[tm:tm + POOL_HALO, :]


def _even_mixer(x, pending, seq, layer, w_in, w_pool, pool_scale, ln_v_g, ln_v_b, w_spatial, b_spatial_t,
                w_out, ln_g, ln_b, ln_row):
    n, d = x.shape
    d_in = w_in.shape[2]
    d_pool = pool_scale.shape[2]
    d_sg = ln_v_g.shape[2]
    kern = functools.partial(_even_kernel, tiles_per_seq=seq // TM, layer=layer, pending=pending is not None)
    src_specs, src_args = _mixer_sources(x, pending)
    return pl.pallas_call(
        kern,
        out_shape=jax.ShapeDtypeStruct((n, d), F32),
        grid=(n // TM,),
        in_specs=src_specs + [
            _HBM,
            _pick_spec(w_pool, layer),
            _pick_spec(pool_scale, layer),
            _pick_spec(ln_v_g, layer),
            _pick_spec(ln_v_b, layer),
            _pick_spec(w_spatial, layer),
            _pick_spec(b_spatial_t, layer),
            _HBM,
            _pick_spec(ln_g, ln_row),
            _pick_spec(ln_b, ln_row),
        ],
        out_specs=pl.BlockSpec((TM, d), lambda i: (i, 0)),
        scratch_shapes=[pltpu.VMEM((POOL_HALO + TM, d_pool), F32), pltpu.VMEM((TM, d_pool + d_sg), BF16),
                        pltpu.VMEM((d, d_in), BF16), pltpu.VMEM((d_pool + d_sg, d), BF16)] + _stage_scratch(d),
        compiler_params=_params(),
        name="even_mixer",
    )(*src_args, w_in, w_pool, pool_scale, ln_v_g, ln_v_b, w_spatial, b_spatial_t, w_out, ln_g, ln_b)


def _odd_kernel(*refs, tiles_per_seq, layer, pending):
    n_src = 6 if pending else 1
    (win_hbm, cwt_ref, cb_ref, wout_hbm, g_ref, b_ref, o_ref, zc_scr,
     win_ref, wout_ref, stage, sems) = refs[n_src:]
    tm, d = o_ref.shape
    seq_tile = pl.program_id(0) % tiles_per_seq

    @pl.when(pl.program_id(0) == 0)
    def _():
        _load_cast(win_hbm.at[layer], win_ref, stage, sems)
        _load_cast(wout_hbm.at[layer], wout_ref, stage, sems)

    @pl.when(seq_tile == 0)
    def _():
        zc_scr[0:CONV_HALO, :] = jnp.zeros((CONV_HALO, d), F32)

    x = _mixer_input(refs[:n_src], pending)
    sub = tm // SUB_TILES

    def in_proj(st):
        xb = x[st * sub:(st + 1) * sub, :].astype(BF16)
        hc = _dot(xb, win_ref[:, d:2 * d])
        hz = _dot(xb, win_ref[:, 2 * d:])
        zc_scr[CONV_HALO + st * sub:CONV_HALO + (st + 1) * sub, :] = hc * hz
        return _dot(xb, win_ref[:, :d])

    def conv_out_proj(st, gate):
        conv = cb_ref[...]
        for j in range(CONV_WIDTH):
            off = CONV_HALO + st * sub - (CONV_WIDTH - 1) + j
            conv = conv + zc_scr[off:off + sub, :] * cwt_ref[j:j + 1, :]
        return _dot((gate * conv).astype(BF16), wout_ref[...])

    gates = [in_proj(st) for st in range(SUB_TILES)]
    ys = [conv_out_proj(st, gates[st]) for st in range(SUB_TILES)]
    for st in range(SUB_TILES):
        rs = slice(st * sub, (st + 1) * sub)
        o_ref[rs, :] = _layer_norm(ALPHA * x[rs, :] + ys[st], g_ref[...], b_ref[...])
    zc_scr[0:CONV_HALO, :] = zc_scr[tm:tm + CONV_HALO, :]


def _odd_mixer(x, pending, seq, layer, w_in, conv_w_t, conv_b, w_out, ln_g, ln_b, ln_row):
    n, d = x.shape
    kern = functools.partial(_odd_kernel, tiles_per_seq=seq // TM, layer=layer, pending=pending is not None)
    src_specs, src_args = _mixer_sources(x, pending)
    return pl.pallas_call(
        kern,
        out_shape=jax.ShapeDtypeStruct((n, d), F32),
        grid=(n // TM,),
        in_specs=src_specs + [
            _HBM,
            _pick_spec(conv_w_t, layer),
            _pick_spec(conv_b, layer),
            _HBM,
            _pick_spec(ln_g, ln_row),
            _pick_spec(ln_b, ln_row),
        ],
        out_specs=pl.BlockSpec((TM, d), lambda i: (i, 0)),
        scratch_shapes=[pltpu.VMEM((CONV_HALO + TM, d), F32),
                        pltpu.VMEM(w_in.shape[1:], BF16), pltpu.VMEM(w_out.shape[1:], BF16)] + _stage_scratch(d),
        compiler_params=_params(),
        name="odd_mixer",
    )(*src_args, w_in, conv_w_t, conv_b, w_out, ln_g, ln_b)


def _kv_kernel(mem_ref, wk_ref, wv_ref, k_ref, v_ref):
    m = mem_ref[...].astype(BF16)
    k_ref[...] = _dot(m, wk_ref[...].astype(BF16)).astype(BF16)
    v_ref[...] = _dot(m, wv_ref[...].astype(BF16)).astype(BF16)


def _memory_kv(mem2d, wk, wv, layer):
    rows, d = mem2d.shape
    out = jax.ShapeDtypeStruct((rows, d), BF16)
    return pl.pallas_call(
        _kv_kernel,
        out_shape=(out, out),
        grid=(1,),
        in_specs=[_const_spec((rows, d)), _pick_spec(wk, layer), _pick_spec(wv, layer)],
        out_specs=(_const_spec((rows, d)), _const_spec((rows, d))),
        compiler_params=_params(),
        name="memory_kv",
    )(mem2d, wk, wv)


def _attn_kernel(x_ref, k_ref, v_ref, wq_hbm, wo_hbm, g_ref, b_ref, wr_ref, br_ref,
                 o_ref, op_ref, rt_ref, rg_ref, cnt_ref, o_scr, carry_scr, wq_ref, wo_ref, stage, sems,
                 *, layer):
    tm, d = x_ref.shape
    hd_dim = d // N_XHEADS
    sub = tm // SUB_TILES

    @pl.when(pl.program_id(0) == 0)
    def _():
        carry_scr[...] = jnp.zeros_like(carry_scr)
        _load_cast(wq_hbm.at[layer], wq_ref, stage, sems)
        _load_cast(wo_hbm.at[layer], wo_ref, stage, sems)

    row_slices = [slice(st * sub, (st + 1) * sub) for st in range(SUB_TILES)]
    qs = [_dot(x_ref[rs, :].astype(BF16), wq_ref[...]) * (1.0 / math.sqrt(hd_dim)) for rs in row_slices]
    for rs, q in zip(row_slices, qs):
        for hd in range(N_XHEADS):
            hs = slice(hd * hd_dim, (hd + 1) * hd_dim)
            s = lax.dot_general(q[:, hs].astype(BF16), k_ref[:, hs], _NT, preferred_element_type=F32)
            p = jnp.exp(s - jnp.max(s, axis=-1, keepdims=True))
            p = p * (1.0 / jnp.sum(p, axis=-1, keepdims=True))
            o_scr[rs, hs] = _dot(p.astype(BF16), v_ref[:, hs]).astype(BF16)
    xas = [_dot(o_scr[rs, :], wo_ref[...]) for rs in row_slices]
    for rs, xa in zip(row_slices, xas):
        out = _layer_norm(ALPHA * x_ref[rs, :] + xa, g_ref[...], b_ref[...])
        o_ref[rs, :] = out
        op_ref[rs, :] = _pack_halves(out)
        table, gates = _route_rows(out, wr_ref, br_ref, carry_scr)
        rt_ref[:, rs] = table
        rg_ref[rs, :] = gates
    cnt_ref[...] = carry_scr[...].astype(I32)


def _cross_attn(x, seq, layer, k, v, wq, wo, ln_g, ln_b, ln_row, router_w, router_b):
    n, d = x.shape
    m = k.shape[1]
    tiles_per_seq = seq // TM
    kvspec = pl.BlockSpec((None, m, d), lambda i: (i // tiles_per_seq, 0, 0))
    out, packed, table, gates, cnt = pl.pallas_call(
        functools.partial(_attn_kernel, layer=layer),
        out_shape=(jax.ShapeDtypeStruct((n, d), F32), jax.ShapeDtypeStruct((n, d // 2), I32),
                   jax.ShapeDtypeStruct((ROUTE_ROWS, n), F32), jax.ShapeDtypeStruct((n, TOP_K), F32),
                   jax.ShapeDtypeStruct((N_EXPERTS, LANES), I32)),
        grid=(n // TM,),
        in_specs=[
            pl.BlockSpec((TM, d), lambda i: (i, 0)),
            kvspec, kvspec,
            _HBM, _HBM,
            _pick_spec(ln_g, ln_row), _pick_spec(ln_b, ln_row),
            _pick_spec(router_w, layer), _pick_spec(router_b, layer),
        ],
        out_specs=(pl.BlockSpec((TM, d), lambda i: (i, 0)), pl.BlockSpec((TM, d // 2), lambda i: (i, 0)),
                   pl.BlockSpec((ROUTE_ROWS, TM), lambda i: (0, i)), pl.BlockSpec((TM, TOP_K), lambda i: (i, 0)),
                   _const_spec((N_EXPERTS, LANES))),
        scratch_shapes=[pltpu.VMEM((TM, d), BF16), pltpu.VMEM((N_EXPERTS, LANES), F32),
                        pltpu.VMEM((d, d), BF16), pltpu.VMEM((d, d), BF16)] + _stage_scratch(d),
        compiler_params=_params(),
        name="cross_attn",
    )(x, k, v, wq, wo, ln_g, ln_b, router_w, router_b)
    return out, packed, table, gates, cnt[:, 0]


def _route_rows(x, w_ref, bias_ref, carry_scr):
    tm = x.shape[0]
    neg = -jnp.inf
    logits = _dot(x.astype(BF16), w_ref[...]) + bias_ref[...]
    lt = jnp.transpose(logits)

    def first_argmax(vals):
        rows = lax.broadcasted_iota(I32, vals.shape, 0).astype(F32)
        mx = jnp.max(vals, axis=0, keepdims=True)
        idx = jnp.min(jnp.where(vals == mx, rows, float(vals.shape[0])), axis=0, keepdims=True)
        return mx, idx, rows

    gl = lt[0:N_GROUPS, :]
    gmax, g_sel, _ = first_argmax(gl)
    gate_g = 1.0 / jnp.sum(jnp.exp(gl - gmax), axis=0, keepdims=True)

    el = lt[EXPERT_ROW0:EXPERT_ROW0 + EXPERTS_PER_GROUP, :]
    for g in range(1, N_GROUPS):
        lo = EXPERT_ROW0 + g * EXPERTS_PER_GROUP
        el = jnp.where(g_sel == float(g), lt[lo:lo + EXPERTS_PER_GROUP, :], el)
    m1, i1, erow = first_argmax(el)
    m2, i2, _ = first_argmax(jnp.where(erow == i1, neg, el))
    e21 = jnp.exp(m2 - m1)
    w1 = 1.0 / (1.0 + e21)
    w2 = e21 / (1.0 + e21)
    e1 = g_sel * EXPERTS_PER_GROUP + i1
    e2 = g_sel * EXPERTS_PER_GROUP + i2

    xrow = lax.broadcasted_iota(I32, (N_EXPERTS, tm), 0).astype(F32)
    oh1 = xrow == e1
    oh2 = xrow == e2
    oh = (oh1 | oh2).astype(BF16)
    r = lax.broadcasted_iota(I32, (tm, tm), 0)
    c = lax.broadcasted_iota(I32, (tm, tm), 1)
    before = _dot(oh, (r < c).astype(BF16)) + carry_scr[:, 0:1]
    rank1 = jnp.sum(jnp.where(oh1, before, 0.0), axis=0, keepdims=True)
    rank2 = jnp.sum(jnp.where(oh2, before, 0.0), axis=0, keepdims=True)
    carry_scr[...] += jnp.sum(oh.astype(F32), axis=1, keepdims=True)

    trow = lax.broadcasted_iota(I32, (ROUTE_ROWS, tm), 0)
    table = jnp.where(trow == 0, e1, jnp.where(trow == 1, e2, jnp.where(trow == 2, rank1,
                                                                      jnp.where(trow == 3, rank2, 0.0))))
    grow = lax.broadcasted_iota(I32, (LANES, tm), 0)
    gates = jnp.transpose(jnp.where(grow == 0, gate_g * w1, jnp.where(grow == 1, gate_g * w2, 0.0)))[:, :TOP_K]
    return table, gates


def _router_weights(wr_g, br_g, wr_e, br_e):
    nl, d, _ = wr_g.shape
    gap = EXPERT_ROW0 - N_GROUPS
    w = jnp.concatenate([wr_g, jnp.zeros((nl, d, gap), F32),
                         jnp.transpose(wr_e, (0, 2, 1, 3)).reshape(nl, d, N_EXPERTS)], axis=2)
    w = jnp.pad(w, ((0, 0), (0, 0), (0, ROUTE_COLS - w.shape[2]))).astype(BF16)
    bias = jnp.concatenate([br_g, jnp.zeros((nl, gap), F32), br_e.reshape(nl, -1)], axis=1)
    bias = jnp.pad(bias, ((0, 0), (0, ROUTE_COLS - bias.shape[1])))
    return w, bias[:, None, :]


def _sc_worker_rows(rows):
    per_worker = rows // SC_WORKERS
    n_chunks = per_worker // SC_CHUNK
    assert per_worker * SC_WORKERS == rows and n_chunks * SC_CHUNK == per_worker and n_chunks % 2 == 0
    return per_worker, n_chunks


def _sc_gather_rows(table_hbm, out_hbm, idx_v, rows_v, gsem, wsem, base, n_chunks):
    def fetch(c, slot):
        off = pl.multiple_of(c * SC_CHUNK, SC_CHUNK)
        return pltpu.make_async_copy(table_hbm.at[idx_v.at[pl.ds(off, SC_CHUNK)]], rows_v.at[slot],
                                     gsem.at[slot])

    def put(c, slot):
        off = pl.multiple_of(c * SC_CHUNK, SC_CHUNK)
        return pltpu.make_async_copy(rows_v.at[slot], out_hbm.at[pl.ds(base + off, SC_CHUNK)], wsem.at[slot])

    fetch(0, 0).start()

    @pl.loop(0, n_chunks, step=2)
    def _(c0):
        for slot in range(2):
            c = c0 + slot

            @pl.when(c + 1 < n_chunks)
            def _():
                @pl.when(c >= 1)
                def _():
                    put(c - 1, 1 - slot).wait()
                fetch(c + 1, 1 - slot).start()

            fetch(c, slot).wait()
            put(c, slot).start()

    put(n_chunks - 2, 0).wait()
    put(n_chunks - 1, 1).wait()


def _sc_row_scratch(per_worker, d, dtype):
    return [pltpu.VMEM((per_worker,), I32), pltpu.VMEM((2, SC_CHUNK, d), dtype),
            pltpu.SemaphoreType.DMA((2,)), pltpu.SemaphoreType.DMA((2,))]


def _sc_gather(table, idx):
    b = idx.shape[0]
    d = table.shape[1]
    per_worker, n_chunks = _sc_worker_rows(b)
    mesh = plsc.VectorSubcoreMesh(core_axis_name="c", subcore_axis_name="s")

    @functools.partial(
        pl.kernel, mesh=mesh,
        out_type=jax.ShapeDtypeStruct((b, d), table.dtype),
        scratch_types=_sc_row_scratch(per_worker, d, table.dtype),
        name="sc_gather",
    )
    def gather(table_hbm, idx_hbm, out_hbm, idx_v, rows_v, gsem, wsem):
        base = (lax.axis_index("s") * SC_CORES + lax.axis_index("c")) * per_worker
        pltpu.sync_copy(idx_hbm.at[pl.ds(base, per_worker)], idx_v)
        _sc_gather_rows(table_hbm, out_hbm, idx_v, rows_v, gsem, wsem, base, n_chunks)

    return gather(table, idx)


def _sc_dispatch(table, dest_flat, rows):
    n, d = table.shape
    a = dest_flat.shape[0]
    lanes = SC_LANES
    per_worker, n_chunks = _sc_worker_rows(rows)
    assert a % lanes == 0 and per_worker % lanes == 0
    mesh = plsc.VectorSubcoreMesh(core_axis_name="c", subcore_axis_name="s")

    @functools.partial(
        pl.kernel, mesh=mesh,
        out_type=jax.ShapeDtypeStruct((rows, d), table.dtype),
        scratch_types=[pltpu.VMEM((a,), I32)] + _sc_row_scratch(per_worker, d, table.dtype),
        compiler_params=pltpu.CompilerParams(needs_layout_passes=False),
        name="sc_dispatch",
    )
    def dispatch(table_hbm, dest_hbm, out_hbm, dest_v, idx_v, rows_v, gsem, wsem):
        base = (lax.axis_index("s") * SC_CORES + lax.axis_index("c")) * per_worker
        pltpu.sync_copy(dest_hbm, dest_v)
        lane = lax.iota(I32, lanes)

        @pl.loop(0, per_worker // lanes)
        def _(i):
            idx_v[pl.ds(i * lanes, lanes)] = lax.rem(base + i * lanes + lane, n)

        for k in range(a // n):
            @plsc.parallel_loop(0, n // lanes, unroll=8)
            def _(i):
                local = dest_v[pl.ds(k * n + i * lanes, lanes)] - base
                mine = (local >= 0) & (local < per_worker)
                plsc.store_scatter(idx_v, [jnp.where(mine, local, 0)], i * lanes + lane, mask=mine)

        _sc_gather_rows(table_hbm, out_hbm, idx_v, rows_v, gsem, wsem, base, n_chunks)

    return dispatch(table, dest_flat)


def _expert_kernel(sched_ref, ni_ref, xs_hbm, w1_hbm, w3_hbm, w2_hbm, y_hbm,
                   x_buf, y_buf, w1_buf, w3_buf, w2_buf, w1_scr, w3_scr, w2_scr, wsems, xsems, ysems,
                   *, layer):
    i = pl.program_id(0)
    n_items = ni_ref[0]
    expert, wslot, run_start, next_expert = (sched_ref[r, i] for r in range(4))
    slot = i % 2

    def for_units(item, fn):
        for units in range(1, ITEM_UNITS + 1):
            @pl.when(sched_ref[5, item] == units)
            def _():
                fn(units * EXPERT_UNIT)

    def x_copy(item, s, rows):
        row0 = pl.multiple_of(sched_ref[4, item], EXPERT_UNIT)
        return pltpu.make_async_copy(xs_hbm.at[pl.ds(row0, rows)], x_buf.at[s, pl.ds(0, rows)], xsems.at[s])

    def y_copy(item, s, rows):
        row0 = pl.multiple_of(sched_ref[4, item], EXPERT_UNIT)
        return pltpu.make_async_copy(y_buf.at[s, pl.ds(0, rows)], y_hbm.at[pl.ds(row0, rows)], ysems.at[s])

    def fetch(e, s):
        return [pltpu.make_async_copy(w_hbm.at[layer, e], buf.at[s], wsems.at[s, j])
                for j, (w_hbm, buf) in enumerate(((w1_hbm, w1_buf), (w3_hbm, w3_buf), (w2_hbm, w2_buf)))]

    def mlp(rows):
        x_lo, x_hi = _unpack_halves(x_buf[slot, 0:rows, :])
        xb = jnp.concatenate([x_lo.astype(BF16), x_hi.astype(BF16)], axis=1)
        h1 = _dot(xb, w1_scr[...])
        h3 = _dot(xb, w3_scr[...])
        hid = h1 * (1.0 / (1.0 + jnp.exp(-h1))) * h3
        y_buf[slot, 0:rows, :] = _pack_halves(_dot(hid.astype(BF16), w2_scr[...]))
        y_copy(i, slot, rows).start()

    @pl.when(i < n_items)
    def _():
        @pl.when(i == 0)
        def _():
            for_units(0, lambda rows: x_copy(0, 0, rows).start())

        for_units(i, lambda rows: x_copy(i, slot, rows).wait())

        @pl.when(i + 1 < n_items)
        def _():
            for_units(i + 1, lambda rows: x_copy(i + 1, 1 - slot, rows).start())

        @pl.when(run_start == 1)
        def _():
            @pl.when(i == 0)
            def _():
                for c in fetch(expert, wslot):
                    c.start()

            for c in fetch(expert, wslot):
                c.wait()

            @pl.when(next_expert >= 0)
            def _():
                for c in fetch(next_expert, 1 - wslot):
                    c.start(priority=1)

            w1_scr[...] = w1_buf[wslot].astype(BF16)
            w3_scr[...] = w3_buf[wslot].astype(BF16)
            w2_scr[...] = w2_buf[wslot].astype(BF16)

        @pl.when(i >= 2)
        def _():
            for_units(i - 2, lambda rows: y_copy(i - 2, slot, rows).wait())

        for_units(i, mlp)

        @pl.when(i == n_items - 1)
        def _():
            @pl.when(i >= 1)
            def _():
                for_units(i - 1, lambda rows: y_copy(i - 1, 1 - slot, rows).wait())

            for_units(i, lambda rows: y_copy(i, slot, rows).wait())
            y_buf[0, 0:EXPERT_UNIT, :] = jnp.zeros((EXPERT_UNIT, y_buf.shape[2]), y_buf.dtype)
            first_free = (sched_ref[4, i] + sched_ref[5, i] * EXPERT_UNIT) // EXPERT_UNIT

            def zero_copy(u):
                return pltpu.make_async_copy(
                    y_buf.at[0, pl.ds(0, EXPERT_UNIT)],
                    y_hbm.at[pl.ds(pl.multiple_of(u * EXPERT_UNIT, EXPERT_UNIT), EXPERT_UNIT)], ysems.at[0])

            def start_zero(u, _):
                zero_copy(u).start()
                return 0

            def wait_zero(u, _):
                zero_copy(u).wait()
                return 0

            lax.fori_loop(first_free, y_hbm.shape[0] // EXPERT_UNIT, start_zero, 0)
            lax.fori_loop(first_free, y_hbm.shape[0] // EXPERT_UNIT, wait_zero, 0)


def _expert_mlp(xs, schedule, n_items, layer, w1, w3, w2):
    d, de = w1.shape[2], w1.shape[3]
    max_rows = ITEM_UNITS * EXPERT_UNIT
    return pl.pallas_call(
        functools.partial(_expert_kernel, layer=layer),
        out_shape=jax.ShapeDtypeStruct(xs.shape, I32),
        grid_spec=pltpu.PrefetchScalarGridSpec(
            num_scalar_prefetch=2,
            grid=(schedule.shape[1],),
            in_specs=[_HBM, _HBM, _HBM, _HBM],
            out_specs=_HBM,
            scratch_shapes=[pltpu.VMEM((2, max_rows, d // 2), I32), pltpu.VMEM((2, max_rows, d // 2), I32),
                            pltpu.VMEM((2, d, de), F32), pltpu.VMEM((2, d, de), F32), pltpu.VMEM((2, de, d), F32),
                            pltpu.VMEM((d, de), BF16), pltpu.VMEM((d, de), BF16), pltpu.VMEM((de, d), BF16),
                            pltpu.SemaphoreType.DMA((2, 3)), pltpu.SemaphoreType.DMA((2,)),
                            pltpu.SemaphoreType.DMA((2,))],
        ),
        compiler_params=_params(),
        name="expert_mlp",
    )(schedule, n_items, xs, w1, w3, w2)


def _combine_kernel(x_ref, y0_ref, y1_ref, gate_ref, g_ref, b_ref, o_ref):
    o_ref[...] = _moe_output(x_ref, y0_ref, y1_ref, gate_ref, g_ref, b_ref)


def _combine(x, yg, gates, ln_g, ln_b, ln_row):
    n, d = x.shape
    tiles = n // TM
    return pl.pallas_call(
        _combine_kernel,
        out_shape=jax.ShapeDtypeStruct((n, d), F32),
        grid=(tiles,),
        in_specs=[pl.BlockSpec((TM, d), lambda i: (i, 0)),
                  pl.BlockSpec((TM, d // 2), lambda i: (i, 0)),
                  pl.BlockSpec((TM, d // 2), lambda i: (i + tiles, 0)),
                  pl.BlockSpec((TM, TOP_K), lambda i: (i, 0)),
                  _pick_spec(ln_g, ln_row), _pick_spec(ln_b, ln_row)],
        out_specs=pl.BlockSpec((TM, d), lambda i: (i, 0)),
        compiler_params=_params(),
        name="combine",
    )(x, yg, yg, gates, ln_g, ln_b)


def _moe_experts(x_packed, table, counts, layer, w1, w3, w2):
    n = x_packed.shape[0]
    max_units = (n * TOP_K + N_EXPERTS * (EXPERT_UNIT - 1) + EXPERT_UNIT - 1) // EXPERT_UNIT
    max_items = (max_units + N_EXPERTS * (ITEM_UNITS - 1) + ITEM_UNITS - 1) // ITEM_UNITS
    experts = table[:TOP_K].astype(I32)
    ranks = table[TOP_K:2 * TOP_K].astype(I32)

    units_e = (counts + EXPERT_UNIT - 1) // EXPERT_UNIT
    units_start = jnp.cumsum(units_e) - units_e
    items_e = (units_e + ITEM_UNITS - 1) // ITEM_UNITS
    items_end = jnp.cumsum(items_e)
    n_items = items_end[-1:].astype(I32)
    item_ids = jnp.arange(max_items, dtype=I32)
    expert_ids = jnp.arange(N_EXPERTS, dtype=I32)
    item_expert = jnp.minimum(jnp.sum(items_end[None, :] <= item_ids[:, None], axis=1), N_EXPERTS - 1)
    later = (expert_ids[None, :] > expert_ids[:, None]) & (items_e[None, :] > 0)
    next_run = jnp.min(jnp.where(later, expert_ids[None, :], N_EXPERTS), axis=1)
    next_run = jnp.where(next_run == N_EXPERTS, -1, next_run)
    per_expert = jnp.stack([items_end - items_e, units_start, units_e, next_run], axis=1)
    mine = (item_expert[:, None] == expert_ids[None, :])[:, :, None]
    first_item, unit0, units, next_expert = jnp.sum(jnp.where(mine, per_expert[None], 0), axis=1).T
    within = item_ids - first_item
    item_row0 = (unit0 + ITEM_UNITS * within) * EXPERT_UNIT
    item_units = jnp.clip(units - ITEM_UNITS * within, 1, ITEM_UNITS)
    run_start_flag = ((within == 0) & (item_ids < n_items[0])).astype(I32)
    slot = (jnp.cumsum(run_start_flag) - 1) % 2
    schedule = jnp.stack([item_expert, slot, run_start_flag, next_expert, item_row0, item_units]).astype(I32)
    start_of = jnp.sum(jnp.where(experts[:, :, None] == expert_ids, units_start * EXPERT_UNIT, 0), axis=-1)
    dest = (start_of + ranks).astype(I32).reshape(-1)
    xs = _sc_dispatch(x_packed, dest, max_units * EXPERT_UNIT)
    y = _expert_mlp(xs, schedule, n_items, layer, w1, w3, w2)
    return _sc_gather(y, dest)


def kernel(x, mem, w_in_even, w_pool, pool_scale, ln_v_g, ln_v_b, w_spatial, b_spatial, w_out_even,
           w_in_odd, conv_w, conv_b, w_out_odd, wq_x, wk_x, wv_x, wo_x, ln_g, ln_b, wr_group,
           br_group, wr_expert, br_expert, w1, w3, w2):
    bsz, seq, d = x.shape
    assert seq % TM == 0 and d % LANES == 0
    mlen = mem.shape[1]
    mem2d = mem.reshape(bsz * mlen, d)
    ln_g = ln_g.reshape(DEPTH * 3, 1, d)
    ln_b = ln_b.reshape(DEPTH * 3, 1, d)
    w_pool = w_pool.astype(BF16)
    pool_scale, ln_v_g, ln_v_b, conv_b = (p[:, None, :] for p in (pool_scale, ln_v_g, ln_v_b, conv_b))
    b_spatial_t = jnp.swapaxes(b_spatial, 1, 2)
    conv_w_t = jnp.swapaxes(conv_w, 1, 2)
    router_w, router_b = _router_weights(wr_group, br_group, wr_expert, br_expert)

    h = x.reshape(bsz * seq, d)
    pending = None
    kv = _memory_kv(mem2d, wk_x, wv_x, 0)
    for l in range(DEPTH):
        i = l // 2
        k, v = (a.reshape(bsz, mlen, d) for a in kv)
        if l % 2 == 0:
            h = _even_mixer(h, pending, seq, i, w_in_even, w_pool, pool_scale, ln_v_g, ln_v_b,
                            w_spatial, b_spatial_t, w_out_even, ln_g, ln_b, 3 * l)
        else:
            h = _odd_mixer(h, pending, seq, i, w_in_odd, conv_w_t, conv_b, w_out_odd, ln_g, ln_b, 3 * l)
        h, hp, table, gates, counts = _cross_attn(h, seq, l, k, v, wq_x, wo_x,
                                                  ln_g, ln_b, 3 * l + 1, router_w, router_b)
        if l + 1 < DEPTH:
            kv = _memory_kv(mem2d, wk_x, wv_x, l + 1)
        yg = _moe_experts(hp, table, counts, l, w1, w3, w2)
        pending = (yg, gates, ln_g, ln_b, 3 * l + 2)
    return _combine(h, *pending).reshape(bsz, seq, d)
```

```python
import functools
import math

import jax
import jax.numpy as jnp
from jax import lax
from jax.experimental import pallas as pl
from jax.experimental.pallas import tpu as pltpu
from jax.experimental.pallas import tpu_sc as plsc

F32 = jnp.float32
BF16 = jnp.bfloat16
I32 = jnp.int32

POOL_WINDOWS = (2, 4, 8, 16)
assert all(w & (w - 1) == 0 for w in POOL_WINDOWS)
N_SG_HEADS = 4
CHUNK = 128
CONV_WIDTH = 3
N_XHEADS = 4
N_GROUPS = 4
EXPERTS_PER_GROUP = 8
N_EXPERTS = N_GROUPS * EXPERTS_PER_GROUP
TOP_K = 2
DEPTH = 4
ALPHA = (2.0 * DEPTH) ** 0.25
LN_EPS = 1e-5

LANES = 128
SC_CORES = 2
SC_WORKERS = 32
SC_LANES = 16
VMEM_LIMIT = 56 * 1024 * 1024

SC_CHUNK = 64
TM = 1024
SUB_TILES = 2
ROUTE_ROWS = 8
POOL_HALO = 16
CONV_HALO = 8
EXPERT_UNIT = 128
ITEM_UNITS = 4
ROUTE_COLS = 128
EXPERT_ROW0 = 8
STAGE_COLS = 512

_NT = (((1,), (1,)), ((), ()))


def _dot(a, b):
    return jnp.dot(a, b, preferred_element_type=F32)


def _layer_norm(y, g, b):
    mu = jnp.mean(y, axis=-1, keepdims=True)
    yc = y - mu
    var = jnp.mean(yc * yc, axis=-1, keepdims=True)
    return yc * lax.rsqrt(var + LN_EPS) * g + b


def _gelu_tanh(x):
    c = math.sqrt(2.0 / math.pi)
    return 0.5 * x * (1.0 + jnp.tanh(c * (x + 0.044715 * (x * x * x))))


def _pack_halves(v):
    c = v.shape[1] // 2
    lo = pltpu.bitcast(v[:, :c].astype(BF16).astype(F32), jnp.uint32)
    hi = pltpu.bitcast(v[:, c:].astype(BF16).astype(F32), jnp.uint32)
    return pltpu.bitcast((hi & jnp.uint32(0xFFFF0000)) | (lo >> 16), I32)


def _unpack_halves(w):
    u = pltpu.bitcast(w, jnp.uint32)
    return pltpu.bitcast(u << 16, F32), pltpu.bitcast(u & jnp.uint32(0xFFFF0000), F32)


def _load_cast(w_hbm, w_scr, stage, sems):
    chunks = w_scr.shape[1] // STAGE_COLS

    def chunk_copy(c):
        return pltpu.make_async_copy(w_hbm.at[:, pl.ds(c * STAGE_COLS, STAGE_COLS)], stage.at[c % 2],
                                     sems.at[c % 2])

    chunk_copy(0).start()
    for c in range(chunks):
        if c + 1 < chunks:
            chunk_copy(c + 1).start()
        chunk_copy(c).wait()
        w_scr[:, c * STAGE_COLS:(c + 1) * STAGE_COLS] = stage[c % 2].astype(BF16)


def _stage_scratch(rows):
    return [pltpu.VMEM((2, rows, STAGE_COLS), F32), pltpu.SemaphoreType.DMA((2,))]


_HBM = pl.BlockSpec(memory_space=pl.ANY)


def _const_spec(shape):
    nd = len(shape)
    return pl.BlockSpec(shape, lambda i: (0,) * nd)


def _pick_spec(stacked, index):
    rest = stacked.shape[1:]
    return pl.BlockSpec((None,) + rest, lambda i: (index,) + (0,) * len(rest))


def _params():
    return pltpu.CompilerParams(dimension_semantics=("arbitrary",), vmem_limit_bytes=VMEM_LIMIT)


def _moe_output(x_ref, y0_ref, y1_ref, gate_ref, g_ref, b_ref):
    gates = gate_ref[...]
    g0, g1 = gates[:, 0:1], gates[:, 1:2]
    y0_lo, y0_hi = _unpack_halves(y0_ref[...])
    y1_lo, y1_hi = _unpack_halves(y1_ref[...])
    ff = jnp.concatenate([g0 * y0_lo + g1 * y1_lo, g0 * y0_hi + g1 * y1_hi], axis=1)
    return _layer_norm(ALPHA * x_ref[...] + ff, g_ref[...], b_ref[...])


def _mixer_input(src, pending):
    if not pending:
        return src[0][...]
    return _moe_output(*src)


def _mixer_sources(x, pending):
    n, d = x.shape
    tiles = n // TM
    specs = [pl.BlockSpec((TM, d), lambda i: (i, 0))]
    args = [x]
    if pending is not None:
        yg, gates, ln_g, ln_b, ln_row = pending
        specs += [pl.BlockSpec((TM, d // 2), lambda i: (i, 0)),
                  pl.BlockSpec((TM, d // 2), lambda i: (i + tiles, 0)),
                  pl.BlockSpec((TM, TOP_K), lambda i: (i, 0)),
                  _pick_spec(ln_g, ln_row), _pick_spec(ln_b, ln_row)]
        args += [yg, yg, gates, ln_g, ln_b]
    return specs, args


def _even_kernel(*refs, tiles_per_seq, layer, pending):
    n_src = 6 if pending else 1
    (win_hbm, wpool_ref, pscale_ref, lvg_ref, lvb_ref, ws_ref, bst_ref, wout_hbm, g_ref, b_ref, o_ref,
     a_scr, cat_scr, win_ref, wout_ref, stage, sems) = refs[n_src:]
    tm = o_ref.shape[0]
    d_pool = a_scr.shape[1]
    d_sg = lvg_ref.shape[1]
    pgd = d_pool // len(POOL_WINDOWS)
    hd_dim = d_sg // N_SG_HEADS
    seq_tile = pl.program_id(0) % tiles_per_seq

    @pl.when(pl.program_id(0) == 0)
    def _():
        _load_cast(win_hbm.at[layer], win_ref, stage, sems)
        _load_cast(wout_hbm.at[layer], wout_ref, stage, sems)

    @pl.when(seq_tile == 0)
    def _():
        a_scr[0:POOL_HALO, :] = jnp.zeros((POOL_HALO, d_pool), F32)

    x = _mixer_input(refs[:n_src], pending)
    sub = tm // SUB_TILES
    row = lax.broadcasted_iota(I32, (CHUNK, CHUNK), 0)
    col = lax.broadcasted_iota(I32, (CHUNK, CHUNK), 1)
    ws_masked = [jnp.where(row >= col, ws_ref[hd], 0.0).astype(BF16) for hd in range(N_SG_HEADS)]

    def in_proj(st):
        h = _dot(x[st * sub:(st + 1) * sub, :].astype(BF16), win_ref[...])
        a_scr[POOL_HALO + st * sub:POOL_HALO + (st + 1) * sub, :] = h[:, :d_pool]
        return h[:, d_pool:]

    def branches_out_proj(st, hz):
        base = st * sub
        pos = seq_tile * tm + base + lax.broadcasted_iota(I32, (sub, 1), 0)
        for g, w in enumerate(POOL_WINDOWS):
            cs = slice(g * pgd, (g + 1) * pgd)
            tok = a_scr[POOL_HALO + base:POOL_HALO + base + sub, cs]
            acc = a_scr[base:POOL_HALO + base + sub, cs]
            span = 1
            while span < w:
                acc = acc[span:, :] + acc[:-span, :]
                span *= 2
            acc = acc[acc.shape[0] - sub:, :]
            cnt = jnp.minimum(pos + 1, w).astype(F32)
            dev = acc * (1.0 / cnt) - tok
            yg = _dot(dev.astype(BF16), wpool_ref[g])
            cat_scr[base:base + sub, cs] = (yg * pscale_ref[:, cs]).astype(BF16)

        z = _gelu_tanh(hz)
        u = z[:, :d_sg]
        v = _layer_norm(z[:, d_sg:], lvg_ref[...], lvb_ref[...]).astype(BF16)
        for hd in range(N_SG_HEADS):
            hs = slice(hd * hd_dim, (hd + 1) * hd_dim)
            bcol = bst_ref[:, hd:hd + 1]
            for ck in range(sub // CHUNK):
                rs = slice(ck * CHUNK, (ck + 1) * CHUNK)
                sv = _dot(ws_masked[hd], v[rs, hs]) + bcol
                cat_scr[base + ck * CHUNK:base + (ck + 1) * CHUNK,
                        d_pool + hd * hd_dim:d_pool + (hd + 1) * hd_dim] = (u[rs, hs] * sv).astype(BF16)
        return _dot(cat_scr[base:base + sub, :], wout_ref[...])

    hzs = [in_proj(st) for st in range(SUB_TILES)]
    mixes = [branches_out_proj(st, hzs[st]) for st in range(SUB_TILES)]
    for st in range(SUB_TILES):
        rs = slice(st * sub, (st + 1) * sub)
        o_ref[rs, :] = _layer_norm(ALPHA * x[rs, :] + mixes[st], g_ref[...], b_ref[...])
    a_scr[0:POOL_HALO, :] = a_scr[tm:tm + POOL_HALO, :]


def _even_mixer(x, pending, seq, layer, w_in, w_pool, pool_scale, ln_v_g, ln_v_b, w_spatial, b_spatial_t,
                w_out, ln_g, ln_b, ln_row):
    n, d = x.shape
    d_in = w_in.shape[2]
    d_pool = pool_scale.shape[2]
    d_sg = ln_v_g.shape[2]
    kern = functools.partial(_even_kernel, tiles_per_seq=seq // TM, layer=layer, pending=pending is not None)
    src_specs, src_args = _mixer_sources(x, pending)
    return pl.pallas_call(
        kern,
        out_shape=jax.ShapeDtypeStruct((n, d), F32),
        grid=(n // TM,),
        in_specs=src_specs + [
            _HBM,
            _pick_spec(w_pool, layer),
            _pick_spec(pool_scale, layer),
            _pick_spec(ln_v_g, layer),
            _pick_spec(ln_v_b, layer),
            _pick_spec(w_spatial, layer),
            _pick_spec(b_spatial_t, layer),
            _HBM,
            _pick_spec(ln_g, ln_row),
            _pick_spec(ln_b, ln_row),
        ],
        out_specs=pl.BlockSpec((TM, d), lambda i: (i, 0)),
        scratch_shapes=[pltpu.VMEM((POOL_HALO + TM, d_pool), F32), pltpu.VMEM((TM, d_pool + d_sg), BF16),
                        pltpu.VMEM((d, d_in), BF16), pltpu.VMEM((d_pool + d_sg, d), BF16)] + _stage_scratch(d),
        compiler_params=_params(),
        name="even_mixer",
    )(*src_args, w_in, w_pool, pool_scale, ln_v_g, ln_v_b, w_spatial, b_spatial_t, w_out, ln_g, ln_b)


def _odd_kernel(*refs, tiles_per_seq, layer, pending):
    n_src = 6 if pending else 1
    (win_hbm, cwt_ref, cb_ref, wout_hbm, g_ref, b_ref, o_ref, zc_scr,
     win_ref, wout_ref, stage, sems) = refs[n_src:]
    tm, d = o_ref.shape
    seq_tile = pl.program_id(0) % tiles_per_seq

    @pl.when(pl.program_id(0) == 0)
    def _():
        _load_cast(win_hbm.at[layer], win_ref, stage, sems)
        _load_cast(wout_hbm.at[layer], wout_ref, stage, sems)

    @pl.when(seq_tile == 0)
    def _():
        zc_scr[0:CONV_HALO, :] = jnp.zeros((CONV_HALO, d), F32)

    x = _mixer_input(refs[:n_src], pending)
    sub = tm // SUB_TILES

    def in_proj(st):
        xb = x[st * sub:(st + 1) * sub, :].astype(BF16)
        hc = _dot(xb, win_ref[:, d:2 * d])
        hz = _dot(xb, win_ref[:, 2 * d:])
        zc_scr[CONV_HALO + st * sub:CONV_HALO + (st + 1) * sub, :] = hc * hz
        return _dot(xb, win_ref[:, :d])

    def conv_out_proj(st, gate):
        conv = cb_ref[...]
        for j in range(CONV_WIDTH):
            off = CONV_HALO + st * sub - (CONV_WIDTH - 1) + j
            conv = conv + zc_scr[off:off + sub, :] * cwt_ref[j:j + 1, :]
        return _dot((gate * conv).astype(BF16), wout_ref[...])

    gates = [in_proj(st) for st in range(SUB_TILES)]
    ys = [conv_out_proj(st, gates[st]) for st in range(SUB_TILES)]
    for st in range(SUB_TILES):
        rs = slice(st * sub, (st + 1) * sub)
        o_ref[rs, :] = _layer_norm(ALPHA * x[rs, :] + ys[st], g_ref[...], b_ref[...])
    zc_scr[0:CONV_HALO, :] = zc_scr[tm:tm + CONV_HALO, :]


def _odd_mixer(x, pending, seq, layer, w_in, conv_w_t, conv_b, w_out, ln_g, ln_b, ln_row):
    n, d = x.shape
    kern = functools.partial(_odd_kernel, tiles_per_seq=seq // TM, layer=layer, pending=pending is not None)
    src_specs, src_args = _mixer_sources(x, pending)
    return pl.pallas_call(
        kern,
        out_shape=jax.ShapeDtypeStruct((n, d), F32),
        grid=(n // TM,),
        in_specs=src_specs + [
            _HBM,
            _pick_spec(conv_w_t, layer),
            _pick_spec(conv_b, layer),
            _HBM,
            _pick_spec(ln_g, ln_row),
            _pick_spec(ln_b, ln_row),
        ],
        out_specs=pl.BlockSpec((TM, d), lambda i: (i, 0)),
        scratch_shapes=[pltpu.VMEM((CONV_HALO + TM, d), F32),
                        pltpu.VMEM(w_in.shape[1:], BF16), pltpu.VMEM(w_out.shape[1:], BF16)] + _stage_scratch(d),
        compiler_params=_params(),
        name="odd_mixer",
    )(*src_args, w_in, conv_w_t, conv_b, w_out, ln_g, ln_b)


def _kv_kernel(mem_ref, wk_ref, wv_ref, k_ref, v_ref):
    m = mem_ref[...].astype(BF16)
    k_ref[...] = _dot(m, wk_ref[...].astype(BF16)).astype(BF16)
    v_ref[...] = _dot(m, wv_ref[...].astype(BF16)).astype(BF16)


def _memory_kv(mem2d, wk, wv, layer):
    rows, d = mem2d.shape
    out = jax.ShapeDtypeStruct((rows, d), BF16)
    return pl.pallas_call(
        _kv_kernel,
        out_shape=(out, out),
        grid=(1,),
        in_specs=[_const_spec((rows, d)), _pick_spec(wk, layer), _pick_spec(wv, layer)],
        out_specs=(_const_spec((rows, d)), _const_spec((rows, d))),
        compiler_params=_params(),
        name="memory_kv",
    )(mem2d, wk, wv)


def _attn_kernel(x_ref, k_ref, v_ref, wq_hbm, wo_hbm, g_ref, b_ref, wr_ref, br_ref,
                 o_ref, op_ref, rt_ref, rg_ref, cnt_ref, o_scr, carry_scr, wq_ref, wo_ref, stage, sems,
                 *, layer):
    tm, d = x_ref.shape
    hd_dim = d // N_XHEADS
    sub = tm // SUB_TILES

    @pl.when(pl.program_id(0) == 0)
    def _():
        carry_scr[...] = jnp.zeros_like(carry_scr)
        _load_cast(wq_hbm.at[layer], wq_ref, stage, sems)
        _load_cast(wo_hbm.at[layer], wo_ref, stage, sems)

    row_slices = [slice(st * sub, (st + 1) * sub) for st in range(SUB_TILES)]
    for rs in row_slices:
        q = _dot(x_ref[rs, :].astype(BF16), wq_ref[...]) * (1.0 / math.sqrt(hd_dim))
        for hd in range(N_XHEADS):
            hs = slice(hd * hd_dim, (hd + 1) * hd_dim)
            s = lax.dot_general(q[:, hs].astype(BF16), k_ref[:, hs], _NT, preferred_element_type=F32)
            p = jnp.exp(s - jnp.max(s, axis=-1, keepdims=True))
            p = p * (1.0 / jnp.sum(p, axis=-1, keepdims=True))
            o_scr[rs, hs] = _dot(p.astype(BF16), v_ref[:, hs]).astype(BF16)
    xas = [_dot(o_scr[rs, :], wo_ref[...]) for rs in row_slices]
    for rs, xa in zip(row_slices, xas):
        out = _layer_norm(ALPHA * x_ref[rs, :] + xa, g_ref[...], b_ref[...])
        o_ref[rs, :] = out
        op_ref[rs, :] = _pack_halves(out)
        table, gates = _route_rows(out, wr_ref, br_ref, carry_scr)
        rt_ref[:, rs] = table
        rg_ref[rs, :] = gates
    cnt_ref[...] = carry_scr[...].astype(I32)


def _cross_attn(x, seq, layer, k, v, wq, wo, ln_g, ln_b, ln_row, router_w, router_b):
    n, d = x.shape
    m = k.shape[1]
    tiles_per_seq = seq // TM
    kvspec = pl.BlockSpec((None, m, d), lambda i: (i // tiles_per_seq, 0, 0))
    out, packed, table, gates, cnt = pl.pallas_call(
        functools.partial(_attn_kernel, layer=layer),
        out_shape=(jax.ShapeDtypeStruct((n, d), F32), jax.ShapeDtypeStruct((n, d // 2), I32),
                   jax.ShapeDtypeStruct((ROUTE_ROWS, n), F32), jax.ShapeDtypeStruct((n, TOP_K), F32),
                   jax.ShapeDtypeStruct((N_EXPERTS, LANES), I32)),
        grid=(n // TM,),
        in_specs=[
            pl.BlockSpec((TM, d), lambda i: (i, 0)),
            kvspec, kvspec,
            _HBM, _HBM,
            _pick_spec(ln_g, ln_row), _pick_spec(ln_b, ln_row),
            _pick_spec(router_w, layer), _pick_spec(router_b, layer),
        ],
        out_specs=(pl.BlockSpec((TM, d), lambda i: (i, 0)), pl.BlockSpec((TM, d // 2), lambda i: (i, 0)),
                   pl.BlockSpec((ROUTE_ROWS, TM), lambda i: (0, i)), pl.BlockSpec((TM, TOP_K), lambda i: (i, 0)),
                   _const_spec((N_EXPERTS, LANES))),
        scratch_shapes=[pltpu.VMEM((TM, d), BF16), pltpu.VMEM((N_EXPERTS, LANES), F32),
                        pltpu.VMEM((d, d), BF16), pltpu.VMEM((d, d), BF16)] + _stage_scratch(d),
        compiler_params=_params(),
        name="cross_attn",
    )(x, k, v, wq, wo, ln_g, ln_b, router_w, router_b)
    return out, packed, table, gates, cnt[:, 0]


def _route_rows(x, w_ref, bias_ref, carry_scr):
    tm = x.shape[0]
    neg = -jnp.inf
    logits = _dot(x.astype(BF16), w_ref[...]) + bias_ref[...]
    lt = jnp.transpose(logits)

    def first_argmax(vals):
        rows = lax.broadcasted_iota(I32, vals.shape, 0).astype(F32)
        mx = jnp.max(vals, axis=0, keepdims=True)
        idx = jnp.min(jnp.where(vals == mx, rows, float(vals.shape[0])), axis=0, keepdims=True)
        return mx, idx, rows

    gl = lt[0:N_GROUPS, :]
    gmax, g_sel, _ = first_argmax(gl)
    gate_g = 1.0 / jnp.sum(jnp.exp(gl - gmax), axis=0, keepdims=True)

    el = lt[EXPERT_ROW0:EXPERT_ROW0 + EXPERTS_PER_GROUP, :]
    for g in range(1, N_GROUPS):
        lo = EXPERT_ROW0 + g * EXPERTS_PER_GROUP
        el = jnp.where(g_sel == float(g), lt[lo:lo + EXPERTS_PER_GROUP, :], el)
    m1, i1, erow = first_argmax(el)
    m2, i2, _ = first_argmax(jnp.where(erow == i1, neg, el))
    e21 = jnp.exp(m2 - m1)
    w1 = 1.0 / (1.0 + e21)
    w2 = e21 / (1.0 + e21)
    e1 = g_sel * EXPERTS_PER_GROUP + i1
    e2 = g_sel * EXPERTS_PER_GROUP + i2

    xrow = lax.broadcasted_iota(I32, (N_EXPERTS, tm), 0).astype(F32)
    oh1 = xrow == e1
    oh2 = xrow == e2
    oh = (oh1 | oh2).astype(BF16)
    r = lax.broadcasted_iota(I32, (tm, tm), 0)
    c = lax.broadcasted_iota(I32, (tm, tm), 1)
    before = _dot(oh, (r < c).astype(BF16)) + carry_scr[:, 0:1]
    rank1 = jnp.sum(jnp.where(oh1, before, 0.0), axis=0, keepdims=True)
    rank2 = jnp.sum(jnp.where(oh2, before, 0.0), axis=0, keepdims=True)
    carry_scr[...] += jnp.sum(oh.astype(F32), axis=1, keepdims=True)

    trow = lax.broadcasted_iota(I32, (ROUTE_ROWS, tm), 0)
    table = jnp.where(trow == 0, e1, jnp.where(trow == 1, e2, jnp.where(trow == 2, rank1,
                                                                      jnp.where(trow == 3, rank2, 0.0))))
    grow = lax.broadcasted_iota(I32, (LANES, tm), 0)
    gates = jnp.transpose(jnp.where(grow == 0, gate_g * w1, jnp.where(grow == 1, gate_g * w2, 0.0)))[:, :TOP_K]
    return table, gates


def _router_weights(wr_g, br_g, wr_e, br_e):
    nl, d, _ = wr_g.shape
    gap = EXPERT_ROW0 - N_GROUPS
    w = jnp.concatenate([wr_g, jnp.zeros((nl, d, gap), F32),
                         jnp.transpose(wr_e, (0, 2, 1, 3)).reshape(nl, d, N_EXPERTS)], axis=2)
    w = jnp.pad(w, ((0, 0), (0, 0), (0, ROUTE_COLS - w.shape[2]))).astype(BF16)
    bias = jnp.concatenate([br_g, jnp.zeros((nl, gap), F32), br_e.reshape(nl, -1)], axis=1)
    bias = jnp.pad(bias, ((0, 0), (0, ROUTE_COLS - bias.shape[1])))
    return w, bias[:, None, :]


def _sc_worker_rows(rows):
    per_worker = rows // SC_WORKERS
    n_chunks = per_worker // SC_CHUNK
    assert per_worker * SC_WORKERS == rows and n_chunks * SC_CHUNK == per_worker and n_chunks % 2 == 0
    return per_worker, n_chunks


def _sc_gather_rows(table_hbm, out_hbm, idx_v, rows_v, gsem, wsem, base, n_chunks):
    def fetch(c, slot):
        off = pl.multiple_of(c * SC_CHUNK, SC_CHUNK)
        return pltpu.make_async_copy(table_hbm.at[idx_v.at[pl.ds(off, SC_CHUNK)]], rows_v.at[slot],
                                     gsem.at[slot])

    def put(c, slot):
        off = pl.multiple_of(c * SC_CHUNK, SC_CHUNK)
        return pltpu.make_async_copy(rows_v.at[slot], out_hbm.at[pl.ds(base + off, SC_CHUNK)], wsem.at[slot])

    fetch(0, 0).start()

    @pl.loop(0, n_chunks, step=2)
    def _(c0):
        for slot in range(2):
            c = c0 + slot

            @pl.when(c + 1 < n_chunks)
            def _():
                @pl.when(c >= 1)
                def _():
                    put(c - 1, 1 - slot).wait()
                fetch(c + 1, 1 - slot).start()

            fetch(c, slot).wait()
            put(c, slot).start()

    put(n_chunks - 2, 0).wait()
    put(n_chunks - 1, 1).wait()


def _sc_row_scratch(per_worker, d, dtype):
    return [pltpu.VMEM((per_worker,), I32), pltpu.VMEM((2, SC_CHUNK, d), dtype),
            pltpu.SemaphoreType.DMA((2,)), pltpu.SemaphoreType.DMA((2,))]


def _sc_gather(table, idx):
    b = idx.shape[0]
    d = table.shape[1]
    per_worker, n_chunks = _sc_worker_rows(b)
    mesh = plsc.VectorSubcoreMesh(core_axis_name="c", subcore_axis_name="s")

    @functools.partial(
        pl.kernel, mesh=mesh,
        out_type=jax.ShapeDtypeStruct((b, d), table.dtype),
        scratch_types=_sc_row_scratch(per_worker, d, table.dtype),
        name="sc_gather",
    )
    def gather(table_hbm, idx_hbm, out_hbm, idx_v, rows_v, gsem, wsem):
        base = (lax.axis_index("s") * SC_CORES + lax.axis_index("c")) * per_worker
        pltpu.sync_copy(idx_hbm.at[pl.ds(base, per_worker)], idx_v)
        _sc_gather_rows(table_hbm, out_hbm, idx_v, rows_v, gsem, wsem, base, n_chunks)

    return gather(table, idx)


def _sc_dispatch(table, dest_flat, rows):
    n, d = table.shape
    a = dest_flat.shape[0]
    lanes = SC_LANES
    per_worker, n_chunks = _sc_worker_rows(rows)
    assert a % lanes == 0 and per_worker % lanes == 0
    mesh = plsc.VectorSubcoreMesh(core_axis_name="c", subcore_axis_name="s")

    @functools.partial(
        pl.kernel, mesh=mesh,
        out_type=jax.ShapeDtypeStruct((rows, d), table.dtype),
        scratch_types=[pltpu.VMEM((a,), I32)] + _sc_row_scratch(per_worker, d, table.dtype),
        compiler_params=pltpu.CompilerParams(needs_layout_passes=False),
        name="sc_dispatch",
    )
    def dispatch(table_hbm, dest_hbm, out_hbm, dest_v, idx_v, rows_v, gsem, wsem):
        base = (lax.axis_index("s") * SC_CORES + lax.axis_index("c")) * per_worker
        pltpu.sync_copy(dest_hbm, dest_v)
        lane = lax.iota(I32, lanes)

        @pl.loop(0, per_worker // lanes)
        def _(i):
            idx_v[pl.ds(i * lanes, lanes)] = lax.rem(base + i * lanes + lane, n)

        for k in range(a // n):
            @plsc.parallel_loop(0, n // lanes, unroll=8)
            def _(i):
                local = dest_v[pl.ds(k * n + i * lanes, lanes)] - base
                mine = (local >= 0) & (local < per_worker)
                plsc.store_scatter(idx_v, [jnp.where(mine, local, 0)], i * lanes + lane, mask=mine)

        _sc_gather_rows(table_hbm, out_hbm, idx_v, rows_v, gsem, wsem, base, n_chunks)

    return dispatch(table, dest_flat)


def _expert_kernel(sched_ref, ni_ref, xs_hbm, w1_hbm, w3_hbm, w2_hbm, y_hbm,
                   x_buf, y_buf, w1_buf, w3_buf, w2_buf, w1_scr, w3_scr, w2_scr, wsems, xsems, ysems,
                   *, layer):
    i = pl.program_id(0)
    n_items = ni_ref[0]
    expert, wslot, run_start, next_expert = (sched_ref[r, i] for r in range(4))
    slot = i % 2

    def for_units(item, fn):
        for units in range(1, ITEM_UNITS + 1):
            @pl.when(sched_ref[5, item] == units)
            def _():
                fn(units * EXPERT_UNIT)

    def x_copy(item, s, rows):
        row0 = pl.multiple_of(sched_ref[4, item], EXPERT_UNIT)
        return pltpu.make_async_copy(xs_hbm.at[pl.ds(row0, rows)], x_buf.at[s, pl.ds(0, rows)], xsems.at[s])

    def y_copy(item, s, rows):
        row0 = pl.multiple_of(sched_ref[4, item], EXPERT_UNIT)
        return pltpu.make_async_copy(y_buf.at[s, pl.ds(0, rows)], y_hbm.at[pl.ds(row0, rows)], ysems.at[s])

    def fetch(e, s):
        return [pltpu.make_async_copy(w_hbm.at[layer, e], buf.at[s], wsems.at[s, j])
                for j, (w_hbm, buf) in enumerate(((w1_hbm, w1_buf), (w3_hbm, w3_buf), (w2_hbm, w2_buf)))]

    def mlp(rows):
        x_lo, x_hi = _unpack_halves(x_buf[slot, 0:rows, :])
        xb = jnp.concatenate([x_lo.astype(BF16), x_hi.astype(BF16)], axis=1)
        h1 = _dot(xb, w1_scr[...])
        h3 = _dot(xb, w3_scr[...])
        hid = h1 * (1.0 / (1.0 + jnp.exp(-h1))) * h3
        y_buf[slot, 0:rows, :] = _pack_halves(_dot(hid.astype(BF16), w2_scr[...]))
        y_copy(i, slot, rows).start()

    @pl.when(i < n_items)
    def _():
        @pl.when(i == 0)
        def _():
            for_units(0, lambda rows: x_copy(0, 0, rows).start())

        for_units(i, lambda rows: x_copy(i, slot, rows).wait())

        @pl.when(i + 1 < n_items)
        def _():
            for_units(i + 1, lambda rows: x_copy(i + 1, 1 - slot, rows).start())

        @pl.when(run_start == 1)
        def _():
            @pl.when(i == 0)
            def _():
                for c in fetch(expert, wslot):
                    c.start()

            for c in fetch(expert, wslot):
                c.wait()

            @pl.when(next_expert >= 0)
            def _():
                for c in fetch(next_expert, 1 - wslot):
                    c.start(priority=1)

            w1_scr[...] = w1_buf[wslot].astype(BF16)
            w3_scr[...] = w3_buf[wslot].astype(BF16)
            w2_scr[...] = w2_buf[wslot].astype(BF16)

        @pl.when(i >= 2)
        def _():
            for_units(i - 2, lambda rows: y_copy(i - 2, slot, rows).wait())

        for_units(i, mlp)

        @pl.when(i == n_items - 1)
        def _():
            @pl.when(i >= 1)
            def _():
                for_units(i - 1, lambda rows: y_copy(i - 1, 1 - slot, rows).wait())

            for_units(i, lambda rows: y_copy(i, slot, rows).wait())
            y_buf[0, 0:EXPERT_UNIT, :] = jnp.zeros((EXPERT_UNIT, y_buf.shape[2]), y_buf.dtype)
            first_free = (sched_ref[4, i] + sched_ref[5, i] * EXPERT_UNIT) // EXPERT_UNIT

            def zero_copy(u):
                return pltpu.make_async_copy(
                    y_buf.at[0, pl.ds(0, EXPERT_UNIT)],
                    y_hbm.at[pl.ds(pl.multiple_of(u * EXPERT_UNIT, EXPERT_UNIT), EXPERT_UNIT)], ysems.at[0])

            def start_zero(u, _):
                zero_copy(u).start()
                return 0

            def wait_zero(u, _):
                zero_copy(u).wait()
                return 0

            lax.fori_loop(first_free, y_hbm.shape[0] // EXPERT_UNIT, start_zero, 0)
            lax.fori_loop(first_free, y_hbm.shape[0] // EXPERT_UNIT, wait_zero, 0)


def _expert_mlp(xs, schedule, n_items, layer, w1, w3, w2):
    d, de = w1.shape[2], w1.shape[3]
    max_rows = ITEM_UNITS * EXPERT_UNIT
    return pl.pallas_call(
        functools.partial(_expert_kernel, layer=layer),
        out_shape=jax.ShapeDtypeStruct(xs.shape, I32),
        grid_spec=pltpu.PrefetchScalarGridSpec(
            num_scalar_prefetch=2,
            grid=(schedule.shape[1],),
            in_specs=[_HBM, _HBM, _HBM, _HBM],
            out_specs=_HBM,
            scratch_shapes=[pltpu.VMEM((2, max_rows, d // 2), I32), pltpu.VMEM((2, max_rows, d // 2), I32),
                            pltpu.VMEM((2, d, de), F32), pltpu.VMEM((2, d, de), F32), pltpu.VMEM((2, de, d), F32),
                            pltpu.VMEM((d, de), BF16), pltpu.VMEM((d, de), BF16), pltpu.VMEM((de, d), BF16),
                            pltpu.SemaphoreType.DMA((2, 3)), pltpu.SemaphoreType.DMA((2,)),
                            pltpu.SemaphoreType.DMA((2,))],
        ),
        compiler_params=_params(),
        name="expert_mlp",
    )(schedule, n_items, xs, w1, w3, w2)


def _combine_kernel(x_ref, y0_ref, y1_ref, gate_ref, g_ref, b_ref, o_ref):
    o_ref[...] = _moe_output(x_ref, y0_ref, y1_ref, gate_ref, g_ref, b_ref)


def _combine(x, yg, gates, ln_g, ln_b, ln_row):
    n, d = x.shape
    tiles = n // TM
    return pl.pallas_call(
        _combine_kernel,
        out_shape=jax.ShapeDtypeStruct((n, d), F32),
        grid=(tiles,),
        in_specs=[pl.BlockSpec((TM, d), lambda i: (i, 0)),
                  pl.BlockSpec((TM, d // 2), lambda i: (i, 0)),
                  pl.BlockSpec((TM, d // 2), lambda i: (i + tiles, 0)),
                  pl.BlockSpec((TM, TOP_K), lambda i: (i, 0)),
                  _pick_spec(ln_g, ln_row), _pick_spec(ln_b, ln_row)],
        out_specs=pl.BlockSpec((TM, d), lambda i: (i, 0)),
        compiler_params=_params(),
        name="combine",
    )(x, yg, yg, gates, ln_g, ln_b)


def _moe_experts(x_packed, table, counts, layer, w1, w3, w2):
    n = x_packed.shape[0]
    max_units = (n * TOP_K + N_EXPERTS * (EXPERT_UNIT - 1) + EXPERT_UNIT - 1) // EXPERT_UNIT
    max_items = (max_units + N_EXPERTS * (ITEM_UNITS - 1) + ITEM_UNITS - 1) // ITEM_UNITS
    experts = table[:TOP_K].astype(I32)
    ranks = table[TOP_K:2 * TOP_K].astype(I32)

    units_e = (counts + EXPERT_UNIT - 1) // EXPERT_UNIT
    units_start = jnp.cumsum(units_e) - units_e
    items_e = (units_e + ITEM_UNITS - 1) // ITEM_UNITS
    items_end = jnp.cumsum(items_e)
    n_items = items_end[-1:].astype(I32)
    item_ids = jnp.arange(max_items, dtype=I32)
    expert_ids = jnp.arange(N_EXPERTS, dtype=I32)
    item_expert = jnp.minimum(jnp.sum(items_end[None, :] <= item_ids[:, None], axis=1), N_EXPERTS - 1)
    later = (expert_ids[None, :] > expert_ids[:, None]) & (items_e[None, :] > 0)
    next_run = jnp.min(jnp.where(later, expert_ids[None, :], N_EXPERTS), axis=1)
    next_run = jnp.where(next_run == N_EXPERTS, -1, next_run)
    per_expert = jnp.stack([items_end - items_e, units_start, units_e, next_run], axis=1)
    mine = (item_expert[:, None] == expert_ids[None, :])[:, :, None]
    first_item, unit0, units, next_expert = jnp.sum(jnp.where(mine, per_expert[None], 0), axis=1).T
    within = item_ids - first_item
    item_row0 = (unit0 + ITEM_UNITS * within) * EXPERT_UNIT
    item_units = jnp.clip(units - ITEM_UNITS * within, 1, ITEM_UNITS)
    run_start_flag = ((within == 0) & (item_ids < n_items[0])).astype(I32)
    slot = (jnp.cumsum(run_start_flag) - 1) % 2
    schedule = jnp.stack([item_expert, slot, run_start_flag, next_expert, item_row0, item_units]).astype(I32)
    start_of = jnp.sum(jnp.where(experts[:, :, None] == expert_ids, units_start * EXPERT_UNIT, 0), axis=-1)
    dest = (start_of + ranks).astype(I32).reshape(-1)
    xs = _sc_dispatch(x_packed, dest, max_units * EXPERT_UNIT)
    y = _expert_mlp(xs, schedule, n_items, layer, w1, w3, w2)
    return _sc_gather(y, dest)


def kernel(x, mem, w_in_even, w_pool, pool_scale, ln_v_g, ln_v_b, w_spatial, b_spatial, w_out_even,
           w_in_odd, conv_w, conv_b, w_out_odd, wq_x, wk_x, wv_x, wo_x, ln_g, ln_b, wr_group,
           br_group, wr_expert, br_expert, w1, w3, w2):
    bsz, seq, d = x.shape
    assert seq % TM == 0 and d % LANES == 0
    mlen = mem.shape[1]
    mem2d = mem.reshape(bsz * mlen, d)
    ln_g = ln_g.reshape(DEPTH * 3, 1, d)
    ln_b = ln_b.reshape(DEPTH * 3, 1, d)
    w_pool = w_pool.astype(BF16)
    pool_scale, ln_v_g, ln_v_b, conv_b = (p[:, None, :] for p in (pool_scale, ln_v_g, ln_v_b, conv_b))
    b_spatial_t = jnp.swapaxes(b_spatial, 1, 2)
    conv_w_t = jnp.swapaxes(conv_w, 1, 2)
    router_w, router_b = _router_weights(wr_group, br_group, wr_expert, br_expert)

    h = x.reshape(bsz * seq, d)
    pending = None
    kv = _memory_kv(mem2d, wk_x, wv_x, 0)
    for l in range(DEPTH):
        i = l // 2
        k, v = (a.reshape(bsz, mlen, d) for a in kv)
        if l % 2 == 0:
            h = _even_mixer(h, pending, seq, i, w_in_even, w_pool, pool_scale, ln_v_g, ln_v_b,
                            w_spatial, b_spatial_t, w_out_even, ln_g, ln_b, 3 * l)
        else:
            h = _odd_mixer(h, pending, seq, i, w_in_odd, conv_w_t, conv_b, w_out_odd, ln_g, ln_b, 3 * l)
        h, hp, table, gates, counts = _cross_attn(h, seq, l, k, v, wq_x, wo_x,
                                                  ln_g, ln_b, 3 * l + 1, router_w, router_b)
        if l + 1 < DEPTH:
            kv = _memory_kv(mem2d, wk_x, wv_x, l + 1)
        yg = _moe_experts(hp, table, counts, l, w1, w3, w2)
        pending = (yg, gates, ln_g, ln_b, 3 * l + 2)
    return _combine(h, *pending).reshape(bsz, seq, d)
```

```python
import functools
import math

import jax
import jax.numpy as jnp
from jax import lax
from jax.experimental import pallas as pl
from jax.experimental.pallas import tpu as pltpu
from jax.experimental.pallas import tpu_sc as plsc

F32 = jnp.float32
BF16 = jnp.bfloat16
I32 = jnp.int32

POOL_WINDOWS = (2, 4, 8, 16)
assert all(w & (w - 1) == 0 for w in POOL_WINDOWS)
N_SG_HEADS = 4
CHUNK = 128
CONV_WIDTH = 3
N_XHEADS = 4
N_GROUPS = 4
EXPERTS_PER_GROUP = 8
N_EXPERTS = N_GROUPS * EXPERTS_PER_GROUP
TOP_K = 2
DEPTH = 4
ALPHA = (2.0 * DEPTH) ** 0.25
LN_EPS = 1e-5

LANES = 128
SC_CORES = 2
SC_WORKERS = 32
SC_LANES = 16
VMEM_LIMIT = 56 * 1024 * 1024

SC_CHUNK = 64
TM = 1024
SUB_TILES = 2
ROUTE_ROWS = 8
POOL_HALO = 16
CONV_HALO = 8
EXPERT_UNIT = 128
ITEM_UNITS = 8
ROUTE_COLS = 128
EXPERT_ROW0 = 8
STAGE_COLS = 512

_NT = (((1,), (1,)), ((), ()))


def _dot(a, b):
    return jnp.dot(a, b, preferred_element_type=F32)


def _layer_norm(y, g, b):
    mu = jnp.mean(y, axis=-1, keepdims=True)
    yc = y - mu
    var = jnp.mean(yc * yc, axis=-1, keepdims=True)
    return yc * lax.rsqrt(var + LN_EPS) * g + b


def _gelu_tanh(x):
    c = math.sqrt(2.0 / math.pi)
    return 0.5 * x * (1.0 + jnp.tanh(c * (x + 0.044715 * (x * x * x))))


def _pack_halves(v):
    c = v.shape[1] // 2
    lo = pltpu.bitcast(v[:, :c].astype(BF16).astype(F32), jnp.uint32)
    hi = pltpu.bitcast(v[:, c:].astype(BF16).astype(F32), jnp.uint32)
    return pltpu.bitcast((hi & jnp.uint32(0xFFFF0000)) | (lo >> 16), I32)


def _unpack_halves(w):
    u = pltpu.bitcast(w, jnp.uint32)
    return pltpu.bitcast(u << 16, F32), pltpu.bitcast(u & jnp.uint32(0xFFFF0000), F32)


def _load_cast(w_hbm, w_scr, stage, sems):
    chunks = w_scr.shape[1] // STAGE_COLS

    def chunk_copy(c):
        return pltpu.make_async_copy(w_hbm.at[:, pl.ds(c * STAGE_COLS, STAGE_COLS)], stage.at[c % 2],
                                     sems.at[c % 2])

    chunk_copy(0).start()
    for c in range(chunks):
        if c + 1 < chunks:
            chunk_copy(c + 1).start()
        chunk_copy(c).wait()
        w_scr[:, c * STAGE_COLS:(c + 1) * STAGE_COLS] = stage[c % 2].astype(BF16)


def _stage_scratch(rows):
    return [pltpu.VMEM((2, rows, STAGE_COLS), F32), pltpu.SemaphoreType.DMA((2,))]


_HBM = pl.BlockSpec(memory_space=pl.ANY)


def _const_spec(shape):
    nd = len(shape)
    return pl.BlockSpec(shape, lambda i: (0,) * nd)


def _pick_spec(stacked, index):
    rest = stacked.shape[1:]
    return pl.BlockSpec((None,) + rest, lambda i: (index,) + (0,) * len(rest))


def _params():
    return pltpu.CompilerParams(dimension_semantics=("arbitrary",), vmem_limit_bytes=VMEM_LIMIT)


def _moe_output(x_ref, y0_ref, y1_ref, gate_ref, g_ref, b_ref):
    gates = gate_ref[...]
    g0, g1 = gates[:, 0:1], gates[:, 1:2]
    y0_lo, y0_hi = _unpack_halves(y0_ref[...])
    y1_lo, y1_hi = _unpack_halves(y1_ref[...])
    ff = jnp.concatenate([g0 * y0_lo + g1 * y1_lo, g0 * y0_hi + g1 * y1_hi], axis=1)
    return _layer_norm(ALPHA * x_ref[...] + ff, g_ref[...], b_ref[...])


def _mixer_input(src, pending):
    if not pending:
        return src[0][...]
    return _moe_output(*src)


def _mixer_sources(x, pending):
    n, d = x.shape
    tiles = n // TM
    specs = [pl.BlockSpec((TM, d), lambda i: (i, 0))]
    args = [x]
    if pending is not None:
        yg, gates, ln_g, ln_b, ln_row = pending
        specs += [pl.BlockSpec((TM, d // 2), lambda i: (i, 0)),
                  pl.BlockSpec((TM, d // 2), lambda i: (i + tiles, 0)),
                  pl.BlockSpec((TM, TOP_K), lambda i: (i, 0)),
                  _pick_spec(ln_g, ln_row), _pick_spec(ln_b, ln_row)]
        args += [yg, yg, gates, ln_g, ln_b]
    return specs, args


def _even_kernel(*refs, tiles_per_seq, layer, pending):
    n_src = 6 if pending else 1
    (win_hbm, wpool_ref, pscale_ref, lvg_ref, lvb_ref, ws_ref, bst_ref, wout_hbm, g_ref, b_ref, o_ref,
     a_scr, cat_scr, win_ref, wout_ref, stage, sems) = refs[n_src:]
    tm = o_ref.shape[0]
    d_pool = a_scr.shape[1]
    d_sg = lvg_ref.shape[1]
    pgd = d_pool // len(POOL_WINDOWS)
    hd_dim = d_sg // N_SG_HEADS
    seq_tile = pl.program_id(0) % tiles_per_seq

    @pl.when(pl.program_id(0) == 0)
    def _():
        _load_cast(win_hbm.at[layer], win_ref, stage, sems)
        _load_cast(wout_hbm.at[layer], wout_ref, stage, sems)

    @pl.when(seq_tile == 0)
    def _():
        a_scr[0:POOL_HALO, :] = jnp.zeros((POOL_HALO, d_pool), F32)

    x = _mixer_input(refs[:n_src], pending)
    sub = tm // SUB_TILES
    row = lax.broadcasted_iota(I32, (CHUNK, CHUNK), 0)
    col = lax.broadcasted_iota(I32, (CHUNK, CHUNK), 1)
    ws_masked = [jnp.where(row >= col, ws_ref[hd], 0.0).astype(BF16) for hd in range(N_SG_HEADS)]

    def in_proj(st):
        h = _dot(x[st * sub:(st + 1) * sub, :].astype(BF16), win_ref[...])
        a_scr[POOL_HALO + st * sub:POOL_HALO + (st + 1) * sub, :] = h[:, :d_pool]
        return h[:, d_pool:]

    def branches_out_proj(st, hz):
        base = st * sub
        pos = seq_tile * tm + base + lax.broadcasted_iota(I32, (sub, 1), 0)
        for g, w in enumerate(POOL_WINDOWS):
            cs = slice(g * pgd, (g + 1) * pgd)
            tok = a_scr[POOL_HALO + base:POOL_HALO + base + sub, cs]
            acc = a_scr[base:POOL_HALO + base + sub, cs]
            span = 1
            while span < w:
                acc = acc[span:, :] + acc[:-span, :]
                span *= 2
            acc = acc[acc.shape[0] - sub:, :]
            cnt = jnp.minimum(pos + 1, w).astype(F32)
            dev = acc * (1.0 / cnt) - tok
            yg = _dot(dev.astype(BF16), wpool_ref[g])
            cat_scr[base:base + sub, cs] = (yg * pscale_ref[:, cs]).astype(BF16)

        z = _gelu_tanh(hz)
        u = z[:, :d_sg]
        v = _layer_norm(z[:, d_sg:], lvg_ref[...], lvb_ref[...]).astype(BF16)
        for hd in range(N_SG_HEADS):
            hs = slice(hd * hd_dim, (hd + 1) * hd_dim)
            bcol = bst_ref[:, hd:hd + 1]
            for ck in range(sub // CHUNK):
                rs = slice(ck * CHUNK, (ck + 1) * CHUNK)
                sv = _dot(ws_masked[hd], v[rs, hs]) + bcol
                cat_scr[base + ck * CHUNK:base + (ck + 1) * CHUNK,
                        d_pool + hd * hd_dim:d_pool + (hd + 1) * hd_dim] = (u[rs, hs] * sv).astype(BF16)
        return _dot(cat_scr[base:base + sub, :], wout_ref[...])

    hzs = [in_proj(st) for st in range(SUB_TILES)]
    mixes = [branches_out_proj(st, hzs[st]) for st in range(SUB_TILES)]
    for st in range(SUB_TILES):
        rs = slice(st * sub, (st + 1) * sub)
        o_ref[rs, :] = _layer_norm(ALPHA * x[rs, :] + mixes[st], g_ref[...], b_ref[...])
    a_scr[0:POOL_HALO, :] = a_scr[tm:tm + POOL_HALO, :]


def _even_mixer(x, pending, seq, layer, w_in, w_pool, pool_scale, ln_v_g, ln_v_b, w_spatial, b_spatial_t,
                w_out, ln_g, ln_b, ln_row):
    n, d = x.shape
    d_in = w_in.shape[2]
    d_pool = pool_scale.shape[2]
    d_sg = ln_v_g.shape[2]
    kern = functools.partial(_even_kernel, tiles_per_seq=seq // TM, layer=layer, pending=pending is not None)
    src_specs, src_args = _mixer_sources(x, pending)
    return pl.pallas_call(
        kern,
        out_shape=jax.ShapeDtypeStruct((n, d), F32),
        grid=(n // TM,),
        in_specs=src_specs + [
            _HBM,
            _pick_spec(w_pool, layer),
            _pick_spec(pool_scale, layer),
            _pick_spec(ln_v_g, layer),
            _pick_spec(ln_v_b, layer),
            _pick_spec(w_spatial, layer),
            _pick_spec(b_spatial_t, layer),
            _HBM,
            _pick_spec(ln_g, ln_row),
            _pick_spec(ln_b, ln_row),
        ],
        out_specs=pl.BlockSpec((TM, d), lambda i: (i, 0)),
        scratch_shapes=[pltpu.VMEM((POOL_HALO + TM, d_pool), F32), pltpu.VMEM((TM, d_pool + d_sg), BF16),
                        pltpu.VMEM((d, d_in), BF16), pltpu.VMEM((d_pool + d_sg, d), BF16)] + _stage_scratch(d),
        compiler_params=_params(),
        name="even_mixer",
    )(*src_args, w_in, w_pool, pool_scale, ln_v_g, ln_v_b, w_spatial, b_spatial_t, w_out, ln_g, ln_b)


def _odd_kernel(*refs, tiles_per_seq, layer, pending):
    n_src = 6 if pending else 1
    (win_hbm, cwt_ref, cb_ref, wout_hbm, g_ref, b_ref, o_ref, zc_scr,
     win_ref, wout_ref, stage, sems) = refs[n_src:]
    tm, d = o_ref.shape
    seq_tile = pl.program_id(0) % tiles_per_seq

    @pl.when(pl.program_id(0) == 0)
    def _():
        _load_cast(win_hbm.at[layer], win_ref, stage, sems)
        _load_cast(wout_hbm.at[layer], wout_ref, stage, sems)

    @pl.when(seq_tile == 0)
    def _():
        zc_scr[0:CONV_HALO, :] = jnp.zeros((CONV_HALO, d), F32)

    x = _mixer_input(refs[:n_src], pending)
    sub = tm // SUB_TILES

    def in_proj(st):
        xb = x[st * sub:(st + 1) * sub, :].astype(BF16)
        hc = _dot(xb, win_ref[:, d:2 * d])
        hz = _dot(xb, win_ref[:, 2 * d:])
        zc_scr[CONV_HALO + st * sub:CONV_HALO + (st + 1) * sub, :] = hc * hz
        return _dot(xb, win_ref[:, :d])

    def conv_out_proj(st, gate):
        conv = cb_ref[...]
        for j in range(CONV_WIDTH):
            off = CONV_HALO + st * sub - (CONV_WIDTH - 1) + j
            conv = conv + zc_scr[off:off + sub, :] * cwt_ref[j:j + 1, :]
        return _dot((gate * conv).astype(BF16), wout_ref[...])

    gates = [in_proj(st) for st in range(SUB_TILES)]
    ys = [conv_out_proj(st, gates[st]) for st in range(SUB_TILES)]
    for st in range(SUB_TILES):
        rs = slice(st * sub, (st + 1) * sub)
        o_ref[rs, :] = _layer_norm(ALPHA * x[rs, :] + ys[st], g_ref[...], b_ref[...])
    zc_scr[0:CONV_HALO, :] = zc_scr[tm:tm + CONV_HALO, :]


def _odd_mixer(x, pending, seq, layer, w_in, conv_w_t, conv_b, w_out, ln_g, ln_b, ln_row):
    n, d = x.shape
    kern = functools.partial(_odd_kernel, tiles_per_seq=seq // TM, layer=layer, pending=pending is not None)
    src_specs, src_args = _mixer_sources(x, pending)
    return pl.pallas_call(
        kern,
        out_shape=jax.ShapeDtypeStruct((n, d), F32),
        grid=(n // TM,),
        in_specs=src_specs + [
            _HBM,
            _pick_spec(conv_w_t, layer),
            _pick_spec(conv_b, layer),
            _HBM,
            _pick_spec(ln_g, ln_row),
            _pick_spec(ln_b, ln_row),
        ],
        out_specs=pl.BlockSpec((TM, d), lambda i: (i, 0)),
        scratch_shapes=[pltpu.VMEM((CONV_HALO + TM, d), F32),
                        pltpu.VMEM(w_in.shape[1:], BF16), pltpu.VMEM(w_out.shape[1:], BF16)] + _stage_scratch(d),
        compiler_params=_params(),
        name="odd_mixer",
    )(*src_args, w_in, conv_w_t, conv_b, w_out, ln_g, ln_b)


def _kv_kernel(mem_ref, wk_ref, wv_ref, k_ref, v_ref):
    m = mem_ref[...].astype(BF16)
    k_ref[...] = _dot(m, wk_ref[...].astype(BF16)).astype(BF16)
    v_ref[...] = _dot(m, wv_ref[...].astype(BF16)).astype(BF16)


def _memory_kv(mem2d, wk, wv, layer):
    rows, d = mem2d.shape
    out = jax.ShapeDtypeStruct((rows, d), BF16)
    return pl.pallas_call(
        _kv_kernel,
        out_shape=(out, out),
        grid=(1,),
        in_specs=[_const_spec((rows, d)), _pick_spec(wk, layer), _pick_spec(wv, layer)],
        out_specs=(_const_spec((rows, d)), _const_spec((rows, d))),
        compiler_params=_params(),
        name="memory_kv",
    )(mem2d, wk, wv)


def _attn_kernel(x_ref, k_ref, v_ref, wq_hbm, wo_hbm, g_ref, b_ref, wr_ref, br_ref,
                 o_ref, op_ref, rt_ref, rg_ref, cnt_ref, o_scr, carry_scr, wq_ref, wo_ref, stage, sems,
                 *, layer):
    tm, d = x_ref.shape
    hd_dim = d // N_XHEADS
    sub = tm // SUB_TILES

    @pl.when(pl.program_id(0) == 0)
    def _():
        carry_scr[...] = jnp.zeros_like(carry_scr)
        _load_cast(wq_hbm.at[layer], wq_ref, stage, sems)
        _load_cast(wo_hbm.at[layer], wo_ref, stage, sems)

    row_slices = [slice(st * sub, (st + 1) * sub) for st in range(SUB_TILES)]
    for rs in row_slices:
        q = _dot(x_ref[rs, :].astype(BF16), wq_ref[...]) * (1.0 / math.sqrt(hd_dim))
        for hd in range(N_XHEADS):
            hs = slice(hd * hd_dim, (hd + 1) * hd_dim)
            s = lax.dot_general(q[:, hs].astype(BF16), k_ref[:, hs], _NT, preferred_element_type=F32)
            p = jnp.exp(s - jnp.max(s, axis=-1, keepdims=True))
            p = p * (1.0 / jnp.sum(p, axis=-1, keepdims=True))
            o_scr[rs, hs] = _dot(p.astype(BF16), v_ref[:, hs]).astype(BF16)
    xas = [_dot(o_scr[rs, :], wo_ref[...]) for rs in row_slices]
    for rs, xa in zip(row_slices, xas):
        out = _layer_norm(ALPHA * x_ref[rs, :] + xa, g_ref[...], b_ref[...])
        o_ref[rs, :] = out
        op_ref[rs, :] = _pack_halves(out)
        table, gates = _route_rows(out, wr_ref, br_ref, carry_scr)
        rt_ref[:, rs] = table
        rg_ref[rs, :] = gates
    cnt_ref[...] = carry_scr[...].astype(I32)


def _cross_attn(x, seq, layer, k, v, wq, wo, ln_g, ln_b, ln_row, router_w, router_b):
    n, d = x.shape
    m = k.shape[1]
    tiles_per_seq = seq // TM
    kvspec = pl.BlockSpec((None, m, d), lambda i: (i // tiles_per_seq, 0, 0))
    out, packed, table, gates, cnt = pl.pallas_call(
        functools.partial(_attn_kernel, layer=layer),
        out_shape=(jax.ShapeDtypeStruct((n, d), F32), jax.ShapeDtypeStruct((n, d // 2), I32),
                   jax.ShapeDtypeStruct((ROUTE_ROWS, n), F32), jax.ShapeDtypeStruct((n, TOP_K), F32),
                   jax.ShapeDtypeStruct((N_EXPERTS, LANES), I32)),
        grid=(n // TM,),
        in_specs=[
            pl.BlockSpec((TM, d), lambda i: (i, 0)),
            kvspec, kvspec,
            _HBM, _HBM,
            _pick_spec(ln_g, ln_row), _pick_spec(ln_b, ln_row),
            _pick_spec(router_w, layer), _pick_spec(router_b, layer),
        ],
        out_specs=(pl.BlockSpec((TM, d), lambda i: (i, 0)), pl.BlockSpec((TM, d // 2), lambda i: (i, 0)),
                   pl.BlockSpec((ROUTE_ROWS, TM), lambda i: (0, i)), pl.BlockSpec((TM, TOP_K), lambda i: (i, 0)),
                   _const_spec((N_EXPERTS, LANES))),
        scratch_shapes=[pltpu.VMEM((TM, d), BF16), pltpu.VMEM((N_EXPERTS, LANES), F32),
                        pltpu.VMEM((d, d), BF16), pltpu.VMEM((d, d), BF16)] + _stage_scratch(d),
        compiler_params=_params(),
        name="cross_attn",
    )(x, k, v, wq, wo, ln_g, ln_b, router_w, router_b)
    return out, packed, table, gates, cnt[:, 0]


def _route_rows(x, w_ref, bias_ref, carry_scr):
    tm = x.shape[0]
    neg = -jnp.inf
    logits = _dot(x.astype(BF16), w_ref[...]) + bias_ref[...]
    lt = jnp.transpose(logits)

    def first_argmax(vals):
        rows = lax.broadcasted_iota(I32, vals.shape, 0).astype(F32)
        mx = jnp.max(vals, axis=0, keepdims=True)
        idx = jnp.min(jnp.where(vals == mx, rows, float(vals.shape[0])), axis=0, keepdims=True)
        return mx, idx, rows

    gl = lt[0:N_GROUPS, :]
    gmax, g_sel, _ = first_argmax(gl)
    gate_g = 1.0 / jnp.sum(jnp.exp(gl - gmax), axis=0, keepdims=True)

    el = lt[EXPERT_ROW0:EXPERT_ROW0 + EXPERTS_PER_GROUP, :]
    for g in range(1, N_GROUPS):
        lo = EXPERT_ROW0 + g * EXPERTS_PER_GROUP
        el = jnp.where(g_sel == float(g), lt[lo:lo + EXPERTS_PER_GROUP, :], el)
    m1, i1, erow = first_argmax(el)
    m2, i2, _ = first_argmax(jnp.where(erow == i1, neg, el))
    e21 = jnp.exp(m2 - m1)
    w1 = 1.0 / (1.0 + e21)
    w2 = e21 / (1.0 + e21)
    e1 = g_sel * EXPERTS_PER_GROUP + i1
    e2 = g_sel * EXPERTS_PER_GROUP + i2

    xrow = lax.broadcasted_iota(I32, (N_EXPERTS, tm), 0).astype(F32)
    oh1 = xrow == e1
    oh2 = xrow == e2
    oh = (oh1 | oh2).astype(BF16)
    r = lax.broadcasted_iota(I32, (tm, tm), 0)
    c = lax.broadcasted_iota(I32, (tm, tm), 1)
    before = _dot(oh, (r < c).astype(BF16)) + carry_scr[:, 0:1]
    rank1 = jnp.sum(jnp.where(oh1, before, 0.0), axis=0, keepdims=True)
    rank2 = jnp.sum(jnp.where(oh2, before, 0.0), axis=0, keepdims=True)
    carry_scr[...] += jnp.sum(oh.astype(F32), axis=1, keepdims=True)

    trow = lax.broadcasted_iota(I32, (ROUTE_ROWS, tm), 0)
    table = jnp.where(trow == 0, e1, jnp.where(trow == 1, e2, jnp.where(trow == 2, rank1,
                                                                      jnp.where(trow == 3, rank2, 0.0))))
    grow = lax.broadcasted_iota(I32, (LANES, tm), 0)
    gates = jnp.transpose(jnp.where(grow == 0, gate_g * w1, jnp.where(grow == 1, gate_g * w2, 0.0)))[:, :TOP_K]
    return table, gates


def _router_weights(wr_g, br_g, wr_e, br_e):
    nl, d, _ = wr_g.shape
    gap = EXPERT_ROW0 - N_GROUPS
    w = jnp.concatenate([wr_g, jnp.zeros((nl, d, gap), F32),
                         jnp.transpose(wr_e, (0, 2, 1, 3)).reshape(nl, d, N_EXPERTS)], axis=2)
    w = jnp.pad(w, ((0, 0), (0, 0), (0, ROUTE_COLS - w.shape[2]))).astype(BF16)
    bias = jnp.concatenate([br_g, jnp.zeros((nl, gap), F32), br_e.reshape(nl, -1)], axis=1)
    bias = jnp.pad(bias, ((0, 0), (0, ROUTE_COLS - bias.shape[1])))
    return w, bias[:, None, :]


def _sc_worker_rows(rows):
    per_worker = rows // SC_WORKERS
    n_chunks = per_worker // SC_CHUNK
    assert per_worker * SC_WORKERS == rows and n_chunks * SC_CHUNK == per_worker and n_chunks % 2 == 0
    return per_worker, n_chunks


def _sc_gather_rows(table_hbm, out_hbm, idx_v, rows_v, gsem, wsem, base, n_chunks):
    def fetch(c, slot):
        off = pl.multiple_of(c * SC_CHUNK, SC_CHUNK)
        return pltpu.make_async_copy(table_hbm.at[idx_v.at[pl.ds(off, SC_CHUNK)]], rows_v.at[slot],
                                     gsem.at[slot])

    def put(c, slot):
        off = pl.multiple_of(c * SC_CHUNK, SC_CHUNK)
        return pltpu.make_async_copy(rows_v.at[slot], out_hbm.at[pl.ds(base + off, SC_CHUNK)], wsem.at[slot])

    fetch(0, 0).start()

    @pl.loop(0, n_chunks, step=2)
    def _(c0):
        for slot in range(2):
            c = c0 + slot

            @pl.when(c + 1 < n_chunks)
            def _():
                @pl.when(c >= 1)
                def _():
                    put(c - 1, 1 - slot).wait()
                fetch(c + 1, 1 - slot).start()

            fetch(c, slot).wait()
            put(c, slot).start()

    put(n_chunks - 2, 0).wait()
    put(n_chunks - 1, 1).wait()


def _sc_row_scratch(per_worker, d, dtype):
    return [pltpu.VMEM((per_worker,), I32), pltpu.VMEM((2, SC_CHUNK, d), dtype),
            pltpu.SemaphoreType.DMA((2,)), pltpu.SemaphoreType.DMA((2,))]


def _sc_gather(table, idx):
    b = idx.shape[0]
    d = table.shape[1]
    per_worker, n_chunks = _sc_worker_rows(b)
    mesh = plsc.VectorSubcoreMesh(core_axis_name="c", subcore_axis_name="s")

    @functools.partial(
        pl.kernel, mesh=mesh,
        out_type=jax.ShapeDtypeStruct((b, d), table.dtype),
        scratch_types=_sc_row_scratch(per_worker, d, table.dtype),
        name="sc_gather",
    )
    def gather(table_hbm, idx_hbm, out_hbm, idx_v, rows_v, gsem, wsem):
        base = (lax.axis_index("s") * SC_CORES + lax.axis_index("c")) * per_worker
        pltpu.sync_copy(idx_hbm.at[pl.ds(base, per_worker)], idx_v)
        _sc_gather_rows(table_hbm, out_hbm, idx_v, rows_v, gsem, wsem, base, n_chunks)

    return gather(table, idx)


def _sc_dispatch(table, dest_flat, rows):
    n, d = table.shape
    a = dest_flat.shape[0]
    lanes = SC_LANES
    per_worker, n_chunks = _sc_worker_rows(rows)
    assert a % lanes == 0 and per_worker % lanes == 0
    mesh = plsc.VectorSubcoreMesh(core_axis_name="c", subcore_axis_name="s")

    @functools.partial(
        pl.kernel, mesh=mesh,
        out_type=jax.ShapeDtypeStruct((rows, d), table.dtype),
        scratch_types=[pltpu.VMEM((a,), I32)] + _sc_row_scratch(per_worker, d, table.dtype),
        compiler_params=pltpu.CompilerParams(needs_layout_passes=False),
        name="sc_dispatch",
    )
    def dispatch(table_hbm, dest_hbm, out_hbm, dest_v, idx_v, rows_v, gsem, wsem):
        base = (lax.axis_index("s") * SC_CORES + lax.axis_index("c")) * per_worker
        pltpu.sync_copy(dest_hbm, dest_v)
        lane = lax.iota(I32, lanes)

        @pl.loop(0, per_worker // lanes)
        def _(i):
            idx_v[pl.ds(i * lanes, lanes)] = lax.rem(base + i * lanes + lane, n)

        for k in range(a // n):
            @plsc.parallel_loop(0, n // lanes, unroll=8)
            def _(i):
                local = dest_v[pl.ds(k * n + i * lanes, lanes)] - base
                mine = (local >= 0) & (local < per_worker)
                plsc.store_scatter(idx_v, [jnp.where(mine, local, 0)], i * lanes + lane, mask=mine)

        _sc_gather_rows(table_hbm, out_hbm, idx_v, rows_v, gsem, wsem, base, n_chunks)

    return dispatch(table, dest_flat)


def _expert_kernel(sched_ref, ni_ref, xs_hbm, w1_hbm, w3_hbm, w2_hbm, y_hbm,
                   x_buf, y_buf, w1_buf, w3_buf, w2_buf, w1_scr, w3_scr, w2_scr, wsems, xsems, ysems,
                   *, layer):
    i = pl.program_id(0)
    n_items = ni_ref[0]
    expert, wslot, run_start, next_expert = (sched_ref[r, i] for r in range(4))
    slot = i % 2

    def for_units(item, fn):
        for units in range(1, ITEM_UNITS + 1):
            @pl.when(sched_ref[5, item] == units)
            def _():
                fn(units * EXPERT_UNIT)

    def x_copy(item, s, rows):
        row0 = pl.multiple_of(sched_ref[4, item], EXPERT_UNIT)
        return pltpu.make_async_copy(xs_hbm.at[pl.ds(row0, rows)], x_buf.at[s, pl.ds(0, rows)], xsems.at[s])

    def y_copy(item, s, rows):
        row0 = pl.multiple_of(sched_ref[4, item], EXPERT_UNIT)
        return pltpu.make_async_copy(y_buf.at[s, pl.ds(0, rows)], y_hbm.at[pl.ds(row0, rows)], ysems.at[s])

    def fetch(e, s):
        return [pltpu.make_async_copy(w_hbm.at[layer, e], buf.at[s], wsems.at[s, j])
                for j, (w_hbm, buf) in enumerate(((w1_hbm, w1_buf), (w3_hbm, w3_buf), (w2_hbm, w2_buf)))]

    def mlp(rows):
        x_lo, x_hi = _unpack_halves(x_buf[slot, 0:rows, :])
        xb = jnp.concatenate([x_lo.astype(BF16), x_hi.astype(BF16)], axis=1)
        h1 = _dot(xb, w1_scr[...])
        h3 = _dot(xb, w3_scr[...])
        hid = h1 * (1.0 / (1.0 + jnp.exp(-h1))) * h3
        y_buf[slot, 0:rows, :] = _pack_halves(_dot(hid.astype(BF16), w2_scr[...]))
        y_copy(i, slot, rows).start()

    @pl.when(i < n_items)
    def _():
        @pl.when(i == 0)
        def _():
            for_units(0, lambda rows: x_copy(0, 0, rows).start())

        for_units(i, lambda rows: x_copy(i, slot, rows).wait())

        @pl.when(i + 1 < n_items)
        def _():
            for_units(i + 1, lambda rows: x_copy(i + 1, 1 - slot, rows).start())

        @pl.when(run_start == 1)
        def _():
            @pl.when(i == 0)
            def _():
                for c in fetch(expert, wslot):
                    c.start()

            for c in fetch(expert, wslot):
                c.wait()

            @pl.when(next_expert >= 0)
            def _():
                for c in fetch(next_expert, 1 - wslot):
                    c.start(priority=1)

            w1_scr[...] = w1_buf[wslot].astype(BF16)
            w3_scr[...] = w3_buf[wslot].astype(BF16)
            w2_scr[...] = w2_buf[wslot].astype(BF16)

        @pl.when(i >= 2)
        def _():
            for_units(i - 2, lambda rows: y_copy(i - 2, slot, rows).wait())

        for_units(i, mlp)

        @pl.when(i == n_items - 1)
        def _():
            @pl.when(i >= 1)
            def _():
                for_units(i - 1, lambda rows: y_copy(i - 1, 1 - slot, rows).wait())

            for_units(i, lambda rows: y_copy(i, slot, rows).wait())
            y_buf[0, 0:EXPERT_UNIT, :] = jnp.zeros((EXPERT_UNIT, y_buf.shape[2]), y_buf.dtype)
            first_free = (sched_ref[4, i] + sched_ref[5, i] * EXPERT_UNIT) // EXPERT_UNIT

            def zero_copy(u):
                return pltpu.make_async_copy(
                    y_buf.at[0, pl.ds(0, EXPERT_UNIT)],
                    y_hbm.at[pl.ds(pl.multiple_of(u * EXPERT_UNIT, EXPERT_UNIT), EXPERT_UNIT)], ysems.at[0])

            def start_zero(u, _):
                zero_copy(u).start()
                return 0

            def wait_zero(u, _):
                zero_copy(u).wait()
                return 0

            lax.fori_loop(first_free, y_hbm.shape[0] // EXPERT_UNIT, start_zero, 0)
            lax.fori_loop(first_free, y_hbm.shape[0] // EXPERT_UNIT, wait_zero, 0)


def _expert_mlp(xs, schedule, n_items, layer, w1, w3, w2):
    d, de = w1.shape[2], w1.shape[3]
    max_rows = ITEM_UNITS * EXPERT_UNIT
    return pl.pallas_call(
        functools.partial(_expert_kernel, layer=layer),
        out_shape=jax.ShapeDtypeStruct(xs.shape, I32),
        grid_spec=pltpu.PrefetchScalarGridSpec(
            num_scalar_prefetch=2,
            grid=(schedule.shape[1],),
            in_specs=[_HBM, _HBM, _HBM, _HBM],
            out_specs=_HBM,
            scratch_shapes=[pltpu.VMEM((2, max_rows, d // 2), I32), pltpu.VMEM((2, max_rows, d // 2), I32),
                            pltpu.VMEM((2, d, de), F32), pltpu.VMEM((2, d, de), F32), pltpu.VMEM((2, de, d), F32),
                            pltpu.VMEM((d, de), BF16), pltpu.VMEM((d, de), BF16), pltpu.VMEM((de, d), BF16),
                            pltpu.SemaphoreType.DMA((2, 3)), pltpu.SemaphoreType.DMA((2,)),
                            pltpu.SemaphoreType.DMA((2,))],
        ),
        compiler_params=_params(),
        name="expert_mlp",
    )(schedule, n_items, xs, w1, w3, w2)


def _combine_kernel(x_ref, y0_ref, y1_ref, gate_ref, g_ref, b_ref, o_ref):
    o_ref[...] = _moe_output(x_ref, y0_ref, y1_ref, gate_ref, g_ref, b_ref)


def _combine(x, yg, gates, ln_g, ln_b, ln_row):
    n, d = x.shape
    tiles = n // TM
    return pl.pallas_call(
        _combine_kernel,
        out_shape=jax.ShapeDtypeStruct((n, d), F32),
        grid=(tiles,),
        in_specs=[pl.BlockSpec((TM, d), lambda i: (i, 0)),
                  pl.BlockSpec((TM, d // 2), lambda i: (i, 0)),
                  pl.BlockSpec((TM, d // 2), lambda i: (i + tiles, 0)),
                  pl.BlockSpec((TM, TOP_K), lambda i: (i, 0)),
                  _pick_spec(ln_g, ln_row), _pick_spec(ln_b, ln_row)],
        out_specs=pl.BlockSpec((TM, d), lambda i: (i, 0)),
        compiler_params=_params(),
        name="combine",
    )(x, yg, yg, gates, ln_g, ln_b)


def _moe_experts(x_packed, table, counts, layer, w1, w3, w2):
    n = x_packed.shape[0]
    max_units = (n * TOP_K + N_EXPERTS * (EXPERT_UNIT - 1) + EXPERT_UNIT - 1) // EXPERT_UNIT
    max_items = (max_units + N_EXPERTS * (ITEM_UNITS - 1) + ITEM_UNITS - 1) // ITEM_UNITS
    experts = table[:TOP_K].astype(I32)
    ranks = table[TOP_K:2 * TOP_K].astype(I32)

    units_e = (counts + EXPERT_UNIT - 1) // EXPERT_UNIT
    units_start = jnp.cumsum(units_e) - units_e
    items_e = (units_e + ITEM_UNITS - 1) // ITEM_UNITS
    items_end = jnp.cumsum(items_e)
    n_items = items_end[-1:].astype(I32)
    item_ids = jnp.arange(max_items, dtype=I32)
    expert_ids = jnp.arange(N_EXPERTS, dtype=I32)
    item_expert = jnp.minimum(jnp.sum(items_end[None, :] <= item_ids[:, None], axis=1), N_EXPERTS - 1)
    later = (expert_ids[None, :] > expert_ids[:, None]) & (items_e[None, :] > 0)
    next_run = jnp.min(jnp.where(later, expert_ids[None, :], N_EXPERTS), axis=1)
    next_run = jnp.where(next_run == N_EXPERTS, -1, next_run)
    per_expert = jnp.stack([items_end - items_e, units_start, units_e, next_run], axis=1)
    mine = (item_expert[:, None] == expert_ids[None, :])[:, :, None]
    first_item, unit0, units, next_expert = jnp.sum(jnp.where(mine, per_expert[None], 0), axis=1).T
    within = item_ids - first_item
    item_row0 = (unit0 + ITEM_UNITS * within) * EXPERT_UNIT
    item_units = jnp.clip(units - ITEM_UNITS * within, 1, ITEM_UNITS)
    run_start_flag = ((within == 0) & (item_ids < n_items[0])).astype(I32)
    slot = (jnp.cumsum(run_start_flag) - 1) % 2
    schedule = jnp.stack([item_expert, slot, run_start_flag, next_expert, item_row0, item_units]).astype(I32)
    start_of = jnp.sum(jnp.where(experts[:, :, None] == expert_ids, units_start * EXPERT_UNIT, 0), axis=-1)
    dest = (start_of + ranks).astype(I32).reshape(-1)
    xs = _sc_dispatch(x_packed, dest, max_units * EXPERT_UNIT)
    y = _expert_mlp(xs, schedule, n_items, layer, w1, w3, w2)
    return _sc_gather(y, dest)


def kernel(x, mem, w_in_even, w_pool, pool_scale, ln_v_g, ln_v_b, w_spatial, b_spatial, w_out_even,
           w_in_odd, conv_w, conv_b, w_out_odd, wq_x, wk_x, wv_x, wo_x, ln_g, ln_b, wr_group,
           br_group, wr_expert, br_expert, w1, w3, w2):
    bsz, seq, d = x.shape
    assert seq % TM == 0 and d % LANES == 0
    mlen = mem.shape[1]
    mem2d = mem.reshape(bsz * mlen, d)
    ln_g = ln_g.reshape(DEPTH * 3, 1, d)
    ln_b = ln_b.reshape(DEPTH * 3, 1, d)
    w_pool = w_pool.astype(BF16)
    pool_scale, ln_v_g, ln_v_b, conv_b = (p[:, None, :] for p in (pool_scale, ln_v_g, ln_v_b, conv_b))
    b_spatial_t = jnp.swapaxes(b_spatial, 1, 2)
    conv_w_t = jnp.swapaxes(conv_w, 1, 2)
    router_w, router_b = _router_weights(wr_group, br_group, wr_expert, br_expert)

    h = x.reshape(bsz * seq, d)
    pending = None
    kv = _memory_kv(mem2d, wk_x, wv_x, 0)
    for l in range(DEPTH):
        i = l // 2
        k, v = (a.reshape(bsz, mlen, d) for a in kv)
        if l % 2 == 0:
            h = _even_mixer(h, pending, seq, i, w_in_even, w_pool, pool_scale, ln_v_g, ln_v_b,
                            w_spatial, b_spatial_t, w_out_even, ln_g, ln_b, 3 * l)
        else:
            h = _odd_mixer(h, pending, seq, i, w_in_odd, conv_w_t, conv_b, w_out_odd, ln_g, ln_b, 3 * l)
        h, hp, table, gates, counts = _cross_attn(h, seq, l, k, v, wq_x, wo_x,
                                                  ln_g, ln_b, 3 * l + 1, router_w, router_b)
        if l + 1 < DEPTH:
            kv = _memory_kv(mem2d, wk_x, wv_x, l + 1)
        yg = _moe_experts(hp, table, counts, l, w1, w3, w2)
        pending = (yg, gates, ln_g, ln_b, 3 * l + 2)
    return _combine(h, *pending).reshape(bsz, seq, d)
```

```python
import functools
import math

import jax
import jax.numpy as jnp
from jax import lax
from jax.experimental import pallas as pl
from jax.experimental.pallas import tpu as pltpu
from jax.experimental.pallas import tpu_sc as plsc

F32 = jnp.float32
BF16 = jnp.bfloat16
I32 = jnp.int32

POOL_WINDOWS = (2, 4, 8, 16)
assert all(w & (w - 1) == 0 for w in POOL_WINDOWS)
N_SG_HEADS = 4
CHUNK = 128
CONV_WIDTH = 3
N_XHEADS = 4
N_GROUPS = 4
EXPERTS_PER_GROUP = 8
N_EXPERTS = N_GROUPS * EXPERTS_PER_GROUP
TOP_K = 2
DEPTH = 4
ALPHA = (2.0 * DEPTH) ** 0.25
LN_EPS = 1e-5

LANES = 128
SC_CORES = 2
SC_WORKERS = 32
SC_LANES = 16
VMEM_LIMIT = 56 * 1024 * 1024

SC_CHUNK = 64
TM = 1024
SUB_TILES = 2
ROUTE_ROWS = 8
POOL_HALO = 16
CONV_HALO = 8
EXPERT_UNIT = 128
ITEM_UNITS = 4
ROUTE_COLS = 128
EXPERT_ROW0 = 8
STAGE_COLS = 512

_NT = (((1,), (1,)), ((), ()))


def _dot(a, b):
    return jnp.dot(a, b, preferred_element_type=F32)


def _layer_norm(y, g, b):
    mu = jnp.mean(y, axis=-1, keepdims=True)
    yc = y - mu
    var = jnp.mean(yc * yc, axis=-1, keepdims=True)
    return yc * lax.rsqrt(var + LN_EPS) * g + b


def _gelu_tanh(x):
    c = math.sqrt(2.0 / math.pi)
    return 0.5 * x * (1.0 + jnp.tanh(c * (x + 0.044715 * (x * x * x))))


def _pack_halves(v):
    c = v.shape[1] // 2
    lo = pltpu.bitcast(v[:, :c].astype(BF16).astype(F32), jnp.uint32)
    hi = pltpu.bitcast(v[:, c:].astype(BF16).astype(F32), jnp.uint32)
    return pltpu.bitcast((hi & jnp.uint32(0xFFFF0000)) | (lo >> 16), I32)


def _unpack_halves(w):
    u = pltpu.bitcast(w, jnp.uint32)
    return pltpu.bitcast(u << 16, F32), pltpu.bitcast(u & jnp.uint32(0xFFFF0000), F32)


def _load_cast(w_hbm, w_scr, stage, sems):
    chunks = w_scr.shape[1] // STAGE_COLS

    def chunk_copy(c):
        return pltpu.make_async_copy(w_hbm.at[:, pl.ds(c * STAGE_COLS, STAGE_COLS)], stage.at[c % 2],
                                     sems.at[c % 2])

    chunk_copy(0).start()
    for c in range(chunks):
        if c + 1 < chunks:
            chunk_copy(c + 1).start()
        chunk_copy(c).wait()
        w_scr[:, c * STAGE_COLS:(c + 1) * STAGE_COLS] = stage[c % 2].astype(BF16)


def _stage_scratch(rows):
    return [pltpu.VMEM((2, rows, STAGE_COLS), F32), pltpu.SemaphoreType.DMA((2,))]


_HBM = pl.BlockSpec(memory_space=pl.ANY)


def _const_spec(shape):
    nd = len(shape)
    return pl.BlockSpec(shape, lambda i: (0,) * nd)


def _pick_spec(stacked, index):
    rest = stacked.shape[1:]
    return pl.BlockSpec((None,) + rest, lambda i: (index,) + (0,) * len(rest))


def _params():
    return pltpu.CompilerParams(dimension_semantics=("arbitrary",), vmem_limit_bytes=VMEM_LIMIT)


def _moe_output(x_ref, y0_ref, y1_ref, gate_ref, g_ref, b_ref):
    gates = gate_ref[...]
    g0, g1 = gates[:, 0:1], gates[:, 1:2]
    y0_lo, y0_hi = _unpack_halves(y0_ref[...])
    y1_lo, y1_hi = _unpack_halves(y1_ref[...])
    ff = jnp.concatenate([g0 * y0_lo + g1 * y1_lo, g0 * y0_hi + g1 * y1_hi], axis=1)
    return _layer_norm(ALPHA * x_ref[...] + ff, g_ref[...], b_ref[...])


def _mixer_input(src, pending):
    if not pending:
        return src[0][...]
    return _moe_output(*src)


def _mixer_sources(x, pending):
    n, d = x.shape
    tiles = n // TM
    specs = [pl.BlockSpec((TM, d), lambda i: (i, 0))]
    args = [x]
    if pending is not None:
        yg, gates, ln_g, ln_b, ln_row = pending
        specs += [pl.BlockSpec((TM, d // 2), lambda i: (i, 0)),
                  pl.BlockSpec((TM, d // 2), lambda i: (i + tiles, 0)),
                  pl.BlockSpec((TM, TOP_K), lambda i: (i, 0)),
                  _pick_spec(ln_g, ln_row), _pick_spec(ln_b, ln_row)]
        args += [yg, yg, gates, ln_g, ln_b]
    return specs, args


def _even_kernel(*refs, tiles_per_seq, layer, pending):
    n_src = 6 if pending else 1
    (win_hbm, wpool_ref, pscale_ref, lvg_ref, lvb_ref, ws_ref, bst_ref, wout_hbm, g_ref, b_ref, o_ref,
     a_scr, cat_scr, win_ref, wout_ref, stage, sems) = refs[n_src:]
    tm = o_ref.shape[0]
    d_pool = a_scr.shape[1]
    d_sg = lvg_ref.shape[1]
    pgd = d_pool // len(POOL_WINDOWS)
    hd_dim = d_sg // N_SG_HEADS
    seq_tile = pl.program_id(0) % tiles_per_seq

    @pl.when(pl.program_id(0) == 0)
    def _():
        _load_cast(win_hbm.at[layer], win_ref, stage, sems)
        _load_cast(wout_hbm.at[layer], wout_ref, stage, sems)

    @pl.when(seq_tile == 0)
    def _():
        a_scr[0:POOL_HALO, :] = jnp.zeros((POOL_HALO, d_pool), F32)

    x = _mixer_input(refs[:n_src], pending)
    sub = tm // SUB_TILES
    row = lax.broadcasted_iota(I32, (CHUNK, CHUNK), 0)
    col = lax.broadcasted_iota(I32, (CHUNK, CHUNK), 1)
    ws_masked = [jnp.where(row >= col, ws_ref[hd], 0.0).astype(BF16) for hd in range(N_SG_HEADS)]

    def in_proj(st):
        h = _dot(x[st * sub:(st + 1) * sub, :].astype(BF16), win_ref[...])
        a_scr[POOL_HALO + st * sub:POOL_HALO + (st + 1) * sub, :] = h[:, :d_pool]
        return h[:, d_pool:]

    def branches_out_proj(st, hz):
        base = st * sub
        pos = seq_tile * tm + base + lax.broadcasted_iota(I32, (sub, 1), 0)
        for g, w in enumerate(POOL_WINDOWS):
            cs = slice(g * pgd, (g + 1) * pgd)
            tok = a_scr[POOL_HALO + base:POOL_HALO + base + sub, cs]
            acc = a_scr[base:POOL_HALO + base + sub, cs]
            span = 1
            while span < w:
                acc = acc[span:, :] + acc[:-span, :]
                span *= 2
            acc = acc[acc.shape[0] - sub:, :]
            cnt = jnp.minimum(pos + 1, w).astype(F32)
            dev = acc * (1.0 / cnt) - tok
            yg = _dot(dev.astype(BF16), wpool_ref[g])
            cat_scr[base:base + sub, cs] = (yg * pscale_ref[:, cs]).astype(BF16)

        z = _gelu_tanh(hz)
        u = z[:, :d_sg]
        v = _layer_norm(z[:, d_sg:], lvg_ref[...], lvb_ref[...]).astype(BF16)
        for hd in range(N_SG_HEADS):
            hs = slice(hd * hd_dim, (hd + 1) * hd_dim)
            bcol = bst_ref[:, hd:hd + 1]
            for ck in range(sub // CHUNK):
                rs = slice(ck * CHUNK, (ck + 1) * CHUNK)
                sv = _dot(ws_masked[hd], v[rs, hs]) + bcol
                cat_scr[base + ck * CHUNK:base + (ck + 1) * CHUNK,
                        d_pool + hd * hd_dim:d_pool + (hd + 1) * hd_dim] = (u[rs, hs] * sv).astype(BF16)
        return _dot(cat_scr[base:base + sub, :], wout_ref[...])

    hzs = [in_proj(st) for st in range(SUB_TILES)]
    mixes = [branches_out_proj(st, hzs[st]) for st in range(SUB_TILES)]
    for st in range(SUB_TILES):
        rs = slice(st * sub, (st + 1) * sub)
        o_ref[rs, :] = _layer_norm(ALPHA * x[rs, :] + mixes[st], g_ref[...], b_ref[...])
    a_scr[0:POOL_HALO, :] = a_scr[tm:tm + POOL_HALO, :]


def _even_mixer(x, pending, seq, layer, w_in, w_pool, pool_scale, ln_v_g, ln_v_b, w_spatial, b_spatial_t,
                w_out, ln_g, ln_b, ln_row):
    n, d = x.shape
    d_in = w_in.shape[2]
    d_pool = pool_scale.shape[2]
    d_sg = ln_v_g.shape[2]
    kern = functools.partial(_even_kernel, tiles_per_seq=seq // TM, layer=layer, pending=pending is not None)
    src_specs, src_args = _mixer_sources(x, pending)
    return pl.pallas_call(
        kern,
        out_shape=jax.ShapeDtypeStruct((n, d), F32),
        grid=(n // TM,),
        in_specs=src_specs + [
            _HBM,
            _pick_spec(w_pool, layer),
            _pick_spec(pool_scale, layer),
            _pick_spec(ln_v_g, layer),
            _pick_spec(ln_v_b, layer),
            _pick_spec(w_spatial, layer),
            _pick_spec(b_spatial_t, layer),
            _HBM,
            _pick_spec(ln_g, ln_row),
            _pick_spec(ln_b, ln_row),
        ],
        out_specs=pl.BlockSpec((TM, d), lambda i: (i, 0)),
        scratch_shapes=[pltpu.VMEM((POOL_HALO + TM, d_pool), F32), pltpu.VMEM((TM, d_pool + d_sg), BF16),
                        pltpu.VMEM((d, d_in), BF16), pltpu.VMEM((d_pool + d_sg, d), BF16)] + _stage_scratch(d),
        compiler_params=_params(),
        name="even_mixer",
    )(*src_args, w_in, w_pool, pool_scale, ln_v_g, ln_v_b, w_spatial, b_spatial_t, w_out, ln_g, ln_b)


def _odd_kernel(*refs, tiles_per_seq, layer, pending):
    n_src = 6 if pending else 1
    (win_hbm, cwt_ref, cb_ref, wout_hbm, g_ref, b_ref, o_ref, zc_scr,
     win_ref, wout_ref, stage, sems) = refs[n_src:]
    tm, d = o_ref.shape
    seq_tile = pl.program_id(0) % tiles_per_seq

    @pl.when(pl.program_id(0) == 0)
    def _():
        _load_cast(win_hbm.at[layer], win_ref, stage, sems)
        _load_cast(wout_hbm.at[layer], wout_ref, stage, sems)

    @pl.when(seq_tile == 0)
    def _():
        zc_scr[0:CONV_HALO, :] = jnp.zeros((CONV_HALO, d), F32)

    x = _mixer_input(refs[:n_src], pending)
    sub = tm // SUB_TILES

    def in_proj(st):
        xb = x[st * sub:(st + 1) * sub, :].astype(BF16)
        hc = _dot(xb, win_ref[:, d:2 * d])
        hz = _dot(xb, win_ref[:, 2 * d:])
        zc_scr[CONV_HALO + st * sub:CONV_HALO + (st + 1) * sub, :] = hc * hz
        return _dot(xb, win_ref[:, :d])

    def conv_out_proj(st, gate):
        conv = cb_ref[...]
        for j in range(CONV_WIDTH):
            off = CONV_HALO + st * sub - (CONV_WIDTH - 1) + j
            conv = conv + zc_scr[off:off + sub, :] * cwt_ref[j:j + 1, :]
        return _dot((gate * conv).astype(BF16), wout_ref[...])

    gates = [in_proj(st) for st in range(SUB_TILES)]
    ys = [conv_out_proj(st, gates[st]) for st in range(SUB_TILES)]
    for st in range(SUB_TILES):
        rs = slice(st * sub, (st + 1) * sub)
        o_ref[rs, :] = _layer_norm(ALPHA * x[rs, :] + ys[st], g_ref[...], b_ref[...])
    zc_scr[0:CONV_HALO, :] = zc_scr[tm:tm + CONV_HALO, :]


def _odd_mixer(x, pending, seq, layer, w_in, conv_w_t, conv_b, w_out, ln_g, ln_b, ln_row):
    n, d = x.shape
    kern = functools.partial(_odd_kernel, tiles_per_seq=seq // TM, layer=layer, pending=pending is not None)
    src_specs, src_args = _mixer_sources(x, pending)
    return pl.pallas_call(
        kern,
        out_shape=jax.ShapeDtypeStruct((n, d), F32),
        grid=(n // TM,),
        in_specs=src_specs + [
            _HBM,
            _pick_spec(conv_w_t, layer),
            _pick_spec(conv_b, layer),
            _HBM,
            _pick_spec(ln_g, ln_row),
            _pick_spec(ln_b, ln_row),
        ],
        out_specs=pl.BlockSpec((TM, d), lambda i: (i, 0)),
        scratch_shapes=[pltpu.VMEM((CONV_HALO + TM, d), F32),
                        pltpu.VMEM(w_in.shape[1:], BF16), pltpu.VMEM(w_out.shape[1:], BF16)] + _stage_scratch(d),
        compiler_params=_params(),
        name="odd_mixer",
    )(*src_args, w_in, conv_w_t, conv_b, w_out, ln_g, ln_b)


def _kv_kernel(mem_ref, wk_ref, wv_ref, k_ref, v_ref):
    m = mem_ref[...].astype(BF16)
    k_ref[...] = _dot(m, wk_ref[...].astype(BF16)).astype(BF16)
    v_ref[...] = _dot(m, wv_ref[...].astype(BF16)).astype(BF16)


def _memory_kv(mem2d, wk, wv, layer):
    rows, d = mem2d.shape
    out = jax.ShapeDtypeStruct((rows, d), BF16)
    return pl.pallas_call(
        _kv_kernel,
        out_shape=(out, out),
        grid=(1,),
        in_specs=[_const_spec((rows, d)), _pick_spec(wk, layer), _pick_spec(wv, layer)],
        out_specs=(_const_spec((rows, d)), _const_spec((rows, d))),
        compiler_params=_params(),
        name="memory_kv",
    )(mem2d, wk, wv)


def _attn_kernel(x_ref, k_ref, v_ref, wq_hbm, wo_hbm, g_ref, b_ref, wr_ref, br_ref,
                 o_ref, op_ref, rt_ref, rg_ref, cnt_ref, o_scr, carry_scr, wq_ref, wo_ref, stage, sems,
                 *, layer):
    tm, d = x_ref.shape
    hd_dim = d // N_XHEADS
    sub = tm // SUB_TILES

    @pl.when(pl.program_id(0) == 0)
    def _():
        carry_scr[...] = jnp.zeros_like(carry_scr)
        _load_cast(wq_hbm.at[layer], wq_ref, stage, sems)
        _load_cast(wo_hbm.at[layer], wo_ref, stage, sems)

    row_slices = [slice(st * sub, (st + 1) * sub) for st in range(SUB_TILES)]
    for rs in row_slices:
        q = _dot(x_ref[rs, :].astype(BF16), wq_ref[...]) * (1.0 / math.sqrt(hd_dim))
        for hd in range(N_XHEADS):
            hs = slice(hd * hd_dim, (hd + 1) * hd_dim)
            s = lax.dot_general(q[:, hs].astype(BF16), k_ref[:, hs], _NT, preferred_element_type=F32)
            p = jnp.exp(s - jnp.max(s, axis=-1, keepdims=True))
            p = p * (1.0 / jnp.sum(p, axis=-1, keepdims=True))
            o_scr[rs, hs] = _dot(p.astype(BF16), v_ref[:, hs]).astype(BF16)
    xas = [_dot(o_scr[rs, :], wo_ref[...]) for rs in row_slices]
    for rs, xa in zip(row_slices, xas):
        out = _layer_norm(ALPHA * x_ref[rs, :] + xa, g_ref[...], b_ref[...])
        o_ref[rs, :] = out
        op_ref[rs, :] = _pack_halves(out)
        table, gates = _route_rows(out, wr_ref, br_ref, carry_scr)
        rt_ref[:, rs] = table
        rg_ref[rs, :] = gates
    cnt_ref[...] = carry_scr[...].astype(I32)


def _cross_attn(x, seq, layer, k, v, wq, wo, ln_g, ln_b, ln_row, router_w, router_b):
    n, d = x.shape
    m = k.shape[1]
    tiles_per_seq = seq // TM
    kvspec = pl.BlockSpec((None, m, d), lambda i: (i // tiles_per_seq, 0, 0))
    out, packed, table, gates, cnt = pl.pallas_call(
        functools.partial(_attn_kernel, layer=layer),
        out_shape=(jax.ShapeDtypeStruct((n, d), F32), jax.ShapeDtypeStruct((n, d // 2), I32),
                   jax.ShapeDtypeStruct((ROUTE_ROWS, n), F32), jax.ShapeDtypeStruct((n, TOP_K), F32),
                   jax.ShapeDtypeStruct((N_EXPERTS, LANES), I32)),
        grid=(n // TM,),
        in_specs=[
            pl.BlockSpec((TM, d), lambda i: (i, 0)),
            kvspec, kvspec,
            _HBM, _HBM,
            _pick_spec(ln_g, ln_row), _pick_spec(ln_b, ln_row),
            _pick_spec(router_w, layer), _pick_spec(router_b, layer),
        ],
        out_specs=(pl.BlockSpec((TM, d), lambda i: (i, 0)), pl.BlockSpec((TM, d // 2), lambda i: (i, 0)),
                   pl.BlockSpec((ROUTE_ROWS, TM), lambda i: (0, i)), pl.BlockSpec((TM, TOP_K), lambda i: (i, 0)),
                   _const_spec((N_EXPERTS, LANES))),
        scratch_shapes=[pltpu.VMEM((TM, d), BF16), pltpu.VMEM((N_EXPERTS, LANES), F32),
                        pltpu.VMEM((d, d), BF16), pltpu.VMEM((d, d), BF16)] + _stage_scratch(d),
        compiler_params=_params(),
        name="cross_attn",
    )(x, k, v, wq, wo, ln_g, ln_b, router_w, router_b)
    return out, packed, table, gates, cnt[:, 0]


def _route_rows(x, w_ref, bias_ref, carry_scr):
    tm = x.shape[0]
    neg = -jnp.inf
    logits = _dot(x.astype(BF16), w_ref[...]) + bias_ref[...]
    lt = jnp.transpose(logits)

    def first_argmax(vals):
        rows = lax.broadcasted_iota(I32, vals.shape, 0).astype(F32)
        mx = jnp.max(vals, axis=0, keepdims=True)
        idx = jnp.min(jnp.where(vals == mx, rows, float(vals.shape[0])), axis=0, keepdims=True)
        return mx, idx, rows

    gl = lt[0:N_GROUPS, :]
    gmax, g_sel, _ = first_argmax(gl)
    gate_g = 1.0 / jnp.sum(jnp.exp(gl - gmax), axis=0, keepdims=True)

    el = lt[EXPERT_ROW0:EXPERT_ROW0 + EXPERTS_PER_GROUP, :]
    for g in range(1, N_GROUPS):
        lo = EXPERT_ROW0 + g * EXPERTS_PER_GROUP
        el = jnp.where(g_sel == float(g), lt[lo:lo + EXPERTS_PER_GROUP, :], el)
    m1, i1, erow = first_argmax(el)
    m2, i2, _ = first_argmax(jnp.where(erow == i1, neg, el))
    e21 = jnp.exp(m2 - m1)
    w1 = 1.0 / (1.0 + e21)
    w2 = e21 / (1.0 + e21)
    e1 = g_sel * EXPERTS_PER_GROUP + i1
    e2 = g_sel * EXPERTS_PER_GROUP + i2

    xrow = lax.broadcasted_iota(I32, (N_EXPERTS, tm), 0).astype(F32)
    oh1 = xrow == e1
    oh2 = xrow == e2
    oh = (oh1 | oh2).astype(BF16)
    r = lax.broadcasted_iota(I32, (tm, tm), 0)
    c = lax.broadcasted_iota(I32, (tm, tm), 1)
    before = _dot(oh, (r < c).astype(BF16)) + carry_scr[:, 0:1]
    rank1 = jnp.sum(jnp.where(oh1, before, 0.0), axis=0, keepdims=True)
    rank2 = jnp.sum(jnp.where(oh2, before, 0.0), axis=0, keepdims=True)
    carry_scr[...] += jnp.sum(oh.astype(F32), axis=1, keepdims=True)

    trow = lax.broadcasted_iota(I32, (ROUTE_ROWS, tm), 0)
    table = jnp.where(trow == 0, e1, jnp.where(trow == 1, e2, jnp.where(trow == 2, rank1,
                                                                      jnp.where(trow == 3, rank2, 0.0))))
    grow = lax.broadcasted_iota(I32, (LANES, tm), 0)
    gates = jnp.transpose(jnp.where(grow == 0, gate_g * w1, jnp.where(grow == 1, gate_g * w2, 0.0)))[:, :TOP_K]
    return table, gates


def _router_weights(wr_g, br_g, wr_e, br_e):
    nl, d, _ = wr_g.shape
    gap = EXPERT_ROW0 - N_GROUPS
    w = jnp.concatenate([wr_g, jnp.zeros((nl, d, gap), F32),
                         jnp.transpose(wr_e, (0, 2, 1, 3)).reshape(nl, d, N_EXPERTS)], axis=2)
    w = jnp.pad(w, ((0, 0), (0, 0), (0, ROUTE_COLS - w.shape[2]))).astype(BF16)
    bias = jnp.concatenate([br_g, jnp.zeros((nl, gap), F32), br_e.reshape(nl, -1)], axis=1)
    bias = jnp.pad(bias, ((0, 0), (0, ROUTE_COLS - bias.shape[1])))
    return w, bias[:, None, :]


def _sc_worker_rows(rows):
    per_worker = rows // SC_WORKERS
    n_chunks = per_worker // SC_CHUNK
    assert per_worker * SC_WORKERS == rows and n_chunks * SC_CHUNK == per_worker and n_chunks % 2 == 0
    return per_worker, n_chunks


def _sc_gather_rows(table_hbm, out_hbm, idx_v, rows_v, gsem, wsem, base, n_chunks):
    def fetch(c, slot):
        off = pl.multiple_of(c * SC_CHUNK, SC_CHUNK)
        return pltpu.make_async_copy(table_hbm.at[idx_v.at[pl.ds(off, SC_CHUNK)]], rows_v.at[slot],
                                     gsem.at[slot])

    def put(c, slot):
        off = pl.multiple_of(c * SC_CHUNK, SC_CHUNK)
        return pltpu.make_async_copy(rows_v.at[slot], out_hbm.at[pl.ds(base + off, SC_CHUNK)], wsem.at[slot])

    fetch(0, 0).start()

    @pl.loop(0, n_chunks, step=2)
    def _(c0):
        for slot in range(2):
            c = c0 + slot

            @pl.when(c + 1 < n_chunks)
            def _():
                @pl.when(c >= 1)
                def _():
                    put(c - 1, 1 - slot).wait()
                fetch(c + 1, 1 - slot).start()

            fetch(c, slot).wait()
            put(c, slot).start()

    put(n_chunks - 2, 0).wait()
    put(n_chunks - 1, 1).wait()


def _sc_row_scratch(per_worker, d, dtype):
    return [pltpu.VMEM((per_worker,), I32), pltpu.VMEM((2, SC_CHUNK, d), dtype),
            pltpu.SemaphoreType.DMA((2,)), pltpu.SemaphoreType.DMA((2,))]


def _sc_gather(table, idx):
    b = idx.shape[0]
    d = table.shape[1]
    per_worker, n_chunks = _sc_worker_rows(b)
    mesh = plsc.VectorSubcoreMesh(core_axis_name="c", subcore_axis_name="s")

    @functools.partial(
        pl.kernel, mesh=mesh,
        out_type=jax.ShapeDtypeStruct((b, d), table.dtype),
        scratch_types=_sc_row_scratch(per_worker, d, table.dtype),
        name="sc_gather",
    )
    def gather(table_hbm, idx_hbm, out_hbm, idx_v, rows_v, gsem, wsem):
        base = (lax.axis_index("s") * SC_CORES + lax.axis_index("c")) * per_worker
        pltpu.sync_copy(idx_hbm.at[pl.ds(base, per_worker)], idx_v)
        _sc_gather_rows(table_hbm, out_hbm, idx_v, rows_v, gsem, wsem, base, n_chunks)

    return gather(table, idx)


def _sc_dispatch(table, dest_flat, rows):
    n, d = table.shape
    top_k = dest_flat.shape[0] // n
    per_worker, n_chunks = _sc_worker_rows(n)
    mesh = plsc.VectorSubcoreMesh(core_axis_name="c", subcore_axis_name="s")

    @functools.partial(
        pl.kernel, mesh=mesh,
        out_type=jax.ShapeDtypeStruct((rows, d), table.dtype),
        scratch_types=[pltpu.VMEM((per_worker,), I32)] * top_k + [
            pltpu.VMEM((2, SC_CHUNK, d), table.dtype),
            pltpu.SemaphoreType.DMA((2,)), pltpu.SemaphoreType.DMA((2, top_k))],
        name="sc_dispatch",
    )
    def dispatch(table_hbm, dest_hbm, out_hbm, *scratch):
        dest_v, (rows_v, gsem, wsem) = scratch[:top_k], scratch[top_k:]
        base = (lax.axis_index("s") * SC_CORES + lax.axis_index("c")) * per_worker
        for k in range(top_k):
            pltpu.sync_copy(dest_hbm.at[pl.ds(k * n + base, per_worker)], dest_v[k])

        def fetch(c, slot):
            off = pl.multiple_of(c * SC_CHUNK, SC_CHUNK)
            return pltpu.make_async_copy(table_hbm.at[pl.ds(base + off, SC_CHUNK)], rows_v.at[slot],
                                         gsem.at[slot])

        def puts(c, slot):
            off = pl.multiple_of(c * SC_CHUNK, SC_CHUNK)
            return [pltpu.make_async_copy(rows_v.at[slot], out_hbm.at[dest_v[k].at[pl.ds(off, SC_CHUNK)]],
                                          wsem.at[slot, k]) for k in range(top_k)]

        fetch(0, 0).start()

        @pl.loop(0, n_chunks, step=2)
        def _(c0):
            for slot in range(2):
                c = c0 + slot

                @pl.when(c + 1 < n_chunks)
                def _():
                    @pl.when(c >= 1)
                    def _():
                        for p in puts(c - 1, 1 - slot):
                            p.wait()
                    fetch(c + 1, 1 - slot).start()

                fetch(c, slot).wait()
                for p in puts(c, slot):
                    p.start()

        for p in puts(n_chunks - 2, 0) + puts(n_chunks - 1, 1):
            p.wait()

    return dispatch(table, dest_flat)


def _expert_kernel(sched_ref, ni_ref, xs_hbm, w1_hbm, w3_hbm, w2_hbm, y_hbm,
                   x_buf, y_buf, w1_buf, w3_buf, w2_buf, w1_scr, w3_scr, w2_scr, wsems, xsems, ysems,
                   *, layer):
    i = pl.program_id(0)
    n_items = ni_ref[0]
    expert, wslot, run_start, next_expert = (sched_ref[r, i] for r in range(4))
    slot = i % 2

    def for_units(item, fn):
        for units in range(1, ITEM_UNITS + 1):
            @pl.when(sched_ref[5, item] == units)
            def _():
                fn(units * EXPERT_UNIT)

    def x_copy(item, s, rows):
        row0 = pl.multiple_of(sched_ref[4, item], EXPERT_UNIT)
        return pltpu.make_async_copy(xs_hbm.at[pl.ds(row0, rows)], x_buf.at[s, pl.ds(0, rows)], xsems.at[s])

    def y_copy(item, s, rows):
        row0 = pl.multiple_of(sched_ref[4, item], EXPERT_UNIT)
        return pltpu.make_async_copy(y_buf.at[s, pl.ds(0, rows)], y_hbm.at[pl.ds(row0, rows)], ysems.at[s])

    def fetch(e, s):
        return [pltpu.make_async_copy(w_hbm.at[layer, e], buf.at[s], wsems.at[s, j])
                for j, (w_hbm, buf) in enumerate(((w1_hbm, w1_buf), (w3_hbm, w3_buf), (w2_hbm, w2_buf)))]

    def mlp(rows):
        x_lo, x_hi = _unpack_halves(x_buf[slot, 0:rows, :])
        xb = jnp.concatenate([x_lo.astype(BF16), x_hi.astype(BF16)], axis=1)
        h1 = _dot(xb, w1_scr[...])
        h3 = _dot(xb, w3_scr[...])
        hid = h1 * (1.0 / (1.0 + jnp.exp(-h1))) * h3
        y_buf[slot, 0:rows, :] = _pack_halves(_dot(hid.astype(BF16), w2_scr[...]))
        y_copy(i, slot, rows).start()

    @pl.when(i < n_items)
    def _():
        @pl.when(i == 0)
        def _():
            for_units(0, lambda rows: x_copy(0, 0, rows).start())

        for_units(i, lambda rows: x_copy(i, slot, rows).wait())

        @pl.when(i + 1 < n_items)
        def _():
            for_units(i + 1, lambda rows: x_copy(i + 1, 1 - slot, rows).start())

        @pl.when(run_start == 1)
        def _():
            @pl.when(i == 0)
            def _():
                for c in fetch(expert, wslot):
                    c.start()

            for c in fetch(expert, wslot):
                c.wait()

            @pl.when(next_expert >= 0)
            def _():
                for c in fetch(next_expert, 1 - wslot):
                    c.start(priority=1)

            w1_scr[...] = w1_buf[wslot].astype(BF16)
            w3_scr[...] = w3_buf[wslot].astype(BF16)
            w2_scr[...] = w2_buf[wslot].astype(BF16)

        @pl.when(i >= 2)
        def _():
            for_units(i - 2, lambda rows: y_copy(i - 2, slot, rows).wait())

        for_units(i, mlp)

        @pl.when(i == n_items - 1)
        def _():
            @pl.when(i >= 1)
            def _():
                for_units(i - 1, lambda rows: y_copy(i - 1, 1 - slot, rows).wait())

            for_units(i, lambda rows: y_copy(i, slot, rows).wait())
            y_buf[0, 0:EXPERT_UNIT, :] = jnp.zeros((EXPERT_UNIT, y_buf.shape[2]), y_buf.dtype)
            first_free = (sched_ref[4, i] + sched_ref[5, i] * EXPERT_UNIT) // EXPERT_UNIT

            def zero_copy(u):
                return pltpu.make_async_copy(
                    y_buf.at[0, pl.ds(0, EXPERT_UNIT)],
                    y_hbm.at[pl.ds(pl.multiple_of(u * EXPERT_UNIT, EXPERT_UNIT), EXPERT_UNIT)], ysems.at[0])

            def start_zero(u, _):
                zero_copy(u).start()
                return 0

            def wait_zero(u, _):
                zero_copy(u).wait()
                return 0

            lax.fori_loop(first_free, y_hbm.shape[0] // EXPERT_UNIT, start_zero, 0)
            lax.fori_loop(first_free, y_hbm.shape[0] // EXPERT_UNIT, wait_zero, 0)


def _expert_mlp(xs, schedule, n_items, layer, w1, w3, w2):
    d, de = w1.shape[2], w1.shape[3]
    max_rows = ITEM_UNITS * EXPERT_UNIT
    return pl.pallas_call(
        functools.partial(_expert_kernel, layer=layer),
        out_shape=jax.ShapeDtypeStruct(xs.shape, I32),
        grid_spec=pltpu.PrefetchScalarGridSpec(
            num_scalar_prefetch=2,
            grid=(schedule.shape[1],),
            in_specs=[_HBM, _HBM, _HBM, _HBM],
            out_specs=_HBM,
            scratch_shapes=[pltpu.VMEM((2, max_rows, d // 2), I32), pltpu.VMEM((2, max_rows, d // 2), I32),
                            pltpu.VMEM((2, d, de), F32), pltpu.VMEM((2, d, de), F32), pltpu.VMEM((2, de, d), F32),
                            pltpu.VMEM((d, de), BF16), pltpu.VMEM((d, de), BF16), pltpu.VMEM((de, d), BF16),
                            pltpu.SemaphoreType.DMA((2, 3)), pltpu.SemaphoreType.DMA((2,)),
                            pltpu.SemaphoreType.DMA((2,))],
        ),
        compiler_params=_params(),
        name="expert_mlp",
    )(schedule, n_items, xs, w1, w3, w2)


def _combine_kernel(x_ref, y0_ref, y1_ref, gate_ref, g_ref, b_ref, o_ref):
    o_ref[...] = _moe_output(x_ref, y0_ref, y1_ref, gate_ref, g_ref, b_ref)


def _combine(x, yg, gates, ln_g, ln_b, ln_row):
    n, d = x.shape
    tiles = n // TM
    return pl.pallas_call(
        _combine_kernel,
        out_shape=jax.ShapeDtypeStruct((n, d), F32),
        grid=(tiles,),
        in_specs=[pl.BlockSpec((TM, d), lambda i: (i, 0)),
                  pl.BlockSpec((TM, d // 2), lambda i: (i, 0)),
                  pl.BlockSpec((TM, d // 2), lambda i: (i + tiles, 0)),
                  pl.BlockSpec((TM, TOP_K), lambda i: (i, 0)),
                  _pick_spec(ln_g, ln_row), _pick_spec(ln_b, ln_row)],
        out_specs=pl.BlockSpec((TM, d), lambda i: (i, 0)),
        compiler_params=_params(),
        name="combine",
    )(x, yg, yg, gates, ln_g, ln_b)


def _moe_experts(x_packed, table, counts, layer, w1, w3, w2):
    n = x_packed.shape[0]
    max_units = (n * TOP_K + N_EXPERTS * (EXPERT_UNIT - 1) + EXPERT_UNIT - 1) // EXPERT_UNIT
    max_items = (max_units + N_EXPERTS * (ITEM_UNITS - 1) + ITEM_UNITS - 1) // ITEM_UNITS
    experts = table[:TOP_K].astype(I32)
    ranks = table[TOP_K:2 * TOP_K].astype(I32)

    units_e = (counts + EXPERT_UNIT - 1) // EXPERT_UNIT
    units_start = jnp.cumsum(units_e) - units_e
    items_e = (units_e + ITEM_UNITS - 1) // ITEM_UNITS
    items_end = jnp.cumsum(items_e)
    n_items = items_end[-1:].astype(I32)
    item_ids = jnp.arange(max_items, dtype=I32)
    expert_ids = jnp.arange(N_EXPERTS, dtype=I32)
    item_expert = jnp.minimum(jnp.sum(items_end[None, :] <= item_ids[:, None], axis=1), N_EXPERTS - 1)
    later = (expert_ids[None, :] > expert_ids[:, None]) & (items_e[None, :] > 0)
    next_run = jnp.min(jnp.where(later, expert_ids[None, :], N_EXPERTS), axis=1)
    next_run = jnp.where(next_run == N_EXPERTS, -1, next_run)
    per_expert = jnp.stack([items_end - items_e, units_start, units_e, next_run], axis=1)
    mine = (item_expert[:, None] == expert_ids[None, :])[:, :, None]
    first_item, unit0, units, next_expert = jnp.sum(jnp.where(mine, per_expert[None], 0), axis=1).T
    within = item_ids - first_item
    item_row0 = (unit0 + ITEM_UNITS * within) * EXPERT_UNIT
    item_units = jnp.clip(units - ITEM_UNITS * within, 1, ITEM_UNITS)
    run_start_flag = ((within == 0) & (item_ids < n_items[0])).astype(I32)
    slot = (jnp.cumsum(run_start_flag) - 1) % 2
    schedule = jnp.stack([item_expert, slot, run_start_flag, next_expert, item_row0, item_units]).astype(I32)
    start_of = jnp.sum(jnp.where(experts[:, :, None] == expert_ids, units_start * EXPERT_UNIT, 0), axis=-1)
    dest = (start_of + ranks).astype(I32).reshape(-1)
    xs = _sc_dispatch(x_packed, dest, max_units * EXPERT_UNIT)
    y = _expert_mlp(xs, schedule, n_items, layer, w1, w3, w2)
    return _sc_gather(y, dest)


def kernel(x, mem, w_in_even, w_pool, pool_scale, ln_v_g, ln_v_b, w_spatial, b_spatial, w_out_even,
           w_in_odd, conv_w, conv_b, w_out_odd, wq_x, wk_x, wv_x, wo_x, ln_g, ln_b, wr_group,
           br_group, wr_expert, br_expert, w1, w3, w2):
    bsz, seq, d = x.shape
    assert seq % TM == 0 and d % LANES == 0
    mlen = mem.shape[1]
    mem2d = mem.reshape(bsz * mlen, d)
    ln_g = ln_g.reshape(DEPTH * 3, 1, d)
    ln_b = ln_b.reshape(DEPTH * 3, 1, d)
    w_pool = w_pool.astype(BF16)
    pool_scale, ln_v_g, ln_v_b, conv_b = (p[:, None, :] for p in (pool_scale, ln_v_g, ln_v_b, conv_b))
    b_spatial_t = jnp.swapaxes(b_spatial, 1, 2)
    conv_w_t = jnp.swapaxes(conv_w, 1, 2)
    router_w, router_b = _router_weights(wr_group, br_group, wr_expert, br_expert)

    h = x.reshape(bsz * seq, d)
    pending = None
    kv = _memory_kv(mem2d, wk_x, wv_x, 0)
    for l in range(DEPTH):
        i = l // 2
        k, v = (a.reshape(bsz, mlen, d) for a in kv)
        if l % 2 == 0:
            h = _even_mixer(h, pending, seq, i, w_in_even, w_pool, pool_scale, ln_v_g, ln_v_b,
                            w_spatial, b_spatial_t, w_out_even, ln_g, ln_b, 3 * l)
        else:
            h = _odd_mixer(h, pending, seq, i, w_in_odd, conv_w_t, conv_b, w_out_odd, ln_g, ln_b, 3 * l)
        h, hp, table, gates, counts = _cross_attn(h, seq, l, k, v, wq_x, wo_x,
                                                  ln_g, ln_b, 3 * l + 1, router_w, router_b)
        if l + 1 < DEPTH:
            kv = _memory_kv(mem2d, wk_x, wv_x, l + 1)
        yg = _moe_experts(hp, table, counts, l, w1, w3, w2)
        pending = (yg, gates, ln_g, ln_b, 3 * l + 2)
    return _combine(h, *pending).reshape(bsz, seq, d)
```

```python
import functools
import math

import jax
import jax.numpy as jnp
from jax import lax
from jax.experimental import pallas as pl
from jax.experimental.pallas import tpu as pltpu
from jax.experimental.pallas import tpu_sc as plsc

F32 = jnp.float32
BF16 = jnp.bfloat16
I32 = jnp.int32

POOL_WINDOWS = (2, 4, 8, 16)
assert all(w & (w - 1) == 0 for w in POOL_WINDOWS)
N_SG_HEADS = 4
CHUNK = 128
CONV_WIDTH = 3
N_XHEADS = 4
N_GROUPS = 4
EXPERTS_PER_GROUP = 8
N_EXPERTS = N_GROUPS * EXPERTS_PER_GROUP
TOP_K = 2
DEPTH = 4
ALPHA = (2.0 * DEPTH) ** 0.25
LN_EPS = 1e-5

LANES = 128
SC_CORES = 2
SC_WORKERS = 32
SC_LANES = 16
VMEM_LIMIT = 56 * 1024 * 1024

SC_CHUNK = 64
TM = 1024
SUB_TILES = 2
ROUTE_ROWS = 8
POOL_HALO = 16
CONV_HALO = 8
EXPERT_UNIT = 128
ITEM_UNITS = 4
ROUTE_COLS = 128
EXPERT_ROW0 = 8
STAGE_COLS = 512

_NT = (((1,), (1,)), ((), ()))


def _dot(a, b):
    return jnp.dot(a, b, preferred_element_type=F32)


def _layer_norm(y, g, b):
    mu = jnp.mean(y, axis=-1, keepdims=True)
    yc = y - mu
    var = jnp.mean(yc * yc, axis=-1, keepdims=True)
    return yc * lax.rsqrt(var + LN_EPS) * g + b


def _gelu_tanh(x):
    c = math.sqrt(2.0 / math.pi)
    return 0.5 * x * (1.0 + jnp.tanh(c * (x + 0.044715 * (x * x * x))))


def _pack_halves(v):
    c = v.shape[1] // 2
    lo = pltpu.bitcast(v[:, :c].astype(BF16).astype(F32), jnp.uint32)
    hi = pltpu.bitcast(v[:, c:].astype(BF16).astype(F32), jnp.uint32)
    return pltpu.bitcast((hi & jnp.uint32(0xFFFF0000)) | (lo >> 16), I32)


def _unpack_halves(w):
    u = pltpu.bitcast(w, jnp.uint32)
    return pltpu.bitcast(u << 16, F32), pltpu.bitcast(u & jnp.uint32(0xFFFF0000), F32)


def _load_cast(w_hbm, w_scr, stage, sems):
    chunks = w_scr.shape[1] // STAGE_COLS

    def chunk_copy(c):
        return pltpu.make_async_copy(w_hbm.at[:, pl.ds(c * STAGE_COLS, STAGE_COLS)], stage.at[c % 2],
                                     sems.at[c % 2])

    chunk_copy(0).start()
    for c in range(chunks):
        if c + 1 < chunks:
            chunk_copy(c + 1).start()
        chunk_copy(c).wait()
        w_scr[:, c * STAGE_COLS:(c + 1) * STAGE_COLS] = stage[c % 2].astype(BF16)


def _stage_scratch(rows):
    return [pltpu.VMEM((2, rows, STAGE_COLS), F32), pltpu.SemaphoreType.DMA((2,))]


_HBM = pl.BlockSpec(memory_space=pl.ANY)


def _const_spec(shape):
    nd = len(shape)
    return pl.BlockSpec(shape, lambda i: (0,) * nd)


def _pick_spec(stacked, index):
    rest = stacked.shape[1:]
    return pl.BlockSpec((None,) + rest, lambda i: (index,) + (0,) * len(rest))


def _params():
    return pltpu.CompilerParams(dimension_semantics=("arbitrary",), vmem_limit_bytes=VMEM_LIMIT)


def _moe_output(x_ref, y0_ref, y1_ref, gate_ref, g_ref, b_ref):
    gates = gate_ref[...]
    g0, g1 = gates[:, 0:1], gates[:, 1:2]
    y0_lo, y0_hi = _unpack_halves(y0_ref[...])
    y1_lo, y1_hi = _unpack_halves(y1_ref[...])
    ff = jnp.concatenate([g0 * y0_lo + g1 * y1_lo, g0 * y0_hi + g1 * y1_hi], axis=1)
    return _layer_norm(ALPHA * x_ref[...] + ff, g_ref[...], b_ref[...])


def _mixer_input(src, pending):
    if not pending:
        return src[0][...]
    return _moe_output(*src)


def _mixer_sources(x, pending):
    n, d = x.shape
    tiles = n // TM
    specs = [pl.BlockSpec((TM, d), lambda i: (i, 0))]
    args = [x]
    if pending is not None:
        yg, gates, ln_g, ln_b, ln_row = pending
        specs += [pl.BlockSpec((TM, d // 2), lambda i: (i, 0)),
                  pl.BlockSpec((TM, d // 2), lambda i: (i + tiles, 0)),
                  pl.BlockSpec((TM, TOP_K), lambda i: (i, 0)),
                  _pick_spec(ln_g, ln_row), _pick_spec(ln_b, ln_row)]
        args += [yg, yg, gates, ln_g, ln_b]
    return specs, args


def _even_kernel(*refs, tiles_per_seq, layer, pending):
    n_src = 6 if pending else 1
    (win_hbm, wpool_ref, pscale_ref, lvg_ref, lvb_ref, ws_ref, bst_ref, wout_hbm, g_ref, b_ref, o_ref,
     a_scr, cat_scr, win_ref, wout_ref, stage, sems) = refs[n_src:]
    tm = o_ref.shape[0]
    d_pool = a_scr.shape[1]
    d_sg = lvg_ref.shape[1]
    pgd = d_pool // len(POOL_WINDOWS)
    hd_dim = d_sg // N_SG_HEADS
    seq_tile = pl.program_id(0) % tiles_per_seq

    @pl.when(pl.program_id(0) == 0)
    def _():
        _load_cast(win_hbm.at[layer], win_ref, stage, sems)
        _load_cast(wout_hbm.at[layer], wout_ref, stage, sems)

    @pl.when(seq_tile == 0)
    def _():
        a_scr[0:POOL_HALO, :] = jnp.zeros((POOL_HALO, d_pool), F32)

    x = _mixer_input(refs[:n_src], pending)
    sub = tm // SUB_TILES
    row = lax.broadcasted_iota(I32, (CHUNK, CHUNK), 0)
    col = lax.broadcasted_iota(I32, (CHUNK, CHUNK), 1)
    ws_masked = [jnp.where(row >= col, ws_ref[hd], 0.0).astype(BF16) for hd in range(N_SG_HEADS)]

    def in_proj(st):
        h = _dot(x[st * sub:(st + 1) * sub, :].astype(BF16), win_ref[...])
        a_scr[POOL_HALO + st * sub:POOL_HALO + (st + 1) * sub, :] = h[:, :d_pool]
        return h[:, d_pool:]

    def branches_out_proj(st, hz):
        base = st * sub
        pos = seq_tile * tm + base + lax.broadcasted_iota(I32, (sub, 1), 0)
        for g, w in enumerate(POOL_WINDOWS):
            cs = slice(g * pgd, (g + 1) * pgd)
            tok = a_scr[POOL_HALO + base:POOL_HALO + base + sub, cs]
            acc = a_scr[base:POOL_HALO + base + sub, cs]
            span = 1
            while span < w:
                acc = acc[span:, :] + acc[:-span, :]
                span *= 2
            acc = acc[acc.shape[0] - sub:, :]
            cnt = jnp.minimum(pos + 1, w).astype(F32)
            dev = acc * (1.0 / cnt) - tok
            yg = _dot(dev.astype(BF16), wpool_ref[g])
            cat_scr[base:base + sub, cs] = (yg * pscale_ref[:, cs]).astype(BF16)

        z = _gelu_tanh(hz)
        u = z[:, :d_sg]
        v = _layer_norm(z[:, d_sg:], lvg_ref[...], lvb_ref[...]).astype(BF16)
        for hd in range(N_SG_HEADS):
            hs = slice(hd * hd_dim, (hd + 1) * hd_dim)
            bcol = bst_ref[:, hd:hd + 1]
            for ck in range(sub // CHUNK):
                rs = slice(ck * CHUNK, (ck + 1) * CHUNK)
                sv = _dot(ws_masked[hd], v[rs, hs]) + bcol
                cat_scr[base + ck * CHUNK:base + (ck + 1) * CHUNK,
                        d_pool + hd * hd_dim:d_pool + (hd + 1) * hd_dim] = (u[rs, hs] * sv).astype(BF16)
        return _dot(cat_scr[base:base + sub, :], wout_ref[...])

    hzs = [in_proj(st) for st in range(SUB_TILES)]
    mixes = [branches_out_proj(st, hzs[st]) for st in range(SUB_TILES)]
    for st in range(SUB_TILES):
        rs = slice(st * sub, (st + 1) * sub)
        o_ref[rs, :] = _layer_norm(ALPHA * x[rs, :] + mixes[st], g_ref[...], b_ref[...])
    a_scr[0:POOL_HALO, :] = a_scr[tm:tm + POOL_HALO, :]


def _even_mixer(x, pending, seq, layer, w_in, w_pool, pool_scale, ln_v_g, ln_v_b, w_spatial, b_spatial_t,
                w_out, ln_g, ln_b, ln_row):
    n, d = x.shape
    d_in = w_in.shape[2]
    d_pool = pool_scale.shape[2]
    d_sg = ln_v_g.shape[2]
    kern = functools.partial(_even_kernel, tiles_per_seq=seq // TM, layer=layer, pending=pending is not None)
    src_specs, src_args = _mixer_sources(x, pending)
    return pl.pallas_call(
        kern,
        out_shape=jax.ShapeDtypeStruct((n, d), F32),
        grid=(n // TM,),
        in_specs=src_specs + [
            _HBM,
            _pick_spec(w_pool, layer),
            _pick_spec(pool_scale, layer),
            _pick_spec(ln_v_g, layer),
            _pick_spec(ln_v_b, layer),
            _pick_spec(w_spatial, layer),
            _pick_spec(b_spatial_t, layer),
            _HBM,
            _pick_spec(ln_g, ln_row),
            _pick_spec(ln_b, ln_row),
        ],
        out_specs=pl.BlockSpec((TM, d), lambda i: (i, 0)),
        scratch_shapes=[pltpu.VMEM((POOL_HALO + TM, d_pool), F32), pltpu.VMEM((TM, d_pool + d_sg), BF16),
                        pltpu.VMEM((d, d_in), BF16), pltpu.VMEM((d_pool + d_sg, d), BF16)] + _stage_scratch(d),
        compiler_params=_params(),
        name="even_mixer",
    )(*src_args, w_in, w_pool, pool_scale, ln_v_g, ln_v_b, w_spatial, b_spatial_t, w_out, ln_g, ln_b)


def _odd_kernel(*refs, tiles_per_seq, layer, pending):
    n_src = 6 if pending else 1
    (win_hbm, cwt_ref, cb_ref, wout_hbm, g_ref, b_ref, o_ref, zc_scr,
     win_ref, wout_ref, stage, sems) = refs[n_src:]
    tm, d = o_ref.shape
    seq_tile = pl.program_id(0) % tiles_per_seq

    @pl.when(pl.program_id(0) == 0)
    def _():
        _load_cast(win_hbm.at[layer], win_ref, stage, sems)
        _load_cast(wout_hbm.at[layer], wout_ref, stage, sems)

    @pl.when(seq_tile == 0)
    def _():
        zc_scr[0:CONV_HALO, :] = jnp.zeros((CONV_HALO, d), F32)

    x = _mixer_input(refs[:n_src], pending)
    sub = tm // SUB_TILES

    def in_proj(st):
        xb = x[st * sub:(st + 1) * sub, :].astype(BF16)
        hc = _dot(xb, win_ref[:, d:2 * d])
        hz = _dot(xb, win_ref[:, 2 * d:])
        zc_scr[CONV_HALO + st * sub:CONV_HALO + (st + 1) * sub, :] = hc * hz
        return _dot(xb, win_ref[:, :d])

    def conv_out_proj(st, gate):
        conv = cb_ref[...]
        for j in range(CONV_WIDTH):
            off = CONV_HALO + st * sub - (CONV_WIDTH - 1) + j
            conv = conv + zc_scr[off:off + sub, :] * cwt_ref[j:j + 1, :]
        return _dot((gate * conv).astype(BF16), wout_ref[...])

    gates = [in_proj(st) for st in range(SUB_TILES)]
    ys = [conv_out_proj(st, gates[st]) for st in range(SUB_TILES)]
    for st in range(SUB_TILES):
        rs = slice(st * sub, (st + 1) * sub)
        o_ref[rs, :] = _layer_norm(ALPHA * x[rs, :] + ys[st], g_ref[...], b_ref[...])
    zc_scr[0:CONV_HALO, :] = zc_scr[tm:tm + CONV_HALO, :]


def _odd_mixer(x, pending, seq, layer, w_in, conv_w_t, conv_b, w_out, ln_g, ln_b, ln_row):
    n, d = x.shape
    kern = functools.partial(_odd_kernel, tiles_per_seq=seq // TM, layer=layer, pending=pending is not None)
    src_specs, src_args = _mixer_sources(x, pending)
    return pl.pallas_call(
        kern,
        out_shape=jax.ShapeDtypeStruct((n, d), F32),
        grid=(n // TM,),
        in_specs=src_specs + [
            _HBM,
            _pick_spec(conv_w_t, layer),
            _pick_spec(conv_b, layer),
            _HBM,
            _pick_spec(ln_g, ln_row),
            _pick_spec(ln_b, ln_row),
        ],
        out_specs=pl.BlockSpec((TM, d), lambda i: (i, 0)),
        scratch_shapes=[pltpu.VMEM((CONV_HALO + TM, d), F32),
                        pltpu.VMEM(w_in.shape[1:], BF16), pltpu.VMEM(w_out.shape[1:], BF16)] + _stage_scratch(d),
        compiler_params=_params(),
        name="odd_mixer",
    )(*src_args, w_in, conv_w_t, conv_b, w_out, ln_g, ln_b)


def _kv_kernel(mem_ref, wk_ref, wv_ref, k_ref, v_ref):
    m = mem_ref[...].astype(BF16)
    k_ref[...] = _dot(m, wk_ref[...].astype(BF16)).astype(BF16)
    v_ref[...] = _dot(m, wv_ref[...].astype(BF16)).astype(BF16)


def _memory_kv(mem2d, wk, wv, layer):
    rows, d = mem2d.shape
    out = jax.ShapeDtypeStruct((rows, d), BF16)
    return pl.pallas_call(
        _kv_kernel,
        out_shape=(out, out),
        grid=(1,),
        in_specs=[_const_spec((rows, d)), _pick_spec(wk, layer), _pick_spec(wv, layer)],
        out_specs=(_const_spec((rows, d)), _const_spec((rows, d))),
        compiler_params=_params(),
        name="memory_kv",
    )(mem2d, wk, wv)


def _attn_kernel(x_ref, k_ref, v_ref, wq_hbm, wo_hbm, g_ref, b_ref, wr_ref, br_ref,
                 o_ref, op_ref, rt_ref, rg_ref, cnt_ref, o_scr, carry_scr, wq_ref, wo_ref, stage, sems,
                 *, layer):
    tm, d = x_ref.shape
    hd_dim = d // N_XHEADS
    sub = tm // SUB_TILES

    @pl.when(pl.program_id(0) == 0)
    def _():
        carry_scr[...] = jnp.zeros_like(carry_scr)
        _load_cast(wq_hbm.at[layer], wq_ref, stage, sems)
        _load_cast(wo_hbm.at[layer], wo_ref, stage, sems)

    row_slices = [slice(st * sub, (st + 1) * sub) for st in range(SUB_TILES)]
    for rs in row_slices:
        q = _dot(x_ref[rs, :].astype(BF16), wq_ref[...]) * (1.0 / math.sqrt(hd_dim))
        for hd in range(N_XHEADS):
            hs = slice(hd * hd_dim, (hd + 1) * hd_dim)
            s = lax.dot_general(q[:, hs].astype(BF16), k_ref[:, hs], _NT, preferred_element_type=F32)
            p = jnp.exp(s - jnp.max(s, axis=-1, keepdims=True))
            p = p * (1.0 / jnp.sum(p, axis=-1, keepdims=True))
            o_scr[rs, hs] = _dot(p.astype(BF16), v_ref[:, hs]).astype(BF16)
    xas = [_dot(o_scr[rs, :], wo_ref[...]) for rs in row_slices]
    for rs, xa in zip(row_slices, xas):
        out = _layer_norm(ALPHA * x_ref[rs, :] + xa, g_ref[...], b_ref[...])
        o_ref[rs, :] = out
        op_ref[rs, :] = _pack_halves(out)
        table, gates = _route_rows(out, wr_ref, br_ref, carry_scr)
        rt_ref[:, rs] = table
        rg_ref[rs, :] = gates
    cnt_ref[...] = carry_scr[...].astype(I32)


def _cross_attn(x, seq, layer, k, v, wq, wo, ln_g, ln_b, ln_row, router_w, router_b):
    n, d = x.shape
    m = k.shape[1]
    tiles_per_seq = seq // TM
    kvspec = pl.BlockSpec((None, m, d), lambda i: (i // tiles_per_seq, 0, 0))
    out, packed, table, gates, cnt = pl.pallas_call(
        functools.partial(_attn_kernel, layer=layer),
        out_shape=(jax.ShapeDtypeStruct((n, d), F32), jax.ShapeDtypeStruct((n, d // 2), I32),
                   jax.ShapeDtypeStruct((ROUTE_ROWS, n), F32), jax.ShapeDtypeStruct((n, TOP_K), F32),
                   jax.ShapeDtypeStruct((N_EXPERTS, LANES), I32)),
        grid=(n // TM,),
        in_specs=[
            pl.BlockSpec((TM, d), lambda i: (i, 0)),
            kvspec, kvspec,
            _HBM, _HBM,
            _pick_spec(ln_g, ln_row), _pick_spec(ln_b, ln_row),
            _pick_spec(router_w, layer), _pick_spec(router_b, layer),
        ],
        out_specs=(pl.BlockSpec((TM, d), lambda i: (i, 0)), pl.BlockSpec((TM, d // 2), lambda i: (i, 0)),
                   pl.BlockSpec((ROUTE_ROWS, TM), lambda i: (0, i)), pl.BlockSpec((TM, TOP_K), lambda i: (i, 0)),
                   _const_spec((N_EXPERTS, LANES))),
        scratch_shapes=[pltpu.VMEM((TM, d), BF16), pltpu.VMEM((N_EXPERTS, LANES), F32),
                        pltpu.VMEM((d, d), BF16), pltpu.VMEM((d, d), BF16)] + _stage_scratch(d),
        compiler_params=_params(),
        name="cross_attn",
    )(x, k, v, wq, wo, ln_g, ln_b, router_w, router_b)
    return out, packed, table, gates, cnt[:, 0]


def _route_rows(x, w_ref, bias_ref, carry_scr):
    tm = x.shape[0]
    neg = -jnp.inf
    logits = _dot(x.astype(BF16), w_ref[...]) + bias_ref[...]
    lt = jnp.transpose(logits)

    def first_argmax(vals):
        rows = lax.broadcasted_iota(I32, vals.shape, 0).astype(F32)
        mx = jnp.max(vals, axis=0, keepdims=True)
        idx = jnp.min(jnp.where(vals == mx, rows, float(vals.shape[0])), axis=0, keepdims=True)
        return mx, idx, rows

    gl = lt[0:N_GROUPS, :]
    gmax, g_sel, _ = first_argmax(gl)
    gate_g = 1.0 / jnp.sum(jnp.exp(gl - gmax), axis=0, keepdims=True)

    el = lt[EXPERT_ROW0:EXPERT_ROW0 + EXPERTS_PER_GROUP, :]
    for g in range(1, N_GROUPS):
        lo = EXPERT_ROW0 + g * EXPERTS_PER_GROUP
        el = jnp.where(g_sel == float(g), lt[lo:lo + EXPERTS_PER_GROUP, :], el)
    m1, i1, erow = first_argmax(el)
    m2, i2, _ = first_argmax(jnp.where(erow == i1, neg, el))
    e21 = jnp.exp(m2 - m1)
    w1 = 1.0 / (1.0 + e21)
    w2 = e21 / (1.0 + e21)
    e1 = g_sel * EXPERTS_PER_GROUP + i1
    e2 = g_sel * EXPERTS_PER_GROUP + i2

    xrow = lax.broadcasted_iota(I32, (N_EXPERTS, tm), 0).astype(F32)
    oh1 = xrow == e1
    oh2 = xrow == e2
    oh = (oh1 | oh2).astype(BF16)
    r = lax.broadcasted_iota(I32, (tm, tm), 0)
    c = lax.broadcasted_iota(I32, (tm, tm), 1)
    before = _dot(oh, (r < c).astype(BF16)) + carry_scr[:, 0:1]
    rank1 = jnp.sum(jnp.where(oh1, before, 0.0), axis=0, keepdims=True)
    rank2 = jnp.sum(jnp.where(oh2, before, 0.0), axis=0, keepdims=True)
    carry_scr[...] += jnp.sum(oh.astype(F32), axis=1, keepdims=True)

    trow = lax.broadcasted_iota(I32, (ROUTE_ROWS, tm), 0)
    table = jnp.where(trow == 0, e1, jnp.where(trow == 1, e2, jnp.where(trow == 2, rank1,
                                                                      jnp.where(trow == 3, rank2, 0.0))))
    grow = lax.broadcasted_iota(I32, (LANES, tm), 0)
    gates = jnp.transpose(jnp.where(grow == 0, gate_g * w1, jnp.where(grow == 1, gate_g * w2, 0.0)))[:, :TOP_K]
    return table, gates


def _router_weights(wr_g, br_g, wr_e, br_e):
    nl, d, _ = wr_g.shape
    gap = EXPERT_ROW0 - N_GROUPS
    w = jnp.concatenate([wr_g, jnp.zeros((nl, d, gap), F32),
                         jnp.transpose(wr_e, (0, 2, 1, 3)).reshape(nl, d, N_EXPERTS)], axis=2)
    w = jnp.pad(w, ((0, 0), (0, 0), (0, ROUTE_COLS - w.shape[2]))).astype(BF16)
    bias = jnp.concatenate([br_g, jnp.zeros((nl, gap), F32), br_e.reshape(nl, -1)], axis=1)
    bias = jnp.pad(bias, ((0, 0), (0, ROUTE_COLS - bias.shape[1])))
    return w, bias[:, None, :]


def _sc_worker_rows(rows):
    per_worker = rows // SC_WORKERS
    n_chunks = per_worker // SC_CHUNK
    assert per_worker * SC_WORKERS == rows and n_chunks * SC_CHUNK == per_worker and n_chunks % 2 == 0
    return per_worker, n_chunks


def _sc_gather_rows(table_hbm, out_hbm, idx_v, rows_v, gsem, wsem, base, n_chunks):
    def fetch(c, slot):
        off = pl.multiple_of(c * SC_CHUNK, SC_CHUNK)
        return pltpu.make_async_copy(table_hbm.at[idx_v.at[pl.ds(off, SC_CHUNK)]], rows_v.at[slot],
                                     gsem.at[slot])

    def put(c, slot):
        off = pl.multiple_of(c * SC_CHUNK, SC_CHUNK)
        return pltpu.make_async_copy(rows_v.at[slot], out_hbm.at[pl.ds(base + off, SC_CHUNK)], wsem.at[slot])

    fetch(0, 0).start()

    @pl.loop(0, n_chunks, step=2)
    def _(c0):
        for slot in range(2):
            c = c0 + slot

            @pl.when(c + 1 < n_chunks)
            def _():
                @pl.when(c >= 1)
                def _():
                    put(c - 1, 1 - slot).wait()
                fetch(c + 1, 1 - slot).start()

            fetch(c, slot).wait()
            put(c, slot).start()

    put(n_chunks - 2, 0).wait()
    put(n_chunks - 1, 1).wait()


def _sc_row_scratch(per_worker, d, dtype):
    return [pltpu.VMEM((per_worker,), I32), pltpu.VMEM((2, SC_CHUNK, d), dtype),
            pltpu.SemaphoreType.DMA((2,)), pltpu.SemaphoreType.DMA((2,))]


def _sc_gather(table, idx):
    b = idx.shape[0]
    d = table.shape[1]
    per_worker, n_chunks = _sc_worker_rows(b)
    mesh = plsc.VectorSubcoreMesh(core_axis_name="c", subcore_axis_name="s")

    @functools.partial(
        pl.kernel, mesh=mesh,
        out_type=jax.ShapeDtypeStruct((b, d), table.dtype),
        scratch_types=_sc_row_scratch(per_worker, d, table.dtype),
        name="sc_gather",
    )
    def gather(table_hbm, idx_hbm, out_hbm, idx_v, rows_v, gsem, wsem):
        base = (lax.axis_index("s") * SC_CORES + lax.axis_index("c")) * per_worker
        pltpu.sync_copy(idx_hbm.at[pl.ds(base, per_worker)], idx_v)
        _sc_gather_rows(table_hbm, out_hbm, idx_v, rows_v, gsem, wsem, base, n_chunks)

    return gather(table, idx)


def _sc_dispatch(table, dest_flat, rows):
    n, d = table.shape
    top_k = dest_flat.shape[0] // n
    per_worker, n_chunks = _sc_worker_rows(n)
    mesh = plsc.VectorSubcoreMesh(core_axis_name="c", subcore_axis_name="s")

    @functools.partial(
        pl.kernel, mesh=mesh,
        out_type=jax.ShapeDtypeStruct((rows, d), table.dtype),
        scratch_types=[pltpu.VMEM((per_worker,), I32)] * top_k + [
            pltpu.VMEM((2, SC_CHUNK, d), table.dtype),
            pltpu.SemaphoreType.DMA((2,)), pltpu.SemaphoreType.DMA((2, top_k))],
        name="sc_dispatch",
    )
    def dispatch(table_hbm, dest_hbm, out_hbm, *scratch):
        dest_v, (rows_v, gsem, wsem) = scratch[:top_k], scratch[top_k:]
        base = (lax.axis_index("s") * SC_CORES + lax.axis_index("c")) * per_worker

        def fetch(c, slot):
            off = pl.multiple_of(c * SC_CHUNK, SC_CHUNK)
            return pltpu.make_async_copy(table_hbm.at[pl.ds(base + off, SC_CHUNK)], rows_v.at[slot],
                                         gsem.at[slot])

        def puts(c, slot):
            off = pl.multiple_of(c * SC_CHUNK, SC_CHUNK)
            return [pltpu.make_async_copy(rows_v.at[slot], out_hbm.at[dest_v[k].at[pl.ds(off, SC_CHUNK)]],
                                          wsem.at[slot, k]) for k in range(top_k)]

        fetch(0, 0).start()
        for k in range(top_k):
            pltpu.sync_copy(dest_hbm.at[pl.ds(k * n + base, per_worker)], dest_v[k])

        @pl.loop(0, n_chunks, step=2)
        def _(c0):
            for slot in range(2):
                c = c0 + slot

                @pl.when(c + 1 < n_chunks)
                def _():
                    @pl.when(c >= 1)
                    def _():
                        for p in puts(c - 1, 1 - slot):
                            p.wait()
                    fetch(c + 1, 1 - slot).start()

                fetch(c, slot).wait()
                for p in puts(c, slot):
                    p.start()

        for p in puts(n_chunks - 2, 0) + puts(n_chunks - 1, 1):
            p.wait()

    return dispatch(table, dest_flat)


def _expert_kernel(sched_ref, ni_ref, xs_hbm, w1_hbm, w3_hbm, w2_hbm, y_hbm,
                   x_buf, y_buf, w1_buf, w3_buf, w2_buf, w1_scr, w3_scr, w2_scr, wsems, xsems, ysems,
                   *, layer):
    i = pl.program_id(0)
    n_items = ni_ref[0]
    expert, wslot, run_start, next_expert = (sched_ref[r, i] for r in range(4))
    slot = i % 2

    def for_units(item, fn):
        for units in range(1, ITEM_UNITS + 1):
            @pl.when(sched_ref[5, item] == units)
            def _():
                fn(units * EXPERT_UNIT)

    def x_copy(item, s, rows):
        row0 = pl.multiple_of(sched_ref[4, item], EXPERT_UNIT)
        return pltpu.make_async_copy(xs_hbm.at[pl.ds(row0, rows)], x_buf.at[s, pl.ds(0, rows)], xsems.at[s])

    def y_copy(item, s, rows):
        row0 = pl.multiple_of(sched_ref[4, item], EXPERT_UNIT)
        return pltpu.make_async_copy(y_buf.at[s, pl.ds(0, rows)], y_hbm.at[pl.ds(row0, rows)], ysems.at[s])

    def fetch(e, s):
        return [pltpu.make_async_copy(w_hbm.at[layer, e], buf.at[s], wsems.at[s, j])
                for j, (w_hbm, buf) in enumerate(((w1_hbm, w1_buf), (w3_hbm, w3_buf), (w2_hbm, w2_buf)))]

    def mlp(rows):
        x_lo, x_hi = _unpack_halves(x_buf[slot, 0:rows, :])
        xb = jnp.concatenate([x_lo.astype(BF16), x_hi.astype(BF16)], axis=1)
        h1 = _dot(xb, w1_scr[...])
        h3 = _dot(xb, w3_scr[...])
        hid = h1 * (1.0 / (1.0 + jnp.exp(-h1))) * h3
        y_buf[slot, 0:rows, :] = _pack_halves(_dot(hid.astype(BF16), w2_scr[...]))
        y_copy(i, slot, rows).start()

    @pl.when(i < n_items)
    def _():
        @pl.when(i == 0)
        def _():
            for_units(0, lambda rows: x_copy(0, 0, rows).start())

        for_units(i, lambda rows: x_copy(i, slot, rows).wait())

        @pl.when(i + 1 < n_items)
        def _():
            for_units(i + 1, lambda rows: x_copy(i + 1, 1 - slot, rows).start())

        @pl.when(run_start == 1)
        def _():
            @pl.when(i == 0)
            def _():
                for c in fetch(expert, wslot):
                    c.start()

            for c in fetch(expert, wslot):
                c.wait()

            @pl.when(next_expert >= 0)
            def _():
                for c in fetch(next_expert, 1 - wslot):
                    c.start(priority=1)

            w1_scr[...] = w1_buf[wslot].astype(BF16)
            w3_scr[...] = w3_buf[wslot].astype(BF16)
            w2_scr[...] = w2_buf[wslot].astype(BF16)

        @pl.when(i >= 2)
        def _():
            for_units(i - 2, lambda rows: y_copy(i - 2, slot, rows).wait())

        for_units(i, mlp)

        @pl.when(i == n_items - 1)
        def _():
            @pl.when(i >= 1)
            def _():
                for_units(i - 1, lambda rows: y_copy(i - 1, 1 - slot, rows).wait())

            for_units(i, lambda rows: y_copy(i, slot, rows).wait())
            y_buf[0, 0:EXPERT_UNIT, :] = jnp.zeros((EXPERT_UNIT, y_buf.shape[2]), y_buf.dtype)
            first_free = (sched_ref[4, i] + sched_ref[5, i] * EXPERT_UNIT) // EXPERT_UNIT

            def zero_copy(u):
                return pltpu.make_async_copy(
                    y_buf.at[0, pl.ds(0, EXPERT_UNIT)],
                    y_hbm.at[pl.ds(pl.multiple_of(u * EXPERT_UNIT, EXPERT_UNIT), EXPERT_UNIT)], ysems.at[0])

            def start_zero(u, _):
                zero_copy(u).start()
                return 0

            def wait_zero(u, _):
                zero_copy(u).wait()
                return 0

            lax.fori_loop(first_free, y_hbm.shape[0] // EXPERT_UNIT, start_zero, 0)
            lax.fori_loop(first_free, y_hbm.shape[0] // EXPERT_UNIT, wait_zero, 0)


def _expert_mlp(xs, schedule, n_items, layer, w1, w3, w2):
    d, de = w1.shape[2], w1.shape[3]
    max_rows = ITEM_UNITS * EXPERT_UNIT
    return pl.pallas_call(
        functools.partial(_expert_kernel, layer=layer),
        out_shape=jax.ShapeDtypeStruct(xs.shape, I32),
        grid_spec=pltpu.PrefetchScalarGridSpec(
            num_scalar_prefetch=2,
            grid=(schedule.shape[1],),
            in_specs=[_HBM, _HBM, _HBM, _HBM],
            out_specs=_HBM,
            scratch_shapes=[pltpu.VMEM((2, max_rows, d // 2), I32), pltpu.VMEM((2, max_rows, d // 2), I32),
                            pltpu.VMEM((2, d, de), F32), pltpu.VMEM((2, d, de), F32), pltpu.VMEM((2, de, d), F32),
                            pltpu.VMEM((d, de), BF16), pltpu.VMEM((d, de), BF16), pltpu.VMEM((de, d), BF16),
                            pltpu.SemaphoreType.DMA((2, 3)), pltpu.SemaphoreType.DMA((2,)),
                            pltpu.SemaphoreType.DMA((2,))],
        ),
        compiler_params=_params(),
        name="expert_mlp",
    )(schedule, n_items, xs, w1, w3, w2)


def _combine_kernel(x_ref, y0_ref, y1_ref, gate_ref, g_ref, b_ref, o_ref):
    o_ref[...] = _moe_output(x_ref, y0_ref, y1_ref, gate_ref, g_ref, b_ref)


def _combine(x, yg, gates, ln_g, ln_b, ln_row):
    n, d = x.shape
    tiles = n // TM
    return pl.pallas_call(
        _combine_kernel,
        out_shape=jax.ShapeDtypeStruct((n, d), F32),
        grid=(tiles,),
        in_specs=[pl.BlockSpec((TM, d), lambda i: (i, 0)),
                  pl.BlockSpec((TM, d // 2), lambda i: (i, 0)),
                  pl.BlockSpec((TM, d // 2), lambda i: (i + tiles, 0)),
                  pl.BlockSpec((TM, TOP_K), lambda i: (i, 0)),
                  _pick_spec(ln_g, ln_row), _pick_spec(ln_b, ln_row)],
        out_specs=pl.BlockSpec((TM, d), lambda i: (i, 0)),
        compiler_params=_params(),
        name="combine",
    )(x, yg, yg, gates, ln_g, ln_b)


def _moe_experts(x_packed, table, counts, layer, w1, w3, w2):
    n = x_packed.shape[0]
    max_units = (n * TOP_K + N_EXPERTS * (EXPERT_UNIT - 1) + EXPERT_UNIT - 1) // EXPERT_UNIT
    max_items = (max_units + N_EXPERTS * (ITEM_UNITS - 1) + ITEM_UNITS - 1) // ITEM_UNITS
    experts = table[:TOP_K].astype(I32)
    ranks = table[TOP_K:2 * TOP_K].astype(I32)

    units_e = (counts + EXPERT_UNIT - 1) // EXPERT_UNIT
    units_start = jnp.cumsum(units_e) - units_e
    items_e = (units_e + ITEM_UNITS - 1) // ITEM_UNITS
    items_end = jnp.cumsum(items_e)
    n_items = items_end[-1:].astype(I32)
    item_ids = jnp.arange(max_items, dtype=I32)
    expert_ids = jnp.arange(N_EXPERTS, dtype=I32)
    item_expert = jnp.minimum(jnp.sum(items_end[None, :] <= item_ids[:, None], axis=1), N_EXPERTS - 1)
    later = (expert_ids[None, :] > expert_ids[:, None]) & (items_e[None, :] > 0)
    next_run = jnp.min(jnp.where(later, expert_ids[None, :], N_EXPERTS), axis=1)
    next_run = jnp.where(next_run == N_EXPERTS, -1, next_run)
    per_expert = jnp.stack([items_end - items_e, units_start, units_e, next_run], axis=1)
    mine = (item_expert[:, None] == expert_ids[None, :])[:, :, None]
    first_item, unit0, units, next_expert = jnp.sum(jnp.where(mine, per_expert[None], 0), axis=1).T
    within = item_ids - first_item
    item_row0 = (unit0 + ITEM_UNITS * within) * EXPERT_UNIT
    item_units = jnp.clip(units - ITEM_UNITS * within, 1, ITEM_UNITS)
    run_start_flag = ((within == 0) & (item_ids < n_items[0])).astype(I32)
    slot = (jnp.cumsum(run_start_flag) - 1) % 2
    schedule = jnp.stack([item_expert, slot, run_start_flag, next_expert, item_row0, item_units]).astype(I32)
    start_of = jnp.sum(jnp.where(experts[:, :, None] == expert_ids, units_start * EXPERT_UNIT, 0), axis=-1)
    dest = (start_of + ranks).astype(I32).reshape(-1)
    xs = _sc_dispatch(x_packed, dest, max_units * EXPERT_UNIT)
    y = _expert_mlp(xs, schedule, n_items, layer, w1, w3, w2)
    return _sc_gather(y, dest)


def kernel(x, mem, w_in_even, w_pool, pool_scale, ln_v_g, ln_v_b, w_spatial, b_spatial, w_out_even,
           w_in_odd, conv_w, conv_b, w_out_odd, wq_x, wk_x, wv_x, wo_x, ln_g, ln_b, wr_group,
           br_group, wr_expert, br_expert, w1, w3, w2):
    bsz, seq, d = x.shape
    assert seq % TM == 0 and d % LANES == 0
    mlen = mem.shape[1]
    mem2d = mem.reshape(bsz * mlen, d)
    ln_g = ln_g.reshape(DEPTH * 3, 1, d)
    ln_b = ln_b.reshape(DEPTH * 3, 1, d)
    w_pool = w_pool.astype(BF16)
    pool_scale, ln_v_g, ln_v_b, conv_b = (p[:, None, :] for p in (pool_scale, ln_v_g, ln_v_b, conv_b))
    b_spatial_t = jnp.swapaxes(b_spatial, 1, 2)
    conv_w_t = jnp.swapaxes(conv_w, 1, 2)
    router_w, router_b = _router_weights(wr_group, br_group, wr_expert, br_expert)

    h = x.reshape(bsz * seq, d)
    pending = None
    kv = _memory_kv(mem2d, wk_x, wv_x, 0)
    for l in range(DEPTH):
        i = l // 2
        k, v = (a.reshape(bsz, mlen, d) for a in kv)
        if l % 2 == 0:
            h = _even_mixer(h, pending, seq, i, w_in_even, w_pool, pool_scale, ln_v_g, ln_v_b,
                            w_spatial, b_spatial_t, w_out_even, ln_g, ln_b, 3 * l)
        else:
            h = _odd_mixer(h, pending, seq, i, w_in_odd, conv_w_t, conv_b, w_out_odd, ln_g, ln_b, 3 * l)
        h, hp, table, gates, counts = _cross_attn(h, seq, l, k, v, wq_x, wo_x,
                                                  ln_g, ln_b, 3 * l + 1, router_w, router_b)
        if l + 1 < DEPTH:
            kv = _memory_kv(mem2d, wk_x, wv_x, l + 1)
        yg = _moe_experts(hp, table, counts, l, w1, w3, w2)
        pending = (yg, gates, ln_g, ln_b, 3 * l + 2)
    return _combine(h, *pending).reshape(bsz, seq, d)
```
